```python
import math
import jax, jax.numpy as jnp
from jax import lax
import numpy as np

D_MODEL = 2048
BATCH = 8
SEQ = 8192
DEPTH = 1

D_CONV = 1024
CONV_GROUPS = 16
CONV_WIDTH = 3
N_HEADS = 8
QK_NOPE = 128
QK_ROPE = 64
QK_HEAD = QK_NOPE + QK_ROPE
V_HEAD = 128
D_ATTN = N_HEADS * V_HEAD
Q_LORA = 512
KV_LORA = 256
ROPE_BASE = 10000.0
Q_BLOCK = 128
D_MIX = D_CONV + D_ATTN
IN_COLS = 4 * D_CONV + Q_LORA + KV_LORA + QK_ROPE + D_ATTN
EPS = 1e-6

kernel_name = "hymba_conv_mla_adaln_layer"


def _rmsnorm(x, g):
    x32 = x.astype(jnp.float32)
    y = x32 * lax.rsqrt(jnp.mean(x32 * x32, axis=-1, keepdims=True) + EPS)
    return (y * g.astype(jnp.float32)).astype(x.dtype)


def _rope_tables(positions):
    inv_freq = ROPE_BASE ** (-jnp.arange(0, QK_ROPE, 2, dtype=jnp.float32) / QK_ROPE)
    ang = positions.astype(jnp.float32)[..., None] * inv_freq
    return jnp.cos(ang), jnp.sin(ang)


def _apply_rope(x, cos, sin):
    half = QK_ROPE // 2
    x32 = x.astype(jnp.float32)
    x1, x2 = x32[..., :half], x32[..., half:]
    out = jnp.concatenate([x1 * cos - x2 * sin, x1 * sin + x2 * cos], axis=-1)
    return out.astype(x.dtype)


def _short_conv_branch(x_c, b_c, c_c, z_c, conv_w):
    u = c_c * x_c
    seq = u.shape[1]
    u_pad = jnp.pad(u, ((0, 0), (CONV_WIDTH - 1, 0), (0, 0)))
    conv = sum(conv_w[k] * u_pad[:, k:k + seq, :] for k in range(CONV_WIDTH))
    y = b_c * conv
    return y * jax.nn.silu(z_c)


def _causal_blocked_attention(q, k, v):
    bsz, seq = q.shape[0], q.shape[1]
    n_blk = seq // Q_BLOCK
    scale = 1.0 / math.sqrt(QK_HEAD)
    q_blocks = q.reshape(bsz, n_blk, Q_BLOCK, N_HEADS, QK_HEAD).transpose(1, 0, 2, 3, 4)
    key_idx = jnp.arange(seq, dtype=jnp.int32)

    def one_block(args):
        qb, blk = args
        q_idx = blk * Q_BLOCK + jnp.arange(Q_BLOCK, dtype=jnp.int32)
        s = jnp.einsum('bqhd,bkhd->bhqk', qb, k).astype(jnp.float32) * scale
        mask = key_idx[None, :] <= q_idx[:, None]
        s = jnp.where(mask[None, None], s, -jnp.inf)
        p = jax.nn.softmax(s, axis=-1).astype(v.dtype)
        return jnp.einsum('bhqk,bkhd->bqhd', p, v)

    out = lax.map(one_block, (q_blocks, jnp.arange(n_blk, dtype=jnp.int32)))
    return out.transpose(1, 0, 2, 3, 4).reshape(bsz, seq, N_HEADS, V_HEAD)


def _mla_branch(c_q, c_kv, k_rope, z_a, cos, sin, q_a_g, w_q_b, kv_a_g, w_kv_b, q_g, k_g):
    bsz, seq = c_q.shape[0], c_q.shape[1]
    q = (_rmsnorm(c_q, q_a_g) @ w_q_b).reshape(bsz, seq, N_HEADS, QK_HEAD)
    kv = (_rmsnorm(c_kv, kv_a_g) @ w_kv_b).reshape(bsz, seq, N_HEADS, QK_NOPE + V_HEAD)
    k_nope, v = kv[..., :QK_NOPE], kv[..., QK_NOPE:]
    k = jnp.concatenate([k_nope, jnp.broadcast_to(k_rope[:, :, None, :], (bsz, seq, N_HEADS, QK_ROPE))], axis=-1)
    q = _rmsnorm(q, q_g)
    k = _rmsnorm(k, k_g)
    cos_h, sin_h = cos[:, :, None, :], sin[:, :, None, :]
    q = jnp.concatenate([q[..., :QK_NOPE], _apply_rope(q[..., QK_NOPE:], cos_h, sin_h)], axis=-1)
    k = jnp.concatenate([k[..., :QK_NOPE], _apply_rope(k[..., QK_NOPE:], cos_h, sin_h)], axis=-1)
    o = _causal_blocked_attention(q, k, v).reshape(bsz, seq, D_ATTN)
    return o * jax.nn.silu(z_a)


def _layer(x, c, cos, sin, ada_w, ada_b, norm_g, w_in, conv_w, q_a_g, w_q_b, kv_a_g, w_kv_b, q_g, k_g, w_out):
    mod = jax.nn.silu(c) @ ada_w + ada_b
    shift, scale, gate = jnp.split(mod, 3, axis=-1)
    h = _rmsnorm(x, norm_g) * (1.0 + scale[:, None, :]) + shift[:, None, :]
    u = h @ w_in
    splits = np.cumsum([D_CONV, D_CONV, D_CONV, D_CONV, Q_LORA, KV_LORA, QK_ROPE])
    x_c, b_c, c_c, z_c, c_q, c_kv, k_rope, z_a = jnp.split(u, splits.tolist(), axis=-1)
    y_conv = _short_conv_branch(x_c, b_c, c_c, z_c, conv_w)
    y_attn = _mla_branch(c_q, c_kv, k_rope, z_a, cos, sin, q_a_g, w_q_b, kv_a_g, w_kv_b, q_g, k_g)
    y = jnp.concatenate([y_conv, y_attn], axis=-1) @ w_out
    return x + gate[:, None, :] * y


def _fwd_setup_inputs(seed: int = 0) -> dict:
    key = jax.random.key(seed)
    ks = jax.random.split(key, 20)
    f32 = jnp.float32

    def nrm(k, shape, fan_in, mult=1.0):
        return jax.random.normal(k, shape, f32) * (mult * fan_in ** -0.5)

    def gain(k, shape):
        return 1.0 + 0.02 * jax.random.normal(k, shape, f32)

    x = jax.random.normal(ks[0], (BATCH, SEQ, D_MODEL), f32)
    c = jax.random.normal(ks[1], (BATCH, D_MODEL), f32)
    positions = jnp.broadcast_to(jnp.arange(SEQ, dtype=jnp.int32), (BATCH, SEQ))
    return {
        "x": x,
        "c": c,
        "positions": positions,
        "ada_w": nrm(ks[2], (DEPTH, D_MODEL, 3 * D_MODEL), D_MODEL, 0.5),
        "ada_b": 0.01 * jax.random.normal(ks[3], (DEPTH, 3 * D_MODEL), f32),
        "norm_g": gain(ks[4], (DEPTH, D_MODEL)),
        "w_in": nrm(ks[5], (DEPTH, D_MODEL, IN_COLS), D_MODEL),
        "conv_w": nrm(ks[6], (DEPTH, CONV_WIDTH, D_CONV), CONV_WIDTH),
        "q_a_g": gain(ks[7], (DEPTH, Q_LORA)),
        "w_q_b": nrm(ks[8], (DEPTH, Q_LORA, N_HEADS * QK_HEAD), Q_LORA),
        "kv_a_g": gain(ks[9], (DEPTH, KV_LORA)),
        "w_kv_b": nrm(ks[10], (DEPTH, KV_LORA, N_HEADS * (QK_NOPE + V_HEAD)), KV_LORA),
        "q_g": gain(ks[11], (DEPTH, QK_HEAD)),
        "k_g": gain(ks[12], (DEPTH, QK_HEAD)),
        "w_out": nrm(ks[13], (DEPTH, D_MIX, D_MODEL), D_MIX),
    }


def _fwd_reference(x, c, positions, ada_w, ada_b, norm_g, w_in, conv_w, q_a_g, w_q_b, kv_a_g, w_kv_b, q_g, k_g, w_out):
    cos, sin = _rope_tables(positions)
    for l in range(DEPTH):
        x = _layer(x, c, cos, sin, ada_w[l], ada_b[l], norm_g[l], w_in[l], conv_w[l],
                   q_a_g[l], w_q_b[l], kv_a_g[l], w_kv_b[l], q_g[l], k_g[l], w_out[l])
    return x


import jax as _jax
import jax.numpy as _jnp

TWIN_FORMAT = 'train_step'
FWD_PARAMS = ['x', 'c', 'positions', 'ada_w', 'ada_b', 'norm_g', 'w_in', 'conv_w', 'q_a_g', 'w_q_b', 'kv_a_g', 'w_kv_b', 'q_g', 'k_g', 'w_out']
TWIN_WEIGHTS = ['ada_w', 'ada_b', 'norm_g', 'w_in', 'conv_w', 'q_a_g', 'w_q_b', 'kv_a_g', 'w_kv_b', 'q_g', 'k_g', 'w_out']
TWIN_DIFF_INPUT = 'x'
TWIN_INPUTS = ['x', 'c', 'positions', 'ada_w', 'ada_b', 'norm_g', 'w_in', 'conv_w', 'q_a_g', 'w_q_b', 'kv_a_g', 'w_kv_b', 'q_g', 'k_g', 'w_out', 'loss_target', 'm_ada_w', 'm_ada_b', 'm_norm_g', 'm_w_in', 'm_conv_w', 'm_q_a_g', 'm_w_q_b', 'm_kv_a_g', 'm_w_kv_b', 'm_q_g', 'm_k_g', 'm_w_out', 'v_ada_w', 'v_ada_b', 'v_norm_g', 'v_w_in', 'v_conv_w', 'v_q_a_g', 'v_w_q_b', 'v_kv_a_g', 'v_w_kv_b', 'v_q_g', 'v_k_g', 'v_w_out']
TWIN_OUTPUTS = ['loss', 'grad_x', 'grad_ada_w', 'grad_ada_b', 'grad_norm_g', 'grad_w_in', 'grad_conv_w', 'grad_q_a_g', 'grad_w_q_b', 'grad_kv_a_g', 'grad_w_kv_b', 'grad_q_g', 'grad_k_g', 'grad_w_out', 'delta_ada_w', 'delta_ada_b', 'delta_norm_g', 'delta_w_in', 'delta_conv_w', 'delta_q_a_g', 'delta_w_q_b', 'delta_kv_a_g', 'delta_w_kv_b', 'delta_q_g', 'delta_k_g', 'delta_w_out', 'new_m_ada_w', 'new_m_ada_b', 'new_m_norm_g', 'new_m_w_in', 'new_m_conv_w', 'new_m_q_a_g', 'new_m_w_q_b', 'new_m_kv_a_g', 'new_m_w_kv_b', 'new_m_q_g', 'new_m_k_g', 'new_m_w_out', 'new_v_ada_w', 'new_v_ada_b', 'new_v_norm_g', 'new_v_w_in', 'new_v_conv_w', 'new_v_q_a_g', 'new_v_w_q_b', 'new_v_kv_a_g', 'new_v_w_kv_b', 'new_v_q_g', 'new_v_k_g', 'new_v_w_out']
TWIN_LEAF_KINDS = {'loss': 'loss', 'grad_x': 'grad_x', 'grad_ada_w': 'grad_w', 'grad_ada_b': 'grad_w', 'grad_norm_g': 'grad_w', 'grad_w_in': 'grad_w', 'grad_conv_w': 'grad_w', 'grad_q_a_g': 'grad_w', 'grad_w_q_b': 'grad_w', 'grad_kv_a_g': 'grad_w', 'grad_w_kv_b': 'grad_w', 'grad_q_g': 'grad_w', 'grad_k_g': 'grad_w', 'grad_w_out': 'grad_w', 'delta_ada_w': 'delta_w', 'delta_ada_b': 'delta_w', 'delta_norm_g': 'delta_w', 'delta_w_in': 'delta_w', 'delta_conv_w': 'delta_w', 'delta_q_a_g': 'delta_w', 'delta_w_q_b': 'delta_w', 'delta_kv_a_g': 'delta_w', 'delta_w_kv_b': 'delta_w', 'delta_q_g': 'delta_w', 'delta_k_g': 'delta_w', 'delta_w_out': 'delta_w', 'new_m_ada_w': 'new_m', 'new_m_ada_b': 'new_m', 'new_m_norm_g': 'new_m', 'new_m_w_in': 'new_m', 'new_m_conv_w': 'new_m', 'new_m_q_a_g': 'new_m', 'new_m_w_q_b': 'new_m', 'new_m_kv_a_g': 'new_m', 'new_m_w_kv_b': 'new_m', 'new_m_q_g': 'new_m', 'new_m_k_g': 'new_m', 'new_m_w_out': 'new_m', 'new_v_ada_w': 'new_v', 'new_v_ada_b': 'new_v', 'new_v_norm_g': 'new_v', 'new_v_w_in': 'new_v', 'new_v_conv_w': 'new_v', 'new_v_q_a_g': 'new_v', 'new_v_w_q_b': 'new_v', 'new_v_kv_a_g': 'new_v', 'new_v_w_kv_b': 'new_v', 'new_v_q_g': 'new_v', 'new_v_k_g': 'new_v', 'new_v_w_out': 'new_v'}


def _forward(args):
    return _fwd_reference(*[args[k] for k in FWD_PARAMS])


def _output_shape():
    def fwd():
        inp = _fwd_setup_inputs(0)
        return _fwd_reference(*[inp[k] for k in FWD_PARAMS])
    out = _jax.eval_shape(fwd)
    return out.shape, out.dtype

N_MICROBATCH = 1
ADAM_LR = 0.001
ADAM_B1 = 0.9
ADAM_B2 = 0.999
ADAM_EPS = 1e-08
ADAM_WD = 0.01
ADAM_STEP = 10
PER_EXAMPLE_BATCH_AXIS = {'x': 0, 'c': 0, 'positions': 0, 'loss_target': 0}
SHARED_INPUTS = []
_WEIGHT_DTYPES = {'ada_w': _jnp.float32, 'ada_b': _jnp.float32, 'norm_g': _jnp.float32, 'w_in': _jnp.float32, 'conv_w': _jnp.float32, 'q_a_g': _jnp.float32, 'w_q_b': _jnp.float32, 'kv_a_g': _jnp.float32, 'w_kv_b': _jnp.float32, 'q_g': _jnp.float32, 'k_g': _jnp.float32, 'w_out': _jnp.float32}
MOMENT_SCALE = {'ada_w': 7.994507e-01, 'ada_b': 2.548657e+00, 'norm_g': 4.243050e+00, 'w_in': 1.192328e-01, 'conv_w': 1.209553e+00, 'q_a_g': 8.073990e-03, 'w_q_b': 5.024050e-03, 'kv_a_g': 3.372728e-01, 'w_kv_b': 1.662654e-02, 'q_g': 3.005692e-02, 'k_g': 3.000680e-02, 'w_out': 4.434421e-02}


def _to_microbatches(a, axis):
    t = _jnp.moveaxis(a, axis, 0)
    t = t.reshape((N_MICROBATCH, t.shape[0] // N_MICROBATCH) + t.shape[1:])
    return _jnp.moveaxis(t, 1, axis + 1)


def setup_inputs(seed: int = 0) -> dict:
    inp = _fwd_setup_inputs(seed)
    key = _jax.random.fold_in(_jax.random.key(seed), 7919)
    shape, _ = _output_shape()
    out = dict(inp)
    out["loss_target"] = _jax.random.normal(_jax.random.fold_in(key, 0), shape, _jnp.float32)
    for i, name in enumerate(TWIN_WEIGHTS):
        w = inp[name].astype(_jnp.float32)
        if MOMENT_SCALE is None:
            s = _jnp.sqrt(_jnp.mean(_jnp.square(w)) + 1e-30)
        else:
            s = MOMENT_SCALE[name]
        km, kv = _jax.random.split(_jax.random.fold_in(key, i + 1))
        out[name] = w
        out["m_" + name] = s * _jax.random.normal(km, w.shape, _jnp.float32)
        out["v_" + name] = (s * s) * _jax.random.uniform(kv, w.shape, _jnp.float32, 0.5, 1.5)
    if N_MICROBATCH > 1:
        for name, axis in PER_EXAMPLE_BATCH_AXIS.items():
            out[name] = _to_microbatches(out[name], axis)
    return {'x': out['x'], 'c': out['c'], 'positions': out['positions'], 'ada_w': out['ada_w'], 'ada_b': out['ada_b'], 'norm_g': out['norm_g'], 'w_in': out['w_in'], 'conv_w': out['conv_w'], 'q_a_g': out['q_a_g'], 'w_q_b': out['w_q_b'], 'kv_a_g': out['kv_a_g'], 'w_kv_b': out['w_kv_b'], 'q_g': out['q_g'], 'k_g': out['k_g'], 'w_out': out['w_out'], 'loss_target': out['loss_target'], 'm_ada_w': out['m_ada_w'], 'm_ada_b': out['m_ada_b'], 'm_norm_g': out['m_norm_g'], 'm_w_in': out['m_w_in'], 'm_conv_w': out['m_conv_w'], 'm_q_a_g': out['m_q_a_g'], 'm_w_q_b': out['m_w_q_b'], 'm_kv_a_g': out['m_kv_a_g'], 'm_w_kv_b': out['m_w_kv_b'], 'm_q_g': out['m_q_g'], 'm_k_g': out['m_k_g'], 'm_w_out': out['m_w_out'], 'v_ada_w': out['v_ada_w'], 'v_ada_b': out['v_ada_b'], 'v_norm_g': out['v_norm_g'], 'v_w_in': out['v_w_in'], 'v_conv_w': out['v_conv_w'], 'v_q_a_g': out['v_q_a_g'], 'v_w_q_b': out['v_w_q_b'], 'v_kv_a_g': out['v_kv_a_g'], 'v_w_kv_b': out['v_w_kv_b'], 'v_q_g': out['v_q_g'], 'v_k_g': out['v_k_g'], 'v_w_out': out['v_w_out']}


def _loss(weights, diff, rest, loss_target):
    with _jax.named_scope("forward"):
        args = {**rest, TWIN_DIFF_INPUT: diff, **{k: w.astype(_WEIGHT_DTYPES[k]) for k, w in weights.items()}}
        y = _forward(args)
    with _jax.named_scope("loss_head"):
        err = _jnp.square(y.astype(_jnp.float32) - loss_target)
        return 0.5 * _jnp.sum(_jnp.mean(err, axis=-1)) if err.ndim else 0.5 * err


def _adamw(w, g, m, v):
    m = ADAM_B1 * m + (1.0 - ADAM_B1) * g
    v = ADAM_B2 * v + (1.0 - ADAM_B2) * _jnp.square(g)
    m_hat = m / (1.0 - ADAM_B1 ** ADAM_STEP)
    v_hat = v / (1.0 - ADAM_B2 ** ADAM_STEP)
    delta = -ADAM_LR * (m_hat / (_jnp.sqrt(v_hat) + ADAM_EPS) + ADAM_WD * w)
    return delta, m, v


def reference(x, c, positions, ada_w, ada_b, norm_g, w_in, conv_w, q_a_g, w_q_b, kv_a_g, w_kv_b, q_g, k_g, w_out, loss_target, m_ada_w, m_ada_b, m_norm_g, m_w_in, m_conv_w, m_q_a_g, m_w_q_b, m_kv_a_g, m_w_kv_b, m_q_g, m_k_g, m_w_out, v_ada_w, v_ada_b, v_norm_g, v_w_in, v_conv_w, v_q_a_g, v_w_q_b, v_kv_a_g, v_w_kv_b, v_q_g, v_k_g, v_w_out):
    given = dict(x=x, c=c, positions=positions, ada_w=ada_w, ada_b=ada_b, norm_g=norm_g, w_in=w_in, conv_w=conv_w, q_a_g=q_a_g, w_q_b=w_q_b, kv_a_g=kv_a_g, w_kv_b=w_kv_b, q_g=q_g, k_g=k_g, w_out=w_out, loss_target=loss_target, m_ada_w=m_ada_w, m_ada_b=m_ada_b, m_norm_g=m_norm_g, m_w_in=m_w_in, m_conv_w=m_conv_w, m_q_a_g=m_q_a_g, m_w_q_b=m_w_q_b, m_kv_a_g=m_kv_a_g, m_w_kv_b=m_w_kv_b, m_q_g=m_q_g, m_k_g=m_k_g, m_w_out=m_w_out, v_ada_w=v_ada_w, v_ada_b=v_ada_b, v_norm_g=v_norm_g, v_w_in=v_w_in, v_conv_w=v_conv_w, v_q_a_g=v_q_a_g, v_w_q_b=v_w_q_b, v_kv_a_g=v_kv_a_g, v_w_kv_b=v_w_kv_b, v_q_g=v_q_g, v_k_g=v_k_g, v_w_out=v_w_out)
    weights = {n: given[n] for n in TWIN_WEIGHTS}
    shared = {n: given[n] for n in SHARED_INPUTS}
    per_example = {n: given[n] for n in ['x', 'c', 'positions']}
    grad_fn = _jax.value_and_grad(_loss, argnums=(0, 1))

    def one_microbatch(ex, loss_target):
        ex = dict(ex)
        diff = ex.pop(TWIN_DIFF_INPUT)
        return grad_fn(weights, diff, {**shared, **ex}, loss_target)

    if N_MICROBATCH == 1:
        loss, (grad_w, grad_x) = one_microbatch(per_example, given["loss_target"])
    else:
        def body(carry, xs):
            loss_sum, grad_sum = carry
            l_k, (gw_k, gx_k) = one_microbatch(xs[0], xs[1])
            with _jax.named_scope("update"):
                return (loss_sum + l_k, _jax.tree.map(_jnp.add, grad_sum, gw_k)), gx_k

        init = (_jnp.zeros((), _jnp.float32), _jax.tree.map(_jnp.zeros_like, weights))
        (loss, grad_w), grad_x = _jax.lax.scan(body, init, (per_example, given["loss_target"]))
    with _jax.named_scope("update"):
        delta_w, new_m, new_v = {}, {}, {}
        for n in TWIN_WEIGHTS:
            delta_w[n], new_m[n], new_v[n] = _adamw(weights[n], grad_w[n], given["m_" + n], given["v_" + n])
    return (loss, grad_x, *[grad_w[n] for n in TWIN_WEIGHTS], *[delta_w[n] for n in TWIN_WEIGHTS],
            *[new_m[n] for n in TWIN_WEIGHTS], *[new_v[n] for n in TWIN_WEIGHTS])
```

```python
import functools
import math

import jax
import jax.numpy as jnp
from jax import lax
from jax.experimental import pallas as pl
from jax.experimental.pallas import tpu as pltpu

F32 = jnp.float32
BF16 = jnp.bfloat16
MESH = pl.DeviceIdType.MESH

D_MODEL = 2048
D_CONV = 1024
N_HEADS = 8
QK_NOPE = 128
QK_ROPE = 64
QK_HEAD = QK_NOPE + QK_ROPE
V_HEAD = 128
D_ATTN = N_HEADS * V_HEAD
Q_LORA = 512
KV_LORA = 256
ROPE_BASE = 10000.0
IN_COLS = 4 * D_CONV + Q_LORA + KV_LORA + QK_ROPE + D_ATTN
EPS = 1e-6
ADAM_LR, ADAM_B1, ADAM_B2, ADAM_EPS, ADAM_WD, ADAM_STEP = 0.001, 0.9, 0.999, 1e-08, 0.01, 10

N_DEV = 8
LANES = 128
QK_PAD = 256
U_COLS = 6144
U_ZA, U_CQ, U_CKV, U_KR = 4096, 5120, 5632, 5888
SCALE = 1.0 / math.sqrt(QK_HEAD)
NEG = -1e30
VMEM_LIMIT = 56 * 1024 * 1024

TM_ELEM = 256
TM_MM = 512
TQ = 512

ROWS_W_IN = D_MODEL * (IN_COLS // N_DEV) // LANES
ROWS_W_Q = Q_LORA * QK_HEAD // LANES
ROWS_W_KV = KV_LORA * (QK_NOPE + V_HEAD) // LANES
ROWS_W_OUT = (D_MODEL // N_DEV) * D_MODEL // LANES
ROWS_PACK = ROWS_W_IN + ROWS_W_Q + ROWS_W_KV + ROWS_W_OUT
SM_MOD, SM_NG, SM_QAG, SM_KVAG, SM_QG, SM_KG, SM_CONV, SM_LOSS = 0, 6144, 8192, 8704, 8960, 9216, 9472, 12544
SM_COLS = 12672


def _params(sem=None, collective=False):
    kw = dict(vmem_limit_bytes=VMEM_LIMIT)
    if sem is not None:
        kw["dimension_semantics"] = sem
    return pltpu.CompilerParams(**kw)


def _sigmoid(z):
    return 1.0 / (1.0 + jnp.exp(-z))


def _rot64(x):
    lane = lax.broadcasted_iota(jnp.int32, x.shape, 1)
    return jnp.where(lane < 32, pltpu.roll(x, 96, 1), pltpu.roll(x, 32, 1))


def _rope(x, cos, sin):
    return x * cos + _rot64(x) * sin


def _rope_t(d, cos, sin):
    return d * cos - _rot64(d) * sin


def _dot(a, b):
    return jnp.dot(a, b, preferred_element_type=F32)


def _dot_nt(a, b):
    return lax.dot_general(a, b, (((1,), (1,)), ((), ())), preferred_element_type=F32)


def _dot_tn(a, b):
    return lax.dot_general(a, b, (((0,), (0,)), ((), ())), preferred_element_type=F32)


def _my_index():
    return 4 * lax.axis_index("x") + 2 * lax.axis_index("y") + lax.axis_index("c")


def _all_gather(blocks, name):
    n = len(blocks)

    def body(*refs):
        ins, outs = refs[:n], refs[n:2 * n]
        send_sems, recv_sems, local_sems = refs[2 * n:]
        x, y, c = lax.axis_index("x"), lax.axis_index("y"), lax.axis_index("c")
        me, sibling = (x, y, c), (x, y, 1 - c)
        chips = [(1 - x, y), (x, 1 - y), (1 - x, 1 - y)]

        def slot(p):
            return 4 * p[0] + 2 * p[1] + p[2]

        def copy(a, k, block, to, src=None):
            dst = outs[a].at[slot(block)]
            return pltpu.make_async_remote_copy(
                src_ref=dst if src is None else src, dst_ref=dst,
                send_sem=send_sems.at[7 * a + k], recv_sem=recv_sems.at[7 * a + k],
                device_id=to, device_id_type=MESH)

        sends = []
        for a in range(n):
            mine = pltpu.make_async_copy(ins[a], outs[a].at[slot(me)], local_sems.at[a])
            mine.start()
            sends.append(mine)
        first = []
        for a in range(n):
            first.append(copy(a, 0, me, sibling, src=ins[a]))
            first += [copy(a, 1 + j, me, (*chip, c), src=ins[a]) for j, chip in enumerate(chips)]
        for cp in first:
            cp.start()
        passed = []
        for a in range(n):
            for j, chip in enumerate(chips):
                copy(a, 1 + j, (*chip, c), me).wait_recv()
                fwd = copy(a, 4 + j, (*chip, c), sibling)
                fwd.start()
                passed.append(fwd)
        for a in range(n):
            copy(a, 0, sibling, me).wait_recv()
            for j, chip in enumerate(chips):
                copy(a, 4 + j, (*chip, 1 - c), me).wait_recv()
        for cp in first + passed:
            cp.wait_send()
        for mine in sends:
            mine.wait()

    any_spec = pl.BlockSpec(memory_space=pl.ANY)
    return pl.pallas_call(
        body, name=name,
        out_shape=[jax.ShapeDtypeStruct((N_DEV,) + b.shape, b.dtype) for b in blocks],
        in_specs=[any_spec] * n, out_specs=[any_spec] * n,
        scratch_shapes=[pltpu.SemaphoreType.DMA((7 * n,)), pltpu.SemaphoreType.DMA((7 * n,)),
                        pltpu.SemaphoreType.DMA((n,))],
    )(*blocks)


def _rs_sibling(g):
    _, rows, cols = g.shape

    def body(g_ref, recv_ref, keep_ref, send_sems, recv_sems, local_sems):
        x, y, c = lax.axis_index("x"), lax.axis_index("y"), lax.axis_index("c")
        sibling = (x, y, 1 - c)
        copies, locals_ = [], []
        for k in range(4):
            cp = pltpu.make_async_remote_copy(
                src_ref=g_ref.at[2 * k + (1 - c)], dst_ref=recv_ref.at[k],
                send_sem=send_sems.at[k], recv_sem=recv_sems.at[k], device_id=sibling, device_id_type=MESH)
            cp.start()
            copies.append(cp)
            lc = pltpu.make_async_copy(g_ref.at[2 * k + c], keep_ref.at[k], local_sems.at[k])
            lc.start()
            locals_.append(lc)
        for cp in copies:
            cp.wait()
        for lc in locals_:
            lc.wait()

    any_spec = pl.BlockSpec(memory_space=pl.ANY)
    shape = jax.ShapeDtypeStruct((4, rows, cols), g.dtype)
    return pl.pallas_call(
        body, name="rs_sibling", out_shape=[shape, shape], in_specs=[any_spec], out_specs=[any_spec, any_spec],
        scratch_shapes=[pltpu.SemaphoreType.DMA((4,)), pltpu.SemaphoreType.DMA((4,)), pltpu.SemaphoreType.DMA((4,))],
    )(g)


def _rs_chips(p):
    _, rows, cols = p.shape

    def body(p_ref, out_ref, send_sems, recv_sems, local_sem):
        x, y, c = lax.axis_index("x"), lax.axis_index("y"), lax.axis_index("c")
        my_chip = 2 * x + y
        chips = [(1 - x, y), (x, 1 - y), (1 - x, 1 - y)]
        mine = pltpu.make_async_copy(p_ref.at[my_chip], out_ref.at[my_chip], local_sem)
        mine.start()
        copies = []
        for j, (px, py) in enumerate(chips):
            cp = pltpu.make_async_remote_copy(
                src_ref=p_ref.at[2 * px + py], dst_ref=out_ref.at[my_chip],
                send_sem=send_sems.at[j], recv_sem=recv_sems.at[j], device_id=(px, py, c), device_id_type=MESH)
            cp.start()
            copies.append(cp)
        for cp in copies:
            cp.wait()
        mine.wait()

    any_spec = pl.BlockSpec(memory_space=pl.ANY)
    return pl.pallas_call(
        body, name="rs_chips", out_shape=jax.ShapeDtypeStruct(p.shape, p.dtype), in_specs=[any_spec], out_specs=any_spec,
        scratch_shapes=[pltpu.SemaphoreType.DMA((3,)), pltpu.SemaphoreType.DMA((3,)), pltpu.SemaphoreType.DMA],
    )(p)


def _sum_leading(a, out_dtype, name):
    k, rows, cols = a.shape
    tr = min(rows, 1728 if rows % 1728 == 0 else rows)

    def body(a_ref, o_ref):
        acc = a_ref[0].astype(F32)
        for i in range(1, k):
            acc = acc + a_ref[i].astype(F32)
        o_ref[...] = acc.astype(out_dtype)

    return pl.pallas_call(
        body, name=name, grid=(rows // tr,),
        in_specs=[pl.BlockSpec((k, tr, cols), lambda i: (0, i, 0))],
        out_specs=pl.BlockSpec((tr, cols), lambda i: (i, 0)),
        out_shape=jax.ShapeDtypeStruct((rows, cols), out_dtype), compiler_params=_params(("parallel",)),
    )(a)


def _add_pairs(a, b, name):
    k, rows, cols = a.shape
    tr = 1728 if rows % 1728 == 0 else rows

    def body(a_ref, b_ref, o_ref):
        o_ref[...] = (a_ref[...].astype(F32) + b_ref[...].astype(F32)).astype(o_ref.dtype)

    spec = pl.BlockSpec((1, tr, cols), lambda i, j: (i, j, 0))
    return pl.pallas_call(
        body, name=name, grid=(k, rows // tr), in_specs=[spec, spec], out_specs=spec,
        out_shape=jax.ShapeDtypeStruct(a.shape, a.dtype), compiler_params=_params(("parallel", "parallel")),
    )(a, b)


def _ada_mod(c16, ada_w_l, ada_b_l):
    def body(c_ref, w_ref, b_ref, o_ref):
        cv = c_ref[...]
        sc = (cv * _sigmoid(cv)).astype(BF16)
        o_ref[...] = _dot(sc, w_ref[...].astype(BF16)) + b_ref[...]

    return pl.pallas_call(
        body, name="ada_mod", out_shape=jax.ShapeDtypeStruct((c16.shape[0], ada_w_l.shape[1]), F32),
        compiler_params=_params(),
    )(c16, ada_w_l, ada_b_l)


def _ada_w_grad(c_t, dmod_my):
    def body(c_ref, d_ref, o_ref):
        cv = c_ref[...]
        sc = cv * _sigmoid(cv)
        acc = sc[:, 0:1] * d_ref[0:1, :]
        for b in range(1, N_DEV):
            acc = acc + sc[:, b:b + 1] * d_ref[b:b + 1, :]
        o_ref[...] = acc

    return pl.pallas_call(
        body, name="ada_w_grad", out_shape=jax.ShapeDtypeStruct((c_t.shape[0], dmod_my.shape[1]), F32),
        compiler_params=_params(),
    )(c_t, dmod_my)


def _rope_tables(pos_col, invf, sign):
    s = pos_col.shape[0]
    tm = min(TM_MM, s)

    def body(p_ref, f_ref, s_ref, cos_ref, sin_ref):
        ang = p_ref[...].astype(F32) * f_ref[...]
        sg = s_ref[...]
        cos_ref[...] = jnp.cos(ang) * jnp.abs(sg)
        sin_ref[...] = jnp.sin(ang) * sg

    row = pl.BlockSpec((1, LANES), lambda i: (0, 0))
    tab = pl.BlockSpec((tm, LANES), lambda i: (i, 0))
    shape = jax.ShapeDtypeStruct((s, LANES), F32)
    return pl.pallas_call(
        body, name="rope_tables", grid=(s // tm,), in_specs=[pl.BlockSpec((tm, 1), lambda i: (i, 0)), row, row],
        out_specs=[tab, tab], out_shape=[shape, shape], compiler_params=_params(("parallel",)),
    )(pos_col, invf, sign)


def _norm_mod(x, norm_g, mod):
    s, d = x.shape
    tm = min(TM_MM, s)

    def body(x_ref, g_ref, mod_ref, h_ref):
        xv = x_ref[...]
        r = lax.rsqrt(jnp.mean(xv * xv, axis=-1, keepdims=True) + EPS)
        hn = xv * r * g_ref[...]
        h_ref[...] = (hn * (1.0 + mod_ref[:, d:2 * d]) + mod_ref[:, 0:d]).astype(BF16)

    return pl.pallas_call(
        body, name="norm_mod", grid=(s // tm,),
        in_specs=[pl.BlockSpec((tm, d), lambda i: (i, 0)), pl.BlockSpec((1, d), lambda i: (0, 0)),
                  pl.BlockSpec((1, 3 * d), lambda i: (0, 0))],
        out_specs=pl.BlockSpec((tm, d), lambda i: (i, 0)), out_shape=jax.ShapeDtypeStruct((s, d), BF16),
        compiler_params=_params(("parallel",)),
    )(x, norm_g, mod)


def _matmul(a, b, *, nt, out_dtype, tm, tn, name):
    m, kdim = a.shape
    n = b.shape[0] if nt else b.shape[1]
    tm, tn = min(tm, m), min(tn, n)

    def body(a_ref, b_ref, o_ref):
        o = _dot_nt(a_ref[...], b_ref[...]) if nt else _dot(a_ref[...], b_ref[...])
        o_ref[...] = o.astype(out_dtype)

    b_spec = pl.BlockSpec((tn, kdim), lambda j, i: (j, 0)) if nt else pl.BlockSpec((kdim, tn), lambda j, i: (0, j))
    return pl.pallas_call(
        body, name=name, grid=(n // tn, m // tm),
        in_specs=[pl.BlockSpec((tm, kdim), lambda j, i: (i, 0)), b_spec],
        out_specs=pl.BlockSpec((tm, tn), lambda j, i: (i, j)), out_shape=jax.ShapeDtypeStruct((m, n), out_dtype),
        compiler_params=_params(("parallel", "parallel")),
    )(a, b)


HALO = 16


def _conv_fwd(u, conv_w):
    s = u.shape[0]
    tm = min(TM_ELEM, s)
    cb = D_CONV

    def body(xc_ref, bc_ref, cc_ref, zc_ref, xp_ref, cp_ref, w_ref, y_ref):
        i = pl.program_id(0)
        uc = cc_ref[...].astype(F32) * xc_ref[...].astype(F32)
        up = cp_ref[...].astype(F32) * xp_ref[...].astype(F32)
        up = jnp.where(i == 0, 0.0, up)
        full = jnp.concatenate([up, uc], axis=0)
        u1 = pltpu.roll(full, 1, 0)[HALO:]
        u2 = pltpu.roll(full, 2, 0)[HALO:]
        w = w_ref[...]
        conv = w[0:1] * u2 + w[1:2] * u1 + w[2:3] * uc
        z = zc_ref[...].astype(F32)
        y_ref[...] = (bc_ref[...].astype(F32) * conv * (z * _sigmoid(z))).astype(BF16)

    def col(j):
        return pl.BlockSpec((tm, cb), lambda i: (i, j))

    def prev(j):
        return pl.BlockSpec((HALO, cb), lambda i: (jnp.maximum(i * (tm // HALO) - 1, 0), j))

    return pl.pallas_call(
        body, name="conv_fwd", grid=(s // tm,),
        in_specs=[col(0), col(1), col(2), col(3), prev(0), prev(2), pl.BlockSpec((3, cb), lambda i: (0, 0))],
        out_specs=pl.BlockSpec((tm, cb), lambda i: (i, 0)), out_shape=jax.ShapeDtypeStruct((s, cb), BF16),
        compiler_params=_params(("parallel",)),
    )(u, u, u, u, u, u, conv_w)


def _conv_bwd(u, dyc, conv_w, du):
    s = u.shape[0]
    tm = min(TM_ELEM, s)
    cb = D_CONV
    nt = s // tm

    def body(xc_ref, bc_ref, cc_ref, zc_ref, xp_ref, cp_ref, bn_ref, zn_ref, dy_ref, dyn_ref, w_ref, du_in, du_ref, dw_ref):
        del du_in
        i = pl.program_id(0)
        xc, cc = xc_ref[...].astype(F32), cc_ref[...].astype(F32)
        bc, z = bc_ref[...].astype(F32), zc_ref[...].astype(F32)
        uc = cc * xc
        up = jnp.where(i == 0, 0.0, cp_ref[...].astype(F32) * xp_ref[...].astype(F32))
        full = jnp.concatenate([up, uc], axis=0)
        u1 = pltpu.roll(full, 1, 0)[HALO:]
        u2 = pltpu.roll(full, 2, 0)[HALO:]
        w = w_ref[...]
        conv = w[0:1] * u2 + w[1:2] * u1 + w[2:3] * uc
        sg = _sigmoid(z)
        sz = z * sg
        dy = dy_ref[...].astype(F32)
        dconv = dy * bc * sz
        zn = zn_ref[...].astype(F32)
        dnext = dyn_ref[...].astype(F32) * bn_ref[...].astype(F32) * (zn * _sigmoid(zn))
        dnext = jnp.where(i == nt - 1, 0.0, dnext)
        fullb = jnp.concatenate([dconv, dnext], axis=0)
        nb = tm + HALO
        d1 = pltpu.roll(fullb, nb - 1, 0)[:tm]
        d2 = pltpu.roll(fullb, nb - 2, 0)[:tm]
        duc = w[2:3] * dconv + w[1:2] * d1 + w[0:1] * d2
        dzc = dy * bc * conv * (sg * (1.0 + z * (1.0 - sg)))
        du_ref[...] = jnp.concatenate([duc * cc, dy * conv * sz, duc * xc, dzc], axis=1).astype(BF16)
        dw = jnp.concatenate([jnp.sum(dconv * u2, axis=0, keepdims=True), jnp.sum(dconv * u1, axis=0, keepdims=True),
                              jnp.sum(dconv * uc, axis=0, keepdims=True), jnp.zeros((5, cb), F32)], axis=0)

        @pl.when(i == 0)
        def _():
            dw_ref[...] = dw

        @pl.when(i > 0)
        def _():
            dw_ref[...] += dw

    def col(j):
        return pl.BlockSpec((tm, cb), lambda i: (i, j))

    def prev(j):
        return pl.BlockSpec((HALO, cb), lambda i: (jnp.maximum(i * (tm // HALO) - 1, 0), j))

    def nxt(j):
        return pl.BlockSpec((HALO, cb), lambda i: (jnp.minimum((i + 1) * (tm // HALO), s // HALO - 1), j))

    return pl.pallas_call(
        body, name="conv_bwd", grid=(nt,),
        in_specs=[col(0), col(1), col(2), col(3), prev(0), prev(2), nxt(1), nxt(3), col(0), nxt(0),
                  pl.BlockSpec((3, cb), lambda i: (0, 0)), pl.BlockSpec(memory_space=pl.ANY)],
        out_specs=[pl.BlockSpec((tm, 4 * cb), lambda i: (i, 0)), pl.BlockSpec((8, cb), lambda i: (0, 0))],
        out_shape=[jax.ShapeDtypeStruct(du.shape, du.dtype), jax.ShapeDtypeStruct((8, cb), F32)],
        input_output_aliases={11: 0}, compiler_params=_params(("arbitrary",)),
    )(u, u, u, u, u, u, u, u, dyc, dyc, conv_w, du)


def _qkv_specs(tm):
    return [pl.BlockSpec((tm, Q_LORA), lambda i: (i, U_CQ // Q_LORA)),
            pl.BlockSpec((tm, KV_LORA), lambda i: (i, U_CKV // KV_LORA)),
            pl.BlockSpec((tm, LANES), lambda i: (i, U_KR // LANES)),
            pl.BlockSpec((tm, LANES), lambda i: (i, 0)), pl.BlockSpec((tm, LANES), lambda i: (i, 0))]


def _full(shape):
    return pl.BlockSpec(shape, lambda i: (0,) * len(shape))


def _qkv_fwd(u, cos, sin, wq, wkv, qag, kvag, qg, kg):
    s = u.shape[0]
    tm = min(TM_ELEM, s)

    def body(cq_ref, ckv_ref, kr_ref, cos_ref, sin_ref, wq_ref, wkv_ref, qag_ref, kvag_ref, qg_ref, kg_ref,
             q_ref, k_ref, v_ref):
        cq = cq_ref[...].astype(F32)
        cqn = cq * lax.rsqrt(jnp.mean(cq * cq, axis=-1, keepdims=True) + EPS) * qag_ref[...]
        qf = _dot(cqn.astype(BF16), wq_ref[...])
        ckv = ckv_ref[...].astype(F32)
        ckvn = ckv * lax.rsqrt(jnp.mean(ckv * ckv, axis=-1, keepdims=True) + EPS) * kvag_ref[...]
        kvf = _dot(ckvn.astype(BF16), wkv_ref[...])
        kr = kr_ref[...].astype(F32)
        cosv, sinv, qgv, kgv = cos_ref[...], sin_ref[...], qg_ref[...], kg_ref[...]
        ss_r = jnp.sum(kr * kr, axis=-1, keepdims=True)
        krr = _rope(kr * kgv[:, QK_NOPE:], cosv, sinv)
        for h in range(N_HEADS):
            qh = qf[:, QK_PAD * h:QK_PAD * (h + 1)]
            rq = lax.rsqrt(jnp.sum(qh * qh, axis=-1, keepdims=True) * (1.0 / QK_HEAD) + EPS)
            qn = qh * rq * qgv
            qo = jnp.concatenate([qn[:, :QK_NOPE], _rope(qn[:, QK_NOPE:], cosv, sinv)], axis=1) * SCALE
            q_ref[h] = qo.astype(BF16)
            kn = kvf[:, QK_NOPE * h:QK_NOPE * (h + 1)]
            rk = lax.rsqrt((jnp.sum(kn * kn, axis=-1, keepdims=True) + ss_r) * (1.0 / QK_HEAD) + EPS)
            k_ref[h] = jnp.concatenate([kn * kgv[:, :QK_NOPE] * rk, krr * rk], axis=1).astype(BF16)
            v_ref[h] = kvf[:, D_ATTN + V_HEAD * h:D_ATTN + V_HEAD * (h + 1)].astype(BF16)

    hq = N_HEADS * QK_PAD
    return pl.pallas_call(
        body, name="qkv_fwd", grid=(s // tm,),
        in_specs=_qkv_specs(tm) + [_full((Q_LORA, hq)), _full((KV_LORA, 2 * D_ATTN)), _full((1, Q_LORA)),
                                   _full((1, KV_LORA)), _full((1, QK_PAD)), _full((1, QK_PAD))],
        out_specs=[pl.BlockSpec((N_HEADS, tm, QK_PAD), lambda i: (0, i, 0)),
                   pl.BlockSpec((N_HEADS, tm, QK_PAD), lambda i: (0, i, 0)),
                   pl.BlockSpec((N_HEADS, tm, V_HEAD), lambda i: (0, i, 0))],
        out_shape=[jax.ShapeDtypeStruct((N_HEADS, s, QK_PAD), BF16), jax.ShapeDtypeStruct((N_HEADS, s, QK_PAD), BF16),
                   jax.ShapeDtypeStruct((N_HEADS, s, V_HEAD), BF16)],
        compiler_params=_params(("parallel",)),
    )(u, u, u, cos, sin, wq, wkv, qag, kvag, qg, kg)


def _qkv_bwd(u, cos, sin, dq, dk, dv, wq, wkv, qag, kvag, qg, kg, du):
    s = u.shape[0]
    tm = min(TM_ELEM, s)

    def body(cq_ref, ckv_ref, kr_ref, cos_ref, sin_ref, dq_ref, dk_ref, dv_ref, wq_ref, wkv_ref, qag_ref, kvag_ref,
             qg_ref, kg_ref, du_in, du_ref, dwq_ref, dwkv_ref, dqag_ref, dkvag_ref, dqg_ref, dkg_ref):
        del du_in
        i = pl.program_id(0)
        cq = cq_ref[...].astype(F32)
        rqa = lax.rsqrt(jnp.mean(cq * cq, axis=-1, keepdims=True) + EPS)
        xq = cq * rqa
        qagv = qag_ref[...]
        cqn = (xq * qagv).astype(BF16)
        qf = _dot(cqn, wq_ref[...])
        ckv = ckv_ref[...].astype(F32)
        rkva = lax.rsqrt(jnp.mean(ckv * ckv, axis=-1, keepdims=True) + EPS)
        xkv = ckv * rkva
        kvagv = kvag_ref[...]
        ckvn = (xkv * kvagv).astype(BF16)
        kvf = _dot(ckvn, wkv_ref[...])
        kr = kr_ref[...].astype(F32)
        cosv, sinv, qgv, kgv = cos_ref[...], sin_ref[...], qg_ref[...], kg_ref[...]
        ss_r = jnp.sum(kr * kr, axis=-1, keepdims=True)
        dqf, dkn = [], []
        dqg = jnp.zeros((1, QK_PAD), F32)
        dkg = jnp.zeros((1, QK_PAD), F32)
        dkr = jnp.zeros((tm, LANES), F32)
        for h in range(N_HEADS):
            qh = qf[:, QK_PAD * h:QK_PAD * (h + 1)]
            rq = lax.rsqrt(jnp.sum(qh * qh, axis=-1, keepdims=True) * (1.0 / QK_HEAD) + EPS)
            xh = qh * rq
            g = dq_ref[h].astype(F32) * SCALE
            dyq = jnp.concatenate([g[:, :QK_NOPE], _rope_t(g[:, QK_NOPE:], cosv, sinv)], axis=1)
            dqg = dqg + jnp.sum(dyq * xh, axis=0, keepdims=True)
            gdy = dyq * qgv
            dqf.append(rq * (gdy - xh * (jnp.sum(gdy * xh, axis=-1, keepdims=True) * (1.0 / QK_HEAD))))
            kn = kvf[:, QK_NOPE * h:QK_NOPE * (h + 1)]
            rk = lax.rsqrt((jnp.sum(kn * kn, axis=-1, keepdims=True) + ss_r) * (1.0 / QK_HEAD) + EPS)
            xk = jnp.concatenate([kn, kr], axis=1) * rk
            gk = dk_ref[h].astype(F32)
            dyk = jnp.concatenate([gk[:, :QK_NOPE], _rope_t(gk[:, QK_NOPE:], cosv, sinv)], axis=1)
            dkg = dkg + jnp.sum(dyk * xk, axis=0, keepdims=True)
            gdyk = dyk * kgv
            dxk = rk * (gdyk - xk * (jnp.sum(gdyk * xk, axis=-1, keepdims=True) * (1.0 / QK_HEAD)))
            dkn.append(dxk[:, :QK_NOPE])
            dkr = dkr + dxk[:, QK_NOPE:]
        dqf_b = jnp.concatenate(dqf, axis=1).astype(BF16)
        dkvf_b = jnp.concatenate([d.astype(BF16) for d in dkn] + [dv_ref[h] for h in range(N_HEADS)], axis=1)
        dwq = _dot_tn(cqn, dqf_b)
        dwkv = _dot_tn(ckvn, dkvf_b)
        dcqn = _dot_nt(dqf_b, wq_ref[...])
        dckvn = _dot_nt(dkvf_b, wkv_ref[...])
        dqag = jnp.sum(dcqn * xq, axis=0, keepdims=True)
        dkvag = jnp.sum(dckvn * xkv, axis=0, keepdims=True)
        gq = dcqn * qagv
        dcq = rqa * (gq - xq * jnp.mean(gq * xq, axis=-1, keepdims=True))
        gkv = dckvn * kvagv
        dckv = rkva * (gkv - xkv * jnp.mean(gkv * xkv, axis=-1, keepdims=True))
        du_ref[...] = jnp.concatenate([dcq, dckv, dkr, jnp.zeros((tm, LANES), F32)], axis=1).astype(BF16)

        @pl.when(i == 0)
        def _():
            dwq_ref[...] = dwq
            dwkv_ref[...] = dwkv
            dqag_ref[...] = dqag
            dkvag_ref[...] = dkvag
            dqg_ref[...] = dqg
            dkg_ref[...] = dkg

        @pl.when(i > 0)
        def _():
            dwq_ref[...] += dwq
            dwkv_ref[...] += dwkv
            dqag_ref[...] += dqag
            dkvag_ref[...] += dkvag
            dqg_ref[...] += dqg
            dkg_ref[...] += dkg

    hq = N_HEADS * QK_PAD
    head = lambda w: pl.BlockSpec((N_HEADS, tm, w), lambda i: (0, i, 0))
    return pl.pallas_call(
        body, name="qkv_bwd", grid=(s // tm,),
        in_specs=_qkv_specs(tm) + [head(QK_PAD), head(QK_PAD), head(V_HEAD), _full((Q_LORA, hq)),
                                   _full((KV_LORA, 2 * D_ATTN)), _full((1, Q_LORA)), _full((1, KV_LORA)),
                                   _full((1, QK_PAD)), _full((1, QK_PAD)), pl.BlockSpec(memory_space=pl.ANY)],
        out_specs=[pl.BlockSpec((tm, 1024), lambda i: (i, U_CQ // 1024)), _full((Q_LORA, hq)),
                   _full((KV_LORA, 2 * D_ATTN)), _full((1, Q_LORA)), _full((1, KV_LORA)), _full((1, QK_PAD)),
                   _full((1, QK_PAD))],
        out_shape=[jax.ShapeDtypeStruct(du.shape, du.dtype), jax.ShapeDtypeStruct((Q_LORA, hq), F32),
                   jax.ShapeDtypeStruct((KV_LORA, 2 * D_ATTN), F32), jax.ShapeDtypeStruct((1, Q_LORA), F32),
                   jax.ShapeDtypeStruct((1, KV_LORA), F32), jax.ShapeDtypeStruct((1, QK_PAD), F32),
                   jax.ShapeDtypeStruct((1, QK_PAD), F32)],
        input_output_aliases={14: 0}, compiler_params=_params(("arbitrary",)),
    )(u, u, u, cos, sin, dq, dk, dv, wq, wkv, qag, kvag, qg, kg, du)


def _flash_fwd(q, k, v):
    nh, s, _ = q.shape
    tq = min(TQ, s)
    nq = s // tq

    def body(q_ref, k_ref, v_ref, o_ref, lse_ref):
        i = pl.program_id(1)
        qv = q_ref[0]

        def step(j, carry, masked):
            m, l, acc = carry
            rows = pl.ds(pl.multiple_of(j * tq, tq), tq)
            sc = _dot_nt(qv, k_ref[0, rows, :])
            if masked:
                qi = lax.broadcasted_iota(jnp.int32, sc.shape, 0)
                ki = lax.broadcasted_iota(jnp.int32, sc.shape, 1)
                sc = jnp.where(ki <= qi, sc, NEG)
            m_new = jnp.maximum(m, jnp.max(sc, axis=-1, keepdims=True))
            p = jnp.exp(sc - m_new)
            alpha = jnp.exp(m - m_new)
            l = alpha * l + jnp.sum(p, axis=-1, keepdims=True)
            acc = alpha * acc + _dot(p.astype(BF16), v_ref[0, rows, :])
            return m_new, l, acc

        init = (jnp.full((tq, 1), NEG, F32), jnp.zeros((tq, 1), F32), jnp.zeros((tq, V_HEAD), F32))
        carry = lax.fori_loop(0, i, lambda j, cr: step(j, cr, False), init)
        m, l, acc = step(i, carry, True)
        o_ref[...] = (acc / l).astype(BF16)
        lse_ref[0] = m + jnp.log(l)

    return pl.pallas_call(
        body, name="flash_fwd", grid=(nh, nq),
        in_specs=[pl.BlockSpec((1, tq, QK_PAD), lambda h, i: (h, i, 0)),
                  pl.BlockSpec((1, s, QK_PAD), lambda h, i: (h, 0, 0)),
                  pl.BlockSpec((1, s, V_HEAD), lambda h, i: (h, 0, 0))],
        out_specs=[pl.BlockSpec((tq, V_HEAD), lambda h, i: (i, h)), pl.BlockSpec((1, tq, 1), lambda h, i: (h, i, 0))],
        out_shape=[jax.ShapeDtypeStruct((s, nh * V_HEAD), BF16), jax.ShapeDtypeStruct((nh, s, 1), F32)],
        compiler_params=_params(("parallel", "arbitrary")),
    )(q, k, v)


def _flash_bwd(q, k, v, do, lse, delta):
    nh, s, _ = q.shape
    tq = min(TQ, s)
    nq = s // tq

    def body(q_ref, k_ref, v_ref, do_ref, lse_ref, dl_ref, dq_ref, dk_ref, dv_ref, dq_acc):
        j = pl.program_id(1)

        @pl.when(j == 0)
        def _():
            dq_acc[...] = jnp.zeros_like(dq_acc)

        kj, vj = k_ref[0], v_ref[0]

        def step(i, carry, masked):
            dk, dv = carry
            rows = pl.ds(pl.multiple_of(i * tq, tq), tq)
            qi, doi = q_ref[0, rows, :], do_ref[rows, :]
            st = _dot_nt(kj, qi)
            pt = jnp.exp(st - lse_ref[0, pl.ds(i, 1), :])
            if masked:
                ki = lax.broadcasted_iota(jnp.int32, st.shape, 0)
                qx = lax.broadcasted_iota(jnp.int32, st.shape, 1)
                pt = jnp.where(ki <= qx, pt, 0.0)
            dv = dv + _dot(pt.astype(BF16), doi)
            dpt = _dot_nt(vj, doi)
            dst = (pt * (dpt - dl_ref[0, pl.ds(i, 1), :])).astype(BF16)
            dk = dk + _dot(dst, qi)
            dq_acc[rows, :] += _dot_tn(dst, kj)
            return dk, dv

        carry = step(j, (jnp.zeros((tq, QK_PAD), F32), jnp.zeros((tq, V_HEAD), F32)), True)
        dk, dv = lax.fori_loop(j + 1, nq, lambda i, cr: step(i, cr, False), carry)
        dk_ref[0] = dk.astype(BF16)
        dv_ref[0] = dv.astype(BF16)

        @pl.when(j == nq - 1)
        def _():
            dq_ref[0] = dq_acc[...].astype(BF16)

    return pl.pallas_call(
        body, name="flash_bwd", grid=(nh, nq),
        in_specs=[pl.BlockSpec((1, s, QK_PAD), lambda h, j: (h, 0, 0)),
                  pl.BlockSpec((1, tq, QK_PAD), lambda h, j: (h, j, 0)),
                  pl.BlockSpec((1, tq, V_HEAD), lambda h, j: (h, j, 0)),
                  pl.BlockSpec((s, V_HEAD), lambda h, j: (0, h)),
                  pl.BlockSpec((1, nq, tq), lambda h, j: (h, 0, 0)),
                  pl.BlockSpec((1, nq, tq), lambda h, j: (h, 0, 0))],
        out_specs=[pl.BlockSpec((1, s, QK_PAD), lambda h, j: (h, 0, 0)),
                   pl.BlockSpec((1, tq, QK_PAD), lambda h, j: (h, j, 0)),
                   pl.BlockSpec((1, tq, V_HEAD), lambda h, j: (h, j, 0))],
        out_shape=[jax.ShapeDtypeStruct((nh, s, QK_PAD), BF16), jax.ShapeDtypeStruct((nh, s, QK_PAD), BF16),
                   jax.ShapeDtypeStruct((nh, s, V_HEAD), BF16)],
        scratch_shapes=[pltpu.VMEM((s, QK_PAD), F32)],
        compiler_params=_params(("parallel", "arbitrary")),
    )(q, k, v, do, lse, delta)


def _tail(x, target, yc, o, u, mod, w_out):
    s, d = x.shape
    tm = min(TM_ELEM, s)

    def body(x_ref, t_ref, yc_ref, o_ref, za_ref, mod_ref, w_ref,
             gx_ref, dy_ref, ycat_ref, dyc_ref, do_ref, du_ref, delta_ref, dgate_ref, loss_ref):
        i = pl.program_id(0)
        za, ov = za_ref[...].astype(F32), o_ref[...].astype(F32)
        sg = _sigmoid(za)
        sl = za * sg
        ycat = jnp.concatenate([yc_ref[...], (ov * sl).astype(BF16)], axis=1)
        ycat_ref[...] = ycat
        y = _dot(ycat, w_ref[...])
        gate = mod_ref[:, 2 * d:3 * d]
        e = x_ref[...] + gate * y - t_ref[...]
        dout = e * (1.0 / d)
        gx_ref[...] = dout
        dy = (dout * gate).astype(BF16)
        dy_ref[...] = dy
        dycat = _dot_nt(dy, w_ref[...])
        dyc_ref[...] = dycat[:, :D_CONV].astype(BF16)
        dya = dycat[:, D_CONV:]
        dov = dya * sl
        do_ref[...] = dov.astype(BF16)
        du_ref[...] = (dya * ov * (sg * (1.0 + za * (1.0 - sg)))).astype(BF16)
        prod = dov * ov
        for h in range(N_HEADS):
            delta_ref[h] = jnp.sum(prod[:, V_HEAD * h:V_HEAD * (h + 1)], axis=-1, keepdims=True)
        dgate = jnp.sum(dout * y, axis=0, keepdims=True)
        part = jnp.sum(jnp.sum(e * e, axis=0, keepdims=True), axis=1, keepdims=True) * (0.5 / d)
        part = jnp.broadcast_to(part, (1, LANES))

        @pl.when(i == 0)
        def _():
            dgate_ref[...] = dgate
            loss_ref[...] = part

        @pl.when(i > 0)
        def _():
            dgate_ref[...] += dgate
            loss_ref[...] += part

    tok = lambda w: pl.BlockSpec((tm, w), lambda i: (i, 0))
    return pl.pallas_call(
        body, name="tail", grid=(s // tm,),
        in_specs=[tok(d), tok(d), tok(D_CONV), tok(D_ATTN), pl.BlockSpec((tm, D_ATTN), lambda i: (i, U_ZA // D_ATTN)),
                  _full((1, 3 * d)), _full((d, d))],
        out_specs=[tok(d), tok(d), tok(d), tok(D_CONV), tok(D_ATTN),
                   pl.BlockSpec((tm, D_ATTN), lambda i: (i, U_ZA // D_ATTN)),
                   pl.BlockSpec((N_HEADS, tm, 1), lambda i: (0, i, 0)), _full((1, d)), _full((1, LANES))],
        out_shape=[jax.ShapeDtypeStruct((s, d), F32), jax.ShapeDtypeStruct((s, d), BF16),
                   jax.ShapeDtypeStruct((s, d), BF16), jax.ShapeDtypeStruct((s, D_CONV), BF16),
                   jax.ShapeDtypeStruct((s, D_ATTN), BF16), jax.ShapeDtypeStruct((s, U_COLS), BF16),
                   jax.ShapeDtypeStruct((N_HEADS, s, 1), F32), jax.ShapeDtypeStruct((1, d), F32),
                   jax.ShapeDtypeStruct((1, LANES), F32)],
        compiler_params=_params(("arbitrary",)),
    )(x, target, yc, o, u, mod, w_out)


def _norm_bwd(x, dh, gx1, norm_g, mod):
    s, d = x.shape
    tm = min(TM_ELEM, s)

    def body(x_ref, dh_ref, gx_ref, g_ref, mod_ref, o_ref, dshift_ref, dscale_ref, dg_ref):
        i = pl.program_id(0)
        xv, dhv, gv = x_ref[...], dh_ref[...].astype(F32), g_ref[...]
        r = lax.rsqrt(jnp.mean(xv * xv, axis=-1, keepdims=True) + EPS)
        xn = xv * r
        dhn = dhv * (1.0 + mod_ref[:, d:2 * d])
        dxn = dhn * gv
        o_ref[...] = gx_ref[...] + r * (dxn - xn * jnp.mean(dxn * xn, axis=-1, keepdims=True))
        dshift = jnp.sum(dhv, axis=0, keepdims=True)
        dscale = jnp.sum(dhv * xn * gv, axis=0, keepdims=True)
        dg = jnp.sum(dhn * xn, axis=0, keepdims=True)

        @pl.when(i == 0)
        def _():
            dshift_ref[...] = dshift
            dscale_ref[...] = dscale
            dg_ref[...] = dg

        @pl.when(i > 0)
        def _():
            dshift_ref[...] += dshift
            dscale_ref[...] += dscale
            dg_ref[...] += dg

    tok = pl.BlockSpec((tm, d), lambda i: (i, 0))
    row = jax.ShapeDtypeStruct((1, d), F32)
    return pl.pallas_call(
        body, name="norm_bwd", grid=(s // tm,),
        in_specs=[tok, tok, tok, _full((1, d)), _full((1, 3 * d))],
        out_specs=[tok, _full((1, d)), _full((1, d)), _full((1, d))],
        out_shape=[jax.ShapeDtypeStruct((s, d), F32), row, row, row],
        compiler_params=_params(("arbitrary",)),
    )(x, dh, gx1, norm_g, mod)


def _adamw(w, g, m, v, name):
    rows, cols = w.shape
    tr = 256 if rows % 256 == 0 else rows

    def body(w_ref, g_ref, m_ref, v_ref, d_ref, nm_ref, nv_ref):
        gv = g_ref[...]
        nm = ADAM_B1 * m_ref[...] + (1.0 - ADAM_B1) * gv
        nv = ADAM_B2 * v_ref[...] + (1.0 - ADAM_B2) * (gv * gv)
        m_hat = nm / (1.0 - ADAM_B1 ** ADAM_STEP)
        v_hat = nv / (1.0 - ADAM_B2 ** ADAM_STEP)
        d_ref[...] = -ADAM_LR * (m_hat / (jnp.sqrt(v_hat) + ADAM_EPS) + ADAM_WD * w_ref[...])
        nm_ref[...] = nm
        nv_ref[...] = nv

    spec = pl.BlockSpec((tr, cols), lambda i: (i, 0))
    shape = jax.ShapeDtypeStruct((rows, cols), F32)
    return pl.pallas_call(
        body, name=name, grid=(rows // tr,), in_specs=[spec] * 4, out_specs=[spec] * 3, out_shape=[shape] * 3,
        compiler_params=_params(("parallel",)),
    )(w, g, m, v)


def _pad_cols(a, n):
    return jnp.pad(a, ((0, 0), (0, n - a.shape[1])))


def kernel(x, c, positions, ada_w, ada_b, norm_g, w_in, conv_w, q_a_g, w_q_b, kv_a_g, w_kv_b, q_g, k_g, w_out, loss_target, m_ada_w, m_ada_b, m_norm_g, m_w_in, m_conv_w, m_q_a_g, m_w_q_b, m_kv_a_g, m_w_kv_b, m_q_g, m_k_g, m_w_out, v_ada_w, v_ada_b, v_norm_g, v_w_in, v_conv_w, v_q_a_g, v_w_q_b, v_kv_a_g, v_w_kv_b, v_q_g, v_k_g, v_w_out):
    me = _my_index()
    s = x.shape[1]
    nq = s // min(TQ, s)
    x2, tgt = x[0], loss_target[0]
    w_in_l, w_q_l, w_kv_l, w_out_l, conv_l, ada_w_l = w_in[0], w_q_b[0], w_kv_b[0], w_out[0], conv_w[0], ada_w[0]
    ada_cols = ada_w_l.shape[1]

    pack = jnp.concatenate([w.astype(BF16).reshape(-1, LANES) for w in (w_in_l, w_q_l, w_kv_l, w_out_l)], axis=0)
    small = jnp.concatenate([c.reshape(-1, LANES), conv_l.reshape(-1, LANES), jnp.zeros((5, LANES), F32)], axis=0)
    pack_g, small_g = _all_gather([pack, small], "gather_weights")
    r0, r1, r2 = ROWS_W_IN, ROWS_W_IN + ROWS_W_Q, ROWS_W_IN + ROWS_W_Q + ROWS_W_KV
    w_in_g = pack_g[:, :r0].reshape(N_DEV, D_MODEL, IN_COLS // N_DEV).transpose(1, 0, 2).reshape(D_MODEL, IN_COLS)
    w_in_p = jnp.concatenate([w_in_g[:, :4 * D_CONV], w_in_g[:, IN_COLS - D_ATTN:], w_in_g[:, 4 * D_CONV:IN_COLS - D_ATTN],
                              jnp.zeros((D_MODEL, U_COLS - IN_COLS), BF16)], axis=1)
    wq_g = pack_g[:, r0:r1].reshape(N_DEV, Q_LORA, QK_HEAD).transpose(1, 0, 2)
    wq_p = jnp.pad(wq_g, ((0, 0), (0, 0), (0, QK_PAD - QK_HEAD))).reshape(Q_LORA, N_HEADS * QK_PAD)
    wkv_g = pack_g[:, r1:r2].reshape(N_DEV, KV_LORA, 2, QK_NOPE)
    wkv_p = wkv_g.transpose(1, 2, 0, 3).reshape(KV_LORA, 2 * D_ATTN)
    w_out_g = pack_g[:, r2:].reshape(D_MODEL, D_MODEL)
    c_all = small_g[:, :D_MODEL // LANES].reshape(N_DEV, D_MODEL)
    conv_g = small_g[:, D_MODEL // LANES:D_MODEL // LANES + 3].transpose(1, 0, 2).reshape(3, D_CONV)

    ada_b_l = lax.dynamic_slice(ada_b, (0, me * ada_cols), (1, ada_cols))
    mod_cols = _ada_mod(jnp.pad(c_all, ((0, 8), (0, 0))), ada_w_l, ada_b_l)[:N_DEV]
    (mod_g,) = _all_gather([mod_cols], "gather_mod")
    mod = lax.dynamic_index_in_dim(mod_g, me, axis=1, keepdims=False).reshape(1, 3 * D_MODEL)

    half = jnp.arange(0, QK_ROPE, 2, dtype=F32) / QK_ROPE
    inv_freq = ROPE_BASE ** (-half)
    zeros64 = jnp.zeros((LANES - QK_ROPE,), F32)
    invf = jnp.concatenate([inv_freq, inv_freq, zeros64]).reshape(1, LANES)
    sign = jnp.concatenate([-jnp.ones((32,), F32), jnp.ones((32,), F32), zeros64]).reshape(1, LANES)
    cos, sin = _rope_tables(positions.reshape(s, 1), invf, sign)
    qg_p, kg_p = _pad_cols(q_g, QK_PAD), _pad_cols(k_g, QK_PAD)

    h = _norm_mod(x2, norm_g, mod)
    u = _matmul(h, w_in_p, nt=False, out_dtype=BF16, tm=TM_MM, tn=1024, name="in_proj")
    yc = _conv_fwd(u, conv_g)
    q, k, v = _qkv_fwd(u, cos, sin, wq_p, wkv_p, q_a_g, kv_a_g, qg_p, kg_p)
    o, lse = _flash_fwd(q, k, v)
    gx1, dy, ycat, dyc, do, du, delta, dgate, loss_row = _tail(x2, tgt, yc, o, u, mod, w_out_g)

    dq, dk, dv = _flash_bwd(q, k, v, do, lse.reshape(N_HEADS, nq, s // nq), delta.reshape(N_HEADS, nq, s // nq))
    du, dconv = _conv_bwd(u, dyc, conv_g, du)
    du, dwq_p, dwkv_p, dqag, dkvag, dqg, dkg = _qkv_bwd(u, cos, sin, dq, dk, dv, wq_p, wkv_p, q_a_g, kv_a_g, qg_p, kg_p, du)
    dw_out = _matmul(ycat.T, dy, nt=False, out_dtype=F32, tm=TM_MM, tn=512, name="dw_out")
    dw_in_p = _matmul(h.T, du, nt=False, out_dtype=F32, tm=TM_MM, tn=512, name="dw_in")
    dh = _matmul(du, w_in_p, nt=True, out_dtype=BF16, tm=TM_MM, tn=512, name="dh")
    grad_x, dshift, dscale, dng = _norm_bwd(x2, dh, gx1, norm_g, mod)

    row = jnp.concatenate([dshift, dscale, dgate, dng, dqag, dkvag, dqg, dkg, dconv[:3].reshape(1, 3 * D_CONV), loss_row], axis=1)
    (rows_g,) = _all_gather([row], "gather_small")
    tot = _sum_leading(rows_g, F32, "sum_small")
    dmod_all = rows_g[:, 0, SM_MOD:SM_NG]
    g_ada_b = tot[:, SM_MOD:SM_NG]
    g_norm_g = tot[:, SM_NG:SM_QAG]
    g_q_a_g = tot[:, SM_QAG:SM_KVAG]
    g_kv_a_g = tot[:, SM_KVAG:SM_QG]
    g_q_g = tot[:, SM_QG:SM_QG + QK_HEAD]
    g_k_g = tot[:, SM_KG:SM_KG + QK_HEAD]
    conv_cols = conv_l.shape[1]
    g_conv = lax.dynamic_slice(tot[:, SM_CONV:SM_LOSS].reshape(3, D_CONV), (0, me * conv_cols), (3, conv_cols))
    loss = tot[0, SM_LOSS]
    dmod_my = lax.dynamic_slice(dmod_all, (0, me * ada_cols), (N_DEV, ada_cols))
    g_ada_w = _ada_w_grad(c_all.T, dmod_my)

    dw_in = jnp.concatenate([dw_in_p[:, :4 * D_CONV], dw_in_p[:, U_CQ:U_CQ + IN_COLS - 4 * D_CONV - D_ATTN],
                             dw_in_p[:, U_ZA:U_ZA + D_ATTN]], axis=1)
    p_in = dw_in.reshape(D_MODEL, N_DEV, IN_COLS // N_DEV).transpose(1, 0, 2).astype(BF16).reshape(N_DEV, ROWS_W_IN, LANES)
    dwq = dwq_p.reshape(Q_LORA, N_HEADS, QK_PAD)[:, :, :QK_HEAD]
    p_q = dwq.transpose(1, 0, 2).astype(BF16).reshape(N_DEV, ROWS_W_Q, LANES)
    dwkv = dwkv_p.reshape(KV_LORA, 2, N_HEADS, QK_NOPE).transpose(2, 0, 1, 3)
    p_kv = dwkv.astype(BF16).reshape(N_DEV, ROWS_W_KV, LANES)
    p_out = dw_out.astype(BF16).reshape(N_DEV, ROWS_W_OUT, LANES)
    gpack = jnp.concatenate([p_in, p_q, p_kv, p_out], axis=1)
    recv, keep = _rs_sibling(gpack)
    pair = _add_pairs(keep, recv, "rs_add")
    quad = _rs_chips(pair)
    gsum = _sum_leading(quad, F32, "rs_sum")
    g_w_in = gsum[:r0].reshape(w_in_l.shape)
    g_w_q = gsum[r0:r1].reshape(w_q_l.shape)
    g_w_kv = gsum[r1:r2].reshape(w_kv_l.shape)
    g_w_out = gsum[r2:].reshape(w_out_l.shape)

    grads = dict(ada_w=g_ada_w, ada_b=g_ada_b, norm_g=g_norm_g, w_in=g_w_in, conv_w=g_conv, q_a_g=g_q_a_g, w_q_b=g_w_q,
                 kv_a_g=g_kv_a_g, w_kv_b=g_w_kv, q_g=g_q_g, k_g=g_k_g, w_out=g_w_out)
    weights = dict(ada_w=(ada_w, m_ada_w, v_ada_w), ada_b=(ada_b, m_ada_b, v_ada_b), norm_g=(norm_g, m_norm_g, v_norm_g),
                   w_in=(w_in, m_w_in, v_w_in), conv_w=(conv_w, m_conv_w, v_conv_w), q_a_g=(q_a_g, m_q_a_g, v_q_a_g),
                   w_q_b=(w_q_b, m_w_q_b, v_w_q_b), kv_a_g=(kv_a_g, m_kv_a_g, v_kv_a_g), w_kv_b=(w_kv_b, m_w_kv_b, v_w_kv_b),
                   q_g=(q_g, m_q_g, v_q_g), k_g=(k_g, m_k_g, v_k_g), w_out=(w_out, m_w_out, v_w_out))
    names = list(grads)
    out_g, out_d, out_m, out_v = [], [], [], []
    for n in names:
        w, m, v_ = weights[n]
        shape2 = w.shape[-2:] if w.ndim == 3 else (1, w.shape[-1])
        g2 = grads[n].reshape(shape2)
        d2, m2, v2 = _adamw(w.reshape(shape2), g2, m.reshape(shape2), v_.reshape(shape2), "adamw_" + n)
        out_g.append(g2.reshape(w.shape))
        out_d.append(d2.reshape(w.shape))
        out_m.append(m2.reshape(w.shape))
        out_v.append(v2.reshape(w.shape))
    return (loss, grad_x.reshape(x.shape), *out_g, *out_d, *out_m, *out_v)
```

```python
import functools
import math

import jax
import jax.numpy as jnp
from jax import lax
from jax.experimental import pallas as pl
from jax.experimental.pallas import tpu as pltpu

F32 = jnp.float32
BF16 = jnp.bfloat16
MESH = pl.DeviceIdType.MESH

D_MODEL = 2048
D_CONV = 1024
N_HEADS = 8
QK_NOPE = 128
QK_ROPE = 64
QK_HEAD = QK_NOPE + QK_ROPE
V_HEAD = 128
D_ATTN = N_HEADS * V_HEAD
Q_LORA = 512
KV_LORA = 256
ROPE_BASE = 10000.0
IN_COLS = 4 * D_CONV + Q_LORA + KV_LORA + QK_ROPE + D_ATTN
EPS = 1e-6
ADAM_LR, ADAM_B1, ADAM_B2, ADAM_EPS, ADAM_WD, ADAM_STEP = 0.001, 0.9, 0.999, 1e-08, 0.01, 10

N_DEV = 8
LANES = 128
QK_PAD = 256
U_COLS = 6144
U_ZA, U_CQ, U_CKV, U_KR = 4096, 5120, 5632, 5888
SCALE = 1.0 / math.sqrt(QK_HEAD)
LOG2E = 1.4426950408889634
LN2 = 0.6931471805599453
NEG = -1e30
VMEM_LIMIT = 56 * 1024 * 1024

TM_ELEM = 256
TM_MM = 512
TQ = 1024

ROWS_W_IN = D_MODEL * (IN_COLS // N_DEV) // LANES
ROWS_W_Q = Q_LORA * QK_HEAD // LANES
ROWS_W_KV = KV_LORA * (QK_NOPE + V_HEAD) // LANES
ROWS_W_OUT = (D_MODEL // N_DEV) * D_MODEL // LANES
ROWS_PACK = ROWS_W_IN + ROWS_W_Q + ROWS_W_KV + ROWS_W_OUT
SM_MOD, SM_NG, SM_QAG, SM_KVAG, SM_QG, SM_KG, SM_CONV, SM_LOSS = 0, 6144, 8192, 8704, 8960, 9216, 9472, 12544
SM_COLS = 12672


def _params(sem=None, collective=False):
    kw = dict(vmem_limit_bytes=VMEM_LIMIT)
    if sem is not None:
        kw["dimension_semantics"] = sem
    return pltpu.CompilerParams(**kw)


def _sigmoid(z):
    return 1.0 / (1.0 + jnp.exp(-z))


def _rot64(x):
    lane = lax.broadcasted_iota(jnp.int32, x.shape, 1)
    return jnp.where(lane < 32, pltpu.roll(x, 96, 1), pltpu.roll(x, 32, 1))


def _rope(x, cos, sin):
    return x * cos + _rot64(x) * sin


def _rope_t(d, cos, sin):
    return d * cos - _rot64(d) * sin


def _dot(a, b):
    return jnp.dot(a, b, preferred_element_type=F32)


def _dot_nt(a, b):
    return lax.dot_general(a, b, (((1,), (1,)), ((), ())), preferred_element_type=F32)


def _dot_tn(a, b):
    return lax.dot_general(a, b, (((0,), (0,)), ((), ())), preferred_element_type=F32)


def _my_index():
    return 4 * lax.axis_index("x") + 2 * lax.axis_index("y") + lax.axis_index("c")


def _all_gather(blocks, name):
    n = len(blocks)

    def body(*refs):
        ins, outs = refs[:n], refs[n:2 * n]
        send_sems, recv_sems, local_sems = refs[2 * n:]
        x, y, c = lax.axis_index("x"), lax.axis_index("y"), lax.axis_index("c")
        me, sibling = (x, y, c), (x, y, 1 - c)
        chips = [(1 - x, y), (x, 1 - y), (1 - x, 1 - y)]

        def slot(p):
            return 4 * p[0] + 2 * p[1] + p[2]

        def copy(a, k, block, to, src=None):
            dst = outs[a].at[slot(block)]
            return pltpu.make_async_remote_copy(
                src_ref=dst if src is None else src, dst_ref=dst,
                send_sem=send_sems.at[7 * a + k], recv_sem=recv_sems.at[7 * a + k],
                device_id=to, device_id_type=MESH)

        sends = []
        for a in range(n):
            mine = pltpu.make_async_copy(ins[a], outs[a].at[slot(me)], local_sems.at[a])
            mine.start()
            sends.append(mine)
        first = []
        for a in range(n):
            first.append(copy(a, 0, me, sibling, src=ins[a]))
            first += [copy(a, 1 + j, me, (*chip, c), src=ins[a]) for j, chip in enumerate(chips)]
        for cp in first:
            cp.start()
        passed = []
        for a in range(n):
            for j, chip in enumerate(chips):
                copy(a, 1 + j, (*chip, c), me).wait_recv()
                fwd = copy(a, 4 + j, (*chip, c), sibling)
                fwd.start()
                passed.append(fwd)
        for a in range(n):
            copy(a, 0, sibling, me).wait_recv()
            for j, chip in enumerate(chips):
                copy(a, 4 + j, (*chip, 1 - c), me).wait_recv()
        for cp in first + passed:
            cp.wait_send()
        for mine in sends:
            mine.wait()

    any_spec = pl.BlockSpec(memory_space=pl.ANY)
    return pl.pallas_call(
        body, name=name,
        out_shape=[jax.ShapeDtypeStruct((N_DEV,) + b.shape, b.dtype) for b in blocks],
        in_specs=[any_spec] * n, out_specs=[any_spec] * n,
        scratch_shapes=[pltpu.SemaphoreType.DMA((7 * n,)), pltpu.SemaphoreType.DMA((7 * n,)),
                        pltpu.SemaphoreType.DMA((n,))],
    )(*blocks)


def _rs_sibling(g):
    _, rows, cols = g.shape

    def body(g_ref, recv_ref, send_sems, recv_sems):
        x, y, c = lax.axis_index("x"), lax.axis_index("y"), lax.axis_index("c")
        copies = []
        for k in range(4):
            cp = pltpu.make_async_remote_copy(
                src_ref=g_ref.at[2 * k + (1 - c)], dst_ref=recv_ref.at[k],
                send_sem=send_sems.at[k], recv_sem=recv_sems.at[k], device_id=(x, y, 1 - c), device_id_type=MESH)
            cp.start()
            copies.append(cp)
        for cp in copies:
            cp.wait()

    any_spec = pl.BlockSpec(memory_space=pl.ANY)
    return pl.pallas_call(
        body, name="rs_sibling", out_shape=jax.ShapeDtypeStruct((4, rows, cols), g.dtype),
        in_specs=[any_spec], out_specs=any_spec,
        scratch_shapes=[pltpu.SemaphoreType.DMA((4,)), pltpu.SemaphoreType.DMA((4,))],
    )(g)


def _rs_chips(p):
    _, rows, cols = p.shape

    def body(p_ref, out_ref, send_sems, recv_sems, local_sem):
        x, y, c = lax.axis_index("x"), lax.axis_index("y"), lax.axis_index("c")
        my_chip = 2 * x + y
        chips = [(1 - x, y), (x, 1 - y), (1 - x, 1 - y)]
        mine = pltpu.make_async_copy(p_ref.at[my_chip], out_ref.at[my_chip], local_sem)
        mine.start()
        copies = []
        for j, (px, py) in enumerate(chips):
            cp = pltpu.make_async_remote_copy(
                src_ref=p_ref.at[2 * px + py], dst_ref=out_ref.at[my_chip],
                send_sem=send_sems.at[j], recv_sem=recv_sems.at[j], device_id=(px, py, c), device_id_type=MESH)
            cp.start()
            copies.append(cp)
        for cp in copies:
            cp.wait()
        mine.wait()

    any_spec = pl.BlockSpec(memory_space=pl.ANY)
    return pl.pallas_call(
        body, name="rs_chips", out_shape=jax.ShapeDtypeStruct(p.shape, p.dtype), in_specs=[any_spec], out_specs=any_spec,
        scratch_shapes=[pltpu.SemaphoreType.DMA((3,)), pltpu.SemaphoreType.DMA((3,)), pltpu.SemaphoreType.DMA],
    )(p)


def _sum_leading(a, out_dtype, name):
    k, rows, cols = a.shape
    tr = min(rows, 1728 if rows % 1728 == 0 else rows)

    def body(a_ref, o_ref):
        acc = a_ref[0].astype(F32)
        for i in range(1, k):
            acc = acc + a_ref[i].astype(F32)
        o_ref[...] = acc.astype(out_dtype)

    return pl.pallas_call(
        body, name=name, grid=(rows // tr,),
        in_specs=[pl.BlockSpec((k, tr, cols), lambda i: (0, i, 0))],
        out_specs=pl.BlockSpec((tr, cols), lambda i: (i, 0)),
        out_shape=jax.ShapeDtypeStruct((rows, cols), out_dtype), compiler_params=_params(("parallel",)),
    )(a)


def _add_pairs(g, recv, core, name):
    k, rows, cols = recv.shape
    tr = 1728 if rows % 1728 == 0 else rows

    def body(c_ref, g_ref, r_ref, o_ref):
        del c_ref
        o_ref[...] = (g_ref[...].astype(F32) + r_ref[...].astype(F32)).astype(o_ref.dtype)

    spec = pl.BlockSpec((1, tr, cols), lambda i, j, c: (i, j, 0))
    grid_spec = pltpu.PrefetchScalarGridSpec(
        num_scalar_prefetch=1, grid=(k, rows // tr),
        in_specs=[pl.BlockSpec((1, tr, cols), lambda i, j, c: (2 * i + c[0], j, 0)), spec], out_specs=spec)
    return pl.pallas_call(
        body, name=name, grid_spec=grid_spec, out_shape=jax.ShapeDtypeStruct(recv.shape, recv.dtype),
        compiler_params=_params(("parallel", "parallel")),
    )(core, g, recv)


def _ada_mod(c16, ada_w_l, ada_b_l):
    def body(c_ref, w_ref, b_ref, o_ref):
        cv = c_ref[...]
        sc = (cv * _sigmoid(cv)).astype(BF16)
        o_ref[...] = _dot(sc, w_ref[...].astype(BF16)) + b_ref[...]

    return pl.pallas_call(
        body, name="ada_mod", out_shape=jax.ShapeDtypeStruct((c16.shape[0], ada_w_l.shape[1]), F32),
        compiler_params=_params(),
    )(c16, ada_w_l, ada_b_l)


def _ada_w_grad(c_t, dmod_my):
    def body(c_ref, d_ref, o_ref):
        cv = c_ref[...]
        sc = cv * _sigmoid(cv)
        acc = sc[:, 0:1] * d_ref[0:1, :]
        for b in range(1, N_DEV):
            acc = acc + sc[:, b:b + 1] * d_ref[b:b + 1, :]
        o_ref[...] = acc

    return pl.pallas_call(
        body, name="ada_w_grad", out_shape=jax.ShapeDtypeStruct((c_t.shape[0], dmod_my.shape[1]), F32),
        compiler_params=_params(),
    )(c_t, dmod_my)


def _rope_tables(pos_col, invf, sign):
    s = pos_col.shape[0]
    tm = min(TM_MM, s)

    def body(p_ref, f_ref, s_ref, cos_ref, sin_ref):
        ang = p_ref[...].astype(F32) * f_ref[...]
        sg = s_ref[...]
        cos_ref[...] = jnp.cos(ang) * jnp.abs(sg)
        sin_ref[...] = jnp.sin(ang) * sg

    row = pl.BlockSpec((1, LANES), lambda i: (0, 0))
    tab = pl.BlockSpec((tm, LANES), lambda i: (i, 0))
    shape = jax.ShapeDtypeStruct((s, LANES), F32)
    return pl.pallas_call(
        body, name="rope_tables", grid=(s // tm,), in_specs=[pl.BlockSpec((tm, 1), lambda i: (i, 0)), row, row],
        out_specs=[tab, tab], out_shape=[shape, shape], compiler_params=_params(("parallel",)),
    )(pos_col, invf, sign)


def _norm_mod(x, norm_g, mod):
    s, d = x.shape
    tm = min(TM_MM, s)

    def body(x_ref, g_ref, mod_ref, h_ref, ht_ref):
        xv = x_ref[...]
        r = lax.rsqrt(jnp.mean(xv * xv, axis=-1, keepdims=True) + EPS)
        hn = xv * r * g_ref[...]
        hv = hn * (1.0 + mod_ref[:, d:2 * d]) + mod_ref[:, 0:d]
        h_ref[...] = hv.astype(BF16)
        ht_ref[...] = hv.T.astype(BF16)

    return pl.pallas_call(
        body, name="norm_mod", grid=(s // tm,),
        in_specs=[pl.BlockSpec((tm, d), lambda i: (i, 0)), pl.BlockSpec((1, d), lambda i: (0, 0)),
                  pl.BlockSpec((1, 3 * d), lambda i: (0, 0))],
        out_specs=[pl.BlockSpec((tm, d), lambda i: (i, 0)), pl.BlockSpec((d, tm), lambda i: (0, i))],
        out_shape=[jax.ShapeDtypeStruct((s, d), BF16), jax.ShapeDtypeStruct((d, s), BF16)],
        compiler_params=_params(("parallel",)),
    )(x, norm_g, mod)


def _matmul(a, b, *, nt, out_dtype, tm, tn, name):
    m, kdim = a.shape
    n = b.shape[0] if nt else b.shape[1]
    tm, tn = min(tm, m), min(tn, n)

    def body(a_ref, b_ref, o_ref):
        o = _dot_nt(a_ref[...], b_ref[...]) if nt else _dot(a_ref[...], b_ref[...])
        o_ref[...] = o.astype(out_dtype)

    b_spec = pl.BlockSpec((tn, kdim), lambda j, i: (j, 0)) if nt else pl.BlockSpec((kdim, tn), lambda j, i: (0, j))
    return pl.pallas_call(
        body, name=name, grid=(n // tn, m // tm),
        in_specs=[pl.BlockSpec((tm, kdim), lambda j, i: (i, 0)), b_spec],
        out_specs=pl.BlockSpec((tm, tn), lambda j, i: (i, j)), out_shape=jax.ShapeDtypeStruct((m, n), out_dtype),
        compiler_params=_params(("parallel", "parallel")),
    )(a, b)


HALO = 16


def _conv_fwd(u, conv_w):
    s = u.shape[0]
    tm = min(TM_ELEM, s)
    cb = D_CONV

    def body(xc_ref, bc_ref, cc_ref, zc_ref, xp_ref, cp_ref, w_ref, y_ref):
        i = pl.program_id(0)
        uc = cc_ref[...].astype(F32) * xc_ref[...].astype(F32)
        up = cp_ref[...].astype(F32) * xp_ref[...].astype(F32)
        up = jnp.where(i == 0, 0.0, up)
        full = jnp.concatenate([up, uc], axis=0)
        u1 = pltpu.roll(full, 1, 0)[HALO:]
        u2 = pltpu.roll(full, 2, 0)[HALO:]
        w = w_ref[...]
        conv = w[0:1] * u2 + w[1:2] * u1 + w[2:3] * uc
        z = zc_ref[...].astype(F32)
        y_ref[...] = (bc_ref[...].astype(F32) * conv * (z * _sigmoid(z))).astype(BF16)

    def col(j):
        return pl.BlockSpec((tm, cb), lambda i: (i, j))

    def prev(j):
        return pl.BlockSpec((HALO, cb), lambda i: (jnp.maximum(i * (tm // HALO) - 1, 0), j))

    return pl.pallas_call(
        body, name="conv_fwd", grid=(s // tm,),
        in_specs=[col(0), col(1), col(2), col(3), prev(0), prev(2), pl.BlockSpec((3, cb), lambda i: (0, 0))],
        out_specs=pl.BlockSpec((tm, cb), lambda i: (i, 0)), out_shape=jax.ShapeDtypeStruct((s, cb), BF16),
        compiler_params=_params(("parallel",)),
    )(u, u, u, u, u, u, conv_w)


def _conv_bwd(u, dyc, conv_w, du):
    s = u.shape[0]
    tm = min(TM_ELEM, s)
    cb = D_CONV
    nt = s // tm

    def body(xc_ref, bc_ref, cc_ref, zc_ref, xp_ref, cp_ref, bn_ref, zn_ref, dy_ref, dyn_ref, w_ref, du_in, du_ref, dw_ref):
        del du_in
        i = pl.program_id(0)
        xc, cc = xc_ref[...].astype(F32), cc_ref[...].astype(F32)
        bc, z = bc_ref[...].astype(F32), zc_ref[...].astype(F32)
        uc = cc * xc
        up = jnp.where(i == 0, 0.0, cp_ref[...].astype(F32) * xp_ref[...].astype(F32))
        full = jnp.concatenate([up, uc], axis=0)
        u1 = pltpu.roll(full, 1, 0)[HALO:]
        u2 = pltpu.roll(full, 2, 0)[HALO:]
        w = w_ref[...]
        conv = w[0:1] * u2 + w[1:2] * u1 + w[2:3] * uc
        sg = _sigmoid(z)
        sz = z * sg
        dy = dy_ref[...].astype(F32)
        dconv = dy * bc * sz
        zn = zn_ref[...].astype(F32)
        dnext = dyn_ref[...].astype(F32) * bn_ref[...].astype(F32) * (zn * _sigmoid(zn))
        dnext = jnp.where(i == nt - 1, 0.0, dnext)
        fullb = jnp.concatenate([dconv, dnext], axis=0)
        nb = tm + HALO
        d1 = pltpu.roll(fullb, nb - 1, 0)[:tm]
        d2 = pltpu.roll(fullb, nb - 2, 0)[:tm]
        duc = w[2:3] * dconv + w[1:2] * d1 + w[0:1] * d2
        dzc = dy * bc * conv * (sg * (1.0 + z * (1.0 - sg)))
        du_ref[...] = jnp.concatenate([duc * cc, dy * conv * sz, duc * xc, dzc], axis=1).astype(BF16)
        dw = jnp.concatenate([jnp.sum(dconv * u2, axis=0, keepdims=True), jnp.sum(dconv * u1, axis=0, keepdims=True),
                              jnp.sum(dconv * uc, axis=0, keepdims=True), jnp.zeros((5, cb), F32)], axis=0)

        @pl.when(i == 0)
        def _():
            dw_ref[...] = dw

        @pl.when(i > 0)
        def _():
            dw_ref[...] += dw

    def col(j):
        return pl.BlockSpec((tm, cb), lambda i: (i, j))

    def prev(j):
        return pl.BlockSpec((HALO, cb), lambda i: (jnp.maximum(i * (tm // HALO) - 1, 0), j))

    def nxt(j):
        return pl.BlockSpec((HALO, cb), lambda i: (jnp.minimum((i + 1) * (tm // HALO), s // HALO - 1), j))

    return pl.pallas_call(
        body, name="conv_bwd", grid=(nt,),
        in_specs=[col(0), col(1), col(2), col(3), prev(0), prev(2), nxt(1), nxt(3), col(0), nxt(0),
                  pl.BlockSpec((3, cb), lambda i: (0, 0)), pl.BlockSpec(memory_space=pl.ANY)],
        out_specs=[pl.BlockSpec((tm, 4 * cb), lambda i: (i, 0)), pl.BlockSpec((8, cb), lambda i: (0, 0))],
        out_shape=[jax.ShapeDtypeStruct(du.shape, du.dtype), jax.ShapeDtypeStruct((8, cb), F32)],
        input_output_aliases={11: 0}, compiler_params=_params(("arbitrary",)),
    )(u, u, u, u, u, u, u, u, dyc, dyc, conv_w, du)


def _qkv_specs(tm):
    return [pl.BlockSpec((tm, Q_LORA), lambda i: (i, U_CQ // Q_LORA)),
            pl.BlockSpec((tm, KV_LORA), lambda i: (i, U_CKV // KV_LORA)),
            pl.BlockSpec((tm, LANES), lambda i: (i, U_KR // LANES)),
            pl.BlockSpec((tm, LANES), lambda i: (i, 0)), pl.BlockSpec((tm, LANES), lambda i: (i, 0))]


def _full(shape):
    return pl.BlockSpec(shape, lambda i: (0,) * len(shape))


def _qkv_fwd(u, cos, sin, wq, wkv, qag, kvag, qg, kg):
    s = u.shape[0]
    tm = min(TM_ELEM, s)

    def body(cq_ref, ckv_ref, kr_ref, cos_ref, sin_ref, wq_ref, wkv_ref, qag_ref, kvag_ref, qg_ref, kg_ref,
             q_ref, k_ref, v_ref):
        cq = cq_ref[...].astype(F32)
        cqn = cq * lax.rsqrt(jnp.mean(cq * cq, axis=-1, keepdims=True) + EPS) * qag_ref[...]
        qf = _dot(cqn.astype(BF16), wq_ref[...])
        ckv = ckv_ref[...].astype(F32)
        ckvn = ckv * lax.rsqrt(jnp.mean(ckv * ckv, axis=-1, keepdims=True) + EPS) * kvag_ref[...]
        kvf = _dot(ckvn.astype(BF16), wkv_ref[...])
        kr = kr_ref[...].astype(F32)
        cosv, sinv, qgv, kgv = cos_ref[...], sin_ref[...], qg_ref[...], kg_ref[...]
        ss_r = jnp.sum(kr * kr, axis=-1, keepdims=True)
        krr = _rope(kr * kgv[:, QK_NOPE:], cosv, sinv)
        for h in range(N_HEADS):
            qh = qf[:, QK_PAD * h:QK_PAD * (h + 1)]
            rq = lax.rsqrt(jnp.sum(qh * qh, axis=-1, keepdims=True) * (1.0 / QK_HEAD) + EPS)
            qn = qh * rq * qgv
            qo = jnp.concatenate([qn[:, :QK_NOPE], _rope(qn[:, QK_NOPE:], cosv, sinv)], axis=1) * (SCALE * LOG2E)
            q_ref[h] = qo.astype(BF16)
            kn = kvf[:, QK_NOPE * h:QK_NOPE * (h + 1)]
            rk = lax.rsqrt((jnp.sum(kn * kn, axis=-1, keepdims=True) + ss_r) * (1.0 / QK_HEAD) + EPS)
            k_ref[h] = jnp.concatenate([kn * kgv[:, :QK_NOPE] * rk, krr * rk], axis=1).astype(BF16)
            vh = kvf[:, D_ATTN + V_HEAD * h:D_ATTN + V_HEAD * (h + 1)]
            v_ref[h] = jnp.concatenate([vh, jnp.ones_like(vh)], axis=1).astype(BF16)

    hq = N_HEADS * QK_PAD
    return pl.pallas_call(
        body, name="qkv_fwd", grid=(s // tm,),
        in_specs=_qkv_specs(tm) + [_full((Q_LORA, hq)), _full((KV_LORA, 2 * D_ATTN)), _full((1, Q_LORA)),
                                   _full((1, KV_LORA)), _full((1, QK_PAD)), _full((1, QK_PAD))],
        out_specs=[pl.BlockSpec((N_HEADS, tm, QK_PAD), lambda i: (0, i, 0)),
                   pl.BlockSpec((N_HEADS, tm, QK_PAD), lambda i: (0, i, 0)),
                   pl.BlockSpec((N_HEADS, tm, 2 * V_HEAD), lambda i: (0, i, 0))],
        out_shape=[jax.ShapeDtypeStruct((N_HEADS, s, QK_PAD), BF16), jax.ShapeDtypeStruct((N_HEADS, s, QK_PAD), BF16),
                   jax.ShapeDtypeStruct((N_HEADS, s, 2 * V_HEAD), BF16)],
        compiler_params=_params(("parallel",)),
    )(u, u, u, cos, sin, wq, wkv, qag, kvag, qg, kg)


def _qkv_bwd(u, cos, sin, dq, dk, dv, wq, wkv, qag, kvag, qg, kg, du):
    s = u.shape[0]
    tm = min(TM_ELEM, s)

    def body(cq_ref, ckv_ref, kr_ref, cos_ref, sin_ref, dq_ref, dk_ref, dv_ref, wq_ref, wkv_ref, qag_ref, kvag_ref,
             qg_ref, kg_ref, du_in, du_ref, dwq_ref, dwkv_ref, dqag_ref, dkvag_ref, dqg_ref, dkg_ref):
        del du_in
        i = pl.program_id(0)
        cq = cq_ref[...].astype(F32)
        rqa = lax.rsqrt(jnp.mean(cq * cq, axis=-1, keepdims=True) + EPS)
        xq = cq * rqa
        qagv = qag_ref[...]
        cqn = (xq * qagv).astype(BF16)
        qf = _dot(cqn, wq_ref[...])
        ckv = ckv_ref[...].astype(F32)
        rkva = lax.rsqrt(jnp.mean(ckv * ckv, axis=-1, keepdims=True) + EPS)
        xkv = ckv * rkva
        kvagv = kvag_ref[...]
        ckvn = (xkv * kvagv).astype(BF16)
        kvf = _dot(ckvn, wkv_ref[...])
        kr = kr_ref[...].astype(F32)
        cosv, sinv, qgv, kgv = cos_ref[...], sin_ref[...], qg_ref[...], kg_ref[...]
        ss_r = jnp.sum(kr * kr, axis=-1, keepdims=True)
        dqf, dkn = [], []
        dqg = jnp.zeros((1, QK_PAD), F32)
        dkg = jnp.zeros((1, QK_PAD), F32)
        dkr = jnp.zeros((tm, LANES), F32)
        for h in range(N_HEADS):
            qh = qf[:, QK_PAD * h:QK_PAD * (h + 1)]
            rq = lax.rsqrt(jnp.sum(qh * qh, axis=-1, keepdims=True) * (1.0 / QK_HEAD) + EPS)
            xh = qh * rq
            g = dq_ref[h].astype(F32) * SCALE
            dyq = jnp.concatenate([g[:, :QK_NOPE], _rope_t(g[:, QK_NOPE:], cosv, sinv)], axis=1)
            dqg = dqg + jnp.sum(dyq * xh, axis=0, keepdims=True)
            gdy = dyq * qgv
            dqf.append(rq * (gdy - xh * (jnp.sum(gdy * xh, axis=-1, keepdims=True) * (1.0 / QK_HEAD))))
            kn = kvf[:, QK_NOPE * h:QK_NOPE * (h + 1)]
            rk = lax.rsqrt((jnp.sum(kn * kn, axis=-1, keepdims=True) + ss_r) * (1.0 / QK_HEAD) + EPS)
            xk = jnp.concatenate([kn, kr], axis=1) * rk
            gk = dk_ref[h].astype(F32)
            dyk = jnp.concatenate([gk[:, :QK_NOPE], _rope_t(gk[:, QK_NOPE:], cosv, sinv)], axis=1)
            dkg = dkg + jnp.sum(dyk * xk, axis=0, keepdims=True)
            gdyk = dyk * kgv
            dxk = rk * (gdyk - xk * (jnp.sum(gdyk * xk, axis=-1, keepdims=True) * (1.0 / QK_HEAD)))
            dkn.append(dxk[:, :QK_NOPE])
            dkr = dkr + dxk[:, QK_NOPE:]
        dqf_b = jnp.concatenate(dqf, axis=1).astype(BF16)
        dkvf_b = jnp.concatenate([d.astype(BF16) for d in dkn] + [dv_ref[h] for h in range(N_HEADS)], axis=1)
        dwq = _dot_tn(cqn, dqf_b)
        dwkv = _dot_tn(ckvn, dkvf_b)
        dcqn = _dot_nt(dqf_b, wq_ref[...])
        dckvn = _dot_nt(dkvf_b, wkv_ref[...])
        dqag = jnp.sum(dcqn * xq, axis=0, keepdims=True)
        dkvag = jnp.sum(dckvn * xkv, axis=0, keepdims=True)
        gq = dcqn * qagv
        dcq = rqa * (gq - xq * jnp.mean(gq * xq, axis=-1, keepdims=True))
        gkv = dckvn * kvagv
        dckv = rkva * (gkv - xkv * jnp.mean(gkv * xkv, axis=-1, keepdims=True))
        du_ref[...] = jnp.concatenate([dcq, dckv, dkr, jnp.zeros((tm, LANES), F32)], axis=1).astype(BF16)

        @pl.when(i == 0)
        def _():
            dwq_ref[...] = dwq
            dwkv_ref[...] = dwkv
            dqag_ref[...] = dqag
            dkvag_ref[...] = dkvag
            dqg_ref[...] = dqg
            dkg_ref[...] = dkg

        @pl.when(i > 0)
        def _():
            dwq_ref[...] += dwq
            dwkv_ref[...] += dwkv
            dqag_ref[...] += dqag
            dkvag_ref[...] += dkvag
            dqg_ref[...] += dqg
            dkg_ref[...] += dkg

    hq = N_HEADS * QK_PAD
    head = lambda w: pl.BlockSpec((N_HEADS, tm, w), lambda i: (0, i, 0))
    return pl.pallas_call(
        body, name="qkv_bwd", grid=(s // tm,),
        in_specs=_qkv_specs(tm) + [head(QK_PAD), head(QK_PAD), head(V_HEAD), _full((Q_LORA, hq)),
                                   _full((KV_LORA, 2 * D_ATTN)), _full((1, Q_LORA)), _full((1, KV_LORA)),
                                   _full((1, QK_PAD)), _full((1, QK_PAD)), pl.BlockSpec(memory_space=pl.ANY)],
        out_specs=[pl.BlockSpec((tm, 1024), lambda i: (i, U_CQ // 1024)), _full((Q_LORA, hq)),
                   _full((KV_LORA, 2 * D_ATTN)), _full((1, Q_LORA)), _full((1, KV_LORA)), _full((1, QK_PAD)),
                   _full((1, QK_PAD))],
        out_shape=[jax.ShapeDtypeStruct(du.shape, du.dtype), jax.ShapeDtypeStruct((Q_LORA, hq), F32),
                   jax.ShapeDtypeStruct((KV_LORA, 2 * D_ATTN), F32), jax.ShapeDtypeStruct((1, Q_LORA), F32),
                   jax.ShapeDtypeStruct((1, KV_LORA), F32), jax.ShapeDtypeStruct((1, QK_PAD), F32),
                   jax.ShapeDtypeStruct((1, QK_PAD), F32)],
        input_output_aliases={14: 0}, compiler_params=_params(("arbitrary",)),
    )(u, u, u, cos, sin, dq, dk, dv, wq, wkv, qag, kvag, qg, kg, du)


def _flash_fwd(q, k, v):
    nh, s, _ = q.shape
    tq = min(TQ, s)
    tk = tq // 2
    nq = s // tq

    def body(q_ref, k_ref, v_ref, o_ref, lse_ref):
        i = pl.program_id(1)
        qv = q_ref[0]

        def step(j, carry, diag_offset=None):
            m, acc = carry
            rows = pl.ds(pl.multiple_of(j * tk, tk), tk)
            sc = _dot_nt(qv, k_ref[0, rows, :])
            if diag_offset is not None:
                qi = lax.broadcasted_iota(jnp.int32, sc.shape, 0)
                ki = lax.broadcasted_iota(jnp.int32, sc.shape, 1) + diag_offset
                sc = jnp.where(ki <= qi, sc, NEG)
            m_new = jnp.maximum(m, jnp.max(sc, axis=-1, keepdims=True))
            p = jnp.exp2(sc - m_new)
            acc = jnp.exp2(m - m_new) * acc + _dot(p.astype(BF16), v_ref[0, rows, :])
            return m_new, acc

        init = (jnp.full((tq, 1), NEG, F32), jnp.zeros((tq, 2 * V_HEAD), F32))
        carry = lax.fori_loop(0, i, lambda p, cr: step(2 * p + 1, step(2 * p, cr)), init)
        m, acc = step(2 * i + 1, step(2 * i, carry, 0), tk)
        l = acc[:, V_HEAD:]
        o_ref[...] = (acc[:, :V_HEAD] / l).astype(BF16)
        lse = m + jnp.log(l[:, 0:1]) * LOG2E
        lse_ref[0] = jnp.broadcast_to(lse, (tq, LANES)).T[0:1, :]

    return pl.pallas_call(
        body, name="flash_fwd", grid=(nh, nq),
        in_specs=[pl.BlockSpec((1, tq, QK_PAD), lambda h, i: (h, i, 0)),
                  pl.BlockSpec((1, s, QK_PAD), lambda h, i: (h, 0, 0)),
                  pl.BlockSpec((1, s, 2 * V_HEAD), lambda h, i: (h, 0, 0))],
        out_specs=[pl.BlockSpec((tq, V_HEAD), lambda h, i: (i, h)), pl.BlockSpec((1, 1, tq), lambda h, i: (h, 0, i))],
        out_shape=[jax.ShapeDtypeStruct((s, nh * V_HEAD), BF16), jax.ShapeDtypeStruct((nh, 1, s), F32)],
        compiler_params=_params(("parallel", "arbitrary")),
    )(q, k, v)


def _flash_bwd(q, k, v, do, lse, delta):
    nh, s, _ = q.shape
    tq = min(TQ, s)
    nq = s // tq

    def body(q_ref, k_ref, v_ref, do_ref, lse_ref, dl_ref, dq_ref, dk_ref, dv_ref, dq_acc):
        j = pl.program_id(1)

        @pl.when(j == 0)
        def _():
            dq_acc[...] = jnp.zeros_like(dq_acc)

        kj, vj = k_ref[0], v_ref[0]

        def step(i, carry, masked):
            dk, dv = carry
            rows = pl.ds(pl.multiple_of(i * tq, tq), tq)
            qi, doi = q_ref[0, rows, :], do_ref[rows, :]
            st = _dot_nt(kj, qi)
            pt = jnp.exp2(st - lse_ref[0, pl.ds(i, 1), :])
            if masked:
                ki = lax.broadcasted_iota(jnp.int32, st.shape, 0)
                qx = lax.broadcasted_iota(jnp.int32, st.shape, 1)
                pt = jnp.where(ki <= qx, pt, 0.0)
            dv = dv + _dot(pt.astype(BF16), doi)
            dpt = _dot_nt(vj, doi)
            dst = (pt * (dpt - dl_ref[0, pl.ds(i, 1), :])).astype(BF16)
            dk = dk + _dot(dst, qi)
            dq_acc[rows, :] += _dot_tn(dst, kj)
            return dk, dv

        carry = step(j, (jnp.zeros((tq, QK_PAD), F32), jnp.zeros((tq, V_HEAD), F32)), True)
        dk, dv = lax.fori_loop(j + 1, nq, lambda i, cr: step(i, cr, False), carry)
        dk_ref[0] = (dk * LN2).astype(BF16)
        dv_ref[0] = dv.astype(BF16)

        @pl.when(j == nq - 1)
        def _():
            dq_ref[0] = dq_acc[...].astype(BF16)

    return pl.pallas_call(
        body, name="flash_bwd", grid=(nh, nq),
        in_specs=[pl.BlockSpec((1, s, QK_PAD), lambda h, j: (h, 0, 0)),
                  pl.BlockSpec((1, tq, QK_PAD), lambda h, j: (h, j, 0)),
                  pl.BlockSpec((1, tq, V_HEAD), lambda h, j: (h, j, 0)),
                  pl.BlockSpec((s, V_HEAD), lambda h, j: (0, h)),
                  pl.BlockSpec((1, nq, tq), lambda h, j: (h, 0, 0)),
                  pl.BlockSpec((1, nq, tq), lambda h, j: (h, 0, 0))],
        out_specs=[pl.BlockSpec((1, s, QK_PAD), lambda h, j: (h, 0, 0)),
                   pl.BlockSpec((1, tq, QK_PAD), lambda h, j: (h, j, 0)),
                   pl.BlockSpec((1, tq, V_HEAD), lambda h, j: (h, j, 0))],
        out_shape=[jax.ShapeDtypeStruct((nh, s, QK_PAD), BF16), jax.ShapeDtypeStruct((nh, s, QK_PAD), BF16),
                   jax.ShapeDtypeStruct((nh, s, V_HEAD), BF16)],
        scratch_shapes=[pltpu.VMEM((s, QK_PAD), F32)],
        compiler_params=_params(("parallel", "arbitrary")),
    )(q, k, v, do, lse, delta)


def _tail(x, target, yc, o, u, mod, w_out):
    s, d = x.shape
    tm = min(TM_ELEM, s)

    def body(x_ref, t_ref, yc_ref, o_ref, za_ref, mod_ref, w_ref,
             gx_ref, dy_ref, ycat_ref, dyc_ref, do_ref, du_ref, delta_ref, dgate_ref, loss_ref):
        i = pl.program_id(0)
        za, ov = za_ref[...].astype(F32), o_ref[...].astype(F32)
        sg = _sigmoid(za)
        sl = za * sg
        ya = ov * sl
        ycat = jnp.concatenate([yc_ref[...], ya.astype(BF16)], axis=1)
        ycat_ref[...] = jnp.concatenate([yc_ref[...].astype(F32).T, ya.T], axis=0).astype(BF16)
        y = _dot(ycat, w_ref[...])
        gate = mod_ref[:, 2 * d:3 * d]
        e = x_ref[...] + gate * y - t_ref[...]
        dout = e * (1.0 / d)
        gx_ref[...] = dout
        dy = (dout * gate).astype(BF16)
        dy_ref[...] = dy
        dycat = _dot_nt(dy, w_ref[...])
        dyc_ref[...] = dycat[:, :D_CONV].astype(BF16)
        dya = dycat[:, D_CONV:]
        dov = dya * sl
        do_ref[...] = dov.astype(BF16)
        du_ref[...] = (dya * ov * (sg * (1.0 + za * (1.0 - sg)))).astype(BF16)
        prod_t = (dov * ov).T
        for h in range(N_HEADS):
            delta_ref[h] = jnp.sum(prod_t[V_HEAD * h:V_HEAD * (h + 1), :], axis=0, keepdims=True)
        dgate = jnp.sum(dout * y, axis=0, keepdims=True)
        part = jnp.sum(jnp.sum(e * e, axis=0, keepdims=True), axis=1, keepdims=True) * (0.5 / d)
        part = jnp.broadcast_to(part, (1, LANES))

        @pl.when(i == 0)
        def _():
            dgate_ref[...] = dgate
            loss_ref[...] = part

        @pl.when(i > 0)
        def _():
            dgate_ref[...] += dgate
            loss_ref[...] += part

    tok = lambda w: pl.BlockSpec((tm, w), lambda i: (i, 0))
    return pl.pallas_call(
        body, name="tail", grid=(s // tm,),
        in_specs=[tok(d), tok(d), tok(D_CONV), tok(D_ATTN), pl.BlockSpec((tm, D_ATTN), lambda i: (i, U_ZA // D_ATTN)),
                  _full((1, 3 * d)), _full((d, d))],
        out_specs=[tok(d), tok(d), pl.BlockSpec((d, tm), lambda i: (0, i)), tok(D_CONV), tok(D_ATTN),
                   pl.BlockSpec((tm, D_ATTN), lambda i: (i, U_ZA // D_ATTN)),
                   pl.BlockSpec((N_HEADS, 1, tm), lambda i: (0, 0, i)), _full((1, d)), _full((1, LANES))],
        out_shape=[jax.ShapeDtypeStruct((s, d), F32), jax.ShapeDtypeStruct((s, d), BF16),
                   jax.ShapeDtypeStruct((d, s), BF16), jax.ShapeDtypeStruct((s, D_CONV), BF16),
                   jax.ShapeDtypeStruct((s, D_ATTN), BF16), jax.ShapeDtypeStruct((s, U_COLS), BF16),
                   jax.ShapeDtypeStruct((N_HEADS, 1, s), F32), jax.ShapeDtypeStruct((1, d), F32),
                   jax.ShapeDtypeStruct((1, LANES), F32)],
        compiler_params=_params(("arbitrary",)),
    )(x, target, yc, o, u, mod, w_out)


def _norm_bwd(x, dh, gx1, norm_g, mod):
    s, d = x.shape
    tm = min(TM_ELEM, s)

    def body(x_ref, dh_ref, gx_ref, g_ref, mod_ref, o_ref, dshift_ref, dscale_ref, dg_ref):
        i = pl.program_id(0)
        xv, dhv, gv = x_ref[...], dh_ref[...].astype(F32), g_ref[...]
        r = lax.rsqrt(jnp.mean(xv * xv, axis=-1, keepdims=True) + EPS)
        xn = xv * r
        dhn = dhv * (1.0 + mod_ref[:, d:2 * d])
        dxn = dhn * gv
        o_ref[...] = gx_ref[...] + r * (dxn - xn * jnp.mean(dxn * xn, axis=-1, keepdims=True))
        dshift = jnp.sum(dhv, axis=0, keepdims=True)
        dscale = jnp.sum(dhv * xn * gv, axis=0, keepdims=True)
        dg = jnp.sum(dhn * xn, axis=0, keepdims=True)

        @pl.when(i == 0)
        def _():
            dshift_ref[...] = dshift
            dscale_ref[...] = dscale
            dg_ref[...] = dg

        @pl.when(i > 0)
        def _():
            dshift_ref[...] += dshift
            dscale_ref[...] += dscale
            dg_ref[...] += dg

    tok = pl.BlockSpec((tm, d), lambda i: (i, 0))
    row = jax.ShapeDtypeStruct((1, d), F32)
    return pl.pallas_call(
        body, name="norm_bwd", grid=(s // tm,),
        in_specs=[tok, tok, tok, _full((1, d)), _full((1, 3 * d))],
        out_specs=[tok, _full((1, d)), _full((1, d)), _full((1, d))],
        out_shape=[jax.ShapeDtypeStruct((s, d), F32), row, row, row],
        compiler_params=_params(("arbitrary",)),
    )(x, dh, gx1, norm_g, mod)


def _adamw(w, g, m, v, name):
    rows, cols = w.shape
    tr = 256 if rows % 256 == 0 else rows

    def body(w_ref, g_ref, m_ref, v_ref, d_ref, nm_ref, nv_ref):
        gv = g_ref[...]
        nm = ADAM_B1 * m_ref[...] + (1.0 - ADAM_B1) * gv
        nv = ADAM_B2 * v_ref[...] + (1.0 - ADAM_B2) * (gv * gv)
        m_hat = nm / (1.0 - ADAM_B1 ** ADAM_STEP)
        v_hat = nv / (1.0 - ADAM_B2 ** ADAM_STEP)
        d_ref[...] = -ADAM_LR * (m_hat / (jnp.sqrt(v_hat) + ADAM_EPS) + ADAM_WD * w_ref[...])
        nm_ref[...] = nm
        nv_ref[...] = nv

    spec = pl.BlockSpec((tr, cols), lambda i: (i, 0))
    shape = jax.ShapeDtypeStruct((rows, cols), F32)
    return pl.pallas_call(
        body, name=name, grid=(rows // tr,), in_specs=[spec] * 4, out_specs=[spec] * 3, out_shape=[shape] * 3,
        compiler_params=_params(("parallel",)),
    )(w, g, m, v)


def _pad_cols(a, n):
    return jnp.pad(a, ((0, 0), (0, n - a.shape[1])))


def kernel(x, c, positions, ada_w, ada_b, norm_g, w_in, conv_w, q_a_g, w_q_b, kv_a_g, w_kv_b, q_g, k_g, w_out, loss_target, m_ada_w, m_ada_b, m_norm_g, m_w_in, m_conv_w, m_q_a_g, m_w_q_b, m_kv_a_g, m_w_kv_b, m_q_g, m_k_g, m_w_out, v_ada_w, v_ada_b, v_norm_g, v_w_in, v_conv_w, v_q_a_g, v_w_q_b, v_kv_a_g, v_w_kv_b, v_q_g, v_k_g, v_w_out):
    me = _my_index()
    s = x.shape[1]
    nq = s // min(TQ, s)
    x2, tgt = x[0], loss_target[0]
    w_in_l, w_q_l, w_kv_l, w_out_l, conv_l, ada_w_l = w_in[0], w_q_b[0], w_kv_b[0], w_out[0], conv_w[0], ada_w[0]
    ada_cols = ada_w_l.shape[1]

    pack = jnp.concatenate([w.astype(BF16).reshape(-1, LANES) for w in (w_in_l, w_q_l, w_kv_l, w_out_l)], axis=0)
    small = jnp.concatenate([c.reshape(-1, LANES), conv_l.reshape(-1, LANES), jnp.zeros((5, LANES), F32)], axis=0)
    pack_g, small_g = _all_gather([pack, small], "gather_weights")
    r0, r1, r2 = ROWS_W_IN, ROWS_W_IN + ROWS_W_Q, ROWS_W_IN + ROWS_W_Q + ROWS_W_KV
    w_in_g = pack_g[:, :r0].reshape(N_DEV, D_MODEL, IN_COLS // N_DEV).transpose(1, 0, 2).reshape(D_MODEL, IN_COLS)
    w_in_p = jnp.concatenate([w_in_g[:, :4 * D_CONV], w_in_g[:, IN_COLS - D_ATTN:], w_in_g[:, 4 * D_CONV:IN_COLS - D_ATTN],
                              jnp.zeros((D_MODEL, U_COLS - IN_COLS), BF16)], axis=1)
    wq_g = pack_g[:, r0:r1].reshape(N_DEV, Q_LORA, QK_HEAD).transpose(1, 0, 2)
    wq_p = jnp.pad(wq_g, ((0, 0), (0, 0), (0, QK_PAD - QK_HEAD))).reshape(Q_LORA, N_HEADS * QK_PAD)
    wkv_g = pack_g[:, r1:r2].reshape(N_DEV, KV_LORA, 2, QK_NOPE)
    wkv_p = wkv_g.transpose(1, 2, 0, 3).reshape(KV_LORA, 2 * D_ATTN)
    w_out_g = pack_g[:, r2:].reshape(D_MODEL, D_MODEL)
    c_all = small_g[:, :D_MODEL // LANES].reshape(N_DEV, D_MODEL)
    conv_g = small_g[:, D_MODEL // LANES:D_MODEL // LANES + 3].transpose(1, 0, 2).reshape(3, D_CONV)

    ada_b_l = lax.dynamic_slice(ada_b, (0, me * ada_cols), (1, ada_cols))
    mod_cols = _ada_mod(jnp.pad(c_all, ((0, 8), (0, 0))), ada_w_l, ada_b_l)[:N_DEV]
    (mod_g,) = _all_gather([mod_cols], "gather_mod")
    mod = lax.dynamic_index_in_dim(mod_g, me, axis=1, keepdims=False).reshape(1, 3 * D_MODEL)

    half = jnp.arange(0, QK_ROPE, 2, dtype=F32) / QK_ROPE
    inv_freq = ROPE_BASE ** (-half)
    zeros64 = jnp.zeros((LANES - QK_ROPE,), F32)
    invf = jnp.concatenate([inv_freq, inv_freq, zeros64]).reshape(1, LANES)
    sign = jnp.concatenate([-jnp.ones((32,), F32), jnp.ones((32,), F32), zeros64]).reshape(1, LANES)
    cos, sin = _rope_tables(positions.reshape(s, 1), invf, sign)
    qg_p, kg_p = _pad_cols(q_g, QK_PAD), _pad_cols(k_g, QK_PAD)

    h, h_t = _norm_mod(x2, norm_g, mod)
    u = _matmul(h, w_in_p, nt=False, out_dtype=BF16, tm=TM_MM, tn=1024, name="in_proj")
    yc = _conv_fwd(u, conv_g)
    q, k, v = _qkv_fwd(u, cos, sin, wq_p, wkv_p, q_a_g, kv_a_g, qg_p, kg_p)
    o, lse = _flash_fwd(q, k, v)
    gx1, dy, ycat_t, dyc, do, du, delta, dgate, loss_row = _tail(x2, tgt, yc, o, u, mod, w_out_g)

    dq, dk, dv = _flash_bwd(q, k, v, do, lse.reshape(N_HEADS, nq, s // nq), delta.reshape(N_HEADS, nq, s // nq))
    du, dconv = _conv_bwd(u, dyc, conv_g, du)
    du, dwq_p, dwkv_p, dqag, dkvag, dqg, dkg = _qkv_bwd(u, cos, sin, dq, dk, dv, wq_p, wkv_p, q_a_g, kv_a_g, qg_p, kg_p, du)
    dw_out = _matmul(ycat_t, dy, nt=False, out_dtype=F32, tm=TM_MM, tn=512, name="dw_out")
    dw_in_p = _matmul(h_t, du, nt=False, out_dtype=F32, tm=TM_MM, tn=512, name="dw_in")
    dh = _matmul(du, w_in_p, nt=True, out_dtype=BF16, tm=TM_MM, tn=512, name="dh")
    grad_x, dshift, dscale, dng = _norm_bwd(x2, dh, gx1, norm_g, mod)

    row = jnp.concatenate([dshift, dscale, dgate, dng, dqag, dkvag, dqg, dkg, dconv[:3].reshape(1, 3 * D_CONV), loss_row], axis=1)
    (rows_g,) = _all_gather([row], "gather_small")
    tot = _sum_leading(rows_g, F32, "sum_small")
    dmod_all = rows_g[:, 0, SM_MOD:SM_NG]
    g_ada_b = tot[:, SM_MOD:SM_NG]
    g_norm_g = tot[:, SM_NG:SM_QAG]
    g_q_a_g = tot[:, SM_QAG:SM_KVAG]
    g_kv_a_g = tot[:, SM_KVAG:SM_QG]
    g_q_g = tot[:, SM_QG:SM_QG + QK_HEAD]
    g_k_g = tot[:, SM_KG:SM_KG + QK_HEAD]
    conv_cols = conv_l.shape[1]
    g_conv = lax.dynamic_slice(tot[:, SM_CONV:SM_LOSS].reshape(3, D_CONV), (0, me * conv_cols), (3, conv_cols))
    loss = tot[0, SM_LOSS]
    dmod_my = lax.dynamic_slice(dmod_all, (0, me * ada_cols), (N_DEV, ada_cols))
    g_ada_w = _ada_w_grad(c_all.T, dmod_my)

    dw_in = jnp.concatenate([dw_in_p[:, :4 * D_CONV], dw_in_p[:, U_CQ:U_CQ + IN_COLS - 4 * D_CONV - D_ATTN],
                             dw_in_p[:, U_ZA:U_ZA + D_ATTN]], axis=1)
    p_in = dw_in.reshape(D_MODEL, N_DEV, IN_COLS // N_DEV).transpose(1, 0, 2).astype(BF16).reshape(N_DEV, ROWS_W_IN, LANES)
    dwq = dwq_p.reshape(Q_LORA, N_HEADS, QK_PAD)[:, :, :QK_HEAD]
    p_q = dwq.transpose(1, 0, 2).astype(BF16).reshape(N_DEV, ROWS_W_Q, LANES)
    dwkv = dwkv_p.reshape(KV_LORA, 2, N_HEADS, QK_NOPE).transpose(2, 0, 1, 3)
    p_kv = dwkv.astype(BF16).reshape(N_DEV, ROWS_W_KV, LANES)
    p_out = dw_out.astype(BF16).reshape(N_DEV, ROWS_W_OUT, LANES)
    gpack = jnp.concatenate([p_in, p_q, p_kv, p_out], axis=1)
    recv = _rs_sibling(gpack)
    pair = _add_pairs(gpack, recv, lax.axis_index("c").astype(jnp.int32).reshape(1), "rs_add")
    quad = _rs_chips(pair)
    gsum = _sum_leading(quad, F32, "rs_sum")
    g_w_in = gsum[:r0].reshape(w_in_l.shape)
    g_w_q = gsum[r0:r1].reshape(w_q_l.shape)
    g_w_kv = gsum[r1:r2].reshape(w_kv_l.shape)
    g_w_out = gsum[r2:].reshape(w_out_l.shape)

    grads = dict(ada_w=g_ada_w, ada_b=g_ada_b, norm_g=g_norm_g, w_in=g_w_in, conv_w=g_conv, q_a_g=g_q_a_g, w_q_b=g_w_q,
                 kv_a_g=g_kv_a_g, w_kv_b=g_w_kv, q_g=g_q_g, k_g=g_k_g, w_out=g_w_out)
    weights = dict(ada_w=(ada_w, m_ada_w, v_ada_w), ada_b=(ada_b, m_ada_b, v_ada_b), norm_g=(norm_g, m_norm_g, v_norm_g),
                   w_in=(w_in, m_w_in, v_w_in), conv_w=(conv_w, m_conv_w, v_conv_w), q_a_g=(q_a_g, m_q_a_g, v_q_a_g),
                   w_q_b=(w_q_b, m_w_q_b, v_w_q_b), kv_a_g=(kv_a_g, m_kv_a_g, v_kv_a_g), w_kv_b=(w_kv_b, m_w_kv_b, v_w_kv_b),
                   q_g=(q_g, m_q_g, v_q_g), k_g=(k_g, m_k_g, v_k_g), w_out=(w_out, m_w_out, v_w_out))
    names = list(grads)
    out_g, out_d, out_m, out_v = [], [], [], []
    for n in names:
        w, m, v_ = weights[n]
        shape2 = w.shape[-2:] if w.ndim == 3 else (1, w.shape[-1])
        g2 = grads[n].reshape(shape2)
        d2, m2, v2 = _adamw(w.reshape(shape2), g2, m.reshape(shape2), v_.reshape(shape2), "adamw_" + n)
        out_g.append(g2.reshape(w.shape))
        out_d.append(d2.reshape(w.shape))
        out_m.append(m2.reshape(w.shape))
        out_v.append(v2.reshape(w.shape))
    return (loss, grad_x.reshape(x.shape), *out_g, *out_d, *out_m, *out_v)
```

```python
import functools
import math

import jax
import jax.numpy as jnp
from jax import lax
from jax.experimental import pallas as pl
from jax.experimental.pallas import tpu as pltpu

F32 = jnp.float32
BF16 = jnp.bfloat16
MESH = pl.DeviceIdType.MESH

D_MODEL = 2048
D_CONV = 1024
N_HEADS = 8
QK_NOPE = 128
QK_ROPE = 64
QK_HEAD = QK_NOPE + QK_ROPE
V_HEAD = 128
D_ATTN = N_HEADS * V_HEAD
Q_LORA = 512
KV_LORA = 256
ROPE_BASE = 10000.0
IN_COLS = 4 * D_CONV + Q_LORA + KV_LORA + QK_ROPE + D_ATTN
EPS = 1e-6
ADAM_LR, ADAM_B1, ADAM_B2, ADAM_EPS, ADAM_WD, ADAM_STEP = 0.001, 0.9, 0.999, 1e-08, 0.01, 10

N_DEV = 8
LANES = 128
QK_PAD = 256
U_COLS = 6144
U_CQ, U_CKV, U_KR, U_ZA = 4096, 4608, 4864, 4928
U_TAIL = 2048
ZA_LO = U_ZA - (U_COLS - U_TAIL) - QK_ROPE
ZA_WIN = D_ATTN + LANES
CW = IN_COLS // 8
EXP_W = 896
W_LO = [(CW * d // 128) * 128 for d in range(8)]
W_OFF = [CW * d - lo for d, lo in enumerate(W_LO)]
SCALE = 1.0 / math.sqrt(QK_HEAD)
LOG2E = 1.4426950408889634
LN2 = 0.6931471805599453
NEG = -1e30
VMEM_LIMIT = 56 * 1024 * 1024

TM_ELEM = 256
TM_MM = 512
TQ = 1024

SM_MOD, SM_NG, SM_QAG, SM_KVAG, SM_QG, SM_KG, SM_CONV, SM_LOSS = 0, 6144, 8192, 8704, 8960, 9216, 9472, 12544
SM_COLS = 12672


def _params(sem=None):
    kw = dict(vmem_limit_bytes=VMEM_LIMIT)
    if sem is not None:
        kw["dimension_semantics"] = sem
    return pltpu.CompilerParams(**kw)


def _sigmoid(z):
    return 1.0 / (1.0 + jnp.exp(-z))


def _rot64(x):
    lane = lax.broadcasted_iota(jnp.int32, x.shape, 1)
    return jnp.where(lane < 32, pltpu.roll(x, 96, 1), pltpu.roll(x, 32, 1))


def _rope(x, cos, sin):
    return x * cos + _rot64(x) * sin


def _rope_t(d, cos, sin):
    return d * cos - _rot64(d) * sin


def _dot(a, b):
    return jnp.dot(a, b, preferred_element_type=F32)


def _dot_nt(a, b):
    return lax.dot_general(a, b, (((1,), (1,)), ((), ())), preferred_element_type=F32)


def _dot_tn(a, b):
    return lax.dot_general(a, b, (((0,), (0,)), ((), ())), preferred_element_type=F32)


def _my_index():
    return 4 * lax.axis_index("x") + 2 * lax.axis_index("y") + lax.axis_index("c")


ANY = pl.BlockSpec(memory_space=pl.ANY)


class _Gather:
    def __init__(self, blocks):
        self.n = n = len(blocks)
        self.out_shape = [jax.ShapeDtypeStruct((N_DEV,) + b.shape, b.dtype) for b in blocks]
        self.scratch = [pltpu.SemaphoreType.DMA((7 * n,)), pltpu.SemaphoreType.DMA((7 * n,)),
                        pltpu.SemaphoreType.DMA((n,))]

    @staticmethod
    def _places():
        x, y, c = lax.axis_index("x"), lax.axis_index("y"), lax.axis_index("c")
        return (x, y, c), (x, y, 1 - c), [(1 - x, y), (x, 1 - y), (1 - x, 1 - y)]

    @staticmethod
    def _copy(outs, sems, a, k, block, to, src=None):
        dst = outs[a].at[4 * block[0] + 2 * block[1] + block[2]]
        return pltpu.make_async_remote_copy(
            src_ref=dst if src is None else src, dst_ref=dst, send_sem=sems[0].at[7 * a + k],
            recv_sem=sems[1].at[7 * a + k], device_id=to, device_id_type=MESH)

    def _first(self, ins, outs, sems):
        me, sibling, chips = self._places()
        first = []
        for a in range(self.n):
            first.append(self._copy(outs, sems, a, 0, me, sibling, src=ins[a]))
            first += [self._copy(outs, sems, a, 1 + j, me, (*chip, me[2]), src=ins[a]) for j, chip in enumerate(chips)]
        return first

    def _passed(self, outs, sems):
        me, sibling, chips = self._places()
        return [self._copy(outs, sems, a, 4 + j, (*chip, me[2]), sibling)
                for a in range(self.n) for j, chip in enumerate(chips)]

    def _mine(self, ins, outs, sems):
        me, _, _ = self._places()
        return [pltpu.make_async_copy(ins[a], outs[a].at[4 * me[0] + 2 * me[1] + me[2]], sems[2].at[a])
                for a in range(self.n)]

    def start(self, ins, outs, sems):
        for cp in self._mine(ins, outs, sems) + self._first(ins, outs, sems):
            cp.start()

    def forward(self, ins, outs, sems):
        del ins
        me, _, chips = self._places()
        passed = self._passed(outs, sems)
        for a in range(self.n):
            for j, chip in enumerate(chips):
                self._copy(outs, sems, a, 1 + j, (*chip, me[2]), me).wait_recv()
                passed[3 * a + j].start()

    def finish(self, ins, outs, sems):
        me, sibling, chips = self._places()
        for a in range(self.n):
            self._copy(outs, sems, a, 0, sibling, me).wait_recv()
            for j, chip in enumerate(chips):
                self._copy(outs, sems, a, 4 + j, (*chip, 1 - me[2]), me).wait_recv()
        for cp in self._first(ins, outs, sems) + self._passed(outs, sems):
            cp.wait_send()
        for cp in self._mine(ins, outs, sems):
            cp.wait()


class _ChipExchange:
    def __init__(self, arrays):
        self.n = n = len(arrays)
        self.out_shape = [jax.ShapeDtypeStruct(a.shape, a.dtype) for a in arrays]
        self.scratch = [pltpu.SemaphoreType.DMA((3 * n,)), pltpu.SemaphoreType.DMA((3 * n,))]

    def _copies(self, ins, outs, sems):
        x, y, c = lax.axis_index("x"), lax.axis_index("y"), lax.axis_index("c")
        return [pltpu.make_async_remote_copy(
            src_ref=ins[a].at[2 * px + py], dst_ref=outs[a].at[2 * x + y], send_sem=sems[0].at[3 * a + j],
            recv_sem=sems[1].at[3 * a + j], device_id=(px, py, c), device_id_type=MESH)
            for a in range(self.n) for j, (px, py) in enumerate([(1 - x, y), (x, 1 - y), (1 - x, 1 - y)])]

    def start(self, ins, outs, sems):
        for cp in self._copies(ins, outs, sems):
            cp.start()

    def forward(self, ins, outs, sems):
        pass

    def finish(self, ins, outs, sems):
        for cp in self._copies(ins, outs, sems):
            cp.wait()


def _all_gather(blocks, name):
    n = len(blocks)
    g = _Gather(blocks)

    def body(*refs):
        ins, outs, sems = refs[:n], refs[n:2 * n], refs[2 * n:]
        g.start(ins, outs, sems)
        g.forward(ins, outs, sems)
        g.finish(ins, outs, sems)

    return pl.pallas_call(body, name=name, out_shape=g.out_shape, in_specs=[ANY] * n, out_specs=[ANY] * n,
                          scratch_shapes=g.scratch)(*blocks)


def _rs_sibling(dw_in, dwq, dwkv, dwout):
    srcs = (dw_in, dwq, dwkv, dwout)
    shapes = [(dw_in.shape[0], EXP_W)] + [a.shape[1:] for a in srcs[1:]]

    def body(in_ref, q_ref, kv_ref, out_ref, rin, rq, rkv, rout, send_sems, recv_sems):
        x, y, c = lax.axis_index("x"), lax.axis_index("y"), lax.axis_index("c")

        def exchange(c_val):
            copies = []
            for k in range(4):
                e = 2 * k + (1 - c_val)
                pairs = [(in_ref.at[:, pl.ds(W_LO[e], EXP_W)], rin.at[k]), (q_ref.at[e], rq.at[k]),
                         (kv_ref.at[e], rkv.at[k]), (out_ref.at[e], rout.at[k])]
                for a, (src, dst) in enumerate(pairs):
                    copies.append(pltpu.make_async_remote_copy(
                        src_ref=src, dst_ref=dst, send_sem=send_sems.at[4 * a + k], recv_sem=recv_sems.at[4 * a + k],
                        device_id=(x, y, 1 - c), device_id_type=MESH))
            for cp in copies:
                cp.start()
            for cp in copies:
                cp.wait()

        for c_val in (0, 1):
            pl.when(c == c_val)(functools.partial(exchange, c_val))

    return pl.pallas_call(
        body, name="rs_sibling", out_shape=[jax.ShapeDtypeStruct((4,) + tuple(sh), a.dtype) for sh, a in zip(shapes, srcs)],
        in_specs=[ANY] * 4, out_specs=[ANY] * 4,
        scratch_shapes=[pltpu.SemaphoreType.DMA((16,)), pltpu.SemaphoreType.DMA((16,))],
    )(*srcs)


def _add_window(dw_in, recv, lo_tiles):
    k, rows, _ = recv.shape

    def body(t_ref, w_ref, r_ref, o_ref):
        del t_ref
        o_ref[0] = (w_ref[...].astype(F32) + r_ref[0].astype(F32)).astype(o_ref.dtype)

    spec = pl.BlockSpec((1, rows, LANES), lambda i, j, t: (i, 0, j))
    grid_spec = pltpu.PrefetchScalarGridSpec(
        num_scalar_prefetch=1, grid=(k, EXP_W // LANES),
        in_specs=[pl.BlockSpec((rows, LANES), lambda i, j, t: (0, t[i] + j)), spec], out_specs=spec)
    return pl.pallas_call(
        body, name="rs_add_in", grid_spec=grid_spec, out_shape=jax.ShapeDtypeStruct(recv.shape, recv.dtype),
        compiler_params=_params(("parallel", "parallel")),
    )(lo_tiles, dw_in, recv)


def _final_sum(p, r, sel, name, unshift_to=None):
    _, rows, cols = p.shape
    tr = 512 if rows % 512 == 0 else rows
    out_cols = cols if unshift_to is None else unshift_to

    def body(sel_ref, p_ref, r0, r1, r2, r3, o_ref):
        own = p_ref[0].astype(F32)
        acc = None
        for k, r_ref in enumerate((r0, r1, r2, r3)):
            term = jnp.where(sel_ref[0] == k, own, r_ref[0].astype(F32))
            acc = term if acc is None else acc + term
        if unshift_to is not None:
            acc = pltpu.roll(acc, sel_ref[5], 1)[:, :unshift_to]
        o_ref[...] = acc

    def slot(k):
        return pl.BlockSpec((1, tr, cols), lambda i, t: (t[k], i, 0))

    grid_spec = pltpu.PrefetchScalarGridSpec(
        num_scalar_prefetch=1, grid=(rows // tr,), in_specs=[slot(0), slot(1), slot(2), slot(3), slot(4)],
        out_specs=pl.BlockSpec((tr, out_cols), lambda i, t: (i, 0)))
    return pl.pallas_call(
        body, name=name, grid_spec=grid_spec, out_shape=jax.ShapeDtypeStruct((rows, out_cols), F32),
        compiler_params=_params(("parallel",)),
    )(sel, p, r, r, r, r)


def _expand_w_in(w, shift):
    rows, cw = w.shape
    tr = 256

    def body(s_ref, w_ref, o_ref, buf):
        buf[...] = jnp.zeros_like(buf)
        buf[:, 0:cw] = w_ref[...]
        o_ref[...] = pltpu.roll(buf[...], s_ref[0], 1).astype(BF16)

    grid_spec = pltpu.PrefetchScalarGridSpec(
        num_scalar_prefetch=1, grid=(rows // tr,), in_specs=[pl.BlockSpec((tr, cw), lambda i, t: (i, 0))],
        out_specs=pl.BlockSpec((tr, EXP_W), lambda i, t: (i, 0)), scratch_shapes=[pltpu.VMEM((tr, EXP_W), F32)])
    return pl.pallas_call(
        body, name="expand_w_in", grid_spec=grid_spec, out_shape=jax.ShapeDtypeStruct((rows, EXP_W), BF16),
        compiler_params=_params(("arbitrary",)),
    )(shift, w)


def _pad_wq(w):
    rows, cw = w.shape

    def body(w_ref, o_ref, buf):
        buf[...] = jnp.zeros_like(buf)
        buf[:, 0:cw] = w_ref[...]
        o_ref[...] = buf[...].astype(BF16)

    return pl.pallas_call(
        body, name="pad_wq", out_shape=jax.ShapeDtypeStruct((rows, QK_PAD), BF16),
        scratch_shapes=[pltpu.VMEM((rows, QK_PAD), F32)], compiler_params=_params(),
    )(w)


def _merge_w_in(e):
    _, rows, _ = e.shape
    tr = 256

    def body(e_ref, o_ref):
        for t in range(U_COLS // LANES):
            lo, hi = t * LANES, (t + 1) * LANES
            parts = [e_ref[d, :, lo - W_LO[d]:hi - W_LO[d]] for d in range(N_DEV)
                     if CW * d < hi and CW * (d + 1) > lo]
            if not parts:
                tile = jnp.zeros((tr, LANES), BF16)
            elif len(parts) == 1:
                tile = parts[0]
            else:
                tile = (parts[0].astype(F32) + parts[1].astype(F32)).astype(BF16)
            o_ref[:, lo:hi] = tile

    return pl.pallas_call(
        body, name="merge_w_in", grid=(rows // tr,),
        in_specs=[pl.BlockSpec((N_DEV, tr, EXP_W), lambda i: (0, i, 0))],
        out_specs=pl.BlockSpec((tr, U_COLS), lambda i: (i, 0)), out_shape=jax.ShapeDtypeStruct((rows, U_COLS), BF16),
        compiler_params=_params(("parallel",)),
    )(e)


def _sum_leading(a, out_dtype, name):
    k, rows, cols = a.shape
    tr = min(rows, 1728 if rows % 1728 == 0 else rows)

    def body(a_ref, o_ref):
        acc = a_ref[0].astype(F32)
        for i in range(1, k):
            acc = acc + a_ref[i].astype(F32)
        o_ref[...] = acc.astype(out_dtype)

    return pl.pallas_call(
        body, name=name, grid=(rows // tr,),
        in_specs=[pl.BlockSpec((k, tr, cols), lambda i: (0, i, 0))],
        out_specs=pl.BlockSpec((tr, cols), lambda i: (i, 0)),
        out_shape=jax.ShapeDtypeStruct((rows, cols), out_dtype), compiler_params=_params(("parallel",)),
    )(a)


def _add_pairs(g, recv, core, name):
    k, rows, cols = recv.shape
    tr = 1728 if rows % 1728 == 0 else rows

    def body(c_ref, g_ref, r_ref, o_ref):
        del c_ref
        o_ref[...] = (g_ref[...].astype(F32) + r_ref[...].astype(F32)).astype(o_ref.dtype)

    spec = pl.BlockSpec((1, tr, cols), lambda i, j, c: (i, j, 0))
    grid_spec = pltpu.PrefetchScalarGridSpec(
        num_scalar_prefetch=1, grid=(k, rows // tr),
        in_specs=[pl.BlockSpec((1, tr, cols), lambda i, j, c: (2 * i + c[0], j, 0)), spec], out_specs=spec)
    return pl.pallas_call(
        body, name=name, grid_spec=grid_spec, out_shape=jax.ShapeDtypeStruct(recv.shape, recv.dtype),
        compiler_params=_params(("parallel", "parallel")),
    )(core, g, recv)


def _ada_mod(c16, ada_w_l, ada_b_l):
    def body(c_ref, w_ref, b_ref, o_ref):
        cv = c_ref[...]
        sc = (cv * _sigmoid(cv)).astype(BF16)
        o_ref[...] = _dot(sc, w_ref[...].astype(BF16)) + b_ref[...]

    return pl.pallas_call(
        body, name="ada_mod", out_shape=jax.ShapeDtypeStruct((c16.shape[0], ada_w_l.shape[1]), F32),
        compiler_params=_params(),
    )(c16, ada_w_l, ada_b_l)


def _ada_w_grad(c_t, dmod_my):
    def body(c_ref, d_ref, o_ref):
        cv = c_ref[...]
        sc = cv * _sigmoid(cv)
        acc = sc[:, 0:1] * d_ref[0:1, :]
        for b in range(1, N_DEV):
            acc = acc + sc[:, b:b + 1] * d_ref[b:b + 1, :]
        o_ref[...] = acc

    return pl.pallas_call(
        body, name="ada_w_grad", out_shape=jax.ShapeDtypeStruct((c_t.shape[0], dmod_my.shape[1]), F32),
        compiler_params=_params(),
    )(c_t, dmod_my)


def _rope_tables(pos_col, invf, sign):
    s = pos_col.shape[0]
    tm = min(TM_MM, s)

    def body(p_ref, f_ref, s_ref, cos_ref, sin_ref):
        ang = p_ref[...].astype(F32) * f_ref[...]
        sg = s_ref[...]
        cos_ref[...] = jnp.cos(ang) * jnp.abs(sg)
        sin_ref[...] = jnp.sin(ang) * sg

    row = pl.BlockSpec((1, LANES), lambda i: (0, 0))
    tab = pl.BlockSpec((tm, LANES), lambda i: (i, 0))
    shape = jax.ShapeDtypeStruct((s, LANES), F32)
    return pl.pallas_call(
        body, name="rope_tables", grid=(s // tm,), in_specs=[pl.BlockSpec((tm, 1), lambda i: (i, 0)), row, row],
        out_specs=[tab, tab], out_shape=[shape, shape], compiler_params=_params(("parallel",)),
    )(pos_col, invf, sign)


def _norm_mod(x, norm_g, mod):
    s, d = x.shape
    tm = min(TM_MM, s)

    def body(x_ref, g_ref, mod_ref, h_ref, ht_ref):
        xv = x_ref[...]
        r = lax.rsqrt(jnp.mean(xv * xv, axis=-1, keepdims=True) + EPS)
        hn = xv * r * g_ref[...]
        hv = hn * (1.0 + mod_ref[:, d:2 * d]) + mod_ref[:, 0:d]
        h_ref[...] = hv.astype(BF16)
        ht_ref[...] = hv.T.astype(BF16)

    return pl.pallas_call(
        body, name="norm_mod", grid=(s // tm,),
        in_specs=[pl.BlockSpec((tm, d), lambda i: (i, 0)), pl.BlockSpec((1, d), lambda i: (0, 0)),
                  pl.BlockSpec((1, 3 * d), lambda i: (0, 0))],
        out_specs=[pl.BlockSpec((tm, d), lambda i: (i, 0)), pl.BlockSpec((d, tm), lambda i: (0, i))],
        out_shape=[jax.ShapeDtypeStruct((s, d), BF16), jax.ShapeDtypeStruct((d, s), BF16)],
        compiler_params=_params(("parallel",)),
    )(x, norm_g, mod)


def _matmul(a, b, *, nt, out_dtype, tm, tn, name, rider=None, rider_inputs=()):
    m, kdim = a.shape
    n = b.shape[0] if nt else b.shape[1]
    tm, tn = min(tm, m), min(tn, n)
    n_in = len(rider_inputs)
    n_out = len(rider.out_shape) if rider else 0
    m_steps = m // tm
    steps = (n // tn) * m_steps

    def body(a_ref, b_ref, *rest):
        r_ins, o_ref, r_outs, sems = rest[:n_in], rest[n_in], rest[n_in + 1:n_in + 1 + n_out], rest[n_in + 1 + n_out:]
        step = pl.program_id(0) * m_steps + pl.program_id(1)
        if rider:
            pl.when(step == 0)(functools.partial(rider.start, r_ins, r_outs, sems))
            pl.when(step == steps // 2)(functools.partial(rider.forward, r_ins, r_outs, sems))
        o = _dot_nt(a_ref[...], b_ref[...]) if nt else _dot(a_ref[...], b_ref[...])
        o_ref[...] = o.astype(out_dtype)
        if rider:
            pl.when(step == steps - 1)(functools.partial(rider.finish, r_ins, r_outs, sems))

    b_spec = pl.BlockSpec((tn, kdim), lambda j, i: (j, 0)) if nt else pl.BlockSpec((kdim, tn), lambda j, i: (0, j))
    out = pl.pallas_call(
        body, name=name, grid=(n // tn, m_steps),
        in_specs=[pl.BlockSpec((tm, kdim), lambda j, i: (i, 0)), b_spec] + [ANY] * n_in,
        out_specs=[pl.BlockSpec((tm, tn), lambda j, i: (i, j))] + [ANY] * n_out,
        out_shape=[jax.ShapeDtypeStruct((m, n), out_dtype)] + (rider.out_shape if rider else []),
        scratch_shapes=rider.scratch if rider else [],
        compiler_params=_params(("arbitrary", "arbitrary") if rider else ("parallel", "parallel")),
    )(a, b, *rider_inputs)
    return out if rider else out[0]


HALO = 16


def _conv_fwd(u, conv_w):
    s = u.shape[0]
    tm = min(TM_ELEM, s)
    cb = D_CONV

    def body(xc_ref, bc_ref, cc_ref, zc_ref, xp_ref, cp_ref, w_ref, y_ref):
        i = pl.program_id(0)
        uc = cc_ref[...].astype(F32) * xc_ref[...].astype(F32)
        up = cp_ref[...].astype(F32) * xp_ref[...].astype(F32)
        up = jnp.where(i == 0, 0.0, up)
        full = jnp.concatenate([up, uc], axis=0)
        u1 = pltpu.roll(full, 1, 0)[HALO:]
        u2 = pltpu.roll(full, 2, 0)[HALO:]
        w = w_ref[...]
        conv = w[0:1] * u2 + w[1:2] * u1 + w[2:3] * uc
        z = zc_ref[...].astype(F32)
        y_ref[...] = (bc_ref[...].astype(F32) * conv * (z * _sigmoid(z))).astype(BF16)

    def col(j):
        return pl.BlockSpec((tm, cb), lambda i: (i, j))

    def prev(j):
        return pl.BlockSpec((HALO, cb), lambda i: (jnp.maximum(i * (tm // HALO) - 1, 0), j))

    return pl.pallas_call(
        body, name="conv_fwd", grid=(s // tm,),
        in_specs=[col(0), col(1), col(2), col(3), prev(0), prev(2), pl.BlockSpec((3, cb), lambda i: (0, 0))],
        out_specs=pl.BlockSpec((tm, cb), lambda i: (i, 0)), out_shape=jax.ShapeDtypeStruct((s, cb), BF16),
        compiler_params=_params(("parallel",)),
    )(u, u, u, u, u, u, conv_w)


def _conv_bwd(u, dyc, conv_w):
    s = u.shape[0]
    tm = min(TM_ELEM, s)
    cb = D_CONV
    nt = s // tm

    def body(xc_ref, bc_ref, cc_ref, zc_ref, xp_ref, cp_ref, bn_ref, zn_ref, dy_ref, dyn_ref, w_ref, du_ref, dw_ref):
        i = pl.program_id(0)
        xc, cc = xc_ref[...].astype(F32), cc_ref[...].astype(F32)
        bc, z = bc_ref[...].astype(F32), zc_ref[...].astype(F32)
        uc = cc * xc
        up = jnp.where(i == 0, 0.0, cp_ref[...].astype(F32) * xp_ref[...].astype(F32))
        full = jnp.concatenate([up, uc], axis=0)
        u1 = pltpu.roll(full, 1, 0)[HALO:]
        u2 = pltpu.roll(full, 2, 0)[HALO:]
        w = w_ref[...]
        conv = w[0:1] * u2 + w[1:2] * u1 + w[2:3] * uc
        sg = _sigmoid(z)
        sz = z * sg
        dy = dy_ref[...].astype(F32)
        dconv = dy * bc * sz
        zn = zn_ref[...].astype(F32)
        dnext = dyn_ref[...].astype(F32) * bn_ref[...].astype(F32) * (zn * _sigmoid(zn))
        dnext = jnp.where(i == nt - 1, 0.0, dnext)
        fullb = jnp.concatenate([dconv, dnext], axis=0)
        nb = tm + HALO
        d1 = pltpu.roll(fullb, nb - 1, 0)[:tm]
        d2 = pltpu.roll(fullb, nb - 2, 0)[:tm]
        duc = w[2:3] * dconv + w[1:2] * d1 + w[0:1] * d2
        dzc = dy * bc * conv * (sg * (1.0 + z * (1.0 - sg)))
        du_ref[...] = jnp.concatenate([duc * cc, dy * conv * sz, duc * xc, dzc], axis=1).astype(BF16)
        dw = jnp.concatenate([jnp.sum(dconv * u2, axis=0, keepdims=True), jnp.sum(dconv * u1, axis=0, keepdims=True),
                              jnp.sum(dconv * uc, axis=0, keepdims=True), jnp.zeros((5, cb), F32)], axis=0)

        @pl.when(i == 0)
        def _():
            dw_ref[...] = dw

        @pl.when(i > 0)
        def _():
            dw_ref[...] += dw

    def col(j):
        return pl.BlockSpec((tm, cb), lambda i: (i, j))

    def prev(j):
        return pl.BlockSpec((HALO, cb), lambda i: (jnp.maximum(i * (tm // HALO) - 1, 0), j))

    def nxt(j):
        return pl.BlockSpec((HALO, cb), lambda i: (jnp.minimum((i + 1) * (tm // HALO), s // HALO - 1), j))

    return pl.pallas_call(
        body, name="conv_bwd", grid=(nt,),
        in_specs=[col(0), col(1), col(2), col(3), prev(0), prev(2), nxt(1), nxt(3), col(0), nxt(0),
                  pl.BlockSpec((3, cb), lambda i: (0, 0))],
        out_specs=[pl.BlockSpec((tm, 4 * cb), lambda i: (i, 0)), pl.BlockSpec((8, cb), lambda i: (0, 0))],
        out_shape=[jax.ShapeDtypeStruct((s, U_COLS), BF16), jax.ShapeDtypeStruct((8, cb), F32)],
        compiler_params=_params(("arbitrary",)),
    )(u, u, u, u, u, u, u, u, dyc, dyc, conv_w)


def _qkv_specs(tm):
    return [pl.BlockSpec((tm, Q_LORA), lambda i: (i, U_CQ // Q_LORA)),
            pl.BlockSpec((tm, KV_LORA), lambda i: (i, U_CKV // KV_LORA)),
            pl.BlockSpec((tm, LANES), lambda i: (i, U_KR // LANES)),
            pl.BlockSpec((tm, LANES), lambda i: (i, 0)), pl.BlockSpec((tm, LANES), lambda i: (i, 0))]


def _full(shape):
    return pl.BlockSpec(shape, lambda i: (0,) * len(shape))


def _k_rope_lanes(blk):
    lane = lax.broadcasted_iota(jnp.int32, blk.shape, 1)
    return jnp.where(lane < QK_ROPE, blk, 0.0)


def _qkv_fwd(u, cos, sin, wq, wkv, qag, kvag, qg, kg):
    s = u.shape[0]
    tm = min(TM_ELEM, s)

    def body(cq_ref, ckv_ref, kr_ref, cos_ref, sin_ref, wq_ref, wkv_ref, qag_ref, kvag_ref, qg_ref, kg_ref,
             q_ref, k_ref, v_ref):
        cq = cq_ref[...].astype(F32)
        cqn = (cq * lax.rsqrt(jnp.mean(cq * cq, axis=-1, keepdims=True) + EPS) * qag_ref[...]).astype(BF16)
        ckv = ckv_ref[...].astype(F32)
        ckvn = (ckv * lax.rsqrt(jnp.mean(ckv * ckv, axis=-1, keepdims=True) + EPS) * kvag_ref[...]).astype(BF16)
        kr = _k_rope_lanes(kr_ref[...].astype(F32))
        cosv, sinv, qgv, kgv = cos_ref[...], sin_ref[...], qg_ref[...], kg_ref[...]
        ss_r = jnp.sum(kr * kr, axis=-1, keepdims=True)
        krr = _rope(kr * kgv[:, QK_NOPE:], cosv, sinv)
        for h in range(N_HEADS):
            qh = _dot(cqn, wq_ref[h])
            rq = lax.rsqrt(jnp.sum(qh * qh, axis=-1, keepdims=True) * (1.0 / QK_HEAD) + EPS)
            qn = qh * rq * qgv
            qo = jnp.concatenate([qn[:, :QK_NOPE], _rope(qn[:, QK_NOPE:], cosv, sinv)], axis=1) * (SCALE * LOG2E)
            q_ref[h] = qo.astype(BF16)
            kvh = _dot(ckvn, wkv_ref[h])
            kn, vh = kvh[:, :QK_NOPE], kvh[:, QK_NOPE:]
            rk = lax.rsqrt((jnp.sum(kn * kn, axis=-1, keepdims=True) + ss_r) * (1.0 / QK_HEAD) + EPS)
            k_ref[h] = jnp.concatenate([kn * kgv[:, :QK_NOPE] * rk, krr * rk], axis=1).astype(BF16)
            v_ref[h] = jnp.concatenate([vh, jnp.ones_like(vh)], axis=1).astype(BF16)

    return pl.pallas_call(
        body, name="qkv_fwd", grid=(s // tm,),
        in_specs=_qkv_specs(tm) + [_full((N_HEADS, Q_LORA, QK_PAD)), _full((N_HEADS, KV_LORA, 2 * V_HEAD)),
                                   _full((1, Q_LORA)), _full((1, KV_LORA)), _full((1, QK_PAD)), _full((1, QK_PAD))],
        out_specs=[pl.BlockSpec((N_HEADS, tm, QK_PAD), lambda i: (0, i, 0)),
                   pl.BlockSpec((N_HEADS, tm, QK_PAD), lambda i: (0, i, 0)),
                   pl.BlockSpec((N_HEADS, tm, 2 * V_HEAD), lambda i: (0, i, 0))],
        out_shape=[jax.ShapeDtypeStruct((N_HEADS, s, QK_PAD), BF16), jax.ShapeDtypeStruct((N_HEADS, s, QK_PAD), BF16),
                   jax.ShapeDtypeStruct((N_HEADS, s, 2 * V_HEAD), BF16)],
        compiler_params=_params(("parallel",)),
    )(u, u, u, cos, sin, wq, wkv, qag, kvag, qg, kg)


def _qkv_bwd(u, cos, sin, dq, dk, dv, dza, wq, wkv, qag, kvag, qg, kg, du):
    s = u.shape[0]
    tm = min(TM_ELEM, s)
    nt = s // tm

    def body(cq_ref, ckv_ref, kr_ref, cos_ref, sin_ref, dq_ref, dk_ref, dv_ref, dza_ref, wq_ref, wkv_ref, qag_ref,
             kvag_ref, qg_ref, kg_ref, du_in, du_ref, dwq_ref, dwkv_ref, dqag_ref, dkvag_ref, dqg_ref, dkg_ref,
             dwq_acc, dwkv_acc):
        del du_in
        i = pl.program_id(0)

        @pl.when(i == 0)
        def _():
            dwq_acc[...] = jnp.zeros_like(dwq_acc)
            dwkv_acc[...] = jnp.zeros_like(dwkv_acc)

        cq = cq_ref[...].astype(F32)
        rqa = lax.rsqrt(jnp.mean(cq * cq, axis=-1, keepdims=True) + EPS)
        xq = cq * rqa
        qagv = qag_ref[...]
        cqn = (xq * qagv).astype(BF16)
        ckv = ckv_ref[...].astype(F32)
        rkva = lax.rsqrt(jnp.mean(ckv * ckv, axis=-1, keepdims=True) + EPS)
        xkv = ckv * rkva
        kvagv = kvag_ref[...]
        ckvn = (xkv * kvagv).astype(BF16)
        kr = _k_rope_lanes(kr_ref[...].astype(F32))
        cosv, sinv, qgv, kgv = cos_ref[...], sin_ref[...], qg_ref[...], kg_ref[...]
        ss_r = jnp.sum(kr * kr, axis=-1, keepdims=True)
        dqg = jnp.zeros((1, QK_PAD), F32)
        dkg = jnp.zeros((1, QK_PAD), F32)
        dkr = jnp.zeros((tm, LANES), F32)
        dcqn = jnp.zeros((tm, Q_LORA), F32)
        dckvn = jnp.zeros((tm, KV_LORA), F32)
        for h in range(N_HEADS):
            qh = _dot(cqn, wq_ref[h])
            rq = lax.rsqrt(jnp.sum(qh * qh, axis=-1, keepdims=True) * (1.0 / QK_HEAD) + EPS)
            xh = qh * rq
            g = dq_ref[h].astype(F32) * SCALE
            dyq = jnp.concatenate([g[:, :QK_NOPE], _rope_t(g[:, QK_NOPE:], cosv, sinv)], axis=1)
            dqg = dqg + jnp.sum(dyq * xh, axis=0, keepdims=True)
            gdy = dyq * qgv
            dqh = (rq * (gdy - xh * (jnp.sum(gdy * xh, axis=-1, keepdims=True) * (1.0 / QK_HEAD)))).astype(BF16)
            dwq_acc[h] += _dot_tn(cqn, dqh)
            dcqn = dcqn + _dot_nt(dqh, wq_ref[h])
            kn = _dot(ckvn, wkv_ref[h])[:, :QK_NOPE]
            rk = lax.rsqrt((jnp.sum(kn * kn, axis=-1, keepdims=True) + ss_r) * (1.0 / QK_HEAD) + EPS)
            xk = jnp.concatenate([kn, kr], axis=1) * rk
            gk = dk_ref[h].astype(F32)
            dyk = jnp.concatenate([gk[:, :QK_NOPE], _rope_t(gk[:, QK_NOPE:], cosv, sinv)], axis=1)
            dkg = dkg + jnp.sum(dyk * xk, axis=0, keepdims=True)
            gdyk = dyk * kgv
            dxk = rk * (gdyk - xk * (jnp.sum(gdyk * xk, axis=-1, keepdims=True) * (1.0 / QK_HEAD)))
            dkr = dkr + dxk[:, QK_NOPE:]
            dkvh = jnp.concatenate([dxk[:, :QK_NOPE].astype(BF16), dv_ref[h]], axis=1)
            dwkv_acc[h] += _dot_tn(ckvn, dkvh)
            dckvn = dckvn + _dot_nt(dkvh, wkv_ref[h])
        dqag = jnp.sum(dcqn * xq, axis=0, keepdims=True)
        dkvag = jnp.sum(dckvn * xkv, axis=0, keepdims=True)
        gq = dcqn * qagv
        dcq = rqa * (gq - xq * jnp.mean(gq * xq, axis=-1, keepdims=True))
        gkv = dckvn * kvagv
        dckv = rkva * (gkv - xkv * jnp.mean(gkv * xkv, axis=-1, keepdims=True))
        win = pltpu.roll(jnp.concatenate([dza_ref[...].astype(F32), jnp.zeros((tm, LANES), F32)], axis=1), QK_ROPE, 1)
        win = win + jnp.concatenate([dkr, jnp.zeros((tm, D_ATTN), F32)], axis=1)
        du_ref[...] = jnp.concatenate([dcq, dckv, win, jnp.zeros((tm, U_TAIL - ZA_LO - ZA_WIN), F32)], axis=1).astype(BF16)

        @pl.when(i == 0)
        def _():
            dqag_ref[...] = dqag
            dkvag_ref[...] = dkvag
            dqg_ref[...] = dqg
            dkg_ref[...] = dkg

        @pl.when(i > 0)
        def _():
            dqag_ref[...] += dqag
            dkvag_ref[...] += dkvag
            dqg_ref[...] += dqg
            dkg_ref[...] += dkg

        @pl.when(i == nt - 1)
        def _():
            dwq_ref[...] = dwq_acc[...].astype(BF16)
            dwkv_ref[...] = dwkv_acc[...].astype(BF16)

    head = lambda w: pl.BlockSpec((N_HEADS, tm, w), lambda i: (0, i, 0))
    wq_shape, wkv_shape = (N_HEADS, Q_LORA, QK_PAD), (N_HEADS, KV_LORA, 2 * V_HEAD)
    return pl.pallas_call(
        body, name="qkv_bwd", grid=(nt,),
        in_specs=_qkv_specs(tm) + [head(QK_PAD), head(QK_PAD), head(V_HEAD), pl.BlockSpec((tm, D_ATTN), lambda i: (i, 0)),
                                   _full(wq_shape), _full(wkv_shape), _full((1, Q_LORA)), _full((1, KV_LORA)),
                                   _full((1, QK_PAD)), _full((1, QK_PAD)), ANY],
        out_specs=[pl.BlockSpec((tm, U_TAIL), lambda i: (i, U_COLS // U_TAIL - 1)), _full(wq_shape), _full(wkv_shape),
                   _full((1, Q_LORA)), _full((1, KV_LORA)), _full((1, QK_PAD)), _full((1, QK_PAD))],
        out_shape=[jax.ShapeDtypeStruct(du.shape, du.dtype), jax.ShapeDtypeStruct(wq_shape, BF16),
                   jax.ShapeDtypeStruct(wkv_shape, BF16), jax.ShapeDtypeStruct((1, Q_LORA), F32),
                   jax.ShapeDtypeStruct((1, KV_LORA), F32), jax.ShapeDtypeStruct((1, QK_PAD), F32),
                   jax.ShapeDtypeStruct((1, QK_PAD), F32)],
        scratch_shapes=[pltpu.VMEM(wq_shape, F32), pltpu.VMEM(wkv_shape, F32)],
        input_output_aliases={15: 0}, compiler_params=_params(("arbitrary",)),
    )(u, u, u, cos, sin, dq, dk, dv, dza, wq, wkv, qag, kvag, qg, kg, du)


def _flash_fwd(q, k, v):
    nh, s, _ = q.shape
    tq = min(TQ, s)
    tk = tq // 2
    nq = s // tq

    def body(q_ref, k_ref, v_ref, o_ref, lse_ref):
        i = pl.program_id(1)
        qv = q_ref[0]

        def step(j, carry, diag_offset=None):
            m, acc = carry
            rows = pl.ds(pl.multiple_of(j * tk, tk), tk)
            sc = _dot_nt(qv, k_ref[0, rows, :])
            if diag_offset is not None:
                qi = lax.broadcasted_iota(jnp.int32, sc.shape, 0)
                ki = lax.broadcasted_iota(jnp.int32, sc.shape, 1) + diag_offset
                sc = jnp.where(ki <= qi, sc, NEG)
            m_new = jnp.maximum(m, jnp.max(sc, axis=-1, keepdims=True))
            p = jnp.exp2(sc - m_new)
            acc = jnp.exp2(m - m_new) * acc + _dot(p.astype(BF16), v_ref[0, rows, :])
            return m_new, acc

        init = (jnp.full((tq, 1), NEG, F32), jnp.zeros((tq, 2 * V_HEAD), F32))
        carry = lax.fori_loop(0, i, lambda p, cr: step(2 * p + 1, step(2 * p, cr)), init)
        m, acc = step(2 * i + 1, step(2 * i, carry, 0), tk)
        l = acc[:, V_HEAD:]
        o_ref[...] = (acc[:, :V_HEAD] / l).astype(BF16)
        lse = m + jnp.log(l[:, 0:1]) * LOG2E
        lse_ref[0] = jnp.broadcast_to(lse, (tq, LANES)).T[0:1, :]

    return pl.pallas_call(
        body, name="flash_fwd", grid=(nh, nq),
        in_specs=[pl.BlockSpec((1, tq, QK_PAD), lambda h, i: (h, i, 0)),
                  pl.BlockSpec((1, s, QK_PAD), lambda h, i: (h, 0, 0)),
                  pl.BlockSpec((1, s, 2 * V_HEAD), lambda h, i: (h, 0, 0))],
        out_specs=[pl.BlockSpec((tq, V_HEAD), lambda h, i: (i, h)), pl.BlockSpec((1, 1, tq), lambda h, i: (h, 0, i))],
        out_shape=[jax.ShapeDtypeStruct((s, nh * V_HEAD), BF16), jax.ShapeDtypeStruct((nh, 1, s), F32)],
        compiler_params=_params(("parallel", "arbitrary")),
    )(q, k, v)


def _flash_bwd(q, k, v, do, lse, delta):
    nh, s, _ = q.shape
    tq = min(TQ, s)
    nq = s // tq

    def body(q_ref, k_ref, v_ref, do_ref, lse_ref, dl_ref, dq_ref, dk_ref, dv_ref, dq_acc):
        j = pl.program_id(1)

        @pl.when(j == 0)
        def _():
            dq_acc[...] = jnp.zeros_like(dq_acc)

        kj, vj = k_ref[0], v_ref[0]

        def step(i, carry, masked):
            dk, dv = carry
            rows = pl.ds(pl.multiple_of(i * tq, tq), tq)
            qi, doi = q_ref[0, rows, :], do_ref[rows, :]
            st = _dot_nt(kj, qi)
            pt = jnp.exp2(st - lse_ref[0, pl.ds(i, 1), :])
            if masked:
                ki = lax.broadcasted_iota(jnp.int32, st.shape, 0)
                qx = lax.broadcasted_iota(jnp.int32, st.shape, 1)
                pt = jnp.where(ki <= qx, pt, 0.0)
            dv = dv + _dot(pt.astype(BF16), doi)
            dpt = _dot_nt(vj, doi)
            dst = (pt * (dpt - dl_ref[0, pl.ds(i, 1), :])).astype(BF16)
            dk = dk + _dot(dst, qi)
            dq_acc[rows, :] += _dot_tn(dst, kj)
            return dk, dv

        carry = step(j, (jnp.zeros((tq, QK_PAD), F32), jnp.zeros((tq, V_HEAD), F32)), True)
        dk, dv = lax.fori_loop(j + 1, nq, lambda i, cr: step(i, cr, False), carry)
        dk_ref[0] = (dk * LN2).astype(BF16)
        dv_ref[0] = dv.astype(BF16)

        @pl.when(j == nq - 1)
        def _():
            dq_ref[0] = dq_acc[...].astype(BF16)

    return pl.pallas_call(
        body, name="flash_bwd", grid=(nh, nq),
        in_specs=[pl.BlockSpec((1, s, QK_PAD), lambda h, j: (h, 0, 0)),
                  pl.BlockSpec((1, tq, QK_PAD), lambda h, j: (h, j, 0)),
                  pl.BlockSpec((1, tq, V_HEAD), lambda h, j: (h, j, 0)),
                  pl.BlockSpec((s, V_HEAD), lambda h, j: (0, h)),
                  pl.BlockSpec((1, nq, tq), lambda h, j: (h, 0, 0)),
                  pl.BlockSpec((1, nq, tq), lambda h, j: (h, 0, 0))],
        out_specs=[pl.BlockSpec((1, s, QK_PAD), lambda h, j: (h, 0, 0)),
                   pl.BlockSpec((1, tq, QK_PAD), lambda h, j: (h, j, 0)),
                   pl.BlockSpec((1, tq, V_HEAD), lambda h, j: (h, j, 0))],
        out_shape=[jax.ShapeDtypeStruct((nh, s, QK_PAD), BF16), jax.ShapeDtypeStruct((nh, s, QK_PAD), BF16),
                   jax.ShapeDtypeStruct((nh, s, V_HEAD), BF16)],
        scratch_shapes=[pltpu.VMEM((s, QK_PAD), F32)],
        compiler_params=_params(("parallel", "arbitrary")),
    )(q, k, v, do, lse, delta)


def _tail(x, target, yc, o, u, mod, w_out):
    s, d = x.shape
    tm = min(TM_ELEM, s)

    def body(x_ref, t_ref, yc_ref, o_ref, za_ref, mod_ref, w_ref,
             gx_ref, dy_ref, ycat_ref, dyc_ref, do_ref, du_ref, delta_ref, dgate_ref, loss_ref):
        i = pl.program_id(0)
        za = pltpu.roll(za_ref[:, ZA_LO:ZA_LO + ZA_WIN].astype(F32), ZA_WIN - QK_ROPE, 1)[:, :D_ATTN]
        ov = o_ref[...].astype(F32)
        sg = _sigmoid(za)
        sl = za * sg
        ya = ov * sl
        ycat = jnp.concatenate([yc_ref[...], ya.astype(BF16)], axis=1)
        ycat_ref[...] = jnp.concatenate([yc_ref[...].astype(F32).T, ya.T], axis=0).astype(BF16)
        y = _dot(ycat, w_ref[...])
        gate = mod_ref[:, 2 * d:3 * d]
        e = x_ref[...] + gate * y - t_ref[...]
        dout = e * (1.0 / d)
        gx_ref[...] = dout
        dy = (dout * gate).astype(BF16)
        dy_ref[...] = dy
        dycat = _dot_nt(dy, w_ref[...])
        dyc_ref[...] = dycat[:, :D_CONV].astype(BF16)
        dya = dycat[:, D_CONV:]
        dov = dya * sl
        do_ref[...] = dov.astype(BF16)
        du_ref[...] = (dya * ov * (sg * (1.0 + za * (1.0 - sg)))).astype(BF16)
        prod_t = (dov * ov).T
        for h in range(N_HEADS):
            delta_ref[h] = jnp.sum(prod_t[V_HEAD * h:V_HEAD * (h + 1), :], axis=0, keepdims=True)
        dgate = jnp.sum(dout * y, axis=0, keepdims=True)
        part = jnp.sum(jnp.sum(e * e, axis=0, keepdims=True), axis=1, keepdims=True) * (0.5 / d)
        part = jnp.broadcast_to(part, (1, LANES))

        @pl.when(i == 0)
        def _():
            dgate_ref[...] = dgate
            loss_ref[...] = part

        @pl.when(i > 0)
        def _():
            dgate_ref[...] += dgate
            loss_ref[...] += part

    tok = lambda w: pl.BlockSpec((tm, w), lambda i: (i, 0))
    return pl.pallas_call(
        body, name="tail", grid=(s // tm,),
        in_specs=[tok(d), tok(d), tok(D_CONV), tok(D_ATTN), pl.BlockSpec((tm, U_TAIL), lambda i: (i, U_COLS // U_TAIL - 1)),
                  _full((1, 3 * d)), _full((d, d))],
        out_specs=[tok(d), tok(d), pl.BlockSpec((d, tm), lambda i: (0, i)), tok(D_CONV), tok(D_ATTN), tok(D_ATTN),
                   pl.BlockSpec((N_HEADS, 1, tm), lambda i: (0, 0, i)), _full((1, d)), _full((1, LANES))],
        out_shape=[jax.ShapeDtypeStruct((s, d), F32), jax.ShapeDtypeStruct((s, d), BF16),
                   jax.ShapeDtypeStruct((d, s), BF16), jax.ShapeDtypeStruct((s, D_CONV), BF16),
                   jax.ShapeDtypeStruct((s, D_ATTN), BF16), jax.ShapeDtypeStruct((s, D_ATTN), BF16),
                   jax.ShapeDtypeStruct((N_HEADS, 1, s), F32), jax.ShapeDtypeStruct((1, d), F32),
                   jax.ShapeDtypeStruct((1, LANES), F32)],
        compiler_params=_params(("arbitrary",)),
    )(x, target, yc, o, u, mod, w_out)


def _norm_bwd(x, dh, gx1, norm_g, mod):
    s, d = x.shape
    tm = min(TM_ELEM, s)

    def body(x_ref, dh_ref, gx_ref, g_ref, mod_ref, o_ref, dshift_ref, dscale_ref, dg_ref):
        i = pl.program_id(0)
        xv, dhv, gv = x_ref[...], dh_ref[...].astype(F32), g_ref[...]
        r = lax.rsqrt(jnp.mean(xv * xv, axis=-1, keepdims=True) + EPS)
        xn = xv * r
        dhn = dhv * (1.0 + mod_ref[:, d:2 * d])
        dxn = dhn * gv
        o_ref[...] = gx_ref[...] + r * (dxn - xn * jnp.mean(dxn * xn, axis=-1, keepdims=True))
        dshift = jnp.sum(dhv, axis=0, keepdims=True)
        dscale = jnp.sum(dhv * xn * gv, axis=0, keepdims=True)
        dg = jnp.sum(dhn * xn, axis=0, keepdims=True)

        @pl.when(i == 0)
        def _():
            dshift_ref[...] = dshift
            dscale_ref[...] = dscale
            dg_ref[...] = dg

        @pl.when(i > 0)
        def _():
            dshift_ref[...] += dshift
            dscale_ref[...] += dscale
            dg_ref[...] += dg

    tok = pl.BlockSpec((tm, d), lambda i: (i, 0))
    row = jax.ShapeDtypeStruct((1, d), F32)
    return pl.pallas_call(
        body, name="norm_bwd", grid=(s // tm,),
        in_specs=[tok, tok, tok, _full((1, d)), _full((1, 3 * d))],
        out_specs=[tok, _full((1, d)), _full((1, d)), _full((1, d))],
        out_shape=[jax.ShapeDtypeStruct((s, d), F32), row, row, row],
        compiler_params=_params(("arbitrary",)),
    )(x, dh, gx1, norm_g, mod)


def _adamw(w, g, m, v, name):
    rows, cols = w.shape
    tr = 256 if rows % 256 == 0 else rows

    def body(w_ref, g_ref, m_ref, v_ref, d_ref, nm_ref, nv_ref):
        gv = g_ref[...]
        nm = ADAM_B1 * m_ref[...] + (1.0 - ADAM_B1) * gv
        nv = ADAM_B2 * v_ref[...] + (1.0 - ADAM_B2) * (gv * gv)
        m_hat = nm / (1.0 - ADAM_B1 ** ADAM_STEP)
        v_hat = nv / (1.0 - ADAM_B2 ** ADAM_STEP)
        d_ref[...] = -ADAM_LR * (m_hat / (jnp.sqrt(v_hat) + ADAM_EPS) + ADAM_WD * w_ref[...])
        nm_ref[...] = nm
        nv_ref[...] = nv

    spec = pl.BlockSpec((tr, cols), lambda i: (i, 0))
    shape = jax.ShapeDtypeStruct((rows, cols), F32)
    return pl.pallas_call(
        body, name=name, grid=(rows // tr,), in_specs=[spec] * 4, out_specs=[spec] * 3, out_shape=[shape] * 3,
        compiler_params=_params(("parallel",)),
    )(w, g, m, v)


def _pad_cols(a, n):
    return jnp.pad(a, ((0, 0), (0, n - a.shape[1])))


def kernel(x, c, positions, ada_w, ada_b, norm_g, w_in, conv_w, q_a_g, w_q_b, kv_a_g, w_kv_b, q_g, k_g, w_out, loss_target, m_ada_w, m_ada_b, m_norm_g, m_w_in, m_conv_w, m_q_a_g, m_w_q_b, m_kv_a_g, m_w_kv_b, m_q_g, m_k_g, m_w_out, v_ada_w, v_ada_b, v_norm_g, v_w_in, v_conv_w, v_q_a_g, v_w_q_b, v_kv_a_g, v_w_kv_b, v_q_g, v_k_g, v_w_out):
    me = _my_index()
    s = x.shape[1]
    nq = s // min(TQ, s)
    x2, tgt = x[0], loss_target[0]
    w_in_l, w_q_l, w_kv_l, w_out_l, conv_l, ada_w_l = w_in[0], w_q_b[0], w_kv_b[0], w_out[0], conv_w[0], ada_w[0]
    ada_cols = ada_w_l.shape[1]

    my_off = ((CW * me) % LANES).astype(jnp.int32)
    small =jnp.concatenate([c.reshape(-1, LANES), conv_l.reshape(-1, LANES), jnp.zeros((5, LANES), F32)], axis=0)
    win_g, small_g = _all_gather([_expand_w_in(w_in_l, my_off.reshape(1)), small], "gather_w_in")
    w_in_p = _merge_w_in(win_g)
    c_all = small_g[:, :D_MODEL // LANES].reshape(N_DEV, D_MODEL)
    conv_g = small_g[:, D_MODEL // LANES:D_MODEL // LANES + 3].transpose(1, 0, 2).reshape(3, D_CONV)

    ada_b_l = lax.dynamic_slice(ada_b, (0, me * ada_cols), (1, ada_cols))
    mod_cols = _ada_mod(jnp.pad(c_all, ((0, 8), (0, 0))), ada_w_l, ada_b_l)[:N_DEV]
    (mod_g,) = _all_gather([mod_cols], "gather_mod")
    mod = lax.dynamic_index_in_dim(mod_g, me, axis=1, keepdims=False).reshape(1, 3 * D_MODEL)

    half = jnp.arange(0, QK_ROPE, 2, dtype=F32) / QK_ROPE
    inv_freq = ROPE_BASE ** (-half)
    zeros64 = jnp.zeros((LANES - QK_ROPE,), F32)
    invf = jnp.concatenate([inv_freq, inv_freq, zeros64]).reshape(1, LANES)
    sign = jnp.concatenate([-jnp.ones((32,), F32), jnp.ones((32,), F32), zeros64]).reshape(1, LANES)
    cos, sin = _rope_tables(positions.reshape(s, 1), invf, sign)
    qg_p, kg_p = _pad_cols(q_g, QK_PAD), _pad_cols(k_g, QK_PAD)

    h, h_t = _norm_mod(x2, norm_g, mod)
    rest = [_pad_wq(w_q_l), w_kv_l.astype(BF16), w_out_l.astype(BF16)]
    u, wq_g, wkv_g, w_out_g = _matmul(h, w_in_p, nt=False, out_dtype=BF16, tm=TM_MM, tn=1024, name="in_proj",
                                      rider=_Gather(rest), rider_inputs=rest)
    w_out_g = w_out_g.reshape(D_MODEL, D_MODEL)
    yc = _conv_fwd(u, conv_g)
    q, k, v = _qkv_fwd(u, cos, sin, wq_g, wkv_g, q_a_g, kv_a_g, qg_p, kg_p)
    o, lse = _flash_fwd(q, k, v)
    gx1, dy, ycat_t, dyc, do, dza, delta, dgate, loss_row = _tail(x2, tgt, yc, o, u, mod, w_out_g)

    dq, dk, dv = _flash_bwd(q, k, v, do, lse.reshape(N_HEADS, nq, s // nq), delta.reshape(N_HEADS, nq, s // nq))
    du, dconv = _conv_bwd(u, dyc, conv_g)
    du, dwq, dwkv, dqag, dkvag, dqg, dkg = _qkv_bwd(u, cos, sin, dq, dk, dv, dza, wq_g, wkv_g, q_a_g, kv_a_g, qg_p, kg_p, du)
    dw_out = _matmul(ycat_t, dy, nt=False, out_dtype=BF16, tm=TM_MM, tn=512, name="dw_out")
    dw_out = dw_out.reshape(N_DEV, D_MODEL // N_DEV, D_MODEL)
    dw_in = _matmul(h_t, du, nt=False, out_dtype=BF16, tm=TM_MM, tn=512, name="dw_in")

    core = lax.axis_index("c").astype(jnp.int32)
    r_in, r_q, r_kv, r_out = _rs_sibling(dw_in, dwq, dwkv, dw_out)
    lo_tiles = ((CW * (2 * jnp.arange(4, dtype=jnp.int32) + core)) // LANES).astype(jnp.int32)
    pairs = [_add_window(dw_in, r_in, lo_tiles), _add_pairs(dwq, r_q, core.reshape(1), "rs_add_q"),
             _add_pairs(dwkv, r_kv, core.reshape(1), "rs_add_kv"), _add_pairs(dw_out, r_out, core.reshape(1), "rs_add_out")]
    dh, *quads = _matmul(du, w_in_p, nt=True, out_dtype=BF16, tm=TM_MM, tn=512, name="dh",
                         rider=_ChipExchange(pairs), rider_inputs=pairs)
    my_chip = 2 * lax.axis_index("x") + lax.axis_index("y")
    written = jnp.where(jnp.arange(4) == my_chip, (jnp.arange(4) + 1) % 4, jnp.arange(4))
    sel = jnp.concatenate([my_chip.reshape(1), written, ((EXP_W - my_off) % EXP_W).reshape(1)]).astype(jnp.int32)
    g_w_in = _final_sum(pairs[0], quads[0], sel, "rs_sum_in", unshift_to=CW)
    g_w_q = _final_sum(pairs[1], quads[1], sel, "rs_sum_q")[:, :QK_HEAD]
    g_w_kv = _final_sum(pairs[2], quads[2], sel, "rs_sum_kv")
    g_w_out = _final_sum(pairs[3], quads[3], sel, "rs_sum_out")
    grad_x, dshift, dscale, dng = _norm_bwd(x2, dh, gx1, norm_g, mod)

    row = jnp.concatenate([dshift, dscale, dgate, dng, dqag, dkvag, dqg, dkg, dconv[:3].reshape(1, 3 * D_CONV), loss_row], axis=1)
    (rows_g,) = _all_gather([row], "gather_small")
    tot = _sum_leading(rows_g, F32, "sum_small")
    dmod_all = rows_g[:, 0, SM_MOD:SM_NG]
    g_ada_b = tot[:, SM_MOD:SM_NG]
    g_norm_g = tot[:, SM_NG:SM_QAG]
    g_q_a_g = tot[:, SM_QAG:SM_KVAG]
    g_kv_a_g = tot[:, SM_KVAG:SM_QG]
    g_q_g = tot[:, SM_QG:SM_QG + QK_HEAD]
    g_k_g = tot[:, SM_KG:SM_KG + QK_HEAD]
    conv_cols = conv_l.shape[1]
    g_conv = lax.dynamic_slice(tot[:, SM_CONV:SM_LOSS].reshape(3, D_CONV), (0, me * conv_cols), (3, conv_cols))
    loss = tot[0, SM_LOSS]
    dmod_my = lax.dynamic_slice(dmod_all, (0, me * ada_cols), (N_DEV, ada_cols))
    g_ada_w = _ada_w_grad(c_all.T, dmod_my)

    grads = dict(ada_w=g_ada_w, ada_b=g_ada_b, norm_g=g_norm_g, w_in=g_w_in, conv_w=g_conv, q_a_g=g_q_a_g, w_q_b=g_w_q,
                 kv_a_g=g_kv_a_g, w_kv_b=g_w_kv, q_g=g_q_g, k_g=g_k_g, w_out=g_w_out)
    weights = dict(ada_w=(ada_w, m_ada_w, v_ada_w), ada_b=(ada_b, m_ada_b, v_ada_b), norm_g=(norm_g, m_norm_g, v_norm_g),
                   w_in=(w_in, m_w_in, v_w_in), conv_w=(conv_w, m_conv_w, v_conv_w), q_a_g=(q_a_g, m_q_a_g, v_q_a_g),
                   w_q_b=(w_q_b, m_w_q_b, v_w_q_b), kv_a_g=(kv_a_g, m_kv_a_g, v_kv_a_g), w_kv_b=(w_kv_b, m_w_kv_b, v_w_kv_b),
                   q_g=(q_g, m_q_g, v_q_g), k_g=(k_g, m_k_g, v_k_g), w_out=(w_out, m_w_out, v_w_out))
    names = list(grads)
    out_g, out_d, out_m, out_v = [], [], [], []
    for n in names:
        w, m, v_ = weights[n]
        shape2 = w.shape[-2:] if w.ndim == 3 else (1, w.shape[-1])
        g2 = grads[n].reshape(shape2)
        d2, m2, v2 = _adamw(w.reshape(shape2), g2, m.reshape(shape2), v_.reshape(shape2), "adamw_" + n)
        out_g.append(g2.reshape(w.shape))
        out_d.append(d2.reshape(w.shape))
        out_m.append(m2.reshape(w.shape))
        out_v.append(v2.reshape(w.shape))
    return (loss, grad_x.reshape(x.shape), *out_g, *out_d, *out_m, *out_v)
```

```python
import functools
import math

import jax
import jax.numpy as jnp
from jax import lax
from jax.experimental import pallas as pl
from jax.experimental.pallas import tpu as pltpu

F32 = jnp.float32
BF16 = jnp.bfloat16
MESH = pl.DeviceIdType.MESH

D_MODEL = 2048
D_CONV = 1024
N_HEADS = 8
QK_NOPE = 128
QK_ROPE = 64
QK_HEAD = QK_NOPE + QK_ROPE
V_HEAD = 128
D_ATTN = N_HEADS * V_HEAD
Q_LORA = 512
KV_LORA = 256
ROPE_BASE = 10000.0
IN_COLS = 4 * D_CONV + Q_LORA + KV_LORA + QK_ROPE + D_ATTN
EPS = 1e-6
ADAM_LR, ADAM_B1, ADAM_B2, ADAM_EPS, ADAM_WD, ADAM_STEP = 0.001, 0.9, 0.999, 1e-08, 0.01, 10

N_DEV = 8
LANES = 128
QK_PAD = 256
U_COLS = 6144
U_CQ, U_CKV, U_KR, U_ZA = 4096, 4608, 4864, 4928
U_TAIL = 2048
ZA_LO = U_ZA - (U_COLS - U_TAIL) - QK_ROPE
ZA_WIN = D_ATTN + LANES
CW = IN_COLS // 8
EXP_W = 896
W_LO = [(CW * d // 128) * 128 for d in range(8)]
W_OFF = [CW * d - lo for d, lo in enumerate(W_LO)]
SCALE = 1.0 / math.sqrt(QK_HEAD)
LOG2E = 1.4426950408889634
LN2 = 0.6931471805599453
NEG = -1e30
VMEM_LIMIT = 56 * 1024 * 1024

TM_ELEM = 256
TM_MM = 512
TQ = 1024

SM_MOD, SM_NG, SM_QAG, SM_KVAG, SM_QG, SM_KG, SM_CONV, SM_LOSS = 0, 6144, 8192, 8704, 8960, 9216, 9472, 12544
SM_COLS = 12672


def _params(sem=None):
    kw = dict(vmem_limit_bytes=VMEM_LIMIT)
    if sem is not None:
        kw["dimension_semantics"] = sem
    return pltpu.CompilerParams(**kw)


def _sigmoid(z):
    return 1.0 / (1.0 + jnp.exp(-z))


def _rot64(x):
    lane = lax.broadcasted_iota(jnp.int32, x.shape, 1)
    return jnp.where(lane < 32, pltpu.roll(x, 96, 1), pltpu.roll(x, 32, 1))


def _rope(x, cos, sin):
    return x * cos + _rot64(x) * sin


def _rope_t(d, cos, sin):
    return d * cos - _rot64(d) * sin


def _dot(a, b):
    return jnp.dot(a, b, preferred_element_type=F32)


def _dot_nt(a, b):
    return lax.dot_general(a, b, (((1,), (1,)), ((), ())), preferred_element_type=F32)


def _dot_tn(a, b):
    return lax.dot_general(a, b, (((0,), (0,)), ((), ())), preferred_element_type=F32)


def _my_index():
    return 4 * lax.axis_index("x") + 2 * lax.axis_index("y") + lax.axis_index("c")


ANY = pl.BlockSpec(memory_space=pl.ANY)


class _Gather:
    def __init__(self, blocks):
        self.n = n = len(blocks)
        self.out_shape = [jax.ShapeDtypeStruct((N_DEV,) + b.shape, b.dtype) for b in blocks]
        self.scratch = [pltpu.SemaphoreType.DMA((7 * n,)), pltpu.SemaphoreType.DMA((7 * n,)),
                        pltpu.SemaphoreType.DMA((n,))]

    @staticmethod
    def _places():
        x, y, c = lax.axis_index("x"), lax.axis_index("y"), lax.axis_index("c")
        return (x, y, c), (x, y, 1 - c), [(1 - x, y), (x, 1 - y), (1 - x, 1 - y)]

    @staticmethod
    def _copy(outs, sems, a, k, block, to, src=None):
        dst = outs[a].at[4 * block[0] + 2 * block[1] + block[2]]
        return pltpu.make_async_remote_copy(
            src_ref=dst if src is None else src, dst_ref=dst, send_sem=sems[0].at[7 * a + k],
            recv_sem=sems[1].at[7 * a + k], device_id=to, device_id_type=MESH)

    def _first(self, ins, outs, sems):
        me, sibling, chips = self._places()
        first = []
        for a in range(self.n):
            first.append(self._copy(outs, sems, a, 0, me, sibling, src=ins[a]))
            first += [self._copy(outs, sems, a, 1 + j, me, (*chip, me[2]), src=ins[a]) for j, chip in enumerate(chips)]
        return first

    def _passed(self, outs, sems):
        me, sibling, chips = self._places()
        return [self._copy(outs, sems, a, 4 + j, (*chip, me[2]), sibling)
                for a in range(self.n) for j, chip in enumerate(chips)]

    def _mine(self, ins, outs, sems):
        me, _, _ = self._places()
        return [pltpu.make_async_copy(ins[a], outs[a].at[4 * me[0] + 2 * me[1] + me[2]], sems[2].at[a])
                for a in range(self.n)]

    def start(self, ins, outs, sems):
        for cp in self._mine(ins, outs, sems) + self._first(ins, outs, sems):
            cp.start()

    def forward(self, ins, outs, sems):
        del ins
        me, _, chips = self._places()
        passed = self._passed(outs, sems)
        for a in range(self.n):
            for j, chip in enumerate(chips):
                self._copy(outs, sems, a, 1 + j, (*chip, me[2]), me).wait_recv()
                passed[3 * a + j].start()

    def finish(self, ins, outs, sems):
        me, sibling, chips = self._places()
        for a in range(self.n):
            self._copy(outs, sems, a, 0, sibling, me).wait_recv()
            for j, chip in enumerate(chips):
                self._copy(outs, sems, a, 4 + j, (*chip, 1 - me[2]), me).wait_recv()
        for cp in self._first(ins, outs, sems) + self._passed(outs, sems):
            cp.wait_send()
        for cp in self._mine(ins, outs, sems):
            cp.wait()


class _ChipExchange:
    def __init__(self, arrays):
        self.n = n = len(arrays)
        self.out_shape = [jax.ShapeDtypeStruct(a.shape, a.dtype) for a in arrays]
        self.scratch = [pltpu.SemaphoreType.DMA((3 * n,)), pltpu.SemaphoreType.DMA((3 * n,))]

    def _copies(self, ins, outs, sems):
        x, y, c = lax.axis_index("x"), lax.axis_index("y"), lax.axis_index("c")
        return [pltpu.make_async_remote_copy(
            src_ref=ins[a].at[2 * px + py], dst_ref=outs[a].at[2 * x + y], send_sem=sems[0].at[3 * a + j],
            recv_sem=sems[1].at[3 * a + j], device_id=(px, py, c), device_id_type=MESH)
            for a in range(self.n) for j, (px, py) in enumerate([(1 - x, y), (x, 1 - y), (1 - x, 1 - y)])]

    def start(self, ins, outs, sems):
        for cp in self._copies(ins, outs, sems):
            cp.start()

    def forward(self, ins, outs, sems):
        pass

    def finish(self, ins, outs, sems):
        for cp in self._copies(ins, outs, sems):
            cp.wait()


def _all_gather(blocks, name):
    n = len(blocks)
    g = _Gather(blocks)

    def body(*refs):
        ins, outs, sems = refs[:n], refs[n:2 * n], refs[2 * n:]
        g.start(ins, outs, sems)
        g.forward(ins, outs, sems)
        g.finish(ins, outs, sems)

    return pl.pallas_call(body, name=name, out_shape=g.out_shape, in_specs=[ANY] * n, out_specs=[ANY] * n,
                          scratch_shapes=g.scratch)(*blocks)


def _rs_sibling(dw_in, dwq, dwkv, dwout):
    srcs = (dw_in, dwq, dwkv, dwout)
    shapes = [(dw_in.shape[0], EXP_W)] + [a.shape[1:] for a in srcs[1:]]

    def body(in_ref, q_ref, kv_ref, out_ref, rin, rq, rkv, rout, send_sems, recv_sems):
        x, y, c = lax.axis_index("x"), lax.axis_index("y"), lax.axis_index("c")

        def exchange(c_val):
            copies = []
            for k in range(4):
                e = 2 * k + (1 - c_val)
                pairs = [(in_ref.at[:, pl.ds(W_LO[e], EXP_W)], rin.at[k]), (q_ref.at[e], rq.at[k]),
                         (kv_ref.at[e], rkv.at[k]), (out_ref.at[e], rout.at[k])]
                for a, (src, dst) in enumerate(pairs):
                    copies.append(pltpu.make_async_remote_copy(
                        src_ref=src, dst_ref=dst, send_sem=send_sems.at[4 * a + k], recv_sem=recv_sems.at[4 * a + k],
                        device_id=(x, y, 1 - c), device_id_type=MESH))
            for cp in copies:
                cp.start()
            for cp in copies:
                cp.wait()

        for c_val in (0, 1):
            pl.when(c == c_val)(functools.partial(exchange, c_val))

    return pl.pallas_call(
        body, name="rs_sibling", out_shape=[jax.ShapeDtypeStruct((4,) + tuple(sh), a.dtype) for sh, a in zip(shapes, srcs)],
        in_specs=[ANY] * 4, out_specs=[ANY] * 4,
        scratch_shapes=[pltpu.SemaphoreType.DMA((16,)), pltpu.SemaphoreType.DMA((16,))],
    )(*srcs)


def _add_window(dw_in, recv, lo_tiles):
    k, rows, _ = recv.shape

    def body(t_ref, w_ref, r_ref, o_ref):
        del t_ref
        o_ref[0] = (w_ref[...].astype(F32) + r_ref[0].astype(F32)).astype(o_ref.dtype)

    spec = pl.BlockSpec((1, rows, LANES), lambda i, j, t: (i, 0, j))
    grid_spec = pltpu.PrefetchScalarGridSpec(
        num_scalar_prefetch=1, grid=(k, EXP_W // LANES),
        in_specs=[pl.BlockSpec((rows, LANES), lambda i, j, t: (0, t[i] + j)), spec], out_specs=spec)
    return pl.pallas_call(
        body, name="rs_add_in", grid_spec=grid_spec, out_shape=jax.ShapeDtypeStruct(recv.shape, recv.dtype),
        compiler_params=_params(("parallel", "parallel")),
    )(lo_tiles, dw_in, recv)


def _final_sum(p, r, sel, name, unshift_to=None):
    _, rows, cols = p.shape
    tr = 512 if rows % 512 == 0 else rows
    out_cols = cols if unshift_to is None else unshift_to

    def body(sel_ref, p_ref, r0, r1, r2, r3, o_ref):
        own = p_ref[0].astype(F32)
        acc = None
        for k, r_ref in enumerate((r0, r1, r2, r3)):
            term = jnp.where(sel_ref[0] == k, own, r_ref[0].astype(F32))
            acc = term if acc is None else acc + term
        if unshift_to is not None:
            acc = pltpu.roll(acc, sel_ref[5], 1)[:, :unshift_to]
        o_ref[...] = acc

    def slot(k):
        return pl.BlockSpec((1, tr, cols), lambda i, t: (t[k], i, 0))

    grid_spec = pltpu.PrefetchScalarGridSpec(
        num_scalar_prefetch=1, grid=(rows // tr,), in_specs=[slot(0), slot(1), slot(2), slot(3), slot(4)],
        out_specs=pl.BlockSpec((tr, out_cols), lambda i, t: (i, 0)))
    return pl.pallas_call(
        body, name=name, grid_spec=grid_spec, out_shape=jax.ShapeDtypeStruct((rows, out_cols), F32),
        compiler_params=_params(("parallel",)),
    )(sel, p, r, r, r, r)


def _expand_w_in(w, shift):
    rows, cw = w.shape
    tr = 256

    def body(s_ref, w_ref, o_ref, buf):
        buf[...] = jnp.zeros_like(buf)
        buf[:, 0:cw] = w_ref[...]
        o_ref[...] = pltpu.roll(buf[...], s_ref[0], 1).astype(BF16)

    grid_spec = pltpu.PrefetchScalarGridSpec(
        num_scalar_prefetch=1, grid=(rows // tr,), in_specs=[pl.BlockSpec((tr, cw), lambda i, t: (i, 0))],
        out_specs=pl.BlockSpec((tr, EXP_W), lambda i, t: (i, 0)), scratch_shapes=[pltpu.VMEM((tr, EXP_W), F32)])
    return pl.pallas_call(
        body, name="expand_w_in", grid_spec=grid_spec, out_shape=jax.ShapeDtypeStruct((rows, EXP_W), BF16),
        compiler_params=_params(("arbitrary",)),
    )(shift, w)


def _pad_wq(w):
    rows, cw = w.shape

    def body(w_ref, o_ref, buf):
        buf[...] = jnp.zeros_like(buf)
        buf[:, 0:cw] = w_ref[...]
        o_ref[...] = buf[...].astype(BF16)

    return pl.pallas_call(
        body, name="pad_wq", out_shape=jax.ShapeDtypeStruct((rows, QK_PAD), BF16),
        scratch_shapes=[pltpu.VMEM((rows, QK_PAD), F32)], compiler_params=_params(),
    )(w)


def _merge_w_in(e):
    _, rows, _ = e.shape
    tr = 256

    def body(e_ref, o_ref):
        for t in range(U_COLS // LANES):
            lo, hi = t * LANES, (t + 1) * LANES
            parts = [e_ref[d, :, lo - W_LO[d]:hi - W_LO[d]] for d in range(N_DEV)
                     if CW * d < hi and CW * (d + 1) > lo]
            if not parts:
                tile = jnp.zeros((tr, LANES), BF16)
            elif len(parts) == 1:
                tile = parts[0]
            else:
                tile = (parts[0].astype(F32) + parts[1].astype(F32)).astype(BF16)
            o_ref[:, lo:hi] = tile

    return pl.pallas_call(
        body, name="merge_w_in", grid=(rows // tr,),
        in_specs=[pl.BlockSpec((N_DEV, tr, EXP_W), lambda i: (0, i, 0))],
        out_specs=pl.BlockSpec((tr, U_COLS), lambda i: (i, 0)), out_shape=jax.ShapeDtypeStruct((rows, U_COLS), BF16),
        compiler_params=_params(("parallel",)),
    )(e)


def _sum_leading(a, out_dtype, name):
    k, rows, cols = a.shape
    tr = min(rows, 1728 if rows % 1728 == 0 else rows)

    def body(a_ref, o_ref):
        acc = a_ref[0].astype(F32)
        for i in range(1, k):
            acc = acc + a_ref[i].astype(F32)
        o_ref[...] = acc.astype(out_dtype)

    return pl.pallas_call(
        body, name=name, grid=(rows // tr,),
        in_specs=[pl.BlockSpec((k, tr, cols), lambda i: (0, i, 0))],
        out_specs=pl.BlockSpec((tr, cols), lambda i: (i, 0)),
        out_shape=jax.ShapeDtypeStruct((rows, cols), out_dtype), compiler_params=_params(("parallel",)),
    )(a)


def _add_pairs(g, recv, core, name):
    k, rows, cols = recv.shape
    tr = 1728 if rows % 1728 == 0 else rows

    def body(c_ref, g_ref, r_ref, o_ref):
        del c_ref
        o_ref[...] = (g_ref[...].astype(F32) + r_ref[...].astype(F32)).astype(o_ref.dtype)

    spec = pl.BlockSpec((1, tr, cols), lambda i, j, c: (i, j, 0))
    grid_spec = pltpu.PrefetchScalarGridSpec(
        num_scalar_prefetch=1, grid=(k, rows // tr),
        in_specs=[pl.BlockSpec((1, tr, cols), lambda i, j, c: (2 * i + c[0], j, 0)), spec], out_specs=spec)
    return pl.pallas_call(
        body, name=name, grid_spec=grid_spec, out_shape=jax.ShapeDtypeStruct(recv.shape, recv.dtype),
        compiler_params=_params(("parallel", "parallel")),
    )(core, g, recv)


def _ada_mod(c16, ada_w_l, ada_b_l):
    def body(c_ref, w_ref, b_ref, o_ref):
        cv = c_ref[...]
        sc = (cv * _sigmoid(cv)).astype(BF16)
        o_ref[...] = _dot(sc, w_ref[...].astype(BF16)) + b_ref[...]

    return pl.pallas_call(
        body, name="ada_mod", out_shape=jax.ShapeDtypeStruct((c16.shape[0], ada_w_l.shape[1]), F32),
        compiler_params=_params(),
    )(c16, ada_w_l, ada_b_l)


def _ada_w_grad(c_t, dmod_my):
    def body(c_ref, d_ref, o_ref):
        cv = c_ref[...]
        sc = cv * _sigmoid(cv)
        acc = sc[:, 0:1] * d_ref[0:1, :]
        for b in range(1, N_DEV):
            acc = acc + sc[:, b:b + 1] * d_ref[b:b + 1, :]
        o_ref[...] = acc

    return pl.pallas_call(
        body, name="ada_w_grad", out_shape=jax.ShapeDtypeStruct((c_t.shape[0], dmod_my.shape[1]), F32),
        compiler_params=_params(),
    )(c_t, dmod_my)


def _rope_tables(pos_col, invf, sign):
    s = pos_col.shape[0]
    tm = min(TM_MM, s)

    def body(p_ref, f_ref, s_ref, cos_ref, sin_ref):
        ang = p_ref[...].astype(F32) * f_ref[...]
        sg = s_ref[...]
        cos_ref[...] = jnp.cos(ang) * jnp.abs(sg)
        sin_ref[...] = jnp.sin(ang) * sg

    row = pl.BlockSpec((1, LANES), lambda i: (0, 0))
    tab = pl.BlockSpec((tm, LANES), lambda i: (i, 0))
    shape = jax.ShapeDtypeStruct((s, LANES), F32)
    return pl.pallas_call(
        body, name="rope_tables", grid=(s // tm,), in_specs=[pl.BlockSpec((tm, 1), lambda i: (i, 0)), row, row],
        out_specs=[tab, tab], out_shape=[shape, shape], compiler_params=_params(("parallel",)),
    )(pos_col, invf, sign)


def _norm_mod(x, norm_g, mod):
    s, d = x.shape
    tm = min(TM_MM, s)

    def body(x_ref, g_ref, mod_ref, h_ref, ht_ref):
        xv = x_ref[...]
        r = lax.rsqrt(jnp.mean(xv * xv, axis=-1, keepdims=True) + EPS)
        hn = xv * r * g_ref[...]
        hv = hn * (1.0 + mod_ref[:, d:2 * d]) + mod_ref[:, 0:d]
        h_ref[...] = hv.astype(BF16)
        ht_ref[...] = hv.T.astype(BF16)

    return pl.pallas_call(
        body, name="norm_mod", grid=(s // tm,),
        in_specs=[pl.BlockSpec((tm, d), lambda i: (i, 0)), pl.BlockSpec((1, d), lambda i: (0, 0)),
                  pl.BlockSpec((1, 3 * d), lambda i: (0, 0))],
        out_specs=[pl.BlockSpec((tm, d), lambda i: (i, 0)), pl.BlockSpec((d, tm), lambda i: (0, i))],
        out_shape=[jax.ShapeDtypeStruct((s, d), BF16), jax.ShapeDtypeStruct((d, s), BF16)],
        compiler_params=_params(("parallel",)),
    )(x, norm_g, mod)


def _matmul(a, b, *, nt, out_dtype, tm, tn, name, rider=None, rider_inputs=()):
    m, kdim = a.shape
    n = b.shape[0] if nt else b.shape[1]
    tm, tn = min(tm, m), min(tn, n)
    n_in = len(rider_inputs)
    n_out = len(rider.out_shape) if rider else 0
    m_steps = m // tm
    steps = (n // tn) * m_steps

    def body(a_ref, b_ref, *rest):
        r_ins, o_ref, r_outs, sems = rest[:n_in], rest[n_in], rest[n_in + 1:n_in + 1 + n_out], rest[n_in + 1 + n_out:]
        step = pl.program_id(0) * m_steps + pl.program_id(1)
        if rider:
            pl.when(step == 0)(functools.partial(rider.start, r_ins, r_outs, sems))
            pl.when(step == steps // 2)(functools.partial(rider.forward, r_ins, r_outs, sems))
        o = _dot_nt(a_ref[...], b_ref[...]) if nt else _dot(a_ref[...], b_ref[...])
        o_ref[...] = o.astype(out_dtype)
        if rider:
            pl.when(step == steps - 1)(functools.partial(rider.finish, r_ins, r_outs, sems))

    b_spec = pl.BlockSpec((tn, kdim), lambda j, i: (j, 0)) if nt else pl.BlockSpec((kdim, tn), lambda j, i: (0, j))
    out = pl.pallas_call(
        body, name=name, grid=(n // tn, m_steps),
        in_specs=[pl.BlockSpec((tm, kdim), lambda j, i: (i, 0)), b_spec] + [ANY] * n_in,
        out_specs=[pl.BlockSpec((tm, tn), lambda j, i: (i, j))] + [ANY] * n_out,
        out_shape=[jax.ShapeDtypeStruct((m, n), out_dtype)] + (rider.out_shape if rider else []),
        scratch_shapes=rider.scratch if rider else [],
        compiler_params=_params(("arbitrary", "arbitrary") if rider else ("parallel", "parallel")),
    )(a, b, *rider_inputs)
    return out if rider else out[0]


HALO = 16


def _conv_fwd(u, conv_w):
    s = u.shape[0]
    tm = min(TM_ELEM, s)
    cb = D_CONV

    def body(xc_ref, bc_ref, cc_ref, zc_ref, xp_ref, cp_ref, w_ref, y_ref):
        i = pl.program_id(0)
        uc = cc_ref[...].astype(F32) * xc_ref[...].astype(F32)
        up = cp_ref[...].astype(F32) * xp_ref[...].astype(F32)
        up = jnp.where(i == 0, 0.0, up)
        full = jnp.concatenate([up, uc], axis=0)
        u1 = pltpu.roll(full, 1, 0)[HALO:]
        u2 = pltpu.roll(full, 2, 0)[HALO:]
        w = w_ref[...]
        conv = w[0:1] * u2 + w[1:2] * u1 + w[2:3] * uc
        z = zc_ref[...].astype(F32)
        y_ref[...] = (bc_ref[...].astype(F32) * conv * (z * _sigmoid(z))).astype(BF16)

    def col(j):
        return pl.BlockSpec((tm, cb), lambda i: (i, j))

    def prev(j):
        return pl.BlockSpec((HALO, cb), lambda i: (jnp.maximum(i * (tm // HALO) - 1, 0), j))

    return pl.pallas_call(
        body, name="conv_fwd", grid=(s // tm,),
        in_specs=[col(0), col(1), col(2), col(3), prev(0), prev(2), pl.BlockSpec((3, cb), lambda i: (0, 0))],
        out_specs=pl.BlockSpec((tm, cb), lambda i: (i, 0)), out_shape=jax.ShapeDtypeStruct((s, cb), BF16),
        compiler_params=_params(("parallel",)),
    )(u, u, u, u, u, u, conv_w)


def _conv_bwd(u, dyc, conv_w):
    s = u.shape[0]
    tm = min(TM_ELEM, s)
    cb = D_CONV
    nt = s // tm

    def body(xc_ref, bc_ref, cc_ref, zc_ref, xp_ref, cp_ref, bn_ref, zn_ref, dy_ref, dyn_ref, w_ref, du_ref, dw_ref):
        i = pl.program_id(0)
        xc, cc = xc_ref[...].astype(F32), cc_ref[...].astype(F32)
        bc, z = bc_ref[...].astype(F32), zc_ref[...].astype(F32)
        uc = cc * xc
        up = jnp.where(i == 0, 0.0, cp_ref[...].astype(F32) * xp_ref[...].astype(F32))
        full = jnp.concatenate([up, uc], axis=0)
        u1 = pltpu.roll(full, 1, 0)[HALO:]
        u2 = pltpu.roll(full, 2, 0)[HALO:]
        w = w_ref[...]
        conv = w[0:1] * u2 + w[1:2] * u1 + w[2:3] * uc
        sg = _sigmoid(z)
        sz = z * sg
        dy = dy_ref[...].astype(F32)
        dconv = dy * bc * sz
        zn = zn_ref[...].astype(F32)
        dnext = dyn_ref[...].astype(F32) * bn_ref[...].astype(F32) * (zn * _sigmoid(zn))
        dnext = jnp.where(i == nt - 1, 0.0, dnext)
        fullb = jnp.concatenate([dconv, dnext], axis=0)
        nb = tm + HALO
        d1 = pltpu.roll(fullb, nb - 1, 0)[:tm]
        d2 = pltpu.roll(fullb, nb - 2, 0)[:tm]
        duc = w[2:3] * dconv + w[1:2] * d1 + w[0:1] * d2
        dzc = dy * bc * conv * (sg * (1.0 + z * (1.0 - sg)))
        du_ref[...] = jnp.concatenate([duc * cc, dy * conv * sz, duc * xc, dzc], axis=1).astype(BF16)
        dw = jnp.concatenate([jnp.sum(dconv * u2, axis=0, keepdims=True), jnp.sum(dconv * u1, axis=0, keepdims=True),
                              jnp.sum(dconv * uc, axis=0, keepdims=True), jnp.zeros((5, cb), F32)], axis=0)

        @pl.when(i == 0)
        def _():
            dw_ref[...] = dw

        @pl.when(i > 0)
        def _():
            dw_ref[...] += dw

    def col(j):
        return pl.BlockSpec((tm, cb), lambda i: (i, j))

    def prev(j):
        return pl.BlockSpec((HALO, cb), lambda i: (jnp.maximum(i * (tm // HALO) - 1, 0), j))

    def nxt(j):
        return pl.BlockSpec((HALO, cb), lambda i: (jnp.minimum((i + 1) * (tm // HALO), s // HALO - 1), j))

    return pl.pallas_call(
        body, name="conv_bwd", grid=(nt,),
        in_specs=[col(0), col(1), col(2), col(3), prev(0), prev(2), nxt(1), nxt(3), col(0), nxt(0),
                  pl.BlockSpec((3, cb), lambda i: (0, 0))],
        out_specs=[pl.BlockSpec((tm, 4 * cb), lambda i: (i, 0)), pl.BlockSpec((8, cb), lambda i: (0, 0))],
        out_shape=[jax.ShapeDtypeStruct((s, U_COLS), BF16), jax.ShapeDtypeStruct((8, cb), F32)],
        compiler_params=_params(("arbitrary",)),
    )(u, u, u, u, u, u, u, u, dyc, dyc, conv_w)


def _qkv_specs(tm):
    return [pl.BlockSpec((tm, Q_LORA), lambda i: (i, U_CQ // Q_LORA)),
            pl.BlockSpec((tm, KV_LORA), lambda i: (i, U_CKV // KV_LORA)),
            pl.BlockSpec((tm, LANES), lambda i: (i, U_KR // LANES)),
            pl.BlockSpec((tm, LANES), lambda i: (i, 0)), pl.BlockSpec((tm, LANES), lambda i: (i, 0))]


def _full(shape):
    return pl.BlockSpec(shape, lambda i: (0,) * len(shape))


def _k_rope_lanes(blk):
    lane = lax.broadcasted_iota(jnp.int32, blk.shape, 1)
    return jnp.where(lane < QK_ROPE, blk, 0.0)


def _qkv_fwd(u, cos, sin, wq, wkv, qag, kvag, qg, kg):
    s = u.shape[0]
    tm = min(TM_ELEM, s)

    def body(cq_ref, ckv_ref, kr_ref, cos_ref, sin_ref, wq_ref, wkv_ref, qag_ref, kvag_ref, qg_ref, kg_ref,
             q_ref, k_ref, v_ref):
        cq = cq_ref[...].astype(F32)
        cqn = (cq * lax.rsqrt(jnp.mean(cq * cq, axis=-1, keepdims=True) + EPS) * qag_ref[...]).astype(BF16)
        ckv = ckv_ref[...].astype(F32)
        ckvn = (ckv * lax.rsqrt(jnp.mean(ckv * ckv, axis=-1, keepdims=True) + EPS) * kvag_ref[...]).astype(BF16)
        kr = _k_rope_lanes(kr_ref[...].astype(F32))
        cosv, sinv, qgv, kgv = cos_ref[...], sin_ref[...], qg_ref[...], kg_ref[...]
        ss_r = jnp.sum(kr * kr, axis=-1, keepdims=True)
        krr = _rope(kr * kgv[:, QK_NOPE:], cosv, sinv)
        qf = _dot(cqn, wq_ref[...])
        kvf = _dot(ckvn, wkv_ref[...])
        for h in range(N_HEADS):
            qh = qf[:, QK_PAD * h:QK_PAD * (h + 1)]
            rq = lax.rsqrt(jnp.sum(qh * qh, axis=-1, keepdims=True) * (1.0 / QK_HEAD) + EPS)
            qn = qh * rq * qgv
            qo = jnp.concatenate([qn[:, :QK_NOPE], _rope(qn[:, QK_NOPE:], cosv, sinv)], axis=1) * (SCALE * LOG2E)
            q_ref[h] = qo.astype(BF16)
            kn = kvf[:, 2 * V_HEAD * h:2 * V_HEAD * h + QK_NOPE]
            vh = kvf[:, 2 * V_HEAD * h + QK_NOPE:2 * V_HEAD * (h + 1)]
            rk = lax.rsqrt((jnp.sum(kn * kn, axis=-1, keepdims=True) + ss_r) * (1.0 / QK_HEAD) + EPS)
            k_ref[h] = jnp.concatenate([kn * kgv[:, :QK_NOPE] * rk, krr * rk], axis=1).astype(BF16)
            v_ref[h] = jnp.concatenate([vh, jnp.ones_like(vh)], axis=1).astype(BF16)

    return pl.pallas_call(
        body, name="qkv_fwd", grid=(s // tm,),
        in_specs=_qkv_specs(tm) + [_full((Q_LORA, N_HEADS * QK_PAD)), _full((KV_LORA, 2 * D_ATTN)),
                                   _full((1, Q_LORA)), _full((1, KV_LORA)), _full((1, QK_PAD)), _full((1, QK_PAD))],
        out_specs=[pl.BlockSpec((N_HEADS, tm, QK_PAD), lambda i: (0, i, 0)),
                   pl.BlockSpec((N_HEADS, tm, QK_PAD), lambda i: (0, i, 0)),
                   pl.BlockSpec((N_HEADS, tm, 2 * V_HEAD), lambda i: (0, i, 0))],
        out_shape=[jax.ShapeDtypeStruct((N_HEADS, s, QK_PAD), BF16), jax.ShapeDtypeStruct((N_HEADS, s, QK_PAD), BF16),
                   jax.ShapeDtypeStruct((N_HEADS, s, 2 * V_HEAD), BF16)],
        compiler_params=_params(("parallel",)),
    )(u, u, u, cos, sin, wq, wkv, qag, kvag, qg, kg)


def _qkv_bwd(u, cos, sin, dq, dk, dv, dza, wq, wkv, qag, kvag, qg, kg, du):
    s = u.shape[0]
    tm = min(TM_ELEM, s)
    nt = s // tm

    def body(cq_ref, ckv_ref, kr_ref, cos_ref, sin_ref, dq_ref, dk_ref, dv_ref, dza_ref, wq_ref, wkv_ref, qag_ref,
             kvag_ref, qg_ref, kg_ref, du_in, du_ref, dwq_ref, dwkv_ref, dqag_ref, dkvag_ref, dqg_ref, dkg_ref,
             dwq_acc, dwkv_acc):
        del du_in
        i = pl.program_id(0)

        @pl.when(i == 0)
        def _():
            dwq_acc[...] = jnp.zeros_like(dwq_acc)
            dwkv_acc[...] = jnp.zeros_like(dwkv_acc)

        cq = cq_ref[...].astype(F32)
        rqa = lax.rsqrt(jnp.mean(cq * cq, axis=-1, keepdims=True) + EPS)
        xq = cq * rqa
        qagv = qag_ref[...]
        cqn = (xq * qagv).astype(BF16)
        ckv = ckv_ref[...].astype(F32)
        rkva = lax.rsqrt(jnp.mean(ckv * ckv, axis=-1, keepdims=True) + EPS)
        xkv = ckv * rkva
        kvagv = kvag_ref[...]
        ckvn = (xkv * kvagv).astype(BF16)
        kr = _k_rope_lanes(kr_ref[...].astype(F32))
        cosv, sinv, qgv, kgv = cos_ref[...], sin_ref[...], qg_ref[...], kg_ref[...]
        ss_r = jnp.sum(kr * kr, axis=-1, keepdims=True)
        dqg = jnp.zeros((1, QK_PAD), F32)
        dkg = jnp.zeros((1, QK_PAD), F32)
        dkr = jnp.zeros((tm, LANES), F32)
        qf = _dot(cqn, wq_ref[...])
        kvf = _dot(ckvn, wkv_ref[...])
        dqf, dkvf = [], []
        for h in range(N_HEADS):
            qh = qf[:, QK_PAD * h:QK_PAD * (h + 1)]
            rq = lax.rsqrt(jnp.sum(qh * qh, axis=-1, keepdims=True) * (1.0 / QK_HEAD) + EPS)
            xh = qh * rq
            g = dq_ref[h].astype(F32) * SCALE
            dyq = jnp.concatenate([g[:, :QK_NOPE], _rope_t(g[:, QK_NOPE:], cosv, sinv)], axis=1)
            dqg = dqg + jnp.sum(dyq * xh, axis=0, keepdims=True)
            gdy = dyq * qgv
            dqf.append((rq * (gdy - xh * (jnp.sum(gdy * xh, axis=-1, keepdims=True) * (1.0 / QK_HEAD)))).astype(BF16))
            kn = kvf[:, 2 * V_HEAD * h:2 * V_HEAD * h + QK_NOPE]
            rk = lax.rsqrt((jnp.sum(kn * kn, axis=-1, keepdims=True) + ss_r) * (1.0 / QK_HEAD) + EPS)
            xk = jnp.concatenate([kn, kr], axis=1) * rk
            gk = dk_ref[h].astype(F32)
            dyk = jnp.concatenate([gk[:, :QK_NOPE], _rope_t(gk[:, QK_NOPE:], cosv, sinv)], axis=1)
            dkg = dkg + jnp.sum(dyk * xk, axis=0, keepdims=True)
            gdyk = dyk * kgv
            dxk = rk * (gdyk - xk * (jnp.sum(gdyk * xk, axis=-1, keepdims=True) * (1.0 / QK_HEAD)))
            dkr = dkr + dxk[:, QK_NOPE:]
            dkvf += [dxk[:, :QK_NOPE].astype(BF16), dv_ref[h]]
        dqf_b, dkvf_b = jnp.concatenate(dqf, axis=1), jnp.concatenate(dkvf, axis=1)
        dwq_acc[...] += _dot_tn(cqn, dqf_b)
        dwkv_acc[...] += _dot_tn(ckvn, dkvf_b)
        dcqn = _dot_nt(dqf_b, wq_ref[...])
        dckvn = _dot_nt(dkvf_b, wkv_ref[...])
        dqag = jnp.sum(dcqn * xq, axis=0, keepdims=True)
        dkvag = jnp.sum(dckvn * xkv, axis=0, keepdims=True)
        gq = dcqn * qagv
        dcq = rqa * (gq - xq * jnp.mean(gq * xq, axis=-1, keepdims=True))
        gkv = dckvn * kvagv
        dckv = rkva * (gkv - xkv * jnp.mean(gkv * xkv, axis=-1, keepdims=True))
        win = pltpu.roll(jnp.concatenate([dza_ref[...].astype(F32), jnp.zeros((tm, LANES), F32)], axis=1), QK_ROPE, 1)
        win = win + jnp.concatenate([dkr, jnp.zeros((tm, D_ATTN), F32)], axis=1)
        du_ref[...] = jnp.concatenate([dcq, dckv, win, jnp.zeros((tm, U_TAIL - ZA_LO - ZA_WIN), F32)], axis=1).astype(BF16)

        @pl.when(i == 0)
        def _():
            dqag_ref[...] = dqag
            dkvag_ref[...] = dkvag
            dqg_ref[...] = dqg
            dkg_ref[...] = dkg

        @pl.when(i > 0)
        def _():
            dqag_ref[...] += dqag
            dkvag_ref[...] += dkvag
            dqg_ref[...] += dqg
            dkg_ref[...] += dkg

        @pl.when(i == nt - 1)
        def _():
            dwq_ref[...] = dwq_acc[...].astype(BF16)
            dwkv_ref[...] = dwkv_acc[...].astype(BF16)

    head = lambda w: pl.BlockSpec((N_HEADS, tm, w), lambda i: (0, i, 0))
    wq_shape, wkv_shape = (Q_LORA, N_HEADS * QK_PAD), (KV_LORA, 2 * D_ATTN)
    return pl.pallas_call(
        body, name="qkv_bwd", grid=(nt,),
        in_specs=_qkv_specs(tm) + [head(QK_PAD), head(QK_PAD), head(V_HEAD), pl.BlockSpec((tm, D_ATTN), lambda i: (i, 0)),
                                   _full(wq_shape), _full(wkv_shape), _full((1, Q_LORA)), _full((1, KV_LORA)),
                                   _full((1, QK_PAD)), _full((1, QK_PAD)), ANY],
        out_specs=[pl.BlockSpec((tm, U_TAIL), lambda i: (i, U_COLS // U_TAIL - 1)), _full(wq_shape), _full(wkv_shape),
                   _full((1, Q_LORA)), _full((1, KV_LORA)), _full((1, QK_PAD)), _full((1, QK_PAD))],
        out_shape=[jax.ShapeDtypeStruct(du.shape, du.dtype), jax.ShapeDtypeStruct(wq_shape, BF16),
                   jax.ShapeDtypeStruct(wkv_shape, BF16), jax.ShapeDtypeStruct((1, Q_LORA), F32),
                   jax.ShapeDtypeStruct((1, KV_LORA), F32), jax.ShapeDtypeStruct((1, QK_PAD), F32),
                   jax.ShapeDtypeStruct((1, QK_PAD), F32)],
        scratch_shapes=[pltpu.VMEM(wq_shape, F32), pltpu.VMEM(wkv_shape, F32)],
        input_output_aliases={15: 0}, compiler_params=_params(("arbitrary",)),
    )(u, u, u, cos, sin, dq, dk, dv, dza, wq, wkv, qag, kvag, qg, kg, du)


def _flash_fwd(q, k, v):
    nh, s, _ = q.shape
    tq = min(TQ, s)
    tk = tq // 2
    nq = s // tq

    def body(q_ref, k_ref, v_ref, o_ref, lse_ref):
        i = pl.program_id(1)
        halves = (q_ref[0, :tk, :], q_ref[0, tk:, :])

        def unit(qh, j, carry, diag=False):
            m, acc = carry
            rows = pl.ds(pl.multiple_of(j * tk, tk), tk)
            sc = _dot_nt(qh, k_ref[0, rows, :])
            if diag:
                qi = lax.broadcasted_iota(jnp.int32, sc.shape, 0)
                ki = lax.broadcasted_iota(jnp.int32, sc.shape, 1)
                sc = jnp.where(ki <= qi, sc, NEG)
            m_new = jnp.maximum(m, jnp.max(sc, axis=-1, keepdims=True))
            p = jnp.exp2(sc - m_new)
            acc = jnp.exp2(m - m_new) * acc + _dot(p.astype(BF16), v_ref[0, rows, :])
            return m_new, acc

        def trip(p, carry):
            c0, c1 = carry
            c0, c1 = unit(halves[0], 2 * p, c0), unit(halves[1], 2 * p, c1)
            return unit(halves[0], 2 * p + 1, c0), unit(halves[1], 2 * p + 1, c1)

        init = (jnp.full((tk, 1), NEG, F32), jnp.zeros((tk, 2 * V_HEAD), F32))
        c0, c1 = lax.fori_loop(0, i, trip, (init, init))
        c0 = unit(halves[0], 2 * i, c0, diag=True)
        c1 = unit(halves[1], 2 * i + 1, unit(halves[1], 2 * i, c1), diag=True)
        for half, (m, acc) in enumerate((c0, c1)):
            l = acc[:, V_HEAD:]
            o_ref[half * tk:(half + 1) * tk, :] = (acc[:, :V_HEAD] / l).astype(BF16)
            lse = m + jnp.log(l[:, 0:1]) * LOG2E
            lse_ref[0, :, half * tk:(half + 1) * tk] = jnp.broadcast_to(lse, (tk, LANES)).T[0:1, :]

    return pl.pallas_call(
        body, name="flash_fwd", grid=(nh, nq),
        in_specs=[pl.BlockSpec((1, tq, QK_PAD), lambda h, i: (h, i, 0)),
                  pl.BlockSpec((1, s, QK_PAD), lambda h, i: (h, 0, 0)),
                  pl.BlockSpec((1, s, 2 * V_HEAD), lambda h, i: (h, 0, 0))],
        out_specs=[pl.BlockSpec((tq, V_HEAD), lambda h, i: (i, h)), pl.BlockSpec((1, 1, tq), lambda h, i: (h, 0, i))],
        out_shape=[jax.ShapeDtypeStruct((s, nh * V_HEAD), BF16), jax.ShapeDtypeStruct((nh, 1, s), F32)],
        compiler_params=_params(("parallel", "arbitrary")),
    )(q, k, v)


def _flash_bwd(q, k, v, do, lse, delta):
    nh, s, _ = q.shape
    tq = min(TQ, s)
    nq = s // tq

    def body(q_ref, k_ref, v_ref, do_ref, lse_ref, dl_ref, dq_ref, dk_ref, dv_ref, dq_acc):
        j = pl.program_id(1)

        @pl.when(j == 0)
        def _():
            dq_acc[...] = jnp.zeros_like(dq_acc)

        kj, vj = k_ref[0], v_ref[0]

        def block(kk, vv, qq, dd, lse, dl, masked):
            st = _dot_nt(kk, qq)
            pt = jnp.exp2(st - lse)
            if masked:
                ki = lax.broadcasted_iota(jnp.int32, st.shape, 0)
                qx = lax.broadcasted_iota(jnp.int32, st.shape, 1)
                pt = jnp.where(ki <= qx, pt, 0.0)
            ddv = _dot(pt.astype(BF16), dd)
            dst = (pt * (_dot_nt(vv, dd) - dl)).astype(BF16)
            ddk = _dot(dst, qq)
            return ddk, ddv, _dot_tn(dst, kk)

        def step(i, carry):
            dk, dv = carry
            rows = pl.ds(pl.multiple_of(i * tq, tq), tq)
            ddk, ddv, ddq = block(kj, vj, q_ref[0, rows, :], do_ref[rows, :], lse_ref[0, pl.ds(i, 1), :],
                                  dl_ref[0, pl.ds(i, 1), :], False)
            dq_acc[rows, :] += ddq
            return dk + ddk, dv + ddv

        th = tq // 2
        lse_j, dl_j = lse_ref[0, pl.ds(j, 1), :], dl_ref[0, pl.ds(j, 1), :]
        parts = []
        for kh, qh, masked in ((0, 0, True), (0, 1, False), (1, 1, True)):
            rows = pl.ds(pl.multiple_of(j * tq + qh * th, th), th)
            ks, qs = slice(kh * th, (kh + 1) * th), slice(qh * th, (qh + 1) * th)
            ddk, ddv, ddq = block(kj[ks], vj[ks], q_ref[0, rows, :], do_ref[rows, :], lse_j[:, qs], dl_j[:, qs], masked)
            dq_acc[rows, :] += ddq
            parts.append((ddk, ddv))
        carry = (jnp.concatenate([parts[0][0] + parts[1][0], parts[2][0]], axis=0),
                 jnp.concatenate([parts[0][1] + parts[1][1], parts[2][1]], axis=0))
        dk, dv = lax.fori_loop(j + 1, nq, step, carry)
        dk_ref[0] = (dk * LN2).astype(BF16)
        dv_ref[0] = dv.astype(BF16)

        @pl.when(j == nq - 1)
        def _():
            dq_ref[0] = dq_acc[...].astype(BF16)

    return pl.pallas_call(
        body, name="flash_bwd", grid=(nh, nq),
        in_specs=[pl.BlockSpec((1, s, QK_PAD), lambda h, j: (h, 0, 0)),
                  pl.BlockSpec((1, tq, QK_PAD), lambda h, j: (h, j, 0)),
                  pl.BlockSpec((1, tq, V_HEAD), lambda h, j: (h, j, 0)),
                  pl.BlockSpec((s, V_HEAD), lambda h, j: (0, h)),
                  pl.BlockSpec((1, nq, tq), lambda h, j: (h, 0, 0)),
                  pl.BlockSpec((1, nq, tq), lambda h, j: (h, 0, 0))],
        out_specs=[pl.BlockSpec((1, s, QK_PAD), lambda h, j: (h, 0, 0)),
                   pl.BlockSpec((1, tq, QK_PAD), lambda h, j: (h, j, 0)),
                   pl.BlockSpec((1, tq, V_HEAD), lambda h, j: (h, j, 0))],
        out_shape=[jax.ShapeDtypeStruct((nh, s, QK_PAD), BF16), jax.ShapeDtypeStruct((nh, s, QK_PAD), BF16),
                   jax.ShapeDtypeStruct((nh, s, V_HEAD), BF16)],
        scratch_shapes=[pltpu.VMEM((s, QK_PAD), F32)],
        compiler_params=_params(("parallel", "arbitrary")),
    )(q, k, v, do, lse, delta)


def _tail(x, target, yc, o, u, mod, w_out):
    s, d = x.shape
    tm = min(TM_ELEM, s)

    def body(x_ref, t_ref, yc_ref, o_ref, za_ref, mod_ref, w_ref,
             gx_ref, dy_ref, ycat_ref, dyc_ref, do_ref, du_ref, delta_ref, dgate_ref, loss_ref):
        i = pl.program_id(0)
        za = pltpu.roll(za_ref[:, ZA_LO:ZA_LO + ZA_WIN].astype(F32), ZA_WIN - QK_ROPE, 1)[:, :D_ATTN]
        ov = o_ref[...].astype(F32)
        sg = _sigmoid(za)
        sl = za * sg
        ya = ov * sl
        ycat = jnp.concatenate([yc_ref[...], ya.astype(BF16)], axis=1)
        ycat_ref[...] = jnp.concatenate([yc_ref[...].astype(F32).T, ya.T], axis=0).astype(BF16)
        y = _dot(ycat, w_ref[...])
        gate = mod_ref[:, 2 * d:3 * d]
        e = x_ref[...] + gate * y - t_ref[...]
        dout = e * (1.0 / d)
        gx_ref[...] = dout
        dy = (dout * gate).astype(BF16)
        dy_ref[...] = dy
        dycat = _dot_nt(dy, w_ref[...])
        dyc_ref[...] = dycat[:, :D_CONV].astype(BF16)
        dya = dycat[:, D_CONV:]
        dov = dya * sl
        do_ref[...] = dov.astype(BF16)
        du_ref[...] = (dya * ov * (sg * (1.0 + za * (1.0 - sg)))).astype(BF16)
        prod_t = (dov * ov).T
        for h in range(N_HEADS):
            delta_ref[h] = jnp.sum(prod_t[V_HEAD * h:V_HEAD * (h + 1), :], axis=0, keepdims=True)
        dgate = jnp.sum(dout * y, axis=0, keepdims=True)
        part = jnp.sum(jnp.sum(e * e, axis=0, keepdims=True), axis=1, keepdims=True) * (0.5 / d)
        part = jnp.broadcast_to(part, (1, LANES))

        @pl.when(i == 0)
        def _():
            dgate_ref[...] = dgate
            loss_ref[...] = part

        @pl.when(i > 0)
        def _():
            dgate_ref[...] += dgate
            loss_ref[...] += part

    tok = lambda w: pl.BlockSpec((tm, w), lambda i: (i, 0))
    return pl.pallas_call(
        body, name="tail", grid=(s // tm,),
        in_specs=[tok(d), tok(d), tok(D_CONV), tok(D_ATTN), pl.BlockSpec((tm, U_TAIL), lambda i: (i, U_COLS // U_TAIL - 1)),
                  _full((1, 3 * d)), _full((d, d))],
        out_specs=[tok(d), tok(d), pl.BlockSpec((d, tm), lambda i: (0, i)), tok(D_CONV), tok(D_ATTN), tok(D_ATTN),
                   pl.BlockSpec((N_HEADS, 1, tm), lambda i: (0, 0, i)), _full((1, d)), _full((1, LANES))],
        out_shape=[jax.ShapeDtypeStruct((s, d), F32), jax.ShapeDtypeStruct((s, d), BF16),
                   jax.ShapeDtypeStruct((d, s), BF16), jax.ShapeDtypeStruct((s, D_CONV), BF16),
                   jax.ShapeDtypeStruct((s, D_ATTN), BF16), jax.ShapeDtypeStruct((s, D_ATTN), BF16),
                   jax.ShapeDtypeStruct((N_HEADS, 1, s), F32), jax.ShapeDtypeStruct((1, d), F32),
                   jax.ShapeDtypeStruct((1, LANES), F32)],
        compiler_params=_params(("arbitrary",)),
    )(x, target, yc, o, u, mod, w_out)


def _norm_bwd(x, dh, gx1, norm_g, mod):
    s, d = x.shape
    tm = min(TM_ELEM, s)

    def body(x_ref, dh_ref, gx_ref, g_ref, mod_ref, o_ref, dshift_ref, dscale_ref, dg_ref):
        i = pl.program_id(0)
        xv, dhv, gv = x_ref[...], dh_ref[...].astype(F32), g_ref[...]
        r = lax.rsqrt(jnp.mean(xv * xv, axis=-1, keepdims=True) + EPS)
        xn = xv * r
        dhn = dhv * (1.0 + mod_ref[:, d:2 * d])
        dxn = dhn * gv
        o_ref[...] = gx_ref[...] + r * (dxn - xn * jnp.mean(dxn * xn, axis=-1, keepdims=True))
        dshift = jnp.sum(dhv, axis=0, keepdims=True)
        dscale = jnp.sum(dhv * xn * gv, axis=0, keepdims=True)
        dg = jnp.sum(dhn * xn, axis=0, keepdims=True)

        @pl.when(i == 0)
        def _():
            dshift_ref[...] = dshift
            dscale_ref[...] = dscale
            dg_ref[...] = dg

        @pl.when(i > 0)
        def _():
            dshift_ref[...] += dshift
            dscale_ref[...] += dscale
            dg_ref[...] += dg

    tok = pl.BlockSpec((tm, d), lambda i: (i, 0))
    row = jax.ShapeDtypeStruct((1, d), F32)
    return pl.pallas_call(
        body, name="norm_bwd", grid=(s // tm,),
        in_specs=[tok, tok, tok, _full((1, d)), _full((1, 3 * d))],
        out_specs=[tok, _full((1, d)), _full((1, d)), _full((1, d))],
        out_shape=[jax.ShapeDtypeStruct((s, d), F32), row, row, row],
        compiler_params=_params(("arbitrary",)),
    )(x, dh, gx1, norm_g, mod)


def _adamw(w, g, m, v, name):
    rows, cols = w.shape
    tr = 256 if rows % 256 == 0 else rows

    def body(w_ref, g_ref, m_ref, v_ref, d_ref, nm_ref, nv_ref):
        gv = g_ref[...]
        nm = ADAM_B1 * m_ref[...] + (1.0 - ADAM_B1) * gv
        nv = ADAM_B2 * v_ref[...] + (1.0 - ADAM_B2) * (gv * gv)
        m_hat = nm / (1.0 - ADAM_B1 ** ADAM_STEP)
        v_hat = nv / (1.0 - ADAM_B2 ** ADAM_STEP)
        d_ref[...] = -ADAM_LR * (m_hat / (jnp.sqrt(v_hat) + ADAM_EPS) + ADAM_WD * w_ref[...])
        nm_ref[...] = nm
        nv_ref[...] = nv

    spec = pl.BlockSpec((tr, cols), lambda i: (i, 0))
    shape = jax.ShapeDtypeStruct((rows, cols), F32)
    return pl.pallas_call(
        body, name=name, grid=(rows // tr,), in_specs=[spec] * 4, out_specs=[spec] * 3, out_shape=[shape] * 3,
        compiler_params=_params(("parallel",)),
    )(w, g, m, v)


def _pad_cols(a, n):
    return jnp.pad(a, ((0, 0), (0, n - a.shape[1])))


def kernel(x, c, positions, ada_w, ada_b, norm_g, w_in, conv_w, q_a_g, w_q_b, kv_a_g, w_kv_b, q_g, k_g, w_out, loss_target, m_ada_w, m_ada_b, m_norm_g, m_w_in, m_conv_w, m_q_a_g, m_w_q_b, m_kv_a_g, m_w_kv_b, m_q_g, m_k_g, m_w_out, v_ada_w, v_ada_b, v_norm_g, v_w_in, v_conv_w, v_q_a_g, v_w_q_b, v_kv_a_g, v_w_kv_b, v_q_g, v_k_g, v_w_out):
    me = _my_index()
    s = x.shape[1]
    nq = s // min(TQ, s)
    x2, tgt = x[0], loss_target[0]
    w_in_l, w_q_l, w_kv_l, w_out_l, conv_l, ada_w_l = w_in[0], w_q_b[0], w_kv_b[0], w_out[0], conv_w[0], ada_w[0]
    ada_cols = ada_w_l.shape[1]

    my_off = ((CW * me) % LANES).astype(jnp.int32)
    small =jnp.concatenate([c.reshape(-1, LANES), conv_l.reshape(-1, LANES), jnp.zeros((5, LANES), F32)], axis=0)
    win_g, small_g = _all_gather([_expand_w_in(w_in_l, my_off.reshape(1)), small], "gather_w_in")
    w_in_p = _merge_w_in(win_g)
    c_all = small_g[:, :D_MODEL // LANES].reshape(N_DEV, D_MODEL)
    conv_g = small_g[:, D_MODEL // LANES:D_MODEL // LANES + 3].transpose(1, 0, 2).reshape(3, D_CONV)

    ada_b_l = lax.dynamic_slice(ada_b, (0, me * ada_cols), (1, ada_cols))
    mod_cols = _ada_mod(jnp.pad(c_all, ((0, 8), (0, 0))), ada_w_l, ada_b_l)[:N_DEV]
    (mod_g,) = _all_gather([mod_cols], "gather_mod")
    mod = lax.dynamic_index_in_dim(mod_g, me, axis=1, keepdims=False).reshape(1, 3 * D_MODEL)

    half = jnp.arange(0, QK_ROPE, 2, dtype=F32) / QK_ROPE
    inv_freq = ROPE_BASE ** (-half)
    zeros64 = jnp.zeros((LANES - QK_ROPE,), F32)
    invf = jnp.concatenate([inv_freq, inv_freq, zeros64]).reshape(1, LANES)
    sign = jnp.concatenate([-jnp.ones((32,), F32), jnp.ones((32,), F32), zeros64]).reshape(1, LANES)
    cos, sin = _rope_tables(positions.reshape(s, 1), invf, sign)
    qg_p, kg_p = _pad_cols(q_g, QK_PAD), _pad_cols(k_g, QK_PAD)

    h, h_t = _norm_mod(x2, norm_g, mod)
    rest = [_pad_wq(w_q_l), w_kv_l.astype(BF16), w_out_l.astype(BF16)]
    u, wq_g, wkv_g, w_out_g = _matmul(h, w_in_p, nt=False, out_dtype=BF16, tm=2 * TM_MM, tn=1024, name="in_proj",
                                      rider=_Gather(rest), rider_inputs=rest)
    w_out_g = w_out_g.reshape(D_MODEL, D_MODEL)
    wq_g = wq_g.transpose(1, 0, 2).reshape(Q_LORA, N_HEADS * QK_PAD)
    wkv_g = wkv_g.transpose(1, 0, 2).reshape(KV_LORA, 2 * D_ATTN)
    yc = _conv_fwd(u, conv_g)
    q, k, v = _qkv_fwd(u, cos, sin, wq_g, wkv_g, q_a_g, kv_a_g, qg_p, kg_p)
    o, lse = _flash_fwd(q, k, v)
    gx1, dy, ycat_t, dyc, do, dza, delta, dgate, loss_row = _tail(x2, tgt, yc, o, u, mod, w_out_g)

    dq, dk, dv = _flash_bwd(q, k, v, do, lse.reshape(N_HEADS, nq, s // nq), delta.reshape(N_HEADS, nq, s // nq))
    du, dconv = _conv_bwd(u, dyc, conv_g)
    du, dwq, dwkv, dqag, dkvag, dqg, dkg = _qkv_bwd(u, cos, sin, dq, dk, dv, dza, wq_g, wkv_g, q_a_g, kv_a_g, qg_p, kg_p, du)
    dwq = dwq.reshape(Q_LORA, N_HEADS, QK_PAD).transpose(1, 0, 2)
    dwkv = dwkv.reshape(KV_LORA, N_HEADS, 2 * V_HEAD).transpose(1, 0, 2)
    dw_out = _matmul(ycat_t, dy, nt=False, out_dtype=BF16, tm=TM_MM, tn=512, name="dw_out")
    dw_out = dw_out.reshape(N_DEV, D_MODEL // N_DEV, D_MODEL)
    dw_in = _matmul(h_t, du, nt=False, out_dtype=BF16, tm=TM_MM, tn=512, name="dw_in")

    core = lax.axis_index("c").astype(jnp.int32)
    r_in, r_q, r_kv, r_out = _rs_sibling(dw_in, dwq, dwkv, dw_out)
    lo_tiles = ((CW * (2 * jnp.arange(4, dtype=jnp.int32) + core)) // LANES).astype(jnp.int32)
    pairs = [_add_window(dw_in, r_in, lo_tiles), _add_pairs(dwq, r_q, core.reshape(1), "rs_add_q"),
             _add_pairs(dwkv, r_kv, core.reshape(1), "rs_add_kv"), _add_pairs(dw_out, r_out, core.reshape(1), "rs_add_out")]
    dh, *quads = _matmul(du, w_in_p, nt=True, out_dtype=BF16, tm=2 * TM_MM, tn=512, name="dh",
                         rider=_ChipExchange(pairs), rider_inputs=pairs)
    my_chip = 2 * lax.axis_index("x") + lax.axis_index("y")
    written = jnp.where(jnp.arange(4) == my_chip, (jnp.arange(4) + 1) % 4, jnp.arange(4))
    sel = jnp.concatenate([my_chip.reshape(1), written, ((EXP_W - my_off) % EXP_W).reshape(1)]).astype(jnp.int32)
    g_w_in = _final_sum(pairs[0], quads[0], sel, "rs_sum_in", unshift_to=CW)
    g_w_q = _final_sum(pairs[1], quads[1], sel, "rs_sum_q")[:, :QK_HEAD]
    g_w_kv = _final_sum(pairs[2], quads[2], sel, "rs_sum_kv")
    g_w_out = _final_sum(pairs[3], quads[3], sel, "rs_sum_out")
    grad_x, dshift, dscale, dng = _norm_bwd(x2, dh, gx1, norm_g, mod)

    row = jnp.concatenate([dshift, dscale, dgate, dng, dqag, dkvag, dqg, dkg, dconv[:3].reshape(1, 3 * D_CONV), loss_row], axis=1)
    (rows_g,) = _all_gather([row], "gather_small")
    tot = _sum_leading(rows_g, F32, "sum_small")
    dmod_all = rows_g[:, 0, SM_MOD:SM_NG]
    g_ada_b = tot[:, SM_MOD:SM_NG]
    g_norm_g = tot[:, SM_NG:SM_QAG]
    g_q_a_g = tot[:, SM_QAG:SM_KVAG]
    g_kv_a_g = tot[:, SM_KVAG:SM_QG]
    g_q_g = tot[:, SM_QG:SM_QG + QK_HEAD]
    g_k_g = tot[:, SM_KG:SM_KG + QK_HEAD]
    conv_cols = conv_l.shape[1]
    g_conv = lax.dynamic_slice(tot[:, SM_CONV:SM_LOSS].reshape(3, D_CONV), (0, me * conv_cols), (3, conv_cols))
    loss = tot[0, SM_LOSS]
    dmod_my = lax.dynamic_slice(dmod_all, (0, me * ada_cols), (N_DEV, ada_cols))
    g_ada_w = _ada_w_grad(c_all.T, dmod_my)

    grads = dict(ada_w=g_ada_w, ada_b=g_ada_b, norm_g=g_norm_g, w_in=g_w_in, conv_w=g_conv, q_a_g=g_q_a_g, w_q_b=g_w_q,
                 kv_a_g=g_kv_a_g, w_kv_b=g_w_kv, q_g=g_q_g, k_g=g_k_g, w_out=g_w_out)
    weights = dict(ada_w=(ada_w, m_ada_w, v_ada_w), ada_b=(ada_b, m_ada_b, v_ada_b), norm_g=(norm_g, m_norm_g, v_norm_g),
                   w_in=(w_in, m_w_in, v_w_in), conv_w=(conv_w, m_conv_w, v_conv_w), q_a_g=(q_a_g, m_q_a_g, v_q_a_g),
                   w_q_b=(w_q_b, m_w_q_b, v_w_q_b), kv_a_g=(kv_a_g, m_kv_a_g, v_kv_a_g), w_kv_b=(w_kv_b, m_w_kv_b, v_w_kv_b),
                   q_g=(q_g, m_q_g, v_q_g), k_g=(k_g, m_k_g, v_k_g), w_out=(w_out, m_w_out, v_w_out))
    names = list(grads)
    out_g, out_d, out_m, out_v = [], [], [], []
    for n in names:
        w, m, v_ = weights[n]
        shape2 = w.shape[-2:] if w.ndim == 3 else (1, w.shape[-1])
        g2 = grads[n].reshape(shape2)
        d2, m2, v2 = _adamw(w.reshape(shape2), g2, m.reshape(shape2), v_.reshape(shape2), "adamw_" + n)
        out_g.append(g2.reshape(w.shape))
        out_d.append(d2.reshape(w.shape))
        out_m.append(m2.reshape(w.shape))
        out_v.append(v2.reshape(w.shape))
    return (loss, grad_x.reshape(x.shape), *out_g, *out_d, *out_m, *out_v)
```

```python
import functools
import math

import jax
import jax.numpy as jnp
from jax import lax
from jax.experimental import pallas as pl
from jax.experimental.pallas import tpu as pltpu

F32 = jnp.float32
BF16 = jnp.bfloat16
MESH = pl.DeviceIdType.MESH

D_MODEL = 2048
D_CONV = 1024
N_HEADS = 8
QK_NOPE = 128
QK_ROPE = 64
QK_HEAD = QK_NOPE + QK_ROPE
V_HEAD = 128
D_ATTN = N_HEADS * V_HEAD
Q_LORA = 512
KV_LORA = 256
ROPE_BASE = 10000.0
IN_COLS = 4 * D_CONV + Q_LORA + KV_LORA + QK_ROPE + D_ATTN
EPS = 1e-6
ADAM_LR, ADAM_B1, ADAM_B2, ADAM_EPS, ADAM_WD, ADAM_STEP = 0.001, 0.9, 0.999, 1e-08, 0.01, 10

N_DEV = 8
LANES = 128
QK_PAD = 256
U_COLS = 6144
U_CQ, U_CKV, U_KR, U_ZA = 4096, 4608, 4864, 4928
U_TAIL = 2048
ZA_LO = U_ZA - (U_COLS - U_TAIL) - QK_ROPE
ZA_WIN = D_ATTN + LANES
CW = IN_COLS // 8
EXP_W = 896
W_LO = [(CW * d // 128) * 128 for d in range(8)]
W_OFF = [CW * d - lo for d, lo in enumerate(W_LO)]
SCALE = 1.0 / math.sqrt(QK_HEAD)
LOG2E = 1.4426950408889634
LN2 = 0.6931471805599453
NEG = -1e30
VMEM_LIMIT = 56 * 1024 * 1024

TM_ELEM = 256
TM_MM = 512
TQ = 1024
Q_CHAINS = 4

SM_MOD, SM_NG, SM_QAG, SM_KVAG, SM_QG, SM_KG, SM_CONV, SM_LOSS = 0, 6144, 8192, 8704, 8960, 9216, 9472, 12544
SM_COLS = 12672


def _params(sem=None):
    kw = dict(vmem_limit_bytes=VMEM_LIMIT)
    if sem is not None:
        kw["dimension_semantics"] = sem
    return pltpu.CompilerParams(**kw)


def _sigmoid(z):
    return 1.0 / (1.0 + jnp.exp(-z))


def _rot64(x):
    lane = lax.broadcasted_iota(jnp.int32, x.shape, 1)
    return jnp.where(lane < 32, pltpu.roll(x, 96, 1), pltpu.roll(x, 32, 1))


def _rope(x, cos, sin):
    return x * cos + _rot64(x) * sin


def _rope_t(d, cos, sin):
    return d * cos - _rot64(d) * sin


def _dot(a, b):
    return jnp.dot(a, b, preferred_element_type=F32)


def _dot_nt(a, b):
    return lax.dot_general(a, b, (((1,), (1,)), ((), ())), preferred_element_type=F32)


def _dot_tn(a, b):
    return lax.dot_general(a, b, (((0,), (0,)), ((), ())), preferred_element_type=F32)


def _my_index():
    return 4 * lax.axis_index("x") + 2 * lax.axis_index("y") + lax.axis_index("c")


ANY = pl.BlockSpec(memory_space=pl.ANY)


class _Gather:
    def __init__(self, blocks, relay=False):
        self.relay = relay
        self.n = n = len(blocks)
        self.out_shape = [jax.ShapeDtypeStruct((N_DEV,) + b.shape, b.dtype) for b in blocks]
        self.scratch = [pltpu.SemaphoreType.DMA((7 * n,)), pltpu.SemaphoreType.DMA((7 * n,)),
                        pltpu.SemaphoreType.DMA((n,))]

    @staticmethod
    def _places():
        x, y, c = lax.axis_index("x"), lax.axis_index("y"), lax.axis_index("c")
        return (x, y, c), (x, y, 1 - c), [(1 - x, y), (x, 1 - y), (1 - x, 1 - y)]

    @staticmethod
    def _copy(outs, sems, a, k, block, to, src=None):
        dst = outs[a].at[4 * block[0] + 2 * block[1] + block[2]]
        return pltpu.make_async_remote_copy(
            src_ref=dst if src is None else src, dst_ref=dst, send_sem=sems[0].at[7 * a + k],
            recv_sem=sems[1].at[7 * a + k], device_id=to, device_id_type=MESH)

    def _first(self, ins, outs, sems):
        me, sibling, chips = self._places()
        first = []
        for a in range(self.n):
            first.append(self._copy(outs, sems, a, 0, me, sibling, src=ins[a]))
            first += [self._copy(outs, sems, a, 1 + j, me, (*chip, me[2]), src=ins[a])
                      for j, chip in enumerate(chips[:2] if self.relay else chips)]
        return first

    def _relays(self, outs, sems):
        if not self.relay:
            return []
        (x, y, c), _, _ = self._places()
        via = (jnp.where(c == 0, 1 - x, x), jnp.where(c == 0, y, 1 - y))
        to = (jnp.where(c == 0, x, 1 - x), jnp.where(c == 0, 1 - y, y))
        return [self._copy(outs, sems, a, 3, (*via, c), (*to, c)) for a in range(self.n)]

    def _passed(self, outs, sems):
        me, sibling, chips = self._places()
        return [self._copy(outs, sems, a, 4 + j, (*chip, me[2]), sibling)
                for a in range(self.n) for j, chip in enumerate(chips)]

    def _mine(self, ins, outs, sems):
        me, _, _ = self._places()
        return [pltpu.make_async_copy(ins[a], outs[a].at[4 * me[0] + 2 * me[1] + me[2]], sems[2].at[a])
                for a in range(self.n)]

    def start(self, ins, outs, sems):
        for cp in self._mine(ins, outs, sems) + self._first(ins, outs, sems):
            cp.start()

    def forward(self, ins, outs, sems):
        del ins
        me, _, chips = self._places()
        passed = self._passed(outs, sems)
        for a in range(self.n):
            for j, chip in enumerate(chips[:2] if self.relay else chips):
                self._copy(outs, sems, a, 1 + j, (*chip, me[2]), me).wait_recv()
                passed[3 * a + j].start()
        for cp in self._relays(outs, sems):
            cp.start()
        if self.relay:
            for a in range(self.n):
                self._copy(outs, sems, a, 3, (*chips[2], me[2]), me).wait_recv()
                passed[3 * a + 2].start()

    def finish(self, ins, outs, sems):
        me, sibling, chips = self._places()
        for a in range(self.n):
            self._copy(outs, sems, a, 0, sibling, me).wait_recv()
            for j, chip in enumerate(chips):
                self._copy(outs, sems, a, 4 + j, (*chip, 1 - me[2]), me).wait_recv()
        for cp in self._first(ins, outs, sems) + self._relays(outs, sems) + self._passed(outs, sems):
            cp.wait_send()
        for cp in self._mine(ins, outs, sems):
            cp.wait()


class _ChipExchange:
    def __init__(self, arrays):
        self.n = n = len(arrays)
        self.out_shape = [jax.ShapeDtypeStruct(a.shape, a.dtype) for a in arrays]
        self.scratch = [pltpu.SemaphoreType.DMA((3 * n,)), pltpu.SemaphoreType.DMA((3 * n,))]

    def _copies(self, ins, outs, sems):
        x, y, c = lax.axis_index("x"), lax.axis_index("y"), lax.axis_index("c")
        return [pltpu.make_async_remote_copy(
            src_ref=ins[a].at[2 * px + py], dst_ref=outs[a].at[2 * x + y], send_sem=sems[0].at[3 * a + j],
            recv_sem=sems[1].at[3 * a + j], device_id=(px, py, c), device_id_type=MESH)
            for a in range(self.n) for j, (px, py) in enumerate([(1 - x, y), (x, 1 - y), (1 - x, 1 - y)])]

    def start(self, ins, outs, sems):
        for cp in self._copies(ins, outs, sems):
            cp.start()

    def forward(self, ins, outs, sems):
        pass

    def finish(self, ins, outs, sems):
        for cp in self._copies(ins, outs, sems):
            cp.wait()


def _all_gather(blocks, name):
    n = len(blocks)
    g = _Gather(blocks)

    def body(*refs):
        ins, outs, sems = refs[:n], refs[n:2 * n], refs[2 * n:]
        g.start(ins, outs, sems)
        g.forward(ins, outs, sems)
        g.finish(ins, outs, sems)

    return pl.pallas_call(body, name=name, out_shape=g.out_shape, in_specs=[ANY] * n, out_specs=[ANY] * n,
                          scratch_shapes=g.scratch)(*blocks)


def _rs_sibling(dw_in, dwq, dwkv, dwout):
    srcs = (dw_in, dwq, dwkv, dwout)
    shapes = [(dw_in.shape[0], EXP_W)] + [a.shape[1:] for a in srcs[1:]]

    def body(in_ref, q_ref, kv_ref, out_ref, rin, rq, rkv, rout, send_sems, recv_sems):
        x, y, c = lax.axis_index("x"), lax.axis_index("y"), lax.axis_index("c")

        def exchange(c_val):
            copies = []
            for k in range(4):
                e = 2 * k + (1 - c_val)
                pairs = [(in_ref.at[:, pl.ds(W_LO[e], EXP_W)], rin.at[k]), (q_ref.at[e], rq.at[k]),
                         (kv_ref.at[e], rkv.at[k]), (out_ref.at[e], rout.at[k])]
                for a, (src, dst) in enumerate(pairs):
                    copies.append(pltpu.make_async_remote_copy(
                        src_ref=src, dst_ref=dst, send_sem=send_sems.at[4 * a + k], recv_sem=recv_sems.at[4 * a + k],
                        device_id=(x, y, 1 - c), device_id_type=MESH))
            for cp in copies:
                cp.start()
            for cp in copies:
                cp.wait()

        for c_val in (0, 1):
            pl.when(c == c_val)(functools.partial(exchange, c_val))

    return pl.pallas_call(
        body, name="rs_sibling", out_shape=[jax.ShapeDtypeStruct((4,) + tuple(sh), a.dtype) for sh, a in zip(shapes, srcs)],
        in_specs=[ANY] * 4, out_specs=[ANY] * 4,
        scratch_shapes=[pltpu.SemaphoreType.DMA((16,)), pltpu.SemaphoreType.DMA((16,))],
    )(*srcs)


def _add_window(dw_in, recv, lo_tiles):
    k, rows, _ = recv.shape

    def body(t_ref, w_ref, r_ref, o_ref):
        del t_ref
        o_ref[0] = (w_ref[...].astype(F32) + r_ref[0].astype(F32)).astype(o_ref.dtype)

    spec = pl.BlockSpec((1, rows, LANES), lambda i, j, t: (i, 0, j))
    grid_spec = pltpu.PrefetchScalarGridSpec(
        num_scalar_prefetch=1, grid=(k, EXP_W // LANES),
        in_specs=[pl.BlockSpec((rows, LANES), lambda i, j, t: (0, t[i] + j)), spec], out_specs=spec)
    return pl.pallas_call(
        body, name="rs_add_in", grid_spec=grid_spec, out_shape=jax.ShapeDtypeStruct(recv.shape, recv.dtype),
        compiler_params=_params(("parallel", "parallel")),
    )(lo_tiles, dw_in, recv)


def _final_sum(p, r, sel, name, unshift_to=None):
    _, rows, cols = p.shape
    tr = 512 if rows % 512 == 0 else rows
    out_cols = cols if unshift_to is None else unshift_to

    def body(sel_ref, p_ref, r0, r1, r2, r3, o_ref):
        own = p_ref[0].astype(F32)
        acc = None
        for k, r_ref in enumerate((r0, r1, r2, r3)):
            term = jnp.where(sel_ref[0] == k, own, r_ref[0].astype(F32))
            acc = term if acc is None else acc + term
        if unshift_to is not None:
            acc = pltpu.roll(acc, sel_ref[5], 1)[:, :unshift_to]
        o_ref[...] = acc

    def slot(k):
        return pl.BlockSpec((1, tr, cols), lambda i, t: (t[k], i, 0))

    grid_spec = pltpu.PrefetchScalarGridSpec(
        num_scalar_prefetch=1, grid=(rows // tr,), in_specs=[slot(0), slot(1), slot(2), slot(3), slot(4)],
        out_specs=pl.BlockSpec((tr, out_cols), lambda i, t: (i, 0)))
    return pl.pallas_call(
        body, name=name, grid_spec=grid_spec, out_shape=jax.ShapeDtypeStruct((rows, out_cols), F32),
        compiler_params=_params(("parallel",)),
    )(sel, p, r, r, r, r)


def _expand_w_in(w, shift):
    rows, cw = w.shape
    tr = 256

    def body(s_ref, w_ref, o_ref, buf):
        buf[...] = jnp.zeros_like(buf)
        buf[:, 0:cw] = w_ref[...]
        o_ref[...] = pltpu.roll(buf[...], s_ref[0], 1).astype(BF16)

    grid_spec = pltpu.PrefetchScalarGridSpec(
        num_scalar_prefetch=1, grid=(rows // tr,), in_specs=[pl.BlockSpec((tr, cw), lambda i, t: (i, 0))],
        out_specs=pl.BlockSpec((tr, EXP_W), lambda i, t: (i, 0)), scratch_shapes=[pltpu.VMEM((tr, EXP_W), F32)])
    return pl.pallas_call(
        body, name="expand_w_in", grid_spec=grid_spec, out_shape=jax.ShapeDtypeStruct((rows, EXP_W), BF16),
        compiler_params=_params(("arbitrary",)),
    )(shift, w)


def _pad_wq(w):
    rows, cw = w.shape

    def body(w_ref, o_ref, buf):
        buf[...] = jnp.zeros_like(buf)
        buf[:, 0:cw] = w_ref[...]
        o_ref[...] = buf[...].astype(BF16)

    return pl.pallas_call(
        body, name="pad_wq", out_shape=jax.ShapeDtypeStruct((rows, QK_PAD), BF16),
        scratch_shapes=[pltpu.VMEM((rows, QK_PAD), F32)], compiler_params=_params(),
    )(w)


def _merge_w_in(e):
    _, rows, _ = e.shape
    tr = 256

    def body(e_ref, o_ref):
        for t in range(U_COLS // LANES):
            lo, hi = t * LANES, (t + 1) * LANES
            parts = [e_ref[d, :, lo - W_LO[d]:hi - W_LO[d]] for d in range(N_DEV)
                     if CW * d < hi and CW * (d + 1) > lo]
            if not parts:
                tile = jnp.zeros((tr, LANES), BF16)
            elif len(parts) == 1:
                tile = parts[0]
            else:
                tile = (parts[0].astype(F32) + parts[1].astype(F32)).astype(BF16)
            o_ref[:, lo:hi] = tile

    return pl.pallas_call(
        body, name="merge_w_in", grid=(rows // tr,),
        in_specs=[pl.BlockSpec((N_DEV, tr, EXP_W), lambda i: (0, i, 0))],
        out_specs=pl.BlockSpec((tr, U_COLS), lambda i: (i, 0)), out_shape=jax.ShapeDtypeStruct((rows, U_COLS), BF16),
        compiler_params=_params(("parallel",)),
    )(e)


def _sum_leading(a, out_dtype, name):
    k, rows, cols = a.shape
    tr = min(rows, 1728 if rows % 1728 == 0 else rows)

    def body(a_ref, o_ref):
        acc = a_ref[0].astype(F32)
        for i in range(1, k):
            acc = acc + a_ref[i].astype(F32)
        o_ref[...] = acc.astype(out_dtype)

    return pl.pallas_call(
        body, name=name, grid=(rows // tr,),
        in_specs=[pl.BlockSpec((k, tr, cols), lambda i: (0, i, 0))],
        out_specs=pl.BlockSpec((tr, cols), lambda i: (i, 0)),
        out_shape=jax.ShapeDtypeStruct((rows, cols), out_dtype), compiler_params=_params(("parallel",)),
    )(a)


def _add_pairs(g, recv, core, name):
    k, rows, cols = recv.shape
    tr = 1728 if rows % 1728 == 0 else rows

    def body(c_ref, g_ref, r_ref, o_ref):
        del c_ref
        o_ref[...] = (g_ref[...].astype(F32) + r_ref[...].astype(F32)).astype(o_ref.dtype)

    spec = pl.BlockSpec((1, tr, cols), lambda i, j, c: (i, j, 0))
    grid_spec = pltpu.PrefetchScalarGridSpec(
        num_scalar_prefetch=1, grid=(k, rows // tr),
        in_specs=[pl.BlockSpec((1, tr, cols), lambda i, j, c: (2 * i + c[0], j, 0)), spec], out_specs=spec)
    return pl.pallas_call(
        body, name=name, grid_spec=grid_spec, out_shape=jax.ShapeDtypeStruct(recv.shape, recv.dtype),
        compiler_params=_params(("parallel", "parallel")),
    )(core, g, recv)


def _ada_mod(c16, ada_w_l, ada_b_l):
    def body(c_ref, w_ref, b_ref, o_ref):
        cv = c_ref[...]
        sc = (cv * _sigmoid(cv)).astype(BF16)
        o_ref[...] = _dot(sc, w_ref[...].astype(BF16)) + b_ref[...]

    return pl.pallas_call(
        body, name="ada_mod", out_shape=jax.ShapeDtypeStruct((c16.shape[0], ada_w_l.shape[1]), F32),
        compiler_params=_params(),
    )(c16, ada_w_l, ada_b_l)


def _ada_w_grad(c_t, dmod_my):
    def body(c_ref, d_ref, o_ref):
        cv = c_ref[...]
        sc = cv * _sigmoid(cv)
        acc = sc[:, 0:1] * d_ref[0:1, :]
        for b in range(1, N_DEV):
            acc = acc + sc[:, b:b + 1] * d_ref[b:b + 1, :]
        o_ref[...] = acc

    return pl.pallas_call(
        body, name="ada_w_grad", out_shape=jax.ShapeDtypeStruct((c_t.shape[0], dmod_my.shape[1]), F32),
        compiler_params=_params(),
    )(c_t, dmod_my)


def _norm_mod(x, norm_g, mod, pos_col, invf, sign, rider, rider_inputs):
    s, d = x.shape
    tm = min(TM_MM, s)
    n_in, n_out = len(rider_inputs), len(rider.out_shape)
    steps = s // tm

    def body(x_ref, g_ref, mod_ref, p_ref, f_ref, s_ref, *rest):
        r_ins, (h_ref, ht_ref, cos_ref, sin_ref) = rest[:n_in], rest[n_in:n_in + 4]
        r_outs, sems = rest[n_in + 4:n_in + 4 + n_out], rest[n_in + 4 + n_out:]
        pl.when(pl.program_id(0) == 0)(functools.partial(rider.start, r_ins, r_outs, sems))
        xv = x_ref[...]
        r = lax.rsqrt(jnp.mean(xv * xv, axis=-1, keepdims=True) + EPS)
        hn = xv * r * g_ref[...]
        hv = hn * (1.0 + mod_ref[:, d:2 * d]) + mod_ref[:, 0:d]
        h_ref[...] = hv.astype(BF16)
        ht_ref[...] = hv.T.astype(BF16)
        ang = p_ref[...].astype(F32) * f_ref[...]
        sg = s_ref[...]
        cos_ref[...] = jnp.cos(ang) * jnp.abs(sg)
        sin_ref[...] = jnp.sin(ang) * sg

        @pl.when(pl.program_id(0) == steps - 1)
        def _():
            rider.forward(r_ins, r_outs, sems)
            rider.finish(r_ins, r_outs, sems)

    row = pl.BlockSpec((1, LANES), lambda i: (0, 0))
    tab = pl.BlockSpec((tm, LANES), lambda i: (i, 0))
    return pl.pallas_call(
        body, name="norm_mod", grid=(steps,),
        in_specs=[pl.BlockSpec((tm, d), lambda i: (i, 0)), pl.BlockSpec((1, d), lambda i: (0, 0)),
                  pl.BlockSpec((1, 3 * d), lambda i: (0, 0)), pl.BlockSpec((tm, 1), lambda i: (i, 0)), row, row]
        + [ANY] * n_in,
        out_specs=[pl.BlockSpec((tm, d), lambda i: (i, 0)), pl.BlockSpec((d, tm), lambda i: (0, i)), tab, tab] + [ANY] * n_out,
        out_shape=[jax.ShapeDtypeStruct((s, d), BF16), jax.ShapeDtypeStruct((d, s), BF16),
                   jax.ShapeDtypeStruct((s, LANES), F32), jax.ShapeDtypeStruct((s, LANES), F32)] + rider.out_shape,
        scratch_shapes=rider.scratch, compiler_params=_params(("arbitrary",)),
    )(x, norm_g, mod, pos_col, invf, sign, *rider_inputs)


def _matmul(a, b, *, nt, out_dtype, tm, tn, name, rider=None, rider_inputs=()):
    m, kdim = a.shape
    n = b.shape[0] if nt else b.shape[1]
    tm, tn = min(tm, m), min(tn, n)
    n_in = len(rider_inputs)
    n_out = len(rider.out_shape) if rider else 0
    m_steps = m // tm
    steps = (n // tn) * m_steps

    def body(a_ref, b_ref, *rest):
        r_ins, o_ref, r_outs, sems = rest[:n_in], rest[n_in], rest[n_in + 1:n_in + 1 + n_out], rest[n_in + 1 + n_out:]
        step = pl.program_id(0) * m_steps + pl.program_id(1)
        if rider:
            pl.when(step == 0)(functools.partial(rider.start, r_ins, r_outs, sems))
            pl.when(step == steps // 2)(functools.partial(rider.forward, r_ins, r_outs, sems))
        o = _dot_nt(a_ref[...], b_ref[...]) if nt else _dot(a_ref[...], b_ref[...])
        o_ref[...] = o.astype(out_dtype)
        if rider:
            pl.when(step == steps - 1)(functools.partial(rider.finish, r_ins, r_outs, sems))

    b_spec = pl.BlockSpec((tn, kdim), lambda j, i: (j, 0)) if nt else pl.BlockSpec((kdim, tn), lambda j, i: (0, j))
    out = pl.pallas_call(
        body, name=name, grid=(n // tn, m_steps),
        in_specs=[pl.BlockSpec((tm, kdim), lambda j, i: (i, 0)), b_spec] + [ANY] * n_in,
        out_specs=[pl.BlockSpec((tm, tn), lambda j, i: (i, j))] + [ANY] * n_out,
        out_shape=[jax.ShapeDtypeStruct((m, n), out_dtype)] + (rider.out_shape if rider else []),
        scratch_shapes=rider.scratch if rider else [],
        compiler_params=_params(("arbitrary", "arbitrary") if rider else ("parallel", "parallel")),
    )(a, b, *rider_inputs)
    return out if rider else out[0]


HALO = 16


def _conv_fwd(u, conv_w):
    s = u.shape[0]
    tm = min(TM_ELEM, s)
    cb = D_CONV

    def body(xc_ref, bc_ref, cc_ref, zc_ref, xp_ref, cp_ref, w_ref, y_ref):
        i = pl.program_id(0)
        uc = cc_ref[...].astype(F32) * xc_ref[...].astype(F32)
        up = cp_ref[...].astype(F32) * xp_ref[...].astype(F32)
        up = jnp.where(i == 0, 0.0, up)
        full = jnp.concatenate([up, uc], axis=0)
        u1 = pltpu.roll(full, 1, 0)[HALO:]
        u2 = pltpu.roll(full, 2, 0)[HALO:]
        w = w_ref[...]
        conv = w[0:1] * u2 + w[1:2] * u1 + w[2:3] * uc
        z = zc_ref[...].astype(F32)
        y_ref[...] = (bc_ref[...].astype(F32) * conv * (z * _sigmoid(z))).astype(BF16)

    def col(j):
        return pl.BlockSpec((tm, cb), lambda i: (i, j))

    def prev(j):
        return pl.BlockSpec((HALO, cb), lambda i: (jnp.maximum(i * (tm // HALO) - 1, 0), j))

    return pl.pallas_call(
        body, name="conv_fwd", grid=(s // tm,),
        in_specs=[col(0), col(1), col(2), col(3), prev(0), prev(2), pl.BlockSpec((3, cb), lambda i: (0, 0))],
        out_specs=pl.BlockSpec((tm, cb), lambda i: (i, 0)), out_shape=jax.ShapeDtypeStruct((s, cb), BF16),
        compiler_params=_params(("parallel",)),
    )(u, u, u, u, u, u, conv_w)


def _conv_bwd(u, dyc, conv_w):
    s = u.shape[0]
    tm = min(TM_ELEM, s)
    cb = D_CONV
    nt = s // tm

    def body(xc_ref, bc_ref, cc_ref, zc_ref, xp_ref, cp_ref, bn_ref, zn_ref, dy_ref, dyn_ref, w_ref, du_ref, dw_ref):
        i = pl.program_id(0)
        xc, cc = xc_ref[...].astype(F32), cc_ref[...].astype(F32)
        bc, z = bc_ref[...].astype(F32), zc_ref[...].astype(F32)
        uc = cc * xc
        up = jnp.where(i == 0, 0.0, cp_ref[...].astype(F32) * xp_ref[...].astype(F32))
        full = jnp.concatenate([up, uc], axis=0)
        u1 = pltpu.roll(full, 1, 0)[HALO:]
        u2 = pltpu.roll(full, 2, 0)[HALO:]
        w = w_ref[...]
        conv = w[0:1] * u2 + w[1:2] * u1 + w[2:3] * uc
        sg = _sigmoid(z)
        sz = z * sg
        dy = dy_ref[...].astype(F32)
        dconv = dy * bc * sz
        zn = zn_ref[...].astype(F32)
        dnext = dyn_ref[...].astype(F32) * bn_ref[...].astype(F32) * (zn * _sigmoid(zn))
        dnext = jnp.where(i == nt - 1, 0.0, dnext)
        fullb = jnp.concatenate([dconv, dnext], axis=0)
        nb = tm + HALO
        d1 = pltpu.roll(fullb, nb - 1, 0)[:tm]
        d2 = pltpu.roll(fullb, nb - 2, 0)[:tm]
        duc = w[2:3] * dconv + w[1:2] * d1 + w[0:1] * d2
        dzc = dy * bc * conv * (sg * (1.0 + z * (1.0 - sg)))
        du_ref[...] = jnp.concatenate([duc * cc, dy * conv * sz, duc * xc, dzc], axis=1).astype(BF16)
        dw = jnp.concatenate([jnp.sum(dconv * u2, axis=0, keepdims=True), jnp.sum(dconv * u1, axis=0, keepdims=True),
                              jnp.sum(dconv * uc, axis=0, keepdims=True), jnp.zeros((5, cb), F32)], axis=0)

        @pl.when(i == 0)
        def _():
            dw_ref[...] = dw

        @pl.when(i > 0)
        def _():
            dw_ref[...] += dw

    def col(j):
        return pl.BlockSpec((tm, cb), lambda i: (i, j))

    def prev(j):
        return pl.BlockSpec((HALO, cb), lambda i: (jnp.maximum(i * (tm // HALO) - 1, 0), j))

    def nxt(j):
        return pl.BlockSpec((HALO, cb), lambda i: (jnp.minimum((i + 1) * (tm // HALO), s // HALO - 1), j))

    return pl.pallas_call(
        body, name="conv_bwd", grid=(nt,),
        in_specs=[col(0), col(1), col(2), col(3), prev(0), prev(2), nxt(1), nxt(3), col(0), nxt(0),
                  pl.BlockSpec((3, cb), lambda i: (0, 0))],
        out_specs=[pl.BlockSpec((tm, 4 * cb), lambda i: (i, 0)), pl.BlockSpec((8, cb), lambda i: (0, 0))],
        out_shape=[jax.ShapeDtypeStruct((s, U_COLS), BF16), jax.ShapeDtypeStruct((8, cb), F32)],
        compiler_params=_params(("arbitrary",)),
    )(u, u, u, u, u, u, u, u, dyc, dyc, conv_w)


def _qkv_specs(tm):
    return [pl.BlockSpec((tm, Q_LORA), lambda i: (i, U_CQ // Q_LORA)),
            pl.BlockSpec((tm, KV_LORA), lambda i: (i, U_CKV // KV_LORA)),
            pl.BlockSpec((tm, LANES), lambda i: (i, U_KR // LANES)),
            pl.BlockSpec((tm, LANES), lambda i: (i, 0)), pl.BlockSpec((tm, LANES), lambda i: (i, 0))]


def _full(shape):
    return pl.BlockSpec(shape, lambda i: (0,) * len(shape))


def _k_rope_lanes(blk):
    lane = lax.broadcasted_iota(jnp.int32, blk.shape, 1)
    return jnp.where(lane < QK_ROPE, blk, 0.0)


def _qkv_fwd(u, cos, sin, wq, wkv, qag, kvag, qg, kg):
    s = u.shape[0]
    tm = min(TM_ELEM, s)

    def body(cq_ref, ckv_ref, kr_ref, cos_ref, sin_ref, wq_ref, wkv_ref, qag_ref, kvag_ref, qg_ref, kg_ref,
             q_ref, k_ref, v_ref):
        cq = cq_ref[...].astype(F32)
        cqn = (cq * lax.rsqrt(jnp.mean(cq * cq, axis=-1, keepdims=True) + EPS) * qag_ref[...]).astype(BF16)
        ckv = ckv_ref[...].astype(F32)
        ckvn = (ckv * lax.rsqrt(jnp.mean(ckv * ckv, axis=-1, keepdims=True) + EPS) * kvag_ref[...]).astype(BF16)
        kr = _k_rope_lanes(kr_ref[...].astype(F32))
        cosv, sinv, qgv, kgv = cos_ref[...], sin_ref[...], qg_ref[...], kg_ref[...]
        ss_r = jnp.sum(kr * kr, axis=-1, keepdims=True)
        krr = _rope(kr * kgv[:, QK_NOPE:], cosv, sinv)
        qf = _dot(cqn, wq_ref[...])
        kvf = _dot(ckvn, wkv_ref[...])
        for h in range(N_HEADS):
            qh = qf[:, QK_PAD * h:QK_PAD * (h + 1)]
            rq = lax.rsqrt(jnp.sum(qh * qh, axis=-1, keepdims=True) * (1.0 / QK_HEAD) + EPS)
            qn = qh * rq * qgv
            qo = jnp.concatenate([qn[:, :QK_NOPE], _rope(qn[:, QK_NOPE:], cosv, sinv)], axis=1) * (SCALE * LOG2E)
            q_ref[h] = qo.astype(BF16)
            kn = kvf[:, 2 * V_HEAD * h:2 * V_HEAD * h + QK_NOPE]
            vh = kvf[:, 2 * V_HEAD * h + QK_NOPE:2 * V_HEAD * (h + 1)]
            rk = lax.rsqrt((jnp.sum(kn * kn, axis=-1, keepdims=True) + ss_r) * (1.0 / QK_HEAD) + EPS)
            k_ref[h] = jnp.concatenate([kn * kgv[:, :QK_NOPE] * rk, krr * rk], axis=1).astype(BF16)
            v_ref[h] = jnp.concatenate([vh, jnp.ones_like(vh)], axis=1).astype(BF16)

    return pl.pallas_call(
        body, name="qkv_fwd", grid=(s // tm,),
        in_specs=_qkv_specs(tm) + [_full((Q_LORA, N_HEADS * QK_PAD)), _full((KV_LORA, 2 * D_ATTN)),
                                   _full((1, Q_LORA)), _full((1, KV_LORA)), _full((1, QK_PAD)), _full((1, QK_PAD))],
        out_specs=[pl.BlockSpec((N_HEADS, tm, QK_PAD), lambda i: (0, i, 0)),
                   pl.BlockSpec((N_HEADS, tm, QK_PAD), lambda i: (0, i, 0)),
                   pl.BlockSpec((N_HEADS, tm, 2 * V_HEAD), lambda i: (0, i, 0))],
        out_shape=[jax.ShapeDtypeStruct((N_HEADS, s, QK_PAD), BF16), jax.ShapeDtypeStruct((N_HEADS, s, QK_PAD), BF16),
                   jax.ShapeDtypeStruct((N_HEADS, s, 2 * V_HEAD), BF16)],
        compiler_params=_params(("parallel",)),
    )(u, u, u, cos, sin, wq, wkv, qag, kvag, qg, kg)


def _qkv_bwd(u, cos, sin, dq, dk, dv, dza, wq, wkv, qag, kvag, qg, kg, du):
    s = u.shape[0]
    tm = min(TM_ELEM, s)
    nt = s // tm

    def body(cq_ref, ckv_ref, kr_ref, cos_ref, sin_ref, dq_ref, dk_ref, dv_ref, dza_ref, wq_ref, wkv_ref, qag_ref,
             kvag_ref, qg_ref, kg_ref, du_in, du_ref, dwq_ref, dwkv_ref, dqag_ref, dkvag_ref, dqg_ref, dkg_ref,
             dwq_acc, dwkv_acc):
        del du_in
        i = pl.program_id(0)

        @pl.when(i == 0)
        def _():
            dwq_acc[...] = jnp.zeros_like(dwq_acc)
            dwkv_acc[...] = jnp.zeros_like(dwkv_acc)

        cq = cq_ref[...].astype(F32)
        rqa = lax.rsqrt(jnp.mean(cq * cq, axis=-1, keepdims=True) + EPS)
        xq = cq * rqa
        qagv = qag_ref[...]
        cqn = (xq * qagv).astype(BF16)
        ckv = ckv_ref[...].astype(F32)
        rkva = lax.rsqrt(jnp.mean(ckv * ckv, axis=-1, keepdims=True) + EPS)
        xkv = ckv * rkva
        kvagv = kvag_ref[...]
        ckvn = (xkv * kvagv).astype(BF16)
        kr = _k_rope_lanes(kr_ref[...].astype(F32))
        cosv, sinv, qgv, kgv = cos_ref[...], sin_ref[...], qg_ref[...], kg_ref[...]
        ss_r = jnp.sum(kr * kr, axis=-1, keepdims=True)
        dqg = jnp.zeros((1, QK_PAD), F32)
        dkg = jnp.zeros((1, QK_PAD), F32)
        dkr = jnp.zeros((tm, LANES), F32)
        qf = _dot(cqn, wq_ref[...])
        kvf = _dot(ckvn, wkv_ref[...])
        dqf, dkvf = [], []
        for h in range(N_HEADS):
            qh = qf[:, QK_PAD * h:QK_PAD * (h + 1)]
            rq = lax.rsqrt(jnp.sum(qh * qh, axis=-1, keepdims=True) * (1.0 / QK_HEAD) + EPS)
            xh = qh * rq
            g = dq_ref[h].astype(F32) * SCALE
            dyq = jnp.concatenate([g[:, :QK_NOPE], _rope_t(g[:, QK_NOPE:], cosv, sinv)], axis=1)
            dqg = dqg + jnp.sum(dyq * xh, axis=0, keepdims=True)
            gdy = dyq * qgv
            dqf.append((rq * (gdy - xh * (jnp.sum(gdy * xh, axis=-1, keepdims=True) * (1.0 / QK_HEAD)))).astype(BF16))
            kn = kvf[:, 2 * V_HEAD * h:2 * V_HEAD * h + QK_NOPE]
            rk = lax.rsqrt((jnp.sum(kn * kn, axis=-1, keepdims=True) + ss_r) * (1.0 / QK_HEAD) + EPS)
            xk = jnp.concatenate([kn, kr], axis=1) * rk
            gk = dk_ref[h].astype(F32)
            dyk = jnp.concatenate([gk[:, :QK_NOPE], _rope_t(gk[:, QK_NOPE:], cosv, sinv)], axis=1)
            dkg = dkg + jnp.sum(dyk * xk, axis=0, keepdims=True)
            gdyk = dyk * kgv
            dxk = rk * (gdyk - xk * (jnp.sum(gdyk * xk, axis=-1, keepdims=True) * (1.0 / QK_HEAD)))
            dkr = dkr + dxk[:, QK_NOPE:]
            dkvf += [dxk[:, :QK_NOPE].astype(BF16), dv_ref[h]]
        dqf_b, dkvf_b = jnp.concatenate(dqf, axis=1), jnp.concatenate(dkvf, axis=1)
        dwq_acc[...] += _dot_tn(cqn, dqf_b)
        dwkv_acc[...] += _dot_tn(ckvn, dkvf_b)
        dcqn = _dot_nt(dqf_b, wq_ref[...])
        dckvn = _dot_nt(dkvf_b, wkv_ref[...])
        dqag = jnp.sum(dcqn * xq, axis=0, keepdims=True)
        dkvag = jnp.sum(dckvn * xkv, axis=0, keepdims=True)
        gq = dcqn * qagv
        dcq = rqa * (gq - xq * jnp.mean(gq * xq, axis=-1, keepdims=True))
        gkv = dckvn * kvagv
        dckv = rkva * (gkv - xkv * jnp.mean(gkv * xkv, axis=-1, keepdims=True))
        win = pltpu.roll(jnp.concatenate([dza_ref[...].astype(F32), jnp.zeros((tm, LANES), F32)], axis=1), QK_ROPE, 1)
        win = win + jnp.concatenate([dkr, jnp.zeros((tm, D_ATTN), F32)], axis=1)
        du_ref[...] = jnp.concatenate([dcq, dckv, win, jnp.zeros((tm, U_TAIL - ZA_LO - ZA_WIN), F32)], axis=1).astype(BF16)

        @pl.when(i == 0)
        def _():
            dqag_ref[...] = dqag
            dkvag_ref[...] = dkvag
            dqg_ref[...] = dqg
            dkg_ref[...] = dkg

        @pl.when(i > 0)
        def _():
            dqag_ref[...] += dqag
            dkvag_ref[...] += dkvag
            dqg_ref[...] += dqg
            dkg_ref[...] += dkg

        @pl.when(i == nt - 1)
        def _():
            dwq_ref[...] = dwq_acc[...].astype(BF16)
            dwkv_ref[...] = dwkv_acc[...].astype(BF16)

    head = lambda w: pl.BlockSpec((N_HEADS, tm, w), lambda i: (0, i, 0))
    wq_shape, wkv_shape = (Q_LORA, N_HEADS * QK_PAD), (KV_LORA, 2 * D_ATTN)
    return pl.pallas_call(
        body, name="qkv_bwd", grid=(nt,),
        in_specs=_qkv_specs(tm) + [head(QK_PAD), head(QK_PAD), head(V_HEAD), pl.BlockSpec((tm, D_ATTN), lambda i: (i, 0)),
                                   _full(wq_shape), _full(wkv_shape), _full((1, Q_LORA)), _full((1, KV_LORA)),
                                   _full((1, QK_PAD)), _full((1, QK_PAD)), ANY],
        out_specs=[pl.BlockSpec((tm, U_TAIL), lambda i: (i, U_COLS // U_TAIL - 1)), _full(wq_shape), _full(wkv_shape),
                   _full((1, Q_LORA)), _full((1, KV_LORA)), _full((1, QK_PAD)), _full((1, QK_PAD))],
        out_shape=[jax.ShapeDtypeStruct(du.shape, du.dtype), jax.ShapeDtypeStruct(wq_shape, BF16),
                   jax.ShapeDtypeStruct(wkv_shape, BF16), jax.ShapeDtypeStruct((1, Q_LORA), F32),
                   jax.ShapeDtypeStruct((1, KV_LORA), F32), jax.ShapeDtypeStruct((1, QK_PAD), F32),
                   jax.ShapeDtypeStruct((1, QK_PAD), F32)],
        scratch_shapes=[pltpu.VMEM(wq_shape, F32), pltpu.VMEM(wkv_shape, F32)],
        input_output_aliases={15: 0}, compiler_params=_params(("arbitrary",)),
    )(u, u, u, cos, sin, dq, dk, dv, dza, wq, wkv, qag, kvag, qg, kg, du)


def _flash_fwd(q, k, v):
    nh, s, _ = q.shape
    tq = min(TQ, s)
    tk = tq // 2
    nq = s // tq
    nch = Q_CHAINS
    tc = tq // nch

    def body(q_ref, k_ref, v_ref, o_ref, lse_ref):
        i = pl.program_id(1)
        chains = [q_ref[0, r * tc:(r + 1) * tc, :] for r in range(nch)]

        def scores(r, j, shift):
            sc = _dot_nt(chains[r], k_ref[0, pl.ds(pl.multiple_of(j * tk, tk), tk), :])
            if shift is not None:
                qi = lax.broadcasted_iota(jnp.int32, sc.shape, 0)
                ki = lax.broadcasted_iota(jnp.int32, sc.shape, 1) + shift
                sc = jnp.where(ki <= qi, sc, NEG)
            return sc

        def softmax(sc, m):
            m_new = jnp.maximum(m, jnp.max(sc, axis=-1, keepdims=True))
            return m_new, jnp.exp2(sc - m_new).astype(BF16), jnp.exp2(m - m_new)

        def run(units, carry):
            ms, accs = [c[0] for c in carry], [c[1] for c in carry]
            sc, pv = {}, {}
            for t in range(len(units) + 2):
                if t < len(units):
                    sc[t] = scores(*units[t])
                if 0 <= t - 1 < len(units):
                    r = units[t - 1][0]
                    ms[r], p, alpha = softmax(sc.pop(t - 1), ms[r])
                    pv[t - 1] = (p, alpha)
                if t - 2 >= 0:
                    r, j, _ = units[t - 2]
                    p, alpha = pv.pop(t - 2)
                    accs[r] = alpha * accs[r] + _dot(p, v_ref[0, pl.ds(pl.multiple_of(j * tk, tk), tk), :])
            return tuple(zip(ms, accs))

        def trip(p, carry):
            return run([(r, 2 * p + b, None) for b in range(2) for r in range(nch)], carry)

        init = (jnp.full((tc, 1), NEG, F32), jnp.zeros((tc, 2 * V_HEAD), F32))
        carry = lax.fori_loop(0, i, trip, (init,) * nch)
        diag = []
        for b in range(2):
            for r in range(nch):
                shift = b * tk - r * tc
                if shift < tc:
                    diag.append((r, 2 * i + b, None if shift + tk - 1 <= 0 else shift))
        carry = run(diag, carry)
        for r, (m, acc) in enumerate(carry):
            l = acc[:, V_HEAD:]
            o_ref[r * tc:(r + 1) * tc, :] = (acc[:, :V_HEAD] / l).astype(BF16)
            lse = m + jnp.log(l[:, 0:1]) * LOG2E
            lse_ref[0, :, r * tc:(r + 1) * tc] = jnp.broadcast_to(lse, (tc, LANES)).T[0:1, :]

    return pl.pallas_call(
        body, name="flash_fwd", grid=(nh, nq),
        in_specs=[pl.BlockSpec((1, tq, QK_PAD), lambda h, i: (h, i, 0)),
                  pl.BlockSpec((1, s, QK_PAD), lambda h, i: (h, 0, 0)),
                  pl.BlockSpec((1, s, 2 * V_HEAD), lambda h, i: (h, 0, 0))],
        out_specs=[pl.BlockSpec((tq, V_HEAD), lambda h, i: (i, h)), pl.BlockSpec((1, 1, tq), lambda h, i: (h, 0, i))],
        out_shape=[jax.ShapeDtypeStruct((s, nh * V_HEAD), BF16), jax.ShapeDtypeStruct((nh, 1, s), F32)],
        compiler_params=_params(("parallel", "arbitrary")),
    )(q, k, v)


def _flash_bwd(q, k, v, do, lse, delta):
    nh, s, _ = q.shape
    tq = min(TQ, s)
    nq = s // tq

    def body(q_ref, k_ref, v_ref, do_ref, lse_ref, dl_ref, dq_ref, dk_ref, dv_ref, dq_acc):
        j = pl.program_id(1)

        @pl.when(j == 0)
        def _():
            dq_acc[...] = jnp.zeros_like(dq_acc)

        kj, vj = k_ref[0], v_ref[0]

        def block(kk, vv, qq, dd, lse, dl, masked):
            st = _dot_nt(kk, qq)
            pt = jnp.exp2(st - lse)
            if masked:
                ki = lax.broadcasted_iota(jnp.int32, st.shape, 0)
                qx = lax.broadcasted_iota(jnp.int32, st.shape, 1)
                pt = jnp.where(ki <= qx, pt, 0.0)
            ddv = _dot(pt.astype(BF16), dd)
            dst = (pt * (_dot_nt(vv, dd) - dl)).astype(BF16)
            ddk = _dot(dst, qq)
            return ddk, ddv, _dot_tn(dst, kk)

        def step(i, carry):
            dk, dv = carry
            rows = pl.ds(pl.multiple_of(i * tq, tq), tq)
            ddk, ddv, ddq = block(kj, vj, q_ref[0, rows, :], do_ref[rows, :], lse_ref[0, pl.ds(i, 1), :],
                                  dl_ref[0, pl.ds(i, 1), :], False)
            dq_acc[rows, :] += ddq
            return dk + ddk, dv + ddv

        th = tq // 2
        lse_j, dl_j = lse_ref[0, pl.ds(j, 1), :], dl_ref[0, pl.ds(j, 1), :]
        parts = []
        for kh, qh, masked in ((0, 0, True), (0, 1, False), (1, 1, True)):
            rows = pl.ds(pl.multiple_of(j * tq + qh * th, th), th)
            ks, qs = slice(kh * th, (kh + 1) * th), slice(qh * th, (qh + 1) * th)
            ddk, ddv, ddq = block(kj[ks], vj[ks], q_ref[0, rows, :], do_ref[rows, :], lse_j[:, qs], dl_j[:, qs], masked)
            dq_acc[rows, :] += ddq
            parts.append((ddk, ddv))
        carry = (jnp.concatenate([parts[0][0] + parts[1][0], parts[2][0]], axis=0),
                 jnp.concatenate([parts[0][1] + parts[1][1], parts[2][1]], axis=0))
        dk, dv = lax.fori_loop(j + 1, nq, step, carry)
        dk_ref[0] = (dk * LN2).astype(BF16)
        dv_ref[0] = dv.astype(BF16)

        @pl.when(j == nq - 1)
        def _():
            dq_ref[0] = dq_acc[...].astype(BF16)

    return pl.pallas_call(
        body, name="flash_bwd", grid=(nh, nq),
        in_specs=[pl.BlockSpec((1, s, QK_PAD), lambda h, j: (h, 0, 0)),
                  pl.BlockSpec((1, tq, QK_PAD), lambda h, j: (h, j, 0)),
                  pl.BlockSpec((1, tq, V_HEAD), lambda h, j: (h, j, 0)),
                  pl.BlockSpec((s, V_HEAD), lambda h, j: (0, h)),
                  pl.BlockSpec((1, nq, tq), lambda h, j: (h, 0, 0)),
                  pl.BlockSpec((1, nq, tq), lambda h, j: (h, 0, 0))],
        out_specs=[pl.BlockSpec((1, s, QK_PAD), lambda h, j: (h, 0, 0)),
                   pl.BlockSpec((1, tq, QK_PAD), lambda h, j: (h, j, 0)),
                   pl.BlockSpec((1, tq, V_HEAD), lambda h, j: (h, j, 0))],
        out_shape=[jax.ShapeDtypeStruct((nh, s, QK_PAD), BF16), jax.ShapeDtypeStruct((nh, s, QK_PAD), BF16),
                   jax.ShapeDtypeStruct((nh, s, V_HEAD), BF16)],
        scratch_shapes=[pltpu.VMEM((s, QK_PAD), F32)],
        compiler_params=_params(("parallel", "arbitrary")),
    )(q, k, v, do, lse, delta)


def _tail(x, target, yc, o, u, mod, w_out):
    s, d = x.shape
    tm = min(TM_ELEM, s)

    def body(x_ref, t_ref, yc_ref, o_ref, za_ref, mod_ref, w_ref,
             gx_ref, dy_ref, ycat_ref, dyc_ref, do_ref, du_ref, delta_ref, dgate_ref, loss_ref):
        i = pl.program_id(0)
        za = pltpu.roll(za_ref[:, ZA_LO:ZA_LO + ZA_WIN].astype(F32), ZA_WIN - QK_ROPE, 1)[:, :D_ATTN]
        ov = o_ref[...].astype(F32)
        sg = _sigmoid(za)
        sl = za * sg
        ya = ov * sl
        ycat = jnp.concatenate([yc_ref[...], ya.astype(BF16)], axis=1)
        ycat_ref[...] = jnp.concatenate([yc_ref[...].astype(F32).T, ya.T], axis=0).astype(BF16)
        y = _dot(ycat, w_ref[...])
        gate = mod_ref[:, 2 * d:3 * d]
        e = x_ref[...] + gate * y - t_ref[...]
        dout = e * (1.0 / d)
        gx_ref[...] = dout
        dy = (dout * gate).astype(BF16)
        dy_ref[...] = dy
        dycat = _dot_nt(dy, w_ref[...])
        dyc_ref[...] = dycat[:, :D_CONV].astype(BF16)
        dya = dycat[:, D_CONV:]
        dov = dya * sl
        do_ref[...] = dov.astype(BF16)
        du_ref[...] = (dya * ov * (sg * (1.0 + za * (1.0 - sg)))).astype(BF16)
        prod_t = (dov * ov).T
        for h in range(N_HEADS):
            delta_ref[h] = jnp.sum(prod_t[V_HEAD * h:V_HEAD * (h + 1), :], axis=0, keepdims=True)
        dgate = jnp.sum(dout * y, axis=0, keepdims=True)
        part = jnp.sum(jnp.sum(e * e, axis=0, keepdims=True), axis=1, keepdims=True) * (0.5 / d)
        part = jnp.broadcast_to(part, (1, LANES))

        @pl.when(i == 0)
        def _():
            dgate_ref[...] = dgate
            loss_ref[...] = part

        @pl.when(i > 0)
        def _():
            dgate_ref[...] += dgate
            loss_ref[...] += part

    tok = lambda w: pl.BlockSpec((tm, w), lambda i: (i, 0))
    return pl.pallas_call(
        body, name="tail", grid=(s // tm,),
        in_specs=[tok(d), tok(d), tok(D_CONV), tok(D_ATTN), pl.BlockSpec((tm, U_TAIL), lambda i: (i, U_COLS // U_TAIL - 1)),
                  _full((1, 3 * d)), _full((d, d))],
        out_specs=[tok(d), tok(d), pl.BlockSpec((d, tm), lambda i: (0, i)), tok(D_CONV), tok(D_ATTN), tok(D_ATTN),
                   pl.BlockSpec((N_HEADS, 1, tm), lambda i: (0, 0, i)), _full((1, d)), _full((1, LANES))],
        out_shape=[jax.ShapeDtypeStruct((s, d), F32), jax.ShapeDtypeStruct((s, d), BF16),
                   jax.ShapeDtypeStruct((d, s), BF16), jax.ShapeDtypeStruct((s, D_CONV), BF16),
                   jax.ShapeDtypeStruct((s, D_ATTN), BF16), jax.ShapeDtypeStruct((s, D_ATTN), BF16),
                   jax.ShapeDtypeStruct((N_HEADS, 1, s), F32), jax.ShapeDtypeStruct((1, d), F32),
                   jax.ShapeDtypeStruct((1, LANES), F32)],
        compiler_params=_params(("arbitrary",)),
    )(x, target, yc, o, u, mod, w_out)


def _norm_bwd(x, dh, gx1, norm_g, mod):
    s, d = x.shape
    tm = min(TM_MM, s)

    def body(x_ref, dh_ref, gx_ref, g_ref, mod_ref, o_ref, dshift_ref, dscale_ref, dg_ref):
        i = pl.program_id(0)
        xv, dhv, gv = x_ref[...], dh_ref[...].astype(F32), g_ref[...]
        r = lax.rsqrt(jnp.mean(xv * xv, axis=-1, keepdims=True) + EPS)
        xn = xv * r
        dhn = dhv * (1.0 + mod_ref[:, d:2 * d])
        dxn = dhn * gv
        o_ref[...] = gx_ref[...] + r * (dxn - xn * jnp.mean(dxn * xn, axis=-1, keepdims=True))
        dshift = jnp.sum(dhv, axis=0, keepdims=True)
        dscale = jnp.sum(dhv * xn * gv, axis=0, keepdims=True)
        dg = jnp.sum(dhn * xn, axis=0, keepdims=True)

        @pl.when(i == 0)
        def _():
            dshift_ref[...] = dshift
            dscale_ref[...] = dscale
            dg_ref[...] = dg

        @pl.when(i > 0)
        def _():
            dshift_ref[...] += dshift
            dscale_ref[...] += dscale
            dg_ref[...] += dg

    tok = pl.BlockSpec((tm, d), lambda i: (i, 0))
    row = jax.ShapeDtypeStruct((1, d), F32)
    return pl.pallas_call(
        body, name="norm_bwd", grid=(s // tm,),
        in_specs=[tok, tok, tok, _full((1, d)), _full((1, 3 * d))],
        out_specs=[tok, _full((1, d)), _full((1, d)), _full((1, d))],
        out_shape=[jax.ShapeDtypeStruct((s, d), F32), row, row, row],
        compiler_params=_params(("arbitrary",)),
    )(x, dh, gx1, norm_g, mod)


def _adamw(w, g, m, v, name):
    rows, cols = w.shape
    tr = 256 if rows % 256 == 0 else rows

    def body(w_ref, g_ref, m_ref, v_ref, d_ref, nm_ref, nv_ref):
        gv = g_ref[...]
        nm = ADAM_B1 * m_ref[...] + (1.0 - ADAM_B1) * gv
        nv = ADAM_B2 * v_ref[...] + (1.0 - ADAM_B2) * (gv * gv)
        m_hat = nm / (1.0 - ADAM_B1 ** ADAM_STEP)
        v_hat = nv / (1.0 - ADAM_B2 ** ADAM_STEP)
        d_ref[...] = -ADAM_LR * (m_hat / (jnp.sqrt(v_hat) + ADAM_EPS) + ADAM_WD * w_ref[...])
        nm_ref[...] = nm
        nv_ref[...] = nv

    spec = pl.BlockSpec((tr, cols), lambda i: (i, 0))
    shape = jax.ShapeDtypeStruct((rows, cols), F32)
    return pl.pallas_call(
        body, name=name, grid=(rows // tr,), in_specs=[spec] * 4, out_specs=[spec] * 3, out_shape=[shape] * 3,
        compiler_params=_params(("parallel",)),
    )(w, g, m, v)


def _pad_cols(a, n):
    return jnp.pad(a, ((0, 0), (0, n - a.shape[1])))


def kernel(x, c, positions, ada_w, ada_b, norm_g, w_in, conv_w, q_a_g, w_q_b, kv_a_g, w_kv_b, q_g, k_g, w_out, loss_target, m_ada_w, m_ada_b, m_norm_g, m_w_in, m_conv_w, m_q_a_g, m_w_q_b, m_kv_a_g, m_w_kv_b, m_q_g, m_k_g, m_w_out, v_ada_w, v_ada_b, v_norm_g, v_w_in, v_conv_w, v_q_a_g, v_w_q_b, v_kv_a_g, v_w_kv_b, v_q_g, v_k_g, v_w_out):
    me = _my_index()
    s = x.shape[1]
    nq = s // min(TQ, s)
    x2, tgt = x[0], loss_target[0]
    w_in_l, w_q_l, w_kv_l, w_out_l, conv_l, ada_w_l = w_in[0], w_q_b[0], w_kv_b[0], w_out[0], conv_w[0], ada_w[0]
    ada_cols = ada_w_l.shape[1]

    small = jnp.concatenate([c.reshape(-1, LANES), conv_l.reshape(-1, LANES), jnp.zeros((5, LANES), F32)], axis=0)
    (small_g,) = _all_gather([small], "gather_c")
    c_all = small_g[:, :D_MODEL // LANES].reshape(N_DEV, D_MODEL)
    conv_g = small_g[:, D_MODEL // LANES:D_MODEL // LANES + 3].transpose(1, 0, 2).reshape(3, D_CONV)

    ada_b_l = lax.dynamic_slice(ada_b, (0, me * ada_cols), (1, ada_cols))
    mod_cols = _ada_mod(jnp.pad(c_all, ((0, 8), (0, 0))), ada_w_l, ada_b_l)[:N_DEV]
    (mod_g,) = _all_gather([mod_cols], "gather_mod")
    mod = lax.dynamic_index_in_dim(mod_g, me, axis=1, keepdims=False).reshape(1, 3 * D_MODEL)

    half = jnp.arange(0, QK_ROPE, 2, dtype=F32) / QK_ROPE
    inv_freq = ROPE_BASE ** (-half)
    zeros64 = jnp.zeros((LANES - QK_ROPE,), F32)
    invf = jnp.concatenate([inv_freq, inv_freq, zeros64]).reshape(1, LANES)
    sign = jnp.concatenate([-jnp.ones((32,), F32), jnp.ones((32,), F32), zeros64]).reshape(1, LANES)
    qg_p, kg_p = _pad_cols(q_g, QK_PAD), _pad_cols(k_g, QK_PAD)

    my_off = ((CW * me) % LANES).astype(jnp.int32)
    win = [_expand_w_in(w_in_l, my_off.reshape(1))]
    h, h_t, cos, sin, win_g = _norm_mod(x2, norm_g, mod, positions.reshape(s, 1), invf, sign, _Gather(win, relay=True), win)
    w_in_p = _merge_w_in(win_g)
    rest = [_pad_wq(w_q_l), w_kv_l.astype(BF16), w_out_l.astype(BF16)]
    u, wq_g, wkv_g, w_out_g = _matmul(h, w_in_p, nt=False, out_dtype=BF16, tm=2 * TM_MM, tn=1024, name="in_proj",
                                      rider=_Gather(rest), rider_inputs=rest)
    w_out_g = w_out_g.reshape(D_MODEL, D_MODEL)
    wq_g = wq_g.transpose(1, 0, 2).reshape(Q_LORA, N_HEADS * QK_PAD)
    wkv_g = wkv_g.transpose(1, 0, 2).reshape(KV_LORA, 2 * D_ATTN)
    yc = _conv_fwd(u, conv_g)
    q, k, v = _qkv_fwd(u, cos, sin, wq_g, wkv_g, q_a_g, kv_a_g, qg_p, kg_p)
    o, lse = _flash_fwd(q, k, v)
    gx1, dy, ycat_t, dyc, do, dza, delta, dgate, loss_row = _tail(x2, tgt, yc, o, u, mod, w_out_g)

    dq, dk, dv = _flash_bwd(q, k, v, do, lse.reshape(N_HEADS, nq, s // nq), delta.reshape(N_HEADS, nq, s // nq))
    du, dconv = _conv_bwd(u, dyc, conv_g)
    du, dwq, dwkv, dqag, dkvag, dqg, dkg = _qkv_bwd(u, cos, sin, dq, dk, dv, dza, wq_g, wkv_g, q_a_g, kv_a_g, qg_p, kg_p, du)
    dwq = dwq.reshape(Q_LORA, N_HEADS, QK_PAD).transpose(1, 0, 2)
    dwkv = dwkv.reshape(KV_LORA, N_HEADS, 2 * V_HEAD).transpose(1, 0, 2)
    dw_out = _matmul(ycat_t, dy, nt=False, out_dtype=BF16, tm=TM_MM, tn=512, name="dw_out")
    dw_out = dw_out.reshape(N_DEV, D_MODEL // N_DEV, D_MODEL)
    dw_in = _matmul(h_t, du, nt=False, out_dtype=BF16, tm=TM_MM, tn=512, name="dw_in")

    core = lax.axis_index("c").astype(jnp.int32)
    r_in, r_q, r_kv, r_out = _rs_sibling(dw_in, dwq, dwkv, dw_out)
    lo_tiles = ((CW * (2 * jnp.arange(4, dtype=jnp.int32) + core)) // LANES).astype(jnp.int32)
    pairs = [_add_window(dw_in, r_in, lo_tiles), _add_pairs(dwq, r_q, core.reshape(1), "rs_add_q"),
             _add_pairs(dwkv, r_kv, core.reshape(1), "rs_add_kv"), _add_pairs(dw_out, r_out, core.reshape(1), "rs_add_out")]
    dh, *quads = _matmul(du, w_in_p, nt=True, out_dtype=BF16, tm=2 * TM_MM, tn=512, name="dh",
                         rider=_ChipExchange(pairs), rider_inputs=pairs)
    my_chip = 2 * lax.axis_index("x") + lax.axis_index("y")
    written = jnp.where(jnp.arange(4) == my_chip, (jnp.arange(4) + 1) % 4, jnp.arange(4))
    sel = jnp.concatenate([my_chip.reshape(1), written, ((EXP_W - my_off) % EXP_W).reshape(1)]).astype(jnp.int32)
    g_w_in = _final_sum(pairs[0], quads[0], sel, "rs_sum_in", unshift_to=CW)
    g_w_q = _final_sum(pairs[1], quads[1], sel, "rs_sum_q")[:, :QK_HEAD]
    g_w_kv = _final_sum(pairs[2], quads[2], sel, "rs_sum_kv")
    g_w_out = _final_sum(pairs[3], quads[3], sel, "rs_sum_out")
    grad_x, dshift, dscale, dng = _norm_bwd(x2, dh, gx1, norm_g, mod)

    row = jnp.concatenate([dshift, dscale, dgate, dng, dqag, dkvag, dqg, dkg, dconv[:3].reshape(1, 3 * D_CONV), loss_row], axis=1)
    (rows_g,) = _all_gather([row], "gather_small")
    tot = _sum_leading(rows_g, F32, "sum_small")
    dmod_all = rows_g[:, 0, SM_MOD:SM_NG]
    g_ada_b = tot[:, SM_MOD:SM_NG]
    g_norm_g = tot[:, SM_NG:SM_QAG]
    g_q_a_g = tot[:, SM_QAG:SM_KVAG]
    g_kv_a_g = tot[:, SM_KVAG:SM_QG]
    g_q_g = tot[:, SM_QG:SM_QG + QK_HEAD]
    g_k_g = tot[:, SM_KG:SM_KG + QK_HEAD]
    conv_cols = conv_l.shape[1]
    g_conv = lax.dynamic_slice(tot[:, SM_CONV:SM_LOSS].reshape(3, D_CONV), (0, me * conv_cols), (3, conv_cols))
    loss = tot[0, SM_LOSS]
    dmod_my = lax.dynamic_slice(dmod_all, (0, me * ada_cols), (N_DEV, ada_cols))
    g_ada_w = _ada_w_grad(c_all.T, dmod_my)

    grads = dict(ada_w=g_ada_w, ada_b=g_ada_b, norm_g=g_norm_g, w_in=g_w_in, conv_w=g_conv, q_a_g=g_q_a_g, w_q_b=g_w_q,
                 kv_a_g=g_kv_a_g, w_kv_b=g_w_kv, q_g=g_q_g, k_g=g_k_g, w_out=g_w_out)
    weights = dict(ada_w=(ada_w, m_ada_w, v_ada_w), ada_b=(ada_b, m_ada_b, v_ada_b), norm_g=(norm_g, m_norm_g, v_norm_g),
                   w_in=(w_in, m_w_in, v_w_in), conv_w=(conv_w, m_conv_w, v_conv_w), q_a_g=(q_a_g, m_q_a_g, v_q_a_g),
                   w_q_b=(w_q_b, m_w_q_b, v_w_q_b), kv_a_g=(kv_a_g, m_kv_a_g, v_kv_a_g), w_kv_b=(w_kv_b, m_w_kv_b, v_w_kv_b),
                   q_g=(q_g, m_q_g, v_q_g), k_g=(k_g, m_k_g, v_k_g), w_out=(w_out, m_w_out, v_w_out))
    names = list(grads)
    out_g, out_d, out_m, out_v = [], [], [], []
    for n in names:
        w, m, v_ = weights[n]
        shape2 = w.shape[-2:] if w.ndim == 3 else (1, w.shape[-1])
        g2 = grads[n].reshape(shape2)
        d2, m2, v2 = _adamw(w.reshape(shape2), g2, m.reshape(shape2), v_.reshape(shape2), "adamw_" + n)
        out_g.append(g2.reshape(w.shape))
        out_d.append(d2.reshape(w.shape))
        out_m.append(m2.reshape(w.shape))
        out_v.append(v2.reshape(w.shape))
    return (loss, grad_x.reshape(x.shape), *out_g, *out_d, *out_m, *out_v)
```

```python
import functools
import math

import jax
import jax.numpy as jnp
from jax import lax
from jax.experimental import pallas as pl
from jax.experimental.pallas import tpu as pltpu

F32 = jnp.float32
BF16 = jnp.bfloat16
MESH = pl.DeviceIdType.MESH

D_MODEL = 2048
D_CONV = 1024
N_HEADS = 8
QK_NOPE = 128
QK_ROPE = 64
QK_HEAD = QK_NOPE + QK_ROPE
V_HEAD = 128
D_ATTN = N_HEADS * V_HEAD
Q_LORA = 512
KV_LORA = 256
ROPE_BASE = 10000.0
IN_COLS = 4 * D_CONV + Q_LORA + KV_LORA + QK_ROPE + D_ATTN
EPS = 1e-6
ADAM_LR, ADAM_B1, ADAM_B2, ADAM_EPS, ADAM_WD, ADAM_STEP = 0.001, 0.9, 0.999, 1e-08, 0.01, 10

N_DEV = 8
LANES = 128
QK_PAD = 256
U_COLS = 6144
U_CQ, U_CKV, U_KR, U_ZA = 4096, 4608, 4864, 4928
U_TAIL = 2048
ZA_LO = U_ZA - (U_COLS - U_TAIL) - QK_ROPE
ZA_WIN = D_ATTN + LANES
CW = IN_COLS // 8
EXP_W = 896
W_LO = [(CW * d // 128) * 128 for d in range(8)]
W_OFF = [CW * d - lo for d, lo in enumerate(W_LO)]
SCALE = 1.0 / math.sqrt(QK_HEAD)
LOG2E = 1.4426950408889634
LN2 = 0.6931471805599453
NEG = -1e30
VMEM_LIMIT = 56 * 1024 * 1024

TM_ELEM = 256
TM_QKV_BWD = 256
TM_MM = 512
TQ = 1024
Q_CHAINS = 4
KV_SPLIT = 2

SM_MOD, SM_NG, SM_QAG, SM_KVAG, SM_QG, SM_KG, SM_CONV, SM_LOSS = 0, 6144, 8192, 8704, 8960, 9216, 9472, 12544
SM_COLS = 12672


def _params(sem=None):
    kw = dict(vmem_limit_bytes=VMEM_LIMIT)
    if sem is not None:
        kw["dimension_semantics"] = sem
    return pltpu.CompilerParams(**kw)


def _sigmoid(z):
    return 1.0 / (1.0 + jnp.exp(-z))


def _rot64(x):
    lane = lax.broadcasted_iota(jnp.int32, x.shape, 1)
    return jnp.where(lane < 32, pltpu.roll(x, 96, 1), pltpu.roll(x, 32, 1))


def _rope(x, cos, sin):
    return x * cos + _rot64(x) * sin


def _rope_t(d, cos, sin):
    return d * cos - _rot64(d) * sin


def _dot(a, b):
    return jnp.dot(a, b, preferred_element_type=F32)


def _dot_nt(a, b):
    return lax.dot_general(a, b, (((1,), (1,)), ((), ())), preferred_element_type=F32)


def _dot_tn(a, b):
    return lax.dot_general(a, b, (((0,), (0,)), ((), ())), preferred_element_type=F32)


def _my_index():
    return 4 * lax.axis_index("x") + 2 * lax.axis_index("y") + lax.axis_index("c")


ANY = pl.BlockSpec(memory_space=pl.ANY)


class _Gather:
    def __init__(self, blocks, relay=False, parts=1):
        self.relay = relay
        self.parts = parts
        self.rows = [b.shape[0] // parts for b in blocks]
        self.n = n = len(blocks) * parts
        self.out_shape = [jax.ShapeDtypeStruct((N_DEV,) + b.shape, b.dtype) for b in blocks]
        self.scratch = [pltpu.SemaphoreType.DMA((7 * n,)), pltpu.SemaphoreType.DMA((7 * n,)),
                        pltpu.SemaphoreType.DMA((n,))]

    @staticmethod
    def _places():
        x, y, c = lax.axis_index("x"), lax.axis_index("y"), lax.axis_index("c")
        return (x, y, c), (x, y, 1 - c), [(1 - x, y), (x, 1 - y), (1 - x, 1 - y)]

    def _src(self, ins, a):
        block, part = divmod(a, self.parts)
        return ins[block] if self.parts == 1 else ins[block].at[pl.ds(part * self.rows[block], self.rows[block])]

    def _dst(self, outs, a, place):
        block, part = divmod(a, self.parts)
        ref = outs[block].at[4 * place[0] + 2 * place[1] + place[2]]
        return ref if self.parts == 1 else ref.at[pl.ds(part * self.rows[block], self.rows[block])]

    def _copy(self, outs, sems, a, k, block, to, src=None):
        dst = self._dst(outs, a, block)
        return pltpu.make_async_remote_copy(
            src_ref=dst if src is None else src, dst_ref=dst, send_sem=sems[0].at[7 * a + k],
            recv_sem=sems[1].at[7 * a + k], device_id=to, device_id_type=MESH)

    def _first(self, ins, outs, sems):
        me, sibling, chips = self._places()
        first = []
        for a in range(self.n):
            first.append(self._copy(outs, sems, a, 0, me, sibling, src=self._src(ins, a)))
            first += [self._copy(outs, sems, a, 1 + j, me, (*chip, me[2]), src=self._src(ins, a))
                      for j, chip in enumerate(chips[:2] if self.relay else chips)]
        return first

    def _relays(self, outs, sems):
        if not self.relay:
            return []
        (x, y, c), _, _ = self._places()
        via = (jnp.where(c == 0, 1 - x, x), jnp.where(c == 0, y, 1 - y))
        to = (jnp.where(c == 0, x, 1 - x), jnp.where(c == 0, 1 - y, y))
        return [self._copy(outs, sems, a, 3, (*via, c), (*to, c)) for a in range(self.n)]

    def _passed(self, outs, sems):
        me, sibling, chips = self._places()
        return [self._copy(outs, sems, a, 4 + j, (*chip, me[2]), sibling)
                for a in range(self.n) for j, chip in enumerate(chips)]

    def _mine(self, ins, outs, sems):
        me, _, _ = self._places()
        return [pltpu.make_async_copy(self._src(ins, a), self._dst(outs, a, me), sems[2].at[a]) for a in range(self.n)]

    def start(self, ins, outs, sems):
        for cp in self._mine(ins, outs, sems) + self._first(ins, outs, sems):
            cp.start()

    def forward(self, ins, outs, sems):
        del ins
        me, _, chips = self._places()
        passed, relays = self._passed(outs, sems), self._relays(outs, sems)
        for a in range(self.n):
            for j, chip in enumerate(chips[:2] if self.relay else chips):
                self._copy(outs, sems, a, 1 + j, (*chip, me[2]), me).wait_recv()
                passed[3 * a + j].start()
            if self.relay:
                relays[a].start()
        if self.relay:
            for a in range(self.n):
                self._copy(outs, sems, a, 3, (*chips[2], me[2]), me).wait_recv()
                passed[3 * a + 2].start()

    def finish(self, ins, outs, sems):
        me, sibling, chips = self._places()
        for a in range(self.n):
            self._copy(outs, sems, a, 0, sibling, me).wait_recv()
            for j, chip in enumerate(chips):
                self._copy(outs, sems, a, 4 + j, (*chip, 1 - me[2]), me).wait_recv()
        for cp in self._first(ins, outs, sems) + self._relays(outs, sems) + self._passed(outs, sems):
            cp.wait_send()
        for cp in self._mine(ins, outs, sems):
            cp.wait()


class _ChipExchange:
    def __init__(self, arrays):
        self.n = n = len(arrays)
        self.out_shape = [jax.ShapeDtypeStruct(a.shape, a.dtype) for a in arrays]
        self.scratch = [pltpu.SemaphoreType.DMA((3 * n,)), pltpu.SemaphoreType.DMA((3 * n,))]

    def _copies(self, ins, outs, sems):
        x, y, c = lax.axis_index("x"), lax.axis_index("y"), lax.axis_index("c")
        return [pltpu.make_async_remote_copy(
            src_ref=ins[a].at[2 * px + py], dst_ref=outs[a].at[2 * x + y], send_sem=sems[0].at[3 * a + j],
            recv_sem=sems[1].at[3 * a + j], device_id=(px, py, c), device_id_type=MESH)
            for a in range(self.n) for j, (px, py) in enumerate([(1 - x, y), (x, 1 - y), (1 - x, 1 - y)])]

    def start(self, ins, outs, sems):
        for cp in self._copies(ins, outs, sems):
            cp.start()

    def forward(self, ins, outs, sems):
        pass

    def finish(self, ins, outs, sems):
        for cp in self._copies(ins, outs, sems):
            cp.wait()


def _all_gather(blocks, name):
    n = len(blocks)
    g = _Gather(blocks)

    def body(*refs):
        ins, outs, sems = refs[:n], refs[n:2 * n], refs[2 * n:]
        g.start(ins, outs, sems)
        g.forward(ins, outs, sems)
        g.finish(ins, outs, sems)

    return pl.pallas_call(body, name=name, out_shape=g.out_shape, in_specs=[ANY] * n, out_specs=[ANY] * n,
                          scratch_shapes=g.scratch)(*blocks)


class _SiblingExchange:
    def __init__(self, arrays, windowed):
        self.n = n = len(arrays)
        self.windowed = windowed
        self.out_shape = [jax.ShapeDtypeStruct((4, a.shape[0], EXP_W) if w else (4,) + a.shape[1:], a.dtype)
                          for a, w in zip(arrays, windowed)]
        self.scratch = [pltpu.SemaphoreType.DMA((4 * n,)), pltpu.SemaphoreType.DMA((4 * n,))]

    def _each(self, ins, outs, sems, act):
        x, y, c = lax.axis_index("x"), lax.axis_index("y"), lax.axis_index("c")

        def branch(c_val):
            for k in range(4):
                e = 2 * k + (1 - c_val)
                for a in range(self.n):
                    src = ins[a].at[:, pl.ds(W_LO[e], EXP_W)] if self.windowed[a] else ins[a].at[e]
                    act(pltpu.make_async_remote_copy(
                        src_ref=src, dst_ref=outs[a].at[k], send_sem=sems[0].at[4 * a + k], recv_sem=sems[1].at[4 * a + k],
                        device_id=(x, y, 1 - c), device_id_type=MESH))

        for c_val in (0, 1):
            pl.when(c == c_val)(functools.partial(branch, c_val))

    def start(self, ins, outs, sems):
        self._each(ins, outs, sems, lambda cp: cp.start())

    def forward(self, ins, outs, sems):
        pass

    def finish(self, ins, outs, sems):
        self._each(ins, outs, sems, lambda cp: cp.wait())


def _exchange(rider, arrays, name):
    n = len(arrays)

    def body(*refs):
        ins, outs, sems = refs[:n], refs[n:n + len(rider.out_shape)], refs[n + len(rider.out_shape):]
        rider.start(ins, outs, sems)
        rider.forward(ins, outs, sems)
        rider.finish(ins, outs, sems)

    return pl.pallas_call(body, name=name, out_shape=rider.out_shape, in_specs=[ANY] * n,
                          out_specs=[ANY] * len(rider.out_shape), scratch_shapes=rider.scratch)(*arrays)


def _add_window(dw_in, recv, lo_tiles):
    k, rows, _ = recv.shape

    def body(t_ref, w_ref, r_ref, o_ref):
        del t_ref
        o_ref[0] = (w_ref[...].astype(F32) + r_ref[0].astype(F32)).astype(o_ref.dtype)

    spec = pl.BlockSpec((1, rows, LANES), lambda i, j, t: (i, 0, j))
    grid_spec = pltpu.PrefetchScalarGridSpec(
        num_scalar_prefetch=1, grid=(k, EXP_W // LANES),
        in_specs=[pl.BlockSpec((rows, LANES), lambda i, j, t: (0, t[i] + j)), spec], out_specs=spec)
    return pl.pallas_call(
        body, name="rs_add_in", grid_spec=grid_spec, out_shape=jax.ShapeDtypeStruct(recv.shape, recv.dtype),
        compiler_params=_params(("parallel", "parallel")),
    )(lo_tiles, dw_in, recv)


def _final_sum(p, r, sel, name, unshift_to=None):
    _, rows, cols = p.shape
    tr = 512 if rows % 512 == 0 else rows
    out_cols = cols if unshift_to is None else unshift_to

    def body(sel_ref, p_ref, r0, r1, r2, r3, o_ref):
        own = p_ref[0].astype(F32)
        acc = None
        for k, r_ref in enumerate((r0, r1, r2, r3)):
            term = jnp.where(sel_ref[0] == k, own, r_ref[0].astype(F32))
            acc = term if acc is None else acc + term
        if unshift_to is not None:
            acc = pltpu.roll(acc, sel_ref[5], 1)[:, :unshift_to]
        o_ref[...] = acc

    def slot(k):
        return pl.BlockSpec((1, tr, cols), lambda i, t: (t[k], i, 0))

    grid_spec = pltpu.PrefetchScalarGridSpec(
        num_scalar_prefetch=1, grid=(rows // tr,), in_specs=[slot(0), slot(1), slot(2), slot(3), slot(4)],
        out_specs=pl.BlockSpec((tr, out_cols), lambda i, t: (i, 0)))
    return pl.pallas_call(
        body, name=name, grid_spec=grid_spec, out_shape=jax.ShapeDtypeStruct((rows, out_cols), F32),
        compiler_params=_params(("parallel",)),
    )(sel, p, r, r, r, r)


def _expand_w_in(w, shift):
    rows, cw = w.shape
    tr = 256

    def body(s_ref, w_ref, o_ref, buf):
        buf[...] = jnp.zeros_like(buf)
        buf[:, 0:cw] = w_ref[...]
        o_ref[...] = pltpu.roll(buf[...], s_ref[0], 1).astype(BF16)

    grid_spec = pltpu.PrefetchScalarGridSpec(
        num_scalar_prefetch=1, grid=(rows // tr,), in_specs=[pl.BlockSpec((tr, cw), lambda i, t: (i, 0))],
        out_specs=pl.BlockSpec((tr, EXP_W), lambda i, t: (i, 0)), scratch_shapes=[pltpu.VMEM((tr, EXP_W), F32)])
    return pl.pallas_call(
        body, name="expand_w_in", grid_spec=grid_spec, out_shape=jax.ShapeDtypeStruct((rows, EXP_W), BF16),
        compiler_params=_params(("arbitrary",)),
    )(shift, w)


def _pad_wq(w):
    rows, cw = w.shape

    def body(w_ref, o_ref, buf):
        buf[...] = jnp.zeros_like(buf)
        buf[:, 0:cw] = w_ref[...]
        o_ref[...] = buf[...].astype(BF16)

    return pl.pallas_call(
        body, name="pad_wq", out_shape=jax.ShapeDtypeStruct((rows, QK_PAD), BF16),
        scratch_shapes=[pltpu.VMEM((rows, QK_PAD), F32)], compiler_params=_params(),
    )(w)


def _merge_w_in(e):
    _, rows, _ = e.shape
    tr = 256

    def body(e_ref, o_ref):
        for t in range(U_COLS // LANES):
            lo, hi = t * LANES, (t + 1) * LANES
            parts = [e_ref[d, :, lo - W_LO[d]:hi - W_LO[d]] for d in range(N_DEV)
                     if CW * d < hi and CW * (d + 1) > lo]
            if not parts:
                tile = jnp.zeros((tr, LANES), BF16)
            elif len(parts) == 1:
                tile = parts[0]
            else:
                tile = (parts[0].astype(F32) + parts[1].astype(F32)).astype(BF16)
            o_ref[:, lo:hi] = tile

    return pl.pallas_call(
        body, name="merge_w_in", grid=(rows // tr,),
        in_specs=[pl.BlockSpec((N_DEV, tr, EXP_W), lambda i: (0, i, 0))],
        out_specs=pl.BlockSpec((tr, U_COLS), lambda i: (i, 0)), out_shape=jax.ShapeDtypeStruct((rows, U_COLS), BF16),
        compiler_params=_params(("parallel",)),
    )(e)


def _sum_leading(a, out_dtype, name):
    k, rows, cols = a.shape
    tr = min(rows, 1728 if rows % 1728 == 0 else rows)

    def body(a_ref, o_ref):
        acc = a_ref[0].astype(F32)
        for i in range(1, k):
            acc = acc + a_ref[i].astype(F32)
        o_ref[...] = acc.astype(out_dtype)

    return pl.pallas_call(
        body, name=name, grid=(rows // tr,),
        in_specs=[pl.BlockSpec((k, tr, cols), lambda i: (0, i, 0))],
        out_specs=pl.BlockSpec((tr, cols), lambda i: (i, 0)),
        out_shape=jax.ShapeDtypeStruct((rows, cols), out_dtype), compiler_params=_params(("parallel",)),
    )(a)


def _add_pairs(g, recv, core, name):
    k, rows, cols = recv.shape
    tr = 1728 if rows % 1728 == 0 else rows

    def body(c_ref, g_ref, r_ref, o_ref):
        del c_ref
        o_ref[...] = (g_ref[...].astype(F32) + r_ref[...].astype(F32)).astype(o_ref.dtype)

    spec = pl.BlockSpec((1, tr, cols), lambda i, j, c: (i, j, 0))
    grid_spec = pltpu.PrefetchScalarGridSpec(
        num_scalar_prefetch=1, grid=(k, rows // tr),
        in_specs=[pl.BlockSpec((1, tr, cols), lambda i, j, c: (2 * i + c[0], j, 0)), spec], out_specs=spec)
    return pl.pallas_call(
        body, name=name, grid_spec=grid_spec, out_shape=jax.ShapeDtypeStruct(recv.shape, recv.dtype),
        compiler_params=_params(("parallel", "parallel")),
    )(core, g, recv)


def _ada_mod(c16, ada_w_l, ada_b_l):
    def body(c_ref, w_ref, b_ref, o_ref):
        cv = c_ref[...]
        sc = (cv * _sigmoid(cv)).astype(BF16)
        o_ref[...] = _dot(sc, w_ref[...].astype(BF16)) + b_ref[...]

    return pl.pallas_call(
        body, name="ada_mod", out_shape=jax.ShapeDtypeStruct((c16.shape[0], ada_w_l.shape[1]), F32),
        compiler_params=_params(),
    )(c16, ada_w_l, ada_b_l)


def _ada_w_grad(c_t, dmod_my):
    def body(c_ref, d_ref, o_ref):
        cv = c_ref[...]
        sc = cv * _sigmoid(cv)
        acc = sc[:, 0:1] * d_ref[0:1, :]
        for b in range(1, N_DEV):
            acc = acc + sc[:, b:b + 1] * d_ref[b:b + 1, :]
        o_ref[...] = acc

    return pl.pallas_call(
        body, name="ada_w_grad", out_shape=jax.ShapeDtypeStruct((c_t.shape[0], dmod_my.shape[1]), F32),
        compiler_params=_params(),
    )(c_t, dmod_my)


def _norm_mod(x, norm_g, mod, pos_col, invf, sign, rider, rider_inputs):
    s, d = x.shape
    tm = min(TM_MM, s)
    n_in, n_out = len(rider_inputs), len(rider.out_shape)
    steps = s // tm

    def body(x_ref, g_ref, mod_ref, p_ref, f_ref, s_ref, *rest):
        r_ins, (h_ref, ht_ref, cos_ref, sin_ref) = rest[:n_in], rest[n_in:n_in + 4]
        r_outs, sems = rest[n_in + 4:n_in + 4 + n_out], rest[n_in + 4 + n_out:]
        pl.when(pl.program_id(0) == 0)(functools.partial(rider.start, r_ins, r_outs, sems))
        xv = x_ref[...]
        r = lax.rsqrt(jnp.mean(xv * xv, axis=-1, keepdims=True) + EPS)
        hn = xv * r * g_ref[...]
        hv = hn * (1.0 + mod_ref[:, d:2 * d]) + mod_ref[:, 0:d]
        h_ref[...] = hv.astype(BF16)
        ht_ref[...] = hv.T.astype(BF16)
        ang = p_ref[...].astype(F32) * f_ref[...]
        sg = s_ref[...]
        cos_ref[...] = jnp.cos(ang) * jnp.abs(sg)
        sin_ref[...] = jnp.sin(ang) * sg

        @pl.when(pl.program_id(0) == steps - 1)
        def _():
            rider.forward(r_ins, r_outs, sems)
            rider.finish(r_ins, r_outs, sems)

    row = pl.BlockSpec((1, LANES), lambda i: (0, 0))
    tab = pl.BlockSpec((tm, LANES), lambda i: (i, 0))
    return pl.pallas_call(
        body, name="norm_mod", grid=(steps,),
        in_specs=[pl.BlockSpec((tm, d), lambda i: (i, 0)), pl.BlockSpec((1, d), lambda i: (0, 0)),
                  pl.BlockSpec((1, 3 * d), lambda i: (0, 0)), pl.BlockSpec((tm, 1), lambda i: (i, 0)), row, row]
        + [ANY] * n_in,
        out_specs=[pl.BlockSpec((tm, d), lambda i: (i, 0)), pl.BlockSpec((d, tm), lambda i: (0, i)), tab, tab] + [ANY] * n_out,
        out_shape=[jax.ShapeDtypeStruct((s, d), BF16), jax.ShapeDtypeStruct((d, s), BF16),
                   jax.ShapeDtypeStruct((s, LANES), F32), jax.ShapeDtypeStruct((s, LANES), F32)] + rider.out_shape,
        scratch_shapes=rider.scratch, compiler_params=_params(("arbitrary",)),
    )(x, norm_g, mod, pos_col, invf, sign, *rider_inputs)


def _matmul(a, b, *, nt, out_dtype, tm, tn, name, rider=None, rider_inputs=()):
    m, kdim = a.shape
    n = b.shape[0] if nt else b.shape[1]
    tm, tn = min(tm, m), min(tn, n)
    n_in = len(rider_inputs)
    n_out = len(rider.out_shape) if rider else 0
    m_steps = m // tm
    steps = (n // tn) * m_steps

    def body(a_ref, b_ref, *rest):
        r_ins, o_ref, r_outs, sems = rest[:n_in], rest[n_in], rest[n_in + 1:n_in + 1 + n_out], rest[n_in + 1 + n_out:]
        step = pl.program_id(0) * m_steps + pl.program_id(1)
        if rider:
            pl.when(step == 0)(functools.partial(rider.start, r_ins, r_outs, sems))
            pl.when(step == steps // 2)(functools.partial(rider.forward, r_ins, r_outs, sems))
        o = _dot_nt(a_ref[...], b_ref[...]) if nt else _dot(a_ref[...], b_ref[...])
        o_ref[...] = o.astype(out_dtype)
        if rider:
            pl.when(step == steps - 1)(functools.partial(rider.finish, r_ins, r_outs, sems))

    b_spec = pl.BlockSpec((tn, kdim), lambda j, i: (j, 0)) if nt else pl.BlockSpec((kdim, tn), lambda j, i: (0, j))
    out = pl.pallas_call(
        body, name=name, grid=(n // tn, m_steps),
        in_specs=[pl.BlockSpec((tm, kdim), lambda j, i: (i, 0)), b_spec] + [ANY] * n_in,
        out_specs=[pl.BlockSpec((tm, tn), lambda j, i: (i, j))] + [ANY] * n_out,
        out_shape=[jax.ShapeDtypeStruct((m, n), out_dtype)] + (rider.out_shape if rider else []),
        scratch_shapes=rider.scratch if rider else [],
        compiler_params=_params(("arbitrary", "arbitrary") if rider else ("parallel", "parallel")),
    )(a, b, *rider_inputs)
    return out if rider else out[0]


HALO = 16


def _conv_specs(tm):
    def col(j):
        return pl.BlockSpec((tm, D_CONV), lambda i: (i, j))

    def prev(j):
        return pl.BlockSpec((HALO, D_CONV), lambda i: (jnp.maximum(i * (tm // HALO) - 1, 0), j))

    return [col(0), col(1), col(2), col(3), prev(0), prev(2)]


def _conv_y(xc_ref, bc_ref, cc_ref, zc_ref, xp_ref, cp_ref, w_ref, first):
    uc = cc_ref[...].astype(F32) * xc_ref[...].astype(F32)
    up = jnp.where(first, 0.0, cp_ref[...].astype(F32) * xp_ref[...].astype(F32))
    full = jnp.concatenate([up, uc], axis=0)
    u1 = pltpu.roll(full, 1, 0)[HALO:]
    u2 = pltpu.roll(full, 2, 0)[HALO:]
    w = w_ref[...]
    conv = w[0:1] * u2 + w[1:2] * u1 + w[2:3] * uc
    z = zc_ref[...].astype(F32)
    return bc_ref[...].astype(F32) * conv * (z * _sigmoid(z))


def _conv_bwd(u, dyc, conv_w):
    s = u.shape[0]
    tm = min(TM_ELEM, s)
    cb = D_CONV
    nt = s // tm

    def body(xc_ref, bc_ref, cc_ref, zc_ref, xp_ref, cp_ref, bn_ref, zn_ref, dy_ref, dyn_ref, w_ref, du_ref, dw_ref):
        i = pl.program_id(0)
        xc, cc = xc_ref[...].astype(F32), cc_ref[...].astype(F32)
        bc, z = bc_ref[...].astype(F32), zc_ref[...].astype(F32)
        uc = cc * xc
        up = jnp.where(i == 0, 0.0, cp_ref[...].astype(F32) * xp_ref[...].astype(F32))
        full = jnp.concatenate([up, uc], axis=0)
        u1 = pltpu.roll(full, 1, 0)[HALO:]
        u2 = pltpu.roll(full, 2, 0)[HALO:]
        w = w_ref[...]
        conv = w[0:1] * u2 + w[1:2] * u1 + w[2:3] * uc
        sg = _sigmoid(z)
        sz = z * sg
        dy = dy_ref[...].astype(F32)
        dconv = dy * bc * sz
        zn = zn_ref[...].astype(F32)
        dnext = dyn_ref[...].astype(F32) * bn_ref[...].astype(F32) * (zn * _sigmoid(zn))
        dnext = jnp.where(i == nt - 1, 0.0, dnext)
        fullb = jnp.concatenate([dconv, dnext], axis=0)
        nb = tm + HALO
        d1 = pltpu.roll(fullb, nb - 1, 0)[:tm]
        d2 = pltpu.roll(fullb, nb - 2, 0)[:tm]
        duc = w[2:3] * dconv + w[1:2] * d1 + w[0:1] * d2
        dzc = dy * bc * conv * (sg * (1.0 + z * (1.0 - sg)))
        du_ref[...] = jnp.concatenate([duc * cc, dy * conv * sz, duc * xc, dzc], axis=1).astype(BF16)
        dw = jnp.concatenate([jnp.sum(dconv * u2, axis=0, keepdims=True), jnp.sum(dconv * u1, axis=0, keepdims=True),
                              jnp.sum(dconv * uc, axis=0, keepdims=True), jnp.zeros((5, cb), F32)], axis=0)

        @pl.when(i == 0)
        def _():
            dw_ref[...] = dw

        @pl.when(i > 0)
        def _():
            dw_ref[...] += dw

    def col(j):
        return pl.BlockSpec((tm, cb), lambda i: (i, j))

    def prev(j):
        return pl.BlockSpec((HALO, cb), lambda i: (jnp.maximum(i * (tm // HALO) - 1, 0), j))

    def nxt(j):
        return pl.BlockSpec((HALO, cb), lambda i: (jnp.minimum((i + 1) * (tm // HALO), s // HALO - 1), j))

    return pl.pallas_call(
        body, name="conv_bwd", grid=(nt,),
        in_specs=[col(0), col(1), col(2), col(3), prev(0), prev(2), nxt(1), nxt(3), col(0), nxt(0),
                  pl.BlockSpec((3, cb), lambda i: (0, 0))],
        out_specs=[pl.BlockSpec((tm, 4 * cb), lambda i: (i, 0)), pl.BlockSpec((8, cb), lambda i: (0, 0))],
        out_shape=[jax.ShapeDtypeStruct((s, U_COLS), BF16), jax.ShapeDtypeStruct((8, cb), F32)],
        compiler_params=_params(("arbitrary",)),
    )(u, u, u, u, u, u, u, u, dyc, dyc, conv_w)


def _qkv_specs(tm):
    return [pl.BlockSpec((tm, Q_LORA), lambda i: (i, U_CQ // Q_LORA)),
            pl.BlockSpec((tm, KV_LORA), lambda i: (i, U_CKV // KV_LORA)),
            pl.BlockSpec((tm, LANES), lambda i: (i, U_KR // LANES)),
            pl.BlockSpec((tm, LANES), lambda i: (i, 0)), pl.BlockSpec((tm, LANES), lambda i: (i, 0))]


def _full(shape):
    return pl.BlockSpec(shape, lambda i: (0,) * len(shape))


def _k_rope_lanes(blk):
    lane = lax.broadcasted_iota(jnp.int32, blk.shape, 1)
    return jnp.where(lane < QK_ROPE, blk, 0.0)


def _qkv_fwd(u, cos, sin, wq, wkv, qag, kvag, qg, kg):
    s = u.shape[0]
    tm = min(TM_ELEM, s)

    def body(cq_ref, ckv_ref, kr_ref, cos_ref, sin_ref, wq_ref, wkv_ref, qag_ref, kvag_ref, qg_ref, kg_ref,
             q_ref, k_ref, v_ref):
        cq = cq_ref[...].astype(F32)
        cqn = (cq * lax.rsqrt(jnp.mean(cq * cq, axis=-1, keepdims=True) + EPS) * qag_ref[...]).astype(BF16)
        ckv = ckv_ref[...].astype(F32)
        ckvn = (ckv * lax.rsqrt(jnp.mean(ckv * ckv, axis=-1, keepdims=True) + EPS) * kvag_ref[...]).astype(BF16)
        kr = _k_rope_lanes(kr_ref[...].astype(F32))
        cosv, sinv, qgv, kgv = cos_ref[...], sin_ref[...], qg_ref[...], kg_ref[...]
        ss_r = jnp.sum(kr * kr, axis=-1, keepdims=True)
        krr = _rope(kr * kgv[:, QK_NOPE:], cosv, sinv)
        qf = _dot(cqn, wq_ref[...])
        kvf = _dot(ckvn, wkv_ref[...])
        for h in range(N_HEADS):
            qh = qf[:, QK_PAD * h:QK_PAD * (h + 1)]
            rq = lax.rsqrt(jnp.sum(qh * qh, axis=-1, keepdims=True) * (1.0 / QK_HEAD) + EPS)
            qn = qh * rq * qgv
            qo = jnp.concatenate([qn[:, :QK_NOPE], _rope(qn[:, QK_NOPE:], cosv, sinv)], axis=1) * (SCALE * LOG2E)
            q_ref[h] = qo.astype(BF16)
            kn = kvf[:, 2 * V_HEAD * h:2 * V_HEAD * h + QK_NOPE]
            vh = kvf[:, 2 * V_HEAD * h + QK_NOPE:2 * V_HEAD * (h + 1)]
            rk = lax.rsqrt((jnp.sum(kn * kn, axis=-1, keepdims=True) + ss_r) * (1.0 / QK_HEAD) + EPS)
            k_ref[h] = jnp.concatenate([kn * kgv[:, :QK_NOPE] * rk, krr * rk], axis=1).astype(BF16)
            v_ref[h] = jnp.concatenate([vh, jnp.ones_like(vh)], axis=1).astype(BF16)

    return pl.pallas_call(
        body, name="qkv_fwd", grid=(s // tm,),
        in_specs=_qkv_specs(tm) + [_full((Q_LORA, N_HEADS * QK_PAD)), _full((KV_LORA, 2 * D_ATTN)),
                                   _full((1, Q_LORA)), _full((1, KV_LORA)), _full((1, QK_PAD)), _full((1, QK_PAD))],
        out_specs=[pl.BlockSpec((N_HEADS, tm, QK_PAD), lambda i: (0, i, 0)),
                   pl.BlockSpec((N_HEADS, tm, QK_PAD), lambda i: (0, i, 0)),
                   pl.BlockSpec((N_HEADS, tm, 2 * V_HEAD), lambda i: (0, i, 0))],
        out_shape=[jax.ShapeDtypeStruct((N_HEADS, s, QK_PAD), BF16), jax.ShapeDtypeStruct((N_HEADS, s, QK_PAD), BF16),
                   jax.ShapeDtypeStruct((N_HEADS, s, 2 * V_HEAD), BF16)],
        compiler_params=_params(("parallel",)),
    )(u, u, u, cos, sin, wq, wkv, qag, kvag, qg, kg)


def _qkv_bwd(u, cos, sin, dq, dk, dv, dza, wq, wkv, qag, kvag, qg, kg, du):
    s = u.shape[0]
    tm = min(TM_QKV_BWD, s)
    nt = s // tm

    def body(cq_ref, ckv_ref, kr_ref, cos_ref, sin_ref, dq_ref, dk_ref, dv_ref, dza_ref, wq_ref, wkv_ref, qag_ref,
             kvag_ref, qg_ref, kg_ref, du_in, du_ref, dwq_ref, dwkv_ref, dqag_ref, dkvag_ref, dqg_ref, dkg_ref,
             dwq_acc, dwkv_acc):
        del du_in
        i = pl.program_id(0)

        @pl.when(i == 0)
        def _():
            dwq_acc[...] = jnp.zeros_like(dwq_acc)
            dwkv_acc[...] = jnp.zeros_like(dwkv_acc)

        cq = cq_ref[...].astype(F32)
        rqa = lax.rsqrt(jnp.mean(cq * cq, axis=-1, keepdims=True) + EPS)
        xq = cq * rqa
        qagv = qag_ref[...]
        cqn = (xq * qagv).astype(BF16)
        ckv = ckv_ref[...].astype(F32)
        rkva = lax.rsqrt(jnp.mean(ckv * ckv, axis=-1, keepdims=True) + EPS)
        xkv = ckv * rkva
        kvagv = kvag_ref[...]
        ckvn = (xkv * kvagv).astype(BF16)
        kr = _k_rope_lanes(kr_ref[...].astype(F32))
        cosv, sinv, qgv, kgv = cos_ref[...], sin_ref[...], qg_ref[...], kg_ref[...]
        ss_r = jnp.sum(kr * kr, axis=-1, keepdims=True)
        dqg = jnp.zeros((1, QK_PAD), F32)
        dkg = jnp.zeros((1, QK_PAD), F32)
        dkr = jnp.zeros((tm, LANES), F32)
        qf = _dot(cqn, wq_ref[...])
        kvf = _dot(ckvn, wkv_ref[...])
        dqf, dkvf = [], []
        for h in range(N_HEADS):
            qh = qf[:, QK_PAD * h:QK_PAD * (h + 1)]
            rq = lax.rsqrt(jnp.sum(qh * qh, axis=-1, keepdims=True) * (1.0 / QK_HEAD) + EPS)
            xh = qh * rq
            g = dq_ref[h].astype(F32) * SCALE
            dyq = jnp.concatenate([g[:, :QK_NOPE], _rope_t(g[:, QK_NOPE:], cosv, sinv)], axis=1)
            dqg = dqg + jnp.sum(dyq * xh, axis=0, keepdims=True)
            gdy = dyq * qgv
            dqf.append((rq * (gdy - xh * (jnp.sum(gdy * xh, axis=-1, keepdims=True) * (1.0 / QK_HEAD)))).astype(BF16))
            kn = kvf[:, 2 * V_HEAD * h:2 * V_HEAD * h + QK_NOPE]
            rk = lax.rsqrt((jnp.sum(kn * kn, axis=-1, keepdims=True) + ss_r) * (1.0 / QK_HEAD) + EPS)
            xk = jnp.concatenate([kn, kr], axis=1) * rk
            gk = dk_ref[h].astype(F32)
            dyk = jnp.concatenate([gk[:, :QK_NOPE], _rope_t(gk[:, QK_NOPE:], cosv, sinv)], axis=1)
            dkg = dkg + jnp.sum(dyk * xk, axis=0, keepdims=True)
            gdyk = dyk * kgv
            dxk = rk * (gdyk - xk * (jnp.sum(gdyk * xk, axis=-1, keepdims=True) * (1.0 / QK_HEAD)))
            dkr = dkr + dxk[:, QK_NOPE:]
            dkvf += [dxk[:, :QK_NOPE].astype(BF16), dv_ref[h]]
        dqf_b, dkvf_b = jnp.concatenate(dqf, axis=1), jnp.concatenate(dkvf, axis=1)
        dwq_acc[...] += _dot_tn(cqn, dqf_b)
        dwkv_acc[...] += _dot_tn(ckvn, dkvf_b)
        dcqn = _dot_nt(dqf_b, wq_ref[...])
        dckvn = _dot_nt(dkvf_b, wkv_ref[...])
        dqag = jnp.sum(dcqn * xq, axis=0, keepdims=True)
        dkvag = jnp.sum(dckvn * xkv, axis=0, keepdims=True)
        gq = dcqn * qagv
        dcq = rqa * (gq - xq * jnp.mean(gq * xq, axis=-1, keepdims=True))
        gkv = dckvn * kvagv
        dckv = rkva * (gkv - xkv * jnp.mean(gkv * xkv, axis=-1, keepdims=True))
        win = pltpu.roll(jnp.concatenate([dza_ref[...].astype(F32), jnp.zeros((tm, LANES), F32)], axis=1), QK_ROPE, 1)
        win = win + jnp.concatenate([dkr, jnp.zeros((tm, D_ATTN), F32)], axis=1)
        du_ref[...] = jnp.concatenate([dcq, dckv, win, jnp.zeros((tm, U_TAIL - ZA_LO - ZA_WIN), F32)], axis=1).astype(BF16)

        @pl.when(i == 0)
        def _():
            dqag_ref[...] = dqag
            dkvag_ref[...] = dkvag
            dqg_ref[...] = dqg
            dkg_ref[...] = dkg

        @pl.when(i > 0)
        def _():
            dqag_ref[...] += dqag
            dkvag_ref[...] += dkvag
            dqg_ref[...] += dqg
            dkg_ref[...] += dkg

        @pl.when(i == nt - 1)
        def _():
            dwq_ref[...] = dwq_acc[...].astype(BF16)
            dwkv_ref[...] = dwkv_acc[...].astype(BF16)

    head = lambda w: pl.BlockSpec((N_HEADS, tm, w), lambda i: (0, i, 0))
    wq_shape, wkv_shape = (Q_LORA, N_HEADS * QK_PAD), (KV_LORA, 2 * D_ATTN)
    return pl.pallas_call(
        body, name="qkv_bwd", grid=(nt,),
        in_specs=_qkv_specs(tm) + [head(QK_PAD), head(QK_PAD), head(V_HEAD), pl.BlockSpec((tm, D_ATTN), lambda i: (i, 0)),
                                   _full(wq_shape), _full(wkv_shape), _full((1, Q_LORA)), _full((1, KV_LORA)),
                                   _full((1, QK_PAD)), _full((1, QK_PAD)), ANY],
        out_specs=[pl.BlockSpec((tm, U_TAIL), lambda i: (i, U_COLS // U_TAIL - 1)), _full(wq_shape), _full(wkv_shape),
                   _full((1, Q_LORA)), _full((1, KV_LORA)), _full((1, QK_PAD)), _full((1, QK_PAD))],
        out_shape=[jax.ShapeDtypeStruct(du.shape, du.dtype), jax.ShapeDtypeStruct(wq_shape, BF16),
                   jax.ShapeDtypeStruct(wkv_shape, BF16), jax.ShapeDtypeStruct((1, Q_LORA), F32),
                   jax.ShapeDtypeStruct((1, KV_LORA), F32), jax.ShapeDtypeStruct((1, QK_PAD), F32),
                   jax.ShapeDtypeStruct((1, QK_PAD), F32)],
        scratch_shapes=[pltpu.VMEM(wq_shape, F32), pltpu.VMEM(wkv_shape, F32)],
        input_output_aliases={15: 0}, compiler_params=_params(("arbitrary",)),
    )(u, u, u, cos, sin, dq, dk, dv, dza, wq, wkv, qag, kvag, qg, kg, du)


def _flash_fwd(q, k, v):
    nh, s, _ = q.shape
    tq = min(TQ, s)
    nkv = KV_SPLIT
    tk = tq // nkv
    nq = s // tq
    nch = Q_CHAINS
    tc = tq // nch

    def body(q_ref, k_ref, v_ref, o_ref, lse_ref):
        i = pl.program_id(1)
        chains = [q_ref[0, r * tc:(r + 1) * tc, :] for r in range(nch)]

        def scores(r, j, shift):
            sc = _dot_nt(chains[r], k_ref[0, pl.ds(pl.multiple_of(j * tk, tk), tk), :])
            if shift is not None:
                qi = lax.broadcasted_iota(jnp.int32, sc.shape, 0)
                ki = lax.broadcasted_iota(jnp.int32, sc.shape, 1) + shift
                sc = jnp.where(ki <= qi, sc, NEG)
            return sc

        def softmax(sc, m):
            m_new = jnp.maximum(m, jnp.max(sc, axis=-1, keepdims=True))
            return m_new, jnp.exp2(sc - m_new).astype(BF16), jnp.exp2(m - m_new)

        def run(units, carry):
            ms, accs = [c[0] for c in carry], [c[1] for c in carry]
            sc, pv = {}, {}
            for t in range(len(units) + 2):
                if t < len(units):
                    sc[t] = scores(*units[t])
                if 0 <= t - 1 < len(units):
                    r = units[t - 1][0]
                    ms[r], p, alpha = softmax(sc.pop(t - 1), ms[r])
                    pv[t - 1] = (p, alpha)
                if t - 2 >= 0:
                    r, j, _ = units[t - 2]
                    p, alpha = pv.pop(t - 2)
                    accs[r] = alpha * accs[r] + _dot(p, v_ref[0, pl.ds(pl.multiple_of(j * tk, tk), tk), :])
            return tuple(zip(ms, accs))

        def trip(p, carry):
            return run([(r, nkv * p + b, None) for b in range(nkv) for r in range(nch)], carry)

        init = (jnp.full((tc, 1), NEG, F32), jnp.zeros((tc, 2 * V_HEAD), F32))
        carry = lax.fori_loop(0, i, trip, (init,) * nch)
        diag = []
        for b in range(nkv):
            for r in range(nch):
                shift = b * tk - r * tc
                if shift < tc:
                    diag.append((r, nkv * i + b, None if shift + tk - 1 <= 0 else shift))
        carry = run(diag, carry)
        for r, (m, acc) in enumerate(carry):
            l = acc[:, V_HEAD:]
            o_ref[r * tc:(r + 1) * tc, :] = (acc[:, :V_HEAD] / l).astype(BF16)
            lse = m + jnp.log(l[:, 0:1]) * LOG2E
            lse_ref[0, :, r * tc:(r + 1) * tc] = jnp.broadcast_to(lse, (tc, LANES)).T[0:1, :]

    return pl.pallas_call(
        body, name="flash_fwd", grid=(nh, nq),
        in_specs=[pl.BlockSpec((1, tq, QK_PAD), lambda h, i: (h, i, 0)),
                  pl.BlockSpec((1, s, QK_PAD), lambda h, i: (h, 0, 0)),
                  pl.BlockSpec((1, s, 2 * V_HEAD), lambda h, i: (h, 0, 0))],
        out_specs=[pl.BlockSpec((tq, V_HEAD), lambda h, i: (i, h)), pl.BlockSpec((1, 1, tq), lambda h, i: (h, 0, i))],
        out_shape=[jax.ShapeDtypeStruct((s, nh * V_HEAD), BF16), jax.ShapeDtypeStruct((nh, 1, s), F32)],
        compiler_params=_params(("parallel", "arbitrary")),
    )(q, k, v)


def _flash_bwd(q, k, v, do, lse, delta):
    nh, s, _ = q.shape
    tq = min(TQ, s)
    nq = s // tq

    def body(q_ref, k_ref, v_ref, do_ref, lse_ref, dl_ref, dq_ref, dk_ref, dv_ref, dq_acc):
        j = pl.program_id(1)

        @pl.when(j == 0)
        def _():
            dq_acc[...] = jnp.zeros_like(dq_acc)

        kj, vj = k_ref[0], v_ref[0]

        def block(kk, vv, qq, dd, lse, dl, masked):
            st = _dot_nt(kk, qq)
            pt = jnp.exp2(st - lse)
            if masked:
                ki = lax.broadcasted_iota(jnp.int32, st.shape, 0)
                qx = lax.broadcasted_iota(jnp.int32, st.shape, 1)
                pt = jnp.where(ki <= qx, pt, 0.0)
            ddv = _dot(pt.astype(BF16), dd)
            dst = (pt * (_dot_nt(vv, dd) - dl)).astype(BF16)
            ddk = _dot(dst, qq)
            return ddk, ddv, _dot_tn(dst, kk)

        def step(i, carry):
            dk, dv = carry
            rows = pl.ds(pl.multiple_of(i * tq, tq), tq)
            ddk, ddv, ddq = block(kj, vj, q_ref[0, rows, :], do_ref[rows, :], lse_ref[0, pl.ds(i, 1), :],
                                  dl_ref[0, pl.ds(i, 1), :], False)
            dq_acc[rows, :] += ddq
            return dk + ddk, dv + ddv

        th = tq // 2
        lse_j, dl_j = lse_ref[0, pl.ds(j, 1), :], dl_ref[0, pl.ds(j, 1), :]
        parts = []
        for kh, qh, masked in ((0, 0, True), (0, 1, False), (1, 1, True)):
            rows = pl.ds(pl.multiple_of(j * tq + qh * th, th), th)
            ks, qs = slice(kh * th, (kh + 1) * th), slice(qh * th, (qh + 1) * th)
            ddk, ddv, ddq = block(kj[ks], vj[ks], q_ref[0, rows, :], do_ref[rows, :], lse_j[:, qs], dl_j[:, qs], masked)
            dq_acc[rows, :] += ddq
            parts.append((ddk, ddv))
        carry = (jnp.concatenate([parts[0][0] + parts[1][0], parts[2][0]], axis=0),
                 jnp.concatenate([parts[0][1] + parts[1][1], parts[2][1]], axis=0))
        dk, dv = lax.fori_loop(j + 1, nq, step, carry)
        dk_ref[0] = (dk * LN2).astype(BF16)
        dv_ref[0] = dv.astype(BF16)

        @pl.when(j == nq - 1)
        def _():
            dq_ref[0] = dq_acc[...].astype(BF16)

    return pl.pallas_call(
        body, name="flash_bwd", grid=(nh, nq),
        in_specs=[pl.BlockSpec((1, s, QK_PAD), lambda h, j: (h, 0, 0)),
                  pl.BlockSpec((1, tq, QK_PAD), lambda h, j: (h, j, 0)),
                  pl.BlockSpec((1, tq, V_HEAD), lambda h, j: (h, j, 0)),
                  pl.BlockSpec((s, V_HEAD), lambda h, j: (0, h)),
                  pl.BlockSpec((1, nq, tq), lambda h, j: (h, 0, 0)),
                  pl.BlockSpec((1, nq, tq), lambda h, j: (h, 0, 0))],
        out_specs=[pl.BlockSpec((1, s, QK_PAD), lambda h, j: (h, 0, 0)),
                   pl.BlockSpec((1, tq, QK_PAD), lambda h, j: (h, j, 0)),
                   pl.BlockSpec((1, tq, V_HEAD), lambda h, j: (h, j, 0))],
        out_shape=[jax.ShapeDtypeStruct((nh, s, QK_PAD), BF16), jax.ShapeDtypeStruct((nh, s, QK_PAD), BF16),
                   jax.ShapeDtypeStruct((nh, s, V_HEAD), BF16)],
        scratch_shapes=[pltpu.VMEM((s, QK_PAD), F32)],
        compiler_params=_params(("parallel", "arbitrary")),
    )(q, k, v, do, lse, delta)


def _tail(x, target, o, u, mod, w_out, conv_w):
    s, d = x.shape
    tm = min(TM_ELEM, s)

    def body(x_ref, t_ref, o_ref, za_ref, mod_ref, w_ref, xc_ref, bc_ref, cc_ref, zc_ref, xp_ref, cp_ref, cw_ref,
             gx_ref, dy_ref, ycat_ref, dyc_ref, do_ref, du_ref, delta_ref, dgate_ref, loss_ref):
        i = pl.program_id(0)
        yc = _conv_y(xc_ref, bc_ref, cc_ref, zc_ref, xp_ref, cp_ref, cw_ref, i == 0)
        za = pltpu.roll(za_ref[:, ZA_LO:ZA_LO + ZA_WIN].astype(F32), ZA_WIN - QK_ROPE, 1)[:, :D_ATTN]
        ov = o_ref[...].astype(F32)
        sg = _sigmoid(za)
        sl = za * sg
        ya = ov * sl
        ycat = jnp.concatenate([yc.astype(BF16), ya.astype(BF16)], axis=1)
        ycat_ref[...] = jnp.concatenate([yc.T, ya.T], axis=0).astype(BF16)
        y = _dot(ycat, w_ref[...])
        gate = mod_ref[:, 2 * d:3 * d]
        e = x_ref[...] + gate * y - t_ref[...]
        dout = e * (1.0 / d)
        gx_ref[...] = dout
        dy = (dout * gate).astype(BF16)
        dy_ref[...] = dy
        dycat = _dot_nt(dy, w_ref[...])
        dyc_ref[...] = dycat[:, :D_CONV].astype(BF16)
        dya = dycat[:, D_CONV:]
        dov = dya * sl
        do_ref[...] = dov.astype(BF16)
        du_ref[...] = (dya * ov * (sg * (1.0 + za * (1.0 - sg)))).astype(BF16)
        prod_t = (dov * ov).T
        for h in range(N_HEADS):
            delta_ref[h] = jnp.sum(prod_t[V_HEAD * h:V_HEAD * (h + 1), :], axis=0, keepdims=True)
        dgate = jnp.sum(dout * y, axis=0, keepdims=True)
        part = jnp.sum(jnp.sum(e * e, axis=0, keepdims=True), axis=1, keepdims=True) * (0.5 / d)
        part = jnp.broadcast_to(part, (1, LANES))

        @pl.when(i == 0)
        def _():
            dgate_ref[...] = dgate
            loss_ref[...] = part

        @pl.when(i > 0)
        def _():
            dgate_ref[...] += dgate
            loss_ref[...] += part

    tok = lambda w: pl.BlockSpec((tm, w), lambda i: (i, 0))
    return pl.pallas_call(
        body, name="tail", grid=(s // tm,),
        in_specs=[tok(d), tok(d), tok(D_ATTN), pl.BlockSpec((tm, U_TAIL), lambda i: (i, U_COLS // U_TAIL - 1)),
                  _full((1, 3 * d)), _full((d, d))] + _conv_specs(tm) + [_full((3, D_CONV))],
        out_specs=[tok(d), tok(d), pl.BlockSpec((d, tm), lambda i: (0, i)), tok(D_CONV), tok(D_ATTN), tok(D_ATTN),
                   pl.BlockSpec((N_HEADS, 1, tm), lambda i: (0, 0, i)), _full((1, d)), _full((1, LANES))],
        out_shape=[jax.ShapeDtypeStruct((s, d), F32), jax.ShapeDtypeStruct((s, d), BF16),
                   jax.ShapeDtypeStruct((d, s), BF16), jax.ShapeDtypeStruct((s, D_CONV), BF16),
                   jax.ShapeDtypeStruct((s, D_ATTN), BF16), jax.ShapeDtypeStruct((s, D_ATTN), BF16),
                   jax.ShapeDtypeStruct((N_HEADS, 1, s), F32), jax.ShapeDtypeStruct((1, d), F32),
                   jax.ShapeDtypeStruct((1, LANES), F32)],
        compiler_params=_params(("arbitrary",)),
    )(x, target, o, u, mod, w_out, u, u, u, u, u, u, conv_w)


def _norm_bwd(x, dh, gx1, norm_g, mod):
    s, d = x.shape
    tm = min(TM_MM, s)

    def body(x_ref, dh_ref, gx_ref, g_ref, mod_ref, o_ref, dshift_ref, dscale_ref, dg_ref):
        i = pl.program_id(0)
        xv, dhv, gv = x_ref[...], dh_ref[...].astype(F32), g_ref[...]
        r = lax.rsqrt(jnp.mean(xv * xv, axis=-1, keepdims=True) + EPS)
        xn = xv * r
        dhn = dhv * (1.0 + mod_ref[:, d:2 * d])
        dxn = dhn * gv
        o_ref[...] = gx_ref[...] + r * (dxn - xn * jnp.mean(dxn * xn, axis=-1, keepdims=True))
        dshift = jnp.sum(dhv, axis=0, keepdims=True)
        dscale = jnp.sum(dhv * xn * gv, axis=0, keepdims=True)
        dg = jnp.sum(dhn * xn, axis=0, keepdims=True)

        @pl.when(i == 0)
        def _():
            dshift_ref[...] = dshift
            dscale_ref[...] = dscale
            dg_ref[...] = dg

        @pl.when(i > 0)
        def _():
            dshift_ref[...] += dshift
            dscale_ref[...] += dscale
            dg_ref[...] += dg

    tok = pl.BlockSpec((tm, d), lambda i: (i, 0))
    row = jax.ShapeDtypeStruct((1, d), F32)
    return pl.pallas_call(
        body, name="norm_bwd", grid=(s // tm,),
        in_specs=[tok, tok, tok, _full((1, d)), _full((1, 3 * d))],
        out_specs=[tok, _full((1, d)), _full((1, d)), _full((1, d))],
        out_shape=[jax.ShapeDtypeStruct((s, d), F32), row, row, row],
        compiler_params=_params(("arbitrary",)),
    )(x, dh, gx1, norm_g, mod)


def _adamw(w, g, m, v, name):
    rows, cols = w.shape
    tr = 256 if rows % 256 == 0 else rows

    def body(w_ref, g_ref, m_ref, v_ref, d_ref, nm_ref, nv_ref):
        gv = g_ref[...]
        nm = ADAM_B1 * m_ref[...] + (1.0 - ADAM_B1) * gv
        nv = ADAM_B2 * v_ref[...] + (1.0 - ADAM_B2) * (gv * gv)
        m_hat = nm / (1.0 - ADAM_B1 ** ADAM_STEP)
        v_hat = nv / (1.0 - ADAM_B2 ** ADAM_STEP)
        d_ref[...] = -ADAM_LR * (m_hat / (jnp.sqrt(v_hat) + ADAM_EPS) + ADAM_WD * w_ref[...])
        nm_ref[...] = nm
        nv_ref[...] = nv

    spec = pl.BlockSpec((tr, cols), lambda i: (i, 0))
    shape = jax.ShapeDtypeStruct((rows, cols), F32)
    return pl.pallas_call(
        body, name=name, grid=(rows // tr,), in_specs=[spec] * 4, out_specs=[spec] * 3, out_shape=[shape] * 3,
        compiler_params=_params(("parallel",)),
    )(w, g, m, v)


def _pad_cols(a, n):
    return jnp.pad(a, ((0, 0), (0, n - a.shape[1])))


def kernel(x, c, positions, ada_w, ada_b, norm_g, w_in, conv_w, q_a_g, w_q_b, kv_a_g, w_kv_b, q_g, k_g, w_out, loss_target, m_ada_w, m_ada_b, m_norm_g, m_w_in, m_conv_w, m_q_a_g, m_w_q_b, m_kv_a_g, m_w_kv_b, m_q_g, m_k_g, m_w_out, v_ada_w, v_ada_b, v_norm_g, v_w_in, v_conv_w, v_q_a_g, v_w_q_b, v_kv_a_g, v_w_kv_b, v_q_g, v_k_g, v_w_out):
    me = _my_index()
    s = x.shape[1]
    nq = s // min(TQ, s)
    x2, tgt = x[0], loss_target[0]
    w_in_l, w_q_l, w_kv_l, w_out_l, conv_l, ada_w_l = w_in[0], w_q_b[0], w_kv_b[0], w_out[0], conv_w[0], ada_w[0]
    ada_cols = ada_w_l.shape[1]

    small = jnp.concatenate([c.reshape(-1, LANES), conv_l.reshape(-1, LANES), jnp.zeros((5, LANES), F32)], axis=0)
    (small_g,) = _all_gather([small], "gather_c")
    c_all = small_g[:, :D_MODEL // LANES].reshape(N_DEV, D_MODEL)
    conv_g = small_g[:, D_MODEL // LANES:D_MODEL // LANES + 3].transpose(1, 0, 2).reshape(3, D_CONV)

    ada_b_l = lax.dynamic_slice(ada_b, (0, me * ada_cols), (1, ada_cols))
    mod_cols = _ada_mod(jnp.pad(c_all, ((0, 8), (0, 0))), ada_w_l, ada_b_l)[:N_DEV]
    (mod_g,) = _all_gather([mod_cols], "gather_mod")
    mod = lax.dynamic_index_in_dim(mod_g, me, axis=1, keepdims=False).reshape(1, 3 * D_MODEL)

    half = jnp.arange(0, QK_ROPE, 2, dtype=F32) / QK_ROPE
    inv_freq = ROPE_BASE ** (-half)
    zeros64 = jnp.zeros((LANES - QK_ROPE,), F32)
    invf = jnp.concatenate([inv_freq, inv_freq, zeros64]).reshape(1, LANES)
    sign = jnp.concatenate([-jnp.ones((32,), F32), jnp.ones((32,), F32), zeros64]).reshape(1, LANES)
    qg_p, kg_p = _pad_cols(q_g, QK_PAD), _pad_cols(k_g, QK_PAD)

    my_off = ((CW * me) % LANES).astype(jnp.int32)
    win = [_expand_w_in(w_in_l, my_off.reshape(1))]
    h, h_t, cos, sin, win_g = _norm_mod(x2, norm_g, mod, positions.reshape(s, 1), invf, sign, _Gather(win, relay=True, parts=4), win)
    w_in_p = _merge_w_in(win_g)
    rest = [_pad_wq(w_q_l), w_kv_l.astype(BF16), w_out_l.astype(BF16)]
    u, wq_g, wkv_g, w_out_g = _matmul(h, w_in_p, nt=False, out_dtype=BF16, tm=2 * TM_MM, tn=1024, name="in_proj",
                                      rider=_Gather(rest), rider_inputs=rest)
    w_out_g = w_out_g.reshape(D_MODEL, D_MODEL)
    wq_g = wq_g.transpose(1, 0, 2).reshape(Q_LORA, N_HEADS * QK_PAD)
    wkv_g = wkv_g.transpose(1, 0, 2).reshape(KV_LORA, 2 * D_ATTN)
    q, k, v = _qkv_fwd(u, cos, sin, wq_g, wkv_g, q_a_g, kv_a_g, qg_p, kg_p)
    o, lse = _flash_fwd(q, k, v)
    gx1, dy, ycat_t, dyc, do, dza, delta, dgate, loss_row = _tail(x2, tgt, o, u, mod, w_out_g, conv_g)

    dq, dk, dv = _flash_bwd(q, k, v, do, lse.reshape(N_HEADS, nq, s // nq), delta.reshape(N_HEADS, nq, s // nq))
    du, dconv = _conv_bwd(u, dyc, conv_g)
    du, dwq, dwkv, dqag, dkvag, dqg, dkg = _qkv_bwd(u, cos, sin, dq, dk, dv, dza, wq_g, wkv_g, q_a_g, kv_a_g, qg_p, kg_p, du)
    dwq = dwq.reshape(Q_LORA, N_HEADS, QK_PAD).transpose(1, 0, 2)
    dwkv = dwkv.reshape(KV_LORA, N_HEADS, 2 * V_HEAD).transpose(1, 0, 2)
    dw_in = _matmul(h_t, du, nt=False, out_dtype=BF16, tm=TM_MM, tn=512, name="dw_in")
    first = [dw_in, dwq, dwkv]
    dw_out, r_in, r_q, r_kv = _matmul(ycat_t, dy, nt=False, out_dtype=BF16, tm=TM_MM, tn=512, name="dw_out",
                                      rider=_SiblingExchange(first, [True, False, False]), rider_inputs=first)
    dw_out = dw_out.reshape(N_DEV, D_MODEL // N_DEV, D_MODEL)
    (r_out,) = _exchange(_SiblingExchange([dw_out], [False]), [dw_out], "rs_sibling_out")
    core = lax.axis_index("c").astype(jnp.int32)
    lo_tiles = ((CW * (2 * jnp.arange(4, dtype=jnp.int32) + core)) // LANES).astype(jnp.int32)
    pairs = [_add_window(dw_in, r_in, lo_tiles), _add_pairs(dwq, r_q, core.reshape(1), "rs_add_q"),
             _add_pairs(dwkv, r_kv, core.reshape(1), "rs_add_kv"), _add_pairs(dw_out, r_out, core.reshape(1), "rs_add_out")]
    dh, *quads = _matmul(du, w_in_p, nt=True, out_dtype=BF16, tm=2 * TM_MM, tn=512, name="dh",
                         rider=_ChipExchange(pairs), rider_inputs=pairs)
    my_chip = 2 * lax.axis_index("x") + lax.axis_index("y")
    written = jnp.where(jnp.arange(4) == my_chip, (jnp.arange(4) + 1) % 4, jnp.arange(4))
    sel = jnp.concatenate([my_chip.reshape(1), written, ((EXP_W - my_off) % EXP_W).reshape(1)]).astype(jnp.int32)
    g_w_in = _final_sum(pairs[0], quads[0], sel, "rs_sum_in", unshift_to=CW)
    g_w_q = _final_sum(pairs[1], quads[1], sel, "rs_sum_q")[:, :QK_HEAD]
    g_w_kv = _final_sum(pairs[2], quads[2], sel, "rs_sum_kv")
    g_w_out = _final_sum(pairs[3], quads[3], sel, "rs_sum_out")
    grad_x, dshift, dscale, dng = _norm_bwd(x2, dh, gx1, norm_g, mod)

    row = jnp.concatenate([dshift, dscale, dgate, dng, dqag, dkvag, dqg, dkg, dconv[:3].reshape(1, 3 * D_CONV), loss_row], axis=1)
    (rows_g,) = _all_gather([row], "gather_small")
    tot = _sum_leading(rows_g, F32, "sum_small")
    dmod_all = rows_g[:, 0, SM_MOD:SM_NG]
    g_ada_b = tot[:, SM_MOD:SM_NG]
    g_norm_g = tot[:, SM_NG:SM_QAG]
    g_q_a_g = tot[:, SM_QAG:SM_KVAG]
    g_kv_a_g = tot[:, SM_KVAG:SM_QG]
    g_q_g = tot[:, SM_QG:SM_QG + QK_HEAD]
    g_k_g = tot[:, SM_KG:SM_KG + QK_HEAD]
    conv_cols = conv_l.shape[1]
    g_conv = lax.dynamic_slice(tot[:, SM_CONV:SM_LOSS].reshape(3, D_CONV), (0, me * conv_cols), (3, conv_cols))
    loss = tot[0, SM_LOSS]
    dmod_my = lax.dynamic_slice(dmod_all, (0, me * ada_cols), (N_DEV, ada_cols))
    g_ada_w = _ada_w_grad(c_all.T, dmod_my)

    grads = dict(ada_w=g_ada_w, ada_b=g_ada_b, norm_g=g_norm_g, w_in=g_w_in, conv_w=g_conv, q_a_g=g_q_a_g, w_q_b=g_w_q,
                 kv_a_g=g_kv_a_g, w_kv_b=g_w_kv, q_g=g_q_g, k_g=g_k_g, w_out=g_w_out)
    weights = dict(ada_w=(ada_w, m_ada_w, v_ada_w), ada_b=(ada_b, m_ada_b, v_ada_b), norm_g=(norm_g, m_norm_g, v_norm_g),
                   w_in=(w_in, m_w_in, v_w_in), conv_w=(conv_w, m_conv_w, v_conv_w), q_a_g=(q_a_g, m_q_a_g, v_q_a_g),
                   w_q_b=(w_q_b, m_w_q_b, v_w_q_b), kv_a_g=(kv_a_g, m_kv_a_g, v_kv_a_g), w_kv_b=(w_kv_b, m_w_kv_b, v_w_kv_b),
                   q_g=(q_g, m_q_g, v_q_g), k_g=(k_g, m_k_g, v_k_g), w_out=(w_out, m_w_out, v_w_out))
    names = list(grads)
    out_g, out_d, out_m, out_v = [], [], [], []
    for n in names:
        w, m, v_ = weights[n]
        shape2 = w.shape[-2:] if w.ndim == 3 else (1, w.shape[-1])
        g2 = grads[n].reshape(shape2)
        d2, m2, v2 = _adamw(w.reshape(shape2), g2, m.reshape(shape2), v_.reshape(shape2), "adamw_" + n)
        out_g.append(g2.reshape(w.shape))
        out_d.append(d2.reshape(w.shape))
        out_m.append(m2.reshape(w.shape))
        out_v.append(v2.reshape(w.shape))
    return (loss, grad_x.reshape(x.shape), *out_g, *out_d, *out_m, *out_v)
```

```python
import functools
import math

import jax
import jax.numpy as jnp
from jax import lax
from jax.experimental import pallas as pl
from jax.experimental.pallas import tpu as pltpu

F32 = jnp.float32
BF16 = jnp.bfloat16
MESH = pl.DeviceIdType.MESH

D_MODEL = 2048
D_CONV = 1024
N_HEADS = 8
QK_NOPE = 128
QK_ROPE = 64
QK_HEAD = QK_NOPE + QK_ROPE
V_HEAD = 128
D_ATTN = N_HEADS * V_HEAD
Q_LORA = 512
KV_LORA = 256
ROPE_BASE = 10000.0
IN_COLS = 4 * D_CONV + Q_LORA + KV_LORA + QK_ROPE + D_ATTN
EPS = 1e-6
ADAM_LR, ADAM_B1, ADAM_B2, ADAM_EPS, ADAM_WD, ADAM_STEP = 0.001, 0.9, 0.999, 1e-08, 0.01, 10

N_DEV = 8
LANES = 128
QK_PAD = 256
U_COLS = 6144
U_CQ, U_CKV, U_KR, U_ZA = 4096, 4608, 4864, 4928
U_TAIL = 2048
ZA_LO = U_ZA - (U_COLS - U_TAIL) - QK_ROPE
ZA_WIN = D_ATTN + LANES
CW = IN_COLS // 8
EXP_W = 896
W_LO = [(CW * d // 128) * 128 for d in range(8)]
W_OFF = [CW * d - lo for d, lo in enumerate(W_LO)]
SCALE = 1.0 / math.sqrt(QK_HEAD)
LOG2E = 1.4426950408889634
LN2 = 0.6931471805599453
NEG = -1e30
VMEM_LIMIT = 56 * 1024 * 1024

TM_ELEM = 256
TM_QKV_BWD = 256
TM_MM = 512
TQ = 1024
Q_CHAINS = 4
KV_SPLIT = 2

SM_MOD, SM_NG, SM_QAG, SM_KVAG, SM_QG, SM_KG, SM_CONV, SM_LOSS = 0, 6144, 8192, 8704, 8960, 9216, 9472, 12544
SM_COLS = 12672


def _params(sem=None):
    kw = dict(vmem_limit_bytes=VMEM_LIMIT)
    if sem is not None:
        kw["dimension_semantics"] = sem
    return pltpu.CompilerParams(**kw)


def _sigmoid(z):
    return 1.0 / (1.0 + jnp.exp(-z))


def _rot64(x):
    lane = lax.broadcasted_iota(jnp.int32, x.shape, 1)
    return jnp.where(lane < 32, pltpu.roll(x, 96, 1), pltpu.roll(x, 32, 1))


def _rope(x, cos, sin):
    return x * cos + _rot64(x) * sin


def _rope_t(d, cos, sin):
    return d * cos - _rot64(d) * sin


def _dot(a, b):
    return jnp.dot(a, b, preferred_element_type=F32)


def _dot_nt(a, b):
    return lax.dot_general(a, b, (((1,), (1,)), ((), ())), preferred_element_type=F32)


def _dot_tn(a, b):
    return lax.dot_general(a, b, (((0,), (0,)), ((), ())), preferred_element_type=F32)


def _my_index():
    return 4 * lax.axis_index("x") + 2 * lax.axis_index("y") + lax.axis_index("c")


ANY = pl.BlockSpec(memory_space=pl.ANY)


class _Gather:
    def __init__(self, blocks, relay=False, parts=1):
        self.relay = relay
        self.parts = parts
        self.rows = [b.shape[0] // parts for b in blocks]
        self.n = n = len(blocks) * parts
        self.out_shape = [jax.ShapeDtypeStruct((N_DEV,) + b.shape, b.dtype) for b in blocks]
        self.scratch = [pltpu.SemaphoreType.DMA((7 * n,)), pltpu.SemaphoreType.DMA((7 * n,)),
                        pltpu.SemaphoreType.DMA((n,))]

    @staticmethod
    def _places():
        x, y, c = lax.axis_index("x"), lax.axis_index("y"), lax.axis_index("c")
        return (x, y, c), (x, y, 1 - c), [(1 - x, y), (x, 1 - y), (1 - x, 1 - y)]

    def _src(self, ins, a):
        block, part = divmod(a, self.parts)
        return ins[block] if self.parts == 1 else ins[block].at[pl.ds(part * self.rows[block], self.rows[block])]

    def _dst(self, outs, a, place):
        block, part = divmod(a, self.parts)
        ref = outs[block].at[4 * place[0] + 2 * place[1] + place[2]]
        return ref if self.parts == 1 else ref.at[pl.ds(part * self.rows[block], self.rows[block])]

    def _copy(self, outs, sems, a, k, block, to, src=None):
        dst = self._dst(outs, a, block)
        return pltpu.make_async_remote_copy(
            src_ref=dst if src is None else src, dst_ref=dst, send_sem=sems[0].at[7 * a + k],
            recv_sem=sems[1].at[7 * a + k], device_id=to, device_id_type=MESH)

    def _first(self, ins, outs, sems):
        me, sibling, chips = self._places()
        first = []
        for a in range(self.n):
            first.append(self._copy(outs, sems, a, 0, me, sibling, src=self._src(ins, a)))
            first += [self._copy(outs, sems, a, 1 + j, me, (*chip, me[2]), src=self._src(ins, a))
                      for j, chip in enumerate(chips[:2] if self.relay else chips)]
        return first

    def _relays(self, outs, sems):
        if not self.relay:
            return []
        (x, y, c), _, _ = self._places()
        via = (jnp.where(c == 0, 1 - x, x), jnp.where(c == 0, y, 1 - y))
        to = (jnp.where(c == 0, x, 1 - x), jnp.where(c == 0, 1 - y, y))
        return [self._copy(outs, sems, a, 3, (*via, c), (*to, c)) for a in range(self.n)]

    def _passed(self, outs, sems):
        me, sibling, chips = self._places()
        return [self._copy(outs, sems, a, 4 + j, (*chip, me[2]), sibling)
                for a in range(self.n) for j, chip in enumerate(chips)]

    def _mine(self, ins, outs, sems):
        me, _, _ = self._places()
        return [pltpu.make_async_copy(self._src(ins, a), self._dst(outs, a, me), sems[2].at[a]) for a in range(self.n)]

    def start(self, ins, outs, sems):
        for cp in self._mine(ins, outs, sems) + self._first(ins, outs, sems):
            cp.start()

    def forward(self, ins, outs, sems):
        del ins
        me, _, chips = self._places()
        passed, relays = self._passed(outs, sems), self._relays(outs, sems)
        for a in range(self.n):
            for j, chip in enumerate(chips[:2] if self.relay else chips):
                self._copy(outs, sems, a, 1 + j, (*chip, me[2]), me).wait_recv()
                passed[3 * a + j].start()
            if self.relay:
                relays[a].start()
        if self.relay:
            for a in range(self.n):
                self._copy(outs, sems, a, 3, (*chips[2], me[2]), me).wait_recv()
                passed[3 * a + 2].start()

    def finish(self, ins, outs, sems):
        me, sibling, chips = self._places()
        for a in range(self.n):
            self._copy(outs, sems, a, 0, sibling, me).wait_recv()
            for j, chip in enumerate(chips):
                self._copy(outs, sems, a, 4 + j, (*chip, 1 - me[2]), me).wait_recv()
        for cp in self._first(ins, outs, sems) + self._relays(outs, sems) + self._passed(outs, sems):
            cp.wait_send()
        for cp in self._mine(ins, outs, sems):
            cp.wait()


class _ChipExchange:
    def __init__(self, arrays):
        self.n = n = len(arrays)
        self.out_shape = [jax.ShapeDtypeStruct(a.shape, a.dtype) for a in arrays]
        self.scratch = [pltpu.SemaphoreType.DMA((3 * n,)), pltpu.SemaphoreType.DMA((3 * n,))]

    def _copies(self, ins, outs, sems):
        x, y, c = lax.axis_index("x"), lax.axis_index("y"), lax.axis_index("c")
        return [pltpu.make_async_remote_copy(
            src_ref=ins[a].at[2 * px + py], dst_ref=outs[a].at[2 * x + y], send_sem=sems[0].at[3 * a + j],
            recv_sem=sems[1].at[3 * a + j], device_id=(px, py, c), device_id_type=MESH)
            for a in range(self.n) for j, (px, py) in enumerate([(1 - x, y), (x, 1 - y), (1 - x, 1 - y)])]

    def start(self, ins, outs, sems):
        for cp in self._copies(ins, outs, sems):
            cp.start()

    def forward(self, ins, outs, sems):
        pass

    def finish(self, ins, outs, sems):
        for cp in self._copies(ins, outs, sems):
            cp.wait()


def _all_gather(blocks, name):
    n = len(blocks)
    g = _Gather(blocks)

    def body(*refs):
        ins, outs, sems = refs[:n], refs[n:2 * n], refs[2 * n:]
        g.start(ins, outs, sems)
        g.forward(ins, outs, sems)
        g.finish(ins, outs, sems)

    return pl.pallas_call(body, name=name, out_shape=g.out_shape, in_specs=[ANY] * n, out_specs=[ANY] * n,
                          scratch_shapes=g.scratch)(*blocks)


class _SiblingExchange:
    def __init__(self, arrays, windowed):
        self.n = n = len(arrays)
        self.windowed = windowed
        self.out_shape = [jax.ShapeDtypeStruct((4, a.shape[0], EXP_W) if w else (4,) + a.shape[1:], a.dtype)
                          for a, w in zip(arrays, windowed)]
        self.scratch = [pltpu.SemaphoreType.DMA((4 * n,)), pltpu.SemaphoreType.DMA((4 * n,))]

    def _each(self, ins, outs, sems, act):
        x, y, c = lax.axis_index("x"), lax.axis_index("y"), lax.axis_index("c")

        def branch(c_val):
            for k in range(4):
                e = 2 * k + (1 - c_val)
                for a in range(self.n):
                    src = ins[a].at[:, pl.ds(W_LO[e], EXP_W)] if self.windowed[a] else ins[a].at[e]
                    act(pltpu.make_async_remote_copy(
                        src_ref=src, dst_ref=outs[a].at[k], send_sem=sems[0].at[4 * a + k], recv_sem=sems[1].at[4 * a + k],
                        device_id=(x, y, 1 - c), device_id_type=MESH))

        for c_val in (0, 1):
            pl.when(c == c_val)(functools.partial(branch, c_val))

    def start(self, ins, outs, sems):
        self._each(ins, outs, sems, lambda cp: cp.start())

    def forward(self, ins, outs, sems):
        pass

    def finish(self, ins, outs, sems):
        self._each(ins, outs, sems, lambda cp: cp.wait())


def _exchange(rider, arrays, name):
    n = len(arrays)

    def body(*refs):
        ins, outs, sems = refs[:n], refs[n:n + len(rider.out_shape)], refs[n + len(rider.out_shape):]
        rider.start(ins, outs, sems)
        rider.forward(ins, outs, sems)
        rider.finish(ins, outs, sems)

    return pl.pallas_call(body, name=name, out_shape=rider.out_shape, in_specs=[ANY] * n,
                          out_specs=[ANY] * len(rider.out_shape), scratch_shapes=rider.scratch)(*arrays)


def _add_window(dw_in, recv, lo_tiles):
    k, rows, _ = recv.shape

    def body(t_ref, w_ref, r_ref, o_ref):
        del t_ref
        o_ref[0] = (w_ref[...].astype(F32) + r_ref[0].astype(F32)).astype(o_ref.dtype)

    spec = pl.BlockSpec((1, rows, LANES), lambda i, j, t: (i, 0, j))
    grid_spec = pltpu.PrefetchScalarGridSpec(
        num_scalar_prefetch=1, grid=(k, EXP_W // LANES),
        in_specs=[pl.BlockSpec((rows, LANES), lambda i, j, t: (0, t[i] + j)), spec], out_specs=spec)
    return pl.pallas_call(
        body, name="rs_add_in", grid_spec=grid_spec, out_shape=jax.ShapeDtypeStruct(recv.shape, recv.dtype),
        compiler_params=_params(("parallel", "parallel")),
    )(lo_tiles, dw_in, recv)


def _final_sum(p, r, sel, name, unshift_to=None):
    _, rows, cols = p.shape
    tr = 512 if rows % 512 == 0 else rows
    out_cols = cols if unshift_to is None else unshift_to

    def body(sel_ref, p_ref, r0, r1, r2, r3, o_ref):
        own = p_ref[0].astype(F32)
        acc = None
        for k, r_ref in enumerate((r0, r1, r2, r3)):
            term = jnp.where(sel_ref[0] == k, own, r_ref[0].astype(F32))
            acc = term if acc is None else acc + term
        if unshift_to is not None:
            acc = pltpu.roll(acc, sel_ref[5], 1)[:, :unshift_to]
        o_ref[...] = acc

    def slot(k):
        return pl.BlockSpec((1, tr, cols), lambda i, t: (t[k], i, 0))

    grid_spec = pltpu.PrefetchScalarGridSpec(
        num_scalar_prefetch=1, grid=(rows // tr,), in_specs=[slot(0), slot(1), slot(2), slot(3), slot(4)],
        out_specs=pl.BlockSpec((tr, out_cols), lambda i, t: (i, 0)))
    return pl.pallas_call(
        body, name=name, grid_spec=grid_spec, out_shape=jax.ShapeDtypeStruct((rows, out_cols), F32),
        compiler_params=_params(("parallel",)),
    )(sel, p, r, r, r, r)


def _expand_w_in(w, shift):
    rows, cw = w.shape
    tr = 256

    def body(s_ref, w_ref, o_ref, buf):
        buf[...] = jnp.zeros_like(buf)
        buf[:, 0:cw] = w_ref[...]
        o_ref[...] = pltpu.roll(buf[...], s_ref[0], 1).astype(BF16)

    grid_spec = pltpu.PrefetchScalarGridSpec(
        num_scalar_prefetch=1, grid=(rows // tr,), in_specs=[pl.BlockSpec((tr, cw), lambda i, t: (i, 0))],
        out_specs=pl.BlockSpec((tr, EXP_W), lambda i, t: (i, 0)), scratch_shapes=[pltpu.VMEM((tr, EXP_W), F32)])
    return pl.pallas_call(
        body, name="expand_w_in", grid_spec=grid_spec, out_shape=jax.ShapeDtypeStruct((rows, EXP_W), BF16),
        compiler_params=_params(("arbitrary",)),
    )(shift, w)


def _pad_wq(w):
    rows, cw = w.shape

    def body(w_ref, o_ref, buf):
        buf[...] = jnp.zeros_like(buf)
        buf[:, 0:cw] = w_ref[...]
        o_ref[...] = buf[...].astype(BF16)

    return pl.pallas_call(
        body, name="pad_wq", out_shape=jax.ShapeDtypeStruct((rows, QK_PAD), BF16),
        scratch_shapes=[pltpu.VMEM((rows, QK_PAD), F32)], compiler_params=_params(),
    )(w)


def _merge_w_in(e):
    _, rows, _ = e.shape
    tr = 256

    def body(e_ref, o_ref):
        for t in range(U_COLS // LANES):
            lo, hi = t * LANES, (t + 1) * LANES
            parts = [e_ref[d, :, lo - W_LO[d]:hi - W_LO[d]] for d in range(N_DEV)
                     if CW * d < hi and CW * (d + 1) > lo]
            if not parts:
                tile = jnp.zeros((tr, LANES), BF16)
            elif len(parts) == 1:
                tile = parts[0]
            else:
                tile = (parts[0].astype(F32) + parts[1].astype(F32)).astype(BF16)
            o_ref[:, lo:hi] = tile

    return pl.pallas_call(
        body, name="merge_w_in", grid=(rows // tr,),
        in_specs=[pl.BlockSpec((N_DEV, tr, EXP_W), lambda i: (0, i, 0))],
        out_specs=pl.BlockSpec((tr, U_COLS), lambda i: (i, 0)), out_shape=jax.ShapeDtypeStruct((rows, U_COLS), BF16),
        compiler_params=_params(("parallel",)),
    )(e)


def _sum_leading(a, out_dtype, name):
    k, rows, cols = a.shape
    tr = min(rows, 1728 if rows % 1728 == 0 else rows)

    def body(a_ref, o_ref):
        acc = a_ref[0].astype(F32)
        for i in range(1, k):
            acc = acc + a_ref[i].astype(F32)
        o_ref[...] = acc.astype(out_dtype)

    return pl.pallas_call(
        body, name=name, grid=(rows // tr,),
        in_specs=[pl.BlockSpec((k, tr, cols), lambda i: (0, i, 0))],
        out_specs=pl.BlockSpec((tr, cols), lambda i: (i, 0)),
        out_shape=jax.ShapeDtypeStruct((rows, cols), out_dtype), compiler_params=_params(("parallel",)),
    )(a)


def _add_pairs(g, recv, core, name):
    k, rows, cols = recv.shape
    tr = 1728 if rows % 1728 == 0 else rows

    def body(c_ref, g_ref, r_ref, o_ref):
        del c_ref
        o_ref[...] = (g_ref[...].astype(F32) + r_ref[...].astype(F32)).astype(o_ref.dtype)

    spec = pl.BlockSpec((1, tr, cols), lambda i, j, c: (i, j, 0))
    grid_spec = pltpu.PrefetchScalarGridSpec(
        num_scalar_prefetch=1, grid=(k, rows // tr),
        in_specs=[pl.BlockSpec((1, tr, cols), lambda i, j, c: (2 * i + c[0], j, 0)), spec], out_specs=spec)
    return pl.pallas_call(
        body, name=name, grid_spec=grid_spec, out_shape=jax.ShapeDtypeStruct(recv.shape, recv.dtype),
        compiler_params=_params(("parallel", "parallel")),
    )(core, g, recv)


def _ada_mod(c16, ada_w_l, ada_b_l):
    def body(c_ref, w_ref, b_ref, o_ref):
        cv = c_ref[...]
        sc = (cv * _sigmoid(cv)).astype(BF16)
        o_ref[...] = _dot(sc, w_ref[...].astype(BF16)) + b_ref[...]

    return pl.pallas_call(
        body, name="ada_mod", out_shape=jax.ShapeDtypeStruct((c16.shape[0], ada_w_l.shape[1]), F32),
        compiler_params=_params(),
    )(c16, ada_w_l, ada_b_l)


def _ada_w_grad(c_t, dmod_my):
    def body(c_ref, d_ref, o_ref):
        cv = c_ref[...]
        sc = cv * _sigmoid(cv)
        acc = sc[:, 0:1] * d_ref[0:1, :]
        for b in range(1, N_DEV):
            acc = acc + sc[:, b:b + 1] * d_ref[b:b + 1, :]
        o_ref[...] = acc

    return pl.pallas_call(
        body, name="ada_w_grad", out_shape=jax.ShapeDtypeStruct((c_t.shape[0], dmod_my.shape[1]), F32),
        compiler_params=_params(),
    )(c_t, dmod_my)


def _norm_mod(x, norm_g, mod, pos_col, invf, sign, rider, rider_inputs):
    s, d = x.shape
    tm = min(TM_MM, s)
    n_in, n_out = len(rider_inputs), len(rider.out_shape)
    steps = s // tm

    def body(x_ref, g_ref, mod_ref, p_ref, f_ref, s_ref, *rest):
        r_ins, (h_ref, ht_ref, cos_ref, sin_ref) = rest[:n_in], rest[n_in:n_in + 4]
        r_outs, sems = rest[n_in + 4:n_in + 4 + n_out], rest[n_in + 4 + n_out:]
        pl.when(pl.program_id(0) == 0)(functools.partial(rider.start, r_ins, r_outs, sems))
        xv = x_ref[...]
        r = lax.rsqrt(jnp.mean(xv * xv, axis=-1, keepdims=True) + EPS)
        hn = xv * r * g_ref[...]
        hv = hn * (1.0 + mod_ref[:, d:2 * d]) + mod_ref[:, 0:d]
        h_ref[...] = hv.astype(BF16)
        ht_ref[...] = hv.T.astype(BF16)
        ang = p_ref[...].astype(F32) * f_ref[...]
        sg = s_ref[...]
        cos_ref[...] = jnp.cos(ang) * jnp.abs(sg)
        sin_ref[...] = jnp.sin(ang) * sg

        @pl.when(pl.program_id(0) == steps - 1)
        def _():
            rider.forward(r_ins, r_outs, sems)
            rider.finish(r_ins, r_outs, sems)

    row = pl.BlockSpec((1, LANES), lambda i: (0, 0))
    tab = pl.BlockSpec((tm, LANES), lambda i: (i, 0))
    return pl.pallas_call(
        body, name="norm_mod", grid=(steps,),
        in_specs=[pl.BlockSpec((tm, d), lambda i: (i, 0)), pl.BlockSpec((1, d), lambda i: (0, 0)),
                  pl.BlockSpec((1, 3 * d), lambda i: (0, 0)), pl.BlockSpec((tm, 1), lambda i: (i, 0)), row, row]
        + [ANY] * n_in,
        out_specs=[pl.BlockSpec((tm, d), lambda i: (i, 0)), pl.BlockSpec((d, tm), lambda i: (0, i)), tab, tab] + [ANY] * n_out,
        out_shape=[jax.ShapeDtypeStruct((s, d), BF16), jax.ShapeDtypeStruct((d, s), BF16),
                   jax.ShapeDtypeStruct((s, LANES), F32), jax.ShapeDtypeStruct((s, LANES), F32)] + rider.out_shape,
        scratch_shapes=rider.scratch, compiler_params=_params(("arbitrary",)),
    )(x, norm_g, mod, pos_col, invf, sign, *rider_inputs)


def _matmul(a, b, *, nt, out_dtype, tm, tn, name, rider=None, rider_inputs=(), a_resident=False):
    m, kdim = a.shape
    n = b.shape[0] if nt else b.shape[1]
    tm, tn = min(tm, m), min(tn, n)
    n_in = len(rider_inputs)
    n_out = len(rider.out_shape) if rider else 0
    m_steps, n_steps = m // tm, n // tn
    steps = n_steps * m_steps
    inner = n_steps if a_resident else m_steps
    tile = (lambda o, i: (o, i)) if a_resident else (lambda o, i: (i, o))

    def body(a_ref, b_ref, *rest):
        r_ins, o_ref, r_outs, sems = rest[:n_in], rest[n_in], rest[n_in + 1:n_in + 1 + n_out], rest[n_in + 1 + n_out:]
        step = pl.program_id(0) * inner + pl.program_id(1)
        if rider:
            pl.when(step == 0)(functools.partial(rider.start, r_ins, r_outs, sems))
            pl.when(step == steps // 2)(functools.partial(rider.forward, r_ins, r_outs, sems))
        o = _dot_nt(a_ref[...], b_ref[...]) if nt else _dot(a_ref[...], b_ref[...])
        o_ref[...] = o.astype(out_dtype)
        if rider:
            pl.when(step == steps - 1)(functools.partial(rider.finish, r_ins, r_outs, sems))

    if nt:
        b_spec = pl.BlockSpec((tn, kdim), lambda o, i: (tile(o, i)[1], 0))
    else:
        b_spec = pl.BlockSpec((kdim, tn), lambda o, i: (0, tile(o, i)[1]))
    out = pl.pallas_call(
        body, name=name, grid=(m_steps, n_steps) if a_resident else (n_steps, m_steps),
        in_specs=[pl.BlockSpec((tm, kdim), lambda o, i: (tile(o, i)[0], 0)), b_spec] + [ANY] * n_in,
        out_specs=[pl.BlockSpec((tm, tn), tile)] + [ANY] * n_out,
        out_shape=[jax.ShapeDtypeStruct((m, n), out_dtype)] + (rider.out_shape if rider else []),
        scratch_shapes=rider.scratch if rider else [],
        compiler_params=_params(("arbitrary", "arbitrary") if rider else ("parallel", "parallel")),
    )(a, b, *rider_inputs)
    return out if rider else out[0]


HALO = 16


def _conv_specs(tm):
    def col(j):
        return pl.BlockSpec((tm, D_CONV), lambda i: (i, j))

    def prev(j):
        return pl.BlockSpec((HALO, D_CONV), lambda i: (jnp.maximum(i * (tm // HALO) - 1, 0), j))

    return [col(0), col(1), col(2), col(3), prev(0), prev(2)]


def _conv_y(xc_ref, bc_ref, cc_ref, zc_ref, xp_ref, cp_ref, w_ref, first):
    uc = cc_ref[...].astype(F32) * xc_ref[...].astype(F32)
    up = jnp.where(first, 0.0, cp_ref[...].astype(F32) * xp_ref[...].astype(F32))
    full = jnp.concatenate([up, uc], axis=0)
    u1 = pltpu.roll(full, 1, 0)[HALO:]
    u2 = pltpu.roll(full, 2, 0)[HALO:]
    w = w_ref[...]
    conv = w[0:1] * u2 + w[1:2] * u1 + w[2:3] * uc
    z = zc_ref[...].astype(F32)
    return bc_ref[...].astype(F32) * conv * (z * _sigmoid(z))


def _conv_bwd(u, dyc, conv_w):
    s = u.shape[0]
    tm = min(TM_ELEM, s)
    cb = D_CONV
    nt = s // tm

    def body(xc_ref, bc_ref, cc_ref, zc_ref, xp_ref, cp_ref, bn_ref, zn_ref, dy_ref, dyn_ref, w_ref, du_ref, dw_ref):
        i = pl.program_id(0)
        xc, cc = xc_ref[...].astype(F32), cc_ref[...].astype(F32)
        bc, z = bc_ref[...].astype(F32), zc_ref[...].astype(F32)
        uc = cc * xc
        up = jnp.where(i == 0, 0.0, cp_ref[...].astype(F32) * xp_ref[...].astype(F32))
        full = jnp.concatenate([up, uc], axis=0)
        u1 = pltpu.roll(full, 1, 0)[HALO:]
        u2 = pltpu.roll(full, 2, 0)[HALO:]
        w = w_ref[...]
        conv = w[0:1] * u2 + w[1:2] * u1 + w[2:3] * uc
        sg = _sigmoid(z)
        sz = z * sg
        dy = dy_ref[...].astype(F32)
        dconv = dy * bc * sz
        zn = zn_ref[...].astype(F32)
        dnext = dyn_ref[...].astype(F32) * bn_ref[...].astype(F32) * (zn * _sigmoid(zn))
        dnext = jnp.where(i == nt - 1, 0.0, dnext)
        fullb = jnp.concatenate([dconv, dnext], axis=0)
        nb = tm + HALO
        d1 = pltpu.roll(fullb, nb - 1, 0)[:tm]
        d2 = pltpu.roll(fullb, nb - 2, 0)[:tm]
        duc = w[2:3] * dconv + w[1:2] * d1 + w[0:1] * d2
        dzc = dy * bc * conv * (sg * (1.0 + z * (1.0 - sg)))
        du_ref[...] = jnp.concatenate([duc * cc, dy * conv * sz, duc * xc, dzc], axis=1).astype(BF16)
        dw = jnp.concatenate([jnp.sum(dconv * u2, axis=0, keepdims=True), jnp.sum(dconv * u1, axis=0, keepdims=True),
                              jnp.sum(dconv * uc, axis=0, keepdims=True), jnp.zeros((5, cb), F32)], axis=0)

        @pl.when(i == 0)
        def _():
            dw_ref[...] = dw

        @pl.when(i > 0)
        def _():
            dw_ref[...] += dw

    def col(j):
        return pl.BlockSpec((tm, cb), lambda i: (i, j))

    def prev(j):
        return pl.BlockSpec((HALO, cb), lambda i: (jnp.maximum(i * (tm // HALO) - 1, 0), j))

    def nxt(j):
        return pl.BlockSpec((HALO, cb), lambda i: (jnp.minimum((i + 1) * (tm // HALO), s // HALO - 1), j))

    return pl.pallas_call(
        body, name="conv_bwd", grid=(nt,),
        in_specs=[col(0), col(1), col(2), col(3), prev(0), prev(2), nxt(1), nxt(3), col(0), nxt(0),
                  pl.BlockSpec((3, cb), lambda i: (0, 0))],
        out_specs=[pl.BlockSpec((tm, 4 * cb), lambda i: (i, 0)), pl.BlockSpec((8, cb), lambda i: (0, 0))],
        out_shape=[jax.ShapeDtypeStruct((s, U_COLS), BF16), jax.ShapeDtypeStruct((8, cb), F32)],
        compiler_params=_params(("arbitrary",)),
    )(u, u, u, u, u, u, u, u, dyc, dyc, conv_w)


def _qkv_specs(tm):
    return [pl.BlockSpec((tm, Q_LORA), lambda i: (i, U_CQ // Q_LORA)),
            pl.BlockSpec((tm, KV_LORA), lambda i: (i, U_CKV // KV_LORA)),
            pl.BlockSpec((tm, LANES), lambda i: (i, U_KR // LANES)),
            pl.BlockSpec((tm, LANES), lambda i: (i, 0)), pl.BlockSpec((tm, LANES), lambda i: (i, 0))]


def _full(shape):
    return pl.BlockSpec(shape, lambda i: (0,) * len(shape))


def _k_rope_lanes(blk):
    lane = lax.broadcasted_iota(jnp.int32, blk.shape, 1)
    return jnp.where(lane < QK_ROPE, blk, 0.0)


def _qkv_fwd(u, cos, sin, wq, wkv, qag, kvag, qg, kg):
    s = u.shape[0]
    tm = min(TM_ELEM, s)

    def body(cq_ref, ckv_ref, kr_ref, cos_ref, sin_ref, wq_ref, wkv_ref, qag_ref, kvag_ref, qg_ref, kg_ref,
             q_ref, k_ref, v_ref):
        cq = cq_ref[...].astype(F32)
        cqn = (cq * lax.rsqrt(jnp.mean(cq * cq, axis=-1, keepdims=True) + EPS) * qag_ref[...]).astype(BF16)
        ckv = ckv_ref[...].astype(F32)
        ckvn = (ckv * lax.rsqrt(jnp.mean(ckv * ckv, axis=-1, keepdims=True) + EPS) * kvag_ref[...]).astype(BF16)
        kr = _k_rope_lanes(kr_ref[...].astype(F32))
        cosv, sinv, qgv, kgv = cos_ref[...], sin_ref[...], qg_ref[...], kg_ref[...]
        ss_r = jnp.sum(kr * kr, axis=-1, keepdims=True)
        krr = _rope(kr * kgv[:, QK_NOPE:], cosv, sinv)
        qf = _dot(cqn, wq_ref[...])
        kvf = _dot(ckvn, wkv_ref[...])
        for h in range(N_HEADS):
            qh = qf[:, QK_PAD * h:QK_PAD * (h + 1)]
            rq = lax.rsqrt(jnp.sum(qh * qh, axis=-1, keepdims=True) * (1.0 / QK_HEAD) + EPS)
            qn = qh * rq * qgv
            qo = jnp.concatenate([qn[:, :QK_NOPE], _rope(qn[:, QK_NOPE:], cosv, sinv)], axis=1) * (SCALE * LOG2E)
            q_ref[h] = qo.astype(BF16)
            kn = kvf[:, 2 * V_HEAD * h:2 * V_HEAD * h + QK_NOPE]
            vh = kvf[:, 2 * V_HEAD * h + QK_NOPE:2 * V_HEAD * (h + 1)]
            rk = lax.rsqrt((jnp.sum(kn * kn, axis=-1, keepdims=True) + ss_r) * (1.0 / QK_HEAD) + EPS)
            k_ref[h] = jnp.concatenate([kn * kgv[:, :QK_NOPE] * rk, krr * rk], axis=1).astype(BF16)
            v_ref[h] = jnp.concatenate([vh, jnp.ones_like(vh)], axis=1).astype(BF16)

    return pl.pallas_call(
        body, name="qkv_fwd", grid=(s // tm,),
        in_specs=_qkv_specs(tm) + [_full((Q_LORA, N_HEADS * QK_PAD)), _full((KV_LORA, 2 * D_ATTN)),
                                   _full((1, Q_LORA)), _full((1, KV_LORA)), _full((1, QK_PAD)), _full((1, QK_PAD))],
        out_specs=[pl.BlockSpec((N_HEADS, tm, QK_PAD), lambda i: (0, i, 0)),
                   pl.BlockSpec((N_HEADS, tm, QK_PAD), lambda i: (0, i, 0)),
                   pl.BlockSpec((N_HEADS, tm, 2 * V_HEAD), lambda i: (0, i, 0))],
        out_shape=[jax.ShapeDtypeStruct((N_HEADS, s, QK_PAD), BF16), jax.ShapeDtypeStruct((N_HEADS, s, QK_PAD), BF16),
                   jax.ShapeDtypeStruct((N_HEADS, s, 2 * V_HEAD), BF16)],
        compiler_params=_params(("parallel",)),
    )(u, u, u, cos, sin, wq, wkv, qag, kvag, qg, kg)


def _qkv_bwd(u, cos, sin, dq, dk, dv, dza, wq, wkv, qag, kvag, qg, kg, du):
    s = u.shape[0]
    tm = min(TM_QKV_BWD, s)
    nt = s // tm

    def body(cq_ref, ckv_ref, kr_ref, cos_ref, sin_ref, dq_ref, dk_ref, dv_ref, dza_ref, wq_ref, wkv_ref, qag_ref,
             kvag_ref, qg_ref, kg_ref, du_in, du_ref, dwq_ref, dwkv_ref, dqag_ref, dkvag_ref, dqg_ref, dkg_ref,
             dwq_acc, dwkv_acc):
        del du_in
        i = pl.program_id(0)

        @pl.when(i == 0)
        def _():
            dwq_acc[...] = jnp.zeros_like(dwq_acc)
            dwkv_acc[...] = jnp.zeros_like(dwkv_acc)

        cq = cq_ref[...].astype(F32)
        rqa = lax.rsqrt(jnp.mean(cq * cq, axis=-1, keepdims=True) + EPS)
        xq = cq * rqa
        qagv = qag_ref[...]
        cqn = (xq * qagv).astype(BF16)
        ckv = ckv_ref[...].astype(F32)
        rkva = lax.rsqrt(jnp.mean(ckv * ckv, axis=-1, keepdims=True) + EPS)
        xkv = ckv * rkva
        kvagv = kvag_ref[...]
        ckvn = (xkv * kvagv).astype(BF16)
        kr = _k_rope_lanes(kr_ref[...].astype(F32))
        cosv, sinv, qgv, kgv = cos_ref[...], sin_ref[...], qg_ref[...], kg_ref[...]
        ss_r = jnp.sum(kr * kr, axis=-1, keepdims=True)
        dqg = jnp.zeros((1, QK_PAD), F32)
        dkg = jnp.zeros((1, QK_PAD), F32)
        dkr = jnp.zeros((tm, LANES), F32)
        qf = _dot(cqn, wq_ref[...])
        kvf = _dot(ckvn, wkv_ref[...])
        dqf, dkvf = [], []
        for h in range(N_HEADS):
            qh = qf[:, QK_PAD * h:QK_PAD * (h + 1)]
            rq = lax.rsqrt(jnp.sum(qh * qh, axis=-1, keepdims=True) * (1.0 / QK_HEAD) + EPS)
            xh = qh * rq
            g = dq_ref[h].astype(F32) * SCALE
            dyq = jnp.concatenate([g[:, :QK_NOPE], _rope_t(g[:, QK_NOPE:], cosv, sinv)], axis=1)
            dqg = dqg + jnp.sum(dyq * xh, axis=0, keepdims=True)
            gdy = dyq * qgv
            dqf.append((rq * (gdy - xh * (jnp.sum(gdy * xh, axis=-1, keepdims=True) * (1.0 / QK_HEAD)))).astype(BF16))
            kn = kvf[:, 2 * V_HEAD * h:2 * V_HEAD * h + QK_NOPE]
            rk = lax.rsqrt((jnp.sum(kn * kn, axis=-1, keepdims=True) + ss_r) * (1.0 / QK_HEAD) + EPS)
            xk = jnp.concatenate([kn, kr], axis=1) * rk
            gk = dk_ref[h].astype(F32)
            dyk = jnp.concatenate([gk[:, :QK_NOPE], _rope_t(gk[:, QK_NOPE:], cosv, sinv)], axis=1)
            dkg = dkg + jnp.sum(dyk * xk, axis=0, keepdims=True)
            gdyk = dyk * kgv
            dxk = rk * (gdyk - xk * (jnp.sum(gdyk * xk, axis=-1, keepdims=True) * (1.0 / QK_HEAD)))
            dkr = dkr + dxk[:, QK_NOPE:]
            dkvf += [dxk[:, :QK_NOPE].astype(BF16), dv_ref[h]]
        dqf_b, dkvf_b = jnp.concatenate(dqf, axis=1), jnp.concatenate(dkvf, axis=1)
        dwq_acc[...] += _dot_tn(cqn, dqf_b)
        dwkv_acc[...] += _dot_tn(ckvn, dkvf_b)
        dcqn = _dot_nt(dqf_b, wq_ref[...])
        dckvn = _dot_nt(dkvf_b, wkv_ref[...])
        dqag = jnp.sum(dcqn * xq, axis=0, keepdims=True)
        dkvag = jnp.sum(dckvn * xkv, axis=0, keepdims=True)
        gq = dcqn * qagv
        dcq = rqa * (gq - xq * jnp.mean(gq * xq, axis=-1, keepdims=True))
        gkv = dckvn * kvagv
        dckv = rkva * (gkv - xkv * jnp.mean(gkv * xkv, axis=-1, keepdims=True))
        win = pltpu.roll(jnp.concatenate([dza_ref[...].astype(F32), jnp.zeros((tm, LANES), F32)], axis=1), QK_ROPE, 1)
        win = win + jnp.concatenate([dkr, jnp.zeros((tm, D_ATTN), F32)], axis=1)
        du_ref[...] = jnp.concatenate([dcq, dckv, win, jnp.zeros((tm, U_TAIL - ZA_LO - ZA_WIN), F32)], axis=1).astype(BF16)

        @pl.when(i == 0)
        def _():
            dqag_ref[...] = dqag
            dkvag_ref[...] = dkvag
            dqg_ref[...] = dqg
            dkg_ref[...] = dkg

        @pl.when(i > 0)
        def _():
            dqag_ref[...] += dqag
            dkvag_ref[...] += dkvag
            dqg_ref[...] += dqg
            dkg_ref[...] += dkg

        @pl.when(i == nt - 1)
        def _():
            dwq_ref[...] = dwq_acc[...].astype(BF16)
            dwkv_ref[...] = dwkv_acc[...].astype(BF16)

    head = lambda w: pl.BlockSpec((N_HEADS, tm, w), lambda i: (0, i, 0))
    wq_shape, wkv_shape = (Q_LORA, N_HEADS * QK_PAD), (KV_LORA, 2 * D_ATTN)
    return pl.pallas_call(
        body, name="qkv_bwd", grid=(nt,),
        in_specs=_qkv_specs(tm) + [head(QK_PAD), head(QK_PAD), head(V_HEAD), pl.BlockSpec((tm, D_ATTN), lambda i: (i, 0)),
                                   _full(wq_shape), _full(wkv_shape), _full((1, Q_LORA)), _full((1, KV_LORA)),
                                   _full((1, QK_PAD)), _full((1, QK_PAD)), ANY],
        out_specs=[pl.BlockSpec((tm, U_TAIL), lambda i: (i, U_COLS // U_TAIL - 1)), _full(wq_shape), _full(wkv_shape),
                   _full((1, Q_LORA)), _full((1, KV_LORA)), _full((1, QK_PAD)), _full((1, QK_PAD))],
        out_shape=[jax.ShapeDtypeStruct(du.shape, du.dtype), jax.ShapeDtypeStruct(wq_shape, BF16),
                   jax.ShapeDtypeStruct(wkv_shape, BF16), jax.ShapeDtypeStruct((1, Q_LORA), F32),
                   jax.ShapeDtypeStruct((1, KV_LORA), F32), jax.ShapeDtypeStruct((1, QK_PAD), F32),
                   jax.ShapeDtypeStruct((1, QK_PAD), F32)],
        scratch_shapes=[pltpu.VMEM(wq_shape, F32), pltpu.VMEM(wkv_shape, F32)],
        input_output_aliases={15: 0}, compiler_params=_params(("arbitrary",)),
    )(u, u, u, cos, sin, dq, dk, dv, dza, wq, wkv, qag, kvag, qg, kg, du)


def _flash_fwd(q, k, v):
    nh, s, _ = q.shape
    tq = min(TQ, s)
    nkv = KV_SPLIT
    tk = tq // nkv
    nq = s // tq
    nch = Q_CHAINS
    tc = tq // nch

    def body(q_ref, k_ref, v_ref, o_ref, lse_ref):
        i = pl.program_id(1)
        chains = [q_ref[0, r * tc:(r + 1) * tc, :] for r in range(nch)]

        def unit(r, j, carry, shift=None):
            m, acc = carry
            rows = pl.ds(pl.multiple_of(j * tk, tk), tk)
            sc = _dot_nt(chains[r], k_ref[0, rows, :])
            if shift is not None:
                qi = lax.broadcasted_iota(jnp.int32, sc.shape, 0)
                ki = lax.broadcasted_iota(jnp.int32, sc.shape, 1) + shift
                sc = jnp.where(ki <= qi, sc, NEG)
            m_new = jnp.maximum(m, jnp.max(sc, axis=-1, keepdims=True))
            p = jnp.exp2(sc - m_new).astype(BF16)
            return m_new, jnp.exp2(m - m_new) * acc + _dot(p, v_ref[0, rows, :])

        def trip(p, carry):
            for b in range(nkv):
                carry = tuple(unit(r, nkv * p + b, cr) for r, cr in enumerate(carry))
            return carry

        init = (jnp.full((tc, 1), NEG, F32), jnp.zeros((tc, 2 * V_HEAD), F32))
        carry = list(lax.fori_loop(0, i, trip, (init,) * nch))
        for b in range(nkv):
            for r in range(nch):
                shift = b * tk - r * tc
                if shift < tc:
                    carry[r] = unit(r, nkv * i + b, carry[r], None if shift + tk - 1 <= 0 else shift)
        for r, (m, acc) in enumerate(carry):
            l = acc[:, V_HEAD:]
            o_ref[r * tc:(r + 1) * tc, :] = (acc[:, :V_HEAD] / l).astype(BF16)
            lse = m + jnp.log(l[:, 0:1]) * LOG2E
            lse_ref[0, :, r * tc:(r + 1) * tc] = jnp.broadcast_to(lse, (tc, LANES)).T[0:1, :]

    return pl.pallas_call(
        body, name="flash_fwd", grid=(nh, nq),
        in_specs=[pl.BlockSpec((1, tq, QK_PAD), lambda h, i: (h, i, 0)),
                  pl.BlockSpec((1, s, QK_PAD), lambda h, i: (h, 0, 0)),
                  pl.BlockSpec((1, s, 2 * V_HEAD), lambda h, i: (h, 0, 0))],
        out_specs=[pl.BlockSpec((tq, V_HEAD), lambda h, i: (i, h)), pl.BlockSpec((1, 1, tq), lambda h, i: (h, 0, i))],
        out_shape=[jax.ShapeDtypeStruct((s, nh * V_HEAD), BF16), jax.ShapeDtypeStruct((nh, 1, s), F32)],
        compiler_params=_params(("parallel", "arbitrary")),
    )(q, k, v)


def _flash_bwd(q, k, v, do, lse, delta):
    nh, s, _ = q.shape
    tq = min(TQ, s)
    nq = s // tq

    def body(q_ref, k_ref, v_ref, do_ref, lse_ref, dl_ref, dq_ref, dk_ref, dv_ref, dq_acc):
        j = pl.program_id(1)

        @pl.when(j == 0)
        def _():
            dq_acc[...] = jnp.zeros_like(dq_acc)

        kj, vj = k_ref[0], v_ref[0]

        def block(kk, vv, qq, dd, lse, dl, masked):
            st = _dot_nt(kk, qq)
            pt = jnp.exp2(st - lse)
            if masked:
                ki = lax.broadcasted_iota(jnp.int32, st.shape, 0)
                qx = lax.broadcasted_iota(jnp.int32, st.shape, 1)
                pt = jnp.where(ki <= qx, pt, 0.0)
            ddv = _dot(pt.astype(BF16), dd)
            dst = (pt * (_dot_nt(vv, dd) - dl)).astype(BF16)
            ddk = _dot(dst, qq)
            return ddk, ddv, _dot_tn(dst, kk)

        def step(i, carry):
            dk, dv = carry
            rows = pl.ds(pl.multiple_of(i * tq, tq), tq)
            ddk, ddv, ddq = block(kj, vj, q_ref[0, rows, :], do_ref[rows, :], lse_ref[0, pl.ds(i, 1), :],
                                  dl_ref[0, pl.ds(i, 1), :], False)
            dq_acc[rows, :] += ddq
            return dk + ddk, dv + ddv

        th = tq // 2
        lse_j, dl_j = lse_ref[0, pl.ds(j, 1), :], dl_ref[0, pl.ds(j, 1), :]
        parts = []
        for kh, qh, masked in ((0, 0, True), (0, 1, False), (1, 1, True)):
            rows = pl.ds(pl.multiple_of(j * tq + qh * th, th), th)
            ks, qs = slice(kh * th, (kh + 1) * th), slice(qh * th, (qh + 1) * th)
            ddk, ddv, ddq = block(kj[ks], vj[ks], q_ref[0, rows, :], do_ref[rows, :], lse_j[:, qs], dl_j[:, qs], masked)
            dq_acc[rows, :] += ddq
            parts.append((ddk, ddv))
        carry = (jnp.concatenate([parts[0][0] + parts[1][0], parts[2][0]], axis=0),
                 jnp.concatenate([parts[0][1] + parts[1][1], parts[2][1]], axis=0))
        dk, dv = lax.fori_loop(j + 1, nq, step, carry)
        dk_ref[0] = (dk * LN2).astype(BF16)
        dv_ref[0] = dv.astype(BF16)

        @pl.when(j == nq - 1)
        def _():
            dq_ref[0] = dq_acc[...].astype(BF16)

    return pl.pallas_call(
        body, name="flash_bwd", grid=(nh, nq),
        in_specs=[pl.BlockSpec((1, s, QK_PAD), lambda h, j: (h, 0, 0)),
                  pl.BlockSpec((1, tq, QK_PAD), lambda h, j: (h, j, 0)),
                  pl.BlockSpec((1, tq, V_HEAD), lambda h, j: (h, j, 0)),
                  pl.BlockSpec((s, V_HEAD), lambda h, j: (0, h)),
                  pl.BlockSpec((1, nq, tq), lambda h, j: (h, 0, 0)),
                  pl.BlockSpec((1, nq, tq), lambda h, j: (h, 0, 0))],
        out_specs=[pl.BlockSpec((1, s, QK_PAD), lambda h, j: (h, 0, 0)),
                   pl.BlockSpec((1, tq, QK_PAD), lambda h, j: (h, j, 0)),
                   pl.BlockSpec((1, tq, V_HEAD), lambda h, j: (h, j, 0))],
        out_shape=[jax.ShapeDtypeStruct((nh, s, QK_PAD), BF16), jax.ShapeDtypeStruct((nh, s, QK_PAD), BF16),
                   jax.ShapeDtypeStruct((nh, s, V_HEAD), BF16)],
        scratch_shapes=[pltpu.VMEM((s, QK_PAD), F32)],
        compiler_params=_params(("parallel", "arbitrary")),
    )(q, k, v, do, lse, delta)


def _tail(x, target, o, u, mod, w_out, conv_w):
    s, d = x.shape
    tm = min(TM_ELEM, s)

    def body(x_ref, t_ref, o_ref, za_ref, mod_ref, w_ref, xc_ref, bc_ref, cc_ref, zc_ref, xp_ref, cp_ref, cw_ref,
             gx_ref, dy_ref, ycat_ref, dyc_ref, do_ref, du_ref, delta_ref, dgate_ref, loss_ref):
        i = pl.program_id(0)
        yc = _conv_y(xc_ref, bc_ref, cc_ref, zc_ref, xp_ref, cp_ref, cw_ref, i == 0)
        za = pltpu.roll(za_ref[:, ZA_LO:ZA_LO + ZA_WIN].astype(F32), ZA_WIN - QK_ROPE, 1)[:, :D_ATTN]
        ov = o_ref[...].astype(F32)
        sg = _sigmoid(za)
        sl = za * sg
        ya = ov * sl
        ycat = jnp.concatenate([yc.astype(BF16), ya.astype(BF16)], axis=1)
        ycat_ref[...] = jnp.concatenate([yc.T, ya.T], axis=0).astype(BF16)
        y = _dot(ycat, w_ref[...])
        gate = mod_ref[:, 2 * d:3 * d]
        e = x_ref[...] + gate * y - t_ref[...]
        dout = e * (1.0 / d)
        gx_ref[...] = dout
        dy = (dout * gate).astype(BF16)
        dy_ref[...] = dy
        dycat = _dot_nt(dy, w_ref[...])
        dyc_ref[...] = dycat[:, :D_CONV].astype(BF16)
        dya = dycat[:, D_CONV:]
        dov = dya * sl
        do_ref[...] = dov.astype(BF16)
        du_ref[...] = (dya * ov * (sg * (1.0 + za * (1.0 - sg)))).astype(BF16)
        prod_t = (dov * ov).T
        for h in range(N_HEADS):
            delta_ref[h] = jnp.sum(prod_t[V_HEAD * h:V_HEAD * (h + 1), :], axis=0, keepdims=True)
        dgate = jnp.sum(dout * y, axis=0, keepdims=True)
        part = jnp.sum(jnp.sum(e * e, axis=0, keepdims=True), axis=1, keepdims=True) * (0.5 / d)
        part = jnp.broadcast_to(part, (1, LANES))

        @pl.when(i == 0)
        def _():
            dgate_ref[...] = dgate
            loss_ref[...] = part

        @pl.when(i > 0)
        def _():
            dgate_ref[...] += dgate
            loss_ref[...] += part

    tok = lambda w: pl.BlockSpec((tm, w), lambda i: (i, 0))
    return pl.pallas_call(
        body, name="tail", grid=(s // tm,),
        in_specs=[tok(d), tok(d), tok(D_ATTN), pl.BlockSpec((tm, U_TAIL), lambda i: (i, U_COLS // U_TAIL - 1)),
                  _full((1, 3 * d)), _full((d, d))] + _conv_specs(tm) + [_full((3, D_CONV))],
        out_specs=[tok(d), tok(d), pl.BlockSpec((d, tm), lambda i: (0, i)), tok(D_CONV), tok(D_ATTN), tok(D_ATTN),
                   pl.BlockSpec((N_HEADS, 1, tm), lambda i: (0, 0, i)), _full((1, d)), _full((1, LANES))],
        out_shape=[jax.ShapeDtypeStruct((s, d), F32), jax.ShapeDtypeStruct((s, d), BF16),
                   jax.ShapeDtypeStruct((d, s), BF16), jax.ShapeDtypeStruct((s, D_CONV), BF16),
                   jax.ShapeDtypeStruct((s, D_ATTN), BF16), jax.ShapeDtypeStruct((s, D_ATTN), BF16),
                   jax.ShapeDtypeStruct((N_HEADS, 1, s), F32), jax.ShapeDtypeStruct((1, d), F32),
                   jax.ShapeDtypeStruct((1, LANES), F32)],
        compiler_params=_params(("arbitrary",)),
    )(x, target, o, u, mod, w_out, u, u, u, u, u, u, conv_w)


def _norm_bwd(x, dh, gx1, norm_g, mod):
    s, d = x.shape
    tm = min(TM_MM, s)

    def body(x_ref, dh_ref, gx_ref, g_ref, mod_ref, o_ref, dshift_ref, dscale_ref, dg_ref):
        i = pl.program_id(0)
        xv, dhv, gv = x_ref[...], dh_ref[...].astype(F32), g_ref[...]
        r = lax.rsqrt(jnp.mean(xv * xv, axis=-1, keepdims=True) + EPS)
        xn = xv * r
        dhn = dhv * (1.0 + mod_ref[:, d:2 * d])
        dxn = dhn * gv
        o_ref[...] = gx_ref[...] + r * (dxn - xn * jnp.mean(dxn * xn, axis=-1, keepdims=True))
        dshift = jnp.sum(dhv, axis=0, keepdims=True)
        dscale = jnp.sum(dhv * xn * gv, axis=0, keepdims=True)
        dg = jnp.sum(dhn * xn, axis=0, keepdims=True)

        @pl.when(i == 0)
        def _():
            dshift_ref[...] = dshift
            dscale_ref[...] = dscale
            dg_ref[...] = dg

        @pl.when(i > 0)
        def _():
            dshift_ref[...] += dshift
            dscale_ref[...] += dscale
            dg_ref[...] += dg

    tok = pl.BlockSpec((tm, d), lambda i: (i, 0))
    row = jax.ShapeDtypeStruct((1, d), F32)
    return pl.pallas_call(
        body, name="norm_bwd", grid=(s // tm,),
        in_specs=[tok, tok, tok, _full((1, d)), _full((1, 3 * d))],
        out_specs=[tok, _full((1, d)), _full((1, d)), _full((1, d))],
        out_shape=[jax.ShapeDtypeStruct((s, d), F32), row, row, row],
        compiler_params=_params(("arbitrary",)),
    )(x, dh, gx1, norm_g, mod)


def _adamw(w, g, m, v, name):
    rows, cols = w.shape
    tr = 256 if rows % 256 == 0 else rows

    def body(w_ref, g_ref, m_ref, v_ref, d_ref, nm_ref, nv_ref):
        gv = g_ref[...]
        nm = ADAM_B1 * m_ref[...] + (1.0 - ADAM_B1) * gv
        nv = ADAM_B2 * v_ref[...] + (1.0 - ADAM_B2) * (gv * gv)
        m_hat = nm / (1.0 - ADAM_B1 ** ADAM_STEP)
        v_hat = nv / (1.0 - ADAM_B2 ** ADAM_STEP)
        d_ref[...] = -ADAM_LR * (m_hat / (jnp.sqrt(v_hat) + ADAM_EPS) + ADAM_WD * w_ref[...])
        nm_ref[...] = nm
        nv_ref[...] = nv

    spec = pl.BlockSpec((tr, cols), lambda i: (i, 0))
    shape = jax.ShapeDtypeStruct((rows, cols), F32)
    return pl.pallas_call(
        body, name=name, grid=(rows // tr,), in_specs=[spec] * 4, out_specs=[spec] * 3, out_shape=[shape] * 3,
        compiler_params=_params(("parallel",)),
    )(w, g, m, v)


def _pad_cols(a, n):
    return jnp.pad(a, ((0, 0), (0, n - a.shape[1])))


def kernel(x, c, positions, ada_w, ada_b, norm_g, w_in, conv_w, q_a_g, w_q_b, kv_a_g, w_kv_b, q_g, k_g, w_out, loss_target, m_ada_w, m_ada_b, m_norm_g, m_w_in, m_conv_w, m_q_a_g, m_w_q_b, m_kv_a_g, m_w_kv_b, m_q_g, m_k_g, m_w_out, v_ada_w, v_ada_b, v_norm_g, v_w_in, v_conv_w, v_q_a_g, v_w_q_b, v_kv_a_g, v_w_kv_b, v_q_g, v_k_g, v_w_out):
    me = _my_index()
    s = x.shape[1]
    nq = s // min(TQ, s)
    x2, tgt = x[0], loss_target[0]
    w_in_l, w_q_l, w_kv_l, w_out_l, conv_l, ada_w_l = w_in[0], w_q_b[0], w_kv_b[0], w_out[0], conv_w[0], ada_w[0]
    ada_cols = ada_w_l.shape[1]

    small = jnp.concatenate([c.reshape(-1, LANES), conv_l.reshape(-1, LANES), jnp.zeros((5, LANES), F32)], axis=0)
    (small_g,) = _all_gather([small], "gather_c")
    c_all = small_g[:, :D_MODEL // LANES].reshape(N_DEV, D_MODEL)
    conv_g = small_g[:, D_MODEL // LANES:D_MODEL // LANES + 3].transpose(1, 0, 2).reshape(3, D_CONV)

    ada_b_l = lax.dynamic_slice(ada_b, (0, me * ada_cols), (1, ada_cols))
    mod_cols = _ada_mod(jnp.pad(c_all, ((0, 8), (0, 0))), ada_w_l, ada_b_l)[:N_DEV]
    (mod_g,) = _all_gather([mod_cols], "gather_mod")
    mod = lax.dynamic_index_in_dim(mod_g, me, axis=1, keepdims=False).reshape(1, 3 * D_MODEL)

    half = jnp.arange(0, QK_ROPE, 2, dtype=F32) / QK_ROPE
    inv_freq = ROPE_BASE ** (-half)
    zeros64 = jnp.zeros((LANES - QK_ROPE,), F32)
    invf = jnp.concatenate([inv_freq, inv_freq, zeros64]).reshape(1, LANES)
    sign = jnp.concatenate([-jnp.ones((32,), F32), jnp.ones((32,), F32), zeros64]).reshape(1, LANES)
    qg_p, kg_p = _pad_cols(q_g, QK_PAD), _pad_cols(k_g, QK_PAD)

    my_off = ((CW * me) % LANES).astype(jnp.int32)
    win = [_expand_w_in(w_in_l, my_off.reshape(1))]
    h, h_t, cos, sin, win_g = _norm_mod(x2, norm_g, mod, positions.reshape(s, 1), invf, sign, _Gather(win, relay=True, parts=4), win)
    w_in_p = _merge_w_in(win_g)
    rest = [_pad_wq(w_q_l), w_kv_l.astype(BF16), w_out_l.astype(BF16)]
    u, wq_g, wkv_g, w_out_g = _matmul(h, w_in_p, nt=False, out_dtype=BF16, tm=2 * TM_MM, tn=1024, name="in_proj",
                                      rider=_Gather(rest), rider_inputs=rest)
    w_out_g = w_out_g.reshape(D_MODEL, D_MODEL)
    wq_g = wq_g.transpose(1, 0, 2).reshape(Q_LORA, N_HEADS * QK_PAD)
    wkv_g = wkv_g.transpose(1, 0, 2).reshape(KV_LORA, 2 * D_ATTN)
    q, k, v = _qkv_fwd(u, cos, sin, wq_g, wkv_g, q_a_g, kv_a_g, qg_p, kg_p)
    o, lse = _flash_fwd(q, k, v)
    gx1, dy, ycat_t, dyc, do, dza, delta, dgate, loss_row = _tail(x2, tgt, o, u, mod, w_out_g, conv_g)

    dq, dk, dv = _flash_bwd(q, k, v, do, lse.reshape(N_HEADS, nq, s // nq), delta.reshape(N_HEADS, nq, s // nq))
    du, dconv = _conv_bwd(u, dyc, conv_g)
    du, dwq, dwkv, dqag, dkvag, dqg, dkg = _qkv_bwd(u, cos, sin, dq, dk, dv, dza, wq_g, wkv_g, q_a_g, kv_a_g, qg_p, kg_p, du)
    dwq = dwq.reshape(Q_LORA, N_HEADS, QK_PAD).transpose(1, 0, 2)
    dwkv = dwkv.reshape(KV_LORA, N_HEADS, 2 * V_HEAD).transpose(1, 0, 2)
    dw_in = _matmul(h_t, du, nt=False, out_dtype=BF16, tm=TM_MM, tn=768, name="dw_in")
    first = [dw_in, dwq, dwkv]
    dw_out, r_in, r_q, r_kv = _matmul(ycat_t, dy, nt=False, out_dtype=BF16, tm=TM_MM, tn=512, name="dw_out",
                                      rider=_SiblingExchange(first, [True, False, False]), rider_inputs=first)
    dw_out = dw_out.reshape(N_DEV, D_MODEL // N_DEV, D_MODEL)
    (r_out,) = _exchange(_SiblingExchange([dw_out], [False]), [dw_out], "rs_sibling_out")
    core = lax.axis_index("c").astype(jnp.int32)
    lo_tiles = ((CW * (2 * jnp.arange(4, dtype=jnp.int32) + core)) // LANES).astype(jnp.int32)
    pairs = [_add_window(dw_in, r_in, lo_tiles), _add_pairs(dwq, r_q, core.reshape(1), "rs_add_q"),
             _add_pairs(dwkv, r_kv, core.reshape(1), "rs_add_kv"), _add_pairs(dw_out, r_out, core.reshape(1), "rs_add_out")]
    dh, *quads = _matmul(du, w_in_p, nt=True, out_dtype=BF16, tm=2 * TM_MM, tn=512, name="dh",
                         rider=_ChipExchange(pairs), rider_inputs=pairs, a_resident=True)
    my_chip = 2 * lax.axis_index("x") + lax.axis_index("y")
    written = jnp.where(jnp.arange(4) == my_chip, (jnp.arange(4) + 1) % 4, jnp.arange(4))
    sel = jnp.concatenate([my_chip.reshape(1), written, ((EXP_W - my_off) % EXP_W).reshape(1)]).astype(jnp.int32)
    g_w_in = _final_sum(pairs[0], quads[0], sel, "rs_sum_in", unshift_to=CW)
    g_w_q = _final_sum(pairs[1], quads[1], sel, "rs_sum_q")[:, :QK_HEAD]
    g_w_kv = _final_sum(pairs[2], quads[2], sel, "rs_sum_kv")
    g_w_out = _final_sum(pairs[3], quads[3], sel, "rs_sum_out")
    grad_x, dshift, dscale, dng = _norm_bwd(x2, dh, gx1, norm_g, mod)

    row = jnp.concatenate([dshift, dscale, dgate, dng, dqag, dkvag, dqg, dkg, dconv[:3].reshape(1, 3 * D_CONV), loss_row], axis=1)
    (rows_g,) = _all_gather([row], "gather_small")
    tot = _sum_leading(rows_g, F32, "sum_small")
    dmod_all = rows_g[:, 0, SM_MOD:SM_NG]
    g_ada_b = tot[:, SM_MOD:SM_NG]
    g_norm_g = tot[:, SM_NG:SM_QAG]
    g_q_a_g = tot[:, SM_QAG:SM_KVAG]
    g_kv_a_g = tot[:, SM_KVAG:SM_QG]
    g_q_g = tot[:, SM_QG:SM_QG + QK_HEAD]
    g_k_g = tot[:, SM_KG:SM_KG + QK_HEAD]
    conv_cols = conv_l.shape[1]
    g_conv = lax.dynamic_slice(tot[:, SM_CONV:SM_LOSS].reshape(3, D_CONV), (0, me * conv_cols), (3, conv_cols))
    loss = tot[0, SM_LOSS]
    dmod_my = lax.dynamic_slice(dmod_all, (0, me * ada_cols), (N_DEV, ada_cols))
    g_ada_w = _ada_w_grad(c_all.T, dmod_my)

    grads = dict(ada_w=g_ada_w, ada_b=g_ada_b, norm_g=g_norm_g, w_in=g_w_in, conv_w=g_conv, q_a_g=g_q_a_g, w_q_b=g_w_q,
                 kv_a_g=g_kv_a_g, w_kv_b=g_w_kv, q_g=g_q_g, k_g=g_k_g, w_out=g_w_out)
    weights = dict(ada_w=(ada_w, m_ada_w, v_ada_w), ada_b=(ada_b, m_ada_b, v_ada_b), norm_g=(norm_g, m_norm_g, v_norm_g),
                   w_in=(w_in, m_w_in, v_w_in), conv_w=(conv_w, m_conv_w, v_conv_w), q_a_g=(q_a_g, m_q_a_g, v_q_a_g),
                   w_q_b=(w_q_b, m_w_q_b, v_w_q_b), kv_a_g=(kv_a_g, m_kv_a_g, v_kv_a_g), w_kv_b=(w_kv_b, m_w_kv_b, v_w_kv_b),
                   q_g=(q_g, m_q_g, v_q_g), k_g=(k_g, m_k_g, v_k_g), w_out=(w_out, m_w_out, v_w_out))
    names = list(grads)
    out_g, out_d, out_m, out_v = [], [], [], []
    for n in names:
        w, m, v_ = weights[n]
        shape2 = w.shape[-2:] if w.ndim == 3 else (1, w.shape[-1])
        g2 = grads[n].reshape(shape2)
        d2, m2, v2 = _adamw(w.reshape(shape2), g2, m.reshape(shape2), v_.reshape(shape2), "adamw_" + n)
        out_g.append(g2.reshape(w.shape))
        out_d.append(d2.reshape(w.shape))
        out_m.append(m2.reshape(w.shape))
        out_v.append(v2.reshape(w.shape))
    return (loss, grad_x.reshape(x.shape), *out_g, *out_d, *out_m, *out_v)
```

```python
import functools
import math

import jax
import jax.numpy as jnp
from jax import lax
from jax.experimental import pallas as pl
from jax.experimental.pallas import tpu as pltpu

F32 = jnp.float32
BF16 = jnp.bfloat16
MESH = pl.DeviceIdType.MESH

D_MODEL = 2048
D_CONV = 1024
N_HEADS = 8
QK_NOPE = 128
QK_ROPE = 64
QK_HEAD = QK_NOPE + QK_ROPE
V_HEAD = 128
D_ATTN = N_HEADS * V_HEAD
Q_LORA = 512
KV_LORA = 256
ROPE_BASE = 10000.0
IN_COLS = 4 * D_CONV + Q_LORA + KV_LORA + QK_ROPE + D_ATTN
EPS = 1e-6
ADAM_LR, ADAM_B1, ADAM_B2, ADAM_EPS, ADAM_WD, ADAM_STEP = 0.001, 0.9, 0.999, 1e-08, 0.01, 10

N_DEV = 8
LANES = 128
QK_PAD = 256
U_COLS = 6144
U_CQ, U_CKV, U_KR, U_ZA = 4096, 4608, 4864, 4928
U_TAIL = 2048
ZA_LO = U_ZA - (U_COLS - U_TAIL) - QK_ROPE
ZA_WIN = D_ATTN + LANES
CW = IN_COLS // 8
EXP_W = 896
W_LO = [(CW * d // 128) * 128 for d in range(8)]
W_OFF = [CW * d - lo for d, lo in enumerate(W_LO)]
SCALE = 1.0 / math.sqrt(QK_HEAD)
LOG2E = 1.4426950408889634
LN2 = 0.6931471805599453
NEG = -1e30
VMEM_LIMIT = 56 * 1024 * 1024

TM_ELEM = 256
TM_QKV_BWD = 256
TM_MM = 512
TQ = 1024
Q_CHAINS = 4
KV_SPLIT = 2

SM_MOD, SM_NG, SM_QAG, SM_KVAG, SM_QG, SM_KG, SM_CONV, SM_LOSS = 0, 6144, 8192, 8704, 8960, 9216, 9472, 12544
SM_COLS = 12672


def _params(sem=None):
    kw = dict(vmem_limit_bytes=VMEM_LIMIT)
    if sem is not None:
        kw["dimension_semantics"] = sem
    return pltpu.CompilerParams(**kw)


def _sigmoid(z):
    return 1.0 / (1.0 + jnp.exp(-z))


def _rot64(x):
    lane = lax.broadcasted_iota(jnp.int32, x.shape, 1)
    return jnp.where(lane < 32, pltpu.roll(x, 96, 1), pltpu.roll(x, 32, 1))


def _rope(x, cos, sin):
    return x * cos + _rot64(x) * sin


def _rope_t(d, cos, sin):
    return d * cos - _rot64(d) * sin


def _dot(a, b):
    return jnp.dot(a, b, preferred_element_type=F32)


def _dot_nt(a, b):
    return lax.dot_general(a, b, (((1,), (1,)), ((), ())), preferred_element_type=F32)


def _dot_tn(a, b):
    return lax.dot_general(a, b, (((0,), (0,)), ((), ())), preferred_element_type=F32)


def _my_index():
    return 4 * lax.axis_index("x") + 2 * lax.axis_index("y") + lax.axis_index("c")


ANY = pl.BlockSpec(memory_space=pl.ANY)


class _Gather:
    def __init__(self, blocks, relay=False, parts=1):
        self.relay = relay
        self.parts = parts
        self.rows = [b.shape[0] // parts for b in blocks]
        self.n = n = len(blocks) * parts
        self.out_shape = [jax.ShapeDtypeStruct((N_DEV,) + b.shape, b.dtype) for b in blocks]
        self.scratch = [pltpu.SemaphoreType.DMA((7 * n,)), pltpu.SemaphoreType.DMA((7 * n,)),
                        pltpu.SemaphoreType.DMA((n,))]

    @staticmethod
    def _places():
        x, y, c = lax.axis_index("x"), lax.axis_index("y"), lax.axis_index("c")
        return (x, y, c), (x, y, 1 - c), [(1 - x, y), (x, 1 - y), (1 - x, 1 - y)]

    def _src(self, ins, a):
        block, part = divmod(a, self.parts)
        return ins[block] if self.parts == 1 else ins[block].at[pl.ds(part * self.rows[block], self.rows[block])]

    def _dst(self, outs, a, place):
        block, part = divmod(a, self.parts)
        ref = outs[block].at[4 * place[0] + 2 * place[1] + place[2]]
        return ref if self.parts == 1 else ref.at[pl.ds(part * self.rows[block], self.rows[block])]

    def _copy(self, outs, sems, a, k, block, to, src=None):
        dst = self._dst(outs, a, block)
        return pltpu.make_async_remote_copy(
            src_ref=dst if src is None else src, dst_ref=dst, send_sem=sems[0].at[7 * a + k],
            recv_sem=sems[1].at[7 * a + k], device_id=to, device_id_type=MESH)

    def _first(self, ins, outs, sems):
        me, sibling, chips = self._places()
        first = []
        for a in range(self.n):
            first.append(self._copy(outs, sems, a, 0, me, sibling, src=self._src(ins, a)))
            first += [self._copy(outs, sems, a, 1 + j, me, (*chip, me[2]), src=self._src(ins, a))
                      for j, chip in enumerate(chips[:2] if self.relay else chips)]
        return first

    def _relays(self, outs, sems):
        if not self.relay:
            return []
        (x, y, c), _, _ = self._places()
        via = (jnp.where(c == 0, 1 - x, x), jnp.where(c == 0, y, 1 - y))
        to = (jnp.where(c == 0, x, 1 - x), jnp.where(c == 0, 1 - y, y))
        return [self._copy(outs, sems, a, 3, (*via, c), (*to, c)) for a in range(self.n)]

    def _passed(self, outs, sems):
        me, sibling, chips = self._places()
        return [self._copy(outs, sems, a, 4 + j, (*chip, me[2]), sibling)
                for a in range(self.n) for j, chip in enumerate(chips)]

    def _mine(self, ins, outs, sems):
        me, _, _ = self._places()
        return [pltpu.make_async_copy(self._src(ins, a), self._dst(outs, a, me), sems[2].at[a]) for a in range(self.n)]

    def start(self, ins, outs, sems):
        for cp in self._mine(ins, outs, sems) + self._first(ins, outs, sems):
            cp.start()

    def forward(self, ins, outs, sems):
        del ins
        me, _, chips = self._places()
        passed, relays = self._passed(outs, sems), self._relays(outs, sems)
        for a in range(self.n):
            for j, chip in enumerate(chips[:2] if self.relay else chips):
                self._copy(outs, sems, a, 1 + j, (*chip, me[2]), me).wait_recv()
                passed[3 * a + j].start()
            if self.relay:
                relays[a].start()
        if self.relay:
            for a in range(self.n):
                self._copy(outs, sems, a, 3, (*chips[2], me[2]), me).wait_recv()
                passed[3 * a + 2].start()

    def finish(self, ins, outs, sems):
        me, sibling, chips = self._places()
        for a in range(self.n):
            self._copy(outs, sems, a, 0, sibling, me).wait_recv()
            for j, chip in enumerate(chips):
                self._copy(outs, sems, a, 4 + j, (*chip, 1 - me[2]), me).wait_recv()
        for cp in self._first(ins, outs, sems) + self._relays(outs, sems) + self._passed(outs, sems):
            cp.wait_send()
        for cp in self._mine(ins, outs, sems):
            cp.wait()


class _ChipExchange:
    def __init__(self, arrays):
        self.n = n = len(arrays)
        self.out_shape = [jax.ShapeDtypeStruct(a.shape, a.dtype) for a in arrays]
        self.scratch = [pltpu.SemaphoreType.DMA((3 * n,)), pltpu.SemaphoreType.DMA((3 * n,))]

    def _copies(self, ins, outs, sems):
        x, y, c = lax.axis_index("x"), lax.axis_index("y"), lax.axis_index("c")
        return [pltpu.make_async_remote_copy(
            src_ref=ins[a].at[2 * px + py], dst_ref=outs[a].at[2 * x + y], send_sem=sems[0].at[3 * a + j],
            recv_sem=sems[1].at[3 * a + j], device_id=(px, py, c), device_id_type=MESH)
            for a in range(self.n) for j, (px, py) in enumerate([(1 - x, y), (x, 1 - y), (1 - x, 1 - y)])]

    def start(self, ins, outs, sems):
        for cp in self._copies(ins, outs, sems):
            cp.start()

    def forward(self, ins, outs, sems):
        pass

    def finish(self, ins, outs, sems):
        for cp in self._copies(ins, outs, sems):
            cp.wait()


def _all_gather(blocks, name):
    n = len(blocks)
    g = _Gather(blocks)

    def body(*refs):
        ins, outs, sems = refs[:n], refs[n:2 * n], refs[2 * n:]
        g.start(ins, outs, sems)
        g.forward(ins, outs, sems)
        g.finish(ins, outs, sems)

    return pl.pallas_call(body, name=name, out_shape=g.out_shape, in_specs=[ANY] * n, out_specs=[ANY] * n,
                          scratch_shapes=g.scratch)(*blocks)


class _SiblingExchange:
    def __init__(self, arrays, windowed):
        self.n = n = len(arrays)
        self.windowed = windowed
        self.out_shape = [jax.ShapeDtypeStruct((4, a.shape[0], EXP_W) if w else (4,) + a.shape[1:], a.dtype)
                          for a, w in zip(arrays, windowed)]
        self.scratch = [pltpu.SemaphoreType.DMA((4 * n,)), pltpu.SemaphoreType.DMA((4 * n,))]

    def _each(self, ins, outs, sems, act):
        x, y, c = lax.axis_index("x"), lax.axis_index("y"), lax.axis_index("c")

        def branch(c_val):
            for k in range(4):
                e = 2 * k + (1 - c_val)
                for a in range(self.n):
                    src = ins[a].at[:, pl.ds(W_LO[e], EXP_W)] if self.windowed[a] else ins[a].at[e]
                    act(pltpu.make_async_remote_copy(
                        src_ref=src, dst_ref=outs[a].at[k], send_sem=sems[0].at[4 * a + k], recv_sem=sems[1].at[4 * a + k],
                        device_id=(x, y, 1 - c), device_id_type=MESH))

        for c_val in (0, 1):
            pl.when(c == c_val)(functools.partial(branch, c_val))

    def start(self, ins, outs, sems):
        self._each(ins, outs, sems, lambda cp: cp.start())

    def forward(self, ins, outs, sems):
        pass

    def finish(self, ins, outs, sems):
        self._each(ins, outs, sems, lambda cp: cp.wait())


def _exchange(rider, arrays, name):
    n = len(arrays)

    def body(*refs):
        ins, outs, sems = refs[:n], refs[n:n + len(rider.out_shape)], refs[n + len(rider.out_shape):]
        rider.start(ins, outs, sems)
        rider.forward(ins, outs, sems)
        rider.finish(ins, outs, sems)

    return pl.pallas_call(body, name=name, out_shape=rider.out_shape, in_specs=[ANY] * n,
                          out_specs=[ANY] * len(rider.out_shape), scratch_shapes=rider.scratch)(*arrays)


def _add_window(dw_in, recv, lo_tiles):
    k, rows, _ = recv.shape

    def body(t_ref, w_ref, r_ref, o_ref):
        del t_ref
        o_ref[0] = (w_ref[...].astype(F32) + r_ref[0].astype(F32)).astype(o_ref.dtype)

    spec = pl.BlockSpec((1, rows, LANES), lambda i, j, t: (i, 0, j))
    grid_spec = pltpu.PrefetchScalarGridSpec(
        num_scalar_prefetch=1, grid=(k, EXP_W // LANES),
        in_specs=[pl.BlockSpec((rows, LANES), lambda i, j, t: (0, t[i] + j)), spec], out_specs=spec)
    return pl.pallas_call(
        body, name="rs_add_in", grid_spec=grid_spec, out_shape=jax.ShapeDtypeStruct(recv.shape, recv.dtype),
        compiler_params=_params(("parallel", "parallel")),
    )(lo_tiles, dw_in, recv)


def _final_sum(p, r, sel, name, unshift_to=None):
    _, rows, cols = p.shape
    tr = 512 if rows % 512 == 0 else rows
    out_cols = cols if unshift_to is None else unshift_to

    def body(sel_ref, p_ref, r0, r1, r2, r3, o_ref):
        own = p_ref[0].astype(F32)
        acc = None
        for k, r_ref in enumerate((r0, r1, r2, r3)):
            term = jnp.where(sel_ref[0] == k, own, r_ref[0].astype(F32))
            acc = term if acc is None else acc + term
        if unshift_to is not None:
            acc = pltpu.roll(acc, sel_ref[5], 1)[:, :unshift_to]
        o_ref[...] = acc

    def slot(k):
        return pl.BlockSpec((1, tr, cols), lambda i, t: (t[k], i, 0))

    grid_spec = pltpu.PrefetchScalarGridSpec(
        num_scalar_prefetch=1, grid=(rows // tr,), in_specs=[slot(0), slot(1), slot(2), slot(3), slot(4)],
        out_specs=pl.BlockSpec((tr, out_cols), lambda i, t: (i, 0)))
    return pl.pallas_call(
        body, name=name, grid_spec=grid_spec, out_shape=jax.ShapeDtypeStruct((rows, out_cols), F32),
        compiler_params=_params(("parallel",)),
    )(sel, p, r, r, r, r)


def _expand_w_in(w, shift):
    rows, cw = w.shape
    tr = 256

    def body(s_ref, w_ref, o_ref, buf):
        buf[...] = jnp.zeros_like(buf)
        buf[:, 0:cw] = w_ref[...]
        o_ref[...] = pltpu.roll(buf[...], s_ref[0], 1).astype(BF16)

    grid_spec = pltpu.PrefetchScalarGridSpec(
        num_scalar_prefetch=1, grid=(rows // tr,), in_specs=[pl.BlockSpec((tr, cw), lambda i, t: (i, 0))],
        out_specs=pl.BlockSpec((tr, EXP_W), lambda i, t: (i, 0)), scratch_shapes=[pltpu.VMEM((tr, EXP_W), F32)])
    return pl.pallas_call(
        body, name="expand_w_in", grid_spec=grid_spec, out_shape=jax.ShapeDtypeStruct((rows, EXP_W), BF16),
        compiler_params=_params(("arbitrary",)),
    )(shift, w)


def _pad_wq(w):
    rows, cw = w.shape

    def body(w_ref, o_ref, buf):
        buf[...] = jnp.zeros_like(buf)
        buf[:, 0:cw] = w_ref[...]
        o_ref[...] = buf[...].astype(BF16)

    return pl.pallas_call(
        body, name="pad_wq", out_shape=jax.ShapeDtypeStruct((rows, QK_PAD), BF16),
        scratch_shapes=[pltpu.VMEM((rows, QK_PAD), F32)], compiler_params=_params(),
    )(w)


def _merge_w_in(e):
    _, rows, _ = e.shape
    tr = 256

    def body(e_ref, o_ref):
        for t in range(U_COLS // LANES):
            lo, hi = t * LANES, (t + 1) * LANES
            parts = [e_ref[d, :, lo - W_LO[d]:hi - W_LO[d]] for d in range(N_DEV)
                     if CW * d < hi and CW * (d + 1) > lo]
            if not parts:
                tile = jnp.zeros((tr, LANES), BF16)
            elif len(parts) == 1:
                tile = parts[0]
            else:
                tile = (parts[0].astype(F32) + parts[1].astype(F32)).astype(BF16)
            o_ref[:, lo:hi] = tile

    return pl.pallas_call(
        body, name="merge_w_in", grid=(rows // tr,),
        in_specs=[pl.BlockSpec((N_DEV, tr, EXP_W), lambda i: (0, i, 0))],
        out_specs=pl.BlockSpec((tr, U_COLS), lambda i: (i, 0)), out_shape=jax.ShapeDtypeStruct((rows, U_COLS), BF16),
        compiler_params=_params(("parallel",)),
    )(e)


def _sum_leading(a, out_dtype, name):
    k, rows, cols = a.shape
    tr = min(rows, 1728 if rows % 1728 == 0 else rows)

    def body(a_ref, o_ref):
        acc = a_ref[0].astype(F32)
        for i in range(1, k):
            acc = acc + a_ref[i].astype(F32)
        o_ref[...] = acc.astype(out_dtype)

    return pl.pallas_call(
        body, name=name, grid=(rows // tr,),
        in_specs=[pl.BlockSpec((k, tr, cols), lambda i: (0, i, 0))],
        out_specs=pl.BlockSpec((tr, cols), lambda i: (i, 0)),
        out_shape=jax.ShapeDtypeStruct((rows, cols), out_dtype), compiler_params=_params(("parallel",)),
    )(a)


def _add_pairs(g, recv, core, name):
    k, rows, cols = recv.shape
    tr = 1728 if rows % 1728 == 0 else rows

    def body(c_ref, g_ref, r_ref, o_ref):
        del c_ref
        o_ref[...] = (g_ref[...].astype(F32) + r_ref[...].astype(F32)).astype(o_ref.dtype)

    spec = pl.BlockSpec((1, tr, cols), lambda i, j, c: (i, j, 0))
    grid_spec = pltpu.PrefetchScalarGridSpec(
        num_scalar_prefetch=1, grid=(k, rows // tr),
        in_specs=[pl.BlockSpec((1, tr, cols), lambda i, j, c: (2 * i + c[0], j, 0)), spec], out_specs=spec)
    return pl.pallas_call(
        body, name=name, grid_spec=grid_spec, out_shape=jax.ShapeDtypeStruct(recv.shape, recv.dtype),
        compiler_params=_params(("parallel", "parallel")),
    )(core, g, recv)


def _ada_mod(c16, ada_w_l, ada_b_l):
    def body(c_ref, w_ref, b_ref, o_ref):
        cv = c_ref[...]
        sc = (cv * _sigmoid(cv)).astype(BF16)
        o_ref[...] = _dot(sc, w_ref[...].astype(BF16)) + b_ref[...]

    return pl.pallas_call(
        body, name="ada_mod", out_shape=jax.ShapeDtypeStruct((c16.shape[0], ada_w_l.shape[1]), F32),
        compiler_params=_params(),
    )(c16, ada_w_l, ada_b_l)


def _ada_w_grad(c_t, dmod_my):
    def body(c_ref, d_ref, o_ref):
        cv = c_ref[...]
        sc = cv * _sigmoid(cv)
        acc = sc[:, 0:1] * d_ref[0:1, :]
        for b in range(1, N_DEV):
            acc = acc + sc[:, b:b + 1] * d_ref[b:b + 1, :]
        o_ref[...] = acc

    return pl.pallas_call(
        body, name="ada_w_grad", out_shape=jax.ShapeDtypeStruct((c_t.shape[0], dmod_my.shape[1]), F32),
        compiler_params=_params(),
    )(c_t, dmod_my)


def _norm_mod(x, norm_g, mod, pos_col, invf, sign, rider, rider_inputs):
    s, d = x.shape
    tm = min(TM_MM, s)
    n_in, n_out = len(rider_inputs), len(rider.out_shape)
    steps = s // tm

    def body(x_ref, g_ref, mod_ref, p_ref, f_ref, s_ref, *rest):
        r_ins, (h_ref, ht_ref, cos_ref, sin_ref) = rest[:n_in], rest[n_in:n_in + 4]
        r_outs, sems = rest[n_in + 4:n_in + 4 + n_out], rest[n_in + 4 + n_out:]
        pl.when(pl.program_id(0) == 0)(functools.partial(rider.start, r_ins, r_outs, sems))
        xv = x_ref[...]
        r = lax.rsqrt(jnp.mean(xv * xv, axis=-1, keepdims=True) + EPS)
        hn = xv * r * g_ref[...]
        hv = hn * (1.0 + mod_ref[:, d:2 * d]) + mod_ref[:, 0:d]
        h_ref[...] = hv.astype(BF16)
        ht_ref[...] = hv.T.astype(BF16)
        ang = p_ref[...].astype(F32) * f_ref[...]
        sg = s_ref[...]
        cos_ref[...] = jnp.cos(ang) * jnp.abs(sg)
        sin_ref[...] = jnp.sin(ang) * sg

        @pl.when(pl.program_id(0) == steps - 1)
        def _():
            rider.forward(r_ins, r_outs, sems)
            rider.finish(r_ins, r_outs, sems)

    row = pl.BlockSpec((1, LANES), lambda i: (0, 0))
    tab = pl.BlockSpec((tm, LANES), lambda i: (i, 0))
    return pl.pallas_call(
        body, name="norm_mod", grid=(steps,),
        in_specs=[pl.BlockSpec((tm, d), lambda i: (i, 0)), pl.BlockSpec((1, d), lambda i: (0, 0)),
                  pl.BlockSpec((1, 3 * d), lambda i: (0, 0)), pl.BlockSpec((tm, 1), lambda i: (i, 0)), row, row]
        + [ANY] * n_in,
        out_specs=[pl.BlockSpec((tm, d), lambda i: (i, 0)), pl.BlockSpec((d, tm), lambda i: (0, i)), tab, tab] + [ANY] * n_out,
        out_shape=[jax.ShapeDtypeStruct((s, d), BF16), jax.ShapeDtypeStruct((d, s), BF16),
                   jax.ShapeDtypeStruct((s, LANES), F32), jax.ShapeDtypeStruct((s, LANES), F32)] + rider.out_shape,
        scratch_shapes=rider.scratch, compiler_params=_params(("arbitrary",)),
    )(x, norm_g, mod, pos_col, invf, sign, *rider_inputs)


def _matmul(a, b, *, nt, out_dtype, tm, tn, name, rider=None, rider_inputs=(), a_resident=False):
    m, kdim = a.shape
    n = b.shape[0] if nt else b.shape[1]
    tm, tn = min(tm, m), min(tn, n)
    n_in = len(rider_inputs)
    n_out = len(rider.out_shape) if rider else 0
    m_steps, n_steps = m // tm, n // tn
    steps = n_steps * m_steps
    inner = n_steps if a_resident else m_steps
    tile = (lambda o, i: (o, i)) if a_resident else (lambda o, i: (i, o))

    def body(a_ref, b_ref, *rest):
        r_ins, o_ref, r_outs, sems = rest[:n_in], rest[n_in], rest[n_in + 1:n_in + 1 + n_out], rest[n_in + 1 + n_out:]
        step = pl.program_id(0) * inner + pl.program_id(1)
        if rider:
            pl.when(step == 0)(functools.partial(rider.start, r_ins, r_outs, sems))
            pl.when(step == steps // 2)(functools.partial(rider.forward, r_ins, r_outs, sems))
        o = _dot_nt(a_ref[...], b_ref[...]) if nt else _dot(a_ref[...], b_ref[...])
        o_ref[...] = o.astype(out_dtype)
        if rider:
            pl.when(step == steps - 1)(functools.partial(rider.finish, r_ins, r_outs, sems))

    if nt:
        b_spec = pl.BlockSpec((tn, kdim), lambda o, i: (tile(o, i)[1], 0))
    else:
        b_spec = pl.BlockSpec((kdim, tn), lambda o, i: (0, tile(o, i)[1]))
    out = pl.pallas_call(
        body, name=name, grid=(m_steps, n_steps) if a_resident else (n_steps, m_steps),
        in_specs=[pl.BlockSpec((tm, kdim), lambda o, i: (tile(o, i)[0], 0)), b_spec] + [ANY] * n_in,
        out_specs=[pl.BlockSpec((tm, tn), tile)] + [ANY] * n_out,
        out_shape=[jax.ShapeDtypeStruct((m, n), out_dtype)] + (rider.out_shape if rider else []),
        scratch_shapes=rider.scratch if rider else [],
        compiler_params=_params(("arbitrary", "arbitrary") if rider else ("parallel", "parallel")),
    )(a, b, *rider_inputs)
    return out if rider else out[0]


HALO = 16


def _conv_specs(tm):
    def col(j):
        return pl.BlockSpec((tm, D_CONV), lambda i: (i, j))

    def prev(j):
        return pl.BlockSpec((HALO, D_CONV), lambda i: (jnp.maximum(i * (tm // HALO) - 1, 0), j))

    return [col(0), col(1), col(2), col(3), prev(0), prev(2)]


def _conv_y(xc_ref, bc_ref, cc_ref, zc_ref, xp_ref, cp_ref, w_ref, first):
    uc = cc_ref[...].astype(F32) * xc_ref[...].astype(F32)
    up = jnp.where(first, 0.0, cp_ref[...].astype(F32) * xp_ref[...].astype(F32))
    full = jnp.concatenate([up, uc], axis=0)
    u1 = pltpu.roll(full, 1, 0)[HALO:]
    u2 = pltpu.roll(full, 2, 0)[HALO:]
    w = w_ref[...]
    conv = w[0:1] * u2 + w[1:2] * u1 + w[2:3] * uc
    z = zc_ref[...].astype(F32)
    return bc_ref[...].astype(F32) * conv * (z * _sigmoid(z))


def _conv_bwd(u, dyc, conv_w):
    s = u.shape[0]
    tm = min(TM_ELEM, s)
    cb = D_CONV
    nt = s // tm

    def body(xc_ref, bc_ref, cc_ref, zc_ref, xp_ref, cp_ref, bn_ref, zn_ref, dy_ref, dyn_ref, w_ref, du_ref, dw_ref):
        i = pl.program_id(0)
        xc, cc = xc_ref[...].astype(F32), cc_ref[...].astype(F32)
        bc, z = bc_ref[...].astype(F32), zc_ref[...].astype(F32)
        uc = cc * xc
        up = jnp.where(i == 0, 0.0, cp_ref[...].astype(F32) * xp_ref[...].astype(F32))
        full = jnp.concatenate([up, uc], axis=0)
        u1 = pltpu.roll(full, 1, 0)[HALO:]
        u2 = pltpu.roll(full, 2, 0)[HALO:]
        w = w_ref[...]
        conv = w[0:1] * u2 + w[1:2] * u1 + w[2:3] * uc
        sg = _sigmoid(z)
        sz = z * sg
        dy = dy_ref[...].astype(F32)
        dconv = dy * bc * sz
        zn = zn_ref[...].astype(F32)
        dnext = dyn_ref[...].astype(F32) * bn_ref[...].astype(F32) * (zn * _sigmoid(zn))
        dnext = jnp.where(i == nt - 1, 0.0, dnext)
        fullb = jnp.concatenate([dconv, dnext], axis=0)
        nb = tm + HALO
        d1 = pltpu.roll(fullb, nb - 1, 0)[:tm]
        d2 = pltpu.roll(fullb, nb - 2, 0)[:tm]
        duc = w[2:3] * dconv + w[1:2] * d1 + w[0:1] * d2
        dzc = dy * bc * conv * (sg * (1.0 + z * (1.0 - sg)))
        du_ref[...] = jnp.concatenate([duc * cc, dy * conv * sz, duc * xc, dzc], axis=1).astype(BF16)
        dw = jnp.concatenate([jnp.sum(dconv * u2, axis=0, keepdims=True), jnp.sum(dconv * u1, axis=0, keepdims=True),
                              jnp.sum(dconv * uc, axis=0, keepdims=True), jnp.zeros((5, cb), F32)], axis=0)

        @pl.when(i == 0)
        def _():
            dw_ref[...] = dw

        @pl.when(i > 0)
        def _():
            dw_ref[...] += dw

    def col(j):
        return pl.BlockSpec((tm, cb), lambda i: (i, j))

    def prev(j):
        return pl.BlockSpec((HALO, cb), lambda i: (jnp.maximum(i * (tm // HALO) - 1, 0), j))

    def nxt(j):
        return pl.BlockSpec((HALO, cb), lambda i: (jnp.minimum((i + 1) * (tm // HALO), s // HALO - 1), j))

    return pl.pallas_call(
        body, name="conv_bwd", grid=(nt,),
        in_specs=[col(0), col(1), col(2), col(3), prev(0), prev(2), nxt(1), nxt(3), col(0), nxt(0),
                  pl.BlockSpec((3, cb), lambda i: (0, 0))],
        out_specs=[pl.BlockSpec((tm, 4 * cb), lambda i: (i, 0)), pl.BlockSpec((8, cb), lambda i: (0, 0))],
        out_shape=[jax.ShapeDtypeStruct((s, U_COLS), BF16), jax.ShapeDtypeStruct((8, cb), F32)],
        compiler_params=_params(("arbitrary",)),
    )(u, u, u, u, u, u, u, u, dyc, dyc, conv_w)


def _qkv_specs(tm):
    return [pl.BlockSpec((tm, Q_LORA), lambda i: (i, U_CQ // Q_LORA)),
            pl.BlockSpec((tm, KV_LORA), lambda i: (i, U_CKV // KV_LORA)),
            pl.BlockSpec((tm, LANES), lambda i: (i, U_KR // LANES)),
            pl.BlockSpec((tm, LANES), lambda i: (i, 0)), pl.BlockSpec((tm, LANES), lambda i: (i, 0))]


def _full(shape):
    return pl.BlockSpec(shape, lambda i: (0,) * len(shape))


def _k_rope_lanes(blk):
    lane = lax.broadcasted_iota(jnp.int32, blk.shape, 1)
    return jnp.where(lane < QK_ROPE, blk, 0.0)


def _qkv_fwd(u, cos, sin, wq, wkv, qag, kvag, qg, kg):
    s = u.shape[0]
    tm = min(TM_ELEM, s)

    def body(cq_ref, ckv_ref, kr_ref, cos_ref, sin_ref, wq_ref, wkv_ref, qag_ref, kvag_ref, qg_ref, kg_ref,
             q_ref, k_ref, v_ref):
        cq = cq_ref[...].astype(F32)
        cqn = (cq * lax.rsqrt(jnp.mean(cq * cq, axis=-1, keepdims=True) + EPS) * qag_ref[...]).astype(BF16)
        ckv = ckv_ref[...].astype(F32)
        ckvn = (ckv * lax.rsqrt(jnp.mean(ckv * ckv, axis=-1, keepdims=True) + EPS) * kvag_ref[...]).astype(BF16)
        kr = _k_rope_lanes(kr_ref[...].astype(F32))
        cosv, sinv, qgv, kgv = cos_ref[...], sin_ref[...], qg_ref[...], kg_ref[...]
        ss_r = jnp.sum(kr * kr, axis=-1, keepdims=True)
        krr = _rope(kr * kgv[:, QK_NOPE:], cosv, sinv)
        qf = _dot(cqn, wq_ref[...])
        kvf = _dot(ckvn, wkv_ref[...])
        heads = range(N_HEADS)
        qh = [qf[:, QK_PAD * h:QK_PAD * (h + 1)] for h in heads]
        kn = [kvf[:, 2 * V_HEAD * h:2 * V_HEAD * h + QK_NOPE] for h in heads]
        rq = [lax.rsqrt(jnp.sum(qh[h] * qh[h], axis=-1, keepdims=True) * (1.0 / QK_HEAD) + EPS) for h in heads]
        rk = [lax.rsqrt((jnp.sum(kn[h] * kn[h], axis=-1, keepdims=True) + ss_r) * (1.0 / QK_HEAD) + EPS) for h in heads]
        for h in heads:
            qn = qh[h] * rq[h] * qgv
            qo = jnp.concatenate([qn[:, :QK_NOPE], _rope(qn[:, QK_NOPE:], cosv, sinv)], axis=1) * (SCALE * LOG2E)
            q_ref[h] = qo.astype(BF16)
            vh = kvf[:, 2 * V_HEAD * h + QK_NOPE:2 * V_HEAD * (h + 1)]
            k_ref[h] = jnp.concatenate([kn[h] * kgv[:, :QK_NOPE] * rk[h], krr * rk[h]], axis=1).astype(BF16)
            v_ref[h] = jnp.concatenate([vh, jnp.ones_like(vh)], axis=1).astype(BF16)

    return pl.pallas_call(
        body, name="qkv_fwd", grid=(s // tm,),
        in_specs=_qkv_specs(tm) + [_full((Q_LORA, N_HEADS * QK_PAD)), _full((KV_LORA, 2 * D_ATTN)),
                                   _full((1, Q_LORA)), _full((1, KV_LORA)), _full((1, QK_PAD)), _full((1, QK_PAD))],
        out_specs=[pl.BlockSpec((N_HEADS, tm, QK_PAD), lambda i: (0, i, 0)),
                   pl.BlockSpec((N_HEADS, tm, QK_PAD), lambda i: (0, i, 0)),
                   pl.BlockSpec((N_HEADS, tm, 2 * V_HEAD), lambda i: (0, i, 0))],
        out_shape=[jax.ShapeDtypeStruct((N_HEADS, s, QK_PAD), BF16), jax.ShapeDtypeStruct((N_HEADS, s, QK_PAD), BF16),
                   jax.ShapeDtypeStruct((N_HEADS, s, 2 * V_HEAD), BF16)],
        compiler_params=_params(("parallel",)),
    )(u, u, u, cos, sin, wq, wkv, qag, kvag, qg, kg)


def _qkv_bwd(u, cos, sin, dq, dk, dv, dza, wq, wkv, qag, kvag, qg, kg, du):
    s = u.shape[0]
    tm = min(TM_QKV_BWD, s)
    nt = s // tm

    def body(cq_ref, ckv_ref, kr_ref, cos_ref, sin_ref, dq_ref, dk_ref, dv_ref, dza_ref, wq_ref, wkv_ref, qag_ref,
             kvag_ref, qg_ref, kg_ref, du_in, du_ref, dwq_ref, dwkv_ref, dqag_ref, dkvag_ref, dqg_ref, dkg_ref,
             dwq_acc, dwkv_acc):
        del du_in
        i = pl.program_id(0)

        @pl.when(i == 0)
        def _():
            dwq_acc[...] = jnp.zeros_like(dwq_acc)
            dwkv_acc[...] = jnp.zeros_like(dwkv_acc)

        cq = cq_ref[...].astype(F32)
        rqa = lax.rsqrt(jnp.mean(cq * cq, axis=-1, keepdims=True) + EPS)
        xq = cq * rqa
        qagv = qag_ref[...]
        cqn = (xq * qagv).astype(BF16)
        ckv = ckv_ref[...].astype(F32)
        rkva = lax.rsqrt(jnp.mean(ckv * ckv, axis=-1, keepdims=True) + EPS)
        xkv = ckv * rkva
        kvagv = kvag_ref[...]
        ckvn = (xkv * kvagv).astype(BF16)
        kr = _k_rope_lanes(kr_ref[...].astype(F32))
        cosv, sinv, qgv, kgv = cos_ref[...], sin_ref[...], qg_ref[...], kg_ref[...]
        ss_r = jnp.sum(kr * kr, axis=-1, keepdims=True)
        dqg = jnp.zeros((1, QK_PAD), F32)
        dkg = jnp.zeros((1, QK_PAD), F32)
        dkr = jnp.zeros((tm, LANES), F32)
        qf = _dot(cqn, wq_ref[...])
        kvf = _dot(ckvn, wkv_ref[...])
        heads = range(N_HEADS)
        qh = [qf[:, QK_PAD * h:QK_PAD * (h + 1)] for h in heads]
        kn = [kvf[:, 2 * V_HEAD * h:2 * V_HEAD * h + QK_NOPE] for h in heads]
        rq = [lax.rsqrt(jnp.sum(qh[h] * qh[h], axis=-1, keepdims=True) * (1.0 / QK_HEAD) + EPS) for h in heads]
        rk = [lax.rsqrt((jnp.sum(kn[h] * kn[h], axis=-1, keepdims=True) + ss_r) * (1.0 / QK_HEAD) + EPS) for h in heads]
        xh = [qh[h] * rq[h] for h in heads]
        xk = [jnp.concatenate([kn[h], kr], axis=1) * rk[h] for h in heads]
        dyq, dyk = [], []
        for h in heads:
            g = dq_ref[h].astype(F32) * SCALE
            dyq.append(jnp.concatenate([g[:, :QK_NOPE], _rope_t(g[:, QK_NOPE:], cosv, sinv)], axis=1))
            gk = dk_ref[h].astype(F32)
            dyk.append(jnp.concatenate([gk[:, :QK_NOPE], _rope_t(gk[:, QK_NOPE:], cosv, sinv)], axis=1))
        for h in heads:
            dqg = dqg + jnp.sum(dyq[h] * xh[h], axis=0, keepdims=True)
            dkg = dkg + jnp.sum(dyk[h] * xk[h], axis=0, keepdims=True)
        gdy = [dyq[h] * qgv for h in heads]
        gdyk = [dyk[h] * kgv for h in heads]
        tq_ = [jnp.sum(gdy[h] * xh[h], axis=-1, keepdims=True) * (1.0 / QK_HEAD) for h in heads]
        tk_ = [jnp.sum(gdyk[h] * xk[h], axis=-1, keepdims=True) * (1.0 / QK_HEAD) for h in heads]
        dqf = [(rq[h] * (gdy[h] - xh[h] * tq_[h])).astype(BF16) for h in heads]
        dkvf = []
        for h in heads:
            dxk = rk[h] * (gdyk[h] - xk[h] * tk_[h])
            dkr = dkr + dxk[:, QK_NOPE:]
            dkvf += [dxk[:, :QK_NOPE].astype(BF16), dv_ref[h]]
        dqf_b, dkvf_b = jnp.concatenate(dqf, axis=1), jnp.concatenate(dkvf, axis=1)
        dwq_acc[...] += _dot_tn(cqn, dqf_b)
        dwkv_acc[...] += _dot_tn(ckvn, dkvf_b)
        dcqn = _dot_nt(dqf_b, wq_ref[...])
        dckvn = _dot_nt(dkvf_b, wkv_ref[...])
        dqag = jnp.sum(dcqn * xq, axis=0, keepdims=True)
        dkvag = jnp.sum(dckvn * xkv, axis=0, keepdims=True)
        gq = dcqn * qagv
        dcq = rqa * (gq - xq * jnp.mean(gq * xq, axis=-1, keepdims=True))
        gkv = dckvn * kvagv
        dckv = rkva * (gkv - xkv * jnp.mean(gkv * xkv, axis=-1, keepdims=True))
        win = pltpu.roll(jnp.concatenate([dza_ref[...].astype(F32), jnp.zeros((tm, LANES), F32)], axis=1), QK_ROPE, 1)
        win = win + jnp.concatenate([dkr, jnp.zeros((tm, D_ATTN), F32)], axis=1)
        du_ref[...] = jnp.concatenate([dcq, dckv, win, jnp.zeros((tm, U_TAIL - ZA_LO - ZA_WIN), F32)], axis=1).astype(BF16)

        @pl.when(i == 0)
        def _():
            dqag_ref[...] = dqag
            dkvag_ref[...] = dkvag
            dqg_ref[...] = dqg
            dkg_ref[...] = dkg

        @pl.when(i > 0)
        def _():
            dqag_ref[...] += dqag
            dkvag_ref[...] += dkvag
            dqg_ref[...] += dqg
            dkg_ref[...] += dkg

        @pl.when(i == nt - 1)
        def _():
            dwq_ref[...] = dwq_acc[...].astype(BF16)
            dwkv_ref[...] = dwkv_acc[...].astype(BF16)

    head = lambda w: pl.BlockSpec((N_HEADS, tm, w), lambda i: (0, i, 0))
    wq_shape, wkv_shape = (Q_LORA, N_HEADS * QK_PAD), (KV_LORA, 2 * D_ATTN)
    return pl.pallas_call(
        body, name="qkv_bwd", grid=(nt,),
        in_specs=_qkv_specs(tm) + [head(QK_PAD), head(QK_PAD), head(V_HEAD), pl.BlockSpec((tm, D_ATTN), lambda i: (i, 0)),
                                   _full(wq_shape), _full(wkv_shape), _full((1, Q_LORA)), _full((1, KV_LORA)),
                                   _full((1, QK_PAD)), _full((1, QK_PAD)), ANY],
        out_specs=[pl.BlockSpec((tm, U_TAIL), lambda i: (i, U_COLS // U_TAIL - 1)), _full(wq_shape), _full(wkv_shape),
                   _full((1, Q_LORA)), _full((1, KV_LORA)), _full((1, QK_PAD)), _full((1, QK_PAD))],
        out_shape=[jax.ShapeDtypeStruct(du.shape, du.dtype), jax.ShapeDtypeStruct(wq_shape, BF16),
                   jax.ShapeDtypeStruct(wkv_shape, BF16), jax.ShapeDtypeStruct((1, Q_LORA), F32),
                   jax.ShapeDtypeStruct((1, KV_LORA), F32), jax.ShapeDtypeStruct((1, QK_PAD), F32),
                   jax.ShapeDtypeStruct((1, QK_PAD), F32)],
        scratch_shapes=[pltpu.VMEM(wq_shape, F32), pltpu.VMEM(wkv_shape, F32)],
        input_output_aliases={15: 0}, compiler_params=_params(("arbitrary",)),
    )(u, u, u, cos, sin, dq, dk, dv, dza, wq, wkv, qag, kvag, qg, kg, du)


def _flash_fwd(q, k, v):
    nh, s, _ = q.shape
    tq = min(TQ, s)
    nkv = KV_SPLIT
    tk = tq // nkv
    nq = s // tq
    nch = Q_CHAINS
    tc = tq // nch

    def body(q_ref, k_ref, v_ref, o_ref, lse_ref):
        i = pl.program_id(1)
        chains = [q_ref[0, r * tc:(r + 1) * tc, :] for r in range(nch)]

        def unit(r, j, carry, shift=None):
            m, acc = carry
            rows = pl.ds(pl.multiple_of(j * tk, tk), tk)
            sc = _dot_nt(chains[r], k_ref[0, rows, :])
            if shift is not None:
                qi = lax.broadcasted_iota(jnp.int32, sc.shape, 0)
                ki = lax.broadcasted_iota(jnp.int32, sc.shape, 1) + shift
                sc = jnp.where(ki <= qi, sc, NEG)
            m_new = jnp.maximum(m, jnp.max(sc, axis=-1, keepdims=True))
            p = jnp.exp2(sc - m_new).astype(BF16)
            return m_new, jnp.exp2(m - m_new) * acc + _dot(p, v_ref[0, rows, :])

        def trip(p, carry):
            for b in range(nkv):
                rows = pl.ds(pl.multiple_of((nkv * p + b) * tk, tk), tk)
                kb, vb = k_ref[0, rows, :], v_ref[0, rows, :]
                sc = [_dot_nt(chains[r], kb) for r in range(nch)]
                m_new = [jnp.maximum(carry[r][0], jnp.max(sc[r], axis=-1, keepdims=True)) for r in range(nch)]
                ps = [jnp.exp2(sc[r] - m_new[r]).astype(BF16) for r in range(nch)]
                carry = tuple((m_new[r], jnp.exp2(carry[r][0] - m_new[r]) * carry[r][1] + _dot(ps[r], vb))
                              for r in range(nch))
            return carry

        init = (jnp.full((tc, 1), NEG, F32), jnp.zeros((tc, 2 * V_HEAD), F32))
        carry = list(lax.fori_loop(0, i, trip, (init,) * nch))
        for b in range(nkv):
            for r in range(nch):
                shift = b * tk - r * tc
                if shift < tc:
                    carry[r] = unit(r, nkv * i + b, carry[r], None if shift + tk - 1 <= 0 else shift)
        for r, (m, acc) in enumerate(carry):
            l = acc[:, V_HEAD:]
            o_ref[r * tc:(r + 1) * tc, :] = (acc[:, :V_HEAD] / l).astype(BF16)
            lse = m + jnp.log(l[:, 0:1]) * LOG2E
            lse_ref[0, :, r * tc:(r + 1) * tc] = jnp.broadcast_to(lse, (tc, LANES)).T[0:1, :]

    return pl.pallas_call(
        body, name="flash_fwd", grid=(nh, nq),
        in_specs=[pl.BlockSpec((1, tq, QK_PAD), lambda h, i: (h, i, 0)),
                  pl.BlockSpec((1, s, QK_PAD), lambda h, i: (h, 0, 0)),
                  pl.BlockSpec((1, s, 2 * V_HEAD), lambda h, i: (h, 0, 0))],
        out_specs=[pl.BlockSpec((tq, V_HEAD), lambda h, i: (i, h)), pl.BlockSpec((1, 1, tq), lambda h, i: (h, 0, i))],
        out_shape=[jax.ShapeDtypeStruct((s, nh * V_HEAD), BF16), jax.ShapeDtypeStruct((nh, 1, s), F32)],
        compiler_params=_params(("parallel", "arbitrary")),
    )(q, k, v)


def _flash_bwd(q, k, v, do, lse, delta):
    nh, s, _ = q.shape
    tq = min(TQ, s)
    nq = s // tq

    def body(q_ref, k_ref, v_ref, do_ref, lse_ref, dl_ref, dq_ref, dk_ref, dv_ref, dq_acc):
        j = pl.program_id(1)

        @pl.when(j == 0)
        def _():
            dq_acc[...] = jnp.zeros_like(dq_acc)

        kj, vj = k_ref[0], v_ref[0]

        def block(kk, vv, qq, dd, lse, dl, masked):
            st = _dot_nt(kk, qq)
            pt = jnp.exp2(st - lse)
            if masked:
                ki = lax.broadcasted_iota(jnp.int32, st.shape, 0)
                qx = lax.broadcasted_iota(jnp.int32, st.shape, 1)
                pt = jnp.where(ki <= qx, pt, 0.0)
            ddv = _dot(pt.astype(BF16), dd)
            dst = (pt * (_dot_nt(vv, dd) - dl)).astype(BF16)
            ddk = _dot(dst, qq)
            return ddk, ddv, _dot_tn(dst, kk)

        def step(i, carry):
            dk, dv = carry
            rows = pl.ds(pl.multiple_of(i * tq, tq), tq)
            ddk, ddv, ddq = block(kj, vj, q_ref[0, rows, :], do_ref[rows, :], lse_ref[0, pl.ds(i, 1), :],
                                  dl_ref[0, pl.ds(i, 1), :], False)
            dq_acc[rows, :] += ddq
            return dk + ddk, dv + ddv

        th = tq // 2
        lse_j, dl_j = lse_ref[0, pl.ds(j, 1), :], dl_ref[0, pl.ds(j, 1), :]
        parts = []
        for kh, qh, masked in ((0, 0, True), (0, 1, False), (1, 1, True)):
            rows = pl.ds(pl.multiple_of(j * tq + qh * th, th), th)
            ks, qs = slice(kh * th, (kh + 1) * th), slice(qh * th, (qh + 1) * th)
            ddk, ddv, ddq = block(kj[ks], vj[ks], q_ref[0, rows, :], do_ref[rows, :], lse_j[:, qs], dl_j[:, qs], masked)
            dq_acc[rows, :] += ddq
            parts.append((ddk, ddv))
        carry = (jnp.concatenate([parts[0][0] + parts[1][0], parts[2][0]], axis=0),
                 jnp.concatenate([parts[0][1] + parts[1][1], parts[2][1]], axis=0))
        dk, dv = lax.fori_loop(j + 1, nq, step, carry)
        dk_ref[0] = (dk * LN2).astype(BF16)
        dv_ref[0] = dv.astype(BF16)

        @pl.when(j == nq - 1)
        def _():
            dq_ref[0] = dq_acc[...].astype(BF16)

    return pl.pallas_call(
        body, name="flash_bwd", grid=(nh, nq),
        in_specs=[pl.BlockSpec((1, s, QK_PAD), lambda h, j: (h, 0, 0)),
                  pl.BlockSpec((1, tq, QK_PAD), lambda h, j: (h, j, 0)),
                  pl.BlockSpec((1, tq, V_HEAD), lambda h, j: (h, j, 0)),
                  pl.BlockSpec((s, V_HEAD), lambda h, j: (0, h)),
                  pl.BlockSpec((1, nq, tq), lambda h, j: (h, 0, 0)),
                  pl.BlockSpec((1, nq, tq), lambda h, j: (h, 0, 0))],
        out_specs=[pl.BlockSpec((1, s, QK_PAD), lambda h, j: (h, 0, 0)),
                   pl.BlockSpec((1, tq, QK_PAD), lambda h, j: (h, j, 0)),
                   pl.BlockSpec((1, tq, V_HEAD), lambda h, j: (h, j, 0))],
        out_shape=[jax.ShapeDtypeStruct((nh, s, QK_PAD), BF16), jax.ShapeDtypeStruct((nh, s, QK_PAD), BF16),
                   jax.ShapeDtypeStruct((nh, s, V_HEAD), BF16)],
        scratch_shapes=[pltpu.VMEM((s, QK_PAD), F32)],
        compiler_params=_params(("parallel", "arbitrary")),
    )(q, k, v, do, lse, delta)


def _tail(x, target, o, u, mod, w_out, conv_w):
    s, d = x.shape
    tm = min(TM_ELEM, s)

    def body(x_ref, t_ref, o_ref, za_ref, mod_ref, w_ref, xc_ref, bc_ref, cc_ref, zc_ref, xp_ref, cp_ref, cw_ref,
             gx_ref, dy_ref, ycat_ref, dyc_ref, do_ref, du_ref, delta_ref, dgate_ref, loss_ref):
        i = pl.program_id(0)
        yc = _conv_y(xc_ref, bc_ref, cc_ref, zc_ref, xp_ref, cp_ref, cw_ref, i == 0)
        za = pltpu.roll(za_ref[:, ZA_LO:ZA_LO + ZA_WIN].astype(F32), ZA_WIN - QK_ROPE, 1)[:, :D_ATTN]
        ov = o_ref[...].astype(F32)
        sg = _sigmoid(za)
        sl = za * sg
        ya = ov * sl
        ycat = jnp.concatenate([yc.astype(BF16), ya.astype(BF16)], axis=1)
        ycat_ref[...] = jnp.concatenate([yc.T, ya.T], axis=0).astype(BF16)
        y = _dot(ycat, w_ref[...])
        gate = mod_ref[:, 2 * d:3 * d]
        e = x_ref[...] + gate * y - t_ref[...]
        dout = e * (1.0 / d)
        gx_ref[...] = dout
        dy = (dout * gate).astype(BF16)
        dy_ref[...] = dy
        dycat = _dot_nt(dy, w_ref[...])
        dyc_ref[...] = dycat[:, :D_CONV].astype(BF16)
        dya = dycat[:, D_CONV:]
        dov = dya * sl
        do_ref[...] = dov.astype(BF16)
        du_ref[...] = (dya * ov * (sg * (1.0 + za * (1.0 - sg)))).astype(BF16)
        prod_t = (dov * ov).T
        for h in range(N_HEADS):
            delta_ref[h] = jnp.sum(prod_t[V_HEAD * h:V_HEAD * (h + 1), :], axis=0, keepdims=True)
        dgate = jnp.sum(dout * y, axis=0, keepdims=True)
        part = jnp.sum(jnp.sum(e * e, axis=0, keepdims=True), axis=1, keepdims=True) * (0.5 / d)
        part = jnp.broadcast_to(part, (1, LANES))

        @pl.when(i == 0)
        def _():
            dgate_ref[...] = dgate
            loss_ref[...] = part

        @pl.when(i > 0)
        def _():
            dgate_ref[...] += dgate
            loss_ref[...] += part

    tok = lambda w: pl.BlockSpec((tm, w), lambda i: (i, 0))
    return pl.pallas_call(
        body, name="tail", grid=(s // tm,),
        in_specs=[tok(d), tok(d), tok(D_ATTN), pl.BlockSpec((tm, U_TAIL), lambda i: (i, U_COLS // U_TAIL - 1)),
                  _full((1, 3 * d)), _full((d, d))] + _conv_specs(tm) + [_full((3, D_CONV))],
        out_specs=[tok(d), tok(d), pl.BlockSpec((d, tm), lambda i: (0, i)), tok(D_CONV), tok(D_ATTN), tok(D_ATTN),
                   pl.BlockSpec((N_HEADS, 1, tm), lambda i: (0, 0, i)), _full((1, d)), _full((1, LANES))],
        out_shape=[jax.ShapeDtypeStruct((s, d), F32), jax.ShapeDtypeStruct((s, d), BF16),
                   jax.ShapeDtypeStruct((d, s), BF16), jax.ShapeDtypeStruct((s, D_CONV), BF16),
                   jax.ShapeDtypeStruct((s, D_ATTN), BF16), jax.ShapeDtypeStruct((s, D_ATTN), BF16),
                   jax.ShapeDtypeStruct((N_HEADS, 1, s), F32), jax.ShapeDtypeStruct((1, d), F32),
                   jax.ShapeDtypeStruct((1, LANES), F32)],
        compiler_params=_params(("arbitrary",)),
    )(x, target, o, u, mod, w_out, u, u, u, u, u, u, conv_w)


def _norm_bwd(x, dh, gx1, norm_g, mod):
    s, d = x.shape
    tm = min(TM_MM, s)

    def body(x_ref, dh_ref, gx_ref, g_ref, mod_ref, o_ref, dshift_ref, dscale_ref, dg_ref):
        i = pl.program_id(0)
        xv, dhv, gv = x_ref[...], dh_ref[...].astype(F32), g_ref[...]
        r = lax.rsqrt(jnp.mean(xv * xv, axis=-1, keepdims=True) + EPS)
        xn = xv * r
        dhn = dhv * (1.0 + mod_ref[:, d:2 * d])
        dxn = dhn * gv
        o_ref[...] = gx_ref[...] + r * (dxn - xn * jnp.mean(dxn * xn, axis=-1, keepdims=True))
        dshift = jnp.sum(dhv, axis=0, keepdims=True)
        dscale = jnp.sum(dhv * xn * gv, axis=0, keepdims=True)
        dg = jnp.sum(dhn * xn, axis=0, keepdims=True)

        @pl.when(i == 0)
        def _():
            dshift_ref[...] = dshift
            dscale_ref[...] = dscale
            dg_ref[...] = dg

        @pl.when(i > 0)
        def _():
            dshift_ref[...] += dshift
            dscale_ref[...] += dscale
            dg_ref[...] += dg

    tok = pl.BlockSpec((tm, d), lambda i: (i, 0))
    row = jax.ShapeDtypeStruct((1, d), F32)
    return pl.pallas_call(
        body, name="norm_bwd", grid=(s // tm,),
        in_specs=[tok, tok, tok, _full((1, d)), _full((1, 3 * d))],
        out_specs=[tok, _full((1, d)), _full((1, d)), _full((1, d))],
        out_shape=[jax.ShapeDtypeStruct((s, d), F32), row, row, row],
        compiler_params=_params(("arbitrary",)),
    )(x, dh, gx1, norm_g, mod)


def _adamw(w, g, m, v, name):
    rows, cols = w.shape
    tr = 256 if rows % 256 == 0 else rows

    def body(w_ref, g_ref, m_ref, v_ref, d_ref, nm_ref, nv_ref):
        gv = g_ref[...]
        nm = ADAM_B1 * m_ref[...] + (1.0 - ADAM_B1) * gv
        nv = ADAM_B2 * v_ref[...] + (1.0 - ADAM_B2) * (gv * gv)
        m_hat = nm / (1.0 - ADAM_B1 ** ADAM_STEP)
        v_hat = nv / (1.0 - ADAM_B2 ** ADAM_STEP)
        d_ref[...] = -ADAM_LR * (m_hat / (jnp.sqrt(v_hat) + ADAM_EPS) + ADAM_WD * w_ref[...])
        nm_ref[...] = nm
        nv_ref[...] = nv

    spec = pl.BlockSpec((tr, cols), lambda i: (i, 0))
    shape = jax.ShapeDtypeStruct((rows, cols), F32)
    return pl.pallas_call(
        body, name=name, grid=(rows // tr,), in_specs=[spec] * 4, out_specs=[spec] * 3, out_shape=[shape] * 3,
        compiler_params=_params(("parallel",)),
    )(w, g, m, v)


def _pad_cols(a, n):
    return jnp.pad(a, ((0, 0), (0, n - a.shape[1])))


def kernel(x, c, positions, ada_w, ada_b, norm_g, w_in, conv_w, q_a_g, w_q_b, kv_a_g, w_kv_b, q_g, k_g, w_out, loss_target, m_ada_w, m_ada_b, m_norm_g, m_w_in, m_conv_w, m_q_a_g, m_w_q_b, m_kv_a_g, m_w_kv_b, m_q_g, m_k_g, m_w_out, v_ada_w, v_ada_b, v_norm_g, v_w_in, v_conv_w, v_q_a_g, v_w_q_b, v_kv_a_g, v_w_kv_b, v_q_g, v_k_g, v_w_out):
    me = _my_index()
    s = x.shape[1]
    nq = s // min(TQ, s)
    x2, tgt = x[0], loss_target[0]
    w_in_l, w_q_l, w_kv_l, w_out_l, conv_l, ada_w_l = w_in[0], w_q_b[0], w_kv_b[0], w_out[0], conv_w[0], ada_w[0]
    ada_cols = ada_w_l.shape[1]

    small = jnp.concatenate([c.reshape(-1, LANES), conv_l.reshape(-1, LANES), jnp.zeros((5, LANES), F32)], axis=0)
    (small_g,) = _all_gather([small], "gather_c")
    c_all = small_g[:, :D_MODEL // LANES].reshape(N_DEV, D_MODEL)
    conv_g = small_g[:, D_MODEL // LANES:D_MODEL // LANES + 3].transpose(1, 0, 2).reshape(3, D_CONV)

    ada_b_l = lax.dynamic_slice(ada_b, (0, me * ada_cols), (1, ada_cols))
    mod_cols = _ada_mod(jnp.pad(c_all, ((0, 8), (0, 0))), ada_w_l, ada_b_l)[:N_DEV]
    (mod_g,) = _all_gather([mod_cols], "gather_mod")
    mod = lax.dynamic_index_in_dim(mod_g, me, axis=1, keepdims=False).reshape(1, 3 * D_MODEL)

    half = jnp.arange(0, QK_ROPE, 2, dtype=F32) / QK_ROPE
    inv_freq = ROPE_BASE ** (-half)
    zeros64 = jnp.zeros((LANES - QK_ROPE,), F32)
    invf = jnp.concatenate([inv_freq, inv_freq, zeros64]).reshape(1, LANES)
    sign = jnp.concatenate([-jnp.ones((32,), F32), jnp.ones((32,), F32), zeros64]).reshape(1, LANES)
    qg_p, kg_p = _pad_cols(q_g, QK_PAD), _pad_cols(k_g, QK_PAD)

    my_off = ((CW * me) % LANES).astype(jnp.int32)
    win = [_expand_w_in(w_in_l, my_off.reshape(1))]
    h, h_t, cos, sin, win_g = _norm_mod(x2, norm_g, mod, positions.reshape(s, 1), invf, sign, _Gather(win, relay=True, parts=4), win)
    w_in_p = _merge_w_in(win_g)
    rest = [_pad_wq(w_q_l), w_kv_l.astype(BF16), w_out_l.astype(BF16)]
    u, wq_g, wkv_g, w_out_g = _matmul(h, w_in_p, nt=False, out_dtype=BF16, tm=2 * TM_MM, tn=1024, name="in_proj",
                                      rider=_Gather(rest), rider_inputs=rest)
    w_out_g = w_out_g.reshape(D_MODEL, D_MODEL)
    wq_g = wq_g.transpose(1, 0, 2).reshape(Q_LORA, N_HEADS * QK_PAD)
    wkv_g = wkv_g.transpose(1, 0, 2).reshape(KV_LORA, 2 * D_ATTN)
    q, k, v = _qkv_fwd(u, cos, sin, wq_g, wkv_g, q_a_g, kv_a_g, qg_p, kg_p)
    o, lse = _flash_fwd(q, k, v)
    gx1, dy, ycat_t, dyc, do, dza, delta, dgate, loss_row = _tail(x2, tgt, o, u, mod, w_out_g, conv_g)

    dq, dk, dv = _flash_bwd(q, k, v, do, lse.reshape(N_HEADS, nq, s // nq), delta.reshape(N_HEADS, nq, s // nq))
    du, dconv = _conv_bwd(u, dyc, conv_g)
    du, dwq, dwkv, dqag, dkvag, dqg, dkg = _qkv_bwd(u, cos, sin, dq, dk, dv, dza, wq_g, wkv_g, q_a_g, kv_a_g, qg_p, kg_p, du)
    dwq = dwq.reshape(Q_LORA, N_HEADS, QK_PAD).transpose(1, 0, 2)
    dwkv = dwkv.reshape(KV_LORA, N_HEADS, 2 * V_HEAD).transpose(1, 0, 2)
    dw_in = _matmul(h_t, du, nt=False, out_dtype=BF16, tm=TM_MM, tn=768, name="dw_in")
    first = [dw_in, dwq, dwkv]
    dw_out, r_in, r_q, r_kv = _matmul(ycat_t, dy, nt=False, out_dtype=BF16, tm=TM_MM, tn=512, name="dw_out",
                                      rider=_SiblingExchange(first, [True, False, False]), rider_inputs=first)
    dw_out = dw_out.reshape(N_DEV, D_MODEL // N_DEV, D_MODEL)
    (r_out,) = _exchange(_SiblingExchange([dw_out], [False]), [dw_out], "rs_sibling_out")
    core = lax.axis_index("c").astype(jnp.int32)
    lo_tiles = ((CW * (2 * jnp.arange(4, dtype=jnp.int32) + core)) // LANES).astype(jnp.int32)
    pairs = [_add_window(dw_in, r_in, lo_tiles), _add_pairs(dwq, r_q, core.reshape(1), "rs_add_q"),
             _add_pairs(dwkv, r_kv, core.reshape(1), "rs_add_kv"), _add_pairs(dw_out, r_out, core.reshape(1), "rs_add_out")]
    dh, *quads = _matmul(du, w_in_p, nt=True, out_dtype=BF16, tm=2 * TM_MM, tn=512, name="dh",
                         rider=_ChipExchange(pairs), rider_inputs=pairs, a_resident=True)
    my_chip = 2 * lax.axis_index("x") + lax.axis_index("y")
    written = jnp.where(jnp.arange(4) == my_chip, (jnp.arange(4) + 1) % 4, jnp.arange(4))
    sel = jnp.concatenate([my_chip.reshape(1), written, ((EXP_W - my_off) % EXP_W).reshape(1)]).astype(jnp.int32)
    g_w_in = _final_sum(pairs[0], quads[0], sel, "rs_sum_in", unshift_to=CW)
    g_w_q = _final_sum(pairs[1], quads[1], sel, "rs_sum_q")[:, :QK_HEAD]
    g_w_kv = _final_sum(pairs[2], quads[2], sel, "rs_sum_kv")
    g_w_out = _final_sum(pairs[3], quads[3], sel, "rs_sum_out")
    grad_x, dshift, dscale, dng = _norm_bwd(x2, dh, gx1, norm_g, mod)

    row = jnp.concatenate([dshift, dscale, dgate, dng, dqag, dkvag, dqg, dkg, dconv[:3].reshape(1, 3 * D_CONV), loss_row], axis=1)
    (rows_g,) = _all_gather([row], "gather_small")
    tot = _sum_leading(rows_g, F32, "sum_small")
    dmod_all = rows_g[:, 0, SM_MOD:SM_NG]
    g_ada_b = tot[:, SM_MOD:SM_NG]
    g_norm_g = tot[:, SM_NG:SM_QAG]
    g_q_a_g = tot[:, SM_QAG:SM_KVAG]
    g_kv_a_g = tot[:, SM_KVAG:SM_QG]
    g_q_g = tot[:, SM_QG:SM_QG + QK_HEAD]
    g_k_g = tot[:, SM_KG:SM_KG + QK_HEAD]
    conv_cols = conv_l.shape[1]
    g_conv = lax.dynamic_slice(tot[:, SM_CONV:SM_LOSS].reshape(3, D_CONV), (0, me * conv_cols), (3, conv_cols))
    loss = tot[0, SM_LOSS]
    dmod_my = lax.dynamic_slice(dmod_all, (0, me * ada_cols), (N_DEV, ada_cols))
    g_ada_w = _ada_w_grad(c_all.T, dmod_my)

    grads = dict(ada_w=g_ada_w, ada_b=g_ada_b, norm_g=g_norm_g, w_in=g_w_in, conv_w=g_conv, q_a_g=g_q_a_g, w_q_b=g_w_q,
                 kv_a_g=g_kv_a_g, w_kv_b=g_w_kv, q_g=g_q_g, k_g=g_k_g, w_out=g_w_out)
    weights = dict(ada_w=(ada_w, m_ada_w, v_ada_w), ada_b=(ada_b, m_ada_b, v_ada_b), norm_g=(norm_g, m_norm_g, v_norm_g),
                   w_in=(w_in, m_w_in, v_w_in), conv_w=(conv_w, m_conv_w, v_conv_w), q_a_g=(q_a_g, m_q_a_g, v_q_a_g),
                   w_q_b=(w_q_b, m_w_q_b, v_w_q_b), kv_a_g=(kv_a_g, m_kv_a_g, v_kv_a_g), w_kv_b=(w_kv_b, m_w_kv_b, v_w_kv_b),
                   q_g=(q_g, m_q_g, v_q_g), k_g=(k_g, m_k_g, v_k_g), w_out=(w_out, m_w_out, v_w_out))
    names = list(grads)
    out_g, out_d, out_m, out_v = [], [], [], []
    for n in names:
        w, m, v_ = weights[n]
        shape2 = w.shape[-2:] if w.ndim == 3 else (1, w.shape[-1])
        g2 = grads[n].reshape(shape2)
        d2, m2, v2 = _adamw(w.reshape(shape2), g2, m.reshape(shape2), v_.reshape(shape2), "adamw_" + n)
        out_g.append(g2.reshape(w.shape))
        out_d.append(d2.reshape(w.shape))
        out_m.append(m2.reshape(w.shape))
        out_v.append(v2.reshape(w.shape))
    return (loss, grad_x.reshape(x.shape), *out_g, *out_d, *out_m, *out_v)
```

```python
import functools
import math

import jax
import jax.numpy as jnp
from jax import lax
from jax.experimental import pallas as pl
from jax.experimental.pallas import tpu as pltpu

F32 = jnp.float32
BF16 = jnp.bfloat16
MESH = pl.DeviceIdType.MESH

D_MODEL = 2048
D_CONV = 1024
N_HEADS = 8
QK_NOPE = 128
QK_ROPE = 64
QK_HEAD = QK_NOPE + QK_ROPE
V_HEAD = 128
D_ATTN = N_HEADS * V_HEAD
Q_LORA = 512
KV_LORA = 256
ROPE_BASE = 10000.0
IN_COLS = 4 * D_CONV + Q_LORA + KV_LORA + QK_ROPE + D_ATTN
EPS = 1e-6
ADAM_LR, ADAM_B1, ADAM_B2, ADAM_EPS, ADAM_WD, ADAM_STEP = 0.001, 0.9, 0.999, 1e-08, 0.01, 10

N_DEV = 8
LANES = 128
QK_PAD = 256
U_COLS = 6144
U_CQ, U_CKV, U_KR, U_ZA = 4096, 4608, 4864, 4928
U_TAIL = 2048
ZA_LO = U_ZA - (U_COLS - U_TAIL) - QK_ROPE
ZA_WIN = D_ATTN + LANES
CW = IN_COLS // 8
EXP_W = 896
W_LO = [(CW * d // 128) * 128 for d in range(8)]
W_OFF = [CW * d - lo for d, lo in enumerate(W_LO)]
SCALE = 1.0 / math.sqrt(QK_HEAD)
LOG2E = 1.4426950408889634
LN2 = 0.6931471805599453
NEG = -1e30
VMEM_LIMIT = 56 * 1024 * 1024

TM_ELEM = 256
TM_QKV_BWD = 256
NORM_ROWS = 16
NORM_GROUP = 4
TM_MM = 512
TQ = 1024
Q_CHAINS = 4
KV_SPLIT = 2

SM_MOD, SM_NG, SM_QAG, SM_KVAG, SM_QG, SM_KG, SM_CONV, SM_LOSS = 0, 6144, 8192, 8704, 8960, 9216, 9472, 12544
SM_COLS = 12672


def _params(sem=None):
    kw = dict(vmem_limit_bytes=VMEM_LIMIT)
    if sem is not None:
        kw["dimension_semantics"] = sem
    return pltpu.CompilerParams(**kw)


def _sigmoid(z):
    return 1.0 / (1.0 + jnp.exp(-z))


def _rot64(x):
    lane = lax.broadcasted_iota(jnp.int32, x.shape, 1)
    return jnp.where(lane < 32, pltpu.roll(x, 96, 1), pltpu.roll(x, 32, 1))


def _rope(x, cos, sin):
    return x * cos + _rot64(x) * sin


def _rope_t(d, cos, sin):
    return d * cos - _rot64(d) * sin


def _dot(a, b):
    return jnp.dot(a, b, preferred_element_type=F32)


def _dot_nt(a, b):
    return lax.dot_general(a, b, (((1,), (1,)), ((), ())), preferred_element_type=F32)


def _dot_tn(a, b):
    return lax.dot_general(a, b, (((0,), (0,)), ((), ())), preferred_element_type=F32)


def _my_index():
    return 4 * lax.axis_index("x") + 2 * lax.axis_index("y") + lax.axis_index("c")


ANY = pl.BlockSpec(memory_space=pl.ANY)


class _Gather:
    def __init__(self, blocks, relay=False, parts=1):
        self.relay = relay
        self.parts = parts
        self.rows = [b.shape[0] // parts for b in blocks]
        self.n = n = len(blocks) * parts
        self.out_shape = [jax.ShapeDtypeStruct((N_DEV,) + b.shape, b.dtype) for b in blocks]
        self.scratch = [pltpu.SemaphoreType.DMA((7 * n,)), pltpu.SemaphoreType.DMA((7 * n,)),
                        pltpu.SemaphoreType.DMA((n,))]

    @staticmethod
    def _places():
        x, y, c = lax.axis_index("x"), lax.axis_index("y"), lax.axis_index("c")
        return (x, y, c), (x, y, 1 - c), [(1 - x, y), (x, 1 - y), (1 - x, 1 - y)]

    def _src(self, ins, a):
        block, part = divmod(a, self.parts)
        return ins[block] if self.parts == 1 else ins[block].at[pl.ds(part * self.rows[block], self.rows[block])]

    def _dst(self, outs, a, place):
        block, part = divmod(a, self.parts)
        ref = outs[block].at[4 * place[0] + 2 * place[1] + place[2]]
        return ref if self.parts == 1 else ref.at[pl.ds(part * self.rows[block], self.rows[block])]

    def _copy(self, outs, sems, a, k, block, to, src=None):
        dst = self._dst(outs, a, block)
        return pltpu.make_async_remote_copy(
            src_ref=dst if src is None else src, dst_ref=dst, send_sem=sems[0].at[7 * a + k],
            recv_sem=sems[1].at[7 * a + k], device_id=to, device_id_type=MESH)

    def _first(self, ins, outs, sems):
        me, sibling, chips = self._places()
        first = []
        for a in range(self.n):
            first.append(self._copy(outs, sems, a, 0, me, sibling, src=self._src(ins, a)))
            first += [self._copy(outs, sems, a, 1 + j, me, (*chip, me[2]), src=self._src(ins, a))
                      for j, chip in enumerate(chips[:2] if self.relay else chips)]
        return first

    def _relays(self, outs, sems):
        if not self.relay:
            return []
        (x, y, c), _, _ = self._places()
        via = (jnp.where(c == 0, 1 - x, x), jnp.where(c == 0, y, 1 - y))
        to = (jnp.where(c == 0, x, 1 - x), jnp.where(c == 0, 1 - y, y))
        return [self._copy(outs, sems, a, 3, (*via, c), (*to, c)) for a in range(self.n)]

    def _passed(self, outs, sems):
        me, sibling, chips = self._places()
        return [self._copy(outs, sems, a, 4 + j, (*chip, me[2]), sibling)
                for a in range(self.n) for j, chip in enumerate(chips)]

    def _mine(self, ins, outs, sems):
        me, _, _ = self._places()
        return [pltpu.make_async_copy(self._src(ins, a), self._dst(outs, a, me), sems[2].at[a]) for a in range(self.n)]

    def start(self, ins, outs, sems):
        for cp in self._mine(ins, outs, sems) + self._first(ins, outs, sems):
            cp.start()

    def forward(self, ins, outs, sems):
        del ins
        me, _, chips = self._places()
        passed, relays = self._passed(outs, sems), self._relays(outs, sems)
        for a in range(self.n):
            for j, chip in enumerate(chips[:2] if self.relay else chips):
                self._copy(outs, sems, a, 1 + j, (*chip, me[2]), me).wait_recv()
                passed[3 * a + j].start()
            if self.relay:
                relays[a].start()
        if self.relay:
            for a in range(self.n):
                self._copy(outs, sems, a, 3, (*chips[2], me[2]), me).wait_recv()
                passed[3 * a + 2].start()

    def finish(self, ins, outs, sems):
        me, sibling, chips = self._places()
        for a in range(self.n):
            self._copy(outs, sems, a, 0, sibling, me).wait_recv()
            for j, chip in enumerate(chips):
                self._copy(outs, sems, a, 4 + j, (*chip, 1 - me[2]), me).wait_recv()
        for cp in self._first(ins, outs, sems) + self._relays(outs, sems) + self._passed(outs, sems):
            cp.wait_send()
        for cp in self._mine(ins, outs, sems):
            cp.wait()


class _ChipExchange:
    def __init__(self, arrays):
        self.n = n = len(arrays)
        self.out_shape = [jax.ShapeDtypeStruct(a.shape, a.dtype) for a in arrays]
        self.scratch = [pltpu.SemaphoreType.DMA((3 * n,)), pltpu.SemaphoreType.DMA((3 * n,))]

    def _copies(self, ins, outs, sems):
        x, y, c = lax.axis_index("x"), lax.axis_index("y"), lax.axis_index("c")
        return [pltpu.make_async_remote_copy(
            src_ref=ins[a].at[2 * px + py], dst_ref=outs[a].at[2 * x + y], send_sem=sems[0].at[3 * a + j],
            recv_sem=sems[1].at[3 * a + j], device_id=(px, py, c), device_id_type=MESH)
            for a in range(self.n) for j, (px, py) in enumerate([(1 - x, y), (x, 1 - y), (1 - x, 1 - y)])]

    def start(self, ins, outs, sems):
        for cp in self._copies(ins, outs, sems):
            cp.start()

    def forward(self, ins, outs, sems):
        pass

    def finish(self, ins, outs, sems):
        for cp in self._copies(ins, outs, sems):
            cp.wait()


def _all_gather(blocks, name):
    n = len(blocks)
    g = _Gather(blocks)

    def body(*refs):
        ins, outs, sems = refs[:n], refs[n:2 * n], refs[2 * n:]
        g.start(ins, outs, sems)
        g.forward(ins, outs, sems)
        g.finish(ins, outs, sems)

    return pl.pallas_call(body, name=name, out_shape=g.out_shape, in_specs=[ANY] * n, out_specs=[ANY] * n,
                          scratch_shapes=g.scratch)(*blocks)


class _SiblingExchange:
    def __init__(self, arrays, windowed):
        self.n = n = len(arrays)
        self.windowed = windowed
        self.out_shape = [jax.ShapeDtypeStruct((4, a.shape[0], EXP_W) if w else (4,) + a.shape[1:], a.dtype)
                          for a, w in zip(arrays, windowed)]
        self.scratch = [pltpu.SemaphoreType.DMA((4 * n,)), pltpu.SemaphoreType.DMA((4 * n,))]

    def _each(self, ins, outs, sems, act):
        x, y, c = lax.axis_index("x"), lax.axis_index("y"), lax.axis_index("c")

        def branch(c_val):
            for k in range(4):
                e = 2 * k + (1 - c_val)
                for a in range(self.n):
                    src = ins[a].at[:, pl.ds(W_LO[e], EXP_W)] if self.windowed[a] else ins[a].at[e]
                    act(pltpu.make_async_remote_copy(
                        src_ref=src, dst_ref=outs[a].at[k], send_sem=sems[0].at[4 * a + k], recv_sem=sems[1].at[4 * a + k],
                        device_id=(x, y, 1 - c), device_id_type=MESH))

        for c_val in (0, 1):
            pl.when(c == c_val)(functools.partial(branch, c_val))

    def start(self, ins, outs, sems):
        self._each(ins, outs, sems, lambda cp: cp.start())

    def forward(self, ins, outs, sems):
        pass

    def finish(self, ins, outs, sems):
        self._each(ins, outs, sems, lambda cp: cp.wait())


def _exchange(rider, arrays, name):
    n = len(arrays)

    def body(*refs):
        ins, outs, sems = refs[:n], refs[n:n + len(rider.out_shape)], refs[n + len(rider.out_shape):]
        rider.start(ins, outs, sems)
        rider.forward(ins, outs, sems)
        rider.finish(ins, outs, sems)

    return pl.pallas_call(body, name=name, out_shape=rider.out_shape, in_specs=[ANY] * n,
                          out_specs=[ANY] * len(rider.out_shape), scratch_shapes=rider.scratch)(*arrays)


def _add_window(dw_in, recv, lo_tiles):
    k, rows, _ = recv.shape

    def body(t_ref, w_ref, r_ref, o_ref):
        del t_ref
        o_ref[0] = (w_ref[...].astype(F32) + r_ref[0].astype(F32)).astype(o_ref.dtype)

    spec = pl.BlockSpec((1, rows, LANES), lambda i, j, t: (i, 0, j))
    grid_spec = pltpu.PrefetchScalarGridSpec(
        num_scalar_prefetch=1, grid=(k, EXP_W // LANES),
        in_specs=[pl.BlockSpec((rows, LANES), lambda i, j, t: (0, t[i] + j)), spec], out_specs=spec)
    return pl.pallas_call(
        body, name="rs_add_in", grid_spec=grid_spec, out_shape=jax.ShapeDtypeStruct(recv.shape, recv.dtype),
        compiler_params=_params(("parallel", "parallel")),
    )(lo_tiles, dw_in, recv)


def _final_sum(p, r, sel, name, unshift_to=None):
    _, rows, cols = p.shape
    tr = 512 if rows % 512 == 0 else rows
    out_cols = cols if unshift_to is None else unshift_to

    def body(sel_ref, p_ref, r0, r1, r2, r3, o_ref):
        own = p_ref[0].astype(F32)
        acc = None
        for k, r_ref in enumerate((r0, r1, r2, r3)):
            term = jnp.where(sel_ref[0] == k, own, r_ref[0].astype(F32))
            acc = term if acc is None else acc + term
        if unshift_to is not None:
            acc = pltpu.roll(acc, sel_ref[5], 1)[:, :unshift_to]
        o_ref[...] = acc

    def slot(k):
        return pl.BlockSpec((1, tr, cols), lambda i, t: (t[k], i, 0))

    grid_spec = pltpu.PrefetchScalarGridSpec(
        num_scalar_prefetch=1, grid=(rows // tr,), in_specs=[slot(0), slot(1), slot(2), slot(3), slot(4)],
        out_specs=pl.BlockSpec((tr, out_cols), lambda i, t: (i, 0)))
    return pl.pallas_call(
        body, name=name, grid_spec=grid_spec, out_shape=jax.ShapeDtypeStruct((rows, out_cols), F32),
        compiler_params=_params(("parallel",)),
    )(sel, p, r, r, r, r)


def _expand_w_in(w, shift):
    rows, cw = w.shape
    tr = 256

    def body(s_ref, w_ref, o_ref, buf):
        buf[...] = jnp.zeros_like(buf)
        buf[:, 0:cw] = w_ref[...]
        o_ref[...] = pltpu.roll(buf[...], s_ref[0], 1).astype(BF16)

    grid_spec = pltpu.PrefetchScalarGridSpec(
        num_scalar_prefetch=1, grid=(rows // tr,), in_specs=[pl.BlockSpec((tr, cw), lambda i, t: (i, 0))],
        out_specs=pl.BlockSpec((tr, EXP_W), lambda i, t: (i, 0)), scratch_shapes=[pltpu.VMEM((tr, EXP_W), F32)])
    return pl.pallas_call(
        body, name="expand_w_in", grid_spec=grid_spec, out_shape=jax.ShapeDtypeStruct((rows, EXP_W), BF16),
        compiler_params=_params(("arbitrary",)),
    )(shift, w)


def _pad_wq(w):
    rows, cw = w.shape

    def body(w_ref, o_ref, buf):
        buf[...] = jnp.zeros_like(buf)
        buf[:, 0:cw] = w_ref[...]
        o_ref[...] = buf[...].astype(BF16)

    return pl.pallas_call(
        body, name="pad_wq", out_shape=jax.ShapeDtypeStruct((rows, QK_PAD), BF16),
        scratch_shapes=[pltpu.VMEM((rows, QK_PAD), F32)], compiler_params=_params(),
    )(w)


def _merge_w_in(e):
    _, rows, _ = e.shape
    tr = 256

    def body(e_ref, o_ref):
        for t in range(U_COLS // LANES):
            lo, hi = t * LANES, (t + 1) * LANES
            parts = [e_ref[d, :, lo - W_LO[d]:hi - W_LO[d]] for d in range(N_DEV)
                     if CW * d < hi and CW * (d + 1) > lo]
            if not parts:
                tile = jnp.zeros((tr, LANES), BF16)
            elif len(parts) == 1:
                tile = parts[0]
            else:
                tile = (parts[0].astype(F32) + parts[1].astype(F32)).astype(BF16)
            o_ref[:, lo:hi] = tile

    return pl.pallas_call(
        body, name="merge_w_in", grid=(rows // tr,),
        in_specs=[pl.BlockSpec((N_DEV, tr, EXP_W), lambda i: (0, i, 0))],
        out_specs=pl.BlockSpec((tr, U_COLS), lambda i: (i, 0)), out_shape=jax.ShapeDtypeStruct((rows, U_COLS), BF16),
        compiler_params=_params(("parallel",)),
    )(e)


def _sum_leading(a, out_dtype, name):
    k, rows, cols = a.shape
    tr = min(rows, 1728 if rows % 1728 == 0 else rows)

    def body(a_ref, o_ref):
        acc = a_ref[0].astype(F32)
        for i in range(1, k):
            acc = acc + a_ref[i].astype(F32)
        o_ref[...] = acc.astype(out_dtype)

    return pl.pallas_call(
        body, name=name, grid=(rows // tr,),
        in_specs=[pl.BlockSpec((k, tr, cols), lambda i: (0, i, 0))],
        out_specs=pl.BlockSpec((tr, cols), lambda i: (i, 0)),
        out_shape=jax.ShapeDtypeStruct((rows, cols), out_dtype), compiler_params=_params(("parallel",)),
    )(a)


def _add_pairs(g, recv, core, name):
    k, rows, cols = recv.shape
    tr = 1728 if rows % 1728 == 0 else rows

    def body(c_ref, g_ref, r_ref, o_ref):
        del c_ref
        o_ref[...] = (g_ref[...].astype(F32) + r_ref[...].astype(F32)).astype(o_ref.dtype)

    spec = pl.BlockSpec((1, tr, cols), lambda i, j, c: (i, j, 0))
    grid_spec = pltpu.PrefetchScalarGridSpec(
        num_scalar_prefetch=1, grid=(k, rows // tr),
        in_specs=[pl.BlockSpec((1, tr, cols), lambda i, j, c: (2 * i + c[0], j, 0)), spec], out_specs=spec)
    return pl.pallas_call(
        body, name=name, grid_spec=grid_spec, out_shape=jax.ShapeDtypeStruct(recv.shape, recv.dtype),
        compiler_params=_params(("parallel", "parallel")),
    )(core, g, recv)


def _ada_mod(c16, ada_w_l, ada_b_l):
    def body(c_ref, w_ref, b_ref, o_ref):
        cv = c_ref[...]
        sc = (cv * _sigmoid(cv)).astype(BF16)
        o_ref[...] = _dot(sc, w_ref[...].astype(BF16)) + b_ref[...]

    return pl.pallas_call(
        body, name="ada_mod", out_shape=jax.ShapeDtypeStruct((c16.shape[0], ada_w_l.shape[1]), F32),
        compiler_params=_params(),
    )(c16, ada_w_l, ada_b_l)


def _ada_w_grad(c_t, dmod_my):
    def body(c_ref, d_ref, o_ref):
        cv = c_ref[...]
        sc = cv * _sigmoid(cv)
        acc = sc[:, 0:1] * d_ref[0:1, :]
        for b in range(1, N_DEV):
            acc = acc + sc[:, b:b + 1] * d_ref[b:b + 1, :]
        o_ref[...] = acc

    return pl.pallas_call(
        body, name="ada_w_grad", out_shape=jax.ShapeDtypeStruct((c_t.shape[0], dmod_my.shape[1]), F32),
        compiler_params=_params(),
    )(c_t, dmod_my)


def _norm_mod(x, norm_g, mod, pos_col, invf, sign, rider, rider_inputs):
    s, d = x.shape
    tm = min(TM_MM, s)
    n_in, n_out = len(rider_inputs), len(rider.out_shape)
    steps = s // tm

    def body(x_ref, g_ref, mod_ref, p_ref, f_ref, s_ref, *rest):
        r_ins, (h_ref, ht_ref, cos_ref, sin_ref) = rest[:n_in], rest[n_in:n_in + 4]
        r_outs, sems = rest[n_in + 4:n_in + 4 + n_out], rest[n_in + 4 + n_out:]
        pl.when(pl.program_id(0) == 0)(functools.partial(rider.start, r_ins, r_outs, sems))
        xv = x_ref[...]
        r = lax.rsqrt(jnp.mean(xv * xv, axis=-1, keepdims=True) + EPS)
        hn = xv * r * g_ref[...]
        hv = hn * (1.0 + mod_ref[:, d:2 * d]) + mod_ref[:, 0:d]
        h_ref[...] = hv.astype(BF16)
        ht_ref[...] = hv.T.astype(BF16)
        ang = p_ref[...].astype(F32) * f_ref[...]
        sg = s_ref[...]
        cos_ref[...] = jnp.cos(ang) * jnp.abs(sg)
        sin_ref[...] = jnp.sin(ang) * sg

        @pl.when(pl.program_id(0) == steps - 1)
        def _():
            rider.forward(r_ins, r_outs, sems)
            rider.finish(r_ins, r_outs, sems)

    row = pl.BlockSpec((1, LANES), lambda i: (0, 0))
    tab = pl.BlockSpec((tm, LANES), lambda i: (i, 0))
    return pl.pallas_call(
        body, name="norm_mod", grid=(steps,),
        in_specs=[pl.BlockSpec((tm, d), lambda i: (i, 0)), pl.BlockSpec((1, d), lambda i: (0, 0)),
                  pl.BlockSpec((1, 3 * d), lambda i: (0, 0)), pl.BlockSpec((tm, 1), lambda i: (i, 0)), row, row]
        + [ANY] * n_in,
        out_specs=[pl.BlockSpec((tm, d), lambda i: (i, 0)), pl.BlockSpec((d, tm), lambda i: (0, i)), tab, tab] + [ANY] * n_out,
        out_shape=[jax.ShapeDtypeStruct((s, d), BF16), jax.ShapeDtypeStruct((d, s), BF16),
                   jax.ShapeDtypeStruct((s, LANES), F32), jax.ShapeDtypeStruct((s, LANES), F32)] + rider.out_shape,
        scratch_shapes=rider.scratch, compiler_params=_params(("arbitrary",)),
    )(x, norm_g, mod, pos_col, invf, sign, *rider_inputs)


def _matmul(a, b, *, nt, out_dtype, tm, tn, name, rider=None, rider_inputs=(), a_resident=False):
    m, kdim = a.shape
    n = b.shape[0] if nt else b.shape[1]
    tm, tn = min(tm, m), min(tn, n)
    n_in = len(rider_inputs)
    n_out = len(rider.out_shape) if rider else 0
    m_steps, n_steps = m // tm, n // tn
    steps = n_steps * m_steps
    inner = n_steps if a_resident else m_steps
    tile = (lambda o, i: (o, i)) if a_resident else (lambda o, i: (i, o))

    def body(a_ref, b_ref, *rest):
        r_ins, o_ref, r_outs, sems = rest[:n_in], rest[n_in], rest[n_in + 1:n_in + 1 + n_out], rest[n_in + 1 + n_out:]
        step = pl.program_id(0) * inner + pl.program_id(1)
        if rider:
            pl.when(step == 0)(functools.partial(rider.start, r_ins, r_outs, sems))
            pl.when(step == steps // 2)(functools.partial(rider.forward, r_ins, r_outs, sems))
        o = _dot_nt(a_ref[...], b_ref[...]) if nt else _dot(a_ref[...], b_ref[...])
        o_ref[...] = o.astype(out_dtype)
        if rider:
            pl.when(step == steps - 1)(functools.partial(rider.finish, r_ins, r_outs, sems))

    if nt:
        b_spec = pl.BlockSpec((tn, kdim), lambda o, i: (tile(o, i)[1], 0))
    else:
        b_spec = pl.BlockSpec((kdim, tn), lambda o, i: (0, tile(o, i)[1]))
    out = pl.pallas_call(
        body, name=name, grid=(m_steps, n_steps) if a_resident else (n_steps, m_steps),
        in_specs=[pl.BlockSpec((tm, kdim), lambda o, i: (tile(o, i)[0], 0)), b_spec] + [ANY] * n_in,
        out_specs=[pl.BlockSpec((tm, tn), tile)] + [ANY] * n_out,
        out_shape=[jax.ShapeDtypeStruct((m, n), out_dtype)] + (rider.out_shape if rider else []),
        scratch_shapes=rider.scratch if rider else [],
        compiler_params=_params(("arbitrary", "arbitrary") if rider else ("parallel", "parallel")),
    )(a, b, *rider_inputs)
    return out if rider else out[0]


HALO = 16


def _conv_specs(tm):
    def col(j):
        return pl.BlockSpec((tm, D_CONV), lambda i: (i, j))

    def prev(j):
        return pl.BlockSpec((HALO, D_CONV), lambda i: (jnp.maximum(i * (tm // HALO) - 1, 0), j))

    return [col(0), col(1), col(2), col(3), prev(0), prev(2)]


def _conv_y(xc_ref, bc_ref, cc_ref, zc_ref, xp_ref, cp_ref, w_ref, first):
    uc = cc_ref[...].astype(F32) * xc_ref[...].astype(F32)
    up = jnp.where(first, 0.0, cp_ref[...].astype(F32) * xp_ref[...].astype(F32))
    full = jnp.concatenate([up, uc], axis=0)
    u1 = pltpu.roll(full, 1, 0)[HALO:]
    u2 = pltpu.roll(full, 2, 0)[HALO:]
    w = w_ref[...]
    conv = w[0:1] * u2 + w[1:2] * u1 + w[2:3] * uc
    z = zc_ref[...].astype(F32)
    return bc_ref[...].astype(F32) * conv * (z * _sigmoid(z))


def _conv_bwd(u, dyc, conv_w):
    s = u.shape[0]
    tm = min(TM_ELEM, s)
    cb = D_CONV
    nt = s // tm

    def body(xc_ref, bc_ref, cc_ref, zc_ref, xp_ref, cp_ref, bn_ref, zn_ref, dy_ref, dyn_ref, w_ref, du_ref, dw_ref):
        i = pl.program_id(0)
        xc, cc = xc_ref[...].astype(F32), cc_ref[...].astype(F32)
        bc, z = bc_ref[...].astype(F32), zc_ref[...].astype(F32)
        uc = cc * xc
        up = jnp.where(i == 0, 0.0, cp_ref[...].astype(F32) * xp_ref[...].astype(F32))
        full = jnp.concatenate([up, uc], axis=0)
        u1 = pltpu.roll(full, 1, 0)[HALO:]
        u2 = pltpu.roll(full, 2, 0)[HALO:]
        w = w_ref[...]
        conv = w[0:1] * u2 + w[1:2] * u1 + w[2:3] * uc
        sg = _sigmoid(z)
        sz = z * sg
        dy = dy_ref[...].astype(F32)
        dconv = dy * bc * sz
        zn = zn_ref[...].astype(F32)
        dnext = dyn_ref[...].astype(F32) * bn_ref[...].astype(F32) * (zn * _sigmoid(zn))
        dnext = jnp.where(i == nt - 1, 0.0, dnext)
        fullb = jnp.concatenate([dconv, dnext], axis=0)
        nb = tm + HALO
        d1 = pltpu.roll(fullb, nb - 1, 0)[:tm]
        d2 = pltpu.roll(fullb, nb - 2, 0)[:tm]
        duc = w[2:3] * dconv + w[1:2] * d1 + w[0:1] * d2
        dzc = dy * bc * conv * (sg * (1.0 + z * (1.0 - sg)))
        du_ref[...] = jnp.concatenate([duc * cc, dy * conv * sz, duc * xc, dzc], axis=1).astype(BF16)
        dw = jnp.concatenate([jnp.sum(dconv * u2, axis=0, keepdims=True), jnp.sum(dconv * u1, axis=0, keepdims=True),
                              jnp.sum(dconv * uc, axis=0, keepdims=True), jnp.zeros((5, cb), F32)], axis=0)

        @pl.when(i == 0)
        def _():
            dw_ref[...] = dw

        @pl.when(i > 0)
        def _():
            dw_ref[...] += dw

    def col(j):
        return pl.BlockSpec((tm, cb), lambda i: (i, j))

    def prev(j):
        return pl.BlockSpec((HALO, cb), lambda i: (jnp.maximum(i * (tm // HALO) - 1, 0), j))

    def nxt(j):
        return pl.BlockSpec((HALO, cb), lambda i: (jnp.minimum((i + 1) * (tm // HALO), s // HALO - 1), j))

    return pl.pallas_call(
        body, name="conv_bwd", grid=(nt,),
        in_specs=[col(0), col(1), col(2), col(3), prev(0), prev(2), nxt(1), nxt(3), col(0), nxt(0),
                  pl.BlockSpec((3, cb), lambda i: (0, 0))],
        out_specs=[pl.BlockSpec((tm, 4 * cb), lambda i: (i, 0)), pl.BlockSpec((8, cb), lambda i: (0, 0))],
        out_shape=[jax.ShapeDtypeStruct((s, U_COLS), BF16), jax.ShapeDtypeStruct((8, cb), F32)],
        compiler_params=_params(("arbitrary",)),
    )(u, u, u, u, u, u, u, u, dyc, dyc, conv_w)


def _qkv_specs(tm):
    return [pl.BlockSpec((tm, Q_LORA), lambda i: (i, U_CQ // Q_LORA)),
            pl.BlockSpec((tm, KV_LORA), lambda i: (i, U_CKV // KV_LORA)),
            pl.BlockSpec((tm, LANES), lambda i: (i, U_KR // LANES)),
            pl.BlockSpec((tm, LANES), lambda i: (i, 0)), pl.BlockSpec((tm, LANES), lambda i: (i, 0))]


def _full(shape):
    return pl.BlockSpec(shape, lambda i: (0,) * len(shape))


def _k_rope_lanes(blk):
    lane = lax.broadcasted_iota(jnp.int32, blk.shape, 1)
    return jnp.where(lane < QK_ROPE, blk, 0.0)


def _qkv_fwd(u, cos, sin, wq, wkv, qag, kvag, qg, kg):
    s = u.shape[0]
    tm = min(TM_ELEM, s)

    def body(cq_ref, ckv_ref, kr_ref, cos_ref, sin_ref, wq_ref, wkv_ref, qag_ref, kvag_ref, qg_ref, kg_ref,
             q_ref, k_ref, v_ref):
        cq = cq_ref[...].astype(F32)
        cqn = (cq * lax.rsqrt(jnp.mean(cq * cq, axis=-1, keepdims=True) + EPS) * qag_ref[...]).astype(BF16)
        ckv = ckv_ref[...].astype(F32)
        ckvn = (ckv * lax.rsqrt(jnp.mean(ckv * ckv, axis=-1, keepdims=True) + EPS) * kvag_ref[...]).astype(BF16)
        kr = _k_rope_lanes(kr_ref[...].astype(F32))
        cosv, sinv, qgv, kgv = cos_ref[...], sin_ref[...], qg_ref[...], kg_ref[...]
        ss_r = jnp.sum(kr * kr, axis=-1, keepdims=True)
        krr = _rope(kr * kgv[:, QK_NOPE:], cosv, sinv)
        qf = _dot(cqn, wq_ref[...])
        kvf = _dot(ckvn, wkv_ref[...])
        heads = range(N_HEADS)
        qh = [qf[:, QK_PAD * h:QK_PAD * (h + 1)] for h in heads]
        kn = [kvf[:, 2 * V_HEAD * h:2 * V_HEAD * h + QK_NOPE] for h in heads]
        rq = [lax.rsqrt(jnp.sum(qh[h] * qh[h], axis=-1, keepdims=True) * (1.0 / QK_HEAD) + EPS) for h in heads]
        rk = [lax.rsqrt((jnp.sum(kn[h] * kn[h], axis=-1, keepdims=True) + ss_r) * (1.0 / QK_HEAD) + EPS) for h in heads]
        for h in heads:
            qn = qh[h] * rq[h] * qgv
            qo = jnp.concatenate([qn[:, :QK_NOPE], _rope(qn[:, QK_NOPE:], cosv, sinv)], axis=1) * (SCALE * LOG2E)
            q_ref[h] = qo.astype(BF16)
            vh = kvf[:, 2 * V_HEAD * h + QK_NOPE:2 * V_HEAD * (h + 1)]
            k_ref[h] = jnp.concatenate([kn[h] * kgv[:, :QK_NOPE] * rk[h], krr * rk[h]], axis=1).astype(BF16)
            v_ref[h] = jnp.concatenate([vh, jnp.ones_like(vh)], axis=1).astype(BF16)

    return pl.pallas_call(
        body, name="qkv_fwd", grid=(s // tm,),
        in_specs=_qkv_specs(tm) + [_full((Q_LORA, N_HEADS * QK_PAD)), _full((KV_LORA, 2 * D_ATTN)),
                                   _full((1, Q_LORA)), _full((1, KV_LORA)), _full((1, QK_PAD)), _full((1, QK_PAD))],
        out_specs=[pl.BlockSpec((N_HEADS, tm, QK_PAD), lambda i: (0, i, 0)),
                   pl.BlockSpec((N_HEADS, tm, QK_PAD), lambda i: (0, i, 0)),
                   pl.BlockSpec((N_HEADS, tm, 2 * V_HEAD), lambda i: (0, i, 0))],
        out_shape=[jax.ShapeDtypeStruct((N_HEADS, s, QK_PAD), BF16), jax.ShapeDtypeStruct((N_HEADS, s, QK_PAD), BF16),
                   jax.ShapeDtypeStruct((N_HEADS, s, 2 * V_HEAD), BF16)],
        compiler_params=_params(("parallel",)),
    )(u, u, u, cos, sin, wq, wkv, qag, kvag, qg, kg)


def _qkv_bwd(u, cos, sin, dq, dk, dv, dza, wq, wkv, qag, kvag, qg, kg, du):
    s = u.shape[0]
    tm = min(TM_QKV_BWD, s)
    nt = s // tm

    def body(cq_ref, ckv_ref, kr_ref, cos_ref, sin_ref, dq_ref, dk_ref, dv_ref, dza_ref, wq_ref, wkv_ref, qag_ref,
             kvag_ref, qg_ref, kg_ref, du_in, du_ref, dwq_ref, dwkv_ref, dqag_ref, dkvag_ref, dqg_ref, dkg_ref,
             dwq_acc, dwkv_acc):
        del du_in
        i = pl.program_id(0)

        @pl.when(i == 0)
        def _():
            dwq_acc[...] = jnp.zeros_like(dwq_acc)
            dwkv_acc[...] = jnp.zeros_like(dwkv_acc)

        cq = cq_ref[...].astype(F32)
        rqa = lax.rsqrt(jnp.mean(cq * cq, axis=-1, keepdims=True) + EPS)
        xq = cq * rqa
        qagv = qag_ref[...]
        cqn = (xq * qagv).astype(BF16)
        ckv = ckv_ref[...].astype(F32)
        rkva = lax.rsqrt(jnp.mean(ckv * ckv, axis=-1, keepdims=True) + EPS)
        xkv = ckv * rkva
        kvagv = kvag_ref[...]
        ckvn = (xkv * kvagv).astype(BF16)
        kr = _k_rope_lanes(kr_ref[...].astype(F32))
        cosv, sinv, qgv, kgv = cos_ref[...], sin_ref[...], qg_ref[...], kg_ref[...]
        ss_r = jnp.sum(kr * kr, axis=-1, keepdims=True)
        dqg = jnp.zeros((1, QK_PAD), F32)
        dkg = jnp.zeros((1, QK_PAD), F32)
        dkr = jnp.zeros((tm, LANES), F32)
        qf = _dot(cqn, wq_ref[...])
        kvf = _dot(ckvn, wkv_ref[...])
        heads = range(N_HEADS)
        qh = [qf[:, QK_PAD * h:QK_PAD * (h + 1)] for h in heads]
        kn = [kvf[:, 2 * V_HEAD * h:2 * V_HEAD * h + QK_NOPE] for h in heads]
        rq = [lax.rsqrt(jnp.sum(qh[h] * qh[h], axis=-1, keepdims=True) * (1.0 / QK_HEAD) + EPS) for h in heads]
        rk = [lax.rsqrt((jnp.sum(kn[h] * kn[h], axis=-1, keepdims=True) + ss_r) * (1.0 / QK_HEAD) + EPS) for h in heads]
        xh = [qh[h] * rq[h] for h in heads]
        xk = [jnp.concatenate([kn[h], kr], axis=1) * rk[h] for h in heads]
        dyq, dyk = [], []
        for h in heads:
            g = dq_ref[h].astype(F32) * SCALE
            dyq.append(jnp.concatenate([g[:, :QK_NOPE], _rope_t(g[:, QK_NOPE:], cosv, sinv)], axis=1))
            gk = dk_ref[h].astype(F32)
            dyk.append(jnp.concatenate([gk[:, :QK_NOPE], _rope_t(gk[:, QK_NOPE:], cosv, sinv)], axis=1))
        for h in heads:
            dqg = dqg + jnp.sum(dyq[h] * xh[h], axis=0, keepdims=True)
            dkg = dkg + jnp.sum(dyk[h] * xk[h], axis=0, keepdims=True)
        gdy = [dyq[h] * qgv for h in heads]
        gdyk = [dyk[h] * kgv for h in heads]
        tq_ = [jnp.sum(gdy[h] * xh[h], axis=-1, keepdims=True) * (1.0 / QK_HEAD) for h in heads]
        tk_ = [jnp.sum(gdyk[h] * xk[h], axis=-1, keepdims=True) * (1.0 / QK_HEAD) for h in heads]
        dqf = [(rq[h] * (gdy[h] - xh[h] * tq_[h])).astype(BF16) for h in heads]
        dkvf = []
        for h in heads:
            dxk = rk[h] * (gdyk[h] - xk[h] * tk_[h])
            dkr = dkr + dxk[:, QK_NOPE:]
            dkvf += [dxk[:, :QK_NOPE].astype(BF16), dv_ref[h]]
        dqf_b, dkvf_b = jnp.concatenate(dqf, axis=1), jnp.concatenate(dkvf, axis=1)
        dwq_acc[...] += _dot_tn(cqn, dqf_b)
        dwkv_acc[...] += _dot_tn(ckvn, dkvf_b)
        dcqn = _dot_nt(dqf_b, wq_ref[...])
        dckvn = _dot_nt(dkvf_b, wkv_ref[...])
        dqag = jnp.sum(dcqn * xq, axis=0, keepdims=True)
        dkvag = jnp.sum(dckvn * xkv, axis=0, keepdims=True)
        gq = dcqn * qagv
        dcq = rqa * (gq - xq * jnp.mean(gq * xq, axis=-1, keepdims=True))
        gkv = dckvn * kvagv
        dckv = rkva * (gkv - xkv * jnp.mean(gkv * xkv, axis=-1, keepdims=True))
        win = pltpu.roll(jnp.concatenate([dza_ref[...].astype(F32), jnp.zeros((tm, LANES), F32)], axis=1), QK_ROPE, 1)
        win = win + jnp.concatenate([dkr, jnp.zeros((tm, D_ATTN), F32)], axis=1)
        du_ref[...] = jnp.concatenate([dcq, dckv, win, jnp.zeros((tm, U_TAIL - ZA_LO - ZA_WIN), F32)], axis=1).astype(BF16)

        @pl.when(i == 0)
        def _():
            dqag_ref[...] = dqag
            dkvag_ref[...] = dkvag
            dqg_ref[...] = dqg
            dkg_ref[...] = dkg

        @pl.when(i > 0)
        def _():
            dqag_ref[...] += dqag
            dkvag_ref[...] += dkvag
            dqg_ref[...] += dqg
            dkg_ref[...] += dkg

        @pl.when(i == nt - 1)
        def _():
            dwq_ref[...] = dwq_acc[...].astype(BF16)
            dwkv_ref[...] = dwkv_acc[...].astype(BF16)

    head = lambda w: pl.BlockSpec((N_HEADS, tm, w), lambda i: (0, i, 0))
    wq_shape, wkv_shape = (Q_LORA, N_HEADS * QK_PAD), (KV_LORA, 2 * D_ATTN)
    return pl.pallas_call(
        body, name="qkv_bwd", grid=(nt,),
        in_specs=_qkv_specs(tm) + [head(QK_PAD), head(QK_PAD), head(V_HEAD), pl.BlockSpec((tm, D_ATTN), lambda i: (i, 0)),
                                   _full(wq_shape), _full(wkv_shape), _full((1, Q_LORA)), _full((1, KV_LORA)),
                                   _full((1, QK_PAD)), _full((1, QK_PAD)), ANY],
        out_specs=[pl.BlockSpec((tm, U_TAIL), lambda i: (i, U_COLS // U_TAIL - 1)), _full(wq_shape), _full(wkv_shape),
                   _full((1, Q_LORA)), _full((1, KV_LORA)), _full((1, QK_PAD)), _full((1, QK_PAD))],
        out_shape=[jax.ShapeDtypeStruct(du.shape, du.dtype), jax.ShapeDtypeStruct(wq_shape, BF16),
                   jax.ShapeDtypeStruct(wkv_shape, BF16), jax.ShapeDtypeStruct((1, Q_LORA), F32),
                   jax.ShapeDtypeStruct((1, KV_LORA), F32), jax.ShapeDtypeStruct((1, QK_PAD), F32),
                   jax.ShapeDtypeStruct((1, QK_PAD), F32)],
        scratch_shapes=[pltpu.VMEM(wq_shape, F32), pltpu.VMEM(wkv_shape, F32)],
        input_output_aliases={15: 0}, compiler_params=_params(("arbitrary",)),
    )(u, u, u, cos, sin, dq, dk, dv, dza, wq, wkv, qag, kvag, qg, kg, du)


def _flash_fwd(q, k, v):
    nh, s, _ = q.shape
    tq = min(TQ, s)
    nkv = KV_SPLIT
    tk = tq // nkv
    nq = s // tq
    nch = Q_CHAINS
    tc = tq // nch

    def body(q_ref, k_ref, v_ref, o_ref, lse_ref):
        i = pl.program_id(1)
        chains = [q_ref[0, r * tc:(r + 1) * tc, :] for r in range(nch)]

        def unit(r, j, carry, shift=None):
            m, acc = carry
            rows = pl.ds(pl.multiple_of(j * tk, tk), tk)
            sc = _dot_nt(chains[r], k_ref[0, rows, :])
            if shift is not None:
                qi = lax.broadcasted_iota(jnp.int32, sc.shape, 0)
                ki = lax.broadcasted_iota(jnp.int32, sc.shape, 1) + shift
                sc = jnp.where(ki <= qi, sc, NEG)
            m_new = jnp.maximum(m, jnp.max(sc, axis=-1, keepdims=True))
            p = jnp.exp2(sc - m_new).astype(BF16)
            return m_new, jnp.exp2(m - m_new) * acc + _dot(p, v_ref[0, rows, :])

        def trip(p, carry):
            for b in range(nkv):
                carry = tuple(unit(r, nkv * p + b, cr) for r, cr in enumerate(carry))
            return carry

        init = (jnp.full((tc, 1), NEG, F32), jnp.zeros((tc, 2 * V_HEAD), F32))
        carry = list(lax.fori_loop(0, i, trip, (init,) * nch))
        for b in range(nkv):
            for r in range(nch):
                shift = b * tk - r * tc
                if shift < tc:
                    carry[r] = unit(r, nkv * i + b, carry[r], None if shift + tk - 1 <= 0 else shift)
        for r, (m, acc) in enumerate(carry):
            l = acc[:, V_HEAD:]
            o_ref[r * tc:(r + 1) * tc, :] = (acc[:, :V_HEAD] / l).astype(BF16)
            lse = m + jnp.log(l[:, 0:1]) * LOG2E
            lse_ref[0, :, r * tc:(r + 1) * tc] = jnp.broadcast_to(lse, (tc, LANES)).T[0:1, :]

    return pl.pallas_call(
        body, name="flash_fwd", grid=(nh, nq),
        in_specs=[pl.BlockSpec((1, tq, QK_PAD), lambda h, i: (h, i, 0)),
                  pl.BlockSpec((1, s, QK_PAD), lambda h, i: (h, 0, 0)),
                  pl.BlockSpec((1, s, 2 * V_HEAD), lambda h, i: (h, 0, 0))],
        out_specs=[pl.BlockSpec((tq, V_HEAD), lambda h, i: (i, h)), pl.BlockSpec((1, 1, tq), lambda h, i: (h, 0, i))],
        out_shape=[jax.ShapeDtypeStruct((s, nh * V_HEAD), BF16), jax.ShapeDtypeStruct((nh, 1, s), F32)],
        compiler_params=_params(("parallel", "arbitrary")),
    )(q, k, v)


def _flash_bwd(q, k, v, do, lse, delta):
    nh, s, _ = q.shape
    tq = min(TQ, s)
    nq = s // tq

    def body(q_ref, k_ref, v_ref, do_ref, lse_ref, dl_ref, dq_ref, dk_ref, dv_ref, dq_acc):
        j = pl.program_id(1)

        @pl.when(j == 0)
        def _():
            dq_acc[...] = jnp.zeros_like(dq_acc)

        kj, vj = k_ref[0], v_ref[0]

        def block(kk, vv, qq, dd, lse, dl, masked):
            st = _dot_nt(kk, qq)
            pt = jnp.exp2(st - lse)
            if masked:
                ki = lax.broadcasted_iota(jnp.int32, st.shape, 0)
                qx = lax.broadcasted_iota(jnp.int32, st.shape, 1)
                pt = jnp.where(ki <= qx, pt, 0.0)
            ddv = _dot(pt.astype(BF16), dd)
            dst = (pt * (_dot_nt(vv, dd) - dl)).astype(BF16)
            ddk = _dot(dst, qq)
            return ddk, ddv, _dot_tn(dst, kk)

        def step(i, carry):
            dk, dv = carry
            rows = pl.ds(pl.multiple_of(i * tq, tq), tq)
            ddk, ddv, ddq = block(kj, vj, q_ref[0, rows, :], do_ref[rows, :], lse_ref[0, pl.ds(i, 1), :],
                                  dl_ref[0, pl.ds(i, 1), :], False)
            dq_acc[rows, :] += ddq
            return dk + ddk, dv + ddv

        th = tq // 2
        lse_j, dl_j = lse_ref[0, pl.ds(j, 1), :], dl_ref[0, pl.ds(j, 1), :]
        parts = []
        for kh, qh, masked in ((0, 0, True), (0, 1, False), (1, 1, True)):
            rows = pl.ds(pl.multiple_of(j * tq + qh * th, th), th)
            ks, qs = slice(kh * th, (kh + 1) * th), slice(qh * th, (qh + 1) * th)
            ddk, ddv, ddq = block(kj[ks], vj[ks], q_ref[0, rows, :], do_ref[rows, :], lse_j[:, qs], dl_j[:, qs], masked)
            dq_acc[rows, :] += ddq
            parts.append((ddk, ddv))
        carry = (jnp.concatenate([parts[0][0] + parts[1][0], parts[2][0]], axis=0),
                 jnp.concatenate([parts[0][1] + parts[1][1], parts[2][1]], axis=0))
        dk, dv = lax.fori_loop(j + 1, nq, step, carry)
        dk_ref[0] = (dk * LN2).astype(BF16)
        dv_ref[0] = dv.astype(BF16)

        @pl.when(j == nq - 1)
        def _():
            dq_ref[0] = dq_acc[...].astype(BF16)

    return pl.pallas_call(
        body, name="flash_bwd", grid=(nh, nq),
        in_specs=[pl.BlockSpec((1, s, QK_PAD), lambda h, j: (h, 0, 0)),
                  pl.BlockSpec((1, tq, QK_PAD), lambda h, j: (h, j, 0)),
                  pl.BlockSpec((1, tq, V_HEAD), lambda h, j: (h, j, 0)),
                  pl.BlockSpec((s, V_HEAD), lambda h, j: (0, h)),
                  pl.BlockSpec((1, nq, tq), lambda h, j: (h, 0, 0)),
                  pl.BlockSpec((1, nq, tq), lambda h, j: (h, 0, 0))],
        out_specs=[pl.BlockSpec((1, s, QK_PAD), lambda h, j: (h, 0, 0)),
                   pl.BlockSpec((1, tq, QK_PAD), lambda h, j: (h, j, 0)),
                   pl.BlockSpec((1, tq, V_HEAD), lambda h, j: (h, j, 0))],
        out_shape=[jax.ShapeDtypeStruct((nh, s, QK_PAD), BF16), jax.ShapeDtypeStruct((nh, s, QK_PAD), BF16),
                   jax.ShapeDtypeStruct((nh, s, V_HEAD), BF16)],
        scratch_shapes=[pltpu.VMEM((s, QK_PAD), F32)],
        compiler_params=_params(("parallel", "arbitrary")),
    )(q, k, v, do, lse, delta)


def _tail(x, target, o, u, mod, w_out, conv_w):
    s, d = x.shape
    tm = min(TM_ELEM, s)

    def body(x_ref, t_ref, o_ref, za_ref, mod_ref, w_ref, xc_ref, bc_ref, cc_ref, zc_ref, xp_ref, cp_ref, cw_ref,
             gx_ref, dy_ref, ycat_ref, dyc_ref, do_ref, du_ref, delta_ref, dgate_ref, loss_ref):
        i = pl.program_id(0)
        za = pltpu.roll(za_ref[:, ZA_LO:ZA_LO + ZA_WIN].astype(F32), ZA_WIN - QK_ROPE, 1)[:, :D_ATTN]
        ov = o_ref[...].astype(F32)
        sg = _sigmoid(za)
        sl = za * sg
        ya = ov * sl
        y = _dot(ya.astype(BF16), w_ref[D_CONV:, :])
        yc = _conv_y(xc_ref, bc_ref, cc_ref, zc_ref, xp_ref, cp_ref, cw_ref, i == 0)
        y = y + _dot(yc.astype(BF16), w_ref[:D_CONV, :])
        ycat_ref[...] = jnp.concatenate([yc.T, ya.T], axis=0).astype(BF16)
        gate = mod_ref[:, 2 * d:3 * d]
        e = x_ref[...] + gate * y - t_ref[...]
        dout = e * (1.0 / d)
        gx_ref[...] = dout
        dy = (dout * gate).astype(BF16)
        dy_ref[...] = dy
        dycat = _dot_nt(dy, w_ref[...])
        dyc_ref[...] = dycat[:, :D_CONV].astype(BF16)
        dya = dycat[:, D_CONV:]
        dov = dya * sl
        do_ref[...] = dov.astype(BF16)
        du_ref[...] = (dya * ov * (sg * (1.0 + za * (1.0 - sg)))).astype(BF16)
        prod_t = (dov * ov).T
        for h in range(N_HEADS):
            delta_ref[h] = jnp.sum(prod_t[V_HEAD * h:V_HEAD * (h + 1), :], axis=0, keepdims=True)
        dgate = jnp.sum(dout * y, axis=0, keepdims=True)
        part = jnp.sum(jnp.sum(e * e, axis=0, keepdims=True), axis=1, keepdims=True) * (0.5 / d)
        part = jnp.broadcast_to(part, (1, LANES))

        @pl.when(i == 0)
        def _():
            dgate_ref[...] = dgate
            loss_ref[...] = part

        @pl.when(i > 0)
        def _():
            dgate_ref[...] += dgate
            loss_ref[...] += part

    tok = lambda w: pl.BlockSpec((tm, w), lambda i: (i, 0))
    return pl.pallas_call(
        body, name="tail", grid=(s // tm,),
        in_specs=[tok(d), tok(d), tok(D_ATTN), pl.BlockSpec((tm, U_TAIL), lambda i: (i, U_COLS // U_TAIL - 1)),
                  _full((1, 3 * d)), _full((d, d))] + _conv_specs(tm) + [_full((3, D_CONV))],
        out_specs=[tok(d), tok(d), pl.BlockSpec((d, tm), lambda i: (0, i)), tok(D_CONV), tok(D_ATTN), tok(D_ATTN),
                   pl.BlockSpec((N_HEADS, 1, tm), lambda i: (0, 0, i)), _full((1, d)), _full((1, LANES))],
        out_shape=[jax.ShapeDtypeStruct((s, d), F32), jax.ShapeDtypeStruct((s, d), BF16),
                   jax.ShapeDtypeStruct((d, s), BF16), jax.ShapeDtypeStruct((s, D_CONV), BF16),
                   jax.ShapeDtypeStruct((s, D_ATTN), BF16), jax.ShapeDtypeStruct((s, D_ATTN), BF16),
                   jax.ShapeDtypeStruct((N_HEADS, 1, s), F32), jax.ShapeDtypeStruct((1, d), F32),
                   jax.ShapeDtypeStruct((1, LANES), F32)],
        compiler_params=_params(("arbitrary",)),
    )(x, target, o, u, mod, w_out, u, u, u, u, u, u, conv_w)


def _norm_bwd(x, dh, gx1, norm_g, mod):
    s, d = x.shape
    tm = min(TM_MM, s)

    def body(x_ref, dh_ref, gx_ref, g_ref, mod_ref, o_ref, dshift_ref, dscale_ref, dg_ref):
        i = pl.program_id(0)
        gv, sc1 = g_ref[...], 1.0 + mod_ref[:, d:2 * d]
        gsc = gv * sc1
        half = NORM_ROWS // 2

        def group(c, acc):
            a_dh, a_dhxn = acc
            ks = range(NORM_GROUP)
            rows = [pl.ds(pl.multiple_of((c * NORM_GROUP + k) * NORM_ROWS, NORM_ROWS), NORM_ROWS) for k in ks]
            xv = [x_ref[rows[k], :] for k in ks]
            dhv = [dh_ref[rows[k], :].astype(F32) for k in ks]
            r = [lax.rsqrt(jnp.mean(xv[k] * xv[k], axis=-1, keepdims=True) + EPS) for k in ks]
            xn = [xv[k] * r[k] for k in ks]
            dxn = [dhv[k] * gsc for k in ks]
            t = [jnp.mean(dxn[k] * xn[k], axis=-1, keepdims=True) for k in ks]
            for k in ks:
                o_ref[rows[k], :] = gx_ref[rows[k], :] + r[k] * (dxn[k] - xn[k] * t[k])
                dhxn = dhv[k] * xn[k]
                a_dh = a_dh + dhv[k][:half] + dhv[k][half:]
                a_dhxn = a_dhxn + dhxn[:half] + dhxn[half:]
            return a_dh, a_dhxn

        zero = jnp.zeros((half, d), F32)
        a_dh, a_dhxn = lax.fori_loop(0, tm // (NORM_ROWS * NORM_GROUP), group, (zero, zero))
        dshift = jnp.sum(a_dh, axis=0, keepdims=True)
        s_dhxn = jnp.sum(a_dhxn, axis=0, keepdims=True)
        dscale, dg = s_dhxn * gv, s_dhxn * sc1

        @pl.when(i == 0)
        def _():
            dshift_ref[...] = dshift
            dscale_ref[...] = dscale
            dg_ref[...] = dg

        @pl.when(i > 0)
        def _():
            dshift_ref[...] += dshift
            dscale_ref[...] += dscale
            dg_ref[...] += dg

    tok = pl.BlockSpec((tm, d), lambda i: (i, 0))
    row = jax.ShapeDtypeStruct((1, d), F32)
    return pl.pallas_call(
        body, name="norm_bwd", grid=(s // tm,),
        in_specs=[tok, tok, tok, _full((1, d)), _full((1, 3 * d))],
        out_specs=[tok, _full((1, d)), _full((1, d)), _full((1, d))],
        out_shape=[jax.ShapeDtypeStruct((s, d), F32), row, row, row],
        compiler_params=_params(("arbitrary",)),
    )(x, dh, gx1, norm_g, mod)


def _adamw(w, g, m, v, name):
    rows, cols = w.shape
    tr = 256 if rows % 256 == 0 else rows

    def body(w_ref, g_ref, m_ref, v_ref, d_ref, nm_ref, nv_ref):
        gv = g_ref[...]
        nm = ADAM_B1 * m_ref[...] + (1.0 - ADAM_B1) * gv
        nv = ADAM_B2 * v_ref[...] + (1.0 - ADAM_B2) * (gv * gv)
        m_hat = nm / (1.0 - ADAM_B1 ** ADAM_STEP)
        v_hat = nv / (1.0 - ADAM_B2 ** ADAM_STEP)
        d_ref[...] = -ADAM_LR * (m_hat / (jnp.sqrt(v_hat) + ADAM_EPS) + ADAM_WD * w_ref[...])
        nm_ref[...] = nm
        nv_ref[...] = nv

    spec = pl.BlockSpec((tr, cols), lambda i: (i, 0))
    shape = jax.ShapeDtypeStruct((rows, cols), F32)
    return pl.pallas_call(
        body, name=name, grid=(rows // tr,), in_specs=[spec] * 4, out_specs=[spec] * 3, out_shape=[shape] * 3,
        compiler_params=_params(("parallel",)),
    )(w, g, m, v)


def _pad_cols(a, n):
    return jnp.pad(a, ((0, 0), (0, n - a.shape[1])))


def kernel(x, c, positions, ada_w, ada_b, norm_g, w_in, conv_w, q_a_g, w_q_b, kv_a_g, w_kv_b, q_g, k_g, w_out, loss_target, m_ada_w, m_ada_b, m_norm_g, m_w_in, m_conv_w, m_q_a_g, m_w_q_b, m_kv_a_g, m_w_kv_b, m_q_g, m_k_g, m_w_out, v_ada_w, v_ada_b, v_norm_g, v_w_in, v_conv_w, v_q_a_g, v_w_q_b, v_kv_a_g, v_w_kv_b, v_q_g, v_k_g, v_w_out):
    me = _my_index()
    s = x.shape[1]
    nq = s // min(TQ, s)
    x2, tgt = x[0], loss_target[0]
    w_in_l, w_q_l, w_kv_l, w_out_l, conv_l, ada_w_l = w_in[0], w_q_b[0], w_kv_b[0], w_out[0], conv_w[0], ada_w[0]
    ada_cols = ada_w_l.shape[1]

    small = jnp.concatenate([c.reshape(-1, LANES), conv_l.reshape(-1, LANES), jnp.zeros((5, LANES), F32)], axis=0)
    (small_g,) = _all_gather([small], "gather_c")
    c_all = small_g[:, :D_MODEL // LANES].reshape(N_DEV, D_MODEL)
    conv_g = small_g[:, D_MODEL // LANES:D_MODEL // LANES + 3].transpose(1, 0, 2).reshape(3, D_CONV)

    ada_b_l = lax.dynamic_slice(ada_b, (0, me * ada_cols), (1, ada_cols))
    mod_cols = _ada_mod(jnp.pad(c_all, ((0, 8), (0, 0))), ada_w_l, ada_b_l)[:N_DEV]
    (mod_g,) = _all_gather([mod_cols], "gather_mod")
    mod = lax.dynamic_index_in_dim(mod_g, me, axis=1, keepdims=False).reshape(1, 3 * D_MODEL)

    half = jnp.arange(0, QK_ROPE, 2, dtype=F32) / QK_ROPE
    inv_freq = ROPE_BASE ** (-half)
    zeros64 = jnp.zeros((LANES - QK_ROPE,), F32)
    invf = jnp.concatenate([inv_freq, inv_freq, zeros64]).reshape(1, LANES)
    sign = jnp.concatenate([-jnp.ones((32,), F32), jnp.ones((32,), F32), zeros64]).reshape(1, LANES)
    qg_p, kg_p = _pad_cols(q_g, QK_PAD), _pad_cols(k_g, QK_PAD)

    my_off = ((CW * me) % LANES).astype(jnp.int32)
    win = [_expand_w_in(w_in_l, my_off.reshape(1))]
    h, h_t, cos, sin, win_g = _norm_mod(x2, norm_g, mod, positions.reshape(s, 1), invf, sign, _Gather(win, relay=True, parts=4), win)
    w_in_p = _merge_w_in(win_g)
    rest = [_pad_wq(w_q_l), w_kv_l.astype(BF16), w_out_l.astype(BF16)]
    u, wq_g, wkv_g, w_out_g = _matmul(h, w_in_p, nt=False, out_dtype=BF16, tm=2 * TM_MM, tn=1024, name="in_proj",
                                      rider=_Gather(rest), rider_inputs=rest)
    w_out_g = w_out_g.reshape(D_MODEL, D_MODEL)
    wq_g = wq_g.transpose(1, 0, 2).reshape(Q_LORA, N_HEADS * QK_PAD)
    wkv_g = wkv_g.transpose(1, 0, 2).reshape(KV_LORA, 2 * D_ATTN)
    q, k, v = _qkv_fwd(u, cos, sin, wq_g, wkv_g, q_a_g, kv_a_g, qg_p, kg_p)
    o, lse = _flash_fwd(q, k, v)
    gx1, dy, ycat_t, dyc, do, dza, delta, dgate, loss_row = _tail(x2, tgt, o, u, mod, w_out_g, conv_g)

    dq, dk, dv = _flash_bwd(q, k, v, do, lse.reshape(N_HEADS, nq, s // nq), delta.reshape(N_HEADS, nq, s // nq))
    du, dconv = _conv_bwd(u, dyc, conv_g)
    du, dwq, dwkv, dqag, dkvag, dqg, dkg = _qkv_bwd(u, cos, sin, dq, dk, dv, dza, wq_g, wkv_g, q_a_g, kv_a_g, qg_p, kg_p, du)
    dwq = dwq.reshape(Q_LORA, N_HEADS, QK_PAD).transpose(1, 0, 2)
    dwkv = dwkv.reshape(KV_LORA, N_HEADS, 2 * V_HEAD).transpose(1, 0, 2)
    dw_in = _matmul(h_t, du, nt=False, out_dtype=BF16, tm=TM_MM, tn=768, name="dw_in")
    first = [dw_in, dwq, dwkv]
    dw_out, r_in, r_q, r_kv = _matmul(ycat_t, dy, nt=False, out_dtype=BF16, tm=TM_MM, tn=512, name="dw_out",
                                      rider=_SiblingExchange(first, [True, False, False]), rider_inputs=first)
    dw_out = dw_out.reshape(N_DEV, D_MODEL // N_DEV, D_MODEL)
    (r_out,) = _exchange(_SiblingExchange([dw_out], [False]), [dw_out], "rs_sibling_out")
    core = lax.axis_index("c").astype(jnp.int32)
    lo_tiles = ((CW * (2 * jnp.arange(4, dtype=jnp.int32) + core)) // LANES).astype(jnp.int32)
    pairs = [_add_window(dw_in, r_in, lo_tiles), _add_pairs(dwq, r_q, core.reshape(1), "rs_add_q"),
             _add_pairs(dwkv, r_kv, core.reshape(1), "rs_add_kv"), _add_pairs(dw_out, r_out, core.reshape(1), "rs_add_out")]
    dh, *quads = _matmul(du, w_in_p, nt=True, out_dtype=BF16, tm=2 * TM_MM, tn=512, name="dh",
                         rider=_ChipExchange(pairs), rider_inputs=pairs, a_resident=True)
    my_chip = 2 * lax.axis_index("x") + lax.axis_index("y")
    written = jnp.where(jnp.arange(4) == my_chip, (jnp.arange(4) + 1) % 4, jnp.arange(4))
    sel = jnp.concatenate([my_chip.reshape(1), written, ((EXP_W - my_off) % EXP_W).reshape(1)]).astype(jnp.int32)
    g_w_in = _final_sum(pairs[0], quads[0], sel, "rs_sum_in", unshift_to=CW)
    g_w_q = _final_sum(pairs[1], quads[1], sel, "rs_sum_q")[:, :QK_HEAD]
    g_w_kv = _final_sum(pairs[2], quads[2], sel, "rs_sum_kv")
    g_w_out = _final_sum(pairs[3], quads[3], sel, "rs_sum_out")
    grad_x, dshift, dscale, dng = _norm_bwd(x2, dh, gx1, norm_g, mod)

    row = jnp.concatenate([dshift, dscale, dgate, dng, dqag, dkvag, dqg, dkg, dconv[:3].reshape(1, 3 * D_CONV), loss_row], axis=1)
    (rows_g,) = _all_gather([row], "gather_small")
    tot = _sum_leading(rows_g, F32, "sum_small")
    dmod_all = rows_g[:, 0, SM_MOD:SM_NG]
    g_ada_b = tot[:, SM_MOD:SM_NG]
    g_norm_g = tot[:, SM_NG:SM_QAG]
    g_q_a_g = tot[:, SM_QAG:SM_KVAG]
    g_kv_a_g = tot[:, SM_KVAG:SM_QG]
    g_q_g = tot[:, SM_QG:SM_QG + QK_HEAD]
    g_k_g = tot[:, SM_KG:SM_KG + QK_HEAD]
    conv_cols = conv_l.shape[1]
    g_conv = lax.dynamic_slice(tot[:, SM_CONV:SM_LOSS].reshape(3, D_CONV), (0, me * conv_cols), (3, conv_cols))
    loss = tot[0, SM_LOSS]
    dmod_my = lax.dynamic_slice(dmod_all, (0, me * ada_cols), (N_DEV, ada_cols))
    g_ada_w = _ada_w_grad(c_all.T, dmod_my)

    grads = dict(ada_w=g_ada_w, ada_b=g_ada_b, norm_g=g_norm_g, w_in=g_w_in, conv_w=g_conv, q_a_g=g_q_a_g, w_q_b=g_w_q,
                 kv_a_g=g_kv_a_g, w_kv_b=g_w_kv, q_g=g_q_g, k_g=g_k_g, w_out=g_w_out)
    weights = dict(ada_w=(ada_w, m_ada_w, v_ada_w), ada_b=(ada_b, m_ada_b, v_ada_b), norm_g=(norm_g, m_norm_g, v_norm_g),
                   w_in=(w_in, m_w_in, v_w_in), conv_w=(conv_w, m_conv_w, v_conv_w), q_a_g=(q_a_g, m_q_a_g, v_q_a_g),
                   w_q_b=(w_q_b, m_w_q_b, v_w_q_b), kv_a_g=(kv_a_g, m_kv_a_g, v_kv_a_g), w_kv_b=(w_kv_b, m_w_kv_b, v_w_kv_b),
                   q_g=(q_g, m_q_g, v_q_g), k_g=(k_g, m_k_g, v_k_g), w_out=(w_out, m_w_out, v_w_out))
    names = list(grads)
    out_g, out_d, out_m, out_v = [], [], [], []
    for n in names:
        w, m, v_ = weights[n]
        shape2 = w.shape[-2:] if w.ndim == 3 else (1, w.shape[-1])
        g2 = grads[n].reshape(shape2)
        d2, m2, v2 = _adamw(w.reshape(shape2), g2, m.reshape(shape2), v_.reshape(shape2), "adamw_" + n)
        out_g.append(g2.reshape(w.shape))
        out_d.append(d2.reshape(w.shape))
        out_m.append(m2.reshape(w.shape))
        out_v.append(v2.reshape(w.shape))
    return (loss, grad_x.reshape(x.shape), *out_g, *out_d, *out_m, *out_v)
```

```python
import functools
import math

import jax
import jax.numpy as jnp
from jax import lax
from jax.experimental import pallas as pl
from jax.experimental.pallas import tpu as pltpu

F32 = jnp.float32
BF16 = jnp.bfloat16
MESH = pl.DeviceIdType.MESH

D_MODEL = 2048
D_CONV = 1024
N_HEADS = 8
QK_NOPE = 128
QK_ROPE = 64
QK_HEAD = QK_NOPE + QK_ROPE
V_HEAD = 128
D_ATTN = N_HEADS * V_HEAD
Q_LORA = 512
KV_LORA = 256
ROPE_BASE = 10000.0
IN_COLS = 4 * D_CONV + Q_LORA + KV_LORA + QK_ROPE + D_ATTN
EPS = 1e-6
ADAM_LR, ADAM_B1, ADAM_B2, ADAM_EPS, ADAM_WD, ADAM_STEP = 0.001, 0.9, 0.999, 1e-08, 0.01, 10

N_DEV = 8
LANES = 128
QK_PAD = 256
U_COLS = 6144
U_CQ, U_CKV, U_KR, U_ZA = 4096, 4608, 4864, 4928
U_TAIL = 2048
ZA_LO = U_ZA - (U_COLS - U_TAIL) - QK_ROPE
ZA_WIN = D_ATTN + LANES
CW = IN_COLS // 8
EXP_W = 896
W_LO = [(CW * d // 128) * 128 for d in range(8)]
W_OFF = [CW * d - lo for d, lo in enumerate(W_LO)]
SCALE = 1.0 / math.sqrt(QK_HEAD)
LOG2E = 1.4426950408889634
LN2 = 0.6931471805599453
NEG = -1e30
VMEM_LIMIT = 56 * 1024 * 1024

TM_ELEM = 256
TM_QKV_BWD = 256
NORM_ROWS = 16
NORM_GROUP = 4
TM_MM = 512
TQ = 1024
Q_CHAINS = 4
KV_SPLIT = 2

SM_MOD, SM_NG, SM_QAG, SM_KVAG, SM_QG, SM_KG, SM_CONV, SM_LOSS = 0, 6144, 8192, 8704, 8960, 9216, 9472, 12544
SM_COLS = 12672


def _params(sem=None):
    kw = dict(vmem_limit_bytes=VMEM_LIMIT)
    if sem is not None:
        kw["dimension_semantics"] = sem
    return pltpu.CompilerParams(**kw)


def _sigmoid(z):
    return 1.0 / (1.0 + jnp.exp(-z))


def _rot64(x):
    lane = lax.broadcasted_iota(jnp.int32, x.shape, 1)
    return jnp.where(lane < 32, pltpu.roll(x, 96, 1), pltpu.roll(x, 32, 1))


def _rope(x, cos, sin):
    return x * cos + _rot64(x) * sin


def _rope_t(d, cos, sin):
    return d * cos - _rot64(d) * sin


def _dot(a, b):
    return jnp.dot(a, b, preferred_element_type=F32)


def _dot_nt(a, b):
    return lax.dot_general(a, b, (((1,), (1,)), ((), ())), preferred_element_type=F32)


def _dot_tn(a, b):
    return lax.dot_general(a, b, (((0,), (0,)), ((), ())), preferred_element_type=F32)


def _my_index():
    return 4 * lax.axis_index("x") + 2 * lax.axis_index("y") + lax.axis_index("c")


ANY = pl.BlockSpec(memory_space=pl.ANY)


class _Gather:
    def __init__(self, blocks, relay=False, parts=1):
        self.relay = relay
        self.parts = parts
        self.rows = [b.shape[0] // parts for b in blocks]
        self.n = n = len(blocks) * parts
        self.out_shape = [jax.ShapeDtypeStruct((N_DEV,) + b.shape, b.dtype) for b in blocks]
        self.scratch = [pltpu.SemaphoreType.DMA((7 * n,)), pltpu.SemaphoreType.DMA((7 * n,)),
                        pltpu.SemaphoreType.DMA((n,))]

    @staticmethod
    def _places():
        x, y, c = lax.axis_index("x"), lax.axis_index("y"), lax.axis_index("c")
        return (x, y, c), (x, y, 1 - c), [(1 - x, y), (x, 1 - y), (1 - x, 1 - y)]

    def _src(self, ins, a):
        block, part = divmod(a, self.parts)
        return ins[block] if self.parts == 1 else ins[block].at[pl.ds(part * self.rows[block], self.rows[block])]

    def _dst(self, outs, a, place):
        block, part = divmod(a, self.parts)
        ref = outs[block].at[4 * place[0] + 2 * place[1] + place[2]]
        return ref if self.parts == 1 else ref.at[pl.ds(part * self.rows[block], self.rows[block])]

    def _copy(self, outs, sems, a, k, block, to, src=None):
        dst = self._dst(outs, a, block)
        return pltpu.make_async_remote_copy(
            src_ref=dst if src is None else src, dst_ref=dst, send_sem=sems[0].at[7 * a + k],
            recv_sem=sems[1].at[7 * a + k], device_id=to, device_id_type=MESH)

    def _first(self, ins, outs, sems):
        me, sibling, chips = self._places()
        first = []
        for a in range(self.n):
            first.append(self._copy(outs, sems, a, 0, me, sibling, src=self._src(ins, a)))
            first += [self._copy(outs, sems, a, 1 + j, me, (*chip, me[2]), src=self._src(ins, a))
                      for j, chip in enumerate(chips[:2] if self.relay else chips)]
        return first

    def _relays(self, outs, sems):
        if not self.relay:
            return []
        (x, y, c), _, _ = self._places()
        via = (jnp.where(c == 0, 1 - x, x), jnp.where(c == 0, y, 1 - y))
        to = (jnp.where(c == 0, x, 1 - x), jnp.where(c == 0, 1 - y, y))
        return [self._copy(outs, sems, a, 3, (*via, c), (*to, c)) for a in range(self.n)]

    def _passed(self, outs, sems):
        me, sibling, chips = self._places()
        return [self._copy(outs, sems, a, 4 + j, (*chip, me[2]), sibling)
                for a in range(self.n) for j, chip in enumerate(chips)]

    def _mine(self, ins, outs, sems):
        me, _, _ = self._places()
        return [pltpu.make_async_copy(self._src(ins, a), self._dst(outs, a, me), sems[2].at[a]) for a in range(self.n)]

    def start(self, ins, outs, sems):
        for cp in self._mine(ins, outs, sems) + self._first(ins, outs, sems):
            cp.start()

    def forward(self, ins, outs, sems):
        del ins
        me, _, chips = self._places()
        passed, relays = self._passed(outs, sems), self._relays(outs, sems)
        for a in range(self.n):
            for j, chip in enumerate(chips[:2] if self.relay else chips):
                self._copy(outs, sems, a, 1 + j, (*chip, me[2]), me).wait_recv()
                passed[3 * a + j].start()
            if self.relay:
                relays[a].start()
        if self.relay:
            for a in range(self.n):
                self._copy(outs, sems, a, 3, (*chips[2], me[2]), me).wait_recv()
                passed[3 * a + 2].start()

    def finish(self, ins, outs, sems):
        me, sibling, chips = self._places()
        for a in range(self.n):
            self._copy(outs, sems, a, 0, sibling, me).wait_recv()
            for j, chip in enumerate(chips):
                self._copy(outs, sems, a, 4 + j, (*chip, 1 - me[2]), me).wait_recv()
        for cp in self._first(ins, outs, sems) + self._relays(outs, sems) + self._passed(outs, sems):
            cp.wait_send()
        for cp in self._mine(ins, outs, sems):
            cp.wait()


class _ChipExchange:
    def __init__(self, arrays):
        self.n = n = len(arrays)
        self.out_shape = [jax.ShapeDtypeStruct(a.shape, a.dtype) for a in arrays]
        self.scratch = [pltpu.SemaphoreType.DMA((3 * n,)), pltpu.SemaphoreType.DMA((3 * n,))]

    def _copies(self, ins, outs, sems):
        x, y, c = lax.axis_index("x"), lax.axis_index("y"), lax.axis_index("c")
        return [pltpu.make_async_remote_copy(
            src_ref=ins[a].at[2 * px + py], dst_ref=outs[a].at[2 * x + y], send_sem=sems[0].at[3 * a + j],
            recv_sem=sems[1].at[3 * a + j], device_id=(px, py, c), device_id_type=MESH)
            for a in range(self.n) for j, (px, py) in enumerate([(1 - x, y), (x, 1 - y), (1 - x, 1 - y)])]

    def start(self, ins, outs, sems):
        for cp in self._copies(ins, outs, sems):
            cp.start()

    def forward(self, ins, outs, sems):
        pass

    def finish(self, ins, outs, sems):
        for cp in self._copies(ins, outs, sems):
            cp.wait()


def _all_gather(blocks, name):
    n = len(blocks)
    g = _Gather(blocks)

    def body(*refs):
        ins, outs, sems = refs[:n], refs[n:2 * n], refs[2 * n:]
        g.start(ins, outs, sems)
        g.forward(ins, outs, sems)
        g.finish(ins, outs, sems)

    return pl.pallas_call(body, name=name, out_shape=g.out_shape, in_specs=[ANY] * n, out_specs=[ANY] * n,
                          scratch_shapes=g.scratch)(*blocks)


class _SiblingExchange:
    def __init__(self, arrays, windowed):
        self.n = n = len(arrays)
        self.windowed = windowed
        self.out_shape = [jax.ShapeDtypeStruct((4, a.shape[0], EXP_W) if w else (4,) + a.shape[1:], a.dtype)
                          for a, w in zip(arrays, windowed)]
        self.scratch = [pltpu.SemaphoreType.DMA((4 * n,)), pltpu.SemaphoreType.DMA((4 * n,))]

    def _each(self, ins, outs, sems, act):
        x, y, c = lax.axis_index("x"), lax.axis_index("y"), lax.axis_index("c")

        def branch(c_val):
            for k in range(4):
                e = 2 * k + (1 - c_val)
                for a in range(self.n):
                    src = ins[a].at[:, pl.ds(W_LO[e], EXP_W)] if self.windowed[a] else ins[a].at[e]
                    act(pltpu.make_async_remote_copy(
                        src_ref=src, dst_ref=outs[a].at[k], send_sem=sems[0].at[4 * a + k], recv_sem=sems[1].at[4 * a + k],
                        device_id=(x, y, 1 - c), device_id_type=MESH))

        for c_val in (0, 1):
            pl.when(c == c_val)(functools.partial(branch, c_val))

    def start(self, ins, outs, sems):
        self._each(ins, outs, sems, lambda cp: cp.start())

    def forward(self, ins, outs, sems):
        pass

    def finish(self, ins, outs, sems):
        self._each(ins, outs, sems, lambda cp: cp.wait())


def _exchange(rider, arrays, name):
    n = len(arrays)

    def body(*refs):
        ins, outs, sems = refs[:n], refs[n:n + len(rider.out_shape)], refs[n + len(rider.out_shape):]
        rider.start(ins, outs, sems)
        rider.forward(ins, outs, sems)
        rider.finish(ins, outs, sems)

    return pl.pallas_call(body, name=name, out_shape=rider.out_shape, in_specs=[ANY] * n,
                          out_specs=[ANY] * len(rider.out_shape), scratch_shapes=rider.scratch)(*arrays)


def _add_window(dw_in, recv, lo_tiles):
    k, rows, _ = recv.shape

    def body(t_ref, w_ref, r_ref, o_ref):
        del t_ref
        o_ref[0] = (w_ref[...].astype(F32) + r_ref[0].astype(F32)).astype(o_ref.dtype)

    spec = pl.BlockSpec((1, rows, LANES), lambda i, j, t: (i, 0, j))
    grid_spec = pltpu.PrefetchScalarGridSpec(
        num_scalar_prefetch=1, grid=(k, EXP_W // LANES),
        in_specs=[pl.BlockSpec((rows, LANES), lambda i, j, t: (0, t[i] + j)), spec], out_specs=spec)
    return pl.pallas_call(
        body, name="rs_add_in", grid_spec=grid_spec, out_shape=jax.ShapeDtypeStruct(recv.shape, recv.dtype),
        compiler_params=_params(("parallel", "parallel")),
    )(lo_tiles, dw_in, recv)


def _final_sum(p, r, sel, name, unshift=False, keep_t=None):
    _, rows, cols = p.shape
    tr = 512 if rows % 512 == 0 else rows

    def body(sel_ref, p_ref, r0, r1, r2, r3, o_ref):
        own = p_ref[0].astype(F32)
        acc = None
        for k, r_ref in enumerate((r0, r1, r2, r3)):
            term = jnp.where(sel_ref[0] == k, own, r_ref[0].astype(F32))
            acc = term if acc is None else acc + term
        if unshift:
            acc = pltpu.roll(acc, sel_ref[5], 1)
        o_ref[...] = acc if keep_t is None else acc.T[:keep_t]

    def slot(k):
        return pl.BlockSpec((1, tr, cols), lambda i, t: (t[k], i, 0))

    if keep_t is None:
        out_spec, out_shape = pl.BlockSpec((tr, cols), lambda i, t: (i, 0)), (rows, cols)
    else:
        out_spec, out_shape = pl.BlockSpec((keep_t, tr), lambda i, t: (0, i)), (keep_t, rows)
    grid_spec = pltpu.PrefetchScalarGridSpec(
        num_scalar_prefetch=1, grid=(rows // tr,), in_specs=[slot(0), slot(1), slot(2), slot(3), slot(4)],
        out_specs=out_spec)
    return pl.pallas_call(
        body, name=name, grid_spec=grid_spec, out_shape=jax.ShapeDtypeStruct(out_shape, F32),
        compiler_params=_params(("parallel",)),
    )(sel, p, r, r, r, r)


def _expand_w_in(w_t, shift):
    cw, rows = w_t.shape
    tr = 256
    pad = -cw % LANES

    def body(s_ref, w_ref, o_ref):
        w = jnp.concatenate([w_ref[...], jnp.zeros((pad, tr), F32)], axis=0).T
        w = jnp.concatenate([w, jnp.zeros((tr, EXP_W - cw - pad), F32)], axis=1)
        o_ref[...] = pltpu.roll(w, s_ref[0], 1).astype(BF16)

    grid_spec = pltpu.PrefetchScalarGridSpec(
        num_scalar_prefetch=1, grid=(rows // tr,), in_specs=[pl.BlockSpec((cw, tr), lambda i, t: (0, i))],
        out_specs=pl.BlockSpec((tr, EXP_W), lambda i, t: (i, 0)))
    return pl.pallas_call(
        body, name="expand_w_in", grid_spec=grid_spec, out_shape=jax.ShapeDtypeStruct((rows, EXP_W), BF16),
        compiler_params=_params(("arbitrary",)),
    )(shift, w_t)


def _pad_wq(w_t):
    cw, rows = w_t.shape

    def body(w_ref, o_ref):
        o_ref[...] = jnp.concatenate([w_ref[...], jnp.zeros((QK_PAD - cw, rows), F32)], axis=0).T.astype(BF16)

    return pl.pallas_call(
        body, name="pad_wq", out_shape=jax.ShapeDtypeStruct((rows, QK_PAD), BF16), compiler_params=_params(),
    )(w_t)


def _merge_w_in(e):
    _, rows, _ = e.shape
    tr = 256

    def body(e_ref, o_ref):
        for t in range(U_COLS // LANES):
            lo, hi = t * LANES, (t + 1) * LANES
            parts = [e_ref[d, :, lo - W_LO[d]:hi - W_LO[d]] for d in range(N_DEV)
                     if CW * d < hi and CW * (d + 1) > lo]
            if not parts:
                tile = jnp.zeros((tr, LANES), BF16)
            elif len(parts) == 1:
                tile = parts[0]
            else:
                tile = (parts[0].astype(F32) + parts[1].astype(F32)).astype(BF16)
            o_ref[:, lo:hi] = tile

    return pl.pallas_call(
        body, name="merge_w_in", grid=(rows // tr,),
        in_specs=[pl.BlockSpec((N_DEV, tr, EXP_W), lambda i: (0, i, 0))],
        out_specs=pl.BlockSpec((tr, U_COLS), lambda i: (i, 0)), out_shape=jax.ShapeDtypeStruct((rows, U_COLS), BF16),
        compiler_params=_params(("parallel",)),
    )(e)


def _sum_leading(a, out_dtype, name):
    k, rows, cols = a.shape
    tr = min(rows, 1728 if rows % 1728 == 0 else rows)

    def body(a_ref, o_ref):
        acc = a_ref[0].astype(F32)
        for i in range(1, k):
            acc = acc + a_ref[i].astype(F32)
        o_ref[...] = acc.astype(out_dtype)

    return pl.pallas_call(
        body, name=name, grid=(rows // tr,),
        in_specs=[pl.BlockSpec((k, tr, cols), lambda i: (0, i, 0))],
        out_specs=pl.BlockSpec((tr, cols), lambda i: (i, 0)),
        out_shape=jax.ShapeDtypeStruct((rows, cols), out_dtype), compiler_params=_params(("parallel",)),
    )(a)


def _add_pairs(g, recv, core, name):
    k, rows, cols = recv.shape
    tr = 1728 if rows % 1728 == 0 else rows

    def body(c_ref, g_ref, r_ref, o_ref):
        del c_ref
        o_ref[...] = (g_ref[...].astype(F32) + r_ref[...].astype(F32)).astype(o_ref.dtype)

    spec = pl.BlockSpec((1, tr, cols), lambda i, j, c: (i, j, 0))
    grid_spec = pltpu.PrefetchScalarGridSpec(
        num_scalar_prefetch=1, grid=(k, rows // tr),
        in_specs=[pl.BlockSpec((1, tr, cols), lambda i, j, c: (2 * i + c[0], j, 0)), spec], out_specs=spec)
    return pl.pallas_call(
        body, name=name, grid_spec=grid_spec, out_shape=jax.ShapeDtypeStruct(recv.shape, recv.dtype),
        compiler_params=_params(("parallel", "parallel")),
    )(core, g, recv)


def _ada_mod(c16, ada_w_l, ada_b_l):
    def body(c_ref, w_ref, b_ref, o_ref):
        cv = c_ref[...]
        sc = (cv * _sigmoid(cv)).astype(BF16)
        o_ref[...] = _dot(sc, w_ref[...].astype(BF16)) + b_ref[...]

    return pl.pallas_call(
        body, name="ada_mod", out_shape=jax.ShapeDtypeStruct((c16.shape[0], ada_w_l.shape[1]), F32),
        compiler_params=_params(),
    )(c16, ada_w_l, ada_b_l)


def _ada_w_grad(c_t, dmod_my):
    def body(c_ref, d_ref, o_ref):
        cv = c_ref[...]
        sc = cv * _sigmoid(cv)
        acc = sc[:, 0:1] * d_ref[0:1, :]
        for b in range(1, N_DEV):
            acc = acc + sc[:, b:b + 1] * d_ref[b:b + 1, :]
        o_ref[...] = acc

    return pl.pallas_call(
        body, name="ada_w_grad", out_shape=jax.ShapeDtypeStruct((c_t.shape[0], dmod_my.shape[1]), F32),
        compiler_params=_params(),
    )(c_t, dmod_my)


def _norm_mod(x, norm_g, mod, pos_col, invf, sign, rider, rider_inputs):
    s, d = x.shape
    tm = min(TM_MM, s)
    n_in, n_out = len(rider_inputs), len(rider.out_shape)
    steps = s // tm

    def body(x_ref, g_ref, mod_ref, p_ref, f_ref, s_ref, *rest):
        r_ins, (h_ref, ht_ref, cos_ref, sin_ref) = rest[:n_in], rest[n_in:n_in + 4]
        r_outs, sems = rest[n_in + 4:n_in + 4 + n_out], rest[n_in + 4 + n_out:]
        pl.when(pl.program_id(0) == 0)(functools.partial(rider.start, r_ins, r_outs, sems))
        xv = x_ref[...]
        r = lax.rsqrt(jnp.mean(xv * xv, axis=-1, keepdims=True) + EPS)
        hn = xv * r * g_ref[...]
        hv = hn * (1.0 + mod_ref[:, d:2 * d]) + mod_ref[:, 0:d]
        h_ref[...] = hv.astype(BF16)
        ht_ref[...] = hv.T.astype(BF16)
        ang = p_ref[...].astype(F32) * f_ref[...]
        sg = s_ref[...]
        cos_ref[...] = jnp.cos(ang) * jnp.abs(sg)
        sin_ref[...] = jnp.sin(ang) * sg

        @pl.when(pl.program_id(0) == steps - 1)
        def _():
            rider.forward(r_ins, r_outs, sems)
            rider.finish(r_ins, r_outs, sems)

    row = pl.BlockSpec((1, LANES), lambda i: (0, 0))
    tab = pl.BlockSpec((tm, LANES), lambda i: (i, 0))
    return pl.pallas_call(
        body, name="norm_mod", grid=(steps,),
        in_specs=[pl.BlockSpec((tm, d), lambda i: (i, 0)), pl.BlockSpec((1, d), lambda i: (0, 0)),
                  pl.BlockSpec((1, 3 * d), lambda i: (0, 0)), pl.BlockSpec((tm, 1), lambda i: (i, 0)), row, row]
        + [ANY] * n_in,
        out_specs=[pl.BlockSpec((tm, d), lambda i: (i, 0)), pl.BlockSpec((d, tm), lambda i: (0, i)), tab, tab] + [ANY] * n_out,
        out_shape=[jax.ShapeDtypeStruct((s, d), BF16), jax.ShapeDtypeStruct((d, s), BF16),
                   jax.ShapeDtypeStruct((s, LANES), F32), jax.ShapeDtypeStruct((s, LANES), F32)] + rider.out_shape,
        scratch_shapes=rider.scratch, compiler_params=_params(("arbitrary",)),
    )(x, norm_g, mod, pos_col, invf, sign, *rider_inputs)


def _matmul(a, b, *, nt, out_dtype, tm, tn, name, rider=None, rider_inputs=(), a_resident=False):
    m, kdim = a.shape
    n = b.shape[0] if nt else b.shape[1]
    tm, tn = min(tm, m), min(tn, n)
    n_in = len(rider_inputs)
    n_out = len(rider.out_shape) if rider else 0
    m_steps, n_steps = m // tm, n // tn
    steps = n_steps * m_steps
    inner = n_steps if a_resident else m_steps
    tile = (lambda o, i: (o, i)) if a_resident else (lambda o, i: (i, o))

    def body(a_ref, b_ref, *rest):
        r_ins, o_ref, r_outs, sems = rest[:n_in], rest[n_in], rest[n_in + 1:n_in + 1 + n_out], rest[n_in + 1 + n_out:]
        step = pl.program_id(0) * inner + pl.program_id(1)
        if rider:
            pl.when(step == 0)(functools.partial(rider.start, r_ins, r_outs, sems))
            pl.when(step == steps // 2)(functools.partial(rider.forward, r_ins, r_outs, sems))
        o = _dot_nt(a_ref[...], b_ref[...]) if nt else _dot(a_ref[...], b_ref[...])
        o_ref[...] = o.astype(out_dtype)
        if rider:
            pl.when(step == steps - 1)(functools.partial(rider.finish, r_ins, r_outs, sems))

    if nt:
        b_spec = pl.BlockSpec((tn, kdim), lambda o, i: (tile(o, i)[1], 0))
    else:
        b_spec = pl.BlockSpec((kdim, tn), lambda o, i: (0, tile(o, i)[1]))
    out = pl.pallas_call(
        body, name=name, grid=(m_steps, n_steps) if a_resident else (n_steps, m_steps),
        in_specs=[pl.BlockSpec((tm, kdim), lambda o, i: (tile(o, i)[0], 0)), b_spec] + [ANY] * n_in,
        out_specs=[pl.BlockSpec((tm, tn), tile)] + [ANY] * n_out,
        out_shape=[jax.ShapeDtypeStruct((m, n), out_dtype)] + (rider.out_shape if rider else []),
        scratch_shapes=rider.scratch if rider else [],
        compiler_params=_params(("arbitrary", "arbitrary") if rider else ("parallel", "parallel")),
    )(a, b, *rider_inputs)
    return out if rider else out[0]


HALO = 16


def _conv_specs(tm):
    def col(j):
        return pl.BlockSpec((tm, D_CONV), lambda i: (i, j))

    def prev(j):
        return pl.BlockSpec((HALO, D_CONV), lambda i: (jnp.maximum(i * (tm // HALO) - 1, 0), j))

    return [col(0), col(1), col(2), col(3), prev(0), prev(2)]


def _conv_y(xc_ref, bc_ref, cc_ref, zc_ref, xp_ref, cp_ref, w_ref, first):
    uc = cc_ref[...].astype(F32) * xc_ref[...].astype(F32)
    up = jnp.where(first, 0.0, cp_ref[...].astype(F32) * xp_ref[...].astype(F32))
    full = jnp.concatenate([up, uc], axis=0)
    u1 = pltpu.roll(full, 1, 0)[HALO:]
    u2 = pltpu.roll(full, 2, 0)[HALO:]
    w = w_ref[...]
    conv = w[0:1] * u2 + w[1:2] * u1 + w[2:3] * uc
    z = zc_ref[...].astype(F32)
    return bc_ref[...].astype(F32) * conv * (z * _sigmoid(z))


def _conv_bwd(u, dyc, conv_w):
    s = u.shape[0]
    tm = min(TM_ELEM, s)
    cb = D_CONV
    nt = s // tm

    def body(xc_ref, bc_ref, cc_ref, zc_ref, xp_ref, cp_ref, bn_ref, zn_ref, dy_ref, dyn_ref, w_ref, du_ref, dw_ref):
        i = pl.program_id(0)
        xc, cc = xc_ref[...].astype(F32), cc_ref[...].astype(F32)
        bc, z = bc_ref[...].astype(F32), zc_ref[...].astype(F32)
        uc = cc * xc
        up = jnp.where(i == 0, 0.0, cp_ref[...].astype(F32) * xp_ref[...].astype(F32))
        full = jnp.concatenate([up, uc], axis=0)
        u1 = pltpu.roll(full, 1, 0)[HALO:]
        u2 = pltpu.roll(full, 2, 0)[HALO:]
        w = w_ref[...]
        conv = w[0:1] * u2 + w[1:2] * u1 + w[2:3] * uc
        sg = _sigmoid(z)
        sz = z * sg
        dy = dy_ref[...].astype(F32)
        dconv = dy * bc * sz
        zn = zn_ref[...].astype(F32)
        dnext = dyn_ref[...].astype(F32) * bn_ref[...].astype(F32) * (zn * _sigmoid(zn))
        dnext = jnp.where(i == nt - 1, 0.0, dnext)
        fullb = jnp.concatenate([dconv, dnext], axis=0)
        nb = tm + HALO
        d1 = pltpu.roll(fullb, nb - 1, 0)[:tm]
        d2 = pltpu.roll(fullb, nb - 2, 0)[:tm]
        duc = w[2:3] * dconv + w[1:2] * d1 + w[0:1] * d2
        dzc = dy * bc * conv * (sg * (1.0 + z * (1.0 - sg)))
        du_ref[...] = jnp.concatenate([duc * cc, dy * conv * sz, duc * xc, dzc], axis=1).astype(BF16)
        dw = jnp.concatenate([jnp.sum(dconv * u2, axis=0, keepdims=True), jnp.sum(dconv * u1, axis=0, keepdims=True),
                              jnp.sum(dconv * uc, axis=0, keepdims=True), jnp.zeros((5, cb), F32)], axis=0)

        @pl.when(i == 0)
        def _():
            dw_ref[...] = dw

        @pl.when(i > 0)
        def _():
            dw_ref[...] += dw

    def col(j):
        return pl.BlockSpec((tm, cb), lambda i: (i, j))

    def prev(j):
        return pl.BlockSpec((HALO, cb), lambda i: (jnp.maximum(i * (tm // HALO) - 1, 0), j))

    def nxt(j):
        return pl.BlockSpec((HALO, cb), lambda i: (jnp.minimum((i + 1) * (tm // HALO), s // HALO - 1), j))

    return pl.pallas_call(
        body, name="conv_bwd", grid=(nt,),
        in_specs=[col(0), col(1), col(2), col(3), prev(0), prev(2), nxt(1), nxt(3), col(0), nxt(0),
                  pl.BlockSpec((3, cb), lambda i: (0, 0))],
        out_specs=[pl.BlockSpec((tm, 4 * cb), lambda i: (i, 0)), pl.BlockSpec((8, cb), lambda i: (0, 0))],
        out_shape=[jax.ShapeDtypeStruct((s, U_COLS), BF16), jax.ShapeDtypeStruct((8, cb), F32)],
        compiler_params=_params(("arbitrary",)),
    )(u, u, u, u, u, u, u, u, dyc, dyc, conv_w)


def _qkv_specs(tm):
    return [pl.BlockSpec((tm, Q_LORA), lambda i: (i, U_CQ // Q_LORA)),
            pl.BlockSpec((tm, KV_LORA), lambda i: (i, U_CKV // KV_LORA)),
            pl.BlockSpec((tm, LANES), lambda i: (i, U_KR // LANES)),
            pl.BlockSpec((tm, LANES), lambda i: (i, 0)), pl.BlockSpec((tm, LANES), lambda i: (i, 0))]


def _full(shape):
    return pl.BlockSpec(shape, lambda i: (0,) * len(shape))


def _k_rope_lanes(blk):
    lane = lax.broadcasted_iota(jnp.int32, blk.shape, 1)
    return jnp.where(lane < QK_ROPE, blk, 0.0)


def _qkv_fwd(u, cos, sin, wq, wkv, qag, kvag, qg, kg):
    s = u.shape[0]
    tm = min(TM_ELEM, s)

    def body(cq_ref, ckv_ref, kr_ref, cos_ref, sin_ref, wq_ref, wkv_ref, qag_ref, kvag_ref, qg_ref, kg_ref,
             q_ref, k_ref, v_ref):
        cq = cq_ref[...].astype(F32)
        cqn = (cq * lax.rsqrt(jnp.mean(cq * cq, axis=-1, keepdims=True) + EPS) * qag_ref[...]).astype(BF16)
        ckv = ckv_ref[...].astype(F32)
        ckvn = (ckv * lax.rsqrt(jnp.mean(ckv * ckv, axis=-1, keepdims=True) + EPS) * kvag_ref[...]).astype(BF16)
        kr = _k_rope_lanes(kr_ref[...].astype(F32))
        cosv, sinv, qgv, kgv = cos_ref[...], sin_ref[...], qg_ref[...], kg_ref[...]
        ss_r = jnp.sum(kr * kr, axis=-1, keepdims=True)
        krr = _rope(kr * kgv[:, QK_NOPE:], cosv, sinv)
        qf = _dot(cqn, wq_ref[...])
        kvf = _dot(ckvn, wkv_ref[...])
        heads = range(N_HEADS)
        qh = [qf[:, QK_PAD * h:QK_PAD * (h + 1)] for h in heads]
        kn = [kvf[:, 2 * V_HEAD * h:2 * V_HEAD * h + QK_NOPE] for h in heads]
        rq = [lax.rsqrt(jnp.sum(qh[h] * qh[h], axis=-1, keepdims=True) * (1.0 / QK_HEAD) + EPS) for h in heads]
        rk = [lax.rsqrt((jnp.sum(kn[h] * kn[h], axis=-1, keepdims=True) + ss_r) * (1.0 / QK_HEAD) + EPS) for h in heads]
        for h in heads:
            qn = qh[h] * rq[h] * qgv
            qo = jnp.concatenate([qn[:, :QK_NOPE], _rope(qn[:, QK_NOPE:], cosv, sinv)], axis=1) * (SCALE * LOG2E)
            q_ref[h] = qo.astype(BF16)
            vh = kvf[:, 2 * V_HEAD * h + QK_NOPE:2 * V_HEAD * (h + 1)]
            k_ref[h] = jnp.concatenate([kn[h] * kgv[:, :QK_NOPE] * rk[h], krr * rk[h]], axis=1).astype(BF16)
            v_ref[h] = jnp.concatenate([vh, jnp.ones_like(vh)], axis=1).astype(BF16)

    return pl.pallas_call(
        body, name="qkv_fwd", grid=(s // tm,),
        in_specs=_qkv_specs(tm) + [_full((Q_LORA, N_HEADS * QK_PAD)), _full((KV_LORA, 2 * D_ATTN)),
                                   _full((1, Q_LORA)), _full((1, KV_LORA)), _full((1, QK_PAD)), _full((1, QK_PAD))],
        out_specs=[pl.BlockSpec((N_HEADS, tm, QK_PAD), lambda i: (0, i, 0)),
                   pl.BlockSpec((N_HEADS, tm, QK_PAD), lambda i: (0, i, 0)),
                   pl.BlockSpec((N_HEADS, tm, 2 * V_HEAD), lambda i: (0, i, 0))],
        out_shape=[jax.ShapeDtypeStruct((N_HEADS, s, QK_PAD), BF16), jax.ShapeDtypeStruct((N_HEADS, s, QK_PAD), BF16),
                   jax.ShapeDtypeStruct((N_HEADS, s, 2 * V_HEAD), BF16)],
        compiler_params=_params(("parallel",)),
    )(u, u, u, cos, sin, wq, wkv, qag, kvag, qg, kg)


def _qkv_bwd(u, cos, sin, dq, dk, dv, dza, wq, wkv, qag, kvag, qg, kg, du):
    s = u.shape[0]
    tm = min(TM_QKV_BWD, s)
    nt = s // tm

    def body(cq_ref, ckv_ref, kr_ref, cos_ref, sin_ref, dq_ref, dk_ref, dv_ref, dza_ref, wq_ref, wkv_ref, qag_ref,
             kvag_ref, qg_ref, kg_ref, du_in, du_ref, dwq_ref, dwkv_ref, dqag_ref, dkvag_ref, dqg_ref, dkg_ref,
             dwq_acc, dwkv_acc):
        del du_in
        i = pl.program_id(0)

        @pl.when(i == 0)
        def _():
            dwq_acc[...] = jnp.zeros_like(dwq_acc)
            dwkv_acc[...] = jnp.zeros_like(dwkv_acc)

        cq = cq_ref[...].astype(F32)
        rqa = lax.rsqrt(jnp.mean(cq * cq, axis=-1, keepdims=True) + EPS)
        xq = cq * rqa
        qagv = qag_ref[...]
        cqn = (xq * qagv).astype(BF16)
        ckv = ckv_ref[...].astype(F32)
        rkva = lax.rsqrt(jnp.mean(ckv * ckv, axis=-1, keepdims=True) + EPS)
        xkv = ckv * rkva
        kvagv = kvag_ref[...]
        ckvn = (xkv * kvagv).astype(BF16)
        kr = _k_rope_lanes(kr_ref[...].astype(F32))
        cosv, sinv, qgv, kgv = cos_ref[...], sin_ref[...], qg_ref[...], kg_ref[...]
        ss_r = jnp.sum(kr * kr, axis=-1, keepdims=True)
        dqg = jnp.zeros((1, QK_PAD), F32)
        dkg = jnp.zeros((1, QK_PAD), F32)
        dkr = jnp.zeros((tm, LANES), F32)
        qf = _dot(cqn, wq_ref[...])
        kvf = _dot(ckvn, wkv_ref[...])
        heads = range(N_HEADS)
        qh = [qf[:, QK_PAD * h:QK_PAD * (h + 1)] for h in heads]
        kn = [kvf[:, 2 * V_HEAD * h:2 * V_HEAD * h + QK_NOPE] for h in heads]
        rq = [lax.rsqrt(jnp.sum(qh[h] * qh[h], axis=-1, keepdims=True) * (1.0 / QK_HEAD) + EPS) for h in heads]
        rk = [lax.rsqrt((jnp.sum(kn[h] * kn[h], axis=-1, keepdims=True) + ss_r) * (1.0 / QK_HEAD) + EPS) for h in heads]
        xh = [qh[h] * rq[h] for h in heads]
        xk = [jnp.concatenate([kn[h], kr], axis=1) * rk[h] for h in heads]
        dyq, dyk = [], []
        for h in heads:
            g = dq_ref[h].astype(F32) * SCALE
            dyq.append(jnp.concatenate([g[:, :QK_NOPE], _rope_t(g[:, QK_NOPE:], cosv, sinv)], axis=1))
            gk = dk_ref[h].astype(F32)
            dyk.append(jnp.concatenate([gk[:, :QK_NOPE], _rope_t(gk[:, QK_NOPE:], cosv, sinv)], axis=1))
        for h in heads:
            dqg = dqg + jnp.sum(dyq[h] * xh[h], axis=0, keepdims=True)
            dkg = dkg + jnp.sum(dyk[h] * xk[h], axis=0, keepdims=True)
        gdy = [dyq[h] * qgv for h in heads]
        gdyk = [dyk[h] * kgv for h in heads]
        tq_ = [jnp.sum(gdy[h] * xh[h], axis=-1, keepdims=True) * (1.0 / QK_HEAD) for h in heads]
        tk_ = [jnp.sum(gdyk[h] * xk[h], axis=-1, keepdims=True) * (1.0 / QK_HEAD) for h in heads]
        dqf = [(rq[h] * (gdy[h] - xh[h] * tq_[h])).astype(BF16) for h in heads]
        dkvf = []
        for h in heads:
            dxk = rk[h] * (gdyk[h] - xk[h] * tk_[h])
            dkr = dkr + dxk[:, QK_NOPE:]
            dkvf += [dxk[:, :QK_NOPE].astype(BF16), dv_ref[h]]
        dqf_b, dkvf_b = jnp.concatenate(dqf, axis=1), jnp.concatenate(dkvf, axis=1)
        dwq_acc[...] += _dot_tn(cqn, dqf_b)
        dwkv_acc[...] += _dot_tn(ckvn, dkvf_b)
        dcqn = _dot_nt(dqf_b, wq_ref[...])
        dckvn = _dot_nt(dkvf_b, wkv_ref[...])
        dqag = jnp.sum(dcqn * xq, axis=0, keepdims=True)
        dkvag = jnp.sum(dckvn * xkv, axis=0, keepdims=True)
        gq = dcqn * qagv
        dcq = rqa * (gq - xq * jnp.mean(gq * xq, axis=-1, keepdims=True))
        gkv = dckvn * kvagv
        dckv = rkva * (gkv - xkv * jnp.mean(gkv * xkv, axis=-1, keepdims=True))
        win = pltpu.roll(jnp.concatenate([dza_ref[...].astype(F32), jnp.zeros((tm, LANES), F32)], axis=1), QK_ROPE, 1)
        win = win + jnp.concatenate([dkr, jnp.zeros((tm, D_ATTN), F32)], axis=1)
        du_ref[...] = jnp.concatenate([dcq, dckv, win, jnp.zeros((tm, U_TAIL - ZA_LO - ZA_WIN), F32)], axis=1).astype(BF16)

        @pl.when(i == 0)
        def _():
            dqag_ref[...] = dqag
            dkvag_ref[...] = dkvag
            dqg_ref[...] = dqg
            dkg_ref[...] = dkg

        @pl.when(i > 0)
        def _():
            dqag_ref[...] += dqag
            dkvag_ref[...] += dkvag
            dqg_ref[...] += dqg
            dkg_ref[...] += dkg

        @pl.when(i == nt - 1)
        def _():
            dwq_ref[...] = dwq_acc[...].astype(BF16)
            dwkv_ref[...] = dwkv_acc[...].astype(BF16)

    head = lambda w: pl.BlockSpec((N_HEADS, tm, w), lambda i: (0, i, 0))
    wq_shape, wkv_shape = (Q_LORA, N_HEADS * QK_PAD), (KV_LORA, 2 * D_ATTN)
    return pl.pallas_call(
        body, name="qkv_bwd", grid=(nt,),
        in_specs=_qkv_specs(tm) + [head(QK_PAD), head(QK_PAD), head(V_HEAD), pl.BlockSpec((tm, D_ATTN), lambda i: (i, 0)),
                                   _full(wq_shape), _full(wkv_shape), _full((1, Q_LORA)), _full((1, KV_LORA)),
                                   _full((1, QK_PAD)), _full((1, QK_PAD)), ANY],
        out_specs=[pl.BlockSpec((tm, U_TAIL), lambda i: (i, U_COLS // U_TAIL - 1)), _full(wq_shape), _full(wkv_shape),
                   _full((1, Q_LORA)), _full((1, KV_LORA)), _full((1, QK_PAD)), _full((1, QK_PAD))],
        out_shape=[jax.ShapeDtypeStruct(du.shape, du.dtype), jax.ShapeDtypeStruct(wq_shape, BF16),
                   jax.ShapeDtypeStruct(wkv_shape, BF16), jax.ShapeDtypeStruct((1, Q_LORA), F32),
                   jax.ShapeDtypeStruct((1, KV_LORA), F32), jax.ShapeDtypeStruct((1, QK_PAD), F32),
                   jax.ShapeDtypeStruct((1, QK_PAD), F32)],
        scratch_shapes=[pltpu.VMEM(wq_shape, F32), pltpu.VMEM(wkv_shape, F32)],
        input_output_aliases={15: 0}, compiler_params=_params(("arbitrary",)),
    )(u, u, u, cos, sin, dq, dk, dv, dza, wq, wkv, qag, kvag, qg, kg, du)


def _flash_fwd(q, k, v):
    nh, s, _ = q.shape
    tq = min(TQ, s)
    nkv = KV_SPLIT
    tk = tq // nkv
    nq = s // tq
    nch = Q_CHAINS
    tc = tq // nch

    def body(q_ref, k_ref, v_ref, o_ref, lse_ref):
        i = pl.program_id(1)
        chains = [q_ref[0, r * tc:(r + 1) * tc, :] for r in range(nch)]

        def unit(r, j, carry, shift=None):
            m, acc = carry
            rows = pl.ds(pl.multiple_of(j * tk, tk), tk)
            sc = _dot_nt(chains[r], k_ref[0, rows, :])
            if shift is not None:
                qi = lax.broadcasted_iota(jnp.int32, sc.shape, 0)
                ki = lax.broadcasted_iota(jnp.int32, sc.shape, 1) + shift
                sc = jnp.where(ki <= qi, sc, NEG)
            m_new = jnp.maximum(m, jnp.max(sc, axis=-1, keepdims=True))
            p = jnp.exp2(sc - m_new).astype(BF16)
            return m_new, jnp.exp2(m - m_new) * acc + _dot(p, v_ref[0, rows, :])

        def trip(p, carry):
            for b in range(nkv):
                carry = tuple(unit(r, nkv * p + b, cr) for r, cr in enumerate(carry))
            return carry

        init = (jnp.full((tc, 1), NEG, F32), jnp.zeros((tc, 2 * V_HEAD), F32))
        carry = list(lax.fori_loop(0, i, trip, (init,) * nch))
        for b in range(nkv):
            for r in range(nch):
                shift = b * tk - r * tc
                if shift < tc:
                    carry[r] = unit(r, nkv * i + b, carry[r], None if shift + tk - 1 <= 0 else shift)
        for r, (m, acc) in enumerate(carry):
            l = acc[:, V_HEAD:]
            o_ref[r * tc:(r + 1) * tc, :] = (acc[:, :V_HEAD] / l).astype(BF16)
            lse = m + jnp.log(l[:, 0:1]) * LOG2E
            lse_ref[0, :, r * tc:(r + 1) * tc] = jnp.broadcast_to(lse, (tc, LANES)).T[0:1, :]

    return pl.pallas_call(
        body, name="flash_fwd", grid=(nh, nq),
        in_specs=[pl.BlockSpec((1, tq, QK_PAD), lambda h, i: (h, i, 0)),
                  pl.BlockSpec((1, s, QK_PAD), lambda h, i: (h, 0, 0)),
                  pl.BlockSpec((1, s, 2 * V_HEAD), lambda h, i: (h, 0, 0))],
        out_specs=[pl.BlockSpec((tq, V_HEAD), lambda h, i: (i, h)), pl.BlockSpec((1, 1, tq), lambda h, i: (h, 0, i))],
        out_shape=[jax.ShapeDtypeStruct((s, nh * V_HEAD), BF16), jax.ShapeDtypeStruct((nh, 1, s), F32)],
        compiler_params=_params(("parallel", "arbitrary")),
    )(q, k, v)


def _flash_bwd(q, k, v, do, lse, delta):
    nh, s, _ = q.shape
    tq = min(TQ, s)
    nq = s // tq

    def body(q_ref, k_ref, v_ref, do_ref, lse_ref, dl_ref, dq_ref, dk_ref, dv_ref, dq_acc):
        j = pl.program_id(1)

        @pl.when(j == 0)
        def _():
            dq_acc[...] = jnp.zeros_like(dq_acc)

        kj, vj = k_ref[0], v_ref[0]

        def block(kk, vv, qq, dd, lse, dl, masked):
            st = _dot_nt(kk, qq)
            pt = jnp.exp2(st - lse)
            if masked:
                ki = lax.broadcasted_iota(jnp.int32, st.shape, 0)
                qx = lax.broadcasted_iota(jnp.int32, st.shape, 1)
                pt = jnp.where(ki <= qx, pt, 0.0)
            ddv = _dot(pt.astype(BF16), dd)
            dst = (pt * (_dot_nt(vv, dd) - dl)).astype(BF16)
            ddk = _dot(dst, qq)
            return ddk, ddv, _dot_tn(dst, kk)

        def step(i, carry):
            dk, dv = carry
            rows = pl.ds(pl.multiple_of(i * tq, tq), tq)
            ddk, ddv, ddq = block(kj, vj, q_ref[0, rows, :], do_ref[rows, :], lse_ref[0, pl.ds(i, 1), :],
                                  dl_ref[0, pl.ds(i, 1), :], False)
            dq_acc[rows, :] += ddq
            return dk + ddk, dv + ddv

        th = tq // 2
        lse_j, dl_j = lse_ref[0, pl.ds(j, 1), :], dl_ref[0, pl.ds(j, 1), :]
        parts = []
        for kh, qh, masked in ((0, 0, True), (0, 1, False), (1, 1, True)):
            rows = pl.ds(pl.multiple_of(j * tq + qh * th, th), th)
            ks, qs = slice(kh * th, (kh + 1) * th), slice(qh * th, (qh + 1) * th)
            ddk, ddv, ddq = block(kj[ks], vj[ks], q_ref[0, rows, :], do_ref[rows, :], lse_j[:, qs], dl_j[:, qs], masked)
            dq_acc[rows, :] += ddq
            parts.append((ddk, ddv))
        carry = (jnp.concatenate([parts[0][0] + parts[1][0], parts[2][0]], axis=0),
                 jnp.concatenate([parts[0][1] + parts[1][1], parts[2][1]], axis=0))
        dk, dv = lax.fori_loop(j + 1, nq, step, carry)
        dk_ref[0] = (dk * LN2).astype(BF16)
        dv_ref[0] = dv.astype(BF16)

        @pl.when(j == nq - 1)
        def _():
            dq_ref[0] = dq_acc[...].astype(BF16)

    return pl.pallas_call(
        body, name="flash_bwd", grid=(nh, nq),
        in_specs=[pl.BlockSpec((1, s, QK_PAD), lambda h, j: (h, 0, 0)),
                  pl.BlockSpec((1, tq, QK_PAD), lambda h, j: (h, j, 0)),
                  pl.BlockSpec((1, tq, V_HEAD), lambda h, j: (h, j, 0)),
                  pl.BlockSpec((s, V_HEAD), lambda h, j: (0, h)),
                  pl.BlockSpec((1, nq, tq), lambda h, j: (h, 0, 0)),
                  pl.BlockSpec((1, nq, tq), lambda h, j: (h, 0, 0))],
        out_specs=[pl.BlockSpec((1, s, QK_PAD), lambda h, j: (h, 0, 0)),
                   pl.BlockSpec((1, tq, QK_PAD), lambda h, j: (h, j, 0)),
                   pl.BlockSpec((1, tq, V_HEAD), lambda h, j: (h, j, 0))],
        out_shape=[jax.ShapeDtypeStruct((nh, s, QK_PAD), BF16), jax.ShapeDtypeStruct((nh, s, QK_PAD), BF16),
                   jax.ShapeDtypeStruct((nh, s, V_HEAD), BF16)],
        scratch_shapes=[pltpu.VMEM((s, QK_PAD), F32)],
        compiler_params=_params(("parallel", "arbitrary")),
    )(q, k, v, do, lse, delta)


def _tail(x, target, o, u, mod, w_out, conv_w):
    s, d = x.shape
    tm = min(TM_ELEM, s)

    def body(x_ref, t_ref, o_ref, za_ref, mod_ref, w_ref, xc_ref, bc_ref, cc_ref, zc_ref, xp_ref, cp_ref, cw_ref,
             gx_ref, dy_ref, ycat_ref, dyc_ref, do_ref, du_ref, delta_ref, dgate_ref, loss_ref):
        i = pl.program_id(0)
        za = pltpu.roll(za_ref[:, ZA_LO:ZA_LO + ZA_WIN].astype(F32), ZA_WIN - QK_ROPE, 1)[:, :D_ATTN]
        ov = o_ref[...].astype(F32)
        sg = _sigmoid(za)
        sl = za * sg
        ya = ov * sl
        y = _dot(ya.astype(BF16), w_ref[D_CONV:, :])
        yc = _conv_y(xc_ref, bc_ref, cc_ref, zc_ref, xp_ref, cp_ref, cw_ref, i == 0)
        y = y + _dot(yc.astype(BF16), w_ref[:D_CONV, :])
        ycat_ref[...] = jnp.concatenate([yc.T, ya.T], axis=0).astype(BF16)
        gate = mod_ref[:, 2 * d:3 * d]
        e = x_ref[...] + gate * y - t_ref[...]
        dout = e * (1.0 / d)
        gx_ref[...] = dout
        dy = (dout * gate).astype(BF16)
        dy_ref[...] = dy
        dycat = _dot_nt(dy, w_ref[...])
        dyc_ref[...] = dycat[:, :D_CONV].astype(BF16)
        dya = dycat[:, D_CONV:]
        dov = dya * sl
        do_ref[...] = dov.astype(BF16)
        du_ref[...] = (dya * ov * (sg * (1.0 + za * (1.0 - sg)))).astype(BF16)
        prod_t = (dov * ov).T
        for h in range(N_HEADS):
            delta_ref[h] = jnp.sum(prod_t[V_HEAD * h:V_HEAD * (h + 1), :], axis=0, keepdims=True)
        dgate = jnp.sum(dout * y, axis=0, keepdims=True)
        part = jnp.sum(jnp.sum(e * e, axis=0, keepdims=True), axis=1, keepdims=True) * (0.5 / d)
        part = jnp.broadcast_to(part, (1, LANES))

        @pl.when(i == 0)
        def _():
            dgate_ref[...] = dgate
            loss_ref[...] = part

        @pl.when(i > 0)
        def _():
            dgate_ref[...] += dgate
            loss_ref[...] += part

    tok = lambda w: pl.BlockSpec((tm, w), lambda i: (i, 0))
    return pl.pallas_call(
        body, name="tail", grid=(s // tm,),
        in_specs=[tok(d), tok(d), tok(D_ATTN), pl.BlockSpec((tm, U_TAIL), lambda i: (i, U_COLS // U_TAIL - 1)),
                  _full((1, 3 * d)), _full((d, d))] + _conv_specs(tm) + [_full((3, D_CONV))],
        out_specs=[tok(d), tok(d), pl.BlockSpec((d, tm), lambda i: (0, i)), tok(D_CONV), tok(D_ATTN), tok(D_ATTN),
                   pl.BlockSpec((N_HEADS, 1, tm), lambda i: (0, 0, i)), _full((1, d)), _full((1, LANES))],
        out_shape=[jax.ShapeDtypeStruct((s, d), F32), jax.ShapeDtypeStruct((s, d), BF16),
                   jax.ShapeDtypeStruct((d, s), BF16), jax.ShapeDtypeStruct((s, D_CONV), BF16),
                   jax.ShapeDtypeStruct((s, D_ATTN), BF16), jax.ShapeDtypeStruct((s, D_ATTN), BF16),
                   jax.ShapeDtypeStruct((N_HEADS, 1, s), F32), jax.ShapeDtypeStruct((1, d), F32),
                   jax.ShapeDtypeStruct((1, LANES), F32)],
        compiler_params=_params(("arbitrary",)),
    )(x, target, o, u, mod, w_out, u, u, u, u, u, u, conv_w)


def _norm_bwd(x, dh, gx1, norm_g, mod):
    s, d = x.shape
    tm = min(TM_MM, s)

    def body(x_ref, dh_ref, gx_ref, g_ref, mod_ref, o_ref, dshift_ref, dscale_ref, dg_ref):
        i = pl.program_id(0)
        gv, sc1 = g_ref[...], 1.0 + mod_ref[:, d:2 * d]
        gsc = gv * sc1
        half = NORM_ROWS // 2

        def group(c, acc):
            a_dh, a_dhxn = acc
            ks = range(NORM_GROUP)
            rows = [pl.ds(pl.multiple_of((c * NORM_GROUP + k) * NORM_ROWS, NORM_ROWS), NORM_ROWS) for k in ks]
            xv = [x_ref[rows[k], :] for k in ks]
            dhv = [dh_ref[rows[k], :].astype(F32) for k in ks]
            r = [lax.rsqrt(jnp.mean(xv[k] * xv[k], axis=-1, keepdims=True) + EPS) for k in ks]
            xn = [xv[k] * r[k] for k in ks]
            dxn = [dhv[k] * gsc for k in ks]
            t = [jnp.mean(dxn[k] * xn[k], axis=-1, keepdims=True) for k in ks]
            for k in ks:
                o_ref[rows[k], :] = gx_ref[rows[k], :] + r[k] * (dxn[k] - xn[k] * t[k])
                dhxn = dhv[k] * xn[k]
                a_dh = a_dh + dhv[k][:half] + dhv[k][half:]
                a_dhxn = a_dhxn + dhxn[:half] + dhxn[half:]
            return a_dh, a_dhxn

        zero = jnp.zeros((half, d), F32)
        a_dh, a_dhxn = lax.fori_loop(0, tm // (NORM_ROWS * NORM_GROUP), group, (zero, zero))
        dshift = jnp.sum(a_dh, axis=0, keepdims=True)
        s_dhxn = jnp.sum(a_dhxn, axis=0, keepdims=True)
        dscale, dg = s_dhxn * gv, s_dhxn * sc1

        @pl.when(i == 0)
        def _():
            dshift_ref[...] = dshift
            dscale_ref[...] = dscale
            dg_ref[...] = dg

        @pl.when(i > 0)
        def _():
            dshift_ref[...] += dshift
            dscale_ref[...] += dscale
            dg_ref[...] += dg

    tok = pl.BlockSpec((tm, d), lambda i: (i, 0))
    row = jax.ShapeDtypeStruct((1, d), F32)
    return pl.pallas_call(
        body, name="norm_bwd", grid=(s // tm,),
        in_specs=[tok, tok, tok, _full((1, d)), _full((1, 3 * d))],
        out_specs=[tok, _full((1, d)), _full((1, d)), _full((1, d))],
        out_shape=[jax.ShapeDtypeStruct((s, d), F32), row, row, row],
        compiler_params=_params(("arbitrary",)),
    )(x, dh, gx1, norm_g, mod)


def _adamw(w, g, m, v, name):
    rows, cols = w.shape
    tr = 256 if rows % 256 == 0 else rows
    tc = 512 if (rows > 256 and tr == rows and cols % 512 == 0) else cols

    def body(w_ref, g_ref, m_ref, v_ref, d_ref, nm_ref, nv_ref):
        gv = g_ref[...]
        nm = ADAM_B1 * m_ref[...] + (1.0 - ADAM_B1) * gv
        nv = ADAM_B2 * v_ref[...] + (1.0 - ADAM_B2) * (gv * gv)
        m_hat = nm / (1.0 - ADAM_B1 ** ADAM_STEP)
        v_hat = nv / (1.0 - ADAM_B2 ** ADAM_STEP)
        d_ref[...] = -ADAM_LR * (m_hat / (jnp.sqrt(v_hat) + ADAM_EPS) + ADAM_WD * w_ref[...])
        nm_ref[...] = nm
        nv_ref[...] = nv

    spec = pl.BlockSpec((tr, tc), lambda i, j: (i, j))
    shape = jax.ShapeDtypeStruct((rows, cols), F32)
    return pl.pallas_call(
        body, name=name, grid=(rows // tr, cols // tc), in_specs=[spec] * 4, out_specs=[spec] * 3, out_shape=[shape] * 3,
        compiler_params=_params(("parallel", "parallel")),
    )(w, g, m, v)


def _pad_cols(a, n):
    return jnp.pad(a, ((0, 0), (0, n - a.shape[1])))


def kernel(x, c, positions, ada_w, ada_b, norm_g, w_in, conv_w, q_a_g, w_q_b, kv_a_g, w_kv_b, q_g, k_g, w_out, loss_target, m_ada_w, m_ada_b, m_norm_g, m_w_in, m_conv_w, m_q_a_g, m_w_q_b, m_kv_a_g, m_w_kv_b, m_q_g, m_k_g, m_w_out, v_ada_w, v_ada_b, v_norm_g, v_w_in, v_conv_w, v_q_a_g, v_w_q_b, v_kv_a_g, v_w_kv_b, v_q_g, v_k_g, v_w_out):
    me = _my_index()
    s = x.shape[1]
    nq = s // min(TQ, s)
    x2, tgt = x[0], loss_target[0]
    w_in_l, w_q_l, w_kv_l, w_out_l, conv_l, ada_w_l = w_in[0], w_q_b[0], w_kv_b[0], w_out[0], conv_w[0], ada_w[0]
    ada_cols = ada_w_l.shape[1]

    small = jnp.concatenate([c.reshape(-1, LANES), conv_l.reshape(-1, LANES), jnp.zeros((5, LANES), F32)], axis=0)
    (small_g,) = _all_gather([small], "gather_c")
    c_all = small_g[:, :D_MODEL // LANES].reshape(N_DEV, D_MODEL)
    conv_g = small_g[:, D_MODEL // LANES:D_MODEL // LANES + 3].transpose(1, 0, 2).reshape(3, D_CONV)

    ada_b_l = lax.dynamic_slice(ada_b, (0, me * ada_cols), (1, ada_cols))
    mod_cols = _ada_mod(jnp.pad(c_all, ((0, 8), (0, 0))), ada_w_l, ada_b_l)[:N_DEV]
    (mod_g,) = _all_gather([mod_cols], "gather_mod")
    mod = lax.dynamic_index_in_dim(mod_g, me, axis=1, keepdims=False).reshape(1, 3 * D_MODEL)

    half = jnp.arange(0, QK_ROPE, 2, dtype=F32) / QK_ROPE
    inv_freq = ROPE_BASE ** (-half)
    zeros64 = jnp.zeros((LANES - QK_ROPE,), F32)
    invf = jnp.concatenate([inv_freq, inv_freq, zeros64]).reshape(1, LANES)
    sign = jnp.concatenate([-jnp.ones((32,), F32), jnp.ones((32,), F32), zeros64]).reshape(1, LANES)
    qg_p, kg_p = _pad_cols(q_g, QK_PAD), _pad_cols(k_g, QK_PAD)

    my_off = ((CW * me) % LANES).astype(jnp.int32)
    win = [_expand_w_in(w_in_l.T, my_off.reshape(1))]
    h, h_t, cos, sin, win_g = _norm_mod(x2, norm_g, mod, positions.reshape(s, 1), invf, sign, _Gather(win, relay=True, parts=4), win)
    w_in_p = _merge_w_in(win_g)
    rest = [_pad_wq(w_q_l.T), w_kv_l.astype(BF16), w_out_l.astype(BF16)]
    u, wq_g, wkv_g, w_out_g = _matmul(h, w_in_p, nt=False, out_dtype=BF16, tm=2 * TM_MM, tn=1024, name="in_proj",
                                      rider=_Gather(rest), rider_inputs=rest)
    w_out_g = w_out_g.reshape(D_MODEL, D_MODEL)
    wq_g = wq_g.transpose(1, 0, 2).reshape(Q_LORA, N_HEADS * QK_PAD)
    wkv_g = wkv_g.transpose(1, 0, 2).reshape(KV_LORA, 2 * D_ATTN)
    q, k, v = _qkv_fwd(u, cos, sin, wq_g, wkv_g, q_a_g, kv_a_g, qg_p, kg_p)
    o, lse = _flash_fwd(q, k, v)
    gx1, dy, ycat_t, dyc, do, dza, delta, dgate, loss_row = _tail(x2, tgt, o, u, mod, w_out_g, conv_g)

    dq, dk, dv = _flash_bwd(q, k, v, do, lse.reshape(N_HEADS, nq, s // nq), delta.reshape(N_HEADS, nq, s // nq))
    du, dconv = _conv_bwd(u, dyc, conv_g)
    du, dwq, dwkv, dqag, dkvag, dqg, dkg = _qkv_bwd(u, cos, sin, dq, dk, dv, dza, wq_g, wkv_g, q_a_g, kv_a_g, qg_p, kg_p, du)
    dwq = dwq.reshape(Q_LORA, N_HEADS, QK_PAD).transpose(1, 0, 2)
    dwkv = dwkv.reshape(KV_LORA, N_HEADS, 2 * V_HEAD).transpose(1, 0, 2)
    dw_in = _matmul(h_t, du, nt=False, out_dtype=BF16, tm=TM_MM, tn=768, name="dw_in")
    first = [dw_in, dwq, dwkv]
    dw_out, r_in, r_q, r_kv = _matmul(ycat_t, dy, nt=False, out_dtype=BF16, tm=TM_MM, tn=512, name="dw_out",
                                      rider=_SiblingExchange(first, [True, False, False]), rider_inputs=first)
    dw_out = dw_out.reshape(N_DEV, D_MODEL // N_DEV, D_MODEL)
    (r_out,) = _exchange(_SiblingExchange([dw_out], [False]), [dw_out], "rs_sibling_out")
    core = lax.axis_index("c").astype(jnp.int32)
    lo_tiles = ((CW * (2 * jnp.arange(4, dtype=jnp.int32) + core)) // LANES).astype(jnp.int32)
    pairs = [_add_window(dw_in, r_in, lo_tiles), _add_pairs(dwq, r_q, core.reshape(1), "rs_add_q"),
             _add_pairs(dwkv, r_kv, core.reshape(1), "rs_add_kv"), _add_pairs(dw_out, r_out, core.reshape(1), "rs_add_out")]
    dh, *quads = _matmul(du, w_in_p, nt=True, out_dtype=BF16, tm=2 * TM_MM, tn=512, name="dh",
                         rider=_ChipExchange(pairs), rider_inputs=pairs, a_resident=True)
    my_chip = 2 * lax.axis_index("x") + lax.axis_index("y")
    written = jnp.where(jnp.arange(4) == my_chip, (jnp.arange(4) + 1) % 4, jnp.arange(4))
    sel = jnp.concatenate([my_chip.reshape(1), written, ((EXP_W - my_off) % EXP_W).reshape(1)]).astype(jnp.int32)
    g_w_in_t = _final_sum(pairs[0], quads[0], sel, "rs_sum_in", unshift=True, keep_t=CW)
    g_w_q_t = _final_sum(pairs[1], quads[1], sel, "rs_sum_q", keep_t=QK_HEAD)
    g_w_kv = _final_sum(pairs[2], quads[2], sel, "rs_sum_kv")
    g_w_out = _final_sum(pairs[3], quads[3], sel, "rs_sum_out")
    grad_x, dshift, dscale, dng = _norm_bwd(x2, dh, gx1, norm_g, mod)

    row = jnp.concatenate([dshift, dscale, dgate, dng, dqag, dkvag, dqg, dkg, dconv[:3].reshape(1, 3 * D_CONV), loss_row], axis=1)
    (rows_g,) = _all_gather([row], "gather_small")
    tot = _sum_leading(rows_g, F32, "sum_small")
    dmod_all = rows_g[:, 0, SM_MOD:SM_NG]
    g_ada_b = tot[:, SM_MOD:SM_NG]
    g_norm_g = tot[:, SM_NG:SM_QAG]
    g_q_a_g = tot[:, SM_QAG:SM_KVAG]
    g_kv_a_g = tot[:, SM_KVAG:SM_QG]
    g_q_g = tot[:, SM_QG:SM_QG + QK_HEAD]
    g_k_g = tot[:, SM_KG:SM_KG + QK_HEAD]
    conv_cols = conv_l.shape[1]
    g_conv = lax.dynamic_slice(tot[:, SM_CONV:SM_LOSS].reshape(3, D_CONV), (0, me * conv_cols), (3, conv_cols))
    loss = tot[0, SM_LOSS]
    dmod_my = lax.dynamic_slice(dmod_all, (0, me * ada_cols), (N_DEV, ada_cols))
    g_ada_w = _ada_w_grad(c_all.T, dmod_my)

    grads = dict(ada_w=g_ada_w, ada_b=g_ada_b, norm_g=g_norm_g, w_in=g_w_in_t, conv_w=g_conv, q_a_g=g_q_a_g, w_q_b=g_w_q_t,
                 kv_a_g=g_kv_a_g, w_kv_b=g_w_kv, q_g=g_q_g, k_g=g_k_g, w_out=g_w_out)
    weights = dict(ada_w=(ada_w, m_ada_w, v_ada_w), ada_b=(ada_b, m_ada_b, v_ada_b), norm_g=(norm_g, m_norm_g, v_norm_g),
                   w_in=(w_in, m_w_in, v_w_in), conv_w=(conv_w, m_conv_w, v_conv_w), q_a_g=(q_a_g, m_q_a_g, v_q_a_g),
                   w_q_b=(w_q_b, m_w_q_b, v_w_q_b), kv_a_g=(kv_a_g, m_kv_a_g, v_kv_a_g), w_kv_b=(w_kv_b, m_w_kv_b, v_w_kv_b),
                   q_g=(q_g, m_q_g, v_q_g), k_g=(k_g, m_k_g, v_k_g), w_out=(w_out, m_w_out, v_w_out))
    names = list(grads)
    out_g, out_d, out_m, out_v = [], [], [], []
    for n in names:
        w, m, v_ = weights[n]
        shape2 = w.shape[-2:] if w.ndim == 3 else (1, w.shape[-1])
        transposed = n in ("w_in", "w_q_b")
        to2 = (lambda a: a.reshape(shape2).T) if transposed else (lambda a: a.reshape(shape2))
        back = (lambda a: a.T.reshape(w.shape)) if transposed else (lambda a: a.reshape(w.shape))
        g2 = grads[n] if transposed else grads[n].reshape(shape2)
        d2, m2, v2 = _adamw(to2(w), g2, to2(m), to2(v_), "adamw_" + n)
        out_g.append(back(g2))
        out_d.append(back(d2))
        out_m.append(back(m2))
        out_v.append(back(v2))
    return (loss, grad_x.reshape(x.shape), *out_g, *out_d, *out_m, *out_v)
```

```python
import functools
import math

import jax
import jax.numpy as jnp
from jax import lax
from jax.experimental import pallas as pl
from jax.experimental.pallas import tpu as pltpu

F32 = jnp.float32
BF16 = jnp.bfloat16
MESH = pl.DeviceIdType.MESH

D_MODEL = 2048
D_CONV = 1024
N_HEADS = 8
QK_NOPE = 128
QK_ROPE = 64
QK_HEAD = QK_NOPE + QK_ROPE
V_HEAD = 128
D_ATTN = N_HEADS * V_HEAD
Q_LORA = 512
KV_LORA = 256
ROPE_BASE = 10000.0
IN_COLS = 4 * D_CONV + Q_LORA + KV_LORA + QK_ROPE + D_ATTN
EPS = 1e-6
ADAM_LR, ADAM_B1, ADAM_B2, ADAM_EPS, ADAM_WD, ADAM_STEP = 0.001, 0.9, 0.999, 1e-08, 0.01, 10

N_DEV = 8
LANES = 128
QK_PAD = 256
U_COLS = 6144
U_CQ, U_CKV, U_KR, U_ZA = 4096, 4608, 4864, 4928
U_TAIL = 2048
ZA_LO = U_ZA - (U_COLS - U_TAIL) - QK_ROPE
ZA_WIN = D_ATTN + LANES
CW = IN_COLS // 8
EXP_W = 896
W_LO = [(CW * d // 128) * 128 for d in range(8)]
W_OFF = [CW * d - lo for d, lo in enumerate(W_LO)]
SCALE = 1.0 / math.sqrt(QK_HEAD)
LOG2E = 1.4426950408889634
LN2 = 0.6931471805599453
NEG = -1e30
VMEM_LIMIT = 56 * 1024 * 1024

TM_ELEM = 256
TM_QKV_BWD = 256
NORM_ROWS = 16
NORM_GROUP = 4
TM_MM = 512
TQ = 1024
Q_CHAINS = 4
KV_SPLIT = 2

SM_MOD, SM_NG, SM_QAG, SM_KVAG, SM_QG, SM_KG, SM_CONV, SM_LOSS = 0, 6144, 8192, 8704, 8960, 9216, 9472, 12544
SM_COLS = 12672


def _params(sem=None):
    kw = dict(vmem_limit_bytes=VMEM_LIMIT)
    if sem is not None:
        kw["dimension_semantics"] = sem
    return pltpu.CompilerParams(**kw)


def _sigmoid(z):
    return 1.0 / (1.0 + jnp.exp(-z))


def _rot64(x):
    lane = lax.broadcasted_iota(jnp.int32, x.shape, 1)
    return jnp.where(lane < 32, pltpu.roll(x, 96, 1), pltpu.roll(x, 32, 1))


def _rope(x, cos, sin):
    return x * cos + _rot64(x) * sin


def _rope_t(d, cos, sin):
    return d * cos - _rot64(d) * sin


def _dot(a, b):
    return jnp.dot(a, b, preferred_element_type=F32)


def _dot_nt(a, b):
    return lax.dot_general(a, b, (((1,), (1,)), ((), ())), preferred_element_type=F32)


def _dot_tn(a, b):
    return lax.dot_general(a, b, (((0,), (0,)), ((), ())), preferred_element_type=F32)


def _my_index():
    return 4 * lax.axis_index("x") + 2 * lax.axis_index("y") + lax.axis_index("c")


ANY = pl.BlockSpec(memory_space=pl.ANY)


class _Gather:
    def __init__(self, blocks, relay=False, parts=1):
        self.relay = relay
        self.parts = parts
        self.rows = [b.shape[0] // parts for b in blocks]
        self.n = n = len(blocks) * parts
        self.out_shape = [jax.ShapeDtypeStruct((N_DEV,) + b.shape, b.dtype) for b in blocks]
        self.scratch = [pltpu.SemaphoreType.DMA((7 * n,)), pltpu.SemaphoreType.DMA((7 * n,)),
                        pltpu.SemaphoreType.DMA((n,))]

    @staticmethod
    def _places():
        x, y, c = lax.axis_index("x"), lax.axis_index("y"), lax.axis_index("c")
        return (x, y, c), (x, y, 1 - c), [(1 - x, y), (x, 1 - y), (1 - x, 1 - y)]

    def _src(self, ins, a):
        block, part = divmod(a, self.parts)
        return ins[block] if self.parts == 1 else ins[block].at[pl.ds(part * self.rows[block], self.rows[block])]

    def _dst(self, outs, a, place):
        block, part = divmod(a, self.parts)
        ref = outs[block].at[4 * place[0] + 2 * place[1] + place[2]]
        return ref if self.parts == 1 else ref.at[pl.ds(part * self.rows[block], self.rows[block])]

    def _copy(self, outs, sems, a, k, block, to, src=None):
        dst = self._dst(outs, a, block)
        return pltpu.make_async_remote_copy(
            src_ref=dst if src is None else src, dst_ref=dst, send_sem=sems[0].at[7 * a + k],
            recv_sem=sems[1].at[7 * a + k], device_id=to, device_id_type=MESH)

    def _first(self, ins, outs, sems):
        me, sibling, chips = self._places()
        first = []
        for a in range(self.n):
            first.append(self._copy(outs, sems, a, 0, me, sibling, src=self._src(ins, a)))
            first += [self._copy(outs, sems, a, 1 + j, me, (*chip, me[2]), src=self._src(ins, a))
                      for j, chip in enumerate(chips[:2] if self.relay else chips)]
        return first

    def _relays(self, outs, sems):
        if not self.relay:
            return []
        (x, y, c), _, _ = self._places()
        via = (jnp.where(c == 0, 1 - x, x), jnp.where(c == 0, y, 1 - y))
        to = (jnp.where(c == 0, x, 1 - x), jnp.where(c == 0, 1 - y, y))
        return [self._copy(outs, sems, a, 3, (*via, c), (*to, c)) for a in range(self.n)]

    def _passed(self, outs, sems):
        me, sibling, chips = self._places()
        return [self._copy(outs, sems, a, 4 + j, (*chip, me[2]), sibling)
                for a in range(self.n) for j, chip in enumerate(chips)]

    def _mine(self, ins, outs, sems):
        me, _, _ = self._places()
        return [pltpu.make_async_copy(self._src(ins, a), self._dst(outs, a, me), sems[2].at[a]) for a in range(self.n)]

    def start(self, ins, outs, sems):
        for cp in self._mine(ins, outs, sems) + self._first(ins, outs, sems):
            cp.start()

    def forward(self, ins, outs, sems):
        del ins
        me, _, chips = self._places()
        passed, relays = self._passed(outs, sems), self._relays(outs, sems)
        for a in range(self.n):
            for j, chip in enumerate(chips[:2] if self.relay else chips):
                self._copy(outs, sems, a, 1 + j, (*chip, me[2]), me).wait_recv()
                passed[3 * a + j].start()
            if self.relay:
                relays[a].start()
        if self.relay:
            for a in range(self.n):
                self._copy(outs, sems, a, 3, (*chips[2], me[2]), me).wait_recv()
                passed[3 * a + 2].start()

    def finish(self, ins, outs, sems):
        me, sibling, chips = self._places()
        for a in range(self.n):
            self._copy(outs, sems, a, 0, sibling, me).wait_recv()
            for j, chip in enumerate(chips):
                self._copy(outs, sems, a, 4 + j, (*chip, 1 - me[2]), me).wait_recv()
        for cp in self._first(ins, outs, sems) + self._relays(outs, sems) + self._passed(outs, sems):
            cp.wait_send()
        for cp in self._mine(ins, outs, sems):
            cp.wait()


class _ChipExchange:
    def __init__(self, arrays):
        self.n = n = len(arrays)
        self.out_shape = [jax.ShapeDtypeStruct(a.shape, a.dtype) for a in arrays]
        self.scratch = [pltpu.SemaphoreType.DMA((3 * n,)), pltpu.SemaphoreType.DMA((3 * n,))]

    def _copies(self, ins, outs, sems):
        x, y, c = lax.axis_index("x"), lax.axis_index("y"), lax.axis_index("c")
        return [pltpu.make_async_remote_copy(
            src_ref=ins[a].at[2 * px + py], dst_ref=outs[a].at[2 * x + y], send_sem=sems[0].at[3 * a + j],
            recv_sem=sems[1].at[3 * a + j], device_id=(px, py, c), device_id_type=MESH)
            for a in range(self.n) for j, (px, py) in enumerate([(1 - x, y), (x, 1 - y), (1 - x, 1 - y)])]

    def start(self, ins, outs, sems):
        for cp in self._copies(ins, outs, sems):
            cp.start()

    def forward(self, ins, outs, sems):
        pass

    def finish(self, ins, outs, sems):
        for cp in self._copies(ins, outs, sems):
            cp.wait()


def _all_gather(blocks, name):
    n = len(blocks)
    g = _Gather(blocks)

    def body(*refs):
        ins, outs, sems = refs[:n], refs[n:2 * n], refs[2 * n:]
        g.start(ins, outs, sems)
        g.forward(ins, outs, sems)
        g.finish(ins, outs, sems)

    return pl.pallas_call(body, name=name, out_shape=g.out_shape, in_specs=[ANY] * n, out_specs=[ANY] * n,
                          scratch_shapes=g.scratch)(*blocks)


class _SiblingExchange:
    def __init__(self, arrays, windowed):
        self.n = n = len(arrays)
        self.windowed = windowed
        self.out_shape = [jax.ShapeDtypeStruct((4, a.shape[0], EXP_W) if w else (4,) + a.shape[1:], a.dtype)
                          for a, w in zip(arrays, windowed)]
        self.scratch = [pltpu.SemaphoreType.DMA((4 * n,)), pltpu.SemaphoreType.DMA((4 * n,))]

    def _each(self, ins, outs, sems, act):
        x, y, c = lax.axis_index("x"), lax.axis_index("y"), lax.axis_index("c")

        def branch(c_val):
            for k in range(4):
                e = 2 * k + (1 - c_val)
                for a in range(self.n):
                    src = ins[a].at[:, pl.ds(W_LO[e], EXP_W)] if self.windowed[a] else ins[a].at[e]
                    act(pltpu.make_async_remote_copy(
                        src_ref=src, dst_ref=outs[a].at[k], send_sem=sems[0].at[4 * a + k], recv_sem=sems[1].at[4 * a + k],
                        device_id=(x, y, 1 - c), device_id_type=MESH))

        for c_val in (0, 1):
            pl.when(c == c_val)(functools.partial(branch, c_val))

    def start(self, ins, outs, sems):
        self._each(ins, outs, sems, lambda cp: cp.start())

    def forward(self, ins, outs, sems):
        pass

    def finish(self, ins, outs, sems):
        self._each(ins, outs, sems, lambda cp: cp.wait())


def _exchange(rider, arrays, name):
    n = len(arrays)

    def body(*refs):
        ins, outs, sems = refs[:n], refs[n:n + len(rider.out_shape)], refs[n + len(rider.out_shape):]
        rider.start(ins, outs, sems)
        rider.forward(ins, outs, sems)
        rider.finish(ins, outs, sems)

    return pl.pallas_call(body, name=name, out_shape=rider.out_shape, in_specs=[ANY] * n,
                          out_specs=[ANY] * len(rider.out_shape), scratch_shapes=rider.scratch)(*arrays)


def _add_window(dw_in, recv, lo_tiles):
    k, rows, _ = recv.shape

    def body(t_ref, w_ref, r_ref, o_ref):
        del t_ref
        o_ref[0] = (w_ref[...].astype(F32) + r_ref[0].astype(F32)).astype(o_ref.dtype)

    spec = pl.BlockSpec((1, rows, LANES), lambda i, j, t: (i, 0, j))
    grid_spec = pltpu.PrefetchScalarGridSpec(
        num_scalar_prefetch=1, grid=(k, EXP_W // LANES),
        in_specs=[pl.BlockSpec((rows, LANES), lambda i, j, t: (0, t[i] + j)), spec], out_specs=spec)
    return pl.pallas_call(
        body, name="rs_add_in", grid_spec=grid_spec, out_shape=jax.ShapeDtypeStruct(recv.shape, recv.dtype),
        compiler_params=_params(("parallel", "parallel")),
    )(lo_tiles, dw_in, recv)


def _final_sum(p, r, sel, name, unshift=False, keep_t=None):
    _, rows, cols = p.shape
    tr = 512 if rows % 512 == 0 else rows

    def body(sel_ref, p_ref, r0, r1, r2, r3, o_ref):
        own = p_ref[0].astype(F32)
        acc = None
        for k, r_ref in enumerate((r0, r1, r2, r3)):
            term = jnp.where(sel_ref[0] == k, own, r_ref[0].astype(F32))
            acc = term if acc is None else acc + term
        if unshift:
            acc = pltpu.roll(acc, sel_ref[5], 1)
        o_ref[...] = acc if keep_t is None else acc.T[:keep_t]

    def slot(k):
        return pl.BlockSpec((1, tr, cols), lambda i, t: (t[k], i, 0))

    if keep_t is None:
        out_spec, out_shape = pl.BlockSpec((tr, cols), lambda i, t: (i, 0)), (rows, cols)
    else:
        out_spec, out_shape = pl.BlockSpec((keep_t, tr), lambda i, t: (0, i)), (keep_t, rows)
    grid_spec = pltpu.PrefetchScalarGridSpec(
        num_scalar_prefetch=1, grid=(rows // tr,), in_specs=[slot(0), slot(1), slot(2), slot(3), slot(4)],
        out_specs=out_spec)
    return pl.pallas_call(
        body, name=name, grid_spec=grid_spec, out_shape=jax.ShapeDtypeStruct(out_shape, F32),
        compiler_params=_params(("parallel",)),
    )(sel, p, r, r, r, r)


def _expand_w_in(w_t, shift):
    cw, rows = w_t.shape
    tr = 256
    pad = -cw % LANES

    def body(s_ref, w_ref, o_ref):
        w = jnp.concatenate([w_ref[...], jnp.zeros((pad, tr), F32)], axis=0).T
        w = jnp.concatenate([w, jnp.zeros((tr, EXP_W - cw - pad), F32)], axis=1)
        o_ref[...] = pltpu.roll(w, s_ref[0], 1).astype(BF16)

    grid_spec = pltpu.PrefetchScalarGridSpec(
        num_scalar_prefetch=1, grid=(rows // tr,), in_specs=[pl.BlockSpec((cw, tr), lambda i, t: (0, i))],
        out_specs=pl.BlockSpec((tr, EXP_W), lambda i, t: (i, 0)))
    return pl.pallas_call(
        body, name="expand_w_in", grid_spec=grid_spec, out_shape=jax.ShapeDtypeStruct((rows, EXP_W), BF16),
        compiler_params=_params(("arbitrary",)),
    )(shift, w_t)


def _pad_wq(w_t):
    cw, rows = w_t.shape

    def body(w_ref, o_ref):
        o_ref[...] = jnp.concatenate([w_ref[...], jnp.zeros((QK_PAD - cw, rows), F32)], axis=0).T.astype(BF16)

    return pl.pallas_call(
        body, name="pad_wq", out_shape=jax.ShapeDtypeStruct((rows, QK_PAD), BF16), compiler_params=_params(),
    )(w_t)


def _merge_w_in(e):
    _, rows, _ = e.shape
    tr = 256

    def body(e_ref, o_ref):
        for t in range(U_COLS // LANES):
            lo, hi = t * LANES, (t + 1) * LANES
            parts = [e_ref[d, :, lo - W_LO[d]:hi - W_LO[d]] for d in range(N_DEV)
                     if CW * d < hi and CW * (d + 1) > lo]
            if not parts:
                tile = jnp.zeros((tr, LANES), BF16)
            elif len(parts) == 1:
                tile = parts[0]
            else:
                tile = (parts[0].astype(F32) + parts[1].astype(F32)).astype(BF16)
            o_ref[:, lo:hi] = tile

    return pl.pallas_call(
        body, name="merge_w_in", grid=(rows // tr,),
        in_specs=[pl.BlockSpec((N_DEV, tr, EXP_W), lambda i: (0, i, 0))],
        out_specs=pl.BlockSpec((tr, U_COLS), lambda i: (i, 0)), out_shape=jax.ShapeDtypeStruct((rows, U_COLS), BF16),
        compiler_params=_params(("parallel",)),
    )(e)


def _sum_leading(a, out_dtype, name):
    k, rows, cols = a.shape
    tr = min(rows, 1728 if rows % 1728 == 0 else rows)

    def body(a_ref, o_ref):
        acc = a_ref[0].astype(F32)
        for i in range(1, k):
            acc = acc + a_ref[i].astype(F32)
        o_ref[...] = acc.astype(out_dtype)

    return pl.pallas_call(
        body, name=name, grid=(rows // tr,),
        in_specs=[pl.BlockSpec((k, tr, cols), lambda i: (0, i, 0))],
        out_specs=pl.BlockSpec((tr, cols), lambda i: (i, 0)),
        out_shape=jax.ShapeDtypeStruct((rows, cols), out_dtype), compiler_params=_params(("parallel",)),
    )(a)


def _add_pairs(g, recv, core, name):
    k, rows, cols = recv.shape
    tr = 1728 if rows % 1728 == 0 else rows

    def body(c_ref, g_ref, r_ref, o_ref):
        del c_ref
        o_ref[...] = (g_ref[...].astype(F32) + r_ref[...].astype(F32)).astype(o_ref.dtype)

    spec = pl.BlockSpec((1, tr, cols), lambda i, j, c: (i, j, 0))
    grid_spec = pltpu.PrefetchScalarGridSpec(
        num_scalar_prefetch=1, grid=(k, rows // tr),
        in_specs=[pl.BlockSpec((1, tr, cols), lambda i, j, c: (2 * i + c[0], j, 0)), spec], out_specs=spec)
    return pl.pallas_call(
        body, name=name, grid_spec=grid_spec, out_shape=jax.ShapeDtypeStruct(recv.shape, recv.dtype),
        compiler_params=_params(("parallel", "parallel")),
    )(core, g, recv)


def _ada_mod(c16, ada_w_l, ada_b_l):
    def body(c_ref, w_ref, b_ref, o_ref):
        cv = c_ref[...]
        sc = (cv * _sigmoid(cv)).astype(BF16)
        o_ref[...] = _dot(sc, w_ref[...].astype(BF16)) + b_ref[...]

    return pl.pallas_call(
        body, name="ada_mod", out_shape=jax.ShapeDtypeStruct((c16.shape[0], ada_w_l.shape[1]), F32),
        compiler_params=_params(),
    )(c16, ada_w_l, ada_b_l)


def _ada_w_grad(c_t, dmod_my):
    def body(c_ref, d_ref, o_ref):
        cv = c_ref[...]
        sc = cv * _sigmoid(cv)
        acc = sc[:, 0:1] * d_ref[0:1, :]
        for b in range(1, N_DEV):
            acc = acc + sc[:, b:b + 1] * d_ref[b:b + 1, :]
        o_ref[...] = acc

    return pl.pallas_call(
        body, name="ada_w_grad", out_shape=jax.ShapeDtypeStruct((c_t.shape[0], dmod_my.shape[1]), F32),
        compiler_params=_params(),
    )(c_t, dmod_my)


def _norm_mod(x, norm_g, mod, pos_col, invf, sign, rider, rider_inputs):
    s, d = x.shape
    tm = min(TM_MM, s)
    n_in, n_out = len(rider_inputs), len(rider.out_shape)
    steps = s // tm

    def body(x_ref, g_ref, mod_ref, p_ref, f_ref, s_ref, *rest):
        r_ins, (h_ref, ht_ref, cos_ref, sin_ref) = rest[:n_in], rest[n_in:n_in + 4]
        r_outs, sems = rest[n_in + 4:n_in + 4 + n_out], rest[n_in + 4 + n_out:]
        pl.when(pl.program_id(0) == 0)(functools.partial(rider.start, r_ins, r_outs, sems))
        xv = x_ref[...]
        r = lax.rsqrt(jnp.mean(xv * xv, axis=-1, keepdims=True) + EPS)
        hn = xv * r * g_ref[...]
        hv = hn * (1.0 + mod_ref[:, d:2 * d]) + mod_ref[:, 0:d]
        h_ref[...] = hv.astype(BF16)
        ht_ref[...] = hv.T.astype(BF16)
        ang = p_ref[...].astype(F32) * f_ref[...]
        sg = s_ref[...]
        cos_ref[...] = jnp.cos(ang) * jnp.abs(sg)
        sin_ref[...] = jnp.sin(ang) * sg

        @pl.when(pl.program_id(0) == steps - 1)
        def _():
            rider.forward(r_ins, r_outs, sems)
            rider.finish(r_ins, r_outs, sems)

    row = pl.BlockSpec((1, LANES), lambda i: (0, 0))
    tab = pl.BlockSpec((tm, LANES), lambda i: (i, 0))
    return pl.pallas_call(
        body, name="norm_mod", grid=(steps,),
        in_specs=[pl.BlockSpec((tm, d), lambda i: (i, 0)), pl.BlockSpec((1, d), lambda i: (0, 0)),
                  pl.BlockSpec((1, 3 * d), lambda i: (0, 0)), pl.BlockSpec((tm, 1), lambda i: (i, 0)), row, row]
        + [ANY] * n_in,
        out_specs=[pl.BlockSpec((tm, d), lambda i: (i, 0)), pl.BlockSpec((d, tm), lambda i: (0, i)), tab, tab] + [ANY] * n_out,
        out_shape=[jax.ShapeDtypeStruct((s, d), BF16), jax.ShapeDtypeStruct((d, s), BF16),
                   jax.ShapeDtypeStruct((s, LANES), F32), jax.ShapeDtypeStruct((s, LANES), F32)] + rider.out_shape,
        scratch_shapes=rider.scratch, compiler_params=_params(("arbitrary",)),
    )(x, norm_g, mod, pos_col, invf, sign, *rider_inputs)


def _matmul(a, b, *, nt, out_dtype, tm, tn, name, rider=None, rider_inputs=(), a_resident=False):
    m, kdim = a.shape
    n = b.shape[0] if nt else b.shape[1]
    tm, tn = min(tm, m), min(tn, n)
    n_in = len(rider_inputs)
    n_out = len(rider.out_shape) if rider else 0
    m_steps, n_steps = m // tm, n // tn
    steps = n_steps * m_steps
    inner = n_steps if a_resident else m_steps
    tile = (lambda o, i: (o, i)) if a_resident else (lambda o, i: (i, o))

    def body(a_ref, b_ref, *rest):
        r_ins, o_ref, r_outs, sems = rest[:n_in], rest[n_in], rest[n_in + 1:n_in + 1 + n_out], rest[n_in + 1 + n_out:]
        step = pl.program_id(0) * inner + pl.program_id(1)
        if rider:
            pl.when(step == 0)(functools.partial(rider.start, r_ins, r_outs, sems))
            pl.when(step == steps // 2)(functools.partial(rider.forward, r_ins, r_outs, sems))
        o = _dot_nt(a_ref[...], b_ref[...]) if nt else _dot(a_ref[...], b_ref[...])
        o_ref[...] = o.astype(out_dtype)
        if rider:
            pl.when(step == steps - 1)(functools.partial(rider.finish, r_ins, r_outs, sems))

    if nt:
        b_spec = pl.BlockSpec((tn, kdim), lambda o, i: (tile(o, i)[1], 0))
    else:
        b_spec = pl.BlockSpec((kdim, tn), lambda o, i: (0, tile(o, i)[1]))
    out = pl.pallas_call(
        body, name=name, grid=(m_steps, n_steps) if a_resident else (n_steps, m_steps),
        in_specs=[pl.BlockSpec((tm, kdim), lambda o, i: (tile(o, i)[0], 0)), b_spec] + [ANY] * n_in,
        out_specs=[pl.BlockSpec((tm, tn), tile)] + [ANY] * n_out,
        out_shape=[jax.ShapeDtypeStruct((m, n), out_dtype)] + (rider.out_shape if rider else []),
        scratch_shapes=rider.scratch if rider else [],
        compiler_params=_params(("arbitrary", "arbitrary") if rider else ("parallel", "parallel")),
    )(a, b, *rider_inputs)
    return out if rider else out[0]


HALO = 16


def _conv_specs(tm):
    def col(j):
        return pl.BlockSpec((tm, D_CONV), lambda i: (i, j))

    def prev(j):
        return pl.BlockSpec((HALO, D_CONV), lambda i: (jnp.maximum(i * (tm // HALO) - 1, 0), j))

    return [col(0), col(1), col(2), col(3), prev(0), prev(2)]


def _conv_y(xc_ref, bc_ref, cc_ref, zc_ref, xp_ref, cp_ref, w_ref, first):
    uc = cc_ref[...].astype(F32) * xc_ref[...].astype(F32)
    up = jnp.where(first, 0.0, cp_ref[...].astype(F32) * xp_ref[...].astype(F32))
    full = jnp.concatenate([up, uc], axis=0)
    u1 = pltpu.roll(full, 1, 0)[HALO:]
    u2 = pltpu.roll(full, 2, 0)[HALO:]
    w = w_ref[...]
    conv = w[0:1] * u2 + w[1:2] * u1 + w[2:3] * uc
    z = zc_ref[...].astype(F32)
    return bc_ref[...].astype(F32) * conv * (z * _sigmoid(z))


def _conv_bwd(u, dyc, conv_w):
    s = u.shape[0]
    tm = min(TM_ELEM, s)
    cb = D_CONV
    nt = s // tm

    def body(xc_ref, bc_ref, cc_ref, zc_ref, xp_ref, cp_ref, bn_ref, zn_ref, dy_ref, dyn_ref, w_ref, du_ref, dw_ref):
        i = pl.program_id(0)
        xc, cc = xc_ref[...].astype(F32), cc_ref[...].astype(F32)
        bc, z = bc_ref[...].astype(F32), zc_ref[...].astype(F32)
        uc = cc * xc
        up = jnp.where(i == 0, 0.0, cp_ref[...].astype(F32) * xp_ref[...].astype(F32))
        full = jnp.concatenate([up, uc], axis=0)
        u1 = pltpu.roll(full, 1, 0)[HALO:]
        u2 = pltpu.roll(full, 2, 0)[HALO:]
        w = w_ref[...]
        conv = w[0:1] * u2 + w[1:2] * u1 + w[2:3] * uc
        sg = _sigmoid(z)
        sz = z * sg
        dy = dy_ref[...].astype(F32)
        dconv = dy * bc * sz
        zn = zn_ref[...].astype(F32)
        dnext = dyn_ref[...].astype(F32) * bn_ref[...].astype(F32) * (zn * _sigmoid(zn))
        dnext = jnp.where(i == nt - 1, 0.0, dnext)
        fullb = jnp.concatenate([dconv, dnext], axis=0)
        nb = tm + HALO
        d1 = pltpu.roll(fullb, nb - 1, 0)[:tm]
        d2 = pltpu.roll(fullb, nb - 2, 0)[:tm]
        duc = w[2:3] * dconv + w[1:2] * d1 + w[0:1] * d2
        dzc = dy * bc * conv * (sg * (1.0 + z * (1.0 - sg)))
        du_ref[...] = jnp.concatenate([duc * cc, dy * conv * sz, duc * xc, dzc], axis=1).astype(BF16)
        dw = jnp.concatenate([jnp.sum(dconv * u2, axis=0, keepdims=True), jnp.sum(dconv * u1, axis=0, keepdims=True),
                              jnp.sum(dconv * uc, axis=0, keepdims=True), jnp.zeros((5, cb), F32)], axis=0)

        @pl.when(i == 0)
        def _():
            dw_ref[...] = dw

        @pl.when(i > 0)
        def _():
            dw_ref[...] += dw

    def col(j):
        return pl.BlockSpec((tm, cb), lambda i: (i, j))

    def prev(j):
        return pl.BlockSpec((HALO, cb), lambda i: (jnp.maximum(i * (tm // HALO) - 1, 0), j))

    def nxt(j):
        return pl.BlockSpec((HALO, cb), lambda i: (jnp.minimum((i + 1) * (tm // HALO), s // HALO - 1), j))

    return pl.pallas_call(
        body, name="conv_bwd", grid=(nt,),
        in_specs=[col(0), col(1), col(2), col(3), prev(0), prev(2), nxt(1), nxt(3), col(0), nxt(0),
                  pl.BlockSpec((3, cb), lambda i: (0, 0))],
        out_specs=[pl.BlockSpec((tm, 4 * cb), lambda i: (i, 0)), pl.BlockSpec((8, cb), lambda i: (0, 0))],
        out_shape=[jax.ShapeDtypeStruct((s, U_COLS), BF16), jax.ShapeDtypeStruct((8, cb), F32)],
        compiler_params=_params(("arbitrary",)),
    )(u, u, u, u, u, u, u, u, dyc, dyc, conv_w)


def _qkv_specs(tm):
    return [pl.BlockSpec((tm, Q_LORA), lambda i: (i, U_CQ // Q_LORA)),
            pl.BlockSpec((tm, KV_LORA), lambda i: (i, U_CKV // KV_LORA)),
            pl.BlockSpec((tm, LANES), lambda i: (i, U_KR // LANES)),
            pl.BlockSpec((tm, LANES), lambda i: (i, 0)), pl.BlockSpec((tm, LANES), lambda i: (i, 0))]


def _full(shape):
    return pl.BlockSpec(shape, lambda i: (0,) * len(shape))


def _k_rope_lanes(blk):
    lane = lax.broadcasted_iota(jnp.int32, blk.shape, 1)
    return jnp.where(lane < QK_ROPE, blk, 0.0)


def _qkv_fwd(u, cos, sin, wq, wkv, qag, kvag, qg, kg):
    s = u.shape[0]
    tm = min(TM_ELEM, s)

    def body(cq_ref, ckv_ref, kr_ref, cos_ref, sin_ref, wq_ref, wkv_ref, qag_ref, kvag_ref, qg_ref, kg_ref,
             q_ref, k_ref, v_ref):
        cq = cq_ref[...].astype(F32)
        cqn = (cq * lax.rsqrt(jnp.mean(cq * cq, axis=-1, keepdims=True) + EPS) * qag_ref[...]).astype(BF16)
        ckv = ckv_ref[...].astype(F32)
        ckvn = (ckv * lax.rsqrt(jnp.mean(ckv * ckv, axis=-1, keepdims=True) + EPS) * kvag_ref[...]).astype(BF16)
        kr = _k_rope_lanes(kr_ref[...].astype(F32))
        cosv, sinv, qgv, kgv = cos_ref[...], sin_ref[...], qg_ref[...], kg_ref[...]
        ss_r = jnp.sum(kr * kr, axis=-1, keepdims=True)
        krr = _rope(kr * kgv[:, QK_NOPE:], cosv, sinv)
        qf = _dot(cqn, wq_ref[...])
        kvf = _dot(ckvn, wkv_ref[...])
        heads = range(N_HEADS)
        qh = [qf[:, QK_PAD * h:QK_PAD * (h + 1)] for h in heads]
        kn = [kvf[:, 2 * V_HEAD * h:2 * V_HEAD * h + QK_NOPE] for h in heads]
        rq = [lax.rsqrt(jnp.sum(qh[h] * qh[h], axis=-1, keepdims=True) * (1.0 / QK_HEAD) + EPS) for h in heads]
        rk = [lax.rsqrt((jnp.sum(kn[h] * kn[h], axis=-1, keepdims=True) + ss_r) * (1.0 / QK_HEAD) + EPS) for h in heads]
        for h in heads:
            qn = qh[h] * rq[h] * qgv
            qo = jnp.concatenate([qn[:, :QK_NOPE], _rope(qn[:, QK_NOPE:], cosv, sinv)], axis=1) * (SCALE * LOG2E)
            q_ref[h] = qo.astype(BF16)
            vh = kvf[:, 2 * V_HEAD * h + QK_NOPE:2 * V_HEAD * (h + 1)]
            k_ref[h] = jnp.concatenate([kn[h] * kgv[:, :QK_NOPE] * rk[h], krr * rk[h]], axis=1).astype(BF16)
            v_ref[h] = jnp.concatenate([vh, jnp.ones_like(vh)], axis=1).astype(BF16)

    return pl.pallas_call(
        body, name="qkv_fwd", grid=(s // tm,),
        in_specs=_qkv_specs(tm) + [_full((Q_LORA, N_HEADS * QK_PAD)), _full((KV_LORA, 2 * D_ATTN)),
                                   _full((1, Q_LORA)), _full((1, KV_LORA)), _full((1, QK_PAD)), _full((1, QK_PAD))],
        out_specs=[pl.BlockSpec((N_HEADS, tm, QK_PAD), lambda i: (0, i, 0)),
                   pl.BlockSpec((N_HEADS, tm, QK_PAD), lambda i: (0, i, 0)),
                   pl.BlockSpec((N_HEADS, tm, 2 * V_HEAD), lambda i: (0, i, 0))],
        out_shape=[jax.ShapeDtypeStruct((N_HEADS, s, QK_PAD), BF16), jax.ShapeDtypeStruct((N_HEADS, s, QK_PAD), BF16),
                   jax.ShapeDtypeStruct((N_HEADS, s, 2 * V_HEAD), BF16)],
        compiler_params=_params(("parallel",)),
    )(u, u, u, cos, sin, wq, wkv, qag, kvag, qg, kg)


def _qkv_bwd(u, cos, sin, dq, dk, dv, dza, wq, wkv, qag, kvag, qg, kg, du):
    s = u.shape[0]
    tm = min(TM_QKV_BWD, s)
    nt = s // tm

    def body(cq_ref, ckv_ref, kr_ref, cos_ref, sin_ref, dq_ref, dk_ref, dv_ref, dza_ref, wq_ref, wkv_ref, qag_ref,
             kvag_ref, qg_ref, kg_ref, du_in, du_ref, dwq_ref, dwkv_ref, dqag_ref, dkvag_ref, dqg_ref, dkg_ref,
             dwq_acc, dwkv_acc):
        del du_in
        i = pl.program_id(0)

        @pl.when(i == 0)
        def _():
            dwq_acc[...] = jnp.zeros_like(dwq_acc)
            dwkv_acc[...] = jnp.zeros_like(dwkv_acc)

        cq = cq_ref[...].astype(F32)
        rqa = lax.rsqrt(jnp.mean(cq * cq, axis=-1, keepdims=True) + EPS)
        xq = cq * rqa
        qagv = qag_ref[...]
        cqn = (xq * qagv).astype(BF16)
        ckv = ckv_ref[...].astype(F32)
        rkva = lax.rsqrt(jnp.mean(ckv * ckv, axis=-1, keepdims=True) + EPS)
        xkv = ckv * rkva
        kvagv = kvag_ref[...]
        ckvn = (xkv * kvagv).astype(BF16)
        kr = _k_rope_lanes(kr_ref[...].astype(F32))
        cosv, sinv, qgv, kgv = cos_ref[...], sin_ref[...], qg_ref[...], kg_ref[...]
        ss_r = jnp.sum(kr * kr, axis=-1, keepdims=True)
        dqg = jnp.zeros((1, QK_PAD), F32)
        dkg = jnp.zeros((1, QK_PAD), F32)
        dkr = jnp.zeros((tm, LANES), F32)
        qf = _dot(cqn, wq_ref[...])
        kvf = _dot(ckvn, wkv_ref[...])
        heads = range(N_HEADS)
        qh = [qf[:, QK_PAD * h:QK_PAD * (h + 1)] for h in heads]
        kn = [kvf[:, 2 * V_HEAD * h:2 * V_HEAD * h + QK_NOPE] for h in heads]
        rq = [lax.rsqrt(jnp.sum(qh[h] * qh[h], axis=-1, keepdims=True) * (1.0 / QK_HEAD) + EPS) for h in heads]
        rk = [lax.rsqrt((jnp.sum(kn[h] * kn[h], axis=-1, keepdims=True) + ss_r) * (1.0 / QK_HEAD) + EPS) for h in heads]
        xh = [qh[h] * rq[h] for h in heads]
        xk = [jnp.concatenate([kn[h], kr], axis=1) * rk[h] for h in heads]
        dyq, dyk = [], []
        for h in heads:
            g = dq_ref[h].astype(F32) * SCALE
            dyq.append(jnp.concatenate([g[:, :QK_NOPE], _rope_t(g[:, QK_NOPE:], cosv, sinv)], axis=1))
            gk = dk_ref[h].astype(F32)
            dyk.append(jnp.concatenate([gk[:, :QK_NOPE], _rope_t(gk[:, QK_NOPE:], cosv, sinv)], axis=1))
        for h in heads:
            dqg = dqg + jnp.sum(dyq[h] * xh[h], axis=0, keepdims=True)
            dkg = dkg + jnp.sum(dyk[h] * xk[h], axis=0, keepdims=True)
        gdy = [dyq[h] * qgv for h in heads]
        gdyk = [dyk[h] * kgv for h in heads]
        tq_ = [jnp.sum(gdy[h] * xh[h], axis=-1, keepdims=True) * (1.0 / QK_HEAD) for h in heads]
        tk_ = [jnp.sum(gdyk[h] * xk[h], axis=-1, keepdims=True) * (1.0 / QK_HEAD) for h in heads]
        dqf = [(rq[h] * (gdy[h] - xh[h] * tq_[h])).astype(BF16) for h in heads]
        dkvf = []
        for h in heads:
            dxk = rk[h] * (gdyk[h] - xk[h] * tk_[h])
            dkr = dkr + dxk[:, QK_NOPE:]
            dkvf += [dxk[:, :QK_NOPE].astype(BF16), dv_ref[h]]
        dqf_b, dkvf_b = jnp.concatenate(dqf, axis=1), jnp.concatenate(dkvf, axis=1)
        dwq_acc[...] += _dot_tn(cqn, dqf_b)
        dwkv_acc[...] += _dot_tn(ckvn, dkvf_b)
        dcqn = _dot_nt(dqf_b, wq_ref[...])
        dckvn = _dot_nt(dkvf_b, wkv_ref[...])
        dqag = jnp.sum(dcqn * xq, axis=0, keepdims=True)
        dkvag = jnp.sum(dckvn * xkv, axis=0, keepdims=True)
        gq = dcqn * qagv
        dcq = rqa * (gq - xq * jnp.mean(gq * xq, axis=-1, keepdims=True))
        gkv = dckvn * kvagv
        dckv = rkva * (gkv - xkv * jnp.mean(gkv * xkv, axis=-1, keepdims=True))
        win = pltpu.roll(jnp.concatenate([dza_ref[...].astype(F32), jnp.zeros((tm, LANES), F32)], axis=1), QK_ROPE, 1)
        win = win + jnp.concatenate([dkr, jnp.zeros((tm, D_ATTN), F32)], axis=1)
        du_ref[...] = jnp.concatenate([dcq, dckv, win, jnp.zeros((tm, U_TAIL - ZA_LO - ZA_WIN), F32)], axis=1).astype(BF16)

        @pl.when(i == 0)
        def _():
            dqag_ref[...] = dqag
            dkvag_ref[...] = dkvag
            dqg_ref[...] = dqg
            dkg_ref[...] = dkg

        @pl.when(i > 0)
        def _():
            dqag_ref[...] += dqag
            dkvag_ref[...] += dkvag
            dqg_ref[...] += dqg
            dkg_ref[...] += dkg

        @pl.when(i == nt - 1)
        def _():
            dwq_ref[...] = dwq_acc[...].astype(BF16)
            dwkv_ref[...] = dwkv_acc[...].astype(BF16)

    head = lambda w: pl.BlockSpec((N_HEADS, tm, w), lambda i: (0, i, 0))
    wq_shape, wkv_shape = (Q_LORA, N_HEADS * QK_PAD), (KV_LORA, 2 * D_ATTN)
    return pl.pallas_call(
        body, name="qkv_bwd", grid=(nt,),
        in_specs=_qkv_specs(tm) + [head(QK_PAD), head(QK_PAD), head(V_HEAD), pl.BlockSpec((tm, D_ATTN), lambda i: (i, 0)),
                                   _full(wq_shape), _full(wkv_shape), _full((1, Q_LORA)), _full((1, KV_LORA)),
                                   _full((1, QK_PAD)), _full((1, QK_PAD)), ANY],
        out_specs=[pl.BlockSpec((tm, U_TAIL), lambda i: (i, U_COLS // U_TAIL - 1)), _full(wq_shape), _full(wkv_shape),
                   _full((1, Q_LORA)), _full((1, KV_LORA)), _full((1, QK_PAD)), _full((1, QK_PAD))],
        out_shape=[jax.ShapeDtypeStruct(du.shape, du.dtype), jax.ShapeDtypeStruct(wq_shape, BF16),
                   jax.ShapeDtypeStruct(wkv_shape, BF16), jax.ShapeDtypeStruct((1, Q_LORA), F32),
                   jax.ShapeDtypeStruct((1, KV_LORA), F32), jax.ShapeDtypeStruct((1, QK_PAD), F32),
                   jax.ShapeDtypeStruct((1, QK_PAD), F32)],
        scratch_shapes=[pltpu.VMEM(wq_shape, F32), pltpu.VMEM(wkv_shape, F32)],
        input_output_aliases={15: 0}, compiler_params=_params(("arbitrary",)),
    )(u, u, u, cos, sin, dq, dk, dv, dza, wq, wkv, qag, kvag, qg, kg, du)


def _flash_fwd(q, k, v):
    nh, s, _ = q.shape
    tq = min(TQ, s)
    nkv = KV_SPLIT
    tk = tq // nkv
    nq = s // tq
    nch = Q_CHAINS
    tc = tq // nch

    def body(q_ref, k_ref, v_ref, o_ref, lse_ref):
        i = pl.program_id(1)
        chains = [q_ref[0, r * tc:(r + 1) * tc, :] for r in range(nch)]

        def unit(r, j, carry, shift=None):
            m, acc = carry
            rows = pl.ds(pl.multiple_of(j * tk, tk), tk)
            sc = _dot_nt(chains[r], k_ref[0, rows, :])
            if shift is not None:
                qi = lax.broadcasted_iota(jnp.int32, sc.shape, 0)
                ki = lax.broadcasted_iota(jnp.int32, sc.shape, 1) + shift
                sc = jnp.where(ki <= qi, sc, NEG)
            m_new = jnp.maximum(m, jnp.max(sc, axis=-1, keepdims=True))
            p = jnp.exp2(sc - m_new).astype(BF16)
            return m_new, jnp.exp2(m - m_new) * acc + _dot(p, v_ref[0, rows, :])

        def trip(p, carry):
            for b in range(nkv):
                carry = tuple(unit(r, nkv * p + b, cr) for r, cr in enumerate(carry))
            return carry

        init = (jnp.full((tc, 1), NEG, F32), jnp.zeros((tc, 2 * V_HEAD), F32))
        carry = list(lax.fori_loop(0, i, trip, (init,) * nch))
        for b in range(nkv):
            for r in range(nch):
                shift = b * tk - r * tc
                if shift < tc:
                    carry[r] = unit(r, nkv * i + b, carry[r], None if shift + tk - 1 <= 0 else shift)
        for r, (m, acc) in enumerate(carry):
            l = acc[:, V_HEAD:]
            o_ref[r * tc:(r + 1) * tc, :] = (acc[:, :V_HEAD] / l).astype(BF16)
            lse = m + jnp.log(l[:, 0:1]) * LOG2E
            lse_ref[0, :, r * tc:(r + 1) * tc] = jnp.broadcast_to(lse, (tc, LANES)).T[0:1, :]

    return pl.pallas_call(
        body, name="flash_fwd", grid=(nh, nq),
        in_specs=[pl.BlockSpec((1, tq, QK_PAD), lambda h, i: (h, i, 0)),
                  pl.BlockSpec((1, s, QK_PAD), lambda h, i: (h, 0, 0)),
                  pl.BlockSpec((1, s, 2 * V_HEAD), lambda h, i: (h, 0, 0))],
        out_specs=[pl.BlockSpec((tq, V_HEAD), lambda h, i: (i, h)), pl.BlockSpec((1, 1, tq), lambda h, i: (h, 0, i))],
        out_shape=[jax.ShapeDtypeStruct((s, nh * V_HEAD), BF16), jax.ShapeDtypeStruct((nh, 1, s), F32)],
        compiler_params=_params(("parallel", "arbitrary")),
    )(q, k, v)


def _flash_bwd(q, k, v, do, lse, delta):
    nh, s, _ = q.shape
    tq = min(TQ, s)
    nq = s // tq

    def body(q_ref, k_ref, v_ref, do_ref, lse_ref, dl_ref, dq_ref, dk_ref, dv_ref, dq_acc):
        j = pl.program_id(1)

        @pl.when(j == 0)
        def _():
            dq_acc[...] = jnp.zeros_like(dq_acc)

        kj, vj = k_ref[0], v_ref[0]

        def block(kk, vv, qq, dd, lse, dl, masked):
            st = _dot_nt(kk, qq)
            pt = jnp.exp2(st - lse)
            if masked:
                ki = lax.broadcasted_iota(jnp.int32, st.shape, 0)
                qx = lax.broadcasted_iota(jnp.int32, st.shape, 1)
                pt = jnp.where(ki <= qx, pt, 0.0)
            ddv = _dot(pt.astype(BF16), dd)
            dst = (pt * (_dot_nt(vv, dd) - dl)).astype(BF16)
            ddq = _dot_tn(dst, kk)
            return _dot(dst, qq), ddv, ddq

        def step(i, carry):
            dk, dv = carry
            rows = pl.ds(pl.multiple_of(i * tq, tq), tq)
            ddk, ddv, ddq = block(kj, vj, q_ref[0, rows, :], do_ref[rows, :], lse_ref[0, pl.ds(i, 1), :],
                                  dl_ref[0, pl.ds(i, 1), :], False)
            dq_acc[rows, :] += ddq
            return dk + ddk, dv + ddv

        th = tq // 2
        lse_j, dl_j = lse_ref[0, pl.ds(j, 1), :], dl_ref[0, pl.ds(j, 1), :]
        parts = []
        for kh, qh, masked in ((0, 0, True), (0, 1, False), (1, 1, True)):
            rows = pl.ds(pl.multiple_of(j * tq + qh * th, th), th)
            ks, qs = slice(kh * th, (kh + 1) * th), slice(qh * th, (qh + 1) * th)
            ddk, ddv, ddq = block(kj[ks], vj[ks], q_ref[0, rows, :], do_ref[rows, :], lse_j[:, qs], dl_j[:, qs], masked)
            dq_acc[rows, :] += ddq
            parts.append((ddk, ddv))
        carry = (jnp.concatenate([parts[0][0] + parts[1][0], parts[2][0]], axis=0),
                 jnp.concatenate([parts[0][1] + parts[1][1], parts[2][1]], axis=0))
        dk, dv = lax.fori_loop(j + 1, nq, step, carry)
        dk_ref[0] = (dk * LN2).astype(BF16)
        dv_ref[0] = dv.astype(BF16)

        @pl.when(j == nq - 1)
        def _():
            dq_ref[0] = dq_acc[...].astype(BF16)

    return pl.pallas_call(
        body, name="flash_bwd", grid=(nh, nq),
        in_specs=[pl.BlockSpec((1, s, QK_PAD), lambda h, j: (h, 0, 0)),
                  pl.BlockSpec((1, tq, QK_PAD), lambda h, j: (h, j, 0)),
                  pl.BlockSpec((1, tq, V_HEAD), lambda h, j: (h, j, 0)),
                  pl.BlockSpec((s, V_HEAD), lambda h, j: (0, h)),
                  pl.BlockSpec((1, nq, tq), lambda h, j: (h, 0, 0)),
                  pl.BlockSpec((1, nq, tq), lambda h, j: (h, 0, 0))],
        out_specs=[pl.BlockSpec((1, s, QK_PAD), lambda h, j: (h, 0, 0)),
                   pl.BlockSpec((1, tq, QK_PAD), lambda h, j: (h, j, 0)),
                   pl.BlockSpec((1, tq, V_HEAD), lambda h, j: (h, j, 0))],
        out_shape=[jax.ShapeDtypeStruct((nh, s, QK_PAD), BF16), jax.ShapeDtypeStruct((nh, s, QK_PAD), BF16),
                   jax.ShapeDtypeStruct((nh, s, V_HEAD), BF16)],
        scratch_shapes=[pltpu.VMEM((s, QK_PAD), F32)],
        compiler_params=_params(("parallel", "arbitrary")),
    )(q, k, v, do, lse, delta)


def _tail(x, target, o, u, mod, w_out, conv_w):
    s, d = x.shape
    tm = min(TM_ELEM, s)

    def body(x_ref, t_ref, o_ref, za_ref, mod_ref, w_ref, xc_ref, bc_ref, cc_ref, zc_ref, xp_ref, cp_ref, cw_ref,
             gx_ref, dy_ref, ycat_ref, dyc_ref, do_ref, du_ref, delta_ref, dgate_ref, loss_ref):
        i = pl.program_id(0)
        za = pltpu.roll(za_ref[:, ZA_LO:ZA_LO + ZA_WIN].astype(F32), ZA_WIN - QK_ROPE, 1)[:, :D_ATTN]
        ov = o_ref[...].astype(F32)
        sg = _sigmoid(za)
        sl = za * sg
        ya = ov * sl
        y = _dot(ya.astype(BF16), w_ref[D_CONV:, :])
        yc = _conv_y(xc_ref, bc_ref, cc_ref, zc_ref, xp_ref, cp_ref, cw_ref, i == 0)
        y = y + _dot(yc.astype(BF16), w_ref[:D_CONV, :])
        ycat_ref[...] = jnp.concatenate([yc.T, ya.T], axis=0).astype(BF16)
        gate = mod_ref[:, 2 * d:3 * d]
        e = x_ref[...] + gate * y - t_ref[...]
        dout = e * (1.0 / d)
        gx_ref[...] = dout
        dy = (dout * gate).astype(BF16)
        dy_ref[...] = dy
        dycat = _dot_nt(dy, w_ref[...])
        dyc_ref[...] = dycat[:, :D_CONV].astype(BF16)
        dya = dycat[:, D_CONV:]
        dov = dya * sl
        do_ref[...] = dov.astype(BF16)
        du_ref[...] = (dya * ov * (sg * (1.0 + za * (1.0 - sg)))).astype(BF16)
        prod_t = (dov * ov).T
        for h in range(N_HEADS):
            delta_ref[h] = jnp.sum(prod_t[V_HEAD * h:V_HEAD * (h + 1), :], axis=0, keepdims=True)
        dgate = jnp.sum(dout * y, axis=0, keepdims=True)
        part = jnp.sum(jnp.sum(e * e, axis=0, keepdims=True), axis=1, keepdims=True) * (0.5 / d)
        part = jnp.broadcast_to(part, (1, LANES))

        @pl.when(i == 0)
        def _():
            dgate_ref[...] = dgate
            loss_ref[...] = part

        @pl.when(i > 0)
        def _():
            dgate_ref[...] += dgate
            loss_ref[...] += part

    tok = lambda w: pl.BlockSpec((tm, w), lambda i: (i, 0))
    return pl.pallas_call(
        body, name="tail", grid=(s // tm,),
        in_specs=[tok(d), tok(d), tok(D_ATTN), pl.BlockSpec((tm, U_TAIL), lambda i: (i, U_COLS // U_TAIL - 1)),
                  _full((1, 3 * d)), _full((d, d))] + _conv_specs(tm) + [_full((3, D_CONV))],
        out_specs=[tok(d), tok(d), pl.BlockSpec((d, tm), lambda i: (0, i)), tok(D_CONV), tok(D_ATTN), tok(D_ATTN),
                   pl.BlockSpec((N_HEADS, 1, tm), lambda i: (0, 0, i)), _full((1, d)), _full((1, LANES))],
        out_shape=[jax.ShapeDtypeStruct((s, d), F32), jax.ShapeDtypeStruct((s, d), BF16),
                   jax.ShapeDtypeStruct((d, s), BF16), jax.ShapeDtypeStruct((s, D_CONV), BF16),
                   jax.ShapeDtypeStruct((s, D_ATTN), BF16), jax.ShapeDtypeStruct((s, D_ATTN), BF16),
                   jax.ShapeDtypeStruct((N_HEADS, 1, s), F32), jax.ShapeDtypeStruct((1, d), F32),
                   jax.ShapeDtypeStruct((1, LANES), F32)],
        compiler_params=_params(("arbitrary",)),
    )(x, target, o, u, mod, w_out, u, u, u, u, u, u, conv_w)


def _norm_bwd(x, dh, gx1, norm_g, mod):
    s, d = x.shape
    tm = min(TM_MM, s)

    def body(x_ref, dh_ref, gx_ref, g_ref, mod_ref, o_ref, dshift_ref, dscale_ref, dg_ref):
        i = pl.program_id(0)
        gv, sc1 = g_ref[...], 1.0 + mod_ref[:, d:2 * d]
        gsc = gv * sc1
        half = NORM_ROWS // 2

        def group(c, acc):
            a_dh, a_dhxn = acc
            ks = range(NORM_GROUP)
            rows = [pl.ds(pl.multiple_of((c * NORM_GROUP + k) * NORM_ROWS, NORM_ROWS), NORM_ROWS) for k in ks]
            xv = [x_ref[rows[k], :] for k in ks]
            dhv = [dh_ref[rows[k], :].astype(F32) for k in ks]
            r = [lax.rsqrt(jnp.mean(xv[k] * xv[k], axis=-1, keepdims=True) + EPS) for k in ks]
            xn = [xv[k] * r[k] for k in ks]
            dxn = [dhv[k] * gsc for k in ks]
            t = [jnp.mean(dxn[k] * xn[k], axis=-1, keepdims=True) for k in ks]
            for k in ks:
                o_ref[rows[k], :] = gx_ref[rows[k], :] + r[k] * (dxn[k] - xn[k] * t[k])
                dhxn = dhv[k] * xn[k]
                a_dh = a_dh + dhv[k][:half] + dhv[k][half:]
                a_dhxn = a_dhxn + dhxn[:half] + dhxn[half:]
            return a_dh, a_dhxn

        zero = jnp.zeros((half, d), F32)
        a_dh, a_dhxn = lax.fori_loop(0, tm // (NORM_ROWS * NORM_GROUP), group, (zero, zero))
        dshift = jnp.sum(a_dh, axis=0, keepdims=True)
        s_dhxn = jnp.sum(a_dhxn, axis=0, keepdims=True)
        dscale, dg = s_dhxn * gv, s_dhxn * sc1

        @pl.when(i == 0)
        def _():
            dshift_ref[...] = dshift
            dscale_ref[...] = dscale
            dg_ref[...] = dg

        @pl.when(i > 0)
        def _():
            dshift_ref[...] += dshift
            dscale_ref[...] += dscale
            dg_ref[...] += dg

    tok = pl.BlockSpec((tm, d), lambda i: (i, 0))
    row = jax.ShapeDtypeStruct((1, d), F32)
    return pl.pallas_call(
        body, name="norm_bwd", grid=(s // tm,),
        in_specs=[tok, tok, tok, _full((1, d)), _full((1, 3 * d))],
        out_specs=[tok, _full((1, d)), _full((1, d)), _full((1, d))],
        out_shape=[jax.ShapeDtypeStruct((s, d), F32), row, row, row],
        compiler_params=_params(("arbitrary",)),
    )(x, dh, gx1, norm_g, mod)


def _adamw(w, g, m, v, name):
    rows, cols = w.shape
    tr = 256 if rows % 256 == 0 else rows
    tc = 512 if (rows > 256 and tr == rows and cols % 512 == 0) else cols

    def body(w_ref, g_ref, m_ref, v_ref, d_ref, nm_ref, nv_ref):
        gv = g_ref[...]
        nm = ADAM_B1 * m_ref[...] + (1.0 - ADAM_B1) * gv
        nv = ADAM_B2 * v_ref[...] + (1.0 - ADAM_B2) * (gv * gv)
        m_hat = nm / (1.0 - ADAM_B1 ** ADAM_STEP)
        v_hat = nv / (1.0 - ADAM_B2 ** ADAM_STEP)
        d_ref[...] = -ADAM_LR * (m_hat / (jnp.sqrt(v_hat) + ADAM_EPS) + ADAM_WD * w_ref[...])
        nm_ref[...] = nm
        nv_ref[...] = nv

    spec = pl.BlockSpec((tr, tc), lambda i, j: (i, j))
    shape = jax.ShapeDtypeStruct((rows, cols), F32)
    return pl.pallas_call(
        body, name=name, grid=(rows // tr, cols // tc), in_specs=[spec] * 4, out_specs=[spec] * 3, out_shape=[shape] * 3,
        compiler_params=_params(("parallel", "parallel")),
    )(w, g, m, v)


def _pad_cols(a, n):
    return jnp.pad(a, ((0, 0), (0, n - a.shape[1])))


def kernel(x, c, positions, ada_w, ada_b, norm_g, w_in, conv_w, q_a_g, w_q_b, kv_a_g, w_kv_b, q_g, k_g, w_out, loss_target, m_ada_w, m_ada_b, m_norm_g, m_w_in, m_conv_w, m_q_a_g, m_w_q_b, m_kv_a_g, m_w_kv_b, m_q_g, m_k_g, m_w_out, v_ada_w, v_ada_b, v_norm_g, v_w_in, v_conv_w, v_q_a_g, v_w_q_b, v_kv_a_g, v_w_kv_b, v_q_g, v_k_g, v_w_out):
    me = _my_index()
    s = x.shape[1]
    nq = s // min(TQ, s)
    x2, tgt = x[0], loss_target[0]
    w_in_l, w_q_l, w_kv_l, w_out_l, conv_l, ada_w_l = w_in[0], w_q_b[0], w_kv_b[0], w_out[0], conv_w[0], ada_w[0]
    ada_cols = ada_w_l.shape[1]

    small = jnp.concatenate([c.reshape(-1, LANES), conv_l.reshape(-1, LANES), jnp.zeros((5, LANES), F32)], axis=0)
    (small_g,) = _all_gather([small], "gather_c")
    c_all = small_g[:, :D_MODEL // LANES].reshape(N_DEV, D_MODEL)
    conv_g = small_g[:, D_MODEL // LANES:D_MODEL // LANES + 3].transpose(1, 0, 2).reshape(3, D_CONV)

    ada_b_l = lax.dynamic_slice(ada_b, (0, me * ada_cols), (1, ada_cols))
    mod_cols = _ada_mod(jnp.pad(c_all, ((0, 8), (0, 0))), ada_w_l, ada_b_l)[:N_DEV]
    (mod_g,) = _all_gather([mod_cols], "gather_mod")
    mod = lax.dynamic_index_in_dim(mod_g, me, axis=1, keepdims=False).reshape(1, 3 * D_MODEL)

    half = jnp.arange(0, QK_ROPE, 2, dtype=F32) / QK_ROPE
    inv_freq = ROPE_BASE ** (-half)
    zeros64 = jnp.zeros((LANES - QK_ROPE,), F32)
    invf = jnp.concatenate([inv_freq, inv_freq, zeros64]).reshape(1, LANES)
    sign = jnp.concatenate([-jnp.ones((32,), F32), jnp.ones((32,), F32), zeros64]).reshape(1, LANES)
    qg_p, kg_p = _pad_cols(q_g, QK_PAD), _pad_cols(k_g, QK_PAD)

    my_off = ((CW * me) % LANES).astype(jnp.int32)
    win = [_expand_w_in(w_in_l.T, my_off.reshape(1))]
    h, h_t, cos, sin, win_g = _norm_mod(x2, norm_g, mod, positions.reshape(s, 1), invf, sign, _Gather(win, relay=True, parts=4), win)
    w_in_p = _merge_w_in(win_g)
    rest = [_pad_wq(w_q_l.T), w_kv_l.astype(BF16), w_out_l.astype(BF16)]
    u, wq_g, wkv_g, w_out_g = _matmul(h, w_in_p, nt=False, out_dtype=BF16, tm=2 * TM_MM, tn=2048, name="in_proj",
                                      rider=_Gather(rest), rider_inputs=rest)
    w_out_g = w_out_g.reshape(D_MODEL, D_MODEL)
    wq_g = wq_g.transpose(1, 0, 2).reshape(Q_LORA, N_HEADS * QK_PAD)
    wkv_g = wkv_g.transpose(1, 0, 2).reshape(KV_LORA, 2 * D_ATTN)
    q, k, v = _qkv_fwd(u, cos, sin, wq_g, wkv_g, q_a_g, kv_a_g, qg_p, kg_p)
    o, lse = _flash_fwd(q, k, v)
    gx1, dy, ycat_t, dyc, do, dza, delta, dgate, loss_row = _tail(x2, tgt, o, u, mod, w_out_g, conv_g)

    dq, dk, dv = _flash_bwd(q, k, v, do, lse.reshape(N_HEADS, nq, s // nq), delta.reshape(N_HEADS, nq, s // nq))
    du, dconv = _conv_bwd(u, dyc, conv_g)
    du, dwq, dwkv, dqag, dkvag, dqg, dkg = _qkv_bwd(u, cos, sin, dq, dk, dv, dza, wq_g, wkv_g, q_a_g, kv_a_g, qg_p, kg_p, du)
    dwq = dwq.reshape(Q_LORA, N_HEADS, QK_PAD).transpose(1, 0, 2)
    dwkv = dwkv.reshape(KV_LORA, N_HEADS, 2 * V_HEAD).transpose(1, 0, 2)
    dw_in = _matmul(h_t, du, nt=False, out_dtype=BF16, tm=TM_MM, tn=768, name="dw_in")
    first = [dw_in, dwq, dwkv]
    dw_out, r_in, r_q, r_kv = _matmul(ycat_t, dy, nt=False, out_dtype=BF16, tm=TM_MM, tn=512, name="dw_out",
                                      rider=_SiblingExchange(first, [True, False, False]), rider_inputs=first)
    dw_out = dw_out.reshape(N_DEV, D_MODEL // N_DEV, D_MODEL)
    (r_out,) = _exchange(_SiblingExchange([dw_out], [False]), [dw_out], "rs_sibling_out")
    core = lax.axis_index("c").astype(jnp.int32)
    lo_tiles = ((CW * (2 * jnp.arange(4, dtype=jnp.int32) + core)) // LANES).astype(jnp.int32)
    pairs = [_add_window(dw_in, r_in, lo_tiles), _add_pairs(dwq, r_q, core.reshape(1), "rs_add_q"),
             _add_pairs(dwkv, r_kv, core.reshape(1), "rs_add_kv"), _add_pairs(dw_out, r_out, core.reshape(1), "rs_add_out")]
    dh, *quads = _matmul(du, w_in_p, nt=True, out_dtype=BF16, tm=2 * TM_MM, tn=512, name="dh",
                         rider=_ChipExchange(pairs), rider_inputs=pairs, a_resident=True)
    my_chip = 2 * lax.axis_index("x") + lax.axis_index("y")
    written = jnp.where(jnp.arange(4) == my_chip, (jnp.arange(4) + 1) % 4, jnp.arange(4))
    sel = jnp.concatenate([my_chip.reshape(1), written, ((EXP_W - my_off) % EXP_W).reshape(1)]).astype(jnp.int32)
    g_w_in_t = _final_sum(pairs[0], quads[0], sel, "rs_sum_in", unshift=True, keep_t=CW)
    g_w_q_t = _final_sum(pairs[1], quads[1], sel, "rs_sum_q", keep_t=QK_HEAD)
    g_w_kv = _final_sum(pairs[2], quads[2], sel, "rs_sum_kv")
    g_w_out = _final_sum(pairs[3], quads[3], sel, "rs_sum_out")
    grad_x, dshift, dscale, dng = _norm_bwd(x2, dh, gx1, norm_g, mod)

    row = jnp.concatenate([dshift, dscale, dgate, dng, dqag, dkvag, dqg, dkg, dconv[:3].reshape(1, 3 * D_CONV), loss_row], axis=1)
    (rows_g,) = _all_gather([row], "gather_small")
    tot = _sum_leading(rows_g, F32, "sum_small")
    dmod_all = rows_g[:, 0, SM_MOD:SM_NG]
    g_ada_b = tot[:, SM_MOD:SM_NG]
    g_norm_g = tot[:, SM_NG:SM_QAG]
    g_q_a_g = tot[:, SM_QAG:SM_KVAG]
    g_kv_a_g = tot[:, SM_KVAG:SM_QG]
    g_q_g = tot[:, SM_QG:SM_QG + QK_HEAD]
    g_k_g = tot[:, SM_KG:SM_KG + QK_HEAD]
    conv_cols = conv_l.shape[1]
    g_conv = lax.dynamic_slice(tot[:, SM_CONV:SM_LOSS].reshape(3, D_CONV), (0, me * conv_cols), (3, conv_cols))
    loss = tot[0, SM_LOSS]
    dmod_my = lax.dynamic_slice(dmod_all, (0, me * ada_cols), (N_DEV, ada_cols))
    g_ada_w = _ada_w_grad(c_all.T, dmod_my)

    grads = dict(ada_w=g_ada_w, ada_b=g_ada_b, norm_g=g_norm_g, w_in=g_w_in_t, conv_w=g_conv, q_a_g=g_q_a_g, w_q_b=g_w_q_t,
                 kv_a_g=g_kv_a_g, w_kv_b=g_w_kv, q_g=g_q_g, k_g=g_k_g, w_out=g_w_out)
    weights = dict(ada_w=(ada_w, m_ada_w, v_ada_w), ada_b=(ada_b, m_ada_b, v_ada_b), norm_g=(norm_g, m_norm_g, v_norm_g),
                   w_in=(w_in, m_w_in, v_w_in), conv_w=(conv_w, m_conv_w, v_conv_w), q_a_g=(q_a_g, m_q_a_g, v_q_a_g),
                   w_q_b=(w_q_b, m_w_q_b, v_w_q_b), kv_a_g=(kv_a_g, m_kv_a_g, v_kv_a_g), w_kv_b=(w_kv_b, m_w_kv_b, v_w_kv_b),
                   q_g=(q_g, m_q_g, v_q_g), k_g=(k_g, m_k_g, v_k_g), w_out=(w_out, m_w_out, v_w_out))
    names = list(grads)
    out_g, out_d, out_m, out_v = [], [], [], []
    for n in names:
        w, m, v_ = weights[n]
        shape2 = w.shape[-2:] if w.ndim == 3 else (1, w.shape[-1])
        transposed = n in ("w_in", "w_q_b")
        to2 = (lambda a: a.reshape(shape2).T) if transposed else (lambda a: a.reshape(shape2))
        back = (lambda a: a.T.reshape(w.shape)) if transposed else (lambda a: a.reshape(w.shape))
        g2 = grads[n] if transposed else grads[n].reshape(shape2)
        d2, m2, v2 = _adamw(to2(w), g2, to2(m), to2(v_), "adamw_" + n)
        out_g.append(back(g2))
        out_d.append(back(d2))
        out_m.append(back(m2))
        out_v.append(back(v2))
    return (loss, grad_x.reshape(x.shape), *out_g, *out_d, *out_m, *out_v)
```

```python
import functools
import math

import jax
import jax.numpy as jnp
from jax import lax
from jax.experimental import pallas as pl
from jax.experimental.pallas import tpu as pltpu

F32 = jnp.float32
BF16 = jnp.bfloat16
MESH = pl.DeviceIdType.MESH

D_MODEL = 2048
D_CONV = 1024
N_HEADS = 8
QK_NOPE = 128
QK_ROPE = 64
QK_HEAD = QK_NOPE + QK_ROPE
V_HEAD = 128
D_ATTN = N_HEADS * V_HEAD
Q_LORA = 512
KV_LORA = 256
ROPE_BASE = 10000.0
IN_COLS = 4 * D_CONV + Q_LORA + KV_LORA + QK_ROPE + D_ATTN
EPS = 1e-6
ADAM_LR, ADAM_B1, ADAM_B2, ADAM_EPS, ADAM_WD, ADAM_STEP = 0.001, 0.9, 0.999, 1e-08, 0.01, 10

N_DEV = 8
LANES = 128
QK_PAD = 256
U_COLS = 6144
U_CQ, U_CKV, U_KR, U_ZA = 4096, 4608, 4864, 4928
U_TAIL = 2048
ZA_LO = U_ZA - (U_COLS - U_TAIL) - QK_ROPE
ZA_WIN = D_ATTN + LANES
CW = IN_COLS // 8
EXP_W = 896
W_LO = [(CW * d // 128) * 128 for d in range(8)]
W_OFF = [CW * d - lo for d, lo in enumerate(W_LO)]
SCALE = 1.0 / math.sqrt(QK_HEAD)
LOG2E = 1.4426950408889634
LN2 = 0.6931471805599453
NEG = -1e30
VMEM_LIMIT = 56 * 1024 * 1024

TM_ELEM = 256
NORM_ROWS = 16
NORM_GROUP = 4
TM_MM = 512
TQ = 1024
Q_CHAINS = 4
KV_SPLIT = 2

SM_MOD, SM_NG, SM_QAG, SM_KVAG, SM_QG, SM_KG, SM_CONV, SM_LOSS = 0, 6144, 8192, 8704, 8960, 9216, 9472, 12544
SM_COLS = 12672


def _params(sem=None):
    kw = dict(vmem_limit_bytes=VMEM_LIMIT)
    if sem is not None:
        kw["dimension_semantics"] = sem
    return pltpu.CompilerParams(**kw)


def _sigmoid(z):
    return 1.0 / (1.0 + jnp.exp(-z))


def _rot64(x):
    lane = lax.broadcasted_iota(jnp.int32, x.shape, 1)
    return jnp.where(lane < 32, pltpu.roll(x, 96, 1), pltpu.roll(x, 32, 1))


def _rope(x, cos, sin):
    return x * cos + _rot64(x) * sin


def _rope_t(d, cos, sin):
    return d * cos - _rot64(d) * sin


def _dot(a, b):
    return jnp.dot(a, b, preferred_element_type=F32)


def _dot_nt(a, b):
    return lax.dot_general(a, b, (((1,), (1,)), ((), ())), preferred_element_type=F32)


def _dot_tn(a, b):
    return lax.dot_general(a, b, (((0,), (0,)), ((), ())), preferred_element_type=F32)


def _my_index():
    return 4 * lax.axis_index("x") + 2 * lax.axis_index("y") + lax.axis_index("c")


ANY = pl.BlockSpec(memory_space=pl.ANY)


class _Gather:
    def __init__(self, blocks, relay=False, parts=1):
        self.relay = relay
        self.parts = parts
        self.rows = [b.shape[0] // parts for b in blocks]
        self.n = n = len(blocks) * parts
        self.out_shape = [jax.ShapeDtypeStruct((N_DEV,) + b.shape, b.dtype) for b in blocks]
        self.scratch = [pltpu.SemaphoreType.DMA((7 * n,)), pltpu.SemaphoreType.DMA((7 * n,)),
                        pltpu.SemaphoreType.DMA((n,))]

    @staticmethod
    def _places():
        x, y, c = lax.axis_index("x"), lax.axis_index("y"), lax.axis_index("c")
        return (x, y, c), (x, y, 1 - c), [(1 - x, y), (x, 1 - y), (1 - x, 1 - y)]

    def _src(self, ins, a):
        block, part = divmod(a, self.parts)
        return ins[block] if self.parts == 1 else ins[block].at[pl.ds(part * self.rows[block], self.rows[block])]

    def _dst(self, outs, a, place):
        block, part = divmod(a, self.parts)
        ref = outs[block].at[4 * place[0] + 2 * place[1] + place[2]]
        return ref if self.parts == 1 else ref.at[pl.ds(part * self.rows[block], self.rows[block])]

    def _copy(self, outs, sems, a, k, block, to, src=None):
        dst = self._dst(outs, a, block)
        return pltpu.make_async_remote_copy(
            src_ref=dst if src is None else src, dst_ref=dst, send_sem=sems[0].at[7 * a + k],
            recv_sem=sems[1].at[7 * a + k], device_id=to, device_id_type=MESH)

    def _first(self, ins, outs, sems):
        me, sibling, chips = self._places()
        first = []
        for a in range(self.n):
            first.append(self._copy(outs, sems, a, 0, me, sibling, src=self._src(ins, a)))
            first += [self._copy(outs, sems, a, 1 + j, me, (*chip, me[2]), src=self._src(ins, a))
                      for j, chip in enumerate(chips[:2] if self.relay else chips)]
        return first

    def _relays(self, outs, sems):
        if not self.relay:
            return []
        (x, y, c), _, _ = self._places()
        via = (jnp.where(c == 0, 1 - x, x), jnp.where(c == 0, y, 1 - y))
        to = (jnp.where(c == 0, x, 1 - x), jnp.where(c == 0, 1 - y, y))
        return [self._copy(outs, sems, a, 3, (*via, c), (*to, c)) for a in range(self.n)]

    def _passed(self, outs, sems):
        me, sibling, chips = self._places()
        return [self._copy(outs, sems, a, 4 + j, (*chip, me[2]), sibling)
                for a in range(self.n) for j, chip in enumerate(chips)]

    def _mine(self, ins, outs, sems):
        me, _, _ = self._places()
        return [pltpu.make_async_copy(self._src(ins, a), self._dst(outs, a, me), sems[2].at[a]) for a in range(self.n)]

    def start(self, ins, outs, sems):
        for cp in self._mine(ins, outs, sems) + self._first(ins, outs, sems):
            cp.start()

    def forward(self, ins, outs, sems):
        del ins
        me, _, chips = self._places()
        passed, relays = self._passed(outs, sems), self._relays(outs, sems)
        for a in range(self.n):
            for j, chip in enumerate(chips[:2] if self.relay else chips):
                self._copy(outs, sems, a, 1 + j, (*chip, me[2]), me).wait_recv()
                passed[3 * a + j].start()
            if self.relay:
                relays[a].start()
        if self.relay:
            for a in range(self.n):
                self._copy(outs, sems, a, 3, (*chips[2], me[2]), me).wait_recv()
                passed[3 * a + 2].start()

    def finish(self, ins, outs, sems):
        me, sibling, chips = self._places()
        for a in range(self.n):
            self._copy(outs, sems, a, 0, sibling, me).wait_recv()
            for j, chip in enumerate(chips):
                self._copy(outs, sems, a, 4 + j, (*chip, 1 - me[2]), me).wait_recv()
        for cp in self._first(ins, outs, sems) + self._relays(outs, sems) + self._passed(outs, sems):
            cp.wait_send()
        for cp in self._mine(ins, outs, sems):
            cp.wait()


class _ChipExchange:
    def __init__(self, arrays):
        self.n = n = len(arrays)
        self.out_shape = [jax.ShapeDtypeStruct(a.shape, a.dtype) for a in arrays]
        self.scratch = [pltpu.SemaphoreType.DMA((3 * n,)), pltpu.SemaphoreType.DMA((3 * n,))]

    def _copies(self, ins, outs, sems):
        x, y, c = lax.axis_index("x"), lax.axis_index("y"), lax.axis_index("c")
        return [pltpu.make_async_remote_copy(
            src_ref=ins[a].at[2 * px + py], dst_ref=outs[a].at[2 * x + y], send_sem=sems[0].at[3 * a + j],
            recv_sem=sems[1].at[3 * a + j], device_id=(px, py, c), device_id_type=MESH)
            for a in range(self.n) for j, (px, py) in enumerate([(1 - x, y), (x, 1 - y), (1 - x, 1 - y)])]

    def start(self, ins, outs, sems):
        for cp in self._copies(ins, outs, sems):
            cp.start()

    def forward(self, ins, outs, sems):
        pass

    def finish(self, ins, outs, sems):
        for cp in self._copies(ins, outs, sems):
            cp.wait()


def _all_gather(blocks, name):
    n = len(blocks)
    g = _Gather(blocks)

    def body(*refs):
        ins, outs, sems = refs[:n], refs[n:2 * n], refs[2 * n:]
        g.start(ins, outs, sems)
        g.forward(ins, outs, sems)
        g.finish(ins, outs, sems)

    return pl.pallas_call(body, name=name, out_shape=g.out_shape, in_specs=[ANY] * n, out_specs=[ANY] * n,
                          scratch_shapes=g.scratch)(*blocks)


class _SiblingExchange:
    def __init__(self, arrays, windowed):
        self.n = n = len(arrays)
        self.windowed = windowed
        self.out_shape = [jax.ShapeDtypeStruct((4, a.shape[0], EXP_W) if w else (4,) + a.shape[1:], a.dtype)
                          for a, w in zip(arrays, windowed)]
        self.scratch = [pltpu.SemaphoreType.DMA((4 * n,)), pltpu.SemaphoreType.DMA((4 * n,))]

    def _each(self, ins, outs, sems, act):
        x, y, c = lax.axis_index("x"), lax.axis_index("y"), lax.axis_index("c")

        def branch(c_val):
            for k in range(4):
                e = 2 * k + (1 - c_val)
                for a in range(self.n):
                    src = ins[a].at[:, pl.ds(W_LO[e], EXP_W)] if self.windowed[a] else ins[a].at[e]
                    act(pltpu.make_async_remote_copy(
                        src_ref=src, dst_ref=outs[a].at[k], send_sem=sems[0].at[4 * a + k], recv_sem=sems[1].at[4 * a + k],
                        device_id=(x, y, 1 - c), device_id_type=MESH))

        for c_val in (0, 1):
            pl.when(c == c_val)(functools.partial(branch, c_val))

    def start(self, ins, outs, sems):
        self._each(ins, outs, sems, lambda cp: cp.start())

    def forward(self, ins, outs, sems):
        pass

    def finish(self, ins, outs, sems):
        self._each(ins, outs, sems, lambda cp: cp.wait())


def _exchange(rider, arrays, name):
    n = len(arrays)

    def body(*refs):
        ins, outs, sems = refs[:n], refs[n:n + len(rider.out_shape)], refs[n + len(rider.out_shape):]
        rider.start(ins, outs, sems)
        rider.forward(ins, outs, sems)
        rider.finish(ins, outs, sems)

    return pl.pallas_call(body, name=name, out_shape=rider.out_shape, in_specs=[ANY] * n,
                          out_specs=[ANY] * len(rider.out_shape), scratch_shapes=rider.scratch)(*arrays)


def _add_window(dw_in, recv, lo_tiles):
    k, rows, _ = recv.shape

    def body(t_ref, w_ref, r_ref, o_ref):
        del t_ref
        o_ref[0] = (w_ref[...].astype(F32) + r_ref[0].astype(F32)).astype(o_ref.dtype)

    spec = pl.BlockSpec((1, rows, LANES), lambda i, j, t: (i, 0, j))
    grid_spec = pltpu.PrefetchScalarGridSpec(
        num_scalar_prefetch=1, grid=(k, EXP_W // LANES),
        in_specs=[pl.BlockSpec((rows, LANES), lambda i, j, t: (0, t[i] + j)), spec], out_specs=spec)
    return pl.pallas_call(
        body, name="rs_add_in", grid_spec=grid_spec, out_shape=jax.ShapeDtypeStruct(recv.shape, recv.dtype),
        compiler_params=_params(("parallel", "parallel")),
    )(lo_tiles, dw_in, recv)


def _final_sum(p, r, sel, name, unshift=False, keep_t=None):
    _, rows, cols = p.shape
    tr = 512 if rows % 512 == 0 else rows

    def body(sel_ref, p_ref, r0, r1, r2, r3, o_ref):
        own = p_ref[0].astype(F32)
        acc = None
        for k, r_ref in enumerate((r0, r1, r2, r3)):
            term = jnp.where(sel_ref[0] == k, own, r_ref[0].astype(F32))
            acc = term if acc is None else acc + term
        if unshift:
            acc = pltpu.roll(acc, sel_ref[5], 1)
        o_ref[...] = acc if keep_t is None else acc.T[:keep_t]

    def slot(k):
        return pl.BlockSpec((1, tr, cols), lambda i, t: (t[k], i, 0))

    if keep_t is None:
        out_spec, out_shape = pl.BlockSpec((tr, cols), lambda i, t: (i, 0)), (rows, cols)
    else:
        out_spec, out_shape = pl.BlockSpec((keep_t, tr), lambda i, t: (0, i)), (keep_t, rows)
    grid_spec = pltpu.PrefetchScalarGridSpec(
        num_scalar_prefetch=1, grid=(rows // tr,), in_specs=[slot(0), slot(1), slot(2), slot(3), slot(4)],
        out_specs=out_spec)
    return pl.pallas_call(
        body, name=name, grid_spec=grid_spec, out_shape=jax.ShapeDtypeStruct(out_shape, F32),
        compiler_params=_params(("parallel",)),
    )(sel, p, r, r, r, r)


def _expand_w_in(w_t, shift):
    cw, rows = w_t.shape
    tr = 256
    pad = -cw % LANES

    def body(s_ref, w_ref, o_ref):
        w = jnp.concatenate([w_ref[...], jnp.zeros((pad, tr), F32)], axis=0).T
        w = jnp.concatenate([w, jnp.zeros((tr, EXP_W - cw - pad), F32)], axis=1)
        o_ref[...] = pltpu.roll(w, s_ref[0], 1).astype(BF16)

    grid_spec = pltpu.PrefetchScalarGridSpec(
        num_scalar_prefetch=1, grid=(rows // tr,), in_specs=[pl.BlockSpec((cw, tr), lambda i, t: (0, i))],
        out_specs=pl.BlockSpec((tr, EXP_W), lambda i, t: (i, 0)))
    return pl.pallas_call(
        body, name="expand_w_in", grid_spec=grid_spec, out_shape=jax.ShapeDtypeStruct((rows, EXP_W), BF16),
        compiler_params=_params(("arbitrary",)),
    )(shift, w_t)


def _pad_wq(w_t):
    cw, rows = w_t.shape

    def body(w_ref, o_ref):
        o_ref[...] = jnp.concatenate([w_ref[...], jnp.zeros((QK_PAD - cw, rows), F32)], axis=0).T.astype(BF16)

    return pl.pallas_call(
        body, name="pad_wq", out_shape=jax.ShapeDtypeStruct((rows, QK_PAD), BF16), compiler_params=_params(),
    )(w_t)


def _merge_w_in(e):
    _, rows, _ = e.shape
    tr = 256

    def body(e_ref, o_ref):
        for t in range(U_COLS // LANES):
            lo, hi = t * LANES, (t + 1) * LANES
            parts = [e_ref[d, :, lo - W_LO[d]:hi - W_LO[d]] for d in range(N_DEV)
                     if CW * d < hi and CW * (d + 1) > lo]
            if not parts:
                tile = jnp.zeros((tr, LANES), BF16)
            elif len(parts) == 1:
                tile = parts[0]
            else:
                tile = (parts[0].astype(F32) + parts[1].astype(F32)).astype(BF16)
            o_ref[:, lo:hi] = tile

    return pl.pallas_call(
        body, name="merge_w_in", grid=(rows // tr,),
        in_specs=[pl.BlockSpec((N_DEV, tr, EXP_W), lambda i: (0, i, 0))],
        out_specs=pl.BlockSpec((tr, U_COLS), lambda i: (i, 0)), out_shape=jax.ShapeDtypeStruct((rows, U_COLS), BF16),
        compiler_params=_params(("parallel",)),
    )(e)


def _sum_leading(a, out_dtype, name):
    k, rows, cols = a.shape
    tr = min(rows, 1728 if rows % 1728 == 0 else rows)

    def body(a_ref, o_ref):
        acc = a_ref[0].astype(F32)
        for i in range(1, k):
            acc = acc + a_ref[i].astype(F32)
        o_ref[...] = acc.astype(out_dtype)

    return pl.pallas_call(
        body, name=name, grid=(rows // tr,),
        in_specs=[pl.BlockSpec((k, tr, cols), lambda i: (0, i, 0))],
        out_specs=pl.BlockSpec((tr, cols), lambda i: (i, 0)),
        out_shape=jax.ShapeDtypeStruct((rows, cols), out_dtype), compiler_params=_params(("parallel",)),
    )(a)


def _add_pairs(g, recv, core, name):
    k, rows, cols = recv.shape
    tr = 1728 if rows % 1728 == 0 else rows

    def body(c_ref, g_ref, r_ref, o_ref):
        del c_ref
        o_ref[...] = (g_ref[...].astype(F32) + r_ref[...].astype(F32)).astype(o_ref.dtype)

    spec = pl.BlockSpec((1, tr, cols), lambda i, j, c: (i, j, 0))
    grid_spec = pltpu.PrefetchScalarGridSpec(
        num_scalar_prefetch=1, grid=(k, rows // tr),
        in_specs=[pl.BlockSpec((1, tr, cols), lambda i, j, c: (2 * i + c[0], j, 0)), spec], out_specs=spec)
    return pl.pallas_call(
        body, name=name, grid_spec=grid_spec, out_shape=jax.ShapeDtypeStruct(recv.shape, recv.dtype),
        compiler_params=_params(("parallel", "parallel")),
    )(core, g, recv)


def _ada_mod(c16, ada_w_l, ada_b_l):
    def body(c_ref, w_ref, b_ref, o_ref):
        cv = c_ref[...]
        sc = (cv * _sigmoid(cv)).astype(BF16)
        o_ref[...] = _dot(sc, w_ref[...].astype(BF16)) + b_ref[...]

    return pl.pallas_call(
        body, name="ada_mod", out_shape=jax.ShapeDtypeStruct((c16.shape[0], ada_w_l.shape[1]), F32),
        compiler_params=_params(),
    )(c16, ada_w_l, ada_b_l)


def _ada_w_grad(c_t, dmod_my):
    def body(c_ref, d_ref, o_ref):
        cv = c_ref[...]
        sc = cv * _sigmoid(cv)
        acc = sc[:, 0:1] * d_ref[0:1, :]
        for b in range(1, N_DEV):
            acc = acc + sc[:, b:b + 1] * d_ref[b:b + 1, :]
        o_ref[...] = acc

    return pl.pallas_call(
        body, name="ada_w_grad", out_shape=jax.ShapeDtypeStruct((c_t.shape[0], dmod_my.shape[1]), F32),
        compiler_params=_params(),
    )(c_t, dmod_my)


def _norm_mod(x, norm_g, mod, pos_col, invf, sign, rider, rider_inputs):
    s, d = x.shape
    tm = min(TM_MM, s)
    n_in, n_out = len(rider_inputs), len(rider.out_shape)
    steps = s // tm

    def body(x_ref, g_ref, mod_ref, p_ref, f_ref, s_ref, *rest):
        r_ins, (h_ref, ht_ref, cos_ref, sin_ref) = rest[:n_in], rest[n_in:n_in + 4]
        r_outs, sems = rest[n_in + 4:n_in + 4 + n_out], rest[n_in + 4 + n_out:]
        pl.when(pl.program_id(0) == 0)(functools.partial(rider.start, r_ins, r_outs, sems))
        xv = x_ref[...]
        r = lax.rsqrt(jnp.mean(xv * xv, axis=-1, keepdims=True) + EPS)
        hn = xv * r * g_ref[...]
        hv = hn * (1.0 + mod_ref[:, d:2 * d]) + mod_ref[:, 0:d]
        h_ref[...] = hv.astype(BF16)
        ht_ref[...] = hv.T.astype(BF16)
        ang = p_ref[...].astype(F32) * f_ref[...]
        sg = s_ref[...]
        cos_ref[...] = jnp.cos(ang) * jnp.abs(sg)
        sin_ref[...] = jnp.sin(ang) * sg

        @pl.when(pl.program_id(0) == steps - 1)
        def _():
            rider.forward(r_ins, r_outs, sems)
            rider.finish(r_ins, r_outs, sems)

    row = pl.BlockSpec((1, LANES), lambda i: (0, 0))
    tab = pl.BlockSpec((tm, LANES), lambda i: (i, 0))
    return pl.pallas_call(
        body, name="norm_mod", grid=(steps,),
        in_specs=[pl.BlockSpec((tm, d), lambda i: (i, 0)), pl.BlockSpec((1, d), lambda i: (0, 0)),
                  pl.BlockSpec((1, 3 * d), lambda i: (0, 0)), pl.BlockSpec((tm, 1), lambda i: (i, 0)), row, row]
        + [ANY] * n_in,
        out_specs=[pl.BlockSpec((tm, d), lambda i: (i, 0)), pl.BlockSpec((d, tm), lambda i: (0, i)), tab, tab] + [ANY] * n_out,
        out_shape=[jax.ShapeDtypeStruct((s, d), BF16), jax.ShapeDtypeStruct((d, s), BF16),
                   jax.ShapeDtypeStruct((s, LANES), F32), jax.ShapeDtypeStruct((s, LANES), F32)] + rider.out_shape,
        scratch_shapes=rider.scratch, compiler_params=_params(("arbitrary",)),
    )(x, norm_g, mod, pos_col, invf, sign, *rider_inputs)


def _matmul(a, b, *, nt, out_dtype, tm, tn, name, rider=None, rider_inputs=(), a_resident=False):
    m, kdim = a.shape
    n = b.shape[0] if nt else b.shape[1]
    tm, tn = min(tm, m), min(tn, n)
    n_in = len(rider_inputs)
    n_out = len(rider.out_shape) if rider else 0
    m_steps, n_steps = m // tm, n // tn
    steps = n_steps * m_steps
    inner = n_steps if a_resident else m_steps
    tile = (lambda o, i: (o, i)) if a_resident else (lambda o, i: (i, o))

    def body(a_ref, b_ref, *rest):
        r_ins, o_ref, r_outs, sems = rest[:n_in], rest[n_in], rest[n_in + 1:n_in + 1 + n_out], rest[n_in + 1 + n_out:]
        step = pl.program_id(0) * inner + pl.program_id(1)
        if rider:
            pl.when(step == 0)(functools.partial(rider.start, r_ins, r_outs, sems))
            pl.when(step == steps // 2)(functools.partial(rider.forward, r_ins, r_outs, sems))
        o = _dot_nt(a_ref[...], b_ref[...]) if nt else _dot(a_ref[...], b_ref[...])
        o_ref[...] = o.astype(out_dtype)
        if rider:
            pl.when(step == steps - 1)(functools.partial(rider.finish, r_ins, r_outs, sems))

    if nt:
        b_spec = pl.BlockSpec((tn, kdim), lambda o, i: (tile(o, i)[1], 0))
    else:
        b_spec = pl.BlockSpec((kdim, tn), lambda o, i: (0, tile(o, i)[1]))
    out = pl.pallas_call(
        body, name=name, grid=(m_steps, n_steps) if a_resident else (n_steps, m_steps),
        in_specs=[pl.BlockSpec((tm, kdim), lambda o, i: (tile(o, i)[0], 0)), b_spec] + [ANY] * n_in,
        out_specs=[pl.BlockSpec((tm, tn), tile)] + [ANY] * n_out,
        out_shape=[jax.ShapeDtypeStruct((m, n), out_dtype)] + (rider.out_shape if rider else []),
        scratch_shapes=rider.scratch if rider else [],
        compiler_params=_params(("arbitrary", "arbitrary") if rider else ("parallel", "parallel")),
    )(a, b, *rider_inputs)
    return out if rider else out[0]


HALO = 16


def _conv_specs(tm):
    def col(j):
        return pl.BlockSpec((tm, D_CONV), lambda i: (i, j))

    def prev(j):
        return pl.BlockSpec((HALO, D_CONV), lambda i: (jnp.maximum(i * (tm // HALO) - 1, 0), j))

    return [col(0), col(1), col(2), col(3), prev(0), prev(2)]


def _conv_y(xc_ref, bc_ref, cc_ref, zc_ref, xp_ref, cp_ref, w_ref, first):
    uc = cc_ref[...].astype(F32) * xc_ref[...].astype(F32)
    up = jnp.where(first, 0.0, cp_ref[...].astype(F32) * xp_ref[...].astype(F32))
    full = jnp.concatenate([up, uc], axis=0)
    u1 = pltpu.roll(full, 1, 0)[HALO:]
    u2 = pltpu.roll(full, 2, 0)[HALO:]
    w = w_ref[...]
    conv = w[0:1] * u2 + w[1:2] * u1 + w[2:3] * uc
    z = zc_ref[...].astype(F32)
    return bc_ref[...].astype(F32) * conv * (z * _sigmoid(z))


def _conv_bwd(u, dyc, conv_w):
    s = u.shape[0]
    tm = min(TM_ELEM, s)
    cb = D_CONV
    nt = s // tm

    def body(xc_ref, bc_ref, cc_ref, zc_ref, xp_ref, cp_ref, bn_ref, zn_ref, dy_ref, dyn_ref, w_ref, du_ref, dw_ref):
        i = pl.program_id(0)
        xc, cc = xc_ref[...].astype(F32), cc_ref[...].astype(F32)
        bc, z = bc_ref[...].astype(F32), zc_ref[...].astype(F32)
        uc = cc * xc
        up = jnp.where(i == 0, 0.0, cp_ref[...].astype(F32) * xp_ref[...].astype(F32))
        full = jnp.concatenate([up, uc], axis=0)
        u1 = pltpu.roll(full, 1, 0)[HALO:]
        u2 = pltpu.roll(full, 2, 0)[HALO:]
        w = w_ref[...]
        conv = w[0:1] * u2 + w[1:2] * u1 + w[2:3] * uc
        sg = _sigmoid(z)
        sz = z * sg
        dy = dy_ref[...].astype(F32)
        dconv = dy * bc * sz
        zn = zn_ref[...].astype(F32)
        dnext = dyn_ref[...].astype(F32) * bn_ref[...].astype(F32) * (zn * _sigmoid(zn))
        dnext = jnp.where(i == nt - 1, 0.0, dnext)
        fullb = jnp.concatenate([dconv, dnext], axis=0)
        nb = tm + HALO
        d1 = pltpu.roll(fullb, nb - 1, 0)[:tm]
        d2 = pltpu.roll(fullb, nb - 2, 0)[:tm]
        duc = w[2:3] * dconv + w[1:2] * d1 + w[0:1] * d2
        dzc = dy * bc * conv * (sg * (1.0 + z * (1.0 - sg)))
        du_ref[...] = jnp.concatenate([duc * cc, dy * conv * sz, duc * xc, dzc], axis=1).astype(BF16)
        dw = jnp.concatenate([jnp.sum(dconv * u2, axis=0, keepdims=True), jnp.sum(dconv * u1, axis=0, keepdims=True),
                              jnp.sum(dconv * uc, axis=0, keepdims=True), jnp.zeros((5, cb), F32)], axis=0)

        @pl.when(i == 0)
        def _():
            dw_ref[...] = dw

        @pl.when(i > 0)
        def _():
            dw_ref[...] += dw

    def col(j):
        return pl.BlockSpec((tm, cb), lambda i: (i, j))

    def prev(j):
        return pl.BlockSpec((HALO, cb), lambda i: (jnp.maximum(i * (tm // HALO) - 1, 0), j))

    def nxt(j):
        return pl.BlockSpec((HALO, cb), lambda i: (jnp.minimum((i + 1) * (tm // HALO), s // HALO - 1), j))

    return pl.pallas_call(
        body, name="conv_bwd", grid=(nt,),
        in_specs=[col(0), col(1), col(2), col(3), prev(0), prev(2), nxt(1), nxt(3), col(0), nxt(0),
                  pl.BlockSpec((3, cb), lambda i: (0, 0))],
        out_specs=[pl.BlockSpec((tm, 4 * cb), lambda i: (i, 0)), pl.BlockSpec((8, cb), lambda i: (0, 0))],
        out_shape=[jax.ShapeDtypeStruct((s, U_COLS), BF16), jax.ShapeDtypeStruct((8, cb), F32)],
        compiler_params=_params(("arbitrary",)),
    )(u, u, u, u, u, u, u, u, dyc, dyc, conv_w)


def _qkv_specs(tm):
    return [pl.BlockSpec((tm, Q_LORA), lambda i: (i, U_CQ // Q_LORA)),
            pl.BlockSpec((tm, KV_LORA), lambda i: (i, U_CKV // KV_LORA)),
            pl.BlockSpec((tm, LANES), lambda i: (i, U_KR // LANES)),
            pl.BlockSpec((tm, LANES), lambda i: (i, 0)), pl.BlockSpec((tm, LANES), lambda i: (i, 0))]


def _full(shape):
    return pl.BlockSpec(shape, lambda i: (0,) * len(shape))


def _k_rope_lanes(blk):
    lane = lax.broadcasted_iota(jnp.int32, blk.shape, 1)
    return jnp.where(lane < QK_ROPE, blk, 0.0)


def _qkv_fwd(u, cos, sin, wq, wkv, qag, kvag, qg, kg):
    s = u.shape[0]
    tm = min(TM_ELEM, s)

    def body(cq_ref, ckv_ref, kr_ref, cos_ref, sin_ref, wq_ref, wkv_ref, qag_ref, kvag_ref, qg_ref, kg_ref,
             q_ref, k_ref, v_ref):
        cq = cq_ref[...].astype(F32)
        cqn = (cq * lax.rsqrt(jnp.mean(cq * cq, axis=-1, keepdims=True) + EPS) * qag_ref[...]).astype(BF16)
        ckv = ckv_ref[...].astype(F32)
        ckvn = (ckv * lax.rsqrt(jnp.mean(ckv * ckv, axis=-1, keepdims=True) + EPS) * kvag_ref[...]).astype(BF16)
        kr = _k_rope_lanes(kr_ref[...].astype(F32))
        cosv, sinv, qgv, kgv = cos_ref[...], sin_ref[...], qg_ref[...], kg_ref[...]
        ss_r = jnp.sum(kr * kr, axis=-1, keepdims=True)
        krr = _rope(kr * kgv[:, QK_NOPE:], cosv, sinv)
        qf = _dot(cqn, wq_ref[...])
        kvf = _dot(ckvn, wkv_ref[...])
        heads = range(N_HEADS)
        qh = [qf[:, QK_PAD * h:QK_PAD * (h + 1)] for h in heads]
        kn = [kvf[:, 2 * V_HEAD * h:2 * V_HEAD * h + QK_NOPE] for h in heads]
        rq = [lax.rsqrt(jnp.sum(qh[h] * qh[h], axis=-1, keepdims=True) * (1.0 / QK_HEAD) + EPS) for h in heads]
        rk = [lax.rsqrt((jnp.sum(kn[h] * kn[h], axis=-1, keepdims=True) + ss_r) * (1.0 / QK_HEAD) + EPS) for h in heads]
        for h in heads:
            qn = qh[h] * rq[h] * qgv
            qo = jnp.concatenate([qn[:, :QK_NOPE], _rope(qn[:, QK_NOPE:], cosv, sinv)], axis=1) * (SCALE * LOG2E)
            q_ref[h] = qo.astype(BF16)
            vh = kvf[:, 2 * V_HEAD * h + QK_NOPE:2 * V_HEAD * (h + 1)]
            k_ref[h] = jnp.concatenate([kn[h] * kgv[:, :QK_NOPE] * rk[h], krr * rk[h]], axis=1).astype(BF16)
            v_ref[h] = jnp.concatenate([vh, jnp.ones_like(vh)], axis=1).astype(BF16)

    return pl.pallas_call(
        body, name="qkv_fwd", grid=(s // tm,),
        in_specs=_qkv_specs(tm) + [_full((Q_LORA, N_HEADS * QK_PAD)), _full((KV_LORA, 2 * D_ATTN)),
                                   _full((1, Q_LORA)), _full((1, KV_LORA)), _full((1, QK_PAD)), _full((1, QK_PAD))],
        out_specs=[pl.BlockSpec((N_HEADS, tm, QK_PAD), lambda i: (0, i, 0)),
                   pl.BlockSpec((N_HEADS, tm, QK_PAD), lambda i: (0, i, 0)),
                   pl.BlockSpec((N_HEADS, tm, 2 * V_HEAD), lambda i: (0, i, 0))],
        out_shape=[jax.ShapeDtypeStruct((N_HEADS, s, QK_PAD), BF16), jax.ShapeDtypeStruct((N_HEADS, s, QK_PAD), BF16),
                   jax.ShapeDtypeStruct((N_HEADS, s, 2 * V_HEAD), BF16)],
        compiler_params=_params(("parallel",)),
    )(u, u, u, cos, sin, wq, wkv, qag, kvag, qg, kg)


def _qkv_bwd(u, cos, sin, dq, dk, dv, dza, wq, wkv, qag, kvag, qg, kg, du):
    s = u.shape[0]
    tm = min(TM_ELEM, s)
    nt = s // tm

    def body(cq_ref, ckv_ref, kr_ref, cos_ref, sin_ref, dq_ref, dk_ref, dv_ref, dza_ref, wq_ref, wkv_ref, qag_ref,
             kvag_ref, qg_ref, kg_ref, du_in, du_ref, dwq_ref, dwkv_ref, dqag_ref, dkvag_ref, dqg_ref, dkg_ref,
             dwq_acc, dwkv_acc):
        del du_in
        i = pl.program_id(0)

        @pl.when(i == 0)
        def _():
            dwq_acc[...] = jnp.zeros_like(dwq_acc)
            dwkv_acc[...] = jnp.zeros_like(dwkv_acc)

        cq = cq_ref[...].astype(F32)
        rqa = lax.rsqrt(jnp.mean(cq * cq, axis=-1, keepdims=True) + EPS)
        xq = cq * rqa
        qagv = qag_ref[...]
        cqn = (xq * qagv).astype(BF16)
        ckv = ckv_ref[...].astype(F32)
        rkva = lax.rsqrt(jnp.mean(ckv * ckv, axis=-1, keepdims=True) + EPS)
        xkv = ckv * rkva
        kvagv = kvag_ref[...]
        ckvn = (xkv * kvagv).astype(BF16)
        kr = _k_rope_lanes(kr_ref[...].astype(F32))
        cosv, sinv, qgv, kgv = cos_ref[...], sin_ref[...], qg_ref[...], kg_ref[...]
        ss_r = jnp.sum(kr * kr, axis=-1, keepdims=True)
        dqg = jnp.zeros((1, QK_PAD), F32)
        dkg = jnp.zeros((1, QK_PAD), F32)
        dkr = jnp.zeros((tm, LANES), F32)
        qf = _dot(cqn, wq_ref[...])
        kvf = _dot(ckvn, wkv_ref[...])
        heads = range(N_HEADS)
        qh = [qf[:, QK_PAD * h:QK_PAD * (h + 1)] for h in heads]
        kn = [kvf[:, 2 * V_HEAD * h:2 * V_HEAD * h + QK_NOPE] for h in heads]
        rq = [lax.rsqrt(jnp.sum(qh[h] * qh[h], axis=-1, keepdims=True) * (1.0 / QK_HEAD) + EPS) for h in heads]
        rk = [lax.rsqrt((jnp.sum(kn[h] * kn[h], axis=-1, keepdims=True) + ss_r) * (1.0 / QK_HEAD) + EPS) for h in heads]
        xh = [qh[h] * rq[h] for h in heads]
        xk = [jnp.concatenate([kn[h], kr], axis=1) * rk[h] for h in heads]
        dyq, dyk = [], []
        for h in heads:
            g = dq_ref[h].astype(F32) * SCALE
            dyq.append(jnp.concatenate([g[:, :QK_NOPE], _rope_t(g[:, QK_NOPE:], cosv, sinv)], axis=1))
            gk = dk_ref[h].astype(F32)
            dyk.append(jnp.concatenate([gk[:, :QK_NOPE], _rope_t(gk[:, QK_NOPE:], cosv, sinv)], axis=1))
        for h in heads:
            dqg = dqg + jnp.sum(dyq[h] * xh[h], axis=0, keepdims=True)
            dkg = dkg + jnp.sum(dyk[h] * xk[h], axis=0, keepdims=True)
        gdy = [dyq[h] * qgv for h in heads]
        gdyk = [dyk[h] * kgv for h in heads]
        tq_ = [jnp.sum(gdy[h] * xh[h], axis=-1, keepdims=True) * (1.0 / QK_HEAD) for h in heads]
        tk_ = [jnp.sum(gdyk[h] * xk[h], axis=-1, keepdims=True) * (1.0 / QK_HEAD) for h in heads]
        dqf = [(rq[h] * (gdy[h] - xh[h] * tq_[h])).astype(BF16) for h in heads]
        dkvf = []
        for h in heads:
            dxk = rk[h] * (gdyk[h] - xk[h] * tk_[h])
            dkr = dkr + dxk[:, QK_NOPE:]
            dkvf += [dxk[:, :QK_NOPE].astype(BF16), dv_ref[h]]
        dqf_b, dkvf_b = jnp.concatenate(dqf, axis=1), jnp.concatenate(dkvf, axis=1)
        dwq_acc[...] += _dot_tn(cqn, dqf_b)
        dwkv_acc[...] += _dot_tn(ckvn, dkvf_b)
        dcqn = _dot_nt(dqf_b, wq_ref[...])
        dckvn = _dot_nt(dkvf_b, wkv_ref[...])
        dqag = jnp.sum(dcqn * xq, axis=0, keepdims=True)
        dkvag = jnp.sum(dckvn * xkv, axis=0, keepdims=True)
        gq = dcqn * qagv
        dcq = rqa * (gq - xq * jnp.mean(gq * xq, axis=-1, keepdims=True))
        gkv = dckvn * kvagv
        dckv = rkva * (gkv - xkv * jnp.mean(gkv * xkv, axis=-1, keepdims=True))
        win = pltpu.roll(jnp.concatenate([dza_ref[...].astype(F32), jnp.zeros((tm, LANES), F32)], axis=1), QK_ROPE, 1)
        win = win + jnp.concatenate([dkr, jnp.zeros((tm, D_ATTN), F32)], axis=1)
        du_ref[...] = jnp.concatenate([dcq, dckv, win, jnp.zeros((tm, U_TAIL - ZA_LO - ZA_WIN), F32)], axis=1).astype(BF16)

        @pl.when(i == 0)
        def _():
            dqag_ref[...] = dqag
            dkvag_ref[...] = dkvag
            dqg_ref[...] = dqg
            dkg_ref[...] = dkg

        @pl.when(i > 0)
        def _():
            dqag_ref[...] += dqag
            dkvag_ref[...] += dkvag
            dqg_ref[...] += dqg
            dkg_ref[...] += dkg

        @pl.when(i == nt - 1)
        def _():
            dwq_ref[...] = dwq_acc[...].astype(BF16)
            dwkv_ref[...] = dwkv_acc[...].astype(BF16)

    head = lambda w: pl.BlockSpec((N_HEADS, tm, w), lambda i: (0, i, 0))
    wq_shape, wkv_shape = (Q_LORA, N_HEADS * QK_PAD), (KV_LORA, 2 * D_ATTN)
    return pl.pallas_call(
        body, name="qkv_bwd", grid=(nt,),
        in_specs=_qkv_specs(tm) + [head(QK_PAD), head(QK_PAD), head(V_HEAD), pl.BlockSpec((tm, D_ATTN), lambda i: (i, 0)),
                                   _full(wq_shape), _full(wkv_shape), _full((1, Q_LORA)), _full((1, KV_LORA)),
                                   _full((1, QK_PAD)), _full((1, QK_PAD)), ANY],
        out_specs=[pl.BlockSpec((tm, U_TAIL), lambda i: (i, U_COLS // U_TAIL - 1)), _full(wq_shape), _full(wkv_shape),
                   _full((1, Q_LORA)), _full((1, KV_LORA)), _full((1, QK_PAD)), _full((1, QK_PAD))],
        out_shape=[jax.ShapeDtypeStruct(du.shape, du.dtype), jax.ShapeDtypeStruct(wq_shape, BF16),
                   jax.ShapeDtypeStruct(wkv_shape, BF16), jax.ShapeDtypeStruct((1, Q_LORA), F32),
                   jax.ShapeDtypeStruct((1, KV_LORA), F32), jax.ShapeDtypeStruct((1, QK_PAD), F32),
                   jax.ShapeDtypeStruct((1, QK_PAD), F32)],
        scratch_shapes=[pltpu.VMEM(wq_shape, F32), pltpu.VMEM(wkv_shape, F32)],
        input_output_aliases={15: 0}, compiler_params=_params(("arbitrary",)),
    )(u, u, u, cos, sin, dq, dk, dv, dza, wq, wkv, qag, kvag, qg, kg, du)


def _flash_fwd(q, k, v):
    nh, s, _ = q.shape
    tq = min(TQ, s)
    nkv = KV_SPLIT
    tk = tq // nkv
    nq = s // tq
    nch = Q_CHAINS
    tc = tq // nch

    def body(q_ref, k_ref, v_ref, o_ref, lse_ref):
        i = pl.program_id(1)
        chains = [q_ref[0, r * tc:(r + 1) * tc, :] for r in range(nch)]

        def unit(r, j, carry, shift=None):
            m, acc = carry
            rows = pl.ds(pl.multiple_of(j * tk, tk), tk)
            sc = _dot_nt(chains[r], k_ref[0, rows, :])
            if shift is not None:
                qi = lax.broadcasted_iota(jnp.int32, sc.shape, 0)
                ki = lax.broadcasted_iota(jnp.int32, sc.shape, 1) + shift
                sc = jnp.where(ki <= qi, sc, NEG)
            m_new = jnp.maximum(m, jnp.max(sc, axis=-1, keepdims=True))
            p = jnp.exp2(sc - m_new).astype(BF16)
            return m_new, jnp.exp2(m - m_new) * acc + _dot(p, v_ref[0, rows, :])

        def trip(p, carry):
            for b in range(nkv):
                carry = tuple(unit(r, nkv * p + b, cr) for r, cr in enumerate(carry))
            return carry

        init = (jnp.full((tc, 1), NEG, F32), jnp.zeros((tc, 2 * V_HEAD), F32))
        carry = list(lax.fori_loop(0, i, trip, (init,) * nch))
        for b in range(nkv):
            for r in range(nch):
                shift = b * tk - r * tc
                if shift < tc:
                    carry[r] = unit(r, nkv * i + b, carry[r], None if shift + tk - 1 <= 0 else shift)
        for r, (m, acc) in enumerate(carry):
            l = acc[:, V_HEAD:]
            o_ref[r * tc:(r + 1) * tc, :] = (acc[:, :V_HEAD] / l).astype(BF16)
            lse = m + jnp.log(l[:, 0:1]) * LOG2E
            lse_ref[0, :, r * tc:(r + 1) * tc] = jnp.broadcast_to(lse, (tc, LANES)).T[0:1, :]

    return pl.pallas_call(
        body, name="flash_fwd", grid=(nh, nq),
        in_specs=[pl.BlockSpec((1, tq, QK_PAD), lambda h, i: (h, i, 0)),
                  pl.BlockSpec((1, s, QK_PAD), lambda h, i: (h, 0, 0)),
                  pl.BlockSpec((1, s, 2 * V_HEAD), lambda h, i: (h, 0, 0))],
        out_specs=[pl.BlockSpec((tq, V_HEAD), lambda h, i: (i, h)), pl.BlockSpec((1, 1, tq), lambda h, i: (h, 0, i))],
        out_shape=[jax.ShapeDtypeStruct((s, nh * V_HEAD), BF16), jax.ShapeDtypeStruct((nh, 1, s), F32)],
        compiler_params=_params(("parallel", "arbitrary")),
    )(q, k, v)


def _flash_bwd(q, k, v, do, lse, delta):
    nh, s, _ = q.shape
    tq = min(TQ, s)
    nq = s // tq

    def body(q_ref, k_ref, v_ref, do_ref, lse_ref, dl_ref, dq_ref, dk_ref, dv_ref, dq_acc):
        j = nq - 1 - pl.program_id(1)

        @pl.when(j == nq - 1)
        def _():
            dq_acc[...] = jnp.zeros_like(dq_acc)

        kj, vj = k_ref[0], v_ref[0]

        def block(kk, vv, qq, dd, lse, dl, masked):
            st = _dot_nt(kk, qq)
            pt = jnp.exp2(st - lse)
            if masked:
                ki = lax.broadcasted_iota(jnp.int32, st.shape, 0)
                qx = lax.broadcasted_iota(jnp.int32, st.shape, 1)
                pt = jnp.where(ki <= qx, pt, 0.0)
            ddv = _dot(pt.astype(BF16), dd)
            dst = (pt * (_dot_nt(vv, dd) - dl)).astype(BF16)
            ddq = _dot_tn(dst, kk)
            return _dot(dst, qq), ddv, ddq

        def step(i, carry):
            dk, dv = carry
            rows = pl.ds(pl.multiple_of(i * tq, tq), tq)
            ddk, ddv, ddq = block(kj, vj, q_ref[0, rows, :], do_ref[rows, :], lse_ref[0, pl.ds(i, 1), :],
                                  dl_ref[0, pl.ds(i, 1), :], False)
            dq_acc[rows, :] += ddq
            return dk + ddk, dv + ddv

        th = tq // 2
        lse_j, dl_j = lse_ref[0, pl.ds(j, 1), :], dl_ref[0, pl.ds(j, 1), :]
        parts = []
        for kh, qh, masked in ((0, 0, True), (0, 1, False), (1, 1, True)):
            rows = pl.ds(pl.multiple_of(j * tq + qh * th, th), th)
            ks, qs = slice(kh * th, (kh + 1) * th), slice(qh * th, (qh + 1) * th)
            ddk, ddv, ddq = block(kj[ks], vj[ks], q_ref[0, rows, :], do_ref[rows, :], lse_j[:, qs], dl_j[:, qs], masked)
            dq_acc[rows, :] += ddq
            parts.append((ddk, ddv))
        carry = (jnp.concatenate([parts[0][0] + parts[1][0], parts[2][0]], axis=0),
                 jnp.concatenate([parts[0][1] + parts[1][1], parts[2][1]], axis=0))
        dk, dv = lax.fori_loop(j + 1, nq, step, carry)
        dk_ref[0] = (dk * LN2).astype(BF16)
        dv_ref[0] = dv.astype(BF16)

        @pl.when(j == 0)
        def _():
            dq_ref[0] = dq_acc[...].astype(BF16)

    return pl.pallas_call(
        body, name="flash_bwd", grid=(nh, nq),
        in_specs=[pl.BlockSpec((1, s, QK_PAD), lambda h, j: (h, 0, 0)),
                  pl.BlockSpec((1, tq, QK_PAD), lambda h, g: (h, nq - 1 - g, 0)),
                  pl.BlockSpec((1, tq, V_HEAD), lambda h, g: (h, nq - 1 - g, 0)),
                  pl.BlockSpec((s, V_HEAD), lambda h, j: (0, h)),
                  pl.BlockSpec((1, nq, tq), lambda h, j: (h, 0, 0)),
                  pl.BlockSpec((1, nq, tq), lambda h, j: (h, 0, 0))],
        out_specs=[pl.BlockSpec((1, s, QK_PAD), lambda h, j: (h, 0, 0)),
                   pl.BlockSpec((1, tq, QK_PAD), lambda h, g: (h, nq - 1 - g, 0)),
                   pl.BlockSpec((1, tq, V_HEAD), lambda h, g: (h, nq - 1 - g, 0))],
        out_shape=[jax.ShapeDtypeStruct((nh, s, QK_PAD), BF16), jax.ShapeDtypeStruct((nh, s, QK_PAD), BF16),
                   jax.ShapeDtypeStruct((nh, s, V_HEAD), BF16)],
        scratch_shapes=[pltpu.VMEM((s, QK_PAD), F32)],
        compiler_params=_params(("parallel", "arbitrary")),
    )(q, k, v, do, lse, delta)


def _tail(x, target, o, u, mod, w_out, conv_w):
    s, d = x.shape
    tm = min(TM_ELEM, s)

    def body(x_ref, t_ref, o_ref, za_ref, mod_ref, w_ref, xc_ref, bc_ref, cc_ref, zc_ref, xp_ref, cp_ref, cw_ref,
             gx_ref, dy_ref, ycat_ref, dyc_ref, do_ref, du_ref, delta_ref, dgate_ref, loss_ref):
        i = pl.program_id(0)
        za = pltpu.roll(za_ref[:, ZA_LO:ZA_LO + ZA_WIN].astype(F32), ZA_WIN - QK_ROPE, 1)[:, :D_ATTN]
        ov = o_ref[...].astype(F32)
        sg = _sigmoid(za)
        sl = za * sg
        ya = ov * sl
        y = _dot(ya.astype(BF16), w_ref[D_CONV:, :])
        yc = _conv_y(xc_ref, bc_ref, cc_ref, zc_ref, xp_ref, cp_ref, cw_ref, i == 0)
        y = y + _dot(yc.astype(BF16), w_ref[:D_CONV, :])
        ycat_ref[...] = jnp.concatenate([yc.T, ya.T], axis=0).astype(BF16)
        gate = mod_ref[:, 2 * d:3 * d]
        e = x_ref[...] + gate * y - t_ref[...]
        dout = e * (1.0 / d)
        gx_ref[...] = dout
        dy = (dout * gate).astype(BF16)
        dy_ref[...] = dy
        dycat = _dot_nt(dy, w_ref[...])
        dyc_ref[...] = dycat[:, :D_CONV].astype(BF16)
        dya = dycat[:, D_CONV:]
        dov = dya * sl
        do_ref[...] = dov.astype(BF16)
        du_ref[...] = (dya * ov * (sg * (1.0 + za * (1.0 - sg)))).astype(BF16)
        prod_t = (dov * ov).T
        for h in range(N_HEADS):
            delta_ref[h] = jnp.sum(prod_t[V_HEAD * h:V_HEAD * (h + 1), :], axis=0, keepdims=True)
        dgate = jnp.sum(dout * y, axis=0, keepdims=True)
        part = jnp.sum(jnp.sum(e * e, axis=0, keepdims=True), axis=1, keepdims=True) * (0.5 / d)
        part = jnp.broadcast_to(part, (1, LANES))

        @pl.when(i == 0)
        def _():
            dgate_ref[...] = dgate
            loss_ref[...] = part

        @pl.when(i > 0)
        def _():
            dgate_ref[...] += dgate
            loss_ref[...] += part

    tok = lambda w: pl.BlockSpec((tm, w), lambda i: (i, 0))
    return pl.pallas_call(
        body, name="tail", grid=(s // tm,),
        in_specs=[tok(d), tok(d), tok(D_ATTN), pl.BlockSpec((tm, U_TAIL), lambda i: (i, U_COLS // U_TAIL - 1)),
                  _full((1, 3 * d)), _full((d, d))] + _conv_specs(tm) + [_full((3, D_CONV))],
        out_specs=[tok(d), tok(d), pl.BlockSpec((d, tm), lambda i: (0, i)), tok(D_CONV), tok(D_ATTN), tok(D_ATTN),
                   pl.BlockSpec((N_HEADS, 1, tm), lambda i: (0, 0, i)), _full((1, d)), _full((1, LANES))],
        out_shape=[jax.ShapeDtypeStruct((s, d), F32), jax.ShapeDtypeStruct((s, d), BF16),
                   jax.ShapeDtypeStruct((d, s), BF16), jax.ShapeDtypeStruct((s, D_CONV), BF16),
                   jax.ShapeDtypeStruct((s, D_ATTN), BF16), jax.ShapeDtypeStruct((s, D_ATTN), BF16),
                   jax.ShapeDtypeStruct((N_HEADS, 1, s), F32), jax.ShapeDtypeStruct((1, d), F32),
                   jax.ShapeDtypeStruct((1, LANES), F32)],
        compiler_params=_params(("arbitrary",)),
    )(x, target, o, u, mod, w_out, u, u, u, u, u, u, conv_w)


def _norm_bwd(x, dh, gx1, norm_g, mod):
    s, d = x.shape
    tm = min(TM_MM, s)

    def body(x_ref, dh_ref, gx_ref, g_ref, mod_ref, o_ref, dshift_ref, dscale_ref, dg_ref):
        i = pl.program_id(0)
        gv, sc1 = g_ref[...], 1.0 + mod_ref[:, d:2 * d]
        gsc = gv * sc1
        half = NORM_ROWS // 2

        def group(c, acc):
            a_dh, a_dhxn = acc
            ks = range(NORM_GROUP)
            rows = [pl.ds(pl.multiple_of((c * NORM_GROUP + k) * NORM_ROWS, NORM_ROWS), NORM_ROWS) for k in ks]
            xv = [x_ref[rows[k], :] for k in ks]
            dhv = [dh_ref[rows[k], :].astype(F32) for k in ks]
            r = [lax.rsqrt(jnp.mean(xv[k] * xv[k], axis=-1, keepdims=True) + EPS) for k in ks]
            xn = [xv[k] * r[k] for k in ks]
            dxn = [dhv[k] * gsc for k in ks]
            t = [jnp.mean(dxn[k] * xn[k], axis=-1, keepdims=True) for k in ks]
            for k in ks:
                o_ref[rows[k], :] = gx_ref[rows[k], :] + r[k] * (dxn[k] - xn[k] * t[k])
                dhxn = dhv[k] * xn[k]
                a_dh = a_dh + dhv[k][:half] + dhv[k][half:]
                a_dhxn = a_dhxn + dhxn[:half] + dhxn[half:]
            return a_dh, a_dhxn

        zero = jnp.zeros((half, d), F32)
        a_dh, a_dhxn = lax.fori_loop(0, tm // (NORM_ROWS * NORM_GROUP), group, (zero, zero))
        dshift = jnp.sum(a_dh, axis=0, keepdims=True)
        s_dhxn = jnp.sum(a_dhxn, axis=0, keepdims=True)
        dscale, dg = s_dhxn * gv, s_dhxn * sc1

        @pl.when(i == 0)
        def _():
            dshift_ref[...] = dshift
            dscale_ref[...] = dscale
            dg_ref[...] = dg

        @pl.when(i > 0)
        def _():
            dshift_ref[...] += dshift
            dscale_ref[...] += dscale
            dg_ref[...] += dg

    tok = pl.BlockSpec((tm, d), lambda i: (i, 0))
    row = jax.ShapeDtypeStruct((1, d), F32)
    return pl.pallas_call(
        body, name="norm_bwd", grid=(s // tm,),
        in_specs=[tok, tok, tok, _full((1, d)), _full((1, 3 * d))],
        out_specs=[tok, _full((1, d)), _full((1, d)), _full((1, d))],
        out_shape=[jax.ShapeDtypeStruct((s, d), F32), row, row, row],
        compiler_params=_params(("arbitrary",)),
    )(x, dh, gx1, norm_g, mod)


def _adamw(w, g, m, v, name):
    rows, cols = w.shape
    tr = 256 if rows % 256 == 0 else rows
    tc = 512 if (rows > 256 and tr == rows and cols % 512 == 0) else cols

    def body(w_ref, g_ref, m_ref, v_ref, d_ref, nm_ref, nv_ref):
        gv = g_ref[...]
        nm = ADAM_B1 * m_ref[...] + (1.0 - ADAM_B1) * gv
        nv = ADAM_B2 * v_ref[...] + (1.0 - ADAM_B2) * (gv * gv)
        m_hat = nm / (1.0 - ADAM_B1 ** ADAM_STEP)
        v_hat = nv / (1.0 - ADAM_B2 ** ADAM_STEP)
        d_ref[...] = -ADAM_LR * (m_hat / (jnp.sqrt(v_hat) + ADAM_EPS) + ADAM_WD * w_ref[...])
        nm_ref[...] = nm
        nv_ref[...] = nv

    spec = pl.BlockSpec((tr, tc), lambda i, j: (i, j))
    shape = jax.ShapeDtypeStruct((rows, cols), F32)
    return pl.pallas_call(
        body, name=name, grid=(rows // tr, cols // tc), in_specs=[spec] * 4, out_specs=[spec] * 3, out_shape=[shape] * 3,
        compiler_params=_params(("parallel", "parallel")),
    )(w, g, m, v)


def _pad_cols(a, n):
    return jnp.pad(a, ((0, 0), (0, n - a.shape[1])))


def kernel(x, c, positions, ada_w, ada_b, norm_g, w_in, conv_w, q_a_g, w_q_b, kv_a_g, w_kv_b, q_g, k_g, w_out, loss_target, m_ada_w, m_ada_b, m_norm_g, m_w_in, m_conv_w, m_q_a_g, m_w_q_b, m_kv_a_g, m_w_kv_b, m_q_g, m_k_g, m_w_out, v_ada_w, v_ada_b, v_norm_g, v_w_in, v_conv_w, v_q_a_g, v_w_q_b, v_kv_a_g, v_w_kv_b, v_q_g, v_k_g, v_w_out):
    me = _my_index()
    s = x.shape[1]
    nq = s // min(TQ, s)
    x2, tgt = x[0], loss_target[0]
    w_in_l, w_q_l, w_kv_l, w_out_l, conv_l, ada_w_l = w_in[0], w_q_b[0], w_kv_b[0], w_out[0], conv_w[0], ada_w[0]
    ada_cols = ada_w_l.shape[1]

    small = jnp.concatenate([c.reshape(-1, LANES), conv_l.reshape(-1, LANES), jnp.zeros((5, LANES), F32)], axis=0)
    (small_g,) = _all_gather([small], "gather_c")
    c_all = small_g[:, :D_MODEL // LANES].reshape(N_DEV, D_MODEL)
    conv_g = small_g[:, D_MODEL // LANES:D_MODEL // LANES + 3].transpose(1, 0, 2).reshape(3, D_CONV)

    ada_b_l = lax.dynamic_slice(ada_b, (0, me * ada_cols), (1, ada_cols))
    mod_cols = _ada_mod(jnp.pad(c_all, ((0, 8), (0, 0))), ada_w_l, ada_b_l)[:N_DEV]
    (mod_g,) = _all_gather([mod_cols], "gather_mod")
    mod = lax.dynamic_index_in_dim(mod_g, me, axis=1, keepdims=False).reshape(1, 3 * D_MODEL)

    half = jnp.arange(0, QK_ROPE, 2, dtype=F32) / QK_ROPE
    inv_freq = ROPE_BASE ** (-half)
    zeros64 = jnp.zeros((LANES - QK_ROPE,), F32)
    invf = jnp.concatenate([inv_freq, inv_freq, zeros64]).reshape(1, LANES)
    sign = jnp.concatenate([-jnp.ones((32,), F32), jnp.ones((32,), F32), zeros64]).reshape(1, LANES)
    qg_p, kg_p = _pad_cols(q_g, QK_PAD), _pad_cols(k_g, QK_PAD)

    my_off = ((CW * me) % LANES).astype(jnp.int32)
    win = [_expand_w_in(w_in_l.T, my_off.reshape(1))]
    h, h_t, cos, sin, win_g = _norm_mod(x2, norm_g, mod, positions.reshape(s, 1), invf, sign, _Gather(win, relay=True, parts=4), win)
    w_in_p = _merge_w_in(win_g)
    rest = [_pad_wq(w_q_l.T), w_kv_l.astype(BF16), w_out_l.astype(BF16)]
    u, wq_g, wkv_g, w_out_g = _matmul(h, w_in_p, nt=False, out_dtype=BF16, tm=2 * TM_MM, tn=2048, name="in_proj",
                                      rider=_Gather(rest), rider_inputs=rest)
    w_out_g = w_out_g.reshape(D_MODEL, D_MODEL)
    wq_g = wq_g.transpose(1, 0, 2).reshape(Q_LORA, N_HEADS * QK_PAD)
    wkv_g = wkv_g.transpose(1, 0, 2).reshape(KV_LORA, 2 * D_ATTN)
    q, k, v = _qkv_fwd(u, cos, sin, wq_g, wkv_g, q_a_g, kv_a_g, qg_p, kg_p)
    o, lse = _flash_fwd(q, k, v)
    gx1, dy, ycat_t, dyc, do, dza, delta, dgate, loss_row = _tail(x2, tgt, o, u, mod, w_out_g, conv_g)

    dq, dk, dv = _flash_bwd(q, k, v, do, lse.reshape(N_HEADS, nq, s // nq), delta.reshape(N_HEADS, nq, s // nq))
    du, dconv = _conv_bwd(u, dyc, conv_g)
    du, dwq, dwkv, dqag, dkvag, dqg, dkg = _qkv_bwd(u, cos, sin, dq, dk, dv, dza, wq_g, wkv_g, q_a_g, kv_a_g, qg_p, kg_p, du)
    dwq = dwq.reshape(Q_LORA, N_HEADS, QK_PAD).transpose(1, 0, 2)
    dwkv = dwkv.reshape(KV_LORA, N_HEADS, 2 * V_HEAD).transpose(1, 0, 2)
    dw_in = _matmul(h_t, du, nt=False, out_dtype=BF16, tm=TM_MM, tn=768, name="dw_in")
    first = [dw_in, dwq, dwkv]
    dw_out, r_in, r_q, r_kv = _matmul(ycat_t, dy, nt=False, out_dtype=BF16, tm=TM_MM, tn=512, name="dw_out",
                                      rider=_SiblingExchange(first, [True, False, False]), rider_inputs=first)
    dw_out = dw_out.reshape(N_DEV, D_MODEL // N_DEV, D_MODEL)
    (r_out,) = _exchange(_SiblingExchange([dw_out], [False]), [dw_out], "rs_sibling_out")
    core = lax.axis_index("c").astype(jnp.int32)
    lo_tiles = ((CW * (2 * jnp.arange(4, dtype=jnp.int32) + core)) // LANES).astype(jnp.int32)
    pairs = [_add_window(dw_in, r_in, lo_tiles), _add_pairs(dwq, r_q, core.reshape(1), "rs_add_q"),
             _add_pairs(dwkv, r_kv, core.reshape(1), "rs_add_kv"), _add_pairs(dw_out, r_out, core.reshape(1), "rs_add_out")]
    dh, *quads = _matmul(du, w_in_p, nt=True, out_dtype=BF16, tm=2 * TM_MM, tn=512, name="dh",
                         rider=_ChipExchange(pairs), rider_inputs=pairs, a_resident=True)
    my_chip = 2 * lax.axis_index("x") + lax.axis_index("y")
    written = jnp.where(jnp.arange(4) == my_chip, (jnp.arange(4) + 1) % 4, jnp.arange(4))
    sel = jnp.concatenate([my_chip.reshape(1), written, ((EXP_W - my_off) % EXP_W).reshape(1)]).astype(jnp.int32)
    g_w_in_t = _final_sum(pairs[0], quads[0], sel, "rs_sum_in", unshift=True, keep_t=CW)
    g_w_q_t = _final_sum(pairs[1], quads[1], sel, "rs_sum_q", keep_t=QK_HEAD)
    g_w_kv = _final_sum(pairs[2], quads[2], sel, "rs_sum_kv")
    g_w_out = _final_sum(pairs[3], quads[3], sel, "rs_sum_out")
    grad_x, dshift, dscale, dng = _norm_bwd(x2, dh, gx1, norm_g, mod)

    row = jnp.concatenate([dshift, dscale, dgate, dng, dqag, dkvag, dqg, dkg, dconv[:3].reshape(1, 3 * D_CONV), loss_row], axis=1)
    (rows_g,) = _all_gather([row], "gather_small")
    tot = _sum_leading(rows_g, F32, "sum_small")
    dmod_all = rows_g[:, 0, SM_MOD:SM_NG]
    g_ada_b = tot[:, SM_MOD:SM_NG]
    g_norm_g = tot[:, SM_NG:SM_QAG]
    g_q_a_g = tot[:, SM_QAG:SM_KVAG]
    g_kv_a_g = tot[:, SM_KVAG:SM_QG]
    g_q_g = tot[:, SM_QG:SM_QG + QK_HEAD]
    g_k_g = tot[:, SM_KG:SM_KG + QK_HEAD]
    conv_cols = conv_l.shape[1]
    g_conv = lax.dynamic_slice(tot[:, SM_CONV:SM_LOSS].reshape(3, D_CONV), (0, me * conv_cols), (3, conv_cols))
    loss = tot[0, SM_LOSS]
    dmod_my = lax.dynamic_slice(dmod_all, (0, me * ada_cols), (N_DEV, ada_cols))
    g_ada_w = _ada_w_grad(c_all.T, dmod_my)

    grads = dict(ada_w=g_ada_w, ada_b=g_ada_b, norm_g=g_norm_g, w_in=g_w_in_t, conv_w=g_conv, q_a_g=g_q_a_g, w_q_b=g_w_q_t,
                 kv_a_g=g_kv_a_g, w_kv_b=g_w_kv, q_g=g_q_g, k_g=g_k_g, w_out=g_w_out)
    weights = dict(ada_w=(ada_w, m_ada_w, v_ada_w), ada_b=(ada_b, m_ada_b, v_ada_b), norm_g=(norm_g, m_norm_g, v_norm_g),
                   w_in=(w_in, m_w_in, v_w_in), conv_w=(conv_w, m_conv_w, v_conv_w), q_a_g=(q_a_g, m_q_a_g, v_q_a_g),
                   w_q_b=(w_q_b, m_w_q_b, v_w_q_b), kv_a_g=(kv_a_g, m_kv_a_g, v_kv_a_g), w_kv_b=(w_kv_b, m_w_kv_b, v_w_kv_b),
                   q_g=(q_g, m_q_g, v_q_g), k_g=(k_g, m_k_g, v_k_g), w_out=(w_out, m_w_out, v_w_out))
    names = list(grads)
    out_g, out_d, out_m, out_v = [], [], [], []
    for n in names:
        w, m, v_ = weights[n]
        shape2 = w.shape[-2:] if w.ndim == 3 else (1, w.shape[-1])
        transposed = n in ("w_in", "w_q_b")
        to2 = (lambda a: a.reshape(shape2).T) if transposed else (lambda a: a.reshape(shape2))
        back = (lambda a: a.T.reshape(w.shape)) if transposed else (lambda a: a.reshape(w.shape))
        g2 = grads[n] if transposed else grads[n].reshape(shape2)
        d2, m2, v2 = _adamw(to2(w), g2, to2(m), to2(v_), "adamw_" + n)
        out_g.append(back(g2))
        out_d.append(back(d2))
        out_m.append(back(m2))
        out_v.append(back(v2))
    return (loss, grad_x.reshape(x.shape), *out_g, *out_d, *out_m, *out_v)
```

```python
import functools
import math

import jax
import jax.numpy as jnp
from jax import lax
from jax.experimental import pallas as pl
from jax.experimental.pallas import tpu as pltpu

F32 = jnp.float32
BF16 = jnp.bfloat16
MESH = pl.DeviceIdType.MESH

D_MODEL = 2048
D_CONV = 1024
N_HEADS = 8
QK_NOPE = 128
QK_ROPE = 64
QK_HEAD = QK_NOPE + QK_ROPE
V_HEAD = 128
D_ATTN = N_HEADS * V_HEAD
Q_LORA = 512
KV_LORA = 256
ROPE_BASE = 10000.0
IN_COLS = 4 * D_CONV + Q_LORA + KV_LORA + QK_ROPE + D_ATTN
EPS = 1e-6
ADAM_LR, ADAM_B1, ADAM_B2, ADAM_EPS, ADAM_WD, ADAM_STEP = 0.001, 0.9, 0.999, 1e-08, 0.01, 10

N_DEV = 8
LANES = 128
QK_PAD = 256
U_COLS = 6144
U_CQ, U_CKV, U_KR, U_ZA = 4096, 4608, 4864, 4928
U_TAIL = 2048
ZA_LO = U_ZA - (U_COLS - U_TAIL) - QK_ROPE
ZA_WIN = D_ATTN + LANES
CW = IN_COLS // 8
EXP_W = 896
W_LO = [(CW * d // 128) * 128 for d in range(8)]
W_OFF = [CW * d - lo for d, lo in enumerate(W_LO)]
SCALE = 1.0 / math.sqrt(QK_HEAD)
LOG2E = 1.4426950408889634
LN2 = 0.6931471805599453
NEG = -1e30
VMEM_LIMIT = 56 * 1024 * 1024

TM_ELEM = 256
NORM_ROWS = 16
NORM_GROUP = 4
TM_MM = 512
TQ = 1024
Q_CHAINS = 4
KV_SPLIT = 2

SM_MOD, SM_NG, SM_QAG, SM_KVAG, SM_QG, SM_KG, SM_CONV, SM_LOSS = 0, 6144, 8192, 8704, 8960, 9216, 9472, 12544
SM_COLS = 12672


def _params(sem=None):
    kw = dict(vmem_limit_bytes=VMEM_LIMIT)
    if sem is not None:
        kw["dimension_semantics"] = sem
    return pltpu.CompilerParams(**kw)


def _sigmoid(z):
    return 1.0 / (1.0 + jnp.exp(-z))


def _rot64(x):
    lane = lax.broadcasted_iota(jnp.int32, x.shape, 1)
    return jnp.where(lane < 32, pltpu.roll(x, 96, 1), pltpu.roll(x, 32, 1))


def _rope(x, cos, sin):
    return x * cos + _rot64(x) * sin


def _rope_t(d, cos, sin):
    return d * cos - _rot64(d) * sin


def _dot(a, b):
    return jnp.dot(a, b, preferred_element_type=F32)


def _dot_nt(a, b):
    return lax.dot_general(a, b, (((1,), (1,)), ((), ())), preferred_element_type=F32)


def _dot_tn(a, b):
    return lax.dot_general(a, b, (((0,), (0,)), ((), ())), preferred_element_type=F32)


def _my_index():
    return 4 * lax.axis_index("x") + 2 * lax.axis_index("y") + lax.axis_index("c")


ANY = pl.BlockSpec(memory_space=pl.ANY)


class _Gather:
    def __init__(self, blocks, relay=False, parts=1):
        self.relay = relay
        self.parts = parts
        self.rows = [b.shape[0] // parts for b in blocks]
        self.n = n = len(blocks) * parts
        self.out_shape = [jax.ShapeDtypeStruct((N_DEV,) + b.shape, b.dtype) for b in blocks]
        self.scratch = [pltpu.SemaphoreType.DMA((7 * n,)), pltpu.SemaphoreType.DMA((7 * n,)),
                        pltpu.SemaphoreType.DMA((n,))]

    @staticmethod
    def _places():
        x, y, c = lax.axis_index("x"), lax.axis_index("y"), lax.axis_index("c")
        return (x, y, c), (x, y, 1 - c), [(1 - x, y), (x, 1 - y), (1 - x, 1 - y)]

    def _src(self, ins, a):
        block, part = divmod(a, self.parts)
        return ins[block] if self.parts == 1 else ins[block].at[pl.ds(part * self.rows[block], self.rows[block])]

    def _dst(self, outs, a, place):
        block, part = divmod(a, self.parts)
        ref = outs[block].at[4 * place[0] + 2 * place[1] + place[2]]
        return ref if self.parts == 1 else ref.at[pl.ds(part * self.rows[block], self.rows[block])]

    def _copy(self, outs, sems, a, k, block, to, src=None):
        dst = self._dst(outs, a, block)
        return pltpu.make_async_remote_copy(
            src_ref=dst if src is None else src, dst_ref=dst, send_sem=sems[0].at[7 * a + k],
            recv_sem=sems[1].at[7 * a + k], device_id=to, device_id_type=MESH)

    def _first(self, ins, outs, sems):
        me, sibling, chips = self._places()
        first = []
        for a in range(self.n):
            first.append(self._copy(outs, sems, a, 0, me, sibling, src=self._src(ins, a)))
            first += [self._copy(outs, sems, a, 1 + j, me, (*chip, me[2]), src=self._src(ins, a))
                      for j, chip in enumerate(chips[:2] if self.relay else chips)]
        return first

    def _relays(self, outs, sems):
        if not self.relay:
            return []
        (x, y, c), _, _ = self._places()
        via = (jnp.where(c == 0, 1 - x, x), jnp.where(c == 0, y, 1 - y))
        to = (jnp.where(c == 0, x, 1 - x), jnp.where(c == 0, 1 - y, y))
        return [self._copy(outs, sems, a, 3, (*via, c), (*to, c)) for a in range(self.n)]

    def _passed(self, outs, sems):
        me, sibling, chips = self._places()
        return [self._copy(outs, sems, a, 4 + j, (*chip, me[2]), sibling)
                for a in range(self.n) for j, chip in enumerate(chips)]

    def _mine(self, ins, outs, sems):
        me, _, _ = self._places()
        return [pltpu.make_async_copy(self._src(ins, a), self._dst(outs, a, me), sems[2].at[a]) for a in range(self.n)]

    def start(self, ins, outs, sems):
        for cp in self._mine(ins, outs, sems) + self._first(ins, outs, sems):
            cp.start()

    def forward(self, ins, outs, sems):
        del ins
        me, _, chips = self._places()
        passed, relays = self._passed(outs, sems), self._relays(outs, sems)
        for a in range(self.n):
            for j, chip in enumerate(chips[:2] if self.relay else chips):
                self._copy(outs, sems, a, 1 + j, (*chip, me[2]), me).wait_recv()
                passed[3 * a + j].start()
            if self.relay:
                relays[a].start()
        if self.relay:
            for a in range(self.n):
                self._copy(outs, sems, a, 3, (*chips[2], me[2]), me).wait_recv()
                passed[3 * a + 2].start()

    def finish(self, ins, outs, sems):
        me, sibling, chips = self._places()
        for a in range(self.n):
            self._copy(outs, sems, a, 0, sibling, me).wait_recv()
            for j, chip in enumerate(chips):
                self._copy(outs, sems, a, 4 + j, (*chip, 1 - me[2]), me).wait_recv()
        for cp in self._first(ins, outs, sems) + self._relays(outs, sems) + self._passed(outs, sems):
            cp.wait_send()
        for cp in self._mine(ins, outs, sems):
            cp.wait()


class _ChipExchange:
    def __init__(self, arrays):
        self.n = n = len(arrays)
        self.out_shape = [jax.ShapeDtypeStruct(a.shape, a.dtype) for a in arrays]
        self.scratch = [pltpu.SemaphoreType.DMA((3 * n,)), pltpu.SemaphoreType.DMA((3 * n,))]

    def _copies(self, ins, outs, sems):
        x, y, c = lax.axis_index("x"), lax.axis_index("y"), lax.axis_index("c")
        return [pltpu.make_async_remote_copy(
            src_ref=ins[a].at[2 * px + py], dst_ref=outs[a].at[2 * x + y], send_sem=sems[0].at[3 * a + j],
            recv_sem=sems[1].at[3 * a + j], device_id=(px, py, c), device_id_type=MESH)
            for a in range(self.n) for j, (px, py) in enumerate([(1 - x, y), (x, 1 - y), (1 - x, 1 - y)])]

    def start(self, ins, outs, sems):
        for cp in self._copies(ins, outs, sems):
            cp.start()

    def forward(self, ins, outs, sems):
        pass

    def finish(self, ins, outs, sems):
        for cp in self._copies(ins, outs, sems):
            cp.wait()


def _all_gather(blocks, name):
    n = len(blocks)
    g = _Gather(blocks)

    def body(*refs):
        ins, outs, sems = refs[:n], refs[n:2 * n], refs[2 * n:]
        g.start(ins, outs, sems)
        g.forward(ins, outs, sems)
        g.finish(ins, outs, sems)

    return pl.pallas_call(body, name=name, out_shape=g.out_shape, in_specs=[ANY] * n, out_specs=[ANY] * n,
                          scratch_shapes=g.scratch)(*blocks)


class _SiblingExchange:
    def __init__(self, arrays, windowed):
        self.n = n = len(arrays)
        self.windowed = windowed
        self.out_shape = [jax.ShapeDtypeStruct((4, a.shape[0], EXP_W) if w else (4,) + a.shape[1:], a.dtype)
                          for a, w in zip(arrays, windowed)]
        self.scratch = [pltpu.SemaphoreType.DMA((4 * n,)), pltpu.SemaphoreType.DMA((4 * n,))]

    def _each(self, ins, outs, sems, act):
        x, y, c = lax.axis_index("x"), lax.axis_index("y"), lax.axis_index("c")

        def branch(c_val):
            for k in range(4):
                e = 2 * k + (1 - c_val)
                for a in range(self.n):
                    src = ins[a].at[:, pl.ds(W_LO[e], EXP_W)] if self.windowed[a] else ins[a].at[e]
                    act(pltpu.make_async_remote_copy(
                        src_ref=src, dst_ref=outs[a].at[k], send_sem=sems[0].at[4 * a + k], recv_sem=sems[1].at[4 * a + k],
                        device_id=(x, y, 1 - c), device_id_type=MESH))

        for c_val in (0, 1):
            pl.when(c == c_val)(functools.partial(branch, c_val))

    def start(self, ins, outs, sems):
        self._each(ins, outs, sems, lambda cp: cp.start())

    def forward(self, ins, outs, sems):
        pass

    def finish(self, ins, outs, sems):
        self._each(ins, outs, sems, lambda cp: cp.wait())


def _exchange(rider, arrays, name):
    n = len(arrays)

    def body(*refs):
        ins, outs, sems = refs[:n], refs[n:n + len(rider.out_shape)], refs[n + len(rider.out_shape):]
        rider.start(ins, outs, sems)
        rider.forward(ins, outs, sems)
        rider.finish(ins, outs, sems)

    return pl.pallas_call(body, name=name, out_shape=rider.out_shape, in_specs=[ANY] * n,
                          out_specs=[ANY] * len(rider.out_shape), scratch_shapes=rider.scratch)(*arrays)


def _add_window(dw_in, recv, lo_tiles):
    k, rows, _ = recv.shape

    def body(t_ref, w_ref, r_ref, o_ref):
        del t_ref
        o_ref[0] = (w_ref[...].astype(F32) + r_ref[0].astype(F32)).astype(o_ref.dtype)

    spec = pl.BlockSpec((1, rows, LANES), lambda i, j, t: (i, 0, j))
    grid_spec = pltpu.PrefetchScalarGridSpec(
        num_scalar_prefetch=1, grid=(k, EXP_W // LANES),
        in_specs=[pl.BlockSpec((rows, LANES), lambda i, j, t: (0, t[i] + j)), spec], out_specs=spec)
    return pl.pallas_call(
        body, name="rs_add_in", grid_spec=grid_spec, out_shape=jax.ShapeDtypeStruct(recv.shape, recv.dtype),
        compiler_params=_params(("parallel", "parallel")),
    )(lo_tiles, dw_in, recv)


def _final_sum(p, r, sel, name, unshift=False, keep_t=None):
    _, rows, cols = p.shape
    tr = 512 if rows % 512 == 0 else rows

    def body(sel_ref, p_ref, r0, r1, r2, r3, o_ref):
        own = p_ref[0].astype(F32)
        acc = None
        for k, r_ref in enumerate((r0, r1, r2, r3)):
            term = jnp.where(sel_ref[0] == k, own, r_ref[0].astype(F32))
            acc = term if acc is None else acc + term
        if unshift:
            acc = pltpu.roll(acc, sel_ref[5], 1)
        o_ref[...] = acc if keep_t is None else acc.T[:keep_t]

    def slot(k):
        return pl.BlockSpec((1, tr, cols), lambda i, t: (t[k], i, 0))

    if keep_t is None:
        out_spec, out_shape = pl.BlockSpec((tr, cols), lambda i, t: (i, 0)), (rows, cols)
    else:
        out_spec, out_shape = pl.BlockSpec((keep_t, tr), lambda i, t: (0, i)), (keep_t, rows)
    grid_spec = pltpu.PrefetchScalarGridSpec(
        num_scalar_prefetch=1, grid=(rows // tr,), in_specs=[slot(0), slot(1), slot(2), slot(3), slot(4)],
        out_specs=out_spec)
    return pl.pallas_call(
        body, name=name, grid_spec=grid_spec, out_shape=jax.ShapeDtypeStruct(out_shape, F32),
        compiler_params=_params(("parallel",)),
    )(sel, p, r, r, r, r)


def _expand_w_in(w_t, shift):
    cw, rows = w_t.shape
    tr = 256
    pad = -cw % LANES

    def body(s_ref, w_ref, o_ref):
        w = jnp.concatenate([w_ref[...], jnp.zeros((pad, tr), F32)], axis=0).T
        w = jnp.concatenate([w, jnp.zeros((tr, EXP_W - cw - pad), F32)], axis=1)
        o_ref[...] = pltpu.roll(w, s_ref[0], 1).astype(BF16)

    grid_spec = pltpu.PrefetchScalarGridSpec(
        num_scalar_prefetch=1, grid=(rows // tr,), in_specs=[pl.BlockSpec((cw, tr), lambda i, t: (0, i))],
        out_specs=pl.BlockSpec((tr, EXP_W), lambda i, t: (i, 0)))
    return pl.pallas_call(
        body, name="expand_w_in", grid_spec=grid_spec, out_shape=jax.ShapeDtypeStruct((rows, EXP_W), BF16),
        compiler_params=_params(("arbitrary",)),
    )(shift, w_t)


def _pad_wq(w_t):
    cw, rows = w_t.shape

    def body(w_ref, o_ref):
        o_ref[...] = jnp.concatenate([w_ref[...], jnp.zeros((QK_PAD - cw, rows), F32)], axis=0).T.astype(BF16)

    return pl.pallas_call(
        body, name="pad_wq", out_shape=jax.ShapeDtypeStruct((rows, QK_PAD), BF16), compiler_params=_params(),
    )(w_t)


def _merge_w_in(e):
    _, rows, _ = e.shape
    tr = 256

    def body(e_ref, o_ref):
        for t in range(U_COLS // LANES):
            lo, hi = t * LANES, (t + 1) * LANES
            parts = [e_ref[d, :, lo - W_LO[d]:hi - W_LO[d]] for d in range(N_DEV)
                     if CW * d < hi and CW * (d + 1) > lo]
            if not parts:
                tile = jnp.zeros((tr, LANES), BF16)
            elif len(parts) == 1:
                tile = parts[0]
            else:
                tile = (parts[0].astype(F32) + parts[1].astype(F32)).astype(BF16)
            o_ref[:, lo:hi] = tile

    return pl.pallas_call(
        body, name="merge_w_in", grid=(rows // tr,),
        in_specs=[pl.BlockSpec((N_DEV, tr, EXP_W), lambda i: (0, i, 0))],
        out_specs=pl.BlockSpec((tr, U_COLS), lambda i: (i, 0)), out_shape=jax.ShapeDtypeStruct((rows, U_COLS), BF16),
        compiler_params=_params(("parallel",)),
    )(e)


def _sum_leading(a, out_dtype, name):
    k, rows, cols = a.shape
    tr = min(rows, 1728 if rows % 1728 == 0 else rows)

    def body(a_ref, o_ref):
        acc = a_ref[0].astype(F32)
        for i in range(1, k):
            acc = acc + a_ref[i].astype(F32)
        o_ref[...] = acc.astype(out_dtype)

    return pl.pallas_call(
        body, name=name, grid=(rows // tr,),
        in_specs=[pl.BlockSpec((k, tr, cols), lambda i: (0, i, 0))],
        out_specs=pl.BlockSpec((tr, cols), lambda i: (i, 0)),
        out_shape=jax.ShapeDtypeStruct((rows, cols), out_dtype), compiler_params=_params(("parallel",)),
    )(a)


def _add_pairs(g, recv, core, name):
    k, rows, cols = recv.shape
    tr = 1728 if rows % 1728 == 0 else rows

    def body(c_ref, g_ref, r_ref, o_ref):
        del c_ref
        o_ref[...] = (g_ref[...].astype(F32) + r_ref[...].astype(F32)).astype(o_ref.dtype)

    spec = pl.BlockSpec((1, tr, cols), lambda i, j, c: (i, j, 0))
    grid_spec = pltpu.PrefetchScalarGridSpec(
        num_scalar_prefetch=1, grid=(k, rows // tr),
        in_specs=[pl.BlockSpec((1, tr, cols), lambda i, j, c: (2 * i + c[0], j, 0)), spec], out_specs=spec)
    return pl.pallas_call(
        body, name=name, grid_spec=grid_spec, out_shape=jax.ShapeDtypeStruct(recv.shape, recv.dtype),
        compiler_params=_params(("parallel", "parallel")),
    )(core, g, recv)


def _ada_mod(c16, ada_w_l, ada_b_l):
    def body(c_ref, w_ref, b_ref, o_ref):
        cv = c_ref[...]
        sc = (cv * _sigmoid(cv)).astype(BF16)
        o_ref[...] = _dot(sc, w_ref[...].astype(BF16)) + b_ref[...]

    return pl.pallas_call(
        body, name="ada_mod", out_shape=jax.ShapeDtypeStruct((c16.shape[0], ada_w_l.shape[1]), F32),
        compiler_params=_params(),
    )(c16, ada_w_l, ada_b_l)


def _ada_w_grad(c_t, dmod_my):
    def body(c_ref, d_ref, o_ref):
        cv = c_ref[...]
        sc = cv * _sigmoid(cv)
        acc = sc[:, 0:1] * d_ref[0:1, :]
        for b in range(1, N_DEV):
            acc = acc + sc[:, b:b + 1] * d_ref[b:b + 1, :]
        o_ref[...] = acc

    return pl.pallas_call(
        body, name="ada_w_grad", out_shape=jax.ShapeDtypeStruct((c_t.shape[0], dmod_my.shape[1]), F32),
        compiler_params=_params(),
    )(c_t, dmod_my)


def _norm_mod(x, norm_g, mod, pos_col, invf, sign, rider, rider_inputs):
    s, d = x.shape
    tm = min(TM_MM, s)
    n_in, n_out = len(rider_inputs), len(rider.out_shape)
    steps = s // tm

    def body(x_ref, g_ref, mod_ref, p_ref, f_ref, s_ref, *rest):
        r_ins, (h_ref, ht_ref, cos_ref, sin_ref) = rest[:n_in], rest[n_in:n_in + 4]
        r_outs, sems = rest[n_in + 4:n_in + 4 + n_out], rest[n_in + 4 + n_out:]
        pl.when(pl.program_id(0) == 0)(functools.partial(rider.start, r_ins, r_outs, sems))
        xv = x_ref[...]
        r = lax.rsqrt(jnp.mean(xv * xv, axis=-1, keepdims=True) + EPS)
        hn = xv * r * g_ref[...]
        hv = hn * (1.0 + mod_ref[:, d:2 * d]) + mod_ref[:, 0:d]
        h_ref[...] = hv.astype(BF16)
        ht_ref[...] = hv.T.astype(BF16)
        ang = p_ref[...].astype(F32) * f_ref[...]
        sg = s_ref[...]
        cos_ref[...] = jnp.cos(ang) * jnp.abs(sg)
        sin_ref[...] = jnp.sin(ang) * sg

        @pl.when(pl.program_id(0) == steps - 1)
        def _():
            rider.forward(r_ins, r_outs, sems)
            rider.finish(r_ins, r_outs, sems)

    row = pl.BlockSpec((1, LANES), lambda i: (0, 0))
    tab = pl.BlockSpec((tm, LANES), lambda i: (i, 0))
    return pl.pallas_call(
        body, name="norm_mod", grid=(steps,),
        in_specs=[pl.BlockSpec((tm, d), lambda i: (i, 0)), pl.BlockSpec((1, d), lambda i: (0, 0)),
                  pl.BlockSpec((1, 3 * d), lambda i: (0, 0)), pl.BlockSpec((tm, 1), lambda i: (i, 0)), row, row]
        + [ANY] * n_in,
        out_specs=[pl.BlockSpec((tm, d), lambda i: (i, 0)), pl.BlockSpec((d, tm), lambda i: (0, i)), tab, tab] + [ANY] * n_out,
        out_shape=[jax.ShapeDtypeStruct((s, d), BF16), jax.ShapeDtypeStruct((d, s), BF16),
                   jax.ShapeDtypeStruct((s, LANES), F32), jax.ShapeDtypeStruct((s, LANES), F32)] + rider.out_shape,
        scratch_shapes=rider.scratch, compiler_params=_params(("arbitrary",)),
    )(x, norm_g, mod, pos_col, invf, sign, *rider_inputs)


def _matmul(a, b, *, nt, out_dtype, tm, tn, name, rider=None, rider_inputs=(), a_resident=False):
    m, kdim = a.shape
    n = b.shape[0] if nt else b.shape[1]
    tm, tn = min(tm, m), min(tn, n)
    n_in = len(rider_inputs)
    n_out = len(rider.out_shape) if rider else 0
    m_steps, n_steps = m // tm, n // tn
    steps = n_steps * m_steps
    inner = n_steps if a_resident else m_steps
    tile = (lambda o, i: (o, i)) if a_resident else (lambda o, i: (i, o))

    def body(a_ref, b_ref, *rest):
        r_ins, o_ref, r_outs, sems = rest[:n_in], rest[n_in], rest[n_in + 1:n_in + 1 + n_out], rest[n_in + 1 + n_out:]
        step = pl.program_id(0) * inner + pl.program_id(1)
        if rider:
            pl.when(step == 0)(functools.partial(rider.start, r_ins, r_outs, sems))
            pl.when(step == steps // 2)(functools.partial(rider.forward, r_ins, r_outs, sems))
        o = _dot_nt(a_ref[...], b_ref[...]) if nt else _dot(a_ref[...], b_ref[...])
        o_ref[...] = o.astype(out_dtype)
        if rider:
            pl.when(step == steps - 1)(functools.partial(rider.finish, r_ins, r_outs, sems))

    if nt:
        b_spec = pl.BlockSpec((tn, kdim), lambda o, i: (tile(o, i)[1], 0))
    else:
        b_spec = pl.BlockSpec((kdim, tn), lambda o, i: (0, tile(o, i)[1]))
    out = pl.pallas_call(
        body, name=name, grid=(m_steps, n_steps) if a_resident else (n_steps, m_steps),
        in_specs=[pl.BlockSpec((tm, kdim), lambda o, i: (tile(o, i)[0], 0)), b_spec] + [ANY] * n_in,
        out_specs=[pl.BlockSpec((tm, tn), tile)] + [ANY] * n_out,
        out_shape=[jax.ShapeDtypeStruct((m, n), out_dtype)] + (rider.out_shape if rider else []),
        scratch_shapes=rider.scratch if rider else [],
        compiler_params=_params(("arbitrary", "arbitrary") if rider else ("parallel", "parallel")),
    )(a, b, *rider_inputs)
    return out if rider else out[0]


HALO = 16


def _conv_specs(tm):
    def col(j):
        return pl.BlockSpec((tm, D_CONV), lambda i: (i, j))

    def prev(j):
        return pl.BlockSpec((HALO, D_CONV), lambda i: (jnp.maximum(i * (tm // HALO) - 1, 0), j))

    return [col(0), col(1), col(2), col(3), prev(0), prev(2)]


def _conv_y(xc_ref, bc_ref, cc_ref, zc_ref, xp_ref, cp_ref, w_ref, first):
    uc = cc_ref[...].astype(F32) * xc_ref[...].astype(F32)
    up = jnp.where(first, 0.0, cp_ref[...].astype(F32) * xp_ref[...].astype(F32))
    full = jnp.concatenate([up, uc], axis=0)
    u1 = pltpu.roll(full, 1, 0)[HALO:]
    u2 = pltpu.roll(full, 2, 0)[HALO:]
    w = w_ref[...]
    conv = w[0:1] * u2 + w[1:2] * u1 + w[2:3] * uc
    z = zc_ref[...].astype(F32)
    return bc_ref[...].astype(F32) * conv * (z * _sigmoid(z))


def _conv_bwd(u, dyc, conv_w):
    s = u.shape[0]
    tm = min(TM_ELEM, s)
    cb = D_CONV
    nt = s // tm

    def body(xc_ref, bc_ref, cc_ref, zc_ref, xp_ref, cp_ref, bn_ref, zn_ref, dy_ref, dyn_ref, w_ref, du_ref, dw_ref):
        i = pl.program_id(0)
        xc, cc = xc_ref[...].astype(F32), cc_ref[...].astype(F32)
        bc, z = bc_ref[...].astype(F32), zc_ref[...].astype(F32)
        uc = cc * xc
        up = jnp.where(i == 0, 0.0, cp_ref[...].astype(F32) * xp_ref[...].astype(F32))
        full = jnp.concatenate([up, uc], axis=0)
        u1 = pltpu.roll(full, 1, 0)[HALO:]
        u2 = pltpu.roll(full, 2, 0)[HALO:]
        w = w_ref[...]
        conv = w[0:1] * u2 + w[1:2] * u1 + w[2:3] * uc
        sg = _sigmoid(z)
        sz = z * sg
        dy = dy_ref[...].astype(F32)
        dconv = dy * bc * sz
        zn = zn_ref[...].astype(F32)
        dnext = dyn_ref[...].astype(F32) * bn_ref[...].astype(F32) * (zn * _sigmoid(zn))
        dnext = jnp.where(i == nt - 1, 0.0, dnext)
        fullb = jnp.concatenate([dconv, dnext], axis=0)
        nb = tm + HALO
        d1 = pltpu.roll(fullb, nb - 1, 0)[:tm]
        d2 = pltpu.roll(fullb, nb - 2, 0)[:tm]
        duc = w[2:3] * dconv + w[1:2] * d1 + w[0:1] * d2
        dzc = dy * bc * conv * (sg * (1.0 + z * (1.0 - sg)))
        du_ref[...] = jnp.concatenate([duc * cc, dy * conv * sz, duc * xc, dzc], axis=1).astype(BF16)
        dw = jnp.concatenate([jnp.sum(dconv * u2, axis=0, keepdims=True), jnp.sum(dconv * u1, axis=0, keepdims=True),
                              jnp.sum(dconv * uc, axis=0, keepdims=True), jnp.zeros((5, cb), F32)], axis=0)

        @pl.when(i == 0)
        def _():
            dw_ref[...] = dw

        @pl.when(i > 0)
        def _():
            dw_ref[...] += dw

    def col(j):
        return pl.BlockSpec((tm, cb), lambda i: (i, j))

    def prev(j):
        return pl.BlockSpec((HALO, cb), lambda i: (jnp.maximum(i * (tm // HALO) - 1, 0), j))

    def nxt(j):
        return pl.BlockSpec((HALO, cb), lambda i: (jnp.minimum((i + 1) * (tm // HALO), s // HALO - 1), j))

    return pl.pallas_call(
        body, name="conv_bwd", grid=(nt,),
        in_specs=[col(0), col(1), col(2), col(3), prev(0), prev(2), nxt(1), nxt(3), col(0), nxt(0),
                  pl.BlockSpec((3, cb), lambda i: (0, 0))],
        out_specs=[pl.BlockSpec((tm, 4 * cb), lambda i: (i, 0)), pl.BlockSpec((8, cb), lambda i: (0, 0))],
        out_shape=[jax.ShapeDtypeStruct((s, U_COLS), BF16), jax.ShapeDtypeStruct((8, cb), F32)],
        compiler_params=_params(("arbitrary",)),
    )(u, u, u, u, u, u, u, u, dyc, dyc, conv_w)


def _qkv_specs(tm):
    return [pl.BlockSpec((tm, Q_LORA), lambda i: (i, U_CQ // Q_LORA)),
            pl.BlockSpec((tm, KV_LORA), lambda i: (i, U_CKV // KV_LORA)),
            pl.BlockSpec((tm, LANES), lambda i: (i, U_KR // LANES)),
            pl.BlockSpec((tm, LANES), lambda i: (i, 0)), pl.BlockSpec((tm, LANES), lambda i: (i, 0))]


def _full(shape):
    return pl.BlockSpec(shape, lambda i: (0,) * len(shape))


def _k_rope_lanes(blk):
    lane = lax.broadcasted_iota(jnp.int32, blk.shape, 1)
    return jnp.where(lane < QK_ROPE, blk, 0.0)


def _qkv_fwd(u, cos, sin, wq, wkv, qag, kvag, qg, kg):
    s = u.shape[0]
    tm = min(TM_ELEM, s)

    def body(cq_ref, ckv_ref, kr_ref, cos_ref, sin_ref, wq_ref, wkv_ref, qag_ref, kvag_ref, qg_ref, kg_ref,
             q_ref, k_ref, v_ref):
        cq = cq_ref[...].astype(F32)
        cqn = (cq * lax.rsqrt(jnp.mean(cq * cq, axis=-1, keepdims=True) + EPS) * qag_ref[...]).astype(BF16)
        ckv = ckv_ref[...].astype(F32)
        ckvn = (ckv * lax.rsqrt(jnp.mean(ckv * ckv, axis=-1, keepdims=True) + EPS) * kvag_ref[...]).astype(BF16)
        kr = _k_rope_lanes(kr_ref[...].astype(F32))
        cosv, sinv, qgv, kgv = cos_ref[...], sin_ref[...], qg_ref[...], kg_ref[...]
        ss_r = jnp.sum(kr * kr, axis=-1, keepdims=True)
        krr = _rope(kr * kgv[:, QK_NOPE:], cosv, sinv)
        qf = _dot(cqn, wq_ref[...])
        kvf = _dot(ckvn, wkv_ref[...])
        heads = range(N_HEADS)
        qh = [qf[:, QK_PAD * h:QK_PAD * (h + 1)] for h in heads]
        kn = [kvf[:, 2 * V_HEAD * h:2 * V_HEAD * h + QK_NOPE] for h in heads]
        rq = [lax.rsqrt(jnp.sum(qh[h] * qh[h], axis=-1, keepdims=True) * (1.0 / QK_HEAD) + EPS) for h in heads]
        rk = [lax.rsqrt((jnp.sum(kn[h] * kn[h], axis=-1, keepdims=True) + ss_r) * (1.0 / QK_HEAD) + EPS) for h in heads]
        for h in heads:
            qn = qh[h] * rq[h] * qgv
            qo = jnp.concatenate([qn[:, :QK_NOPE], _rope(qn[:, QK_NOPE:], cosv, sinv)], axis=1) * (SCALE * LOG2E)
            q_ref[h] = qo.astype(BF16)
            vh = kvf[:, 2 * V_HEAD * h + QK_NOPE:2 * V_HEAD * (h + 1)]
            k_ref[h] = jnp.concatenate([kn[h] * kgv[:, :QK_NOPE] * rk[h], krr * rk[h]], axis=1).astype(BF16)
            v_ref[h] = jnp.concatenate([vh, jnp.ones_like(vh)], axis=1).astype(BF16)

    return pl.pallas_call(
        body, name="qkv_fwd", grid=(s // tm,),
        in_specs=_qkv_specs(tm) + [_full((Q_LORA, N_HEADS * QK_PAD)), _full((KV_LORA, 2 * D_ATTN)),
                                   _full((1, Q_LORA)), _full((1, KV_LORA)), _full((1, QK_PAD)), _full((1, QK_PAD))],
        out_specs=[pl.BlockSpec((N_HEADS, tm, QK_PAD), lambda i: (0, i, 0)),
                   pl.BlockSpec((N_HEADS, tm, QK_PAD), lambda i: (0, i, 0)),
                   pl.BlockSpec((N_HEADS, tm, 2 * V_HEAD), lambda i: (0, i, 0))],
        out_shape=[jax.ShapeDtypeStruct((N_HEADS, s, QK_PAD), BF16), jax.ShapeDtypeStruct((N_HEADS, s, QK_PAD), BF16),
                   jax.ShapeDtypeStruct((N_HEADS, s, 2 * V_HEAD), BF16)],
        compiler_params=_params(("parallel",)),
    )(u, u, u, cos, sin, wq, wkv, qag, kvag, qg, kg)


def _qkv_bwd(u, cos, sin, dq, dk, dv, dza, wq, wkv, qag, kvag, qg, kg, du):
    s = u.shape[0]
    tm = min(TM_ELEM, s)
    nt = s // tm

    def body(cq_ref, ckv_ref, kr_ref, cos_ref, sin_ref, dq_ref, dk_ref, dv_ref, dza_ref, wq_ref, wkv_ref, qag_ref,
             kvag_ref, qg_ref, kg_ref, du_in, du_ref, dwq_ref, dwkv_ref, dqag_ref, dkvag_ref, dqg_ref, dkg_ref,
             dwq_acc, dwkv_acc):
        del du_in
        i = pl.program_id(0)

        @pl.when(i == 0)
        def _():
            dwq_acc[...] = jnp.zeros_like(dwq_acc)
            dwkv_acc[...] = jnp.zeros_like(dwkv_acc)

        cq = cq_ref[...].astype(F32)
        rqa = lax.rsqrt(jnp.mean(cq * cq, axis=-1, keepdims=True) + EPS)
        xq = cq * rqa
        qagv = qag_ref[...]
        cqn = (xq * qagv).astype(BF16)
        ckv = ckv_ref[...].astype(F32)
        rkva = lax.rsqrt(jnp.mean(ckv * ckv, axis=-1, keepdims=True) + EPS)
        xkv = ckv * rkva
        kvagv = kvag_ref[...]
        ckvn = (xkv * kvagv).astype(BF16)
        kr = _k_rope_lanes(kr_ref[...].astype(F32))
        cosv, sinv, qgv, kgv = cos_ref[...], sin_ref[...], qg_ref[...], kg_ref[...]
        ss_r = jnp.sum(kr * kr, axis=-1, keepdims=True)
        dqg = jnp.zeros((1, QK_PAD), F32)
        dkg = jnp.zeros((1, QK_PAD), F32)
        dkr = jnp.zeros((tm, LANES), F32)
        qf = _dot(cqn, wq_ref[...])
        kvf = _dot(ckvn, wkv_ref[...])
        heads = range(N_HEADS)
        qh = [qf[:, QK_PAD * h:QK_PAD * (h + 1)] for h in heads]
        kn = [kvf[:, 2 * V_HEAD * h:2 * V_HEAD * h + QK_NOPE] for h in heads]
        rq = [lax.rsqrt(jnp.sum(qh[h] * qh[h], axis=-1, keepdims=True) * (1.0 / QK_HEAD) + EPS) for h in heads]
        rk = [lax.rsqrt((jnp.sum(kn[h] * kn[h], axis=-1, keepdims=True) + ss_r) * (1.0 / QK_HEAD) + EPS) for h in heads]
        xh = [qh[h] * rq[h] for h in heads]
        xk = [jnp.concatenate([kn[h], kr], axis=1) * rk[h] for h in heads]
        dyq, dyk = [], []
        for h in heads:
            g = dq_ref[h].astype(F32) * SCALE
            dyq.append(jnp.concatenate([g[:, :QK_NOPE], _rope_t(g[:, QK_NOPE:], cosv, sinv)], axis=1))
            gk = dk_ref[h].astype(F32)
            dyk.append(jnp.concatenate([gk[:, :QK_NOPE], _rope_t(gk[:, QK_NOPE:], cosv, sinv)], axis=1))
        for h in heads:
            dqg = dqg + jnp.sum(dyq[h] * xh[h], axis=0, keepdims=True)
            dkg = dkg + jnp.sum(dyk[h] * xk[h], axis=0, keepdims=True)
        gdy = [dyq[h] * qgv for h in heads]
        gdyk = [dyk[h] * kgv for h in heads]
        tq_ = [jnp.sum(gdy[h] * xh[h], axis=-1, keepdims=True) * (1.0 / QK_HEAD) for h in heads]
        tk_ = [jnp.sum(gdyk[h] * xk[h], axis=-1, keepdims=True) * (1.0 / QK_HEAD) for h in heads]
        dqf = [(rq[h] * (gdy[h] - xh[h] * tq_[h])).astype(BF16) for h in heads]
        dkvf = []
        for h in heads:
            dxk = rk[h] * (gdyk[h] - xk[h] * tk_[h])
            dkr = dkr + dxk[:, QK_NOPE:]
            dkvf += [dxk[:, :QK_NOPE].astype(BF16), dv_ref[h]]
        dqf_b, dkvf_b = jnp.concatenate(dqf, axis=1), jnp.concatenate(dkvf, axis=1)
        dwq_acc[...] += _dot_tn(cqn, dqf_b)
        dwkv_acc[...] += _dot_tn(ckvn, dkvf_b)
        dcqn = _dot_nt(dqf_b, wq_ref[...])
        dckvn = _dot_nt(dkvf_b, wkv_ref[...])
        dqag = jnp.sum(dcqn * xq, axis=0, keepdims=True)
        dkvag = jnp.sum(dckvn * xkv, axis=0, keepdims=True)
        gq = dcqn * qagv
        dcq = rqa * (gq - xq * jnp.mean(gq * xq, axis=-1, keepdims=True))
        gkv = dckvn * kvagv
        dckv = rkva * (gkv - xkv * jnp.mean(gkv * xkv, axis=-1, keepdims=True))
        win = pltpu.roll(jnp.concatenate([dza_ref[...].astype(F32), jnp.zeros((tm, LANES), F32)], axis=1), QK_ROPE, 1)
        win = win + jnp.concatenate([dkr, jnp.zeros((tm, D_ATTN), F32)], axis=1)
        du_ref[...] = jnp.concatenate([dcq, dckv, win, jnp.zeros((tm, U_TAIL - ZA_LO - ZA_WIN), F32)], axis=1).astype(BF16)

        @pl.when(i == 0)
        def _():
            dqag_ref[...] = dqag
            dkvag_ref[...] = dkvag
            dqg_ref[...] = dqg
            dkg_ref[...] = dkg

        @pl.when(i > 0)
        def _():
            dqag_ref[...] += dqag
            dkvag_ref[...] += dkvag
            dqg_ref[...] += dqg
            dkg_ref[...] += dkg

        @pl.when(i == nt - 1)
        def _():
            dwq_ref[...] = dwq_acc[...].astype(BF16)
            dwkv_ref[...] = dwkv_acc[...].astype(BF16)

    head = lambda w: pl.BlockSpec((N_HEADS, tm, w), lambda i: (0, i, 0))
    wq_shape, wkv_shape = (Q_LORA, N_HEADS * QK_PAD), (KV_LORA, 2 * D_ATTN)
    return pl.pallas_call(
        body, name="qkv_bwd", grid=(nt,),
        in_specs=_qkv_specs(tm) + [head(QK_PAD), head(QK_PAD), head(V_HEAD), pl.BlockSpec((tm, D_ATTN), lambda i: (i, 0)),
                                   _full(wq_shape), _full(wkv_shape), _full((1, Q_LORA)), _full((1, KV_LORA)),
                                   _full((1, QK_PAD)), _full((1, QK_PAD)), ANY],
        out_specs=[pl.BlockSpec((tm, U_TAIL), lambda i: (i, U_COLS // U_TAIL - 1)), _full(wq_shape), _full(wkv_shape),
                   _full((1, Q_LORA)), _full((1, KV_LORA)), _full((1, QK_PAD)), _full((1, QK_PAD))],
        out_shape=[jax.ShapeDtypeStruct(du.shape, du.dtype), jax.ShapeDtypeStruct(wq_shape, BF16),
                   jax.ShapeDtypeStruct(wkv_shape, BF16), jax.ShapeDtypeStruct((1, Q_LORA), F32),
                   jax.ShapeDtypeStruct((1, KV_LORA), F32), jax.ShapeDtypeStruct((1, QK_PAD), F32),
                   jax.ShapeDtypeStruct((1, QK_PAD), F32)],
        scratch_shapes=[pltpu.VMEM(wq_shape, F32), pltpu.VMEM(wkv_shape, F32)],
        input_output_aliases={15: 0}, compiler_params=_params(("arbitrary",)),
    )(u, u, u, cos, sin, dq, dk, dv, dza, wq, wkv, qag, kvag, qg, kg, du)


def _flash_fwd(q, k, v):
    nh, s, _ = q.shape
    tq = min(TQ, s)
    nkv = KV_SPLIT
    tk = tq // nkv
    nq = s // tq
    nch = Q_CHAINS
    tc = tq // nch

    def body(q_ref, k_ref, v_ref, o_ref, lse_ref):
        i = pl.program_id(1)
        chains = [q_ref[0, r * tc:(r + 1) * tc, :] for r in range(nch)]

        def unit(r, j, carry, shift=None):
            m, acc = carry
            rows = pl.ds(pl.multiple_of(j * tk, tk), tk)
            sc = _dot_nt(chains[r], k_ref[0, rows, :])
            if shift is not None:
                qi = lax.broadcasted_iota(jnp.int32, sc.shape, 0)
                ki = lax.broadcasted_iota(jnp.int32, sc.shape, 1) + shift
                sc = jnp.where(ki <= qi, sc, NEG)
            m_new = jnp.maximum(m, jnp.max(sc, axis=-1, keepdims=True))
            p = jnp.exp2(sc - m_new).astype(BF16)
            return m_new, jnp.exp2(m - m_new) * acc + _dot(p, v_ref[0, rows, :])

        def trip(p, carry):
            for b in range(nkv):
                carry = tuple(unit(r, nkv * p + b, cr) for r, cr in enumerate(carry))
            return carry

        init = (jnp.full((tc, 1), NEG, F32), jnp.zeros((tc, 2 * V_HEAD), F32))
        carry = list(lax.fori_loop(0, i, trip, (init,) * nch))
        for b in range(nkv):
            for r in range(nch):
                shift = b * tk - r * tc
                if shift < tc:
                    carry[r] = unit(r, nkv * i + b, carry[r], None if shift + tk - 1 <= 0 else shift)
        for r, (m, acc) in enumerate(carry):
            l = acc[:, V_HEAD:]
            o_ref[r * tc:(r + 1) * tc, :] = (acc[:, :V_HEAD] / l).astype(BF16)
            lse = m + jnp.log(l[:, 0:1]) * LOG2E
            lse_ref[0, :, r * tc:(r + 1) * tc] = jnp.broadcast_to(lse, (tc, LANES)).T[0:1, :]

    return pl.pallas_call(
        body, name="flash_fwd", grid=(nh, nq),
        in_specs=[pl.BlockSpec((1, tq, QK_PAD), lambda h, i: (h, i, 0)),
                  pl.BlockSpec((1, s, QK_PAD), lambda h, i: (h, 0, 0)),
                  pl.BlockSpec((1, s, 2 * V_HEAD), lambda h, i: (h, 0, 0))],
        out_specs=[pl.BlockSpec((tq, V_HEAD), lambda h, i: (i, h)), pl.BlockSpec((1, 1, tq), lambda h, i: (h, 0, i))],
        out_shape=[jax.ShapeDtypeStruct((s, nh * V_HEAD), BF16), jax.ShapeDtypeStruct((nh, 1, s), F32)],
        compiler_params=_params(("parallel", "arbitrary")),
    )(q, k, v)


def _flash_bwd(q, k, v, do, lse, delta):
    nh, s, _ = q.shape
    tq = min(TQ, s)
    nq = s // tq
    kps = 2 if nq % 2 == 0 else 1
    ng = nq // kps

    def body(q_ref, k_ref, v_ref, do_ref, lse_ref, dl_ref, dq_ref, dk_ref, dv_ref, dq_acc):
        g = ng - 1 - pl.program_id(1)

        @pl.when(g == ng - 1)
        def _():
            dq_acc[...] = jnp.zeros_like(dq_acc)

        for sub in reversed(range(kps)):
            kv_block(q_ref, k_ref, v_ref, do_ref, lse_ref, dl_ref, dk_ref, dv_ref, dq_acc, g * kps + sub, sub)

        @pl.when(g == 0)
        def _():
            dq_ref[0] = dq_acc[...].astype(BF16)

    def kv_block(q_ref, k_ref, v_ref, do_ref, lse_ref, dl_ref, dk_ref, dv_ref, dq_acc, j, sub):
        own = slice(sub * tq, (sub + 1) * tq)
        kj, vj = k_ref[0, own, :], v_ref[0, own, :]

        def block(kk, vv, qq, dd, lse, dl, masked):
            st = _dot_nt(kk, qq)
            pt = jnp.exp2(st - lse)
            if masked:
                ki = lax.broadcasted_iota(jnp.int32, st.shape, 0)
                qx = lax.broadcasted_iota(jnp.int32, st.shape, 1)
                pt = jnp.where(ki <= qx, pt, 0.0)
            ddv = _dot(pt.astype(BF16), dd)
            dst = (pt * (_dot_nt(vv, dd) - dl)).astype(BF16)
            ddq = _dot_tn(dst, kk)
            return _dot(dst, qq), ddv, ddq

        def step(i, carry):
            dk, dv = carry
            rows = pl.ds(pl.multiple_of(i * tq, tq), tq)
            ddk, ddv, ddq = block(kj, vj, q_ref[0, rows, :], do_ref[rows, :], lse_ref[0, pl.ds(i, 1), :],
                                  dl_ref[0, pl.ds(i, 1), :], False)
            dq_acc[rows, :] += ddq
            return dk + ddk, dv + ddv

        th = tq // 2
        lse_j, dl_j = lse_ref[0, pl.ds(j, 1), :], dl_ref[0, pl.ds(j, 1), :]
        parts = []
        for kh, qh, masked in ((0, 0, True), (0, 1, False), (1, 1, True)):
            rows = pl.ds(pl.multiple_of(j * tq + qh * th, th), th)
            ks, qs = slice(kh * th, (kh + 1) * th), slice(qh * th, (qh + 1) * th)
            ddk, ddv, ddq = block(kj[ks], vj[ks], q_ref[0, rows, :], do_ref[rows, :], lse_j[:, qs], dl_j[:, qs], masked)
            dq_acc[rows, :] += ddq
            parts.append((ddk, ddv))
        carry = (jnp.concatenate([parts[0][0] + parts[1][0], parts[2][0]], axis=0),
                 jnp.concatenate([parts[0][1] + parts[1][1], parts[2][1]], axis=0))
        dk, dv = lax.fori_loop(j + 1, nq, step, carry)
        dk_ref[0, own, :] = (dk * LN2).astype(BF16)
        dv_ref[0, own, :] = dv.astype(BF16)

    return pl.pallas_call(
        body, name="flash_bwd", grid=(nh, ng),
        in_specs=[pl.BlockSpec((1, s, QK_PAD), lambda h, j: (h, 0, 0)),
                  pl.BlockSpec((1, kps * tq, QK_PAD), lambda h, g: (h, ng - 1 - g, 0)),
                  pl.BlockSpec((1, kps * tq, V_HEAD), lambda h, g: (h, ng - 1 - g, 0)),
                  pl.BlockSpec((s, V_HEAD), lambda h, j: (0, h)),
                  pl.BlockSpec((1, nq, tq), lambda h, j: (h, 0, 0)),
                  pl.BlockSpec((1, nq, tq), lambda h, j: (h, 0, 0))],
        out_specs=[pl.BlockSpec((1, s, QK_PAD), lambda h, j: (h, 0, 0)),
                   pl.BlockSpec((1, kps * tq, QK_PAD), lambda h, g: (h, ng - 1 - g, 0)),
                   pl.BlockSpec((1, kps * tq, V_HEAD), lambda h, g: (h, ng - 1 - g, 0))],
        out_shape=[jax.ShapeDtypeStruct((nh, s, QK_PAD), BF16), jax.ShapeDtypeStruct((nh, s, QK_PAD), BF16),
                   jax.ShapeDtypeStruct((nh, s, V_HEAD), BF16)],
        scratch_shapes=[pltpu.VMEM((s, QK_PAD), F32)],
        compiler_params=_params(("parallel", "arbitrary")),
    )(q, k, v, do, lse, delta)


def _tail(x, target, o, u, mod, w_out, conv_w):
    s, d = x.shape
    tm = min(TM_ELEM, s)

    def body(x_ref, t_ref, o_ref, za_ref, mod_ref, w_ref, xc_ref, bc_ref, cc_ref, zc_ref, xp_ref, cp_ref, cw_ref,
             gx_ref, dy_ref, ycat_ref, dyc_ref, do_ref, du_ref, delta_ref, dgate_ref, loss_ref):
        i = pl.program_id(0)
        za = pltpu.roll(za_ref[:, ZA_LO:ZA_LO + ZA_WIN].astype(F32), ZA_WIN - QK_ROPE, 1)[:, :D_ATTN]
        ov = o_ref[...].astype(F32)
        sg = _sigmoid(za)
        sl = za * sg
        ya = ov * sl
        y = _dot(ya.astype(BF16), w_ref[D_CONV:, :])
        yc = _conv_y(xc_ref, bc_ref, cc_ref, zc_ref, xp_ref, cp_ref, cw_ref, i == 0)
        y = y + _dot(yc.astype(BF16), w_ref[:D_CONV, :])
        ycat_ref[...] = jnp.concatenate([yc.T, ya.T], axis=0).astype(BF16)
        gate = mod_ref[:, 2 * d:3 * d]
        e = x_ref[...] + gate * y - t_ref[...]
        dout = e * (1.0 / d)
        gx_ref[...] = dout
        dy = (dout * gate).astype(BF16)
        dy_ref[...] = dy
        dycat = _dot_nt(dy, w_ref[...])
        dyc_ref[...] = dycat[:, :D_CONV].astype(BF16)
        dya = dycat[:, D_CONV:]
        dov = dya * sl
        do_ref[...] = dov.astype(BF16)
        du_ref[...] = (dya * ov * (sg * (1.0 + za * (1.0 - sg)))).astype(BF16)
        prod_t = (dov * ov).T
        for h in range(N_HEADS):
            delta_ref[h] = jnp.sum(prod_t[V_HEAD * h:V_HEAD * (h + 1), :], axis=0, keepdims=True)
        dgate = jnp.sum(dout * y, axis=0, keepdims=True)
        part = jnp.sum(jnp.sum(e * e, axis=0, keepdims=True), axis=1, keepdims=True) * (0.5 / d)
        part = jnp.broadcast_to(part, (1, LANES))

        @pl.when(i == 0)
        def _():
            dgate_ref[...] = dgate
            loss_ref[...] = part

        @pl.when(i > 0)
        def _():
            dgate_ref[...] += dgate
            loss_ref[...] += part

    tok = lambda w: pl.BlockSpec((tm, w), lambda i: (i, 0))
    return pl.pallas_call(
        body, name="tail", grid=(s // tm,),
        in_specs=[tok(d), tok(d), tok(D_ATTN), pl.BlockSpec((tm, U_TAIL), lambda i: (i, U_COLS // U_TAIL - 1)),
                  _full((1, 3 * d)), _full((d, d))] + _conv_specs(tm) + [_full((3, D_CONV))],
        out_specs=[tok(d), tok(d), pl.BlockSpec((d, tm), lambda i: (0, i)), tok(D_CONV), tok(D_ATTN), tok(D_ATTN),
                   pl.BlockSpec((N_HEADS, 1, tm), lambda i: (0, 0, i)), _full((1, d)), _full((1, LANES))],
        out_shape=[jax.ShapeDtypeStruct((s, d), F32), jax.ShapeDtypeStruct((s, d), BF16),
                   jax.ShapeDtypeStruct((d, s), BF16), jax.ShapeDtypeStruct((s, D_CONV), BF16),
                   jax.ShapeDtypeStruct((s, D_ATTN), BF16), jax.ShapeDtypeStruct((s, D_ATTN), BF16),
                   jax.ShapeDtypeStruct((N_HEADS, 1, s), F32), jax.ShapeDtypeStruct((1, d), F32),
                   jax.ShapeDtypeStruct((1, LANES), F32)],
        compiler_params=_params(("arbitrary",)),
    )(x, target, o, u, mod, w_out, u, u, u, u, u, u, conv_w)


def _norm_bwd(x, dh, gx1, norm_g, mod):
    s, d = x.shape
    tm = min(TM_MM, s)

    def body(x_ref, dh_ref, gx_ref, g_ref, mod_ref, o_ref, dshift_ref, dscale_ref, dg_ref):
        i = pl.program_id(0)
        gv, sc1 = g_ref[...], 1.0 + mod_ref[:, d:2 * d]
        gsc = gv * sc1
        half = NORM_ROWS // 2

        def group(c, acc):
            a_dh, a_dhxn = acc
            ks = range(NORM_GROUP)
            rows = [pl.ds(pl.multiple_of((c * NORM_GROUP + k) * NORM_ROWS, NORM_ROWS), NORM_ROWS) for k in ks]
            xv = [x_ref[rows[k], :] for k in ks]
            dhv = [dh_ref[rows[k], :].astype(F32) for k in ks]
            r = [lax.rsqrt(jnp.mean(xv[k] * xv[k], axis=-1, keepdims=True) + EPS) for k in ks]
            xn = [xv[k] * r[k] for k in ks]
            dxn = [dhv[k] * gsc for k in ks]
            t = [jnp.mean(dxn[k] * xn[k], axis=-1, keepdims=True) for k in ks]
            for k in ks:
                o_ref[rows[k], :] = gx_ref[rows[k], :] + r[k] * (dxn[k] - xn[k] * t[k])
                dhxn = dhv[k] * xn[k]
                a_dh = a_dh + dhv[k][:half] + dhv[k][half:]
                a_dhxn = a_dhxn + dhxn[:half] + dhxn[half:]
            return a_dh, a_dhxn

        zero = jnp.zeros((half, d), F32)
        a_dh, a_dhxn = lax.fori_loop(0, tm // (NORM_ROWS * NORM_GROUP), group, (zero, zero))
        dshift = jnp.sum(a_dh, axis=0, keepdims=True)
        s_dhxn = jnp.sum(a_dhxn, axis=0, keepdims=True)
        dscale, dg = s_dhxn * gv, s_dhxn * sc1

        @pl.when(i == 0)
        def _():
            dshift_ref[...] = dshift
            dscale_ref[...] = dscale
            dg_ref[...] = dg

        @pl.when(i > 0)
        def _():
            dshift_ref[...] += dshift
            dscale_ref[...] += dscale
            dg_ref[...] += dg

    tok = pl.BlockSpec((tm, d), lambda i: (i, 0))
    row = jax.ShapeDtypeStruct((1, d), F32)
    return pl.pallas_call(
        body, name="norm_bwd", grid=(s // tm,),
        in_specs=[tok, tok, tok, _full((1, d)), _full((1, 3 * d))],
        out_specs=[tok, _full((1, d)), _full((1, d)), _full((1, d))],
        out_shape=[jax.ShapeDtypeStruct((s, d), F32), row, row, row],
        compiler_params=_params(("arbitrary",)),
    )(x, dh, gx1, norm_g, mod)


def _adamw(w, g, m, v, name):
    rows, cols = w.shape
    tr = 256 if rows % 256 == 0 else rows
    tc = 512 if (rows > 256 and tr == rows and cols % 512 == 0) else cols

    def body(w_ref, g_ref, m_ref, v_ref, d_ref, nm_ref, nv_ref):
        gv = g_ref[...]
        nm = ADAM_B1 * m_ref[...] + (1.0 - ADAM_B1) * gv
        nv = ADAM_B2 * v_ref[...] + (1.0 - ADAM_B2) * (gv * gv)
        m_hat = nm / (1.0 - ADAM_B1 ** ADAM_STEP)
        v_hat = nv / (1.0 - ADAM_B2 ** ADAM_STEP)
        d_ref[...] = -ADAM_LR * (m_hat / (jnp.sqrt(v_hat) + ADAM_EPS) + ADAM_WD * w_ref[...])
        nm_ref[...] = nm
        nv_ref[...] = nv

    spec = pl.BlockSpec((tr, tc), lambda i, j: (i, j))
    shape = jax.ShapeDtypeStruct((rows, cols), F32)
    return pl.pallas_call(
        body, name=name, grid=(rows // tr, cols // tc), in_specs=[spec] * 4, out_specs=[spec] * 3, out_shape=[shape] * 3,
        compiler_params=_params(("parallel", "parallel")),
    )(w, g, m, v)


def _pad_cols(a, n):
    return jnp.pad(a, ((0, 0), (0, n - a.shape[1])))


def kernel(x, c, positions, ada_w, ada_b, norm_g, w_in, conv_w, q_a_g, w_q_b, kv_a_g, w_kv_b, q_g, k_g, w_out, loss_target, m_ada_w, m_ada_b, m_norm_g, m_w_in, m_conv_w, m_q_a_g, m_w_q_b, m_kv_a_g, m_w_kv_b, m_q_g, m_k_g, m_w_out, v_ada_w, v_ada_b, v_norm_g, v_w_in, v_conv_w, v_q_a_g, v_w_q_b, v_kv_a_g, v_w_kv_b, v_q_g, v_k_g, v_w_out):
    me = _my_index()
    s = x.shape[1]
    nq = s // min(TQ, s)
    x2, tgt = x[0], loss_target[0]
    w_in_l, w_q_l, w_kv_l, w_out_l, conv_l, ada_w_l = w_in[0], w_q_b[0], w_kv_b[0], w_out[0], conv_w[0], ada_w[0]
    ada_cols = ada_w_l.shape[1]

    small = jnp.concatenate([c.reshape(-1, LANES), conv_l.reshape(-1, LANES), jnp.zeros((5, LANES), F32)], axis=0)
    (small_g,) = _all_gather([small], "gather_c")
    c_all = small_g[:, :D_MODEL // LANES].reshape(N_DEV, D_MODEL)
    conv_g = small_g[:, D_MODEL // LANES:D_MODEL // LANES + 3].transpose(1, 0, 2).reshape(3, D_CONV)

    ada_b_l = lax.dynamic_slice(ada_b, (0, me * ada_cols), (1, ada_cols))
    mod_cols = _ada_mod(jnp.pad(c_all, ((0, 8), (0, 0))), ada_w_l, ada_b_l)[:N_DEV]
    (mod_g,) = _all_gather([mod_cols], "gather_mod")
    mod = lax.dynamic_index_in_dim(mod_g, me, axis=1, keepdims=False).reshape(1, 3 * D_MODEL)

    half = jnp.arange(0, QK_ROPE, 2, dtype=F32) / QK_ROPE
    inv_freq = ROPE_BASE ** (-half)
    zeros64 = jnp.zeros((LANES - QK_ROPE,), F32)
    invf = jnp.concatenate([inv_freq, inv_freq, zeros64]).reshape(1, LANES)
    sign = jnp.concatenate([-jnp.ones((32,), F32), jnp.ones((32,), F32), zeros64]).reshape(1, LANES)
    qg_p, kg_p = _pad_cols(q_g, QK_PAD), _pad_cols(k_g, QK_PAD)

    my_off = ((CW * me) % LANES).astype(jnp.int32)
    win = [_expand_w_in(w_in_l.T, my_off.reshape(1))]
    h, h_t, cos, sin, win_g = _norm_mod(x2, norm_g, mod, positions.reshape(s, 1), invf, sign, _Gather(win, relay=True, parts=4), win)
    w_in_p = _merge_w_in(win_g)
    rest = [_pad_wq(w_q_l.T), w_kv_l.astype(BF16), w_out_l.astype(BF16)]
    u, wq_g, wkv_g, w_out_g = _matmul(h, w_in_p, nt=False, out_dtype=BF16, tm=2 * TM_MM, tn=2048, name="in_proj",
                                      rider=_Gather(rest), rider_inputs=rest)
    w_out_g = w_out_g.reshape(D_MODEL, D_MODEL)
    wq_g = wq_g.transpose(1, 0, 2).reshape(Q_LORA, N_HEADS * QK_PAD)
    wkv_g = wkv_g.transpose(1, 0, 2).reshape(KV_LORA, 2 * D_ATTN)
    q, k, v = _qkv_fwd(u, cos, sin, wq_g, wkv_g, q_a_g, kv_a_g, qg_p, kg_p)
    o, lse = _flash_fwd(q, k, v)
    gx1, dy, ycat_t, dyc, do, dza, delta, dgate, loss_row = _tail(x2, tgt, o, u, mod, w_out_g, conv_g)

    dq, dk, dv = _flash_bwd(q, k, v, do, lse.reshape(N_HEADS, nq, s // nq), delta.reshape(N_HEADS, nq, s // nq))
    du, dconv = _conv_bwd(u, dyc, conv_g)
    du, dwq, dwkv, dqag, dkvag, dqg, dkg = _qkv_bwd(u, cos, sin, dq, dk, dv, dza, wq_g, wkv_g, q_a_g, kv_a_g, qg_p, kg_p, du)
    dwq = dwq.reshape(Q_LORA, N_HEADS, QK_PAD).transpose(1, 0, 2)
    dwkv = dwkv.reshape(KV_LORA, N_HEADS, 2 * V_HEAD).transpose(1, 0, 2)
    dw_in = _matmul(h_t, du, nt=False, out_dtype=BF16, tm=TM_MM, tn=768, name="dw_in")
    first = [dw_in, dwq, dwkv]
    dw_out, r_in, r_q, r_kv = _matmul(ycat_t, dy, nt=False, out_dtype=BF16, tm=TM_MM, tn=512, name="dw_out",
                                      rider=_SiblingExchange(first, [True, False, False]), rider_inputs=first)
    dw_out = dw_out.reshape(N_DEV, D_MODEL // N_DEV, D_MODEL)
    (r_out,) = _exchange(_SiblingExchange([dw_out], [False]), [dw_out], "rs_sibling_out")
    core = lax.axis_index("c").astype(jnp.int32)
    lo_tiles = ((CW * (2 * jnp.arange(4, dtype=jnp.int32) + core)) // LANES).astype(jnp.int32)
    pairs = [_add_window(dw_in, r_in, lo_tiles), _add_pairs(dwq, r_q, core.reshape(1), "rs_add_q"),
             _add_pairs(dwkv, r_kv, core.reshape(1), "rs_add_kv"), _add_pairs(dw_out, r_out, core.reshape(1), "rs_add_out")]
    dh, *quads = _matmul(du, w_in_p, nt=True, out_dtype=BF16, tm=2 * TM_MM, tn=512, name="dh",
                         rider=_ChipExchange(pairs), rider_inputs=pairs, a_resident=True)
    my_chip = 2 * lax.axis_index("x") + lax.axis_index("y")
    written = jnp.where(jnp.arange(4) == my_chip, (jnp.arange(4) + 1) % 4, jnp.arange(4))
    sel = jnp.concatenate([my_chip.reshape(1), written, ((EXP_W - my_off) % EXP_W).reshape(1)]).astype(jnp.int32)
    g_w_in_t = _final_sum(pairs[0], quads[0], sel, "rs_sum_in", unshift=True, keep_t=CW)
    g_w_q_t = _final_sum(pairs[1], quads[1], sel, "rs_sum_q", keep_t=QK_HEAD)
    g_w_kv = _final_sum(pairs[2], quads[2], sel, "rs_sum_kv")
    g_w_out = _final_sum(pairs[3], quads[3], sel, "rs_sum_out")
    grad_x, dshift, dscale, dng = _norm_bwd(x2, dh, gx1, norm_g, mod)

    row = jnp.concatenate([dshift, dscale, dgate, dng, dqag, dkvag, dqg, dkg, dconv[:3].reshape(1, 3 * D_CONV), loss_row], axis=1)
    (rows_g,) = _all_gather([row], "gather_small")
    tot = _sum_leading(rows_g, F32, "sum_small")
    dmod_all = rows_g[:, 0, SM_MOD:SM_NG]
    g_ada_b = tot[:, SM_MOD:SM_NG]
    g_norm_g = tot[:, SM_NG:SM_QAG]
    g_q_a_g = tot[:, SM_QAG:SM_KVAG]
    g_kv_a_g = tot[:, SM_KVAG:SM_QG]
    g_q_g = tot[:, SM_QG:SM_QG + QK_HEAD]
    g_k_g = tot[:, SM_KG:SM_KG + QK_HEAD]
    conv_cols = conv_l.shape[1]
    g_conv = lax.dynamic_slice(tot[:, SM_CONV:SM_LOSS].reshape(3, D_CONV), (0, me * conv_cols), (3, conv_cols))
    loss = tot[0, SM_LOSS]
    dmod_my = lax.dynamic_slice(dmod_all, (0, me * ada_cols), (N_DEV, ada_cols))
    g_ada_w = _ada_w_grad(c_all.T, dmod_my)

    grads = dict(ada_w=g_ada_w, ada_b=g_ada_b, norm_g=g_norm_g, w_in=g_w_in_t, conv_w=g_conv, q_a_g=g_q_a_g, w_q_b=g_w_q_t,
                 kv_a_g=g_kv_a_g, w_kv_b=g_w_kv, q_g=g_q_g, k_g=g_k_g, w_out=g_w_out)
    weights = dict(ada_w=(ada_w, m_ada_w, v_ada_w), ada_b=(ada_b, m_ada_b, v_ada_b), norm_g=(norm_g, m_norm_g, v_norm_g),
                   w_in=(w_in, m_w_in, v_w_in), conv_w=(conv_w, m_conv_w, v_conv_w), q_a_g=(q_a_g, m_q_a_g, v_q_a_g),
                   w_q_b=(w_q_b, m_w_q_b, v_w_q_b), kv_a_g=(kv_a_g, m_kv_a_g, v_kv_a_g), w_kv_b=(w_kv_b, m_w_kv_b, v_w_kv_b),
                   q_g=(q_g, m_q_g, v_q_g), k_g=(k_g, m_k_g, v_k_g), w_out=(w_out, m_w_out, v_w_out))
    names = list(grads)
    out_g, out_d, out_m, out_v = [], [], [], []
    for n in names:
        w, m, v_ = weights[n]
        shape2 = w.shape[-2:] if w.ndim == 3 else (1, w.shape[-1])
        transposed = n in ("w_in", "w_q_b")
        to2 = (lambda a: a.reshape(shape2).T) if transposed else (lambda a: a.reshape(shape2))
        back = (lambda a: a.T.reshape(w.shape)) if transposed else (lambda a: a.reshape(w.shape))
        g2 = grads[n] if transposed else grads[n].reshape(shape2)
        d2, m2, v2 = _adamw(to2(w), g2, to2(m), to2(v_), "adamw_" + n)
        out_g.append(back(g2))
        out_d.append(back(d2))
        out_m.append(back(m2))
        out_v.append(back(v2))
    return (loss, grad_x.reshape(x.shape), *out_g, *out_d, *out_m, *out_v)
```

```python
import functools
import math

import jax
import jax.numpy as jnp
from jax import lax
from jax.experimental import pallas as pl
from jax.experimental.pallas import tpu as pltpu

F32 = jnp.float32
BF16 = jnp.bfloat16
MESH = pl.DeviceIdType.MESH

D_MODEL = 2048
D_CONV = 1024
N_HEADS = 8
QK_NOPE = 128
QK_ROPE = 64
QK_HEAD = QK_NOPE + QK_ROPE
V_HEAD = 128
D_ATTN = N_HEADS * V_HEAD
Q_LORA = 512
KV_LORA = 256
ROPE_BASE = 10000.0
IN_COLS = 4 * D_CONV + Q_LORA + KV_LORA + QK_ROPE + D_ATTN
EPS = 1e-6
ADAM_LR, ADAM_B1, ADAM_B2, ADAM_EPS, ADAM_WD, ADAM_STEP = 0.001, 0.9, 0.999, 1e-08, 0.01, 10

N_DEV = 8
LANES = 128
QK_PAD = 256
U_COLS = 6144
U_CQ, U_CKV, U_KR, U_ZA = 4096, 4608, 4864, 4928
U_TAIL = 2048
ZA_LO = U_ZA - (U_COLS - U_TAIL) - QK_ROPE
ZA_WIN = D_ATTN + LANES
CW = IN_COLS // 8
EXP_W = 896
W_LO = [(CW * d // 128) * 128 for d in range(8)]
W_OFF = [CW * d - lo for d, lo in enumerate(W_LO)]
SCALE = 1.0 / math.sqrt(QK_HEAD)
LOG2E = 1.4426950408889634
LN2 = 0.6931471805599453
NEG = -1e30
VMEM_LIMIT = 56 * 1024 * 1024

TM_ELEM = 256
NORM_ROWS = 16
NORM_GROUP = 4
TM_MM = 512
TQ = 1024
Q_CHAINS = 4
KV_SPLIT = 2

SM_MOD, SM_NG, SM_QAG, SM_KVAG, SM_QG, SM_KG, SM_CONV, SM_LOSS = 0, 6144, 8192, 8704, 8960, 9216, 9472, 12544
SM_COLS = 12672


def _params(sem=None):
    kw = dict(vmem_limit_bytes=VMEM_LIMIT)
    if sem is not None:
        kw["dimension_semantics"] = sem
    return pltpu.CompilerParams(**kw)


def _sigmoid(z):
    return 1.0 / (1.0 + jnp.exp(-z))


def _rot64(x):
    lane = lax.broadcasted_iota(jnp.int32, x.shape, 1)
    return jnp.where(lane < 32, pltpu.roll(x, 96, 1), pltpu.roll(x, 32, 1))


def _rope(x, cos, sin):
    return x * cos + _rot64(x) * sin


def _rope_t(d, cos, sin):
    return d * cos - _rot64(d) * sin


def _dot(a, b):
    return jnp.dot(a, b, preferred_element_type=F32)


def _dot_nt(a, b):
    return lax.dot_general(a, b, (((1,), (1,)), ((), ())), preferred_element_type=F32)


def _dot_tn(a, b):
    return lax.dot_general(a, b, (((0,), (0,)), ((), ())), preferred_element_type=F32)


def _my_index():
    return 4 * lax.axis_index("x") + 2 * lax.axis_index("y") + lax.axis_index("c")


ANY = pl.BlockSpec(memory_space=pl.ANY)


class _Gather:
    def __init__(self, blocks, relay=False, parts=1):
        self.relay = relay
        self.parts = parts
        self.rows = [b.shape[0] // parts for b in blocks]
        self.n = n = len(blocks) * parts
        self.out_shape = [jax.ShapeDtypeStruct((N_DEV,) + b.shape, b.dtype) for b in blocks]
        self.scratch = [pltpu.SemaphoreType.DMA((7 * n,)), pltpu.SemaphoreType.DMA((7 * n,)),
                        pltpu.SemaphoreType.DMA((n,))]

    @staticmethod
    def _places():
        x, y, c = lax.axis_index("x"), lax.axis_index("y"), lax.axis_index("c")
        return (x, y, c), (x, y, 1 - c), [(1 - x, y), (x, 1 - y), (1 - x, 1 - y)]

    def _src(self, ins, a):
        block, part = divmod(a, self.parts)
        return ins[block] if self.parts == 1 else ins[block].at[pl.ds(part * self.rows[block], self.rows[block])]

    def _dst(self, outs, a, place):
        block, part = divmod(a, self.parts)
        ref = outs[block].at[4 * place[0] + 2 * place[1] + place[2]]
        return ref if self.parts == 1 else ref.at[pl.ds(part * self.rows[block], self.rows[block])]

    def _copy(self, outs, sems, a, k, block, to, src=None):
        dst = self._dst(outs, a, block)
        return pltpu.make_async_remote_copy(
            src_ref=dst if src is None else src, dst_ref=dst, send_sem=sems[0].at[7 * a + k],
            recv_sem=sems[1].at[7 * a + k], device_id=to, device_id_type=MESH)

    def _first(self, ins, outs, sems):
        me, sibling, chips = self._places()
        first = []
        for a in range(self.n):
            first.append(self._copy(outs, sems, a, 0, me, sibling, src=self._src(ins, a)))
            first += [self._copy(outs, sems, a, 1 + j, me, (*chip, me[2]), src=self._src(ins, a))
                      for j, chip in enumerate(chips[:2] if self.relay else chips)]
        return first

    def _relays(self, outs, sems):
        if not self.relay:
            return []
        (x, y, c), _, _ = self._places()
        via = (jnp.where(c == 0, 1 - x, x), jnp.where(c == 0, y, 1 - y))
        to = (jnp.where(c == 0, x, 1 - x), jnp.where(c == 0, 1 - y, y))
        return [self._copy(outs, sems, a, 3, (*via, c), (*to, c)) for a in range(self.n)]

    def _passed(self, outs, sems):
        me, sibling, chips = self._places()
        return [self._copy(outs, sems, a, 4 + j, (*chip, me[2]), sibling)
                for a in range(self.n) for j, chip in enumerate(chips)]

    def _mine(self, ins, outs, sems):
        me, _, _ = self._places()
        return [pltpu.make_async_copy(self._src(ins, a), self._dst(outs, a, me), sems[2].at[a]) for a in range(self.n)]

    def start(self, ins, outs, sems):
        for cp in self._mine(ins, outs, sems) + self._first(ins, outs, sems):
            cp.start()

    def forward(self, ins, outs, sems):
        del ins
        me, _, chips = self._places()
        passed, relays = self._passed(outs, sems), self._relays(outs, sems)
        for a in range(self.n):
            for j, chip in enumerate(chips[:2] if self.relay else chips):
                self._copy(outs, sems, a, 1 + j, (*chip, me[2]), me).wait_recv()
                passed[3 * a + j].start()
            if self.relay:
                relays[a].start()
        if self.relay:
            for a in range(self.n):
                self._copy(outs, sems, a, 3, (*chips[2], me[2]), me).wait_recv()
                passed[3 * a + 2].start()

    def finish(self, ins, outs, sems):
        me, sibling, chips = self._places()
        for a in range(self.n):
            self._copy(outs, sems, a, 0, sibling, me).wait_recv()
            for j, chip in enumerate(chips):
                self._copy(outs, sems, a, 4 + j, (*chip, 1 - me[2]), me).wait_recv()
        for cp in self._first(ins, outs, sems) + self._relays(outs, sems) + self._passed(outs, sems):
            cp.wait_send()
        for cp in self._mine(ins, outs, sems):
            cp.wait()


class _ChipExchange:
    def __init__(self, arrays):
        self.n = n = len(arrays)
        self.out_shape = [jax.ShapeDtypeStruct(a.shape, a.dtype) for a in arrays]
        self.scratch = [pltpu.SemaphoreType.DMA((3 * n,)), pltpu.SemaphoreType.DMA((3 * n,))]

    def _copies(self, ins, outs, sems):
        x, y, c = lax.axis_index("x"), lax.axis_index("y"), lax.axis_index("c")
        return [pltpu.make_async_remote_copy(
            src_ref=ins[a].at[2 * px + py], dst_ref=outs[a].at[2 * x + y], send_sem=sems[0].at[3 * a + j],
            recv_sem=sems[1].at[3 * a + j], device_id=(px, py, c), device_id_type=MESH)
            for a in range(self.n) for j, (px, py) in enumerate([(1 - x, y), (x, 1 - y), (1 - x, 1 - y)])]

    def start(self, ins, outs, sems):
        for cp in self._copies(ins, outs, sems):
            cp.start()

    def forward(self, ins, outs, sems):
        pass

    def finish(self, ins, outs, sems):
        for cp in self._copies(ins, outs, sems):
            cp.wait()


def _all_gather(blocks, name):
    n = len(blocks)
    g = _Gather(blocks)

    def body(*refs):
        ins, outs, sems = refs[:n], refs[n:2 * n], refs[2 * n:]
        g.start(ins, outs, sems)
        g.forward(ins, outs, sems)
        g.finish(ins, outs, sems)

    return pl.pallas_call(body, name=name, out_shape=g.out_shape, in_specs=[ANY] * n, out_specs=[ANY] * n,
                          scratch_shapes=g.scratch)(*blocks)


class _SiblingExchange:
    def __init__(self, arrays, windowed):
        self.n = n = len(arrays)
        self.windowed = windowed
        self.out_shape = [jax.ShapeDtypeStruct((4, a.shape[0], EXP_W) if w else (4,) + a.shape[1:], a.dtype)
                          for a, w in zip(arrays, windowed)]
        self.scratch = [pltpu.SemaphoreType.DMA((4 * n,)), pltpu.SemaphoreType.DMA((4 * n,))]

    def _each(self, ins, outs, sems, act):
        x, y, c = lax.axis_index("x"), lax.axis_index("y"), lax.axis_index("c")

        def branch(c_val):
            for k in range(4):
                e = 2 * k + (1 - c_val)
                for a in range(self.n):
                    src = ins[a].at[:, pl.ds(W_LO[e], EXP_W)] if self.windowed[a] else ins[a].at[e]
                    act(pltpu.make_async_remote_copy(
                        src_ref=src, dst_ref=outs[a].at[k], send_sem=sems[0].at[4 * a + k], recv_sem=sems[1].at[4 * a + k],
                        device_id=(x, y, 1 - c), device_id_type=MESH))

        for c_val in (0, 1):
            pl.when(c == c_val)(functools.partial(branch, c_val))

    def start(self, ins, outs, sems):
        self._each(ins, outs, sems, lambda cp: cp.start())

    def forward(self, ins, outs, sems):
        pass

    def finish(self, ins, outs, sems):
        self._each(ins, outs, sems, lambda cp: cp.wait())


def _exchange(rider, arrays, name):
    n = len(arrays)

    def body(*refs):
        ins, outs, sems = refs[:n], refs[n:n + len(rider.out_shape)], refs[n + len(rider.out_shape):]
        rider.start(ins, outs, sems)
        rider.forward(ins, outs, sems)
        rider.finish(ins, outs, sems)

    return pl.pallas_call(body, name=name, out_shape=rider.out_shape, in_specs=[ANY] * n,
                          out_specs=[ANY] * len(rider.out_shape), scratch_shapes=rider.scratch)(*arrays)


def _add_window(dw_in, recv, lo_tiles):
    k, rows, _ = recv.shape

    def body(t_ref, w_ref, r_ref, o_ref):
        del t_ref
        o_ref[0] = (w_ref[...].astype(F32) + r_ref[0].astype(F32)).astype(o_ref.dtype)

    spec = pl.BlockSpec((1, rows, LANES), lambda i, j, t: (i, 0, j))
    grid_spec = pltpu.PrefetchScalarGridSpec(
        num_scalar_prefetch=1, grid=(k, EXP_W // LANES),
        in_specs=[pl.BlockSpec((rows, LANES), lambda i, j, t: (0, t[i] + j)), spec], out_specs=spec)
    return pl.pallas_call(
        body, name="rs_add_in", grid_spec=grid_spec, out_shape=jax.ShapeDtypeStruct(recv.shape, recv.dtype),
        compiler_params=_params(("parallel", "parallel")),
    )(lo_tiles, dw_in, recv)


def _final_sum(p, r, sel, name, unshift=False, keep_t=None):
    _, rows, cols = p.shape
    tr = 512 if rows % 512 == 0 else rows

    def body(sel_ref, p_ref, r0, r1, r2, r3, o_ref):
        own = p_ref[0].astype(F32)
        acc = None
        for k, r_ref in enumerate((r0, r1, r2, r3)):
            term = jnp.where(sel_ref[0] == k, own, r_ref[0].astype(F32))
            acc = term if acc is None else acc + term
        if unshift:
            acc = pltpu.roll(acc, sel_ref[5], 1)
        o_ref[...] = acc if keep_t is None else acc.T[:keep_t]

    def slot(k):
        return pl.BlockSpec((1, tr, cols), lambda i, t: (t[k], i, 0))

    if keep_t is None:
        out_spec, out_shape = pl.BlockSpec((tr, cols), lambda i, t: (i, 0)), (rows, cols)
    else:
        out_spec, out_shape = pl.BlockSpec((keep_t, tr), lambda i, t: (0, i)), (keep_t, rows)
    grid_spec = pltpu.PrefetchScalarGridSpec(
        num_scalar_prefetch=1, grid=(rows // tr,), in_specs=[slot(0), slot(1), slot(2), slot(3), slot(4)],
        out_specs=out_spec)
    return pl.pallas_call(
        body, name=name, grid_spec=grid_spec, out_shape=jax.ShapeDtypeStruct(out_shape, F32),
        compiler_params=_params(("parallel",)),
    )(sel, p, r, r, r, r)


def _expand_w_in(w_t, shift):
    cw, rows = w_t.shape
    tr = 256
    pad = -cw % LANES

    def body(s_ref, w_ref, o_ref):
        w = jnp.concatenate([w_ref[...], jnp.zeros((pad, tr), F32)], axis=0).T
        w = jnp.concatenate([w, jnp.zeros((tr, EXP_W - cw - pad), F32)], axis=1)
        o_ref[...] = pltpu.roll(w, s_ref[0], 1).astype(BF16)

    grid_spec = pltpu.PrefetchScalarGridSpec(
        num_scalar_prefetch=1, grid=(rows // tr,), in_specs=[pl.BlockSpec((cw, tr), lambda i, t: (0, i))],
        out_specs=pl.BlockSpec((tr, EXP_W), lambda i, t: (i, 0)))
    return pl.pallas_call(
        body, name="expand_w_in", grid_spec=grid_spec, out_shape=jax.ShapeDtypeStruct((rows, EXP_W), BF16),
        compiler_params=_params(("arbitrary",)),
    )(shift, w_t)


def _pad_wq(w_t):
    cw, rows = w_t.shape

    def body(w_ref, o_ref):
        o_ref[...] = jnp.concatenate([w_ref[...], jnp.zeros((QK_PAD - cw, rows), F32)], axis=0).T.astype(BF16)

    return pl.pallas_call(
        body, name="pad_wq", out_shape=jax.ShapeDtypeStruct((rows, QK_PAD), BF16), compiler_params=_params(),
    )(w_t)


def _merge_w_in(e):
    _, rows, _ = e.shape
    tr = 256

    def body(e_ref, o_ref):
        for t in range(U_COLS // LANES):
            lo, hi = t * LANES, (t + 1) * LANES
            parts = [e_ref[d, :, lo - W_LO[d]:hi - W_LO[d]] for d in range(N_DEV)
                     if CW * d < hi and CW * (d + 1) > lo]
            if not parts:
                tile = jnp.zeros((tr, LANES), BF16)
            elif len(parts) == 1:
                tile = parts[0]
            else:
                tile = (parts[0].astype(F32) + parts[1].astype(F32)).astype(BF16)
            o_ref[:, lo:hi] = tile

    return pl.pallas_call(
        body, name="merge_w_in", grid=(rows // tr,),
        in_specs=[pl.BlockSpec((N_DEV, tr, EXP_W), lambda i: (0, i, 0))],
        out_specs=pl.BlockSpec((tr, U_COLS), lambda i: (i, 0)), out_shape=jax.ShapeDtypeStruct((rows, U_COLS), BF16),
        compiler_params=_params(("parallel",)),
    )(e)


def _sum_leading(a, out_dtype, name):
    k, rows, cols = a.shape
    tr = min(rows, 1728 if rows % 1728 == 0 else rows)

    def body(a_ref, o_ref):
        acc = a_ref[0].astype(F32)
        for i in range(1, k):
            acc = acc + a_ref[i].astype(F32)
        o_ref[...] = acc.astype(out_dtype)

    return pl.pallas_call(
        body, name=name, grid=(rows // tr,),
        in_specs=[pl.BlockSpec((k, tr, cols), lambda i: (0, i, 0))],
        out_specs=pl.BlockSpec((tr, cols), lambda i: (i, 0)),
        out_shape=jax.ShapeDtypeStruct((rows, cols), out_dtype), compiler_params=_params(("parallel",)),
    )(a)


def _add_pairs(g, recv, core, name):
    k, rows, cols = recv.shape
    tr = 1728 if rows % 1728 == 0 else rows

    def body(c_ref, g_ref, r_ref, o_ref):
        del c_ref
        o_ref[...] = (g_ref[...].astype(F32) + r_ref[...].astype(F32)).astype(o_ref.dtype)

    spec = pl.BlockSpec((1, tr, cols), lambda i, j, c: (i, j, 0))
    grid_spec = pltpu.PrefetchScalarGridSpec(
        num_scalar_prefetch=1, grid=(k, rows // tr),
        in_specs=[pl.BlockSpec((1, tr, cols), lambda i, j, c: (2 * i + c[0], j, 0)), spec], out_specs=spec)
    return pl.pallas_call(
        body, name=name, grid_spec=grid_spec, out_shape=jax.ShapeDtypeStruct(recv.shape, recv.dtype),
        compiler_params=_params(("parallel", "parallel")),
    )(core, g, recv)


def _ada_mod(c16, ada_w_l, ada_b_l):
    def body(c_ref, w_ref, b_ref, o_ref):
        cv = c_ref[...]
        sc = (cv * _sigmoid(cv)).astype(BF16)
        o_ref[...] = _dot(sc, w_ref[...].astype(BF16)) + b_ref[...]

    return pl.pallas_call(
        body, name="ada_mod", out_shape=jax.ShapeDtypeStruct((c16.shape[0], ada_w_l.shape[1]), F32),
        compiler_params=_params(),
    )(c16, ada_w_l, ada_b_l)


def _ada_w_grad(c_t, dmod_my):
    def body(c_ref, d_ref, o_ref):
        cv = c_ref[...]
        sc = cv * _sigmoid(cv)
        acc = sc[:, 0:1] * d_ref[0:1, :]
        for b in range(1, N_DEV):
            acc = acc + sc[:, b:b + 1] * d_ref[b:b + 1, :]
        o_ref[...] = acc

    return pl.pallas_call(
        body, name="ada_w_grad", out_shape=jax.ShapeDtypeStruct((c_t.shape[0], dmod_my.shape[1]), F32),
        compiler_params=_params(),
    )(c_t, dmod_my)


def _norm_mod(x, norm_g, mod, pos_col, invf, sign, rider, rider_inputs):
    s, d = x.shape
    tm = min(TM_MM, s)
    n_in, n_out = len(rider_inputs), len(rider.out_shape)
    steps = s // tm

    def body(x_ref, g_ref, mod_ref, p_ref, f_ref, s_ref, *rest):
        r_ins, (h_ref, ht_ref, cos_ref, sin_ref) = rest[:n_in], rest[n_in:n_in + 4]
        r_outs, sems = rest[n_in + 4:n_in + 4 + n_out], rest[n_in + 4 + n_out:]
        pl.when(pl.program_id(0) == 0)(functools.partial(rider.start, r_ins, r_outs, sems))
        xv = x_ref[...]
        r = lax.rsqrt(jnp.mean(xv * xv, axis=-1, keepdims=True) + EPS)
        hn = xv * r * g_ref[...]
        hv = hn * (1.0 + mod_ref[:, d:2 * d]) + mod_ref[:, 0:d]
        h_ref[...] = hv.astype(BF16)
        ht_ref[...] = hv.T.astype(BF16)
        ang = p_ref[...].astype(F32) * f_ref[...]
        sg = s_ref[...]
        cos_ref[...] = jnp.cos(ang) * jnp.abs(sg)
        sin_ref[...] = jnp.sin(ang) * sg

        @pl.when(pl.program_id(0) == steps - 1)
        def _():
            rider.forward(r_ins, r_outs, sems)
            rider.finish(r_ins, r_outs, sems)

    row = pl.BlockSpec((1, LANES), lambda i: (0, 0))
    tab = pl.BlockSpec((tm, LANES), lambda i: (i, 0))
    return pl.pallas_call(
        body, name="norm_mod", grid=(steps,),
        in_specs=[pl.BlockSpec((tm, d), lambda i: (i, 0)), pl.BlockSpec((1, d), lambda i: (0, 0)),
                  pl.BlockSpec((1, 3 * d), lambda i: (0, 0)), pl.BlockSpec((tm, 1), lambda i: (i, 0)), row, row]
        + [ANY] * n_in,
        out_specs=[pl.BlockSpec((tm, d), lambda i: (i, 0)), pl.BlockSpec((d, tm), lambda i: (0, i)), tab, tab] + [ANY] * n_out,
        out_shape=[jax.ShapeDtypeStruct((s, d), BF16), jax.ShapeDtypeStruct((d, s), BF16),
                   jax.ShapeDtypeStruct((s, LANES), F32), jax.ShapeDtypeStruct((s, LANES), F32)] + rider.out_shape,
        scratch_shapes=rider.scratch, compiler_params=_params(("arbitrary",)),
    )(x, norm_g, mod, pos_col, invf, sign, *rider_inputs)


def _matmul(a, b, *, nt, out_dtype, tm, tn, name, rider=None, rider_inputs=(), a_resident=False):
    m, kdim = a.shape
    n = b.shape[0] if nt else b.shape[1]
    tm, tn = min(tm, m), min(tn, n)
    n_in = len(rider_inputs)
    n_out = len(rider.out_shape) if rider else 0
    m_steps, n_steps = m // tm, n // tn
    steps = n_steps * m_steps
    inner = n_steps if a_resident else m_steps
    tile = (lambda o, i: (o, i)) if a_resident else (lambda o, i: (i, o))

    def body(a_ref, b_ref, *rest):
        r_ins, o_ref, r_outs, sems = rest[:n_in], rest[n_in], rest[n_in + 1:n_in + 1 + n_out], rest[n_in + 1 + n_out:]
        step = pl.program_id(0) * inner + pl.program_id(1)
        if rider:
            pl.when(step == 0)(functools.partial(rider.start, r_ins, r_outs, sems))
            pl.when(step == steps // 2)(functools.partial(rider.forward, r_ins, r_outs, sems))
        o = _dot_nt(a_ref[...], b_ref[...]) if nt else _dot(a_ref[...], b_ref[...])
        o_ref[...] = o.astype(out_dtype)
        if rider:
            pl.when(step == steps - 1)(functools.partial(rider.finish, r_ins, r_outs, sems))

    if nt:
        b_spec = pl.BlockSpec((tn, kdim), lambda o, i: (tile(o, i)[1], 0))
    else:
        b_spec = pl.BlockSpec((kdim, tn), lambda o, i: (0, tile(o, i)[1]))
    out = pl.pallas_call(
        body, name=name, grid=(m_steps, n_steps) if a_resident else (n_steps, m_steps),
        in_specs=[pl.BlockSpec((tm, kdim), lambda o, i: (tile(o, i)[0], 0)), b_spec] + [ANY] * n_in,
        out_specs=[pl.BlockSpec((tm, tn), tile)] + [ANY] * n_out,
        out_shape=[jax.ShapeDtypeStruct((m, n), out_dtype)] + (rider.out_shape if rider else []),
        scratch_shapes=rider.scratch if rider else [],
        compiler_params=_params(("arbitrary", "arbitrary") if rider else ("parallel", "parallel")),
    )(a, b, *rider_inputs)
    return out if rider else out[0]


HALO = 16


def _conv_specs(tm):
    def col(j):
        return pl.BlockSpec((tm, D_CONV), lambda i: (i, j))

    def prev(j):
        return pl.BlockSpec((HALO, D_CONV), lambda i: (jnp.maximum(i * (tm // HALO) - 1, 0), j))

    return [col(0), col(1), col(2), col(3), prev(0), prev(2)]


def _conv_y(xc_ref, bc_ref, cc_ref, zc_ref, xp_ref, cp_ref, w_ref, first):
    uc = cc_ref[...].astype(F32) * xc_ref[...].astype(F32)
    up = jnp.where(first, 0.0, cp_ref[...].astype(F32) * xp_ref[...].astype(F32))
    full = jnp.concatenate([up, uc], axis=0)
    u1 = pltpu.roll(full, 1, 0)[HALO:]
    u2 = pltpu.roll(full, 2, 0)[HALO:]
    w = w_ref[...]
    conv = w[0:1] * u2 + w[1:2] * u1 + w[2:3] * uc
    z = zc_ref[...].astype(F32)
    return bc_ref[...].astype(F32) * conv * (z * _sigmoid(z))


def _conv_bwd(u, dyc, conv_w):
    s = u.shape[0]
    tm = min(TM_MM, s)
    cb = D_CONV
    nt = s // tm

    def body(xc_ref, bc_ref, cc_ref, zc_ref, xp_ref, cp_ref, bn_ref, zn_ref, dy_ref, dyn_ref, w_ref, du_ref, dw_ref):
        i = pl.program_id(0)
        xc, cc = xc_ref[...].astype(F32), cc_ref[...].astype(F32)
        bc, z = bc_ref[...].astype(F32), zc_ref[...].astype(F32)
        uc = cc * xc
        up = jnp.where(i == 0, 0.0, cp_ref[...].astype(F32) * xp_ref[...].astype(F32))
        full = jnp.concatenate([up, uc], axis=0)
        u1 = pltpu.roll(full, 1, 0)[HALO:]
        u2 = pltpu.roll(full, 2, 0)[HALO:]
        w = w_ref[...]
        conv = w[0:1] * u2 + w[1:2] * u1 + w[2:3] * uc
        sg = _sigmoid(z)
        sz = z * sg
        dy = dy_ref[...].astype(F32)
        dconv = dy * bc * sz
        zn = zn_ref[...].astype(F32)
        dnext = dyn_ref[...].astype(F32) * bn_ref[...].astype(F32) * (zn * _sigmoid(zn))
        dnext = jnp.where(i == nt - 1, 0.0, dnext)
        fullb = jnp.concatenate([dconv, dnext], axis=0)
        nb = tm + HALO
        d1 = pltpu.roll(fullb, nb - 1, 0)[:tm]
        d2 = pltpu.roll(fullb, nb - 2, 0)[:tm]
        duc = w[2:3] * dconv + w[1:2] * d1 + w[0:1] * d2
        dzc = dy * bc * conv * (sg * (1.0 + z * (1.0 - sg)))
        du_ref[...] = jnp.concatenate([duc * cc, dy * conv * sz, duc * xc, dzc], axis=1).astype(BF16)
        dw = jnp.concatenate([jnp.sum(dconv * u2, axis=0, keepdims=True), jnp.sum(dconv * u1, axis=0, keepdims=True),
                              jnp.sum(dconv * uc, axis=0, keepdims=True), jnp.zeros((5, cb), F32)], axis=0)

        @pl.when(i == 0)
        def _():
            dw_ref[...] = dw

        @pl.when(i > 0)
        def _():
            dw_ref[...] += dw

    def col(j):
        return pl.BlockSpec((tm, cb), lambda i: (i, j))

    def prev(j):
        return pl.BlockSpec((HALO, cb), lambda i: (jnp.maximum(i * (tm // HALO) - 1, 0), j))

    def nxt(j):
        return pl.BlockSpec((HALO, cb), lambda i: (jnp.minimum((i + 1) * (tm // HALO), s // HALO - 1), j))

    return pl.pallas_call(
        body, name="conv_bwd", grid=(nt,),
        in_specs=[col(0), col(1), col(2), col(3), prev(0), prev(2), nxt(1), nxt(3), col(0), nxt(0),
                  pl.BlockSpec((3, cb), lambda i: (0, 0))],
        out_specs=[pl.BlockSpec((tm, 4 * cb), lambda i: (i, 0)), pl.BlockSpec((8, cb), lambda i: (0, 0))],
        out_shape=[jax.ShapeDtypeStruct((s, U_COLS), BF16), jax.ShapeDtypeStruct((8, cb), F32)],
        compiler_params=_params(("arbitrary",)),
    )(u, u, u, u, u, u, u, u, dyc, dyc, conv_w)


def _qkv_specs(tm):
    return [pl.BlockSpec((tm, Q_LORA), lambda i: (i, U_CQ // Q_LORA)),
            pl.BlockSpec((tm, KV_LORA), lambda i: (i, U_CKV // KV_LORA)),
            pl.BlockSpec((tm, LANES), lambda i: (i, U_KR // LANES)),
            pl.BlockSpec((tm, LANES), lambda i: (i, 0)), pl.BlockSpec((tm, LANES), lambda i: (i, 0))]


def _full(shape):
    return pl.BlockSpec(shape, lambda i: (0,) * len(shape))


def _k_rope_lanes(blk):
    lane = lax.broadcasted_iota(jnp.int32, blk.shape, 1)
    return jnp.where(lane < QK_ROPE, blk, 0.0)


def _qkv_fwd(u, cos, sin, wq, wkv, qag, kvag, qg, kg):
    s = u.shape[0]
    tm = min(TM_MM, s)

    def body(cq_ref, ckv_ref, kr_ref, cos_ref, sin_ref, wq_ref, wkv_ref, qag_ref, kvag_ref, qg_ref, kg_ref,
             q_ref, k_ref, v_ref):
        cq = cq_ref[...].astype(F32)
        cqn = (cq * lax.rsqrt(jnp.mean(cq * cq, axis=-1, keepdims=True) + EPS) * qag_ref[...]).astype(BF16)
        ckv = ckv_ref[...].astype(F32)
        ckvn = (ckv * lax.rsqrt(jnp.mean(ckv * ckv, axis=-1, keepdims=True) + EPS) * kvag_ref[...]).astype(BF16)
        kr = _k_rope_lanes(kr_ref[...].astype(F32))
        cosv, sinv, qgv, kgv = cos_ref[...], sin_ref[...], qg_ref[...], kg_ref[...]
        ss_r = jnp.sum(kr * kr, axis=-1, keepdims=True)
        krr = _rope(kr * kgv[:, QK_NOPE:], cosv, sinv)
        qf = _dot(cqn, wq_ref[...])
        kvf = _dot(ckvn, wkv_ref[...])
        heads = range(N_HEADS)
        qh = [qf[:, QK_PAD * h:QK_PAD * (h + 1)] for h in heads]
        kn = [kvf[:, 2 * V_HEAD * h:2 * V_HEAD * h + QK_NOPE] for h in heads]
        rq = [lax.rsqrt(jnp.sum(qh[h] * qh[h], axis=-1, keepdims=True) * (1.0 / QK_HEAD) + EPS) for h in heads]
        rk = [lax.rsqrt((jnp.sum(kn[h] * kn[h], axis=-1, keepdims=True) + ss_r) * (1.0 / QK_HEAD) + EPS) for h in heads]
        for h in heads:
            qn = qh[h] * rq[h] * qgv
            qo = jnp.concatenate([qn[:, :QK_NOPE], _rope(qn[:, QK_NOPE:], cosv, sinv)], axis=1) * (SCALE * LOG2E)
            q_ref[h] = qo.astype(BF16)
            vh = kvf[:, 2 * V_HEAD * h + QK_NOPE:2 * V_HEAD * (h + 1)]
            k_ref[h] = jnp.concatenate([kn[h] * kgv[:, :QK_NOPE] * rk[h], krr * rk[h]], axis=1).astype(BF16)
            v_ref[h] = jnp.concatenate([vh, jnp.ones_like(vh)], axis=1).astype(BF16)

    return pl.pallas_call(
        body, name="qkv_fwd", grid=(s // tm,),
        in_specs=_qkv_specs(tm) + [_full((Q_LORA, N_HEADS * QK_PAD)), _full((KV_LORA, 2 * D_ATTN)),
                                   _full((1, Q_LORA)), _full((1, KV_LORA)), _full((1, QK_PAD)), _full((1, QK_PAD))],
        out_specs=[pl.BlockSpec((N_HEADS, tm, QK_PAD), lambda i: (0, i, 0)),
                   pl.BlockSpec((N_HEADS, tm, QK_PAD), lambda i: (0, i, 0)),
                   pl.BlockSpec((N_HEADS, tm, 2 * V_HEAD), lambda i: (0, i, 0))],
        out_shape=[jax.ShapeDtypeStruct((N_HEADS, s, QK_PAD), BF16), jax.ShapeDtypeStruct((N_HEADS, s, QK_PAD), BF16),
                   jax.ShapeDtypeStruct((N_HEADS, s, 2 * V_HEAD), BF16)],
        compiler_params=_params(("parallel",)),
    )(u, u, u, cos, sin, wq, wkv, qag, kvag, qg, kg)


def _qkv_bwd(u, cos, sin, dq, dk, dv, dza, wq, wkv, qag, kvag, qg, kg, du):
    s = u.shape[0]
    tm = min(TM_ELEM, s)
    nt = s // tm

    def body(cq_ref, ckv_ref, kr_ref, cos_ref, sin_ref, dq_ref, dk_ref, dv_ref, dza_ref, wq_ref, wkv_ref, qag_ref,
             kvag_ref, qg_ref, kg_ref, du_in, du_ref, dwq_ref, dwkv_ref, dqag_ref, dkvag_ref, dqg_ref, dkg_ref,
             dwq_acc, dwkv_acc):
        del du_in
        i = pl.program_id(0)

        @pl.when(i == 0)
        def _():
            dwq_acc[...] = jnp.zeros_like(dwq_acc)
            dwkv_acc[...] = jnp.zeros_like(dwkv_acc)

        cq = cq_ref[...].astype(F32)
        rqa = lax.rsqrt(jnp.mean(cq * cq, axis=-1, keepdims=True) + EPS)
        xq = cq * rqa
        qagv = qag_ref[...]
        cqn = (xq * qagv).astype(BF16)
        ckv = ckv_ref[...].astype(F32)
        rkva = lax.rsqrt(jnp.mean(ckv * ckv, axis=-1, keepdims=True) + EPS)
        xkv = ckv * rkva
        kvagv = kvag_ref[...]
        ckvn = (xkv * kvagv).astype(BF16)
        kr = _k_rope_lanes(kr_ref[...].astype(F32))
        cosv, sinv, qgv, kgv = cos_ref[...], sin_ref[...], qg_ref[...], kg_ref[...]
        ss_r = jnp.sum(kr * kr, axis=-1, keepdims=True)
        dqg = jnp.zeros((1, QK_PAD), F32)
        dkg = jnp.zeros((1, QK_PAD), F32)
        dkr = jnp.zeros((tm, LANES), F32)
        qf = _dot(cqn, wq_ref[...])
        kvf = _dot(ckvn, wkv_ref[...])
        heads = range(N_HEADS)
        qh = [qf[:, QK_PAD * h:QK_PAD * (h + 1)] for h in heads]
        kn = [kvf[:, 2 * V_HEAD * h:2 * V_HEAD * h + QK_NOPE] for h in heads]
        rq = [lax.rsqrt(jnp.sum(qh[h] * qh[h], axis=-1, keepdims=True) * (1.0 / QK_HEAD) + EPS) for h in heads]
        rk = [lax.rsqrt((jnp.sum(kn[h] * kn[h], axis=-1, keepdims=True) + ss_r) * (1.0 / QK_HEAD) + EPS) for h in heads]
        xh = [qh[h] * rq[h] for h in heads]
        xk = [jnp.concatenate([kn[h], kr], axis=1) * rk[h] for h in heads]
        dyq, dyk = [], []
        for h in heads:
            g = dq_ref[h].astype(F32) * SCALE
            dyq.append(jnp.concatenate([g[:, :QK_NOPE], _rope_t(g[:, QK_NOPE:], cosv, sinv)], axis=1))
            gk = dk_ref[h].astype(F32)
            dyk.append(jnp.concatenate([gk[:, :QK_NOPE], _rope_t(gk[:, QK_NOPE:], cosv, sinv)], axis=1))
        for h in heads:
            dqg = dqg + jnp.sum(dyq[h] * xh[h], axis=0, keepdims=True)
            dkg = dkg + jnp.sum(dyk[h] * xk[h], axis=0, keepdims=True)
        gdy = [dyq[h] * qgv for h in heads]
        gdyk = [dyk[h] * kgv for h in heads]
        tq_ = [jnp.sum(gdy[h] * xh[h], axis=-1, keepdims=True) * (1.0 / QK_HEAD) for h in heads]
        tk_ = [jnp.sum(gdyk[h] * xk[h], axis=-1, keepdims=True) * (1.0 / QK_HEAD) for h in heads]
        dqf = [(rq[h] * (gdy[h] - xh[h] * tq_[h])).astype(BF16) for h in heads]
        dkvf = []
        for h in heads:
            dxk = rk[h] * (gdyk[h] - xk[h] * tk_[h])
            dkr = dkr + dxk[:, QK_NOPE:]
            dkvf += [dxk[:, :QK_NOPE].astype(BF16), dv_ref[h]]
        dqf_b, dkvf_b = jnp.concatenate(dqf, axis=1), jnp.concatenate(dkvf, axis=1)
        dwq_acc[...] += _dot_tn(cqn, dqf_b)
        dwkv_acc[...] += _dot_tn(ckvn, dkvf_b)
        dcqn = _dot_nt(dqf_b, wq_ref[...])
        dckvn = _dot_nt(dkvf_b, wkv_ref[...])
        dqag = jnp.sum(dcqn * xq, axis=0, keepdims=True)
        dkvag = jnp.sum(dckvn * xkv, axis=0, keepdims=True)
        gq = dcqn * qagv
        dcq = rqa * (gq - xq * jnp.mean(gq * xq, axis=-1, keepdims=True))
        gkv = dckvn * kvagv
        dckv = rkva * (gkv - xkv * jnp.mean(gkv * xkv, axis=-1, keepdims=True))
        win = pltpu.roll(jnp.concatenate([dza_ref[...].astype(F32), jnp.zeros((tm, LANES), F32)], axis=1), QK_ROPE, 1)
        win = win + jnp.concatenate([dkr, jnp.zeros((tm, D_ATTN), F32)], axis=1)
        du_ref[...] = jnp.concatenate([dcq, dckv, win, jnp.zeros((tm, U_TAIL - ZA_LO - ZA_WIN), F32)], axis=1).astype(BF16)

        @pl.when(i == 0)
        def _():
            dqag_ref[...] = dqag
            dkvag_ref[...] = dkvag
            dqg_ref[...] = dqg
            dkg_ref[...] = dkg

        @pl.when(i > 0)
        def _():
            dqag_ref[...] += dqag
            dkvag_ref[...] += dkvag
            dqg_ref[...] += dqg
            dkg_ref[...] += dkg

        @pl.when(i == nt - 1)
        def _():
            dwq_ref[...] = dwq_acc[...].astype(BF16)
            dwkv_ref[...] = dwkv_acc[...].astype(BF16)

    head = lambda w: pl.BlockSpec((N_HEADS, tm, w), lambda i: (0, i, 0))
    wq_shape, wkv_shape = (Q_LORA, N_HEADS * QK_PAD), (KV_LORA, 2 * D_ATTN)
    return pl.pallas_call(
        body, name="qkv_bwd", grid=(nt,),
        in_specs=_qkv_specs(tm) + [head(QK_PAD), head(QK_PAD), head(V_HEAD), pl.BlockSpec((tm, D_ATTN), lambda i: (i, 0)),
                                   _full(wq_shape), _full(wkv_shape), _full((1, Q_LORA)), _full((1, KV_LORA)),
                                   _full((1, QK_PAD)), _full((1, QK_PAD)), ANY],
        out_specs=[pl.BlockSpec((tm, U_TAIL), lambda i: (i, U_COLS // U_TAIL - 1)), _full(wq_shape), _full(wkv_shape),
                   _full((1, Q_LORA)), _full((1, KV_LORA)), _full((1, QK_PAD)), _full((1, QK_PAD))],
        out_shape=[jax.ShapeDtypeStruct(du.shape, du.dtype), jax.ShapeDtypeStruct(wq_shape, BF16),
                   jax.ShapeDtypeStruct(wkv_shape, BF16), jax.ShapeDtypeStruct((1, Q_LORA), F32),
                   jax.ShapeDtypeStruct((1, KV_LORA), F32), jax.ShapeDtypeStruct((1, QK_PAD), F32),
                   jax.ShapeDtypeStruct((1, QK_PAD), F32)],
        scratch_shapes=[pltpu.VMEM(wq_shape, F32), pltpu.VMEM(wkv_shape, F32)],
        input_output_aliases={15: 0}, compiler_params=_params(("arbitrary",)),
    )(u, u, u, cos, sin, dq, dk, dv, dza, wq, wkv, qag, kvag, qg, kg, du)


def _flash_fwd(q, k, v):
    nh, s, _ = q.shape
    tq = min(TQ, s)
    nkv = KV_SPLIT
    tk = tq // nkv
    nq = s // tq
    nch = Q_CHAINS
    tc = tq // nch

    def body(q_ref, k_ref, v_ref, o_ref, lse_ref):
        i = pl.program_id(1)
        chains = [q_ref[0, r * tc:(r + 1) * tc, :] for r in range(nch)]

        def unit(r, j, carry, shift=None):
            m, acc = carry
            rows = pl.ds(pl.multiple_of(j * tk, tk), tk)
            sc = _dot_nt(chains[r], k_ref[0, rows, :])
            if shift is not None:
                qi = lax.broadcasted_iota(jnp.int32, sc.shape, 0)
                ki = lax.broadcasted_iota(jnp.int32, sc.shape, 1) + shift
                sc = jnp.where(ki <= qi, sc, NEG)
            m_new = jnp.maximum(m, jnp.max(sc, axis=-1, keepdims=True))
            p = jnp.exp2(sc - m_new).astype(BF16)
            return m_new, jnp.exp2(m - m_new) * acc + _dot(p, v_ref[0, rows, :])

        def trip(p, carry):
            for b in range(nkv):
                carry = tuple(unit(r, nkv * p + b, cr) for r, cr in enumerate(carry))
            return carry

        init = (jnp.full((tc, 1), NEG, F32), jnp.zeros((tc, 2 * V_HEAD), F32))
        carry = list(lax.fori_loop(0, i, trip, (init,) * nch))
        for b in range(nkv):
            for r in range(nch):
                shift = b * tk - r * tc
                if shift < tc:
                    carry[r] = unit(r, nkv * i + b, carry[r], None if shift + tk - 1 <= 0 else shift)
        for r, (m, acc) in enumerate(carry):
            l = acc[:, V_HEAD:]
            o_ref[r * tc:(r + 1) * tc, :] = (acc[:, :V_HEAD] / l).astype(BF16)
            lse = m + jnp.log(l[:, 0:1]) * LOG2E
            lse_ref[0, :, r * tc:(r + 1) * tc] = jnp.broadcast_to(lse, (tc, LANES)).T[0:1, :]

    return pl.pallas_call(
        body, name="flash_fwd", grid=(nh, nq),
        in_specs=[pl.BlockSpec((1, tq, QK_PAD), lambda h, i: (h, i, 0)),
                  pl.BlockSpec((1, s, QK_PAD), lambda h, i: (h, 0, 0)),
                  pl.BlockSpec((1, s, 2 * V_HEAD), lambda h, i: (h, 0, 0))],
        out_specs=[pl.BlockSpec((tq, V_HEAD), lambda h, i: (i, h)), pl.BlockSpec((1, 1, tq), lambda h, i: (h, 0, i))],
        out_shape=[jax.ShapeDtypeStruct((s, nh * V_HEAD), BF16), jax.ShapeDtypeStruct((nh, 1, s), F32)],
        compiler_params=_params(("parallel", "arbitrary")),
    )(q, k, v)


def _flash_bwd(q, k, v, do, lse, delta):
    nh, s, _ = q.shape
    tq = min(TQ, s)
    nq = s // tq
    kps = 2 if nq % 2 == 0 else 1
    ng = nq // kps

    def body(q_ref, k_ref, v_ref, do_ref, lse_ref, dl_ref, dq_ref, dk_ref, dv_ref, dq_acc):
        g = ng - 1 - pl.program_id(1)

        @pl.when(g == ng - 1)
        def _():
            dq_acc[...] = jnp.zeros_like(dq_acc)

        for sub in reversed(range(kps)):
            kv_block(q_ref, k_ref, v_ref, do_ref, lse_ref, dl_ref, dk_ref, dv_ref, dq_acc, g * kps + sub, sub)

        @pl.when(g == 0)
        def _():
            dq_ref[0] = dq_acc[...].astype(BF16)

    def kv_block(q_ref, k_ref, v_ref, do_ref, lse_ref, dl_ref, dk_ref, dv_ref, dq_acc, j, sub):
        own = slice(sub * tq, (sub + 1) * tq)
        kj, vj = k_ref[0, own, :], v_ref[0, own, :]

        def block(kk, vv, qq, dd, lse, dl, masked):
            st = _dot_nt(kk, qq)
            pt = jnp.exp2(st - lse)
            if masked:
                ki = lax.broadcasted_iota(jnp.int32, st.shape, 0)
                qx = lax.broadcasted_iota(jnp.int32, st.shape, 1)
                pt = jnp.where(ki <= qx, pt, 0.0)
            ddv = _dot(pt.astype(BF16), dd)
            dst = (pt * (_dot_nt(vv, dd) - dl)).astype(BF16)
            ddq = _dot_tn(dst, kk)
            return _dot(dst, qq), ddv, ddq

        def step(i, carry):
            dk, dv = carry
            rows = pl.ds(pl.multiple_of(i * tq, tq), tq)
            ddk, ddv, ddq = block(kj, vj, q_ref[0, rows, :], do_ref[rows, :], lse_ref[0, pl.ds(i, 1), :],
                                  dl_ref[0, pl.ds(i, 1), :], False)
            dq_acc[rows, :] += ddq
            return dk + ddk, dv + ddv

        th = tq // 2
        lse_j, dl_j = lse_ref[0, pl.ds(j, 1), :], dl_ref[0, pl.ds(j, 1), :]
        parts = []
        for kh, qh, masked in ((0, 0, True), (0, 1, False), (1, 1, True)):
            rows = pl.ds(pl.multiple_of(j * tq + qh * th, th), th)
            ks, qs = slice(kh * th, (kh + 1) * th), slice(qh * th, (qh + 1) * th)
            ddk, ddv, ddq = block(kj[ks], vj[ks], q_ref[0, rows, :], do_ref[rows, :], lse_j[:, qs], dl_j[:, qs], masked)
            dq_acc[rows, :] += ddq
            parts.append((ddk, ddv))
        carry = (jnp.concatenate([parts[0][0] + parts[1][0], parts[2][0]], axis=0),
                 jnp.concatenate([parts[0][1] + parts[1][1], parts[2][1]], axis=0))
        dk, dv = lax.fori_loop(j + 1, nq, step, carry)
        dk_ref[0, own, :] = (dk * LN2).astype(BF16)
        dv_ref[0, own, :] = dv.astype(BF16)

    return pl.pallas_call(
        body, name="flash_bwd", grid=(nh, ng),
        in_specs=[pl.BlockSpec((1, s, QK_PAD), lambda h, j: (h, 0, 0)),
                  pl.BlockSpec((1, kps * tq, QK_PAD), lambda h, g: (h, ng - 1 - g, 0)),
                  pl.BlockSpec((1, kps * tq, V_HEAD), lambda h, g: (h, ng - 1 - g, 0)),
                  pl.BlockSpec((s, V_HEAD), lambda h, j: (0, h)),
                  pl.BlockSpec((1, nq, tq), lambda h, j: (h, 0, 0)),
                  pl.BlockSpec((1, nq, tq), lambda h, j: (h, 0, 0))],
        out_specs=[pl.BlockSpec((1, s, QK_PAD), lambda h, j: (h, 0, 0)),
                   pl.BlockSpec((1, kps * tq, QK_PAD), lambda h, g: (h, ng - 1 - g, 0)),
                   pl.BlockSpec((1, kps * tq, V_HEAD), lambda h, g: (h, ng - 1 - g, 0))],
        out_shape=[jax.ShapeDtypeStruct((nh, s, QK_PAD), BF16), jax.ShapeDtypeStruct((nh, s, QK_PAD), BF16),
                   jax.ShapeDtypeStruct((nh, s, V_HEAD), BF16)],
        scratch_shapes=[pltpu.VMEM((s, QK_PAD), F32)],
        compiler_params=_params(("parallel", "arbitrary")),
    )(q, k, v, do, lse, delta)


def _tail(x, target, o, u, mod, w_out, conv_w):
    s, d = x.shape
    tm = min(TM_ELEM, s)

    def body(x_ref, t_ref, o_ref, za_ref, mod_ref, w_ref, xc_ref, bc_ref, cc_ref, zc_ref, xp_ref, cp_ref, cw_ref,
             gx_ref, dy_ref, ycat_ref, dyc_ref, do_ref, du_ref, delta_ref, dgate_ref, loss_ref):
        i = pl.program_id(0)
        za = pltpu.roll(za_ref[:, ZA_LO:ZA_LO + ZA_WIN].astype(F32), ZA_WIN - QK_ROPE, 1)[:, :D_ATTN]
        ov = o_ref[...].astype(F32)
        sg = _sigmoid(za)
        sl = za * sg
        ya = ov * sl
        y = _dot(ya.astype(BF16), w_ref[D_CONV:, :])
        yc = _conv_y(xc_ref, bc_ref, cc_ref, zc_ref, xp_ref, cp_ref, cw_ref, i == 0)
        y = y + _dot(yc.astype(BF16), w_ref[:D_CONV, :])
        ycat_ref[...] = jnp.concatenate([yc.T, ya.T], axis=0).astype(BF16)
        gate = mod_ref[:, 2 * d:3 * d]
        e = x_ref[...] + gate * y - t_ref[...]
        dout = e * (1.0 / d)
        gx_ref[...] = dout
        dy = (dout * gate).astype(BF16)
        dy_ref[...] = dy
        dycat = _dot_nt(dy, w_ref[...])
        dyc_ref[...] = dycat[:, :D_CONV].astype(BF16)
        dya = dycat[:, D_CONV:]
        dov = dya * sl
        do_ref[...] = dov.astype(BF16)
        du_ref[...] = (dya * ov * (sg * (1.0 + za * (1.0 - sg)))).astype(BF16)
        prod_t = (dov * ov).T
        for h in range(N_HEADS):
            delta_ref[h] = jnp.sum(prod_t[V_HEAD * h:V_HEAD * (h + 1), :], axis=0, keepdims=True)
        dgate = jnp.sum(dout * y, axis=0, keepdims=True)
        part = jnp.sum(jnp.sum(e * e, axis=0, keepdims=True), axis=1, keepdims=True) * (0.5 / d)
        part = jnp.broadcast_to(part, (1, LANES))

        @pl.when(i == 0)
        def _():
            dgate_ref[...] = dgate
            loss_ref[...] = part

        @pl.when(i > 0)
        def _():
            dgate_ref[...] += dgate
            loss_ref[...] += part

    tok = lambda w: pl.BlockSpec((tm, w), lambda i: (i, 0))
    return pl.pallas_call(
        body, name="tail", grid=(s // tm,),
        in_specs=[tok(d), tok(d), tok(D_ATTN), pl.BlockSpec((tm, U_TAIL), lambda i: (i, U_COLS // U_TAIL - 1)),
                  _full((1, 3 * d)), _full((d, d))] + _conv_specs(tm) + [_full((3, D_CONV))],
        out_specs=[tok(d), tok(d), pl.BlockSpec((d, tm), lambda i: (0, i)), tok(D_CONV), tok(D_ATTN), tok(D_ATTN),
                   pl.BlockSpec((N_HEADS, 1, tm), lambda i: (0, 0, i)), _full((1, d)), _full((1, LANES))],
        out_shape=[jax.ShapeDtypeStruct((s, d), F32), jax.ShapeDtypeStruct((s, d), BF16),
                   jax.ShapeDtypeStruct((d, s), BF16), jax.ShapeDtypeStruct((s, D_CONV), BF16),
                   jax.ShapeDtypeStruct((s, D_ATTN), BF16), jax.ShapeDtypeStruct((s, D_ATTN), BF16),
                   jax.ShapeDtypeStruct((N_HEADS, 1, s), F32), jax.ShapeDtypeStruct((1, d), F32),
                   jax.ShapeDtypeStruct((1, LANES), F32)],
        compiler_params=_params(("arbitrary",)),
    )(x, target, o, u, mod, w_out, u, u, u, u, u, u, conv_w)


def _norm_bwd(x, dh, gx1, norm_g, mod):
    s, d = x.shape
    tm = min(TM_MM, s)

    def body(x_ref, dh_ref, gx_ref, g_ref, mod_ref, o_ref, dshift_ref, dscale_ref, dg_ref):
        i = pl.program_id(0)
        gv, sc1 = g_ref[...], 1.0 + mod_ref[:, d:2 * d]
        gsc = gv * sc1
        half = NORM_ROWS // 2

        def group(c, acc):
            a_dh, a_dhxn = acc
            ks = range(NORM_GROUP)
            rows = [pl.ds(pl.multiple_of((c * NORM_GROUP + k) * NORM_ROWS, NORM_ROWS), NORM_ROWS) for k in ks]
            xv = [x_ref[rows[k], :] for k in ks]
            dhv = [dh_ref[rows[k], :].astype(F32) for k in ks]
            r = [lax.rsqrt(jnp.mean(xv[k] * xv[k], axis=-1, keepdims=True) + EPS) for k in ks]
            xn = [xv[k] * r[k] for k in ks]
            dxn = [dhv[k] * gsc for k in ks]
            t = [jnp.mean(dxn[k] * xn[k], axis=-1, keepdims=True) for k in ks]
            for k in ks:
                o_ref[rows[k], :] = gx_ref[rows[k], :] + r[k] * (dxn[k] - xn[k] * t[k])
                dhxn = dhv[k] * xn[k]
                a_dh = a_dh + dhv[k][:half] + dhv[k][half:]
                a_dhxn = a_dhxn + dhxn[:half] + dhxn[half:]
            return a_dh, a_dhxn

        zero = jnp.zeros((half, d), F32)
        a_dh, a_dhxn = lax.fori_loop(0, tm // (NORM_ROWS * NORM_GROUP), group, (zero, zero))
        dshift = jnp.sum(a_dh, axis=0, keepdims=True)
        s_dhxn = jnp.sum(a_dhxn, axis=0, keepdims=True)
        dscale, dg = s_dhxn * gv, s_dhxn * sc1

        @pl.when(i == 0)
        def _():
            dshift_ref[...] = dshift
            dscale_ref[...] = dscale
            dg_ref[...] = dg

        @pl.when(i > 0)
        def _():
            dshift_ref[...] += dshift
            dscale_ref[...] += dscale
            dg_ref[...] += dg

    tok = pl.BlockSpec((tm, d), lambda i: (i, 0))
    row = jax.ShapeDtypeStruct((1, d), F32)
    return pl.pallas_call(
        body, name="norm_bwd", grid=(s // tm,),
        in_specs=[tok, tok, tok, _full((1, d)), _full((1, 3 * d))],
        out_specs=[tok, _full((1, d)), _full((1, d)), _full((1, d))],
        out_shape=[jax.ShapeDtypeStruct((s, d), F32), row, row, row],
        compiler_params=_params(("arbitrary",)),
    )(x, dh, gx1, norm_g, mod)


def _adamw(w, g, m, v, name):
    rows, cols = w.shape
    tr = 256 if rows % 256 == 0 else rows
    tc = 512 if (rows > 256 and tr == rows and cols % 512 == 0) else cols

    def body(w_ref, g_ref, m_ref, v_ref, d_ref, nm_ref, nv_ref):
        gv = g_ref[...]
        nm = ADAM_B1 * m_ref[...] + (1.0 - ADAM_B1) * gv
        nv = ADAM_B2 * v_ref[...] + (1.0 - ADAM_B2) * (gv * gv)
        m_hat = nm / (1.0 - ADAM_B1 ** ADAM_STEP)
        v_hat = nv / (1.0 - ADAM_B2 ** ADAM_STEP)
        d_ref[...] = -ADAM_LR * (m_hat / (jnp.sqrt(v_hat) + ADAM_EPS) + ADAM_WD * w_ref[...])
        nm_ref[...] = nm
        nv_ref[...] = nv

    spec = pl.BlockSpec((tr, tc), lambda i, j: (i, j))
    shape = jax.ShapeDtypeStruct((rows, cols), F32)
    return pl.pallas_call(
        body, name=name, grid=(rows // tr, cols // tc), in_specs=[spec] * 4, out_specs=[spec] * 3, out_shape=[shape] * 3,
        compiler_params=_params(("parallel", "parallel")),
    )(w, g, m, v)


def _pad_cols(a, n):
    return jnp.pad(a, ((0, 0), (0, n - a.shape[1])))


def kernel(x, c, positions, ada_w, ada_b, norm_g, w_in, conv_w, q_a_g, w_q_b, kv_a_g, w_kv_b, q_g, k_g, w_out, loss_target, m_ada_w, m_ada_b, m_norm_g, m_w_in, m_conv_w, m_q_a_g, m_w_q_b, m_kv_a_g, m_w_kv_b, m_q_g, m_k_g, m_w_out, v_ada_w, v_ada_b, v_norm_g, v_w_in, v_conv_w, v_q_a_g, v_w_q_b, v_kv_a_g, v_w_kv_b, v_q_g, v_k_g, v_w_out):
    me = _my_index()
    s = x.shape[1]
    nq = s // min(TQ, s)
    x2, tgt = x[0], loss_target[0]
    w_in_l, w_q_l, w_kv_l, w_out_l, conv_l, ada_w_l = w_in[0], w_q_b[0], w_kv_b[0], w_out[0], conv_w[0], ada_w[0]
    ada_cols = ada_w_l.shape[1]

    small = jnp.concatenate([c.reshape(-1, LANES), conv_l.reshape(-1, LANES), jnp.zeros((5, LANES), F32)], axis=0)
    (small_g,) = _all_gather([small], "gather_c")
    c_all = small_g[:, :D_MODEL // LANES].reshape(N_DEV, D_MODEL)
    conv_g = small_g[:, D_MODEL // LANES:D_MODEL // LANES + 3].transpose(1, 0, 2).reshape(3, D_CONV)

    ada_b_l = lax.dynamic_slice(ada_b, (0, me * ada_cols), (1, ada_cols))
    mod_cols = _ada_mod(jnp.pad(c_all, ((0, 8), (0, 0))), ada_w_l, ada_b_l)[:N_DEV]
    (mod_g,) = _all_gather([mod_cols], "gather_mod")
    mod = lax.dynamic_index_in_dim(mod_g, me, axis=1, keepdims=False).reshape(1, 3 * D_MODEL)

    half = jnp.arange(0, QK_ROPE, 2, dtype=F32) / QK_ROPE
    inv_freq = ROPE_BASE ** (-half)
    zeros64 = jnp.zeros((LANES - QK_ROPE,), F32)
    invf = jnp.concatenate([inv_freq, inv_freq, zeros64]).reshape(1, LANES)
    sign = jnp.concatenate([-jnp.ones((32,), F32), jnp.ones((32,), F32), zeros64]).reshape(1, LANES)
    qg_p, kg_p = _pad_cols(q_g, QK_PAD), _pad_cols(k_g, QK_PAD)

    my_off = ((CW * me) % LANES).astype(jnp.int32)
    win = [_expand_w_in(w_in_l.T, my_off.reshape(1))]
    h, h_t, cos, sin, win_g = _norm_mod(x2, norm_g, mod, positions.reshape(s, 1), invf, sign, _Gather(win, relay=True, parts=4), win)
    w_in_p = _merge_w_in(win_g)
    rest = [_pad_wq(w_q_l.T), w_kv_l.astype(BF16), w_out_l.astype(BF16)]
    u, wq_g, wkv_g, w_out_g = _matmul(h, w_in_p, nt=False, out_dtype=BF16, tm=2 * TM_MM, tn=2048, name="in_proj",
                                      rider=_Gather(rest), rider_inputs=rest)
    w_out_g = w_out_g.reshape(D_MODEL, D_MODEL)
    wq_g = wq_g.transpose(1, 0, 2).reshape(Q_LORA, N_HEADS * QK_PAD)
    wkv_g = wkv_g.transpose(1, 0, 2).reshape(KV_LORA, 2 * D_ATTN)
    q, k, v = _qkv_fwd(u, cos, sin, wq_g, wkv_g, q_a_g, kv_a_g, qg_p, kg_p)
    o, lse = _flash_fwd(q, k, v)
    gx1, dy, ycat_t, dyc, do, dza, delta, dgate, loss_row = _tail(x2, tgt, o, u, mod, w_out_g, conv_g)

    dq, dk, dv = _flash_bwd(q, k, v, do, lse.reshape(N_HEADS, nq, s // nq), delta.reshape(N_HEADS, nq, s // nq))
    du, dconv = _conv_bwd(u, dyc, conv_g)
    du, dwq, dwkv, dqag, dkvag, dqg, dkg = _qkv_bwd(u, cos, sin, dq, dk, dv, dza, wq_g, wkv_g, q_a_g, kv_a_g, qg_p, kg_p, du)
    dwq = dwq.reshape(Q_LORA, N_HEADS, QK_PAD).transpose(1, 0, 2)
    dwkv = dwkv.reshape(KV_LORA, N_HEADS, 2 * V_HEAD).transpose(1, 0, 2)
    dw_in = _matmul(h_t, du, nt=False, out_dtype=BF16, tm=TM_MM, tn=768, name="dw_in")
    first = [dw_in, dwq, dwkv]
    dw_out, r_in, r_q, r_kv = _matmul(ycat_t, dy, nt=False, out_dtype=BF16, tm=TM_MM, tn=512, name="dw_out",
                                      rider=_SiblingExchange(first, [True, False, False]), rider_inputs=first)
    dw_out = dw_out.reshape(N_DEV, D_MODEL // N_DEV, D_MODEL)
    (r_out,) = _exchange(_SiblingExchange([dw_out], [False]), [dw_out], "rs_sibling_out")
    core = lax.axis_index("c").astype(jnp.int32)
    lo_tiles = ((CW * (2 * jnp.arange(4, dtype=jnp.int32) + core)) // LANES).astype(jnp.int32)
    pairs = [_add_window(dw_in, r_in, lo_tiles), _add_pairs(dwq, r_q, core.reshape(1), "rs_add_q"),
             _add_pairs(dwkv, r_kv, core.reshape(1), "rs_add_kv"), _add_pairs(dw_out, r_out, core.reshape(1), "rs_add_out")]
    dh, *quads = _matmul(du, w_in_p, nt=True, out_dtype=BF16, tm=2 * TM_MM, tn=512, name="dh",
                         rider=_ChipExchange(pairs), rider_inputs=pairs, a_resident=True)
    my_chip = 2 * lax.axis_index("x") + lax.axis_index("y")
    written = jnp.where(jnp.arange(4) == my_chip, (jnp.arange(4) + 1) % 4, jnp.arange(4))
    sel = jnp.concatenate([my_chip.reshape(1), written, ((EXP_W - my_off) % EXP_W).reshape(1)]).astype(jnp.int32)
    g_w_in_t = _final_sum(pairs[0], quads[0], sel, "rs_sum_in", unshift=True, keep_t=CW)
    g_w_q_t = _final_sum(pairs[1], quads[1], sel, "rs_sum_q", keep_t=QK_HEAD)
    g_w_kv = _final_sum(pairs[2], quads[2], sel, "rs_sum_kv")
    g_w_out = _final_sum(pairs[3], quads[3], sel, "rs_sum_out")
    grad_x, dshift, dscale, dng = _norm_bwd(x2, dh, gx1, norm_g, mod)

    row = jnp.concatenate([dshift, dscale, dgate, dng, dqag, dkvag, dqg, dkg, dconv[:3].reshape(1, 3 * D_CONV), loss_row], axis=1)
    (rows_g,) = _all_gather([row], "gather_small")
    tot = _sum_leading(rows_g, F32, "sum_small")
    dmod_all = rows_g[:, 0, SM_MOD:SM_NG]
    g_ada_b = tot[:, SM_MOD:SM_NG]
    g_norm_g = tot[:, SM_NG:SM_QAG]
    g_q_a_g = tot[:, SM_QAG:SM_KVAG]
    g_kv_a_g = tot[:, SM_KVAG:SM_QG]
    g_q_g = tot[:, SM_QG:SM_QG + QK_HEAD]
    g_k_g = tot[:, SM_KG:SM_KG + QK_HEAD]
    conv_cols = conv_l.shape[1]
    g_conv = lax.dynamic_slice(tot[:, SM_CONV:SM_LOSS].reshape(3, D_CONV), (0, me * conv_cols), (3, conv_cols))
    loss = tot[0, SM_LOSS]
    dmod_my = lax.dynamic_slice(dmod_all, (0, me * ada_cols), (N_DEV, ada_cols))
    g_ada_w = _ada_w_grad(c_all.T, dmod_my)

    grads = dict(ada_w=g_ada_w, ada_b=g_ada_b, norm_g=g_norm_g, w_in=g_w_in_t, conv_w=g_conv, q_a_g=g_q_a_g, w_q_b=g_w_q_t,
                 kv_a_g=g_kv_a_g, w_kv_b=g_w_kv, q_g=g_q_g, k_g=g_k_g, w_out=g_w_out)
    weights = dict(ada_w=(ada_w, m_ada_w, v_ada_w), ada_b=(ada_b, m_ada_b, v_ada_b), norm_g=(norm_g, m_norm_g, v_norm_g),
                   w_in=(w_in, m_w_in, v_w_in), conv_w=(conv_w, m_conv_w, v_conv_w), q_a_g=(q_a_g, m_q_a_g, v_q_a_g),
                   w_q_b=(w_q_b, m_w_q_b, v_w_q_b), kv_a_g=(kv_a_g, m_kv_a_g, v_kv_a_g), w_kv_b=(w_kv_b, m_w_kv_b, v_w_kv_b),
                   q_g=(q_g, m_q_g, v_q_g), k_g=(k_g, m_k_g, v_k_g), w_out=(w_out, m_w_out, v_w_out))
    names = list(grads)
    out_g, out_d, out_m, out_v = [], [], [], []
    for n in names:
        w, m, v_ = weights[n]
        shape2 = w.shape[-2:] if w.ndim == 3 else (1, w.shape[-1])
        transposed = n in ("w_in", "w_q_b")
        to2 = (lambda a: a.reshape(shape2).T) if transposed else (lambda a: a.reshape(shape2))
        back = (lambda a: a.T.reshape(w.shape)) if transposed else (lambda a: a.reshape(w.shape))
        g2 = grads[n] if transposed else grads[n].reshape(shape2)
        d2, m2, v2 = _adamw(to2(w), g2, to2(m), to2(v_), "adamw_" + n)
        out_g.append(back(g2))
        out_d.append(back(d2))
        out_m.append(back(m2))
        out_v.append(back(v2))
    return (loss, grad_x.reshape(x.shape), *out_g, *out_d, *out_m, *out_v)
```

```python
import functools
import math

import jax
import jax.numpy as jnp
from jax import lax
from jax.experimental import pallas as pl
from jax.experimental.pallas import tpu as pltpu

F32 = jnp.float32
BF16 = jnp.bfloat16
MESH = pl.DeviceIdType.MESH

D_MODEL = 2048
D_CONV = 1024
N_HEADS = 8
QK_NOPE = 128
QK_ROPE = 64
QK_HEAD = QK_NOPE + QK_ROPE
V_HEAD = 128
D_ATTN = N_HEADS * V_HEAD
Q_LORA = 512
KV_LORA = 256
ROPE_BASE = 10000.0
IN_COLS = 4 * D_CONV + Q_LORA + KV_LORA + QK_ROPE + D_ATTN
EPS = 1e-6
ADAM_LR, ADAM_B1, ADAM_B2, ADAM_EPS, ADAM_WD, ADAM_STEP = 0.001, 0.9, 0.999, 1e-08, 0.01, 10

N_DEV = 8
LANES = 128
QK_PAD = 256
U_COLS = 6144
U_CQ, U_CKV, U_KR, U_ZA = 4096, 4608, 4864, 4928
U_TAIL = 2048
ZA_LO = U_ZA - (U_COLS - U_TAIL) - QK_ROPE
ZA_WIN = D_ATTN + LANES
CW = IN_COLS // 8
EXP_W = 896
W_LO = [(CW * d // 128) * 128 for d in range(8)]
W_OFF = [CW * d - lo for d, lo in enumerate(W_LO)]
SCALE = 1.0 / math.sqrt(QK_HEAD)
LOG2E = 1.4426950408889634
LN2 = 0.6931471805599453
NEG = -1e30
VMEM_LIMIT = 56 * 1024 * 1024

TM_ELEM = 256
NORM_ROWS = 16
NORM_GROUP = 4
TM_MM = 512
TQ = 1024
Q_CHAINS = 4
KV_SPLIT = 2

SM_MOD, SM_NG, SM_QAG, SM_KVAG, SM_QG, SM_KG, SM_CONV, SM_LOSS = 0, 6144, 8192, 8704, 8960, 9216, 9472, 12544
SM_COLS = 12672


def _params(sem=None):
    kw = dict(vmem_limit_bytes=VMEM_LIMIT)
    if sem is not None:
        kw["dimension_semantics"] = sem
    return pltpu.CompilerParams(**kw)


def _sigmoid(z):
    return 1.0 / (1.0 + jnp.exp(-z))


def _rot64(x):
    lane = lax.broadcasted_iota(jnp.int32, x.shape, 1)
    return jnp.where(lane < 32, pltpu.roll(x, 96, 1), pltpu.roll(x, 32, 1))


def _rope(x, cos, sin):
    return x * cos + _rot64(x) * sin


def _rope_t(d, cos, sin):
    return d * cos - _rot64(d) * sin


def _dot(a, b):
    return jnp.dot(a, b, preferred_element_type=F32)


def _dot_nt(a, b):
    return lax.dot_general(a, b, (((1,), (1,)), ((), ())), preferred_element_type=F32)


def _dot_tn(a, b):
    return lax.dot_general(a, b, (((0,), (0,)), ((), ())), preferred_element_type=F32)


def _my_index():
    return 4 * lax.axis_index("x") + 2 * lax.axis_index("y") + lax.axis_index("c")


ANY = pl.BlockSpec(memory_space=pl.ANY)


class _Gather:
    def __init__(self, blocks, relay=False, parts=1):
        self.relay = relay
        self.parts = parts
        self.rows = [b.shape[0] // parts for b in blocks]
        self.n = n = len(blocks) * parts
        self.out_shape = [jax.ShapeDtypeStruct((N_DEV,) + b.shape, b.dtype) for b in blocks]
        self.scratch = [pltpu.SemaphoreType.DMA((7 * n,)), pltpu.SemaphoreType.DMA((7 * n,)),
                        pltpu.SemaphoreType.DMA((n,))]

    @staticmethod
    def _places():
        x, y, c = lax.axis_index("x"), lax.axis_index("y"), lax.axis_index("c")
        return (x, y, c), (x, y, 1 - c), [(1 - x, y), (x, 1 - y), (1 - x, 1 - y)]

    def _src(self, ins, a):
        block, part = divmod(a, self.parts)
        return ins[block] if self.parts == 1 else ins[block].at[pl.ds(part * self.rows[block], self.rows[block])]

    def _dst(self, outs, a, place):
        block, part = divmod(a, self.parts)
        ref = outs[block].at[4 * place[0] + 2 * place[1] + place[2]]
        return ref if self.parts == 1 else ref.at[pl.ds(part * self.rows[block], self.rows[block])]

    def _copy(self, outs, sems, a, k, block, to, src=None):
        dst = self._dst(outs, a, block)
        return pltpu.make_async_remote_copy(
            src_ref=dst if src is None else src, dst_ref=dst, send_sem=sems[0].at[7 * a + k],
            recv_sem=sems[1].at[7 * a + k], device_id=to, device_id_type=MESH)

    def _first(self, ins, outs, sems):
        me, sibling, chips = self._places()
        first = []
        for a in range(self.n):
            first.append(self._copy(outs, sems, a, 0, me, sibling, src=self._src(ins, a)))
            first += [self._copy(outs, sems, a, 1 + j, me, (*chip, me[2]), src=self._src(ins, a))
                      for j, chip in enumerate(chips[:2] if self.relay else chips)]
        return first

    def _relays(self, outs, sems):
        if not self.relay:
            return []
        (x, y, c), _, _ = self._places()
        via = (jnp.where(c == 0, 1 - x, x), jnp.where(c == 0, y, 1 - y))
        to = (jnp.where(c == 0, x, 1 - x), jnp.where(c == 0, 1 - y, y))
        return [self._copy(outs, sems, a, 3, (*via, c), (*to, c)) for a in range(self.n)]

    def _passed(self, outs, sems):
        me, sibling, chips = self._places()
        return [self._copy(outs, sems, a, 4 + j, (*chip, me[2]), sibling)
                for a in range(self.n) for j, chip in enumerate(chips)]

    def _mine(self, ins, outs, sems):
        me, _, _ = self._places()
        return [pltpu.make_async_copy(self._src(ins, a), self._dst(outs, a, me), sems[2].at[a]) for a in range(self.n)]

    def start(self, ins, outs, sems):
        for cp in self._mine(ins, outs, sems) + self._first(ins, outs, sems):
            cp.start()

    def forward(self, ins, outs, sems):
        del ins
        me, _, chips = self._places()
        passed, relays = self._passed(outs, sems), self._relays(outs, sems)
        for a in range(self.n):
            for j, chip in enumerate(chips[:2] if self.relay else chips):
                self._copy(outs, sems, a, 1 + j, (*chip, me[2]), me).wait_recv()
                passed[3 * a + j].start()
            if self.relay:
                relays[a].start()
        if self.relay:
            for a in range(self.n):
                self._copy(outs, sems, a, 3, (*chips[2], me[2]), me).wait_recv()
                passed[3 * a + 2].start()

    def finish(self, ins, outs, sems):
        me, sibling, chips = self._places()
        for a in range(self.n):
            self._copy(outs, sems, a, 0, sibling, me).wait_recv()
            for j, chip in enumerate(chips):
                self._copy(outs, sems, a, 4 + j, (*chip, 1 - me[2]), me).wait_recv()
        for cp in self._first(ins, outs, sems) + self._relays(outs, sems) + self._passed(outs, sems):
            cp.wait_send()
        for cp in self._mine(ins, outs, sems):
            cp.wait()


class _GatherDirect:
    FLIPS = [(0, 0, 1), (1, 0, 0), (0, 1, 0), (1, 1, 0), (1, 0, 1), (0, 1, 1), (1, 1, 1)]

    def __init__(self, blocks):
        self.n = n = len(blocks)
        self.out_shape = [jax.ShapeDtypeStruct((N_DEV,) + b.shape, b.dtype) for b in blocks]
        self.scratch = [pltpu.SemaphoreType.DMA((7 * n,)), pltpu.SemaphoreType.DMA((7 * n,)),
                        pltpu.SemaphoreType.DMA((n,))]

    def _copies(self, ins, outs, sems):
        x, y, c = lax.axis_index("x"), lax.axis_index("y"), lax.axis_index("c")
        mine = 4 * x + 2 * y + c
        remote = [pltpu.make_async_remote_copy(
            src_ref=ins[a], dst_ref=outs[a].at[mine], send_sem=sems[0].at[7 * a + k], recv_sem=sems[1].at[7 * a + k],
            device_id=(1 - x if fx else x, 1 - y if fy else y, 1 - c if fc else c), device_id_type=MESH)
            for a in range(self.n) for k, (fx, fy, fc) in enumerate(self.FLIPS)]
        local = [pltpu.make_async_copy(ins[a], outs[a].at[mine], sems[2].at[a]) for a in range(self.n)]
        return remote + local

    def start(self, ins, outs, sems):
        for cp in self._copies(ins, outs, sems):
            cp.start()

    def forward(self, ins, outs, sems):
        pass

    def finish(self, ins, outs, sems):
        for cp in self._copies(ins, outs, sems):
            cp.wait()


class _ChipExchange:
    def __init__(self, arrays):
        self.n = n = len(arrays)
        self.out_shape = [jax.ShapeDtypeStruct(a.shape, a.dtype) for a in arrays]
        self.scratch = [pltpu.SemaphoreType.DMA((3 * n,)), pltpu.SemaphoreType.DMA((3 * n,))]

    def _copies(self, ins, outs, sems):
        x, y, c = lax.axis_index("x"), lax.axis_index("y"), lax.axis_index("c")
        return [pltpu.make_async_remote_copy(
            src_ref=ins[a].at[2 * px + py], dst_ref=outs[a].at[2 * x + y], send_sem=sems[0].at[3 * a + j],
            recv_sem=sems[1].at[3 * a + j], device_id=(px, py, c), device_id_type=MESH)
            for a in range(self.n) for j, (px, py) in enumerate([(1 - x, y), (x, 1 - y), (1 - x, 1 - y)])]

    def start(self, ins, outs, sems):
        for cp in self._copies(ins, outs, sems):
            cp.start()

    def forward(self, ins, outs, sems):
        pass

    def finish(self, ins, outs, sems):
        for cp in self._copies(ins, outs, sems):
            cp.wait()


class _SiblingExchange:
    def __init__(self, arrays, windowed):
        self.n = n = len(arrays)
        self.windowed = windowed
        self.out_shape = [jax.ShapeDtypeStruct((4, a.shape[0], EXP_W) if w else (4,) + a.shape[1:], a.dtype)
                          for a, w in zip(arrays, windowed)]
        self.scratch = [pltpu.SemaphoreType.DMA((4 * n,)), pltpu.SemaphoreType.DMA((4 * n,))]

    def _each(self, ins, outs, sems, act):
        x, y, c = lax.axis_index("x"), lax.axis_index("y"), lax.axis_index("c")

        def branch(c_val):
            for k in range(4):
                e = 2 * k + (1 - c_val)
                for a in range(self.n):
                    src = ins[a].at[:, pl.ds(W_LO[e], EXP_W)] if self.windowed[a] else ins[a].at[e]
                    act(pltpu.make_async_remote_copy(
                        src_ref=src, dst_ref=outs[a].at[k], send_sem=sems[0].at[4 * a + k], recv_sem=sems[1].at[4 * a + k],
                        device_id=(x, y, 1 - c), device_id_type=MESH))

        for c_val in (0, 1):
            pl.when(c == c_val)(functools.partial(branch, c_val))

    def start(self, ins, outs, sems):
        self._each(ins, outs, sems, lambda cp: cp.start())

    def forward(self, ins, outs, sems):
        pass

    def finish(self, ins, outs, sems):
        self._each(ins, outs, sems, lambda cp: cp.wait())


def _exchange(rider, arrays, name):
    n = len(arrays)

    def body(*refs):
        ins, outs, sems = refs[:n], refs[n:n + len(rider.out_shape)], refs[n + len(rider.out_shape):]
        rider.start(ins, outs, sems)
        rider.forward(ins, outs, sems)
        rider.finish(ins, outs, sems)

    return pl.pallas_call(body, name=name, out_shape=rider.out_shape, in_specs=[ANY] * n,
                          out_specs=[ANY] * len(rider.out_shape), scratch_shapes=rider.scratch)(*arrays)


def _add_window(dw_in, recv, lo_tiles):
    k, rows, _ = recv.shape
    tiles = EXP_W // LANES

    def body(t_ref, *refs):
        del t_ref
        r_ref, o_ref = refs[tiles], refs[tiles + 1]
        own = jnp.concatenate([w_ref[...] for w_ref in refs[:tiles]], axis=1)
        o_ref[0] = (own.astype(F32) + r_ref[0].astype(F32)).astype(o_ref.dtype)

    def tile(j):
        return pl.BlockSpec((rows, LANES), lambda i, t: (0, t[i] + j))

    spec = pl.BlockSpec((1, rows, EXP_W), lambda i, t: (i, 0, 0))
    grid_spec = pltpu.PrefetchScalarGridSpec(
        num_scalar_prefetch=1, grid=(k,), in_specs=[tile(j) for j in range(tiles)] + [spec], out_specs=spec)
    return pl.pallas_call(
        body, name="rs_add_in", grid_spec=grid_spec, out_shape=jax.ShapeDtypeStruct(recv.shape, recv.dtype),
        compiler_params=_params(("parallel",)),
    )(lo_tiles, *([dw_in] * tiles), recv)


def _final_sum(p, r, sel, name, unshift=False, keep_t=None):
    _, rows, cols = p.shape
    tr = 512 if rows % 512 == 0 else rows

    def body(sel_ref, p_ref, r0, r1, r2, r3, o_ref):
        own = p_ref[0].astype(F32)
        acc = None
        for k, r_ref in enumerate((r0, r1, r2, r3)):
            term = jnp.where(sel_ref[0] == k, own, r_ref[0].astype(F32))
            acc = term if acc is None else acc + term
        if unshift:
            acc = pltpu.roll(acc, sel_ref[5], 1)
        o_ref[...] = acc if keep_t is None else acc.T[:keep_t]

    def slot(k):
        return pl.BlockSpec((1, tr, cols), lambda i, t: (t[k], i, 0))

    if keep_t is None:
        out_spec, out_shape = pl.BlockSpec((tr, cols), lambda i, t: (i, 0)), (rows, cols)
    else:
        out_spec, out_shape = pl.BlockSpec((keep_t, tr), lambda i, t: (0, i)), (keep_t, rows)
    grid_spec = pltpu.PrefetchScalarGridSpec(
        num_scalar_prefetch=1, grid=(rows // tr,), in_specs=[slot(0), slot(1), slot(2), slot(3), slot(4)],
        out_specs=out_spec)
    return pl.pallas_call(
        body, name=name, grid_spec=grid_spec, out_shape=jax.ShapeDtypeStruct(out_shape, F32),
        compiler_params=_params(("parallel",)),
    )(sel, p, r, r, r, r)


def _expand_w_in(w_t, shift):
    cw, rows = w_t.shape
    tr = 256
    pad = -cw % LANES

    def body(s_ref, w_ref, o_ref):
        w = jnp.concatenate([w_ref[...], jnp.zeros((pad, tr), F32)], axis=0).T
        w = jnp.concatenate([w, jnp.zeros((tr, EXP_W - cw - pad), F32)], axis=1)
        o_ref[...] = pltpu.roll(w, s_ref[0], 1).astype(BF16)

    grid_spec = pltpu.PrefetchScalarGridSpec(
        num_scalar_prefetch=1, grid=(rows // tr,), in_specs=[pl.BlockSpec((cw, tr), lambda i, t: (0, i))],
        out_specs=pl.BlockSpec((tr, EXP_W), lambda i, t: (i, 0)))
    return pl.pallas_call(
        body, name="expand_w_in", grid_spec=grid_spec, out_shape=jax.ShapeDtypeStruct((rows, EXP_W), BF16),
        compiler_params=_params(("arbitrary",)),
    )(shift, w_t)


def _pad_wq(w_t):
    cw, rows = w_t.shape

    def body(w_ref, o_ref):
        o_ref[...] = jnp.concatenate([w_ref[...], jnp.zeros((QK_PAD - cw, rows), F32)], axis=0).T.astype(BF16)

    return pl.pallas_call(
        body, name="pad_wq", out_shape=jax.ShapeDtypeStruct((rows, QK_PAD), BF16), compiler_params=_params(),
    )(w_t)


def _merge_w_in(e):
    _, rows, _ = e.shape
    tr = 256

    def body(e_ref, o_ref):
        for t in range(U_COLS // LANES):
            lo, hi = t * LANES, (t + 1) * LANES
            parts = [e_ref[d, :, lo - W_LO[d]:hi - W_LO[d]] for d in range(N_DEV)
                     if CW * d < hi and CW * (d + 1) > lo]
            if not parts:
                tile = jnp.zeros((tr, LANES), BF16)
            elif len(parts) == 1:
                tile = parts[0]
            else:
                tile = (parts[0].astype(F32) + parts[1].astype(F32)).astype(BF16)
            o_ref[:, lo:hi] = tile

    return pl.pallas_call(
        body, name="merge_w_in", grid=(rows // tr,),
        in_specs=[pl.BlockSpec((N_DEV, tr, EXP_W), lambda i: (0, i, 0))],
        out_specs=pl.BlockSpec((tr, U_COLS), lambda i: (i, 0)), out_shape=jax.ShapeDtypeStruct((rows, U_COLS), BF16),
        compiler_params=_params(("parallel",)),
    )(e)


def _sum_leading(a, out_dtype, name):
    k, rows, cols = a.shape
    tr = min(rows, 1728 if rows % 1728 == 0 else rows)

    def body(a_ref, o_ref):
        acc = a_ref[0].astype(F32)
        for i in range(1, k):
            acc = acc + a_ref[i].astype(F32)
        o_ref[...] = acc.astype(out_dtype)

    return pl.pallas_call(
        body, name=name, grid=(rows // tr,),
        in_specs=[pl.BlockSpec((k, tr, cols), lambda i: (0, i, 0))],
        out_specs=pl.BlockSpec((tr, cols), lambda i: (i, 0)),
        out_shape=jax.ShapeDtypeStruct((rows, cols), out_dtype), compiler_params=_params(("parallel",)),
    )(a)


def _add_pairs(g, recv, core, name):
    k, rows, cols = recv.shape
    tr = 1728 if rows % 1728 == 0 else rows

    def body(c_ref, g_ref, r_ref, o_ref):
        del c_ref
        o_ref[...] = (g_ref[...].astype(F32) + r_ref[...].astype(F32)).astype(o_ref.dtype)

    spec = pl.BlockSpec((1, tr, cols), lambda i, j, c: (i, j, 0))
    grid_spec = pltpu.PrefetchScalarGridSpec(
        num_scalar_prefetch=1, grid=(k, rows // tr),
        in_specs=[pl.BlockSpec((1, tr, cols), lambda i, j, c: (2 * i + c[0], j, 0)), spec], out_specs=spec)
    return pl.pallas_call(
        body, name=name, grid_spec=grid_spec, out_shape=jax.ShapeDtypeStruct(recv.shape, recv.dtype),
        compiler_params=_params(("parallel", "parallel")),
    )(core, g, recv)


def _ada_mod(c16, ada_w_l, ada_b_l):
    def body(c_ref, w_ref, b_ref, o_ref):
        cv = c_ref[...]
        sc = (cv * _sigmoid(cv)).astype(BF16)
        o_ref[...] = _dot(sc, w_ref[...].astype(BF16)) + b_ref[...]

    return pl.pallas_call(
        body, name="ada_mod", out_shape=jax.ShapeDtypeStruct((c16.shape[0], ada_w_l.shape[1]), F32),
        compiler_params=_params(),
    )(c16, ada_w_l, ada_b_l)


def _ada_w_grad(c_t, dmod_my):
    def body(c_ref, d_ref, o_ref):
        cv = c_ref[...]
        sc = cv * _sigmoid(cv)
        acc = sc[:, 0:1] * d_ref[0:1, :]
        for b in range(1, N_DEV):
            acc = acc + sc[:, b:b + 1] * d_ref[b:b + 1, :]
        o_ref[...] = acc

    return pl.pallas_call(
        body, name="ada_w_grad", out_shape=jax.ShapeDtypeStruct((c_t.shape[0], dmod_my.shape[1]), F32),
        compiler_params=_params(),
    )(c_t, dmod_my)


def _norm_mod(x, norm_g, mod, pos_col, invf, sign, rider, rider_inputs):
    s, d = x.shape
    tm = min(TM_MM, s)
    n_in, n_out = len(rider_inputs), len(rider.out_shape)
    steps = s // tm

    def body(x_ref, g_ref, mod_ref, p_ref, f_ref, s_ref, *rest):
        r_ins, (h_ref, ht_ref, cos_ref, sin_ref) = rest[:n_in], rest[n_in:n_in + 4]
        r_outs, sems = rest[n_in + 4:n_in + 4 + n_out], rest[n_in + 4 + n_out:]
        pl.when(pl.program_id(0) == 0)(functools.partial(rider.start, r_ins, r_outs, sems))
        xv = x_ref[...]
        r = lax.rsqrt(jnp.mean(xv * xv, axis=-1, keepdims=True) + EPS)
        hn = xv * r * g_ref[...]
        hv = hn * (1.0 + mod_ref[:, d:2 * d]) + mod_ref[:, 0:d]
        h_ref[...] = hv.astype(BF16)
        ht_ref[...] = hv.T.astype(BF16)
        ang = p_ref[...].astype(F32) * f_ref[...]
        sg = s_ref[...]
        cos_ref[...] = jnp.cos(ang) * jnp.abs(sg)
        sin_ref[...] = jnp.sin(ang) * sg

        @pl.when(pl.program_id(0) == steps - 1)
        def _():
            rider.forward(r_ins, r_outs, sems)
            rider.finish(r_ins, r_outs, sems)

    row = pl.BlockSpec((1, LANES), lambda i: (0, 0))
    tab = pl.BlockSpec((tm, LANES), lambda i: (i, 0))
    return pl.pallas_call(
        body, name="norm_mod", grid=(steps,),
        in_specs=[pl.BlockSpec((tm, d), lambda i: (i, 0)), pl.BlockSpec((1, d), lambda i: (0, 0)),
                  pl.BlockSpec((1, 3 * d), lambda i: (0, 0)), pl.BlockSpec((tm, 1), lambda i: (i, 0)), row, row]
        + [ANY] * n_in,
        out_specs=[pl.BlockSpec((tm, d), lambda i: (i, 0)), pl.BlockSpec((d, tm), lambda i: (0, i)), tab, tab] + [ANY] * n_out,
        out_shape=[jax.ShapeDtypeStruct((s, d), BF16), jax.ShapeDtypeStruct((d, s), BF16),
                   jax.ShapeDtypeStruct((s, LANES), F32), jax.ShapeDtypeStruct((s, LANES), F32)] + rider.out_shape,
        scratch_shapes=rider.scratch, compiler_params=_params(("arbitrary",)),
    )(x, norm_g, mod, pos_col, invf, sign, *rider_inputs)


def _matmul(a, b, *, nt, out_dtype, tm, tn, name, rider=None, rider_inputs=(), a_resident=False):
    m, kdim = a.shape
    n = b.shape[0] if nt else b.shape[1]
    tm, tn = min(tm, m), min(tn, n)
    n_in = len(rider_inputs)
    n_out = len(rider.out_shape) if rider else 0
    m_steps, n_steps = m // tm, n // tn
    steps = n_steps * m_steps
    inner = n_steps if a_resident else m_steps
    tile = (lambda o, i: (o, i)) if a_resident else (lambda o, i: (i, o))

    def body(a_ref, b_ref, *rest):
        r_ins, o_ref, r_outs, sems = rest[:n_in], rest[n_in], rest[n_in + 1:n_in + 1 + n_out], rest[n_in + 1 + n_out:]
        step = pl.program_id(0) * inner + pl.program_id(1)
        if rider:
            pl.when(step == 0)(functools.partial(rider.start, r_ins, r_outs, sems))
            pl.when(step == steps // 2)(functools.partial(rider.forward, r_ins, r_outs, sems))
        o = _dot_nt(a_ref[...], b_ref[...]) if nt else _dot(a_ref[...], b_ref[...])
        o_ref[...] = o.astype(out_dtype)
        if rider:
            pl.when(step == steps - 1)(functools.partial(rider.finish, r_ins, r_outs, sems))

    if nt:
        b_spec = pl.BlockSpec((tn, kdim), lambda o, i: (tile(o, i)[1], 0))
    else:
        b_spec = pl.BlockSpec((kdim, tn), lambda o, i: (0, tile(o, i)[1]))
    out = pl.pallas_call(
        body, name=name, grid=(m_steps, n_steps) if a_resident else (n_steps, m_steps),
        in_specs=[pl.BlockSpec((tm, kdim), lambda o, i: (tile(o, i)[0], 0)), b_spec] + [ANY] * n_in,
        out_specs=[pl.BlockSpec((tm, tn), tile)] + [ANY] * n_out,
        out_shape=[jax.ShapeDtypeStruct((m, n), out_dtype)] + (rider.out_shape if rider else []),
        scratch_shapes=rider.scratch if rider else [],
        compiler_params=_params(("arbitrary", "arbitrary") if rider else ("parallel", "parallel")),
    )(a, b, *rider_inputs)
    return out if rider else out[0]


HALO = 16


def _conv_specs(tm):
    def col(j):
        return pl.BlockSpec((tm, D_CONV), lambda i: (i, j))

    def prev(j):
        return pl.BlockSpec((HALO, D_CONV), lambda i: (jnp.maximum(i * (tm // HALO) - 1, 0), j))

    return [col(0), col(1), col(2), col(3), prev(0), prev(2)]


def _conv_y(xc_ref, bc_ref, cc_ref, zc_ref, xp_ref, cp_ref, w_ref, first):
    uc = cc_ref[...].astype(F32) * xc_ref[...].astype(F32)
    up = jnp.where(first, 0.0, cp_ref[...].astype(F32) * xp_ref[...].astype(F32))
    full = jnp.concatenate([up, uc], axis=0)
    u1 = pltpu.roll(full, 1, 0)[HALO:]
    u2 = pltpu.roll(full, 2, 0)[HALO:]
    w = w_ref[...]
    conv = w[0:1] * u2 + w[1:2] * u1 + w[2:3] * uc
    z = zc_ref[...].astype(F32)
    return bc_ref[...].astype(F32) * conv * (z * _sigmoid(z))


def _conv_bwd(u, dyc, conv_w):
    s = u.shape[0]
    tm = min(TM_MM, s)
    cb = D_CONV
    nt = s // tm

    def body(xc_ref, bc_ref, cc_ref, zc_ref, xp_ref, cp_ref, bn_ref, zn_ref, dy_ref, dyn_ref, w_ref, du_ref, dw_ref):
        i = pl.program_id(0)
        xc, cc = xc_ref[...].astype(F32), cc_ref[...].astype(F32)
        bc, z = bc_ref[...].astype(F32), zc_ref[...].astype(F32)
        uc = cc * xc
        up = jnp.where(i == 0, 0.0, cp_ref[...].astype(F32) * xp_ref[...].astype(F32))
        full = jnp.concatenate([up, uc], axis=0)
        u1 = pltpu.roll(full, 1, 0)[HALO:]
        u2 = pltpu.roll(full, 2, 0)[HALO:]
        w = w_ref[...]
        conv = w[0:1] * u2 + w[1:2] * u1 + w[2:3] * uc
        sg = _sigmoid(z)
        sz = z * sg
        dy = dy_ref[...].astype(F32)
        dconv = dy * bc * sz
        zn = zn_ref[...].astype(F32)
        dnext = dyn_ref[...].astype(F32) * bn_ref[...].astype(F32) * (zn * _sigmoid(zn))
        dnext = jnp.where(i == nt - 1, 0.0, dnext)
        fullb = jnp.concatenate([dconv, dnext], axis=0)
        nb = tm + HALO
        d1 = pltpu.roll(fullb, nb - 1, 0)[:tm]
        d2 = pltpu.roll(fullb, nb - 2, 0)[:tm]
        duc = w[2:3] * dconv + w[1:2] * d1 + w[0:1] * d2
        dzc = dy * bc * conv * (sg * (1.0 + z * (1.0 - sg)))
        du_ref[...] = jnp.concatenate([duc * cc, dy * conv * sz, duc * xc, dzc], axis=1).astype(BF16)
        dw = jnp.concatenate([jnp.sum(dconv * u2, axis=0, keepdims=True), jnp.sum(dconv * u1, axis=0, keepdims=True),
                              jnp.sum(dconv * uc, axis=0, keepdims=True), jnp.zeros((5, cb), F32)], axis=0)

        @pl.when(i == 0)
        def _():
            dw_ref[...] = dw

        @pl.when(i > 0)
        def _():
            dw_ref[...] += dw

    def col(j):
        return pl.BlockSpec((tm, cb), lambda i: (i, j))

    def prev(j):
        return pl.BlockSpec((HALO, cb), lambda i: (jnp.maximum(i * (tm // HALO) - 1, 0), j))

    def nxt(j):
        return pl.BlockSpec((HALO, cb), lambda i: (jnp.minimum((i + 1) * (tm // HALO), s // HALO - 1), j))

    return pl.pallas_call(
        body, name="conv_bwd", grid=(nt,),
        in_specs=[col(0), col(1), col(2), col(3), prev(0), prev(2), nxt(1), nxt(3), col(0), nxt(0),
                  pl.BlockSpec((3, cb), lambda i: (0, 0))],
        out_specs=[pl.BlockSpec((tm, 4 * cb), lambda i: (i, 0)), pl.BlockSpec((8, cb), lambda i: (0, 0))],
        out_shape=[jax.ShapeDtypeStruct((s, U_COLS), BF16), jax.ShapeDtypeStruct((8, cb), F32)],
        compiler_params=_params(("arbitrary",)),
    )(u, u, u, u, u, u, u, u, dyc, dyc, conv_w)


def _qkv_specs(tm):
    return [pl.BlockSpec((tm, Q_LORA), lambda i: (i, U_CQ // Q_LORA)),
            pl.BlockSpec((tm, KV_LORA), lambda i: (i, U_CKV // KV_LORA)),
            pl.BlockSpec((tm, LANES), lambda i: (i, U_KR // LANES)),
            pl.BlockSpec((tm, LANES), lambda i: (i, 0)), pl.BlockSpec((tm, LANES), lambda i: (i, 0))]


def _full(shape):
    return pl.BlockSpec(shape, lambda i: (0,) * len(shape))


def _k_rope_lanes(blk):
    lane = lax.broadcasted_iota(jnp.int32, blk.shape, 1)
    return jnp.where(lane < QK_ROPE, blk, 0.0)


def _qkv_fwd(u, cos, sin, wq, wkv, qag, kvag, qg, kg):
    s = u.shape[0]
    tm = min(TM_MM, s)

    def body(cq_ref, ckv_ref, kr_ref, cos_ref, sin_ref, wq_ref, wkv_ref, qag_ref, kvag_ref, qg_ref, kg_ref,
             q_ref, k_ref, v_ref):
        cq = cq_ref[...].astype(F32)
        cqn = (cq * lax.rsqrt(jnp.mean(cq * cq, axis=-1, keepdims=True) + EPS) * qag_ref[...]).astype(BF16)
        ckv = ckv_ref[...].astype(F32)
        ckvn = (ckv * lax.rsqrt(jnp.mean(ckv * ckv, axis=-1, keepdims=True) + EPS) * kvag_ref[...]).astype(BF16)
        kr = _k_rope_lanes(kr_ref[...].astype(F32))
        cosv, sinv, qgv, kgv = cos_ref[...], sin_ref[...], qg_ref[...], kg_ref[...]
        ss_r = jnp.sum(kr * kr, axis=-1, keepdims=True)
        krr = _rope(kr * kgv[:, QK_NOPE:], cosv, sinv)
        qf = _dot(cqn, wq_ref[...])
        kvf = _dot(ckvn, wkv_ref[...])
        heads = range(N_HEADS)
        qh = [qf[:, QK_PAD * h:QK_PAD * (h + 1)] for h in heads]
        kn = [kvf[:, 2 * V_HEAD * h:2 * V_HEAD * h + QK_NOPE] for h in heads]
        rq = [lax.rsqrt(jnp.sum(qh[h] * qh[h], axis=-1, keepdims=True) * (1.0 / QK_HEAD) + EPS) for h in heads]
        rk = [lax.rsqrt((jnp.sum(kn[h] * kn[h], axis=-1, keepdims=True) + ss_r) * (1.0 / QK_HEAD) + EPS) for h in heads]
        for h in heads:
            qn = qh[h] * rq[h] * qgv
            qo = jnp.concatenate([qn[:, :QK_NOPE], _rope(qn[:, QK_NOPE:], cosv, sinv)], axis=1) * (SCALE * LOG2E)
            q_ref[h] = qo.astype(BF16)
            vh = kvf[:, 2 * V_HEAD * h + QK_NOPE:2 * V_HEAD * (h + 1)]
            k_ref[h] = jnp.concatenate([kn[h] * kgv[:, :QK_NOPE] * rk[h], krr * rk[h]], axis=1).astype(BF16)
            v_ref[h] = jnp.concatenate([vh, jnp.ones_like(vh)], axis=1).astype(BF16)

    return pl.pallas_call(
        body, name="qkv_fwd", grid=(s // tm,),
        in_specs=_qkv_specs(tm) + [_full((Q_LORA, N_HEADS * QK_PAD)), _full((KV_LORA, 2 * D_ATTN)),
                                   _full((1, Q_LORA)), _full((1, KV_LORA)), _full((1, QK_PAD)), _full((1, QK_PAD))],
        out_specs=[pl.BlockSpec((N_HEADS, tm, QK_PAD), lambda i: (0, i, 0)),
                   pl.BlockSpec((N_HEADS, tm, QK_PAD), lambda i: (0, i, 0)),
                   pl.BlockSpec((N_HEADS, tm, 2 * V_HEAD), lambda i: (0, i, 0))],
        out_shape=[jax.ShapeDtypeStruct((N_HEADS, s, QK_PAD), BF16), jax.ShapeDtypeStruct((N_HEADS, s, QK_PAD), BF16),
                   jax.ShapeDtypeStruct((N_HEADS, s, 2 * V_HEAD), BF16)],
        compiler_params=_params(("parallel",)),
    )(u, u, u, cos, sin, wq, wkv, qag, kvag, qg, kg)


def _qkv_bwd(u, cos, sin, dq, dk, dv, dza, wq, wkv, qag, kvag, qg, kg, du):
    s = u.shape[0]
    tm = min(TM_ELEM, s)
    nt = s // tm

    def body(cq_ref, ckv_ref, kr_ref, cos_ref, sin_ref, dq_ref, dk_ref, dv_ref, dza_ref, wq_ref, wkv_ref, qag_ref,
             kvag_ref, qg_ref, kg_ref, du_in, du_ref, dwq_ref, dwkv_ref, dqag_ref, dkvag_ref, dqg_ref, dkg_ref,
             dwq_acc, dwkv_acc):
        del du_in
        i = pl.program_id(0)

        @pl.when(i == 0)
        def _():
            dwq_acc[...] = jnp.zeros_like(dwq_acc)
            dwkv_acc[...] = jnp.zeros_like(dwkv_acc)

        cq = cq_ref[...].astype(F32)
        rqa = lax.rsqrt(jnp.mean(cq * cq, axis=-1, keepdims=True) + EPS)
        xq = cq * rqa
        qagv = qag_ref[...]
        cqn = (xq * qagv).astype(BF16)
        ckv = ckv_ref[...].astype(F32)
        rkva = lax.rsqrt(jnp.mean(ckv * ckv, axis=-1, keepdims=True) + EPS)
        xkv = ckv * rkva
        kvagv = kvag_ref[...]
        ckvn = (xkv * kvagv).astype(BF16)
        kr = _k_rope_lanes(kr_ref[...].astype(F32))
        cosv, sinv, qgv, kgv = cos_ref[...], sin_ref[...], qg_ref[...], kg_ref[...]
        ss_r = jnp.sum(kr * kr, axis=-1, keepdims=True)
        dqg = jnp.zeros((1, QK_PAD), F32)
        dkg = jnp.zeros((1, QK_PAD), F32)
        dkr = jnp.zeros((tm, LANES), F32)
        qf = _dot(cqn, wq_ref[...])
        kvf = _dot(ckvn, wkv_ref[...])
        heads = range(N_HEADS)
        qh = [qf[:, QK_PAD * h:QK_PAD * (h + 1)] for h in heads]
        kn = [kvf[:, 2 * V_HEAD * h:2 * V_HEAD * h + QK_NOPE] for h in heads]
        rq = [lax.rsqrt(jnp.sum(qh[h] * qh[h], axis=-1, keepdims=True) * (1.0 / QK_HEAD) + EPS) for h in heads]
        rk = [lax.rsqrt((jnp.sum(kn[h] * kn[h], axis=-1, keepdims=True) + ss_r) * (1.0 / QK_HEAD) + EPS) for h in heads]
        xh = [qh[h] * rq[h] for h in heads]
        xk = [jnp.concatenate([kn[h], kr], axis=1) * rk[h] for h in heads]
        dyq, dyk = [], []
        for h in heads:
            g = dq_ref[h].astype(F32) * SCALE
            dyq.append(jnp.concatenate([g[:, :QK_NOPE], _rope_t(g[:, QK_NOPE:], cosv, sinv)], axis=1))
            gk = dk_ref[h].astype(F32)
            dyk.append(jnp.concatenate([gk[:, :QK_NOPE], _rope_t(gk[:, QK_NOPE:], cosv, sinv)], axis=1))
        for h in heads:
            dqg = dqg + jnp.sum(dyq[h] * xh[h], axis=0, keepdims=True)
            dkg = dkg + jnp.sum(dyk[h] * xk[h], axis=0, keepdims=True)
        gdy = [dyq[h] * qgv for h in heads]
        gdyk = [dyk[h] * kgv for h in heads]
        tq_ = [jnp.sum(gdy[h] * xh[h], axis=-1, keepdims=True) * (1.0 / QK_HEAD) for h in heads]
        tk_ = [jnp.sum(gdyk[h] * xk[h], axis=-1, keepdims=True) * (1.0 / QK_HEAD) for h in heads]
        dqf = [(rq[h] * (gdy[h] - xh[h] * tq_[h])).astype(BF16) for h in heads]
        dkvf = []
        for h in heads:
            dxk = rk[h] * (gdyk[h] - xk[h] * tk_[h])
            dkr = dkr + dxk[:, QK_NOPE:]
            dkvf += [dxk[:, :QK_NOPE].astype(BF16), dv_ref[h]]
        dqf_b, dkvf_b = jnp.concatenate(dqf, axis=1), jnp.concatenate(dkvf, axis=1)
        dwq_acc[...] += _dot_tn(cqn, dqf_b)
        dwkv_acc[...] += _dot_tn(ckvn, dkvf_b)
        dcqn = _dot_nt(dqf_b, wq_ref[...])
        dckvn = _dot_nt(dkvf_b, wkv_ref[...])
        dqag = jnp.sum(dcqn * xq, axis=0, keepdims=True)
        dkvag = jnp.sum(dckvn * xkv, axis=0, keepdims=True)
        gq = dcqn * qagv
        dcq = rqa * (gq - xq * jnp.mean(gq * xq, axis=-1, keepdims=True))
        gkv = dckvn * kvagv
        dckv = rkva * (gkv - xkv * jnp.mean(gkv * xkv, axis=-1, keepdims=True))
        win = pltpu.roll(jnp.concatenate([dza_ref[...].astype(F32), jnp.zeros((tm, LANES), F32)], axis=1), QK_ROPE, 1)
        win = win + jnp.concatenate([dkr, jnp.zeros((tm, D_ATTN), F32)], axis=1)
        du_ref[...] = jnp.concatenate([dcq, dckv, win, jnp.zeros((tm, U_TAIL - ZA_LO - ZA_WIN), F32)], axis=1).astype(BF16)

        @pl.when(i == 0)
        def _():
            dqag_ref[...] = dqag
            dkvag_ref[...] = dkvag
            dqg_ref[...] = dqg
            dkg_ref[...] = dkg

        @pl.when(i > 0)
        def _():
            dqag_ref[...] += dqag
            dkvag_ref[...] += dkvag
            dqg_ref[...] += dqg
            dkg_ref[...] += dkg

        @pl.when(i == nt - 1)
        def _():
            dwq_ref[...] = dwq_acc[...].astype(BF16)
            dwkv_ref[...] = dwkv_acc[...].astype(BF16)

    head = lambda w: pl.BlockSpec((N_HEADS, tm, w), lambda i: (0, i, 0))
    wq_shape, wkv_shape = (Q_LORA, N_HEADS * QK_PAD), (KV_LORA, 2 * D_ATTN)
    return pl.pallas_call(
        body, name="qkv_bwd", grid=(nt,),
        in_specs=_qkv_specs(tm) + [head(QK_PAD), head(QK_PAD), head(V_HEAD), pl.BlockSpec((tm, D_ATTN), lambda i: (i, 0)),
                                   _full(wq_shape), _full(wkv_shape), _full((1, Q_LORA)), _full((1, KV_LORA)),
                                   _full((1, QK_PAD)), _full((1, QK_PAD)), ANY],
        out_specs=[pl.BlockSpec((tm, U_TAIL), lambda i: (i, U_COLS // U_TAIL - 1)), _full(wq_shape), _full(wkv_shape),
                   _full((1, Q_LORA)), _full((1, KV_LORA)), _full((1, QK_PAD)), _full((1, QK_PAD))],
        out_shape=[jax.ShapeDtypeStruct(du.shape, du.dtype), jax.ShapeDtypeStruct(wq_shape, BF16),
                   jax.ShapeDtypeStruct(wkv_shape, BF16), jax.ShapeDtypeStruct((1, Q_LORA), F32),
                   jax.ShapeDtypeStruct((1, KV_LORA), F32), jax.ShapeDtypeStruct((1, QK_PAD), F32),
                   jax.ShapeDtypeStruct((1, QK_PAD), F32)],
        scratch_shapes=[pltpu.VMEM(wq_shape, F32), pltpu.VMEM(wkv_shape, F32)],
        input_output_aliases={15: 0}, compiler_params=_params(("arbitrary",)),
    )(u, u, u, cos, sin, dq, dk, dv, dza, wq, wkv, qag, kvag, qg, kg, du)


def _flash_fwd(q, k, v):
    nh, s, _ = q.shape
    tq = min(TQ, s)
    nkv = KV_SPLIT
    tk = tq // nkv
    nq = s // tq
    nch = Q_CHAINS
    tc = tq // nch

    def body(q_ref, k_ref, v_ref, o_ref, lse_ref):
        i = pl.program_id(1)
        chains = [q_ref[0, r * tc:(r + 1) * tc, :] for r in range(nch)]

        def unit(r, j, carry, shift=None):
            m, acc = carry
            rows = pl.ds(pl.multiple_of(j * tk, tk), tk)
            sc = _dot_nt(chains[r], k_ref[0, rows, :])
            if shift is not None:
                qi = lax.broadcasted_iota(jnp.int32, sc.shape, 0)
                ki = lax.broadcasted_iota(jnp.int32, sc.shape, 1) + shift
                sc = jnp.where(ki <= qi, sc, NEG)
            m_new = jnp.maximum(m, jnp.max(sc, axis=-1, keepdims=True))
            p = jnp.exp2(sc - m_new).astype(BF16)
            return m_new, jnp.exp2(m - m_new) * acc + _dot(p, v_ref[0, rows, :])

        def trip(p, carry):
            for b in range(nkv):
                carry = tuple(unit(r, nkv * p + b, cr) for r, cr in enumerate(carry))
            return carry

        init = (jnp.full((tc, 1), NEG, F32), jnp.zeros((tc, 2 * V_HEAD), F32))
        carry = list(lax.fori_loop(0, i, trip, (init,) * nch))
        for b in range(nkv):
            for r in range(nch):
                shift = b * tk - r * tc
                if shift < tc:
                    carry[r] = unit(r, nkv * i + b, carry[r], None if shift + tk - 1 <= 0 else shift)
        for r, (m, acc) in enumerate(carry):
            l = acc[:, V_HEAD:]
            o_ref[r * tc:(r + 1) * tc, :] = (acc[:, :V_HEAD] / l).astype(BF16)
            lse = m + jnp.log(l[:, 0:1]) * LOG2E
            lse_ref[0, :, r * tc:(r + 1) * tc] = jnp.broadcast_to(lse, (tc, LANES)).T[0:1, :]

    return pl.pallas_call(
        body, name="flash_fwd", grid=(nh, nq),
        in_specs=[pl.BlockSpec((1, tq, QK_PAD), lambda h, i: (h, i, 0)),
                  pl.BlockSpec((1, s, QK_PAD), lambda h, i: (h, 0, 0)),
                  pl.BlockSpec((1, s, 2 * V_HEAD), lambda h, i: (h, 0, 0))],
        out_specs=[pl.BlockSpec((tq, V_HEAD), lambda h, i: (i, h)), pl.BlockSpec((1, 1, tq), lambda h, i: (h, 0, i))],
        out_shape=[jax.ShapeDtypeStruct((s, nh * V_HEAD), BF16), jax.ShapeDtypeStruct((nh, 1, s), F32)],
        compiler_params=_params(("parallel", "arbitrary")),
    )(q, k, v)


def _flash_bwd(q, k, v, do, lse, delta):
    nh, s, _ = q.shape
    tq = min(TQ, s)
    nq = s // tq
    kps = 2 if nq % 2 == 0 else 1
    ng = nq // kps

    def body(q_ref, k_ref, v_ref, do_ref, lse_ref, dl_ref, dq_ref, dk_ref, dv_ref, dq_acc):
        g = ng - 1 - pl.program_id(1)

        @pl.when(g == ng - 1)
        def _():
            dq_acc[...] = jnp.zeros_like(dq_acc)

        for sub in reversed(range(kps)):
            kv_block(q_ref, k_ref, v_ref, do_ref, lse_ref, dl_ref, dk_ref, dv_ref, dq_acc, g * kps + sub, sub)

        @pl.when(g == 0)
        def _():
            dq_ref[0] = dq_acc[...].astype(BF16)

    def kv_block(q_ref, k_ref, v_ref, do_ref, lse_ref, dl_ref, dk_ref, dv_ref, dq_acc, j, sub):
        own = slice(sub * tq, (sub + 1) * tq)
        kj, vj = k_ref[0, own, :], v_ref[0, own, :]

        def block(kk, vv, qq, dd, lse, dl, masked):
            st = _dot_nt(kk, qq)
            pt = jnp.exp2(st - lse)
            if masked:
                ki = lax.broadcasted_iota(jnp.int32, st.shape, 0)
                qx = lax.broadcasted_iota(jnp.int32, st.shape, 1)
                pt = jnp.where(ki <= qx, pt, 0.0)
            ddv = _dot(pt.astype(BF16), dd)
            dst = (pt * (_dot_nt(vv, dd) - dl)).astype(BF16)
            ddq = _dot_tn(dst, kk)
            return _dot(dst, qq), ddv, ddq

        def step(i, carry):
            dk, dv = carry
            rows = pl.ds(pl.multiple_of(i * tq, tq), tq)
            ddk, ddv, ddq = block(kj, vj, q_ref[0, rows, :], do_ref[rows, :], lse_ref[0, pl.ds(i, 1), :],
                                  dl_ref[0, pl.ds(i, 1), :], False)
            dq_acc[rows, :] += ddq
            return dk + ddk, dv + ddv

        th = tq // 2
        lse_j, dl_j = lse_ref[0, pl.ds(j, 1), :], dl_ref[0, pl.ds(j, 1), :]
        parts = []
        for kh, qh, masked in ((0, 0, True), (0, 1, False), (1, 1, True)):
            rows = pl.ds(pl.multiple_of(j * tq + qh * th, th), th)
            ks, qs = slice(kh * th, (kh + 1) * th), slice(qh * th, (qh + 1) * th)
            ddk, ddv, ddq = block(kj[ks], vj[ks], q_ref[0, rows, :], do_ref[rows, :], lse_j[:, qs], dl_j[:, qs], masked)
            dq_acc[rows, :] += ddq
            parts.append((ddk, ddv))
        carry = (jnp.concatenate([parts[0][0] + parts[1][0], parts[2][0]], axis=0),
                 jnp.concatenate([parts[0][1] + parts[1][1], parts[2][1]], axis=0))
        dk, dv = lax.fori_loop(j + 1, nq, step, carry)
        dk_ref[0, own, :] = (dk * LN2).astype(BF16)
        dv_ref[0, own, :] = dv.astype(BF16)

    return pl.pallas_call(
        body, name="flash_bwd", grid=(nh, ng),
        in_specs=[pl.BlockSpec((1, s, QK_PAD), lambda h, j: (h, 0, 0)),
                  pl.BlockSpec((1, kps * tq, QK_PAD), lambda h, g: (h, ng - 1 - g, 0)),
                  pl.BlockSpec((1, kps * tq, V_HEAD), lambda h, g: (h, ng - 1 - g, 0)),
                  pl.BlockSpec((s, V_HEAD), lambda h, j: (0, h)),
                  pl.BlockSpec((1, nq, tq), lambda h, j: (h, 0, 0)),
                  pl.BlockSpec((1, nq, tq), lambda h, j: (h, 0, 0))],
        out_specs=[pl.BlockSpec((1, s, QK_PAD), lambda h, j: (h, 0, 0)),
                   pl.BlockSpec((1, kps * tq, QK_PAD), lambda h, g: (h, ng - 1 - g, 0)),
                   pl.BlockSpec((1, kps * tq, V_HEAD), lambda h, g: (h, ng - 1 - g, 0))],
        out_shape=[jax.ShapeDtypeStruct((nh, s, QK_PAD), BF16), jax.ShapeDtypeStruct((nh, s, QK_PAD), BF16),
                   jax.ShapeDtypeStruct((nh, s, V_HEAD), BF16)],
        scratch_shapes=[pltpu.VMEM((s, QK_PAD), F32)],
        compiler_params=_params(("parallel", "arbitrary")),
    )(q, k, v, do, lse, delta)


def _tail(x, target, o, u, mod, w_out, conv_w):
    s, d = x.shape
    tm = min(TM_ELEM, s)

    def body(x_ref, t_ref, o_ref, za_ref, mod_ref, w_ref, xc_ref, bc_ref, cc_ref, zc_ref, xp_ref, cp_ref, cw_ref,
             gx_ref, dy_ref, ycat_ref, dyc_ref, do_ref, du_ref, delta_ref, dgate_ref, loss_ref):
        i = pl.program_id(0)
        za = pltpu.roll(za_ref[:, ZA_LO:ZA_LO + ZA_WIN].astype(F32), ZA_WIN - QK_ROPE, 1)[:, :D_ATTN]
        ov = o_ref[...].astype(F32)
        sg = _sigmoid(za)
        sl = za * sg
        ya = ov * sl
        y = _dot(ya.astype(BF16), w_ref[D_CONV:, :])
        yc = _conv_y(xc_ref, bc_ref, cc_ref, zc_ref, xp_ref, cp_ref, cw_ref, i == 0)
        y = y + _dot(yc.astype(BF16), w_ref[:D_CONV, :])
        ycat_ref[...] = jnp.concatenate([yc.T, ya.T], axis=0).astype(BF16)
        gate = mod_ref[:, 2 * d:3 * d]
        e = x_ref[...] + gate * y - t_ref[...]
        dout = e * (1.0 / d)
        gx_ref[...] = dout
        dy = (dout * gate).astype(BF16)
        dy_ref[...] = dy
        dycat = _dot_nt(dy, w_ref[...])
        dyc_ref[...] = dycat[:, :D_CONV].astype(BF16)
        dya = dycat[:, D_CONV:]
        dov = dya * sl
        do_ref[...] = dov.astype(BF16)
        du_ref[...] = (dya * ov * (sg * (1.0 + za * (1.0 - sg)))).astype(BF16)
        prod_t = (dov * ov).T
        for h in range(N_HEADS):
            delta_ref[h] = jnp.sum(prod_t[V_HEAD * h:V_HEAD * (h + 1), :], axis=0, keepdims=True)
        dgate = jnp.sum(dout * y, axis=0, keepdims=True)
        part = jnp.sum(jnp.sum(e * e, axis=0, keepdims=True), axis=1, keepdims=True) * (0.5 / d)
        part = jnp.broadcast_to(part, (1, LANES))

        @pl.when(i == 0)
        def _():
            dgate_ref[...] = dgate
            loss_ref[...] = part

        @pl.when(i > 0)
        def _():
            dgate_ref[...] += dgate
            loss_ref[...] += part

    tok = lambda w: pl.BlockSpec((tm, w), lambda i: (i, 0))
    return pl.pallas_call(
        body, name="tail", grid=(s // tm,),
        in_specs=[tok(d), tok(d), tok(D_ATTN), pl.BlockSpec((tm, U_TAIL), lambda i: (i, U_COLS // U_TAIL - 1)),
                  _full((1, 3 * d)), _full((d, d))] + _conv_specs(tm) + [_full((3, D_CONV))],
        out_specs=[tok(d), tok(d), pl.BlockSpec((d, tm), lambda i: (0, i)), tok(D_CONV), tok(D_ATTN), tok(D_ATTN),
                   pl.BlockSpec((N_HEADS, 1, tm), lambda i: (0, 0, i)), _full((1, d)), _full((1, LANES))],
        out_shape=[jax.ShapeDtypeStruct((s, d), F32), jax.ShapeDtypeStruct((s, d), BF16),
                   jax.ShapeDtypeStruct((d, s), BF16), jax.ShapeDtypeStruct((s, D_CONV), BF16),
                   jax.ShapeDtypeStruct((s, D_ATTN), BF16), jax.ShapeDtypeStruct((s, D_ATTN), BF16),
                   jax.ShapeDtypeStruct((N_HEADS, 1, s), F32), jax.ShapeDtypeStruct((1, d), F32),
                   jax.ShapeDtypeStruct((1, LANES), F32)],
        compiler_params=_params(("arbitrary",)),
    )(x, target, o, u, mod, w_out, u, u, u, u, u, u, conv_w)


def _norm_bwd(x, dh, gx1, norm_g, mod):
    s, d = x.shape
    tm = min(TM_MM, s)

    def body(x_ref, dh_ref, gx_ref, g_ref, mod_ref, o_ref, dshift_ref, dscale_ref, dg_ref):
        i = pl.program_id(0)
        gv, sc1 = g_ref[...], 1.0 + mod_ref[:, d:2 * d]
        gsc = gv * sc1
        half = NORM_ROWS // 2

        def group(c, acc):
            a_dh, a_dhxn = acc
            ks = range(NORM_GROUP)
            rows = [pl.ds(pl.multiple_of((c * NORM_GROUP + k) * NORM_ROWS, NORM_ROWS), NORM_ROWS) for k in ks]
            xv = [x_ref[rows[k], :] for k in ks]
            dhv = [dh_ref[rows[k], :].astype(F32) for k in ks]
            r = [lax.rsqrt(jnp.mean(xv[k] * xv[k], axis=-1, keepdims=True) + EPS) for k in ks]
            xn = [xv[k] * r[k] for k in ks]
            dxn = [dhv[k] * gsc for k in ks]
            t = [jnp.mean(dxn[k] * xn[k], axis=-1, keepdims=True) for k in ks]
            for k in ks:
                o_ref[rows[k], :] = gx_ref[rows[k], :] + r[k] * (dxn[k] - xn[k] * t[k])
                dhxn = dhv[k] * xn[k]
                a_dh = a_dh + dhv[k][:half] + dhv[k][half:]
                a_dhxn = a_dhxn + dhxn[:half] + dhxn[half:]
            return a_dh, a_dhxn

        zero = jnp.zeros((half, d), F32)
        a_dh, a_dhxn = lax.fori_loop(0, tm // (NORM_ROWS * NORM_GROUP), group, (zero, zero))
        dshift = jnp.sum(a_dh, axis=0, keepdims=True)
        s_dhxn = jnp.sum(a_dhxn, axis=0, keepdims=True)
        dscale, dg = s_dhxn * gv, s_dhxn * sc1

        @pl.when(i == 0)
        def _():
            dshift_ref[...] = dshift
            dscale_ref[...] = dscale
            dg_ref[...] = dg

        @pl.when(i > 0)
        def _():
            dshift_ref[...] += dshift
            dscale_ref[...] += dscale
            dg_ref[...] += dg

    tok = pl.BlockSpec((tm, d), lambda i: (i, 0))
    row = jax.ShapeDtypeStruct((1, d), F32)
    return pl.pallas_call(
        body, name="norm_bwd", grid=(s // tm,),
        in_specs=[tok, tok, tok, _full((1, d)), _full((1, 3 * d))],
        out_specs=[tok, _full((1, d)), _full((1, d)), _full((1, d))],
        out_shape=[jax.ShapeDtypeStruct((s, d), F32), row, row, row],
        compiler_params=_params(("arbitrary",)),
    )(x, dh, gx1, norm_g, mod)


def _adamw(w, g, m, v, name):
    rows, cols = w.shape
    tr = 256 if rows % 256 == 0 else rows
    tc = 512 if (rows > 256 and tr == rows and cols % 512 == 0) else cols

    def body(w_ref, g_ref, m_ref, v_ref, d_ref, nm_ref, nv_ref):
        gv = g_ref[...]
        nm = ADAM_B1 * m_ref[...] + (1.0 - ADAM_B1) * gv
        nv = ADAM_B2 * v_ref[...] + (1.0 - ADAM_B2) * (gv * gv)
        m_hat = nm / (1.0 - ADAM_B1 ** ADAM_STEP)
        v_hat = nv / (1.0 - ADAM_B2 ** ADAM_STEP)
        d_ref[...] = -ADAM_LR * (m_hat / (jnp.sqrt(v_hat) + ADAM_EPS) + ADAM_WD * w_ref[...])
        nm_ref[...] = nm
        nv_ref[...] = nv

    spec = pl.BlockSpec((tr, tc), lambda i, j: (i, j))
    shape = jax.ShapeDtypeStruct((rows, cols), F32)
    return pl.pallas_call(
        body, name=name, grid=(rows // tr, cols // tc), in_specs=[spec] * 4, out_specs=[spec] * 3, out_shape=[shape] * 3,
        compiler_params=_params(("parallel", "parallel")),
    )(w, g, m, v)


def _pad_cols(a, n):
    return jnp.pad(a, ((0, 0), (0, n - a.shape[1])))


def kernel(x, c, positions, ada_w, ada_b, norm_g, w_in, conv_w, q_a_g, w_q_b, kv_a_g, w_kv_b, q_g, k_g, w_out, loss_target, m_ada_w, m_ada_b, m_norm_g, m_w_in, m_conv_w, m_q_a_g, m_w_q_b, m_kv_a_g, m_w_kv_b, m_q_g, m_k_g, m_w_out, v_ada_w, v_ada_b, v_norm_g, v_w_in, v_conv_w, v_q_a_g, v_w_q_b, v_kv_a_g, v_w_kv_b, v_q_g, v_k_g, v_w_out):
    me = _my_index()
    s = x.shape[1]
    nq = s // min(TQ, s)
    x2, tgt = x[0], loss_target[0]
    w_in_l, w_q_l, w_kv_l, w_out_l, conv_l, ada_w_l = w_in[0], w_q_b[0], w_kv_b[0], w_out[0], conv_w[0], ada_w[0]
    ada_cols = ada_w_l.shape[1]

    small = jnp.concatenate([c.reshape(-1, LANES), conv_l.reshape(-1, LANES), jnp.zeros((5, LANES), F32)], axis=0)
    (small_g,) = _exchange(_GatherDirect([small]), [small], "gather_c")
    c_all = small_g[:, :D_MODEL // LANES].reshape(N_DEV, D_MODEL)
    conv_g = small_g[:, D_MODEL // LANES:D_MODEL // LANES + 3].transpose(1, 0, 2).reshape(3, D_CONV)

    ada_b_l = lax.dynamic_slice(ada_b, (0, me * ada_cols), (1, ada_cols))
    mod_cols = _ada_mod(jnp.pad(c_all, ((0, 8), (0, 0))), ada_w_l, ada_b_l)[:N_DEV]
    (mod_g,) = _exchange(_GatherDirect([mod_cols]), [mod_cols], "gather_mod")
    mod = lax.dynamic_index_in_dim(mod_g, me, axis=1, keepdims=False).reshape(1, 3 * D_MODEL)

    half = jnp.arange(0, QK_ROPE, 2, dtype=F32) / QK_ROPE
    inv_freq = ROPE_BASE ** (-half)
    zeros64 = jnp.zeros((LANES - QK_ROPE,), F32)
    invf = jnp.concatenate([inv_freq, inv_freq, zeros64]).reshape(1, LANES)
    sign = jnp.concatenate([-jnp.ones((32,), F32), jnp.ones((32,), F32), zeros64]).reshape(1, LANES)
    qg_p, kg_p = _pad_cols(q_g, QK_PAD), _pad_cols(k_g, QK_PAD)

    my_off = ((CW * me) % LANES).astype(jnp.int32)
    win = [_expand_w_in(w_in_l.T, my_off.reshape(1))]
    h, h_t, cos, sin, win_g = _norm_mod(x2, norm_g, mod, positions.reshape(s, 1), invf, sign, _Gather(win, relay=True, parts=4), win)
    w_in_p = _merge_w_in(win_g)
    rest = [_pad_wq(w_q_l.T), w_kv_l.astype(BF16), w_out_l.astype(BF16)]
    u, wq_g, wkv_g, w_out_g = _matmul(h, w_in_p, nt=False, out_dtype=BF16, tm=2 * TM_MM, tn=2048, name="in_proj",
                                      rider=_Gather(rest), rider_inputs=rest)
    w_out_g = w_out_g.reshape(D_MODEL, D_MODEL)
    wq_g = wq_g.transpose(1, 0, 2).reshape(Q_LORA, N_HEADS * QK_PAD)
    wkv_g = wkv_g.transpose(1, 0, 2).reshape(KV_LORA, 2 * D_ATTN)
    q, k, v = _qkv_fwd(u, cos, sin, wq_g, wkv_g, q_a_g, kv_a_g, qg_p, kg_p)
    o, lse = _flash_fwd(q, k, v)
    gx1, dy, ycat_t, dyc, do, dza, delta, dgate, loss_row = _tail(x2, tgt, o, u, mod, w_out_g, conv_g)

    dq, dk, dv = _flash_bwd(q, k, v, do, lse.reshape(N_HEADS, nq, s // nq), delta.reshape(N_HEADS, nq, s // nq))
    du, dconv = _conv_bwd(u, dyc, conv_g)
    du, dwq, dwkv, dqag, dkvag, dqg, dkg = _qkv_bwd(u, cos, sin, dq, dk, dv, dza, wq_g, wkv_g, q_a_g, kv_a_g, qg_p, kg_p, du)
    dwq = dwq.reshape(Q_LORA, N_HEADS, QK_PAD).transpose(1, 0, 2)
    dwkv = dwkv.reshape(KV_LORA, N_HEADS, 2 * V_HEAD).transpose(1, 0, 2)
    dw_in = _matmul(h_t, du, nt=False, out_dtype=BF16, tm=TM_MM, tn=768, name="dw_in")
    first = [dw_in, dwq, dwkv]
    dw_out, r_in, r_q, r_kv = _matmul(ycat_t, dy, nt=False, out_dtype=BF16, tm=TM_MM, tn=512, name="dw_out",
                                      rider=_SiblingExchange(first, [True, False, False]), rider_inputs=first)
    dw_out = dw_out.reshape(N_DEV, D_MODEL // N_DEV, D_MODEL)
    (r_out,) = _exchange(_SiblingExchange([dw_out], [False]), [dw_out], "rs_sibling_out")
    core = lax.axis_index("c").astype(jnp.int32)
    lo_tiles = ((CW * (2 * jnp.arange(4, dtype=jnp.int32) + core)) // LANES).astype(jnp.int32)
    pairs = [_add_window(dw_in, r_in, lo_tiles), _add_pairs(dwq, r_q, core.reshape(1), "rs_add_q"),
             _add_pairs(dwkv, r_kv, core.reshape(1), "rs_add_kv"), _add_pairs(dw_out, r_out, core.reshape(1), "rs_add_out")]
    dh, *quads = _matmul(du, w_in_p, nt=True, out_dtype=BF16, tm=2 * TM_MM, tn=512, name="dh",
                         rider=_ChipExchange(pairs), rider_inputs=pairs, a_resident=True)
    my_chip = 2 * lax.axis_index("x") + lax.axis_index("y")
    written = jnp.where(jnp.arange(4) == my_chip, (jnp.arange(4) + 1) % 4, jnp.arange(4))
    sel = jnp.concatenate([my_chip.reshape(1), written, ((EXP_W - my_off) % EXP_W).reshape(1)]).astype(jnp.int32)
    g_w_in_t = _final_sum(pairs[0], quads[0], sel, "rs_sum_in", unshift=True, keep_t=CW)
    g_w_q_t = _final_sum(pairs[1], quads[1], sel, "rs_sum_q", keep_t=QK_HEAD)
    g_w_kv = _final_sum(pairs[2], quads[2], sel, "rs_sum_kv")
    g_w_out = _final_sum(pairs[3], quads[3], sel, "rs_sum_out")
    grad_x, dshift, dscale, dng = _norm_bwd(x2, dh, gx1, norm_g, mod)

    row = jnp.concatenate([dshift, dscale, dgate, dng, dqag, dkvag, dqg, dkg, dconv[:3].reshape(1, 3 * D_CONV), loss_row], axis=1)
    (rows_g,) = _exchange(_GatherDirect([row]), [row], "gather_small")
    tot = _sum_leading(rows_g, F32, "sum_small")
    dmod_all = rows_g[:, 0, SM_MOD:SM_NG]
    g_ada_b = tot[:, SM_MOD:SM_NG]
    g_norm_g = tot[:, SM_NG:SM_QAG]
    g_q_a_g = tot[:, SM_QAG:SM_KVAG]
    g_kv_a_g = tot[:, SM_KVAG:SM_QG]
    g_q_g = tot[:, SM_QG:SM_QG + QK_HEAD]
    g_k_g = tot[:, SM_KG:SM_KG + QK_HEAD]
    conv_cols = conv_l.shape[1]
    g_conv = lax.dynamic_slice(tot[:, SM_CONV:SM_LOSS].reshape(3, D_CONV), (0, me * conv_cols), (3, conv_cols))
    loss = tot[0, SM_LOSS]
    dmod_my = lax.dynamic_slice(dmod_all, (0, me * ada_cols), (N_DEV, ada_cols))
    g_ada_w = _ada_w_grad(c_all.T, dmod_my)

    grads = dict(ada_w=g_ada_w, ada_b=g_ada_b, norm_g=g_norm_g, w_in=g_w_in_t, conv_w=g_conv, q_a_g=g_q_a_g, w_q_b=g_w_q_t,
                 kv_a_g=g_kv_a_g, w_kv_b=g_w_kv, q_g=g_q_g, k_g=g_k_g, w_out=g_w_out)
    weights = dict(ada_w=(ada_w, m_ada_w, v_ada_w), ada_b=(ada_b, m_ada_b, v_ada_b), norm_g=(norm_g, m_norm_g, v_norm_g),
                   w_in=(w_in, m_w_in, v_w_in), conv_w=(conv_w, m_conv_w, v_conv_w), q_a_g=(q_a_g, m_q_a_g, v_q_a_g),
                   w_q_b=(w_q_b, m_w_q_b, v_w_q_b), kv_a_g=(kv_a_g, m_kv_a_g, v_kv_a_g), w_kv_b=(w_kv_b, m_w_kv_b, v_w_kv_b),
                   q_g=(q_g, m_q_g, v_q_g), k_g=(k_g, m_k_g, v_k_g), w_out=(w_out, m_w_out, v_w_out))
    names = list(grads)
    out_g, out_d, out_m, out_v = [], [], [], []
    for n in names:
        w, m, v_ = weights[n]
        shape2 = w.shape[-2:] if w.ndim == 3 else (1, w.shape[-1])
        transposed = n in ("w_in", "w_q_b")
        to2 = (lambda a: a.reshape(shape2).T) if transposed else (lambda a: a.reshape(shape2))
        back = (lambda a: a.T.reshape(w.shape)) if transposed else (lambda a: a.reshape(w.shape))
        g2 = grads[n] if transposed else grads[n].reshape(shape2)
        d2, m2, v2 = _adamw(to2(w), g2, to2(m), to2(v_), "adamw_" + n)
        out_g.append(back(g2))
        out_d.append(back(d2))
        out_m.append(back(m2))
        out_v.append(back(v2))
    return (loss, grad_x.reshape(x.shape), *out_g, *out_d, *out_m, *out_v)
```

```python
import functools
import math

import jax
import jax.numpy as jnp
from jax import lax
from jax.experimental import pallas as pl
from jax.experimental.pallas import tpu as pltpu

F32 = jnp.float32
BF16 = jnp.bfloat16
MESH = pl.DeviceIdType.MESH

D_MODEL = 2048
D_CONV = 1024
N_HEADS = 8
QK_NOPE = 128
QK_ROPE = 64
QK_HEAD = QK_NOPE + QK_ROPE
V_HEAD = 128
D_ATTN = N_HEADS * V_HEAD
Q_LORA = 512
KV_LORA = 256
ROPE_BASE = 10000.0
IN_COLS = 4 * D_CONV + Q_LORA + KV_LORA + QK_ROPE + D_ATTN
EPS = 1e-6
ADAM_LR, ADAM_B1, ADAM_B2, ADAM_EPS, ADAM_WD, ADAM_STEP = 0.001, 0.9, 0.999, 1e-08, 0.01, 10

N_DEV = 8
LANES = 128
QK_PAD = 256
U_COLS = 6144
U_CQ, U_CKV, U_KR, U_ZA = 4096, 4608, 4864, 4928
U_TAIL = 2048
ZA_LO = U_ZA - (U_COLS - U_TAIL) - QK_ROPE
ZA_WIN = D_ATTN + LANES
CW = IN_COLS // 8
EXP_W = 896
W_LO = [(CW * d // 128) * 128 for d in range(8)]
W_OFF = [CW * d - lo for d, lo in enumerate(W_LO)]
SCALE = 1.0 / math.sqrt(QK_HEAD)
LOG2E = 1.4426950408889634
LN2 = 0.6931471805599453
NEG = -1e30
VMEM_LIMIT = 56 * 1024 * 1024

TM_ELEM = 256
NORM_ROWS = 16
NORM_GROUP = 4
TM_MM = 512
TQ = 1024
Q_CHAINS = 4
KV_SPLIT = 2

SM_MOD, SM_NG, SM_QAG, SM_KVAG, SM_QG, SM_KG, SM_CONV, SM_LOSS = 0, 6144, 8192, 8704, 8960, 9216, 9472, 12544
SM_COLS = 12672


def _params(sem=None):
    kw = dict(vmem_limit_bytes=VMEM_LIMIT)
    if sem is not None:
        kw["dimension_semantics"] = sem
    return pltpu.CompilerParams(**kw)


def _sigmoid(z):
    return 1.0 / (1.0 + jnp.exp(-z))


def _rot64(x):
    lane = lax.broadcasted_iota(jnp.int32, x.shape, 1)
    return jnp.where(lane < 32, pltpu.roll(x, 96, 1), pltpu.roll(x, 32, 1))


def _rope(x, cos, sin):
    return x * cos + _rot64(x) * sin


def _rope_t(d, cos, sin):
    return d * cos - _rot64(d) * sin


def _dot(a, b):
    return jnp.dot(a, b, preferred_element_type=F32)


def _dot_nt(a, b):
    return lax.dot_general(a, b, (((1,), (1,)), ((), ())), preferred_element_type=F32)


def _dot_tn(a, b):
    return lax.dot_general(a, b, (((0,), (0,)), ((), ())), preferred_element_type=F32)


def _my_index():
    return 4 * lax.axis_index("x") + 2 * lax.axis_index("y") + lax.axis_index("c")


ANY = pl.BlockSpec(memory_space=pl.ANY)


class _Gather:
    def __init__(self, blocks, relay=False, parts=1):
        self.relay = relay
        self.parts = parts
        self.rows = [b.shape[0] // parts for b in blocks]
        self.n = n = len(blocks) * parts
        self.out_shape = [jax.ShapeDtypeStruct((N_DEV,) + b.shape, b.dtype) for b in blocks]
        self.scratch = [pltpu.SemaphoreType.DMA((7 * n,)), pltpu.SemaphoreType.DMA((7 * n,)),
                        pltpu.SemaphoreType.DMA((n,))]

    @staticmethod
    def _places():
        x, y, c = lax.axis_index("x"), lax.axis_index("y"), lax.axis_index("c")
        return (x, y, c), (x, y, 1 - c), [(1 - x, y), (x, 1 - y), (1 - x, 1 - y)]

    def _src(self, ins, a):
        block, part = divmod(a, self.parts)
        return ins[block] if self.parts == 1 else ins[block].at[pl.ds(part * self.rows[block], self.rows[block])]

    def _dst(self, outs, a, place):
        block, part = divmod(a, self.parts)
        ref = outs[block].at[4 * place[0] + 2 * place[1] + place[2]]
        return ref if self.parts == 1 else ref.at[pl.ds(part * self.rows[block], self.rows[block])]

    def _copy(self, outs, sems, a, k, block, to, src=None):
        dst = self._dst(outs, a, block)
        return pltpu.make_async_remote_copy(
            src_ref=dst if src is None else src, dst_ref=dst, send_sem=sems[0].at[7 * a + k],
            recv_sem=sems[1].at[7 * a + k], device_id=to, device_id_type=MESH)

    def _first(self, ins, outs, sems):
        me, sibling, chips = self._places()
        first = []
        for a in range(self.n):
            first.append(self._copy(outs, sems, a, 0, me, sibling, src=self._src(ins, a)))
            first += [self._copy(outs, sems, a, 1 + j, me, (*chip, me[2]), src=self._src(ins, a))
                      for j, chip in enumerate(chips[:2] if self.relay else chips)]
        return first

    def _relays(self, outs, sems):
        if not self.relay:
            return []
        (x, y, c), _, _ = self._places()
        via = (jnp.where(c == 0, 1 - x, x), jnp.where(c == 0, y, 1 - y))
        to = (jnp.where(c == 0, x, 1 - x), jnp.where(c == 0, 1 - y, y))
        return [self._copy(outs, sems, a, 3, (*via, c), (*to, c)) for a in range(self.n)]

    def _passed(self, outs, sems):
        me, sibling, chips = self._places()
        return [self._copy(outs, sems, a, 4 + j, (*chip, me[2]), sibling)
                for a in range(self.n) for j, chip in enumerate(chips)]

    def _mine(self, ins, outs, sems):
        me, _, _ = self._places()
        return [pltpu.make_async_copy(self._src(ins, a), self._dst(outs, a, me), sems[2].at[a]) for a in range(self.n)]

    def start(self, ins, outs, sems):
        for cp in self._mine(ins, outs, sems) + self._first(ins, outs, sems):
            cp.start()

    def forward(self, ins, outs, sems):
        del ins
        me, _, chips = self._places()
        passed, relays = self._passed(outs, sems), self._relays(outs, sems)
        for a in range(self.n):
            for j, chip in enumerate(chips[:2] if self.relay else chips):
                self._copy(outs, sems, a, 1 + j, (*chip, me[2]), me).wait_recv()
                passed[3 * a + j].start()
            if self.relay:
                relays[a].start()
        if self.relay:
            for a in range(self.n):
                self._copy(outs, sems, a, 3, (*chips[2], me[2]), me).wait_recv()
                passed[3 * a + 2].start()

    def finish(self, ins, outs, sems):
        me, sibling, chips = self._places()
        for a in range(self.n):
            self._copy(outs, sems, a, 0, sibling, me).wait_recv()
            for j, chip in enumerate(chips):
                self._copy(outs, sems, a, 4 + j, (*chip, 1 - me[2]), me).wait_recv()
        for cp in self._first(ins, outs, sems) + self._relays(outs, sems) + self._passed(outs, sems):
            cp.wait_send()
        for cp in self._mine(ins, outs, sems):
            cp.wait()


class _GatherDirect:
    FLIPS = [(0, 0, 1), (1, 0, 0), (0, 1, 0), (1, 1, 0), (1, 0, 1), (0, 1, 1), (1, 1, 1)]

    def __init__(self, blocks):
        self.n = n = len(blocks)
        self.out_shape = [jax.ShapeDtypeStruct((N_DEV,) + b.shape, b.dtype) for b in blocks]
        self.scratch = [pltpu.SemaphoreType.DMA((7 * n,)), pltpu.SemaphoreType.DMA((7 * n,)),
                        pltpu.SemaphoreType.DMA((n,))]

    def _copies(self, ins, outs, sems):
        x, y, c = lax.axis_index("x"), lax.axis_index("y"), lax.axis_index("c")
        mine = 4 * x + 2 * y + c
        remote = [pltpu.make_async_remote_copy(
            src_ref=ins[a], dst_ref=outs[a].at[mine], send_sem=sems[0].at[7 * a + k], recv_sem=sems[1].at[7 * a + k],
            device_id=(1 - x if fx else x, 1 - y if fy else y, 1 - c if fc else c), device_id_type=MESH)
            for a in range(self.n) for k, (fx, fy, fc) in enumerate(self.FLIPS)]
        local = [pltpu.make_async_copy(ins[a], outs[a].at[mine], sems[2].at[a]) for a in range(self.n)]
        return remote + local

    def start(self, ins, outs, sems):
        for cp in self._copies(ins, outs, sems):
            cp.start()

    def forward(self, ins, outs, sems):
        pass

    def finish(self, ins, outs, sems):
        for cp in self._copies(ins, outs, sems):
            cp.wait()


class _ChipExchange:
    def __init__(self, arrays):
        self.n = n = len(arrays)
        self.out_shape = [jax.ShapeDtypeStruct(a.shape, a.dtype) for a in arrays]
        self.scratch = [pltpu.SemaphoreType.DMA((3 * n,)), pltpu.SemaphoreType.DMA((3 * n,))]

    def _copies(self, ins, outs, sems):
        x, y, c = lax.axis_index("x"), lax.axis_index("y"), lax.axis_index("c")
        return [pltpu.make_async_remote_copy(
            src_ref=ins[a].at[2 * px + py], dst_ref=outs[a].at[2 * x + y], send_sem=sems[0].at[3 * a + j],
            recv_sem=sems[1].at[3 * a + j], device_id=(px, py, c), device_id_type=MESH)
            for a in range(self.n) for j, (px, py) in enumerate([(1 - x, y), (x, 1 - y), (1 - x, 1 - y)])]

    def start(self, ins, outs, sems):
        for cp in self._copies(ins, outs, sems):
            cp.start()

    def forward(self, ins, outs, sems):
        pass

    def finish(self, ins, outs, sems):
        for cp in self._copies(ins, outs, sems):
            cp.wait()


class _SiblingExchange:
    def __init__(self, arrays, windowed):
        self.n = n = len(arrays)
        self.windowed = windowed
        self.out_shape = [jax.ShapeDtypeStruct((4, a.shape[0], EXP_W) if w else (4,) + a.shape[1:], a.dtype)
                          for a, w in zip(arrays, windowed)]
        self.scratch = [pltpu.SemaphoreType.DMA((4 * n,)), pltpu.SemaphoreType.DMA((4 * n,))]

    def _each(self, ins, outs, sems, act):
        x, y, c = lax.axis_index("x"), lax.axis_index("y"), lax.axis_index("c")

        def branch(c_val):
            for k in range(4):
                e = 2 * k + (1 - c_val)
                for a in range(self.n):
                    src = ins[a].at[:, pl.ds(W_LO[e], EXP_W)] if self.windowed[a] else ins[a].at[e]
                    act(pltpu.make_async_remote_copy(
                        src_ref=src, dst_ref=outs[a].at[k], send_sem=sems[0].at[4 * a + k], recv_sem=sems[1].at[4 * a + k],
                        device_id=(x, y, 1 - c), device_id_type=MESH))

        for c_val in (0, 1):
            pl.when(c == c_val)(functools.partial(branch, c_val))

    def start(self, ins, outs, sems):
        self._each(ins, outs, sems, lambda cp: cp.start())

    def forward(self, ins, outs, sems):
        pass

    def finish(self, ins, outs, sems):
        self._each(ins, outs, sems, lambda cp: cp.wait())


def _exchange(rider, arrays, name):
    n = len(arrays)

    def body(*refs):
        ins, outs, sems = refs[:n], refs[n:n + len(rider.out_shape)], refs[n + len(rider.out_shape):]
        rider.start(ins, outs, sems)
        rider.forward(ins, outs, sems)
        rider.finish(ins, outs, sems)

    return pl.pallas_call(body, name=name, out_shape=rider.out_shape, in_specs=[ANY] * n,
                          out_specs=[ANY] * len(rider.out_shape), scratch_shapes=rider.scratch)(*arrays)


def _add_window(dw_in, recv, lo_tiles):
    k, rows, _ = recv.shape
    tiles = EXP_W // LANES

    def body(t_ref, *refs):
        del t_ref
        r_ref, o_ref = refs[tiles], refs[tiles + 1]
        own = jnp.concatenate([w_ref[...] for w_ref in refs[:tiles]], axis=1)
        o_ref[0] = (own.astype(F32) + r_ref[0].astype(F32)).astype(o_ref.dtype)

    def tile(j):
        return pl.BlockSpec((rows, LANES), lambda i, t: (0, t[i] + j))

    spec = pl.BlockSpec((1, rows, EXP_W), lambda i, t: (i, 0, 0))
    grid_spec = pltpu.PrefetchScalarGridSpec(
        num_scalar_prefetch=1, grid=(k,), in_specs=[tile(j) for j in range(tiles)] + [spec], out_specs=spec)
    return pl.pallas_call(
        body, name="rs_add_in", grid_spec=grid_spec, out_shape=jax.ShapeDtypeStruct(recv.shape, recv.dtype),
        compiler_params=_params(("parallel",)),
    )(lo_tiles, *([dw_in] * tiles), recv)


def _final_sum(p, r, sel, name, unshift=False, keep_t=None):
    _, rows, cols = p.shape
    tr = 512 if rows % 512 == 0 else rows

    def body(sel_ref, p_ref, r0, r1, r2, r3, o_ref):
        own = p_ref[0].astype(F32)
        acc = None
        for k, r_ref in enumerate((r0, r1, r2, r3)):
            term = jnp.where(sel_ref[0] == k, own, r_ref[0].astype(F32))
            acc = term if acc is None else acc + term
        if unshift:
            acc = pltpu.roll(acc, sel_ref[5], 1)
        o_ref[...] = acc if keep_t is None else acc.T[:keep_t]

    def slot(k):
        return pl.BlockSpec((1, tr, cols), lambda i, t: (t[k], i, 0))

    if keep_t is None:
        out_spec, out_shape = pl.BlockSpec((tr, cols), lambda i, t: (i, 0)), (rows, cols)
    else:
        out_spec, out_shape = pl.BlockSpec((keep_t, tr), lambda i, t: (0, i)), (keep_t, rows)
    grid_spec = pltpu.PrefetchScalarGridSpec(
        num_scalar_prefetch=1, grid=(rows // tr,), in_specs=[slot(0), slot(1), slot(2), slot(3), slot(4)],
        out_specs=out_spec)
    return pl.pallas_call(
        body, name=name, grid_spec=grid_spec, out_shape=jax.ShapeDtypeStruct(out_shape, F32),
        compiler_params=_params(("parallel",)),
    )(sel, p, r, r, r, r)


def _expand_w_in(w_t, shift):
    cw, rows = w_t.shape
    tr = TM_MM
    pad = -cw % LANES

    def body(s_ref, w_ref, o_ref):
        w = jnp.concatenate([w_ref[...], jnp.zeros((pad, tr), F32)], axis=0).T
        w = jnp.concatenate([w, jnp.zeros((tr, EXP_W - cw - pad), F32)], axis=1)
        o_ref[...] = pltpu.roll(w, s_ref[0], 1).astype(BF16)

    grid_spec = pltpu.PrefetchScalarGridSpec(
        num_scalar_prefetch=1, grid=(rows // tr,), in_specs=[pl.BlockSpec((cw, tr), lambda i, t: (0, i))],
        out_specs=pl.BlockSpec((tr, EXP_W), lambda i, t: (i, 0)))
    return pl.pallas_call(
        body, name="expand_w_in", grid_spec=grid_spec, out_shape=jax.ShapeDtypeStruct((rows, EXP_W), BF16),
        compiler_params=_params(("arbitrary",)),
    )(shift, w_t)


def _pad_wq(w_t):
    cw, rows = w_t.shape

    def body(w_ref, o_ref):
        o_ref[...] = jnp.concatenate([w_ref[...], jnp.zeros((QK_PAD - cw, rows), F32)], axis=0).T.astype(BF16)

    return pl.pallas_call(
        body, name="pad_wq", out_shape=jax.ShapeDtypeStruct((rows, QK_PAD), BF16), compiler_params=_params(),
    )(w_t)


def _merge_w_in(e):
    _, rows, _ = e.shape
    tr = TM_MM

    def body(e_ref, o_ref):
        for t in range(U_COLS // LANES):
            lo, hi = t * LANES, (t + 1) * LANES
            parts = [e_ref[d, :, lo - W_LO[d]:hi - W_LO[d]] for d in range(N_DEV)
                     if CW * d < hi and CW * (d + 1) > lo]
            if not parts:
                tile = jnp.zeros((tr, LANES), BF16)
            elif len(parts) == 1:
                tile = parts[0]
            else:
                tile = (parts[0].astype(F32) + parts[1].astype(F32)).astype(BF16)
            o_ref[:, lo:hi] = tile

    return pl.pallas_call(
        body, name="merge_w_in", grid=(rows // tr,),
        in_specs=[pl.BlockSpec((N_DEV, tr, EXP_W), lambda i: (0, i, 0))],
        out_specs=pl.BlockSpec((tr, U_COLS), lambda i: (i, 0)), out_shape=jax.ShapeDtypeStruct((rows, U_COLS), BF16),
        compiler_params=_params(("parallel",)),
    )(e)


def _sum_leading(a, out_dtype, name):
    k, rows, cols = a.shape
    tr = min(rows, 1728 if rows % 1728 == 0 else rows)

    def body(a_ref, o_ref):
        acc = a_ref[0].astype(F32)
        for i in range(1, k):
            acc = acc + a_ref[i].astype(F32)
        o_ref[...] = acc.astype(out_dtype)

    return pl.pallas_call(
        body, name=name, grid=(rows // tr,),
        in_specs=[pl.BlockSpec((k, tr, cols), lambda i: (0, i, 0))],
        out_specs=pl.BlockSpec((tr, cols), lambda i: (i, 0)),
        out_shape=jax.ShapeDtypeStruct((rows, cols), out_dtype), compiler_params=_params(("parallel",)),
    )(a)


def _add_pairs(g, recv, core, name):
    k, rows, cols = recv.shape
    tr = 1728 if rows % 1728 == 0 else rows

    def body(c_ref, g_ref, r_ref, o_ref):
        del c_ref
        o_ref[...] = (g_ref[...].astype(F32) + r_ref[...].astype(F32)).astype(o_ref.dtype)

    spec = pl.BlockSpec((1, tr, cols), lambda i, j, c: (i, j, 0))
    grid_spec = pltpu.PrefetchScalarGridSpec(
        num_scalar_prefetch=1, grid=(k, rows // tr),
        in_specs=[pl.BlockSpec((1, tr, cols), lambda i, j, c: (2 * i + c[0], j, 0)), spec], out_specs=spec)
    return pl.pallas_call(
        body, name=name, grid_spec=grid_spec, out_shape=jax.ShapeDtypeStruct(recv.shape, recv.dtype),
        compiler_params=_params(("parallel", "parallel")),
    )(core, g, recv)


def _ada_mod(c16, ada_w_l, ada_b_l):
    def body(c_ref, w_ref, b_ref, o_ref):
        cv = c_ref[...]
        sc = (cv * _sigmoid(cv)).astype(BF16)
        o_ref[...] = _dot(sc, w_ref[...].astype(BF16)) + b_ref[...]

    return pl.pallas_call(
        body, name="ada_mod", out_shape=jax.ShapeDtypeStruct((c16.shape[0], ada_w_l.shape[1]), F32),
        compiler_params=_params(),
    )(c16, ada_w_l, ada_b_l)


def _ada_w_grad(c_t, dmod_my):
    def body(c_ref, d_ref, o_ref):
        cv = c_ref[...]
        sc = cv * _sigmoid(cv)
        acc = sc[:, 0:1] * d_ref[0:1, :]
        for b in range(1, N_DEV):
            acc = acc + sc[:, b:b + 1] * d_ref[b:b + 1, :]
        o_ref[...] = acc

    return pl.pallas_call(
        body, name="ada_w_grad", out_shape=jax.ShapeDtypeStruct((c_t.shape[0], dmod_my.shape[1]), F32),
        compiler_params=_params(),
    )(c_t, dmod_my)


def _norm_mod(x, norm_g, mod, pos_col, invf, sign, rider, rider_inputs):
    s, d = x.shape
    tm = min(TM_MM, s)
    n_in, n_out = len(rider_inputs), len(rider.out_shape)
    steps = s // tm

    def body(x_ref, g_ref, mod_ref, p_ref, f_ref, s_ref, *rest):
        r_ins, (h_ref, ht_ref, cos_ref, sin_ref) = rest[:n_in], rest[n_in:n_in + 4]
        r_outs, sems = rest[n_in + 4:n_in + 4 + n_out], rest[n_in + 4 + n_out:]
        pl.when(pl.program_id(0) == 0)(functools.partial(rider.start, r_ins, r_outs, sems))
        xv = x_ref[...]
        r = lax.rsqrt(jnp.mean(xv * xv, axis=-1, keepdims=True) + EPS)
        hn = xv * r * g_ref[...]
        hv = hn * (1.0 + mod_ref[:, d:2 * d]) + mod_ref[:, 0:d]
        h_ref[...] = hv.astype(BF16)
        ht_ref[...] = hv.T.astype(BF16)
        ang = p_ref[...].astype(F32) * f_ref[...]
        sg = s_ref[...]
        cos_ref[...] = jnp.cos(ang) * jnp.abs(sg)
        sin_ref[...] = jnp.sin(ang) * sg

        @pl.when(pl.program_id(0) == steps - 1)
        def _():
            rider.forward(r_ins, r_outs, sems)
            rider.finish(r_ins, r_outs, sems)

    row = pl.BlockSpec((1, LANES), lambda i: (0, 0))
    tab = pl.BlockSpec((tm, LANES), lambda i: (i, 0))
    return pl.pallas_call(
        body, name="norm_mod", grid=(steps,),
        in_specs=[pl.BlockSpec((tm, d), lambda i: (i, 0)), pl.BlockSpec((1, d), lambda i: (0, 0)),
                  pl.BlockSpec((1, 3 * d), lambda i: (0, 0)), pl.BlockSpec((tm, 1), lambda i: (i, 0)), row, row]
        + [ANY] * n_in,
        out_specs=[pl.BlockSpec((tm, d), lambda i: (i, 0)), pl.BlockSpec((d, tm), lambda i: (0, i)), tab, tab] + [ANY] * n_out,
        out_shape=[jax.ShapeDtypeStruct((s, d), BF16), jax.ShapeDtypeStruct((d, s), BF16),
                   jax.ShapeDtypeStruct((s, LANES), F32), jax.ShapeDtypeStruct((s, LANES), F32)] + rider.out_shape,
        scratch_shapes=rider.scratch, compiler_params=_params(("arbitrary",)),
    )(x, norm_g, mod, pos_col, invf, sign, *rider_inputs)


def _matmul(a, b, *, nt, out_dtype, tm, tn, name, rider=None, rider_inputs=(), a_resident=False):
    m, kdim = a.shape
    n = b.shape[0] if nt else b.shape[1]
    tm, tn = min(tm, m), min(tn, n)
    n_in = len(rider_inputs)
    n_out = len(rider.out_shape) if rider else 0
    m_steps, n_steps = m // tm, n // tn
    steps = n_steps * m_steps
    inner = n_steps if a_resident else m_steps
    tile = (lambda o, i: (o, i)) if a_resident else (lambda o, i: (i, o))

    def body(a_ref, b_ref, *rest):
        r_ins, o_ref, r_outs, sems = rest[:n_in], rest[n_in], rest[n_in + 1:n_in + 1 + n_out], rest[n_in + 1 + n_out:]
        step = pl.program_id(0) * inner + pl.program_id(1)
        if rider:
            pl.when(step == 0)(functools.partial(rider.start, r_ins, r_outs, sems))
            pl.when(step == steps // 2)(functools.partial(rider.forward, r_ins, r_outs, sems))
        o = _dot_nt(a_ref[...], b_ref[...]) if nt else _dot(a_ref[...], b_ref[...])
        o_ref[...] = o.astype(out_dtype)
        if rider:
            pl.when(step == steps - 1)(functools.partial(rider.finish, r_ins, r_outs, sems))

    if nt:
        b_spec = pl.BlockSpec((tn, kdim), lambda o, i: (tile(o, i)[1], 0))
    else:
        b_spec = pl.BlockSpec((kdim, tn), lambda o, i: (0, tile(o, i)[1]))
    out = pl.pallas_call(
        body, name=name, grid=(m_steps, n_steps) if a_resident else (n_steps, m_steps),
        in_specs=[pl.BlockSpec((tm, kdim), lambda o, i: (tile(o, i)[0], 0)), b_spec] + [ANY] * n_in,
        out_specs=[pl.BlockSpec((tm, tn), tile)] + [ANY] * n_out,
        out_shape=[jax.ShapeDtypeStruct((m, n), out_dtype)] + (rider.out_shape if rider else []),
        scratch_shapes=rider.scratch if rider else [],
        compiler_params=_params(("arbitrary", "arbitrary") if rider else ("parallel", "parallel")),
    )(a, b, *rider_inputs)
    return out if rider else out[0]


HALO = 16


def _conv_specs(tm):
    def col(j):
        return pl.BlockSpec((tm, D_CONV), lambda i: (i, j))

    def prev(j):
        return pl.BlockSpec((HALO, D_CONV), lambda i: (jnp.maximum(i * (tm // HALO) - 1, 0), j))

    return [col(0), col(1), col(2), col(3), prev(0), prev(2)]


def _conv_y(xc_ref, bc_ref, cc_ref, zc_ref, xp_ref, cp_ref, w_ref, first):
    uc = cc_ref[...].astype(F32) * xc_ref[...].astype(F32)
    up = jnp.where(first, 0.0, cp_ref[...].astype(F32) * xp_ref[...].astype(F32))
    full = jnp.concatenate([up, uc], axis=0)
    u1 = pltpu.roll(full, 1, 0)[HALO:]
    u2 = pltpu.roll(full, 2, 0)[HALO:]
    w = w_ref[...]
    conv = w[0:1] * u2 + w[1:2] * u1 + w[2:3] * uc
    z = zc_ref[...].astype(F32)
    return bc_ref[...].astype(F32) * conv * (z * _sigmoid(z))


def _conv_bwd(u, dyc, conv_w):
    s = u.shape[0]
    tm = min(TM_MM, s)
    cb = D_CONV
    nt = s // tm

    def body(xc_ref, bc_ref, cc_ref, zc_ref, xp_ref, cp_ref, bn_ref, zn_ref, dy_ref, dyn_ref, w_ref, du_ref, dw_ref):
        i = pl.program_id(0)
        xc, cc = xc_ref[...].astype(F32), cc_ref[...].astype(F32)
        bc, z = bc_ref[...].astype(F32), zc_ref[...].astype(F32)
        uc = cc * xc
        up = jnp.where(i == 0, 0.0, cp_ref[...].astype(F32) * xp_ref[...].astype(F32))
        full = jnp.concatenate([up, uc], axis=0)
        u1 = pltpu.roll(full, 1, 0)[HALO:]
        u2 = pltpu.roll(full, 2, 0)[HALO:]
        w = w_ref[...]
        conv = w[0:1] * u2 + w[1:2] * u1 + w[2:3] * uc
        sg = _sigmoid(z)
        sz = z * sg
        dy = dy_ref[...].astype(F32)
        dconv = dy * bc * sz
        zn = zn_ref[...].astype(F32)
        dnext = dyn_ref[...].astype(F32) * bn_ref[...].astype(F32) * (zn * _sigmoid(zn))
        dnext = jnp.where(i == nt - 1, 0.0, dnext)
        fullb = jnp.concatenate([dconv, dnext], axis=0)
        nb = tm + HALO
        d1 = pltpu.roll(fullb, nb - 1, 0)[:tm]
        d2 = pltpu.roll(fullb, nb - 2, 0)[:tm]
        duc = w[2:3] * dconv + w[1:2] * d1 + w[0:1] * d2
        dzc = dy * bc * conv * (sg * (1.0 + z * (1.0 - sg)))
        du_ref[...] = jnp.concatenate([duc * cc, dy * conv * sz, duc * xc, dzc], axis=1).astype(BF16)
        dw = jnp.concatenate([jnp.sum(dconv * u2, axis=0, keepdims=True), jnp.sum(dconv * u1, axis=0, keepdims=True),
                              jnp.sum(dconv * uc, axis=0, keepdims=True), jnp.zeros((5, cb), F32)], axis=0)

        @pl.when(i == 0)
        def _():
            dw_ref[...] = dw

        @pl.when(i > 0)
        def _():
            dw_ref[...] += dw

    def col(j):
        return pl.BlockSpec((tm, cb), lambda i: (i, j))

    def prev(j):
        return pl.BlockSpec((HALO, cb), lambda i: (jnp.maximum(i * (tm // HALO) - 1, 0), j))

    def nxt(j):
        return pl.BlockSpec((HALO, cb), lambda i: (jnp.minimum((i + 1) * (tm // HALO), s // HALO - 1), j))

    return pl.pallas_call(
        body, name="conv_bwd", grid=(nt,),
        in_specs=[col(0), col(1), col(2), col(3), prev(0), prev(2), nxt(1), nxt(3), col(0), nxt(0),
                  pl.BlockSpec((3, cb), lambda i: (0, 0))],
        out_specs=[pl.BlockSpec((tm, 4 * cb), lambda i: (i, 0)), pl.BlockSpec((8, cb), lambda i: (0, 0))],
        out_shape=[jax.ShapeDtypeStruct((s, U_COLS), BF16), jax.ShapeDtypeStruct((8, cb), F32)],
        compiler_params=_params(("arbitrary",)),
    )(u, u, u, u, u, u, u, u, dyc, dyc, conv_w)


def _qkv_specs(tm):
    return [pl.BlockSpec((tm, Q_LORA), lambda i: (i, U_CQ // Q_LORA)),
            pl.BlockSpec((tm, KV_LORA), lambda i: (i, U_CKV // KV_LORA)),
            pl.BlockSpec((tm, LANES), lambda i: (i, U_KR // LANES)),
            pl.BlockSpec((tm, LANES), lambda i: (i, 0)), pl.BlockSpec((tm, LANES), lambda i: (i, 0))]


def _full(shape):
    return pl.BlockSpec(shape, lambda i: (0,) * len(shape))


def _k_rope_lanes(blk):
    lane = lax.broadcasted_iota(jnp.int32, blk.shape, 1)
    return jnp.where(lane < QK_ROPE, blk, 0.0)


def _qkv_fwd(u, cos, sin, wq, wkv, qag, kvag, qg, kg):
    s = u.shape[0]
    tm = min(TM_MM, s)

    def body(cq_ref, ckv_ref, kr_ref, cos_ref, sin_ref, wq_ref, wkv_ref, qag_ref, kvag_ref, qg_ref, kg_ref,
             q_ref, k_ref, v_ref):
        cq = cq_ref[...].astype(F32)
        cqn = (cq * lax.rsqrt(jnp.mean(cq * cq, axis=-1, keepdims=True) + EPS) * qag_ref[...]).astype(BF16)
        ckv = ckv_ref[...].astype(F32)
        ckvn = (ckv * lax.rsqrt(jnp.mean(ckv * ckv, axis=-1, keepdims=True) + EPS) * kvag_ref[...]).astype(BF16)
        kr = _k_rope_lanes(kr_ref[...].astype(F32))
        cosv, sinv, qgv, kgv = cos_ref[...], sin_ref[...], qg_ref[...], kg_ref[...]
        ss_r = jnp.sum(kr * kr, axis=-1, keepdims=True)
        krr = _rope(kr * kgv[:, QK_NOPE:], cosv, sinv)
        qf = _dot(cqn, wq_ref[...])
        kvf = _dot(ckvn, wkv_ref[...])
        heads = range(N_HEADS)
        qh = [qf[:, QK_PAD * h:QK_PAD * (h + 1)] for h in heads]
        kn = [kvf[:, 2 * V_HEAD * h:2 * V_HEAD * h + QK_NOPE] for h in heads]
        rq = [lax.rsqrt(jnp.sum(qh[h] * qh[h], axis=-1, keepdims=True) * (1.0 / QK_HEAD) + EPS) for h in heads]
        rk = [lax.rsqrt((jnp.sum(kn[h] * kn[h], axis=-1, keepdims=True) + ss_r) * (1.0 / QK_HEAD) + EPS) for h in heads]
        for h in heads:
            qn = qh[h] * rq[h] * qgv
            qo = jnp.concatenate([qn[:, :QK_NOPE], _rope(qn[:, QK_NOPE:], cosv, sinv)], axis=1) * (SCALE * LOG2E)
            q_ref[h] = qo.astype(BF16)
            vh = kvf[:, 2 * V_HEAD * h + QK_NOPE:2 * V_HEAD * (h + 1)]
            k_ref[h] = jnp.concatenate([kn[h] * kgv[:, :QK_NOPE] * rk[h], krr * rk[h]], axis=1).astype(BF16)
            v_ref[h] = jnp.concatenate([vh, jnp.ones_like(vh)], axis=1).astype(BF16)

    return pl.pallas_call(
        body, name="qkv_fwd", grid=(s // tm,),
        in_specs=_qkv_specs(tm) + [_full((Q_LORA, N_HEADS * QK_PAD)), _full((KV_LORA, 2 * D_ATTN)),
                                   _full((1, Q_LORA)), _full((1, KV_LORA)), _full((1, QK_PAD)), _full((1, QK_PAD))],
        out_specs=[pl.BlockSpec((N_HEADS, tm, QK_PAD), lambda i: (0, i, 0)),
                   pl.BlockSpec((N_HEADS, tm, QK_PAD), lambda i: (0, i, 0)),
                   pl.BlockSpec((N_HEADS, tm, 2 * V_HEAD), lambda i: (0, i, 0))],
        out_shape=[jax.ShapeDtypeStruct((N_HEADS, s, QK_PAD), BF16), jax.ShapeDtypeStruct((N_HEADS, s, QK_PAD), BF16),
                   jax.ShapeDtypeStruct((N_HEADS, s, 2 * V_HEAD), BF16)],
        compiler_params=_params(("parallel",)),
    )(u, u, u, cos, sin, wq, wkv, qag, kvag, qg, kg)


def _qkv_bwd(u, cos, sin, dq, dk, dv, dza, wq, wkv, qag, kvag, qg, kg, du):
    s = u.shape[0]
    tm = min(TM_ELEM, s)
    nt = s // tm

    def body(cq_ref, ckv_ref, kr_ref, cos_ref, sin_ref, dq_ref, dk_ref, dv_ref, dza_ref, wq_ref, wkv_ref, qag_ref,
             kvag_ref, qg_ref, kg_ref, du_in, du_ref, dwq_ref, dwkv_ref, dqag_ref, dkvag_ref, dqg_ref, dkg_ref,
             dwq_acc, dwkv_acc):
        del du_in
        i = pl.program_id(0)

        @pl.when(i == 0)
        def _():
            dwq_acc[...] = jnp.zeros_like(dwq_acc)
            dwkv_acc[...] = jnp.zeros_like(dwkv_acc)

        cq = cq_ref[...].astype(F32)
        rqa = lax.rsqrt(jnp.mean(cq * cq, axis=-1, keepdims=True) + EPS)
        xq = cq * rqa
        qagv = qag_ref[...]
        cqn = (xq * qagv).astype(BF16)
        ckv = ckv_ref[...].astype(F32)
        rkva = lax.rsqrt(jnp.mean(ckv * ckv, axis=-1, keepdims=True) + EPS)
        xkv = ckv * rkva
        kvagv = kvag_ref[...]
        ckvn = (xkv * kvagv).astype(BF16)
        kr = _k_rope_lanes(kr_ref[...].astype(F32))
        cosv, sinv, qgv, kgv = cos_ref[...], sin_ref[...], qg_ref[...], kg_ref[...]
        ss_r = jnp.sum(kr * kr, axis=-1, keepdims=True)
        dqg = jnp.zeros((1, QK_PAD), F32)
        dkg = jnp.zeros((1, QK_PAD), F32)
        dkr = jnp.zeros((tm, LANES), F32)
        qf = _dot(cqn, wq_ref[...])
        kvf = _dot(ckvn, wkv_ref[...])
        heads = range(N_HEADS)
        qh = [qf[:, QK_PAD * h:QK_PAD * (h + 1)] for h in heads]
        kn = [kvf[:, 2 * V_HEAD * h:2 * V_HEAD * h + QK_NOPE] for h in heads]
        rq = [lax.rsqrt(jnp.sum(qh[h] * qh[h], axis=-1, keepdims=True) * (1.0 / QK_HEAD) + EPS) for h in heads]
        rk = [lax.rsqrt((jnp.sum(kn[h] * kn[h], axis=-1, keepdims=True) + ss_r) * (1.0 / QK_HEAD) + EPS) for h in heads]
        xh = [qh[h] * rq[h] for h in heads]
        xk = [jnp.concatenate([kn[h], kr], axis=1) * rk[h] for h in heads]
        dyq, dyk = [], []
        for h in heads:
            g = dq_ref[h].astype(F32)
            dyq.append(jnp.concatenate([g[:, :QK_NOPE], _rope_t(g[:, QK_NOPE:], cosv, sinv)], axis=1))
            gk = dk_ref[h].astype(F32)
            dyk.append(jnp.concatenate([gk[:, :QK_NOPE], _rope_t(gk[:, QK_NOPE:], cosv, sinv)], axis=1))
        for h in heads:
            dqg = dqg + jnp.sum(dyq[h] * xh[h], axis=0, keepdims=True)
            dkg = dkg + jnp.sum(dyk[h] * xk[h], axis=0, keepdims=True)
        dqg = dqg * SCALE
        qgv = qgv * SCALE
        gdy = [dyq[h] * qgv for h in heads]
        gdyk = [dyk[h] * kgv for h in heads]
        tq_ = [jnp.sum(gdy[h] * xh[h], axis=-1, keepdims=True) * (1.0 / QK_HEAD) for h in heads]
        tk_ = [jnp.sum(gdyk[h] * xk[h], axis=-1, keepdims=True) * (1.0 / QK_HEAD) for h in heads]
        dqf = [(rq[h] * (gdy[h] - xh[h] * tq_[h])).astype(BF16) for h in heads]
        dkvf = []
        for h in heads:
            dxk = rk[h] * (gdyk[h] - xk[h] * tk_[h])
            dkr = dkr + dxk[:, QK_NOPE:]
            dkvf += [dxk[:, :QK_NOPE].astype(BF16), dv_ref[h]]
        dqf_b, dkvf_b = jnp.concatenate(dqf, axis=1), jnp.concatenate(dkvf, axis=1)
        dwq_acc[...] += _dot_tn(cqn, dqf_b)
        dwkv_acc[...] += _dot_tn(ckvn, dkvf_b)
        dcqn = _dot_nt(dqf_b, wq_ref[...])
        dckvn = _dot_nt(dkvf_b, wkv_ref[...])
        dqag = jnp.sum(dcqn * xq, axis=0, keepdims=True)
        dkvag = jnp.sum(dckvn * xkv, axis=0, keepdims=True)
        gq = dcqn * qagv
        dcq = rqa * (gq - xq * jnp.mean(gq * xq, axis=-1, keepdims=True))
        gkv = dckvn * kvagv
        dckv = rkva * (gkv - xkv * jnp.mean(gkv * xkv, axis=-1, keepdims=True))
        win = pltpu.roll(jnp.concatenate([dza_ref[...].astype(F32), jnp.zeros((tm, LANES), F32)], axis=1), QK_ROPE, 1)
        win = win + jnp.concatenate([dkr, jnp.zeros((tm, D_ATTN), F32)], axis=1)
        du_ref[...] = jnp.concatenate([dcq, dckv, win, jnp.zeros((tm, U_TAIL - ZA_LO - ZA_WIN), F32)], axis=1).astype(BF16)

        @pl.when(i == 0)
        def _():
            dqag_ref[...] = dqag
            dkvag_ref[...] = dkvag
            dqg_ref[...] = dqg
            dkg_ref[...] = dkg

        @pl.when(i > 0)
        def _():
            dqag_ref[...] += dqag
            dkvag_ref[...] += dkvag
            dqg_ref[...] += dqg
            dkg_ref[...] += dkg

        @pl.when(i == nt - 1)
        def _():
            dwq_ref[...] = dwq_acc[...].astype(BF16)
            dwkv_ref[...] = dwkv_acc[...].astype(BF16)

    head = lambda w: pl.BlockSpec((N_HEADS, tm, w), lambda i: (0, i, 0))
    wq_shape, wkv_shape = (Q_LORA, N_HEADS * QK_PAD), (KV_LORA, 2 * D_ATTN)
    return pl.pallas_call(
        body, name="qkv_bwd", grid=(nt,),
        in_specs=_qkv_specs(tm) + [head(QK_PAD), head(QK_PAD), head(V_HEAD), pl.BlockSpec((tm, D_ATTN), lambda i: (i, 0)),
                                   _full(wq_shape), _full(wkv_shape), _full((1, Q_LORA)), _full((1, KV_LORA)),
                                   _full((1, QK_PAD)), _full((1, QK_PAD)), ANY],
        out_specs=[pl.BlockSpec((tm, U_TAIL), lambda i: (i, U_COLS // U_TAIL - 1)), _full(wq_shape), _full(wkv_shape),
                   _full((1, Q_LORA)), _full((1, KV_LORA)), _full((1, QK_PAD)), _full((1, QK_PAD))],
        out_shape=[jax.ShapeDtypeStruct(du.shape, du.dtype), jax.ShapeDtypeStruct(wq_shape, BF16),
                   jax.ShapeDtypeStruct(wkv_shape, BF16), jax.ShapeDtypeStruct((1, Q_LORA), F32),
                   jax.ShapeDtypeStruct((1, KV_LORA), F32), jax.ShapeDtypeStruct((1, QK_PAD), F32),
                   jax.ShapeDtypeStruct((1, QK_PAD), F32)],
        scratch_shapes=[pltpu.VMEM(wq_shape, F32), pltpu.VMEM(wkv_shape, F32)],
        input_output_aliases={15: 0}, compiler_params=_params(("arbitrary",)),
    )(u, u, u, cos, sin, dq, dk, dv, dza, wq, wkv, qag, kvag, qg, kg, du)


def _flash_fwd(q, k, v):
    nh, s, _ = q.shape
    tq = min(TQ, s)
    nkv = KV_SPLIT
    tk = tq // nkv
    nq = s // tq
    nch = Q_CHAINS
    tc = tq // nch

    def body(q_ref, k_ref, v_ref, o_ref, lse_ref):
        i = pl.program_id(1)
        chains = [q_ref[0, r * tc:(r + 1) * tc, :] for r in range(nch)]

        def unit(r, j, carry, shift=None):
            m, acc = carry
            rows = pl.ds(pl.multiple_of(j * tk, tk), tk)
            sc = _dot_nt(chains[r], k_ref[0, rows, :])
            if shift is not None:
                qi = lax.broadcasted_iota(jnp.int32, sc.shape, 0)
                ki = lax.broadcasted_iota(jnp.int32, sc.shape, 1) + shift
                sc = jnp.where(ki <= qi, sc, NEG)
            m_new = jnp.maximum(m, jnp.max(sc, axis=-1, keepdims=True))
            p = jnp.exp2(sc - m_new).astype(BF16)
            return m_new, jnp.exp2(m - m_new) * acc + _dot(p, v_ref[0, rows, :])

        def trip(p, carry):
            for b in range(nkv):
                carry = tuple(unit(r, nkv * p + b, cr) for r, cr in enumerate(carry))
            return carry

        init = (jnp.full((tc, 1), NEG, F32), jnp.zeros((tc, 2 * V_HEAD), F32))
        carry = list(lax.fori_loop(0, i, trip, (init,) * nch))
        for b in range(nkv):
            for r in range(nch):
                shift = b * tk - r * tc
                if shift < tc:
                    carry[r] = unit(r, nkv * i + b, carry[r], None if shift + tk - 1 <= 0 else shift)
        for r, (m, acc) in enumerate(carry):
            l = acc[:, V_HEAD:]
            o_ref[r * tc:(r + 1) * tc, :] = (acc[:, :V_HEAD] / l).astype(BF16)
            lse = m + jnp.log(l[:, 0:1]) * LOG2E
            lse_ref[0, :, r * tc:(r + 1) * tc] = jnp.broadcast_to(lse, (tc, LANES)).T[0:1, :]

    return pl.pallas_call(
        body, name="flash_fwd", grid=(nh, nq),
        in_specs=[pl.BlockSpec((1, tq, QK_PAD), lambda h, i: (h, i, 0)),
                  pl.BlockSpec((1, s, QK_PAD), lambda h, i: (h, 0, 0)),
                  pl.BlockSpec((1, s, 2 * V_HEAD), lambda h, i: (h, 0, 0))],
        out_specs=[pl.BlockSpec((tq, V_HEAD), lambda h, i: (i, h)), pl.BlockSpec((1, 1, tq), lambda h, i: (h, 0, i))],
        out_shape=[jax.ShapeDtypeStruct((s, nh * V_HEAD), BF16), jax.ShapeDtypeStruct((nh, 1, s), F32)],
        compiler_params=_params(("parallel", "arbitrary")),
    )(q, k, v)


def _flash_bwd(q, k, v, do, lse, delta):
    nh, s, _ = q.shape
    tq = min(TQ, s)
    nq = s // tq
    kps = 2 if nq % 2 == 0 else 1
    ng = nq // kps

    def body(q_ref, k_ref, v_ref, do_ref, lse_ref, dl_ref, dq_ref, dk_ref, dv_ref, dq_acc):
        g = ng - 1 - pl.program_id(1)

        @pl.when(g == ng - 1)
        def _():
            dq_acc[...] = jnp.zeros_like(dq_acc)

        for sub in reversed(range(kps)):
            kv_block(q_ref, k_ref, v_ref, do_ref, lse_ref, dl_ref, dk_ref, dv_ref, dq_acc, g * kps + sub, sub)

        @pl.when(g == 0)
        def _():
            dq_ref[0] = dq_acc[...].astype(BF16)

    def kv_block(q_ref, k_ref, v_ref, do_ref, lse_ref, dl_ref, dk_ref, dv_ref, dq_acc, j, sub):
        own = slice(sub * tq, (sub + 1) * tq)
        kj, vj = k_ref[0, own, :], v_ref[0, own, :]

        def block(kk, vv, qq, dd, lse, dl, masked):
            st = _dot_nt(kk, qq)
            pt = jnp.exp2(st - lse)
            if masked:
                ki = lax.broadcasted_iota(jnp.int32, st.shape, 0)
                qx = lax.broadcasted_iota(jnp.int32, st.shape, 1)
                pt = jnp.where(ki <= qx, pt, 0.0)
            ddv = _dot(pt.astype(BF16), dd)
            dst = (pt * (_dot_nt(vv, dd) - dl)).astype(BF16)
            ddq = _dot_tn(dst, kk)
            return _dot(dst, qq), ddv, ddq

        def step(i, carry):
            dk, dv = carry
            rows = pl.ds(pl.multiple_of(i * tq, tq), tq)
            ddk, ddv, ddq = block(kj, vj, q_ref[0, rows, :], do_ref[rows, :], lse_ref[0, pl.ds(i, 1), :],
                                  dl_ref[0, pl.ds(i, 1), :], False)
            dq_acc[rows, :] += ddq
            return dk + ddk, dv + ddv

        th = tq // 2
        lse_j, dl_j = lse_ref[0, pl.ds(j, 1), :], dl_ref[0, pl.ds(j, 1), :]
        parts = []
        for kh, qh, masked in ((0, 0, True), (0, 1, False), (1, 1, True)):
            rows = pl.ds(pl.multiple_of(j * tq + qh * th, th), th)
            ks, qs = slice(kh * th, (kh + 1) * th), slice(qh * th, (qh + 1) * th)
            ddk, ddv, ddq = block(kj[ks], vj[ks], q_ref[0, rows, :], do_ref[rows, :], lse_j[:, qs], dl_j[:, qs], masked)
            dq_acc[rows, :] += ddq
            parts.append((ddk, ddv))
        carry = (jnp.concatenate([parts[0][0] + parts[1][0], parts[2][0]], axis=0),
                 jnp.concatenate([parts[0][1] + parts[1][1], parts[2][1]], axis=0))
        dk, dv = lax.fori_loop(j + 1, nq, step, carry)
        dk_ref[0, own, :] = (dk * LN2).astype(BF16)
        dv_ref[0, own, :] = dv.astype(BF16)

    return pl.pallas_call(
        body, name="flash_bwd", grid=(nh, ng),
        in_specs=[pl.BlockSpec((1, s, QK_PAD), lambda h, j: (h, 0, 0)),
                  pl.BlockSpec((1, kps * tq, QK_PAD), lambda h, g: (h, ng - 1 - g, 0)),
                  pl.BlockSpec((1, kps * tq, V_HEAD), lambda h, g: (h, ng - 1 - g, 0)),
                  pl.BlockSpec((s, V_HEAD), lambda h, j: (0, h)),
                  pl.BlockSpec((1, nq, tq), lambda h, j: (h, 0, 0)),
                  pl.BlockSpec((1, nq, tq), lambda h, j: (h, 0, 0))],
        out_specs=[pl.BlockSpec((1, s, QK_PAD), lambda h, j: (h, 0, 0)),
                   pl.BlockSpec((1, kps * tq, QK_PAD), lambda h, g: (h, ng - 1 - g, 0)),
                   pl.BlockSpec((1, kps * tq, V_HEAD), lambda h, g: (h, ng - 1 - g, 0))],
        out_shape=[jax.ShapeDtypeStruct((nh, s, QK_PAD), BF16), jax.ShapeDtypeStruct((nh, s, QK_PAD), BF16),
                   jax.ShapeDtypeStruct((nh, s, V_HEAD), BF16)],
        scratch_shapes=[pltpu.VMEM((s, QK_PAD), F32)],
        compiler_params=_params(("parallel", "arbitrary")),
    )(q, k, v, do, lse, delta)


def _tail(x, target, o, u, mod, w_out, conv_w):
    s, d = x.shape
    tm = min(TM_ELEM, s)

    def body(x_ref, t_ref, o_ref, za_ref, mod_ref, w_ref, xc_ref, bc_ref, cc_ref, zc_ref, xp_ref, cp_ref, cw_ref,
             gx_ref, dy_ref, ycat_ref, dyc_ref, do_ref, du_ref, delta_ref, dgate_ref, loss_ref):
        i = pl.program_id(0)
        za = pltpu.roll(za_ref[:, ZA_LO:ZA_LO + ZA_WIN].astype(F32), ZA_WIN - QK_ROPE, 1)[:, :D_ATTN]
        ov = o_ref[...].astype(F32)
        sg = _sigmoid(za)
        sl = za * sg
        ya = ov * sl
        y = _dot(ya.astype(BF16), w_ref[D_CONV:, :])
        yc = _conv_y(xc_ref, bc_ref, cc_ref, zc_ref, xp_ref, cp_ref, cw_ref, i == 0)
        y = y + _dot(yc.astype(BF16), w_ref[:D_CONV, :])
        ycat_ref[...] = jnp.concatenate([yc.T, ya.T], axis=0).astype(BF16)
        gate = mod_ref[:, 2 * d:3 * d]
        e = x_ref[...] + gate * y - t_ref[...]
        dout = e * (1.0 / d)
        gx_ref[...] = dout
        dy = (dout * gate).astype(BF16)
        dy_ref[...] = dy
        dycat = _dot_nt(dy, w_ref[...])
        dyc_ref[...] = dycat[:, :D_CONV].astype(BF16)
        dya = dycat[:, D_CONV:]
        dov = dya * sl
        do_ref[...] = dov.astype(BF16)
        du_ref[...] = (dya * ov * (sg * (1.0 + za * (1.0 - sg)))).astype(BF16)
        prod_t = (dov * ov).T
        for h in range(N_HEADS):
            delta_ref[h] = jnp.sum(prod_t[V_HEAD * h:V_HEAD * (h + 1), :], axis=0, keepdims=True)
        dgate = jnp.sum(dout * y, axis=0, keepdims=True)
        part = jnp.sum(jnp.sum(e * e, axis=0, keepdims=True), axis=1, keepdims=True) * (0.5 / d)
        part = jnp.broadcast_to(part, (1, LANES))

        @pl.when(i == 0)
        def _():
            dgate_ref[...] = dgate
            loss_ref[...] = part

        @pl.when(i > 0)
        def _():
            dgate_ref[...] += dgate
            loss_ref[...] += part

    tok = lambda w: pl.BlockSpec((tm, w), lambda i: (i, 0))
    return pl.pallas_call(
        body, name="tail", grid=(s // tm,),
        in_specs=[tok(d), tok(d), tok(D_ATTN), pl.BlockSpec((tm, U_TAIL), lambda i: (i, U_COLS // U_TAIL - 1)),
                  _full((1, 3 * d)), _full((d, d))] + _conv_specs(tm) + [_full((3, D_CONV))],
        out_specs=[tok(d), tok(d), pl.BlockSpec((d, tm), lambda i: (0, i)), tok(D_CONV), tok(D_ATTN), tok(D_ATTN),
                   pl.BlockSpec((N_HEADS, 1, tm), lambda i: (0, 0, i)), _full((1, d)), _full((1, LANES))],
        out_shape=[jax.ShapeDtypeStruct((s, d), F32), jax.ShapeDtypeStruct((s, d), BF16),
                   jax.ShapeDtypeStruct((d, s), BF16), jax.ShapeDtypeStruct((s, D_CONV), BF16),
                   jax.ShapeDtypeStruct((s, D_ATTN), BF16), jax.ShapeDtypeStruct((s, D_ATTN), BF16),
                   jax.ShapeDtypeStruct((N_HEADS, 1, s), F32), jax.ShapeDtypeStruct((1, d), F32),
                   jax.ShapeDtypeStruct((1, LANES), F32)],
        compiler_params=_params(("arbitrary",)),
    )(x, target, o, u, mod, w_out, u, u, u, u, u, u, conv_w)


def _norm_bwd(x, dh, gx1, norm_g, mod):
    s, d = x.shape
    tm = min(TM_MM, s)

    def body(x_ref, dh_ref, gx_ref, g_ref, mod_ref, o_ref, dshift_ref, dscale_ref, dg_ref):
        i = pl.program_id(0)
        gv, sc1 = g_ref[...], 1.0 + mod_ref[:, d:2 * d]
        gsc = gv * sc1
        half = NORM_ROWS // 2

        def group(c, acc):
            a_dh, a_dhxn = acc
            ks = range(NORM_GROUP)
            rows = [pl.ds(pl.multiple_of((c * NORM_GROUP + k) * NORM_ROWS, NORM_ROWS), NORM_ROWS) for k in ks]
            xv = [x_ref[rows[k], :] for k in ks]
            dhv = [dh_ref[rows[k], :].astype(F32) for k in ks]
            r = [lax.rsqrt(jnp.mean(xv[k] * xv[k], axis=-1, keepdims=True) + EPS) for k in ks]
            xn = [xv[k] * r[k] for k in ks]
            dxn = [dhv[k] * gsc for k in ks]
            t = [jnp.mean(dxn[k] * xn[k], axis=-1, keepdims=True) for k in ks]
            for k in ks:
                o_ref[rows[k], :] = gx_ref[rows[k], :] + r[k] * (dxn[k] - xn[k] * t[k])
                dhxn = dhv[k] * xn[k]
                a_dh = a_dh + dhv[k][:half] + dhv[k][half:]
                a_dhxn = a_dhxn + dhxn[:half] + dhxn[half:]
            return a_dh, a_dhxn

        zero = jnp.zeros((half, d), F32)
        a_dh, a_dhxn = lax.fori_loop(0, tm // (NORM_ROWS * NORM_GROUP), group, (zero, zero))
        dshift = jnp.sum(a_dh, axis=0, keepdims=True)
        s_dhxn = jnp.sum(a_dhxn, axis=0, keepdims=True)
        dscale, dg = s_dhxn * gv, s_dhxn * sc1

        @pl.when(i == 0)
        def _():
            dshift_ref[...] = dshift
            dscale_ref[...] = dscale
            dg_ref[...] = dg

        @pl.when(i > 0)
        def _():
            dshift_ref[...] += dshift
            dscale_ref[...] += dscale
            dg_ref[...] += dg

    tok = pl.BlockSpec((tm, d), lambda i: (i, 0))
    row = jax.ShapeDtypeStruct((1, d), F32)
    return pl.pallas_call(
        body, name="norm_bwd", grid=(s // tm,),
        in_specs=[tok, tok, tok, _full((1, d)), _full((1, 3 * d))],
        out_specs=[tok, _full((1, d)), _full((1, d)), _full((1, d))],
        out_shape=[jax.ShapeDtypeStruct((s, d), F32), row, row, row],
        compiler_params=_params(("arbitrary",)),
    )(x, dh, gx1, norm_g, mod)


def _adamw(w, g, m, v, name):
    rows, cols = w.shape
    tr = 256 if rows % 256 == 0 else rows
    tc = 512 if (rows > 256 and tr == rows and cols % 512 == 0) else cols

    def body(w_ref, g_ref, m_ref, v_ref, d_ref, nm_ref, nv_ref):
        gv = g_ref[...]
        nm = ADAM_B1 * m_ref[...] + (1.0 - ADAM_B1) * gv
        nv = ADAM_B2 * v_ref[...] + (1.0 - ADAM_B2) * (gv * gv)
        m_hat = nm / (1.0 - ADAM_B1 ** ADAM_STEP)
        v_hat = nv / (1.0 - ADAM_B2 ** ADAM_STEP)
        d_ref[...] = -ADAM_LR * (m_hat / (jnp.sqrt(v_hat) + ADAM_EPS) + ADAM_WD * w_ref[...])
        nm_ref[...] = nm
        nv_ref[...] = nv

    spec = pl.BlockSpec((tr, tc), lambda i, j: (i, j))
    shape = jax.ShapeDtypeStruct((rows, cols), F32)
    return pl.pallas_call(
        body, name=name, grid=(rows // tr, cols // tc), in_specs=[spec] * 4, out_specs=[spec] * 3, out_shape=[shape] * 3,
        compiler_params=_params(("parallel", "parallel")),
    )(w, g, m, v)


def _pad_cols(a, n):
    return jnp.pad(a, ((0, 0), (0, n - a.shape[1])))


def kernel(x, c, positions, ada_w, ada_b, norm_g, w_in, conv_w, q_a_g, w_q_b, kv_a_g, w_kv_b, q_g, k_g, w_out, loss_target, m_ada_w, m_ada_b, m_norm_g, m_w_in, m_conv_w, m_q_a_g, m_w_q_b, m_kv_a_g, m_w_kv_b, m_q_g, m_k_g, m_w_out, v_ada_w, v_ada_b, v_norm_g, v_w_in, v_conv_w, v_q_a_g, v_w_q_b, v_kv_a_g, v_w_kv_b, v_q_g, v_k_g, v_w_out):
    me = _my_index()
    s = x.shape[1]
    nq = s // min(TQ, s)
    x2, tgt = x[0], loss_target[0]
    w_in_l, w_q_l, w_kv_l, w_out_l, conv_l, ada_w_l = w_in[0], w_q_b[0], w_kv_b[0], w_out[0], conv_w[0], ada_w[0]
    ada_cols = ada_w_l.shape[1]

    small = jnp.concatenate([c.reshape(-1, LANES), conv_l.reshape(-1, LANES), jnp.zeros((5, LANES), F32)], axis=0)
    (small_g,) = _exchange(_GatherDirect([small]), [small], "gather_c")
    c_all = small_g[:, :D_MODEL // LANES].reshape(N_DEV, D_MODEL)
    conv_g = small_g[:, D_MODEL // LANES:D_MODEL // LANES + 3].transpose(1, 0, 2).reshape(3, D_CONV)

    ada_b_l = lax.dynamic_slice(ada_b, (0, me * ada_cols), (1, ada_cols))
    mod_cols = _ada_mod(jnp.pad(c_all, ((0, 8), (0, 0))), ada_w_l, ada_b_l)[:N_DEV]
    (mod_g,) = _exchange(_GatherDirect([mod_cols]), [mod_cols], "gather_mod")
    mod = lax.dynamic_index_in_dim(mod_g, me, axis=1, keepdims=False).reshape(1, 3 * D_MODEL)

    half = jnp.arange(0, QK_ROPE, 2, dtype=F32) / QK_ROPE
    inv_freq = ROPE_BASE ** (-half)
    zeros64 = jnp.zeros((LANES - QK_ROPE,), F32)
    invf = jnp.concatenate([inv_freq, inv_freq, zeros64]).reshape(1, LANES)
    sign = jnp.concatenate([-jnp.ones((32,), F32), jnp.ones((32,), F32), zeros64]).reshape(1, LANES)
    qg_p, kg_p = _pad_cols(q_g, QK_PAD), _pad_cols(k_g, QK_PAD)

    my_off = ((CW * me) % LANES).astype(jnp.int32)
    win = [_expand_w_in(w_in_l.T, my_off.reshape(1))]
    h, h_t, cos, sin, win_g = _norm_mod(x2, norm_g, mod, positions.reshape(s, 1), invf, sign, _Gather(win, relay=True, parts=4), win)
    w_in_p = _merge_w_in(win_g)
    rest = [_pad_wq(w_q_l.T), w_kv_l.astype(BF16), w_out_l.astype(BF16)]
    u, wq_g, wkv_g, w_out_g = _matmul(h, w_in_p, nt=False, out_dtype=BF16, tm=2 * TM_MM, tn=2048, name="in_proj",
                                      rider=_Gather(rest), rider_inputs=rest)
    w_out_g = w_out_g.reshape(D_MODEL, D_MODEL)
    wq_g = wq_g.transpose(1, 0, 2).reshape(Q_LORA, N_HEADS * QK_PAD)
    wkv_g = wkv_g.transpose(1, 0, 2).reshape(KV_LORA, 2 * D_ATTN)
    q, k, v = _qkv_fwd(u, cos, sin, wq_g, wkv_g, q_a_g, kv_a_g, qg_p, kg_p)
    o, lse = _flash_fwd(q, k, v)
    gx1, dy, ycat_t, dyc, do, dza, delta, dgate, loss_row = _tail(x2, tgt, o, u, mod, w_out_g, conv_g)

    dq, dk, dv = _flash_bwd(q, k, v, do, lse.reshape(N_HEADS, nq, s // nq), delta.reshape(N_HEADS, nq, s // nq))
    du, dconv = _conv_bwd(u, dyc, conv_g)
    du, dwq, dwkv, dqag, dkvag, dqg, dkg = _qkv_bwd(u, cos, sin, dq, dk, dv, dza, wq_g, wkv_g, q_a_g, kv_a_g, qg_p, kg_p, du)
    dwq = dwq.reshape(Q_LORA, N_HEADS, QK_PAD).transpose(1, 0, 2)
    dwkv = dwkv.reshape(KV_LORA, N_HEADS, 2 * V_HEAD).transpose(1, 0, 2)
    dw_in = _matmul(h_t, du, nt=False, out_dtype=BF16, tm=TM_MM, tn=768, name="dw_in")
    first = [dw_in, dwq, dwkv]
    dw_out, r_in, r_q, r_kv = _matmul(ycat_t, dy, nt=False, out_dtype=BF16, tm=TM_MM, tn=512, name="dw_out",
                                      rider=_SiblingExchange(first, [True, False, False]), rider_inputs=first)
    dw_out = dw_out.reshape(N_DEV, D_MODEL // N_DEV, D_MODEL)
    (r_out,) = _exchange(_SiblingExchange([dw_out], [False]), [dw_out], "rs_sibling_out")
    core = lax.axis_index("c").astype(jnp.int32)
    lo_tiles = ((CW * (2 * jnp.arange(4, dtype=jnp.int32) + core)) // LANES).astype(jnp.int32)
    pairs = [_add_window(dw_in, r_in, lo_tiles), _add_pairs(dwq, r_q, core.reshape(1), "rs_add_q"),
             _add_pairs(dwkv, r_kv, core.reshape(1), "rs_add_kv"), _add_pairs(dw_out, r_out, core.reshape(1), "rs_add_out")]
    dh, *quads = _matmul(du, w_in_p, nt=True, out_dtype=BF16, tm=2 * TM_MM, tn=512, name="dh",
                         rider=_ChipExchange(pairs), rider_inputs=pairs, a_resident=True)
    my_chip = 2 * lax.axis_index("x") + lax.axis_index("y")
    written = jnp.where(jnp.arange(4) == my_chip, (jnp.arange(4) + 1) % 4, jnp.arange(4))
    sel = jnp.concatenate([my_chip.reshape(1), written, ((EXP_W - my_off) % EXP_W).reshape(1)]).astype(jnp.int32)
    g_w_in_t = _final_sum(pairs[0], quads[0], sel, "rs_sum_in", unshift=True, keep_t=CW)
    g_w_q_t = _final_sum(pairs[1], quads[1], sel, "rs_sum_q", keep_t=QK_HEAD)
    g_w_kv = _final_sum(pairs[2], quads[2], sel, "rs_sum_kv")
    g_w_out = _final_sum(pairs[3], quads[3], sel, "rs_sum_out")
    grad_x, dshift, dscale, dng = _norm_bwd(x2, dh, gx1, norm_g, mod)

    row = jnp.concatenate([dshift, dscale, dgate, dng, dqag, dkvag, dqg, dkg, dconv[:3].reshape(1, 3 * D_CONV), loss_row], axis=1)
    (rows_g,) = _exchange(_GatherDirect([row]), [row], "gather_small")
    tot = _sum_leading(rows_g, F32, "sum_small")
    dmod_all = rows_g[:, 0, SM_MOD:SM_NG]
    g_ada_b = tot[:, SM_MOD:SM_NG]
    g_norm_g = tot[:, SM_NG:SM_QAG]
    g_q_a_g = tot[:, SM_QAG:SM_KVAG]
    g_kv_a_g = tot[:, SM_KVAG:SM_QG]
    g_q_g = tot[:, SM_QG:SM_QG + QK_HEAD]
    g_k_g = tot[:, SM_KG:SM_KG + QK_HEAD]
    conv_cols = conv_l.shape[1]
    g_conv = lax.dynamic_slice(tot[:, SM_CONV:SM_LOSS].reshape(3, D_CONV), (0, me * conv_cols), (3, conv_cols))
    loss = tot[0, SM_LOSS]
    dmod_my = lax.dynamic_slice(dmod_all, (0, me * ada_cols), (N_DEV, ada_cols))
    g_ada_w = _ada_w_grad(c_all.T, dmod_my)

    grads = dict(ada_w=g_ada_w, ada_b=g_ada_b, norm_g=g_norm_g, w_in=g_w_in_t, conv_w=g_conv, q_a_g=g_q_a_g, w_q_b=g_w_q_t,
                 kv_a_g=g_kv_a_g, w_kv_b=g_w_kv, q_g=g_q_g, k_g=g_k_g, w_out=g_w_out)
    weights = dict(ada_w=(ada_w, m_ada_w, v_ada_w), ada_b=(ada_b, m_ada_b, v_ada_b), norm_g=(norm_g, m_norm_g, v_norm_g),
                   w_in=(w_in, m_w_in, v_w_in), conv_w=(conv_w, m_conv_w, v_conv_w), q_a_g=(q_a_g, m_q_a_g, v_q_a_g),
                   w_q_b=(w_q_b, m_w_q_b, v_w_q_b), kv_a_g=(kv_a_g, m_kv_a_g, v_kv_a_g), w_kv_b=(w_kv_b, m_w_kv_b, v_w_kv_b),
                   q_g=(q_g, m_q_g, v_q_g), k_g=(k_g, m_k_g, v_k_g), w_out=(w_out, m_w_out, v_w_out))
    names = list(grads)
    out_g, out_d, out_m, out_v = [], [], [], []
    for n in names:
        w, m, v_ = weights[n]
        shape2 = w.shape[-2:] if w.ndim == 3 else (1, w.shape[-1])
        transposed = n in ("w_in", "w_q_b")
        to2 = (lambda a: a.reshape(shape2).T) if transposed else (lambda a: a.reshape(shape2))
        back = (lambda a: a.T.reshape(w.shape)) if transposed else (lambda a: a.reshape(w.shape))
        g2 = grads[n] if transposed else grads[n].reshape(shape2)
        d2, m2, v2 = _adamw(to2(w), g2, to2(m), to2(v_), "adamw_" + n)
        out_g.append(back(g2))
        out_d.append(back(d2))
        out_m.append(back(m2))
        out_v.append(back(v2))
    return (loss, grad_x.reshape(x.shape), *out_g, *out_d, *out_m, *out_v)
```

```python
import functools
import math

import jax
import jax.numpy as jnp
from jax import lax
from jax.experimental import pallas as pl
from jax.experimental.pallas import tpu as pltpu

F32 = jnp.float32
BF16 = jnp.bfloat16
MESH = pl.DeviceIdType.MESH

D_MODEL = 2048
D_CONV = 1024
N_HEADS = 8
QK_NOPE = 128
QK_ROPE = 64
QK_HEAD = QK_NOPE + QK_ROPE
V_HEAD = 128
D_ATTN = N_HEADS * V_HEAD
Q_LORA = 512
KV_LORA = 256
ROPE_BASE = 10000.0
IN_COLS = 4 * D_CONV + Q_LORA + KV_LORA + QK_ROPE + D_ATTN
EPS = 1e-6
ADAM_LR, ADAM_B1, ADAM_B2, ADAM_EPS, ADAM_WD, ADAM_STEP = 0.001, 0.9, 0.999, 1e-08, 0.01, 10

N_DEV = 8
LANES = 128
QK_PAD = 256
U_COLS = 6144
U_CQ, U_CKV, U_KR, U_ZA = 4096, 4608, 4864, 4928
U_TAIL = 2048
ZA_LO = U_ZA - (U_COLS - U_TAIL) - QK_ROPE
ZA_WIN = D_ATTN + LANES
CW = IN_COLS // 8
EXP_W = 896
W_LO = [(CW * d // 128) * 128 for d in range(8)]
W_OFF = [CW * d - lo for d, lo in enumerate(W_LO)]
SCALE = 1.0 / math.sqrt(QK_HEAD)
LOG2E = 1.4426950408889634
LN2 = 0.6931471805599453
NEG = -1e30
VMEM_LIMIT = 56 * 1024 * 1024

TM_ELEM = 256
NORM_ROWS = 16
NORM_GROUP = 4
TM_MM = 512
TQ = 1024
Q_CHAINS = 4
KV_SPLIT = 2

SM_MOD, SM_NG, SM_QAG, SM_KVAG, SM_QG, SM_KG, SM_CONV, SM_LOSS = 0, 6144, 8192, 8704, 8960, 9216, 9472, 12544
SM_COLS = 12672


def _params(sem=None):
    kw = dict(vmem_limit_bytes=VMEM_LIMIT)
    if sem is not None:
        kw["dimension_semantics"] = sem
    return pltpu.CompilerParams(**kw)


def _sigmoid(z):
    return 1.0 / (1.0 + jnp.exp(-z))


def _rot64(x):
    lane = lax.broadcasted_iota(jnp.int32, x.shape, 1)
    return jnp.where(lane < 32, pltpu.roll(x, 96, 1), pltpu.roll(x, 32, 1))


def _rope(x, cos, sin):
    return x * cos + _rot64(x) * sin


def _rope_t(d, cos, sin):
    return d * cos - _rot64(d) * sin


def _dot(a, b):
    return jnp.dot(a, b, preferred_element_type=F32)


def _dot_nt(a, b):
    return lax.dot_general(a, b, (((1,), (1,)), ((), ())), preferred_element_type=F32)


def _dot_tn(a, b):
    return lax.dot_general(a, b, (((0,), (0,)), ((), ())), preferred_element_type=F32)


def _my_index():
    return 4 * lax.axis_index("x") + 2 * lax.axis_index("y") + lax.axis_index("c")


ANY = pl.BlockSpec(memory_space=pl.ANY)


class _Gather:
    def __init__(self, blocks, relay=False, parts=1):
        self.relay = relay
        self.parts = parts
        self.rows = [b.shape[0] // parts for b in blocks]
        self.n = n = len(blocks) * parts
        self.out_shape = [jax.ShapeDtypeStruct((N_DEV,) + b.shape, b.dtype) for b in blocks]
        self.scratch = [pltpu.SemaphoreType.DMA((7 * n,)), pltpu.SemaphoreType.DMA((7 * n,)),
                        pltpu.SemaphoreType.DMA((n,))]

    @staticmethod
    def _places():
        x, y, c = lax.axis_index("x"), lax.axis_index("y"), lax.axis_index("c")
        return (x, y, c), (x, y, 1 - c), [(1 - x, y), (x, 1 - y), (1 - x, 1 - y)]

    def _src(self, ins, a):
        block, part = divmod(a, self.parts)
        return ins[block] if self.parts == 1 else ins[block].at[pl.ds(part * self.rows[block], self.rows[block])]

    def _dst(self, outs, a, place):
        block, part = divmod(a, self.parts)
        ref = outs[block].at[4 * place[0] + 2 * place[1] + place[2]]
        return ref if self.parts == 1 else ref.at[pl.ds(part * self.rows[block], self.rows[block])]

    def _copy(self, outs, sems, a, k, block, to, src=None):
        dst = self._dst(outs, a, block)
        return pltpu.make_async_remote_copy(
            src_ref=dst if src is None else src, dst_ref=dst, send_sem=sems[0].at[7 * a + k],
            recv_sem=sems[1].at[7 * a + k], device_id=to, device_id_type=MESH)

    def _first(self, ins, outs, sems):
        me, sibling, chips = self._places()
        first = []
        for a in range(self.n):
            first.append(self._copy(outs, sems, a, 0, me, sibling, src=self._src(ins, a)))
            first += [self._copy(outs, sems, a, 1 + j, me, (*chip, me[2]), src=self._src(ins, a))
                      for j, chip in enumerate(chips[:2] if self.relay else chips)]
        return first

    def _relays(self, outs, sems):
        if not self.relay:
            return []
        (x, y, c), _, _ = self._places()
        via = (jnp.where(c == 0, 1 - x, x), jnp.where(c == 0, y, 1 - y))
        to = (jnp.where(c == 0, x, 1 - x), jnp.where(c == 0, 1 - y, y))
        return [self._copy(outs, sems, a, 3, (*via, c), (*to, c)) for a in range(self.n)]

    def _passed(self, outs, sems):
        me, sibling, chips = self._places()
        return [self._copy(outs, sems, a, 4 + j, (*chip, me[2]), sibling)
                for a in range(self.n) for j, chip in enumerate(chips)]

    def _mine(self, ins, outs, sems):
        me, _, _ = self._places()
        return [pltpu.make_async_copy(self._src(ins, a), self._dst(outs, a, me), sems[2].at[a]) for a in range(self.n)]

    def start(self, ins, outs, sems):
        for cp in self._mine(ins, outs, sems) + self._first(ins, outs, sems):
            cp.start()

    def forward(self, ins, outs, sems):
        del ins
        me, _, chips = self._places()
        passed, relays = self._passed(outs, sems), self._relays(outs, sems)
        for a in range(self.n):
            for j, chip in enumerate(chips[:2] if self.relay else chips):
                self._copy(outs, sems, a, 1 + j, (*chip, me[2]), me).wait_recv()
                passed[3 * a + j].start()
            if self.relay:
                relays[a].start()
        if self.relay:
            for a in range(self.n):
                self._copy(outs, sems, a, 3, (*chips[2], me[2]), me).wait_recv()
                passed[3 * a + 2].start()

    def finish(self, ins, outs, sems):
        me, sibling, chips = self._places()
        for a in range(self.n):
            self._copy(outs, sems, a, 0, sibling, me).wait_recv()
            for j, chip in enumerate(chips):
                self._copy(outs, sems, a, 4 + j, (*chip, 1 - me[2]), me).wait_recv()
        for cp in self._first(ins, outs, sems) + self._relays(outs, sems) + self._passed(outs, sems):
            cp.wait_send()
        for cp in self._mine(ins, outs, sems):
            cp.wait()


class _GatherDirect:
    FLIPS = [(0, 0, 1), (1, 0, 0), (0, 1, 0), (1, 1, 0), (1, 0, 1), (0, 1, 1), (1, 1, 1)]

    def __init__(self, blocks):
        self.n = n = len(blocks)
        self.out_shape = [jax.ShapeDtypeStruct((N_DEV,) + b.shape, b.dtype) for b in blocks]
        self.scratch = [pltpu.SemaphoreType.DMA((7 * n,)), pltpu.SemaphoreType.DMA((7 * n,)),
                        pltpu.SemaphoreType.DMA((n,))]

    def _copies(self, ins, outs, sems):
        x, y, c = lax.axis_index("x"), lax.axis_index("y"), lax.axis_index("c")
        mine = 4 * x + 2 * y + c
        remote = [pltpu.make_async_remote_copy(
            src_ref=ins[a], dst_ref=outs[a].at[mine], send_sem=sems[0].at[7 * a + k], recv_sem=sems[1].at[7 * a + k],
            device_id=(1 - x if fx else x, 1 - y if fy else y, 1 - c if fc else c), device_id_type=MESH)
            for a in range(self.n) for k, (fx, fy, fc) in enumerate(self.FLIPS)]
        local = [pltpu.make_async_copy(ins[a], outs[a].at[mine], sems[2].at[a]) for a in range(self.n)]
        return remote + local

    def start(self, ins, outs, sems):
        for cp in self._copies(ins, outs, sems):
            cp.start()

    def forward(self, ins, outs, sems):
        pass

    def finish(self, ins, outs, sems):
        for cp in self._copies(ins, outs, sems):
            cp.wait()


class _ChipExchange:
    def __init__(self, arrays):
        self.n = n = len(arrays)
        self.out_shape = [jax.ShapeDtypeStruct(a.shape, a.dtype) for a in arrays]
        self.scratch = [pltpu.SemaphoreType.DMA((3 * n,)), pltpu.SemaphoreType.DMA((3 * n,))]

    def _copies(self, ins, outs, sems):
        x, y, c = lax.axis_index("x"), lax.axis_index("y"), lax.axis_index("c")
        return [pltpu.make_async_remote_copy(
            src_ref=ins[a].at[2 * px + py], dst_ref=outs[a].at[2 * x + y], send_sem=sems[0].at[3 * a + j],
            recv_sem=sems[1].at[3 * a + j], device_id=(px, py, c), device_id_type=MESH)
            for a in range(self.n) for j, (px, py) in enumerate([(1 - x, y), (x, 1 - y), (1 - x, 1 - y)])]

    def start(self, ins, outs, sems):
        for cp in self._copies(ins, outs, sems):
            cp.start()

    def forward(self, ins, outs, sems):
        pass

    def finish(self, ins, outs, sems):
        for cp in self._copies(ins, outs, sems):
            cp.wait()


class _SiblingExchange:
    def __init__(self, arrays, windowed):
        self.n = n = len(arrays)
        self.windowed = windowed
        self.out_shape = [jax.ShapeDtypeStruct((4, a.shape[0], EXP_W) if w else (4,) + a.shape[1:], a.dtype)
                          for a, w in zip(arrays, windowed)]
        self.scratch = [pltpu.SemaphoreType.DMA((4 * n,)), pltpu.SemaphoreType.DMA((4 * n,))]

    def _each(self, ins, outs, sems, act):
        x, y, c = lax.axis_index("x"), lax.axis_index("y"), lax.axis_index("c")

        def branch(c_val):
            for k in range(4):
                e = 2 * k + (1 - c_val)
                for a in range(self.n):
                    src = ins[a].at[:, pl.ds(W_LO[e], EXP_W)] if self.windowed[a] else ins[a].at[e]
                    act(pltpu.make_async_remote_copy(
                        src_ref=src, dst_ref=outs[a].at[k], send_sem=sems[0].at[4 * a + k], recv_sem=sems[1].at[4 * a + k],
                        device_id=(x, y, 1 - c), device_id_type=MESH))

        for c_val in (0, 1):
            pl.when(c == c_val)(functools.partial(branch, c_val))

    def start(self, ins, outs, sems):
        self._each(ins, outs, sems, lambda cp: cp.start())

    def forward(self, ins, outs, sems):
        pass

    def finish(self, ins, outs, sems):
        self._each(ins, outs, sems, lambda cp: cp.wait())


def _exchange(rider, arrays, name):
    n = len(arrays)

    def body(*refs):
        ins, outs, sems = refs[:n], refs[n:n + len(rider.out_shape)], refs[n + len(rider.out_shape):]
        rider.start(ins, outs, sems)
        rider.forward(ins, outs, sems)
        rider.finish(ins, outs, sems)

    return pl.pallas_call(body, name=name, out_shape=rider.out_shape, in_specs=[ANY] * n,
                          out_specs=[ANY] * len(rider.out_shape), scratch_shapes=rider.scratch)(*arrays)


def _add_window(dw_in, recv, lo_tiles):
    k, rows, _ = recv.shape
    tiles = EXP_W // LANES

    def body(t_ref, *refs):
        del t_ref
        r_ref, o_ref = refs[tiles], refs[tiles + 1]
        own = jnp.concatenate([w_ref[...] for w_ref in refs[:tiles]], axis=1)
        o_ref[0] = (own.astype(F32) + r_ref[0].astype(F32)).astype(o_ref.dtype)

    def tile(j):
        return pl.BlockSpec((rows, LANES), lambda i, t: (0, t[i] + j))

    spec = pl.BlockSpec((1, rows, EXP_W), lambda i, t: (i, 0, 0))
    grid_spec = pltpu.PrefetchScalarGridSpec(
        num_scalar_prefetch=1, grid=(k,), in_specs=[tile(j) for j in range(tiles)] + [spec], out_specs=spec)
    return pl.pallas_call(
        body, name="rs_add_in", grid_spec=grid_spec, out_shape=jax.ShapeDtypeStruct(recv.shape, recv.dtype),
        compiler_params=_params(("parallel",)),
    )(lo_tiles, *([dw_in] * tiles), recv)


def _final_sum(p, r, sel, name, unshift=False, keep_t=None):
    _, rows, cols = p.shape
    tr = 512 if rows % 512 == 0 else rows

    def body(sel_ref, p_ref, r0, r1, r2, r3, o_ref):
        own = p_ref[0].astype(F32)
        acc = None
        for k, r_ref in enumerate((r0, r1, r2, r3)):
            term = jnp.where(sel_ref[0] == k, own, r_ref[0].astype(F32))
            acc = term if acc is None else acc + term
        if unshift:
            acc = pltpu.roll(acc, sel_ref[5], 1)
        o_ref[...] = acc if keep_t is None else acc.T[:keep_t]

    def slot(k):
        return pl.BlockSpec((1, tr, cols), lambda i, t: (t[k], i, 0))

    if keep_t is None:
        out_spec, out_shape = pl.BlockSpec((tr, cols), lambda i, t: (i, 0)), (rows, cols)
    else:
        out_spec, out_shape = pl.BlockSpec((keep_t, tr), lambda i, t: (0, i)), (keep_t, rows)
    grid_spec = pltpu.PrefetchScalarGridSpec(
        num_scalar_prefetch=1, grid=(rows // tr,), in_specs=[slot(0), slot(1), slot(2), slot(3), slot(4)],
        out_specs=out_spec)
    return pl.pallas_call(
        body, name=name, grid_spec=grid_spec, out_shape=jax.ShapeDtypeStruct(out_shape, F32),
        compiler_params=_params(("parallel",)),
    )(sel, p, r, r, r, r)


def _expand_w_in(w_t, shift):
    cw, rows = w_t.shape
    tr = TM_MM
    pad = -cw % LANES

    def body(s_ref, w_ref, o_ref):
        w = jnp.concatenate([w_ref[...], jnp.zeros((pad, tr), F32)], axis=0).T
        w = jnp.concatenate([w, jnp.zeros((tr, EXP_W - cw - pad), F32)], axis=1)
        o_ref[...] = pltpu.roll(w, s_ref[0], 1).astype(BF16)

    grid_spec = pltpu.PrefetchScalarGridSpec(
        num_scalar_prefetch=1, grid=(rows // tr,), in_specs=[pl.BlockSpec((cw, tr), lambda i, t: (0, i))],
        out_specs=pl.BlockSpec((tr, EXP_W), lambda i, t: (i, 0)))
    return pl.pallas_call(
        body, name="expand_w_in", grid_spec=grid_spec, out_shape=jax.ShapeDtypeStruct((rows, EXP_W), BF16),
        compiler_params=_params(("arbitrary",)),
    )(shift, w_t)


def _pad_wq(w_t):
    cw, rows = w_t.shape

    def body(w_ref, o_ref):
        o_ref[...] = jnp.concatenate([w_ref[...], jnp.zeros((QK_PAD - cw, rows), F32)], axis=0).T.astype(BF16)

    return pl.pallas_call(
        body, name="pad_wq", out_shape=jax.ShapeDtypeStruct((rows, QK_PAD), BF16), compiler_params=_params(),
    )(w_t)


def _merge_w_in(e):
    _, rows, _ = e.shape
    tr = TM_MM

    def body(e_ref, o_ref):
        for t in range(U_COLS // LANES):
            lo, hi = t * LANES, (t + 1) * LANES
            parts = [e_ref[d, :, lo - W_LO[d]:hi - W_LO[d]] for d in range(N_DEV)
                     if CW * d < hi and CW * (d + 1) > lo]
            if not parts:
                tile = jnp.zeros((tr, LANES), BF16)
            elif len(parts) == 1:
                tile = parts[0]
            else:
                tile = (parts[0].astype(F32) + parts[1].astype(F32)).astype(BF16)
            o_ref[:, lo:hi] = tile

    return pl.pallas_call(
        body, name="merge_w_in", grid=(rows // tr,),
        in_specs=[pl.BlockSpec((N_DEV, tr, EXP_W), lambda i: (0, i, 0))],
        out_specs=pl.BlockSpec((tr, U_COLS), lambda i: (i, 0)), out_shape=jax.ShapeDtypeStruct((rows, U_COLS), BF16),
        compiler_params=_params(("parallel",)),
    )(e)


def _sum_leading(a, out_dtype, name):
    k, rows, cols = a.shape
    tr = min(rows, 1728 if rows % 1728 == 0 else rows)

    def body(a_ref, o_ref):
        acc = a_ref[0].astype(F32)
        for i in range(1, k):
            acc = acc + a_ref[i].astype(F32)
        o_ref[...] = acc.astype(out_dtype)

    return pl.pallas_call(
        body, name=name, grid=(rows // tr,),
        in_specs=[pl.BlockSpec((k, tr, cols), lambda i: (0, i, 0))],
        out_specs=pl.BlockSpec((tr, cols), lambda i: (i, 0)),
        out_shape=jax.ShapeDtypeStruct((rows, cols), out_dtype), compiler_params=_params(("parallel",)),
    )(a)


def _add_pairs(gs, recvs, core, name):
    n = len(gs)

    def body(c_ref, *refs):
        del c_ref
        for a in range(n):
            refs[2 * n + a][...] = (refs[a][...].astype(F32) + refs[n + a][...].astype(F32)).astype(refs[2 * n + a].dtype)

    def mine(r):
        return pl.BlockSpec((1,) + r.shape[1:], lambda i, c: (2 * i + c[0], 0, 0))

    def kth(r):
        return pl.BlockSpec((1,) + r.shape[1:], lambda i, c: (i, 0, 0))

    grid_spec = pltpu.PrefetchScalarGridSpec(
        num_scalar_prefetch=1, grid=(4,), in_specs=[mine(r) for r in recvs] + [kth(r) for r in recvs],
        out_specs=[kth(r) for r in recvs])
    return pl.pallas_call(
        body, name=name, grid_spec=grid_spec, out_shape=[jax.ShapeDtypeStruct(r.shape, r.dtype) for r in recvs],
        compiler_params=_params(("parallel",)),
    )(core, *gs, *recvs)


def _final_sums(ps, rs, sel, keep_t, name):
    n = len(ps)

    def body(sel_ref, *refs):
        for a in range(n):
            own = refs[5 * a][0].astype(F32)
            acc = None
            for k in range(4):
                term = jnp.where(sel_ref[0] == k, own, refs[5 * a + 1 + k][0].astype(F32))
                acc = term if acc is None else acc + term
            refs[5 * n + a][...] = acc if keep_t[a] is None else acc.T[:keep_t[a]]

    def slot(p, k):
        return pl.BlockSpec((1,) + p.shape[1:], lambda i, t: (t[k], 0, 0))

    out_shapes = [p.shape[1:] if kt is None else (kt, p.shape[1]) for p, kt in zip(ps, keep_t)]
    grid_spec = pltpu.PrefetchScalarGridSpec(
        num_scalar_prefetch=1, grid=(1,), in_specs=[slot(p, k) for p in ps for k in range(5)],
        out_specs=[pl.BlockSpec(sh, lambda i, t: (0, 0)) for sh in out_shapes])
    operands = [x for p, r in zip(ps, rs) for x in (p, r, r, r, r)]
    return pl.pallas_call(
        body, name=name, grid_spec=grid_spec, out_shape=[jax.ShapeDtypeStruct(sh, F32) for sh in out_shapes],
        compiler_params=_params(("arbitrary",)),
    )(sel, *operands)


def _ada_mod(c16, ada_w_l, ada_b_l):
    def body(c_ref, w_ref, b_ref, o_ref):
        cv = c_ref[...]
        sc = (cv * _sigmoid(cv)).astype(BF16)
        o_ref[...] = _dot(sc, w_ref[...].astype(BF16)) + b_ref[...]

    return pl.pallas_call(
        body, name="ada_mod", out_shape=jax.ShapeDtypeStruct((c16.shape[0], ada_w_l.shape[1]), F32),
        compiler_params=_params(),
    )(c16, ada_w_l, ada_b_l)


def _ada_w_grad(c_t, dmod_my):
    def body(c_ref, d_ref, o_ref):
        cv = c_ref[...]
        sc = cv * _sigmoid(cv)
        acc = sc[:, 0:1] * d_ref[0:1, :]
        for b in range(1, N_DEV):
            acc = acc + sc[:, b:b + 1] * d_ref[b:b + 1, :]
        o_ref[...] = acc

    return pl.pallas_call(
        body, name="ada_w_grad", out_shape=jax.ShapeDtypeStruct((c_t.shape[0], dmod_my.shape[1]), F32),
        compiler_params=_params(),
    )(c_t, dmod_my)


def _norm_mod(x, norm_g, mod, pos_col, invf, sign, rider, rider_inputs):
    s, d = x.shape
    tm = min(TM_MM, s)
    n_in, n_out = len(rider_inputs), len(rider.out_shape)
    steps = s // tm

    def body(x_ref, g_ref, mod_ref, p_ref, f_ref, s_ref, *rest):
        r_ins, (h_ref, ht_ref, cos_ref, sin_ref) = rest[:n_in], rest[n_in:n_in + 4]
        r_outs, sems = rest[n_in + 4:n_in + 4 + n_out], rest[n_in + 4 + n_out:]
        pl.when(pl.program_id(0) == 0)(functools.partial(rider.start, r_ins, r_outs, sems))
        xv = x_ref[...]
        r = lax.rsqrt(jnp.mean(xv * xv, axis=-1, keepdims=True) + EPS)
        hn = xv * r * g_ref[...]
        hv = hn * (1.0 + mod_ref[:, d:2 * d]) + mod_ref[:, 0:d]
        h_ref[...] = hv.astype(BF16)
        ht_ref[...] = hv.T.astype(BF16)
        ang = p_ref[...].astype(F32) * f_ref[...]
        sg = s_ref[...]
        cos_ref[...] = jnp.cos(ang) * jnp.abs(sg)
        sin_ref[...] = jnp.sin(ang) * sg

        @pl.when(pl.program_id(0) == steps - 1)
        def _():
            rider.forward(r_ins, r_outs, sems)
            rider.finish(r_ins, r_outs, sems)

    row = pl.BlockSpec((1, LANES), lambda i: (0, 0))
    tab = pl.BlockSpec((tm, LANES), lambda i: (i, 0))
    return pl.pallas_call(
        body, name="norm_mod", grid=(steps,),
        in_specs=[pl.BlockSpec((tm, d), lambda i: (i, 0)), pl.BlockSpec((1, d), lambda i: (0, 0)),
                  pl.BlockSpec((1, 3 * d), lambda i: (0, 0)), pl.BlockSpec((tm, 1), lambda i: (i, 0)), row, row]
        + [ANY] * n_in,
        out_specs=[pl.BlockSpec((tm, d), lambda i: (i, 0)), pl.BlockSpec((d, tm), lambda i: (0, i)), tab, tab] + [ANY] * n_out,
        out_shape=[jax.ShapeDtypeStruct((s, d), BF16), jax.ShapeDtypeStruct((d, s), BF16),
                   jax.ShapeDtypeStruct((s, LANES), F32), jax.ShapeDtypeStruct((s, LANES), F32)] + rider.out_shape,
        scratch_shapes=rider.scratch, compiler_params=_params(("arbitrary",)),
    )(x, norm_g, mod, pos_col, invf, sign, *rider_inputs)


def _matmul(a, b, *, nt, out_dtype, tm, tn, name, rider=None, rider_inputs=(), a_resident=False):
    m, kdim = a.shape
    n = b.shape[0] if nt else b.shape[1]
    tm, tn = min(tm, m), min(tn, n)
    n_in = len(rider_inputs)
    n_out = len(rider.out_shape) if rider else 0
    m_steps, n_steps = m // tm, n // tn
    steps = n_steps * m_steps
    inner = n_steps if a_resident else m_steps
    tile = (lambda o, i: (o, i)) if a_resident else (lambda o, i: (i, o))

    def body(a_ref, b_ref, *rest):
        r_ins, o_ref, r_outs, sems = rest[:n_in], rest[n_in], rest[n_in + 1:n_in + 1 + n_out], rest[n_in + 1 + n_out:]
        step = pl.program_id(0) * inner + pl.program_id(1)
        if rider:
            pl.when(step == 0)(functools.partial(rider.start, r_ins, r_outs, sems))
            pl.when(step == steps // 2)(functools.partial(rider.forward, r_ins, r_outs, sems))
        o = _dot_nt(a_ref[...], b_ref[...]) if nt else _dot(a_ref[...], b_ref[...])
        o_ref[...] = o.astype(out_dtype)
        if rider:
            pl.when(step == steps - 1)(functools.partial(rider.finish, r_ins, r_outs, sems))

    if nt:
        b_spec = pl.BlockSpec((tn, kdim), lambda o, i: (tile(o, i)[1], 0))
    else:
        b_spec = pl.BlockSpec((kdim, tn), lambda o, i: (0, tile(o, i)[1]))
    out = pl.pallas_call(
        body, name=name, grid=(m_steps, n_steps) if a_resident else (n_steps, m_steps),
        in_specs=[pl.BlockSpec((tm, kdim), lambda o, i: (tile(o, i)[0], 0)), b_spec] + [ANY] * n_in,
        out_specs=[pl.BlockSpec((tm, tn), tile)] + [ANY] * n_out,
        out_shape=[jax.ShapeDtypeStruct((m, n), out_dtype)] + (rider.out_shape if rider else []),
        scratch_shapes=rider.scratch if rider else [],
        compiler_params=_params(("arbitrary", "arbitrary") if rider else ("parallel", "parallel")),
    )(a, b, *rider_inputs)
    return out if rider else out[0]


HALO = 16


def _conv_specs(tm):
    def col(j):
        return pl.BlockSpec((tm, D_CONV), lambda i: (i, j))

    def prev(j):
        return pl.BlockSpec((HALO, D_CONV), lambda i: (jnp.maximum(i * (tm // HALO) - 1, 0), j))

    return [col(0), col(1), col(2), col(3), prev(0), prev(2)]


def _conv_y(xc_ref, bc_ref, cc_ref, zc_ref, xp_ref, cp_ref, w_ref, first):
    uc = cc_ref[...].astype(F32) * xc_ref[...].astype(F32)
    up = jnp.where(first, 0.0, cp_ref[...].astype(F32) * xp_ref[...].astype(F32))
    full = jnp.concatenate([up, uc], axis=0)
    u1 = pltpu.roll(full, 1, 0)[HALO:]
    u2 = pltpu.roll(full, 2, 0)[HALO:]
    w = w_ref[...]
    conv = w[0:1] * u2 + w[1:2] * u1 + w[2:3] * uc
    z = zc_ref[...].astype(F32)
    return bc_ref[...].astype(F32) * conv * (z * _sigmoid(z))


def _conv_bwd(u, dyc, conv_w):
    s = u.shape[0]
    tm = min(TM_MM, s)
    cb = D_CONV
    nt = s // tm

    def body(xc_ref, bc_ref, cc_ref, zc_ref, xp_ref, cp_ref, bn_ref, zn_ref, dy_ref, dyn_ref, w_ref, du_ref, dw_ref):
        i = pl.program_id(0)
        xc, cc = xc_ref[...].astype(F32), cc_ref[...].astype(F32)
        bc, z = bc_ref[...].astype(F32), zc_ref[...].astype(F32)
        uc = cc * xc
        up = jnp.where(i == 0, 0.0, cp_ref[...].astype(F32) * xp_ref[...].astype(F32))
        full = jnp.concatenate([up, uc], axis=0)
        u1 = pltpu.roll(full, 1, 0)[HALO:]
        u2 = pltpu.roll(full, 2, 0)[HALO:]
        w = w_ref[...]
        conv = w[0:1] * u2 + w[1:2] * u1 + w[2:3] * uc
        sg = _sigmoid(z)
        sz = z * sg
        dy = dy_ref[...].astype(F32)
        dconv = dy * bc * sz
        zn = zn_ref[...].astype(F32)
        dnext = dyn_ref[...].astype(F32) * bn_ref[...].astype(F32) * (zn * _sigmoid(zn))
        dnext = jnp.where(i == nt - 1, 0.0, dnext)
        fullb = jnp.concatenate([dconv, dnext], axis=0)
        nb = tm + HALO
        d1 = pltpu.roll(fullb, nb - 1, 0)[:tm]
        d2 = pltpu.roll(fullb, nb - 2, 0)[:tm]
        duc = w[2:3] * dconv + w[1:2] * d1 + w[0:1] * d2
        dzc = dy * bc * conv * (sg * (1.0 + z * (1.0 - sg)))
        du_ref[...] = jnp.concatenate([duc * cc, dy * conv * sz, duc * xc, dzc], axis=1).astype(BF16)
        dw = jnp.concatenate([jnp.sum(dconv * u2, axis=0, keepdims=True), jnp.sum(dconv * u1, axis=0, keepdims=True),
                              jnp.sum(dconv * uc, axis=0, keepdims=True), jnp.zeros((5, cb), F32)], axis=0)

        @pl.when(i == 0)
        def _():
            dw_ref[...] = dw

        @pl.when(i > 0)
        def _():
            dw_ref[...] += dw

    def col(j):
        return pl.BlockSpec((tm, cb), lambda i: (i, j))

    def prev(j):
        return pl.BlockSpec((HALO, cb), lambda i: (jnp.maximum(i * (tm // HALO) - 1, 0), j))

    def nxt(j):
        return pl.BlockSpec((HALO, cb), lambda i: (jnp.minimum((i + 1) * (tm // HALO), s // HALO - 1), j))

    return pl.pallas_call(
        body, name="conv_bwd", grid=(nt,),
        in_specs=[col(0), col(1), col(2), col(3), prev(0), prev(2), nxt(1), nxt(3), col(0), nxt(0),
                  pl.BlockSpec((3, cb), lambda i: (0, 0))],
        out_specs=[pl.BlockSpec((tm, 4 * cb), lambda i: (i, 0)), pl.BlockSpec((8, cb), lambda i: (0, 0))],
        out_shape=[jax.ShapeDtypeStruct((s, U_COLS), BF16), jax.ShapeDtypeStruct((8, cb), F32)],
        compiler_params=_params(("arbitrary",)),
    )(u, u, u, u, u, u, u, u, dyc, dyc, conv_w)


def _qkv_specs(tm):
    return [pl.BlockSpec((tm, Q_LORA), lambda i: (i, U_CQ // Q_LORA)),
            pl.BlockSpec((tm, KV_LORA), lambda i: (i, U_CKV // KV_LORA)),
            pl.BlockSpec((tm, LANES), lambda i: (i, U_KR // LANES)),
            pl.BlockSpec((tm, LANES), lambda i: (i, 0)), pl.BlockSpec((tm, LANES), lambda i: (i, 0))]


def _full(shape):
    return pl.BlockSpec(shape, lambda i: (0,) * len(shape))


def _k_rope_lanes(blk):
    lane = lax.broadcasted_iota(jnp.int32, blk.shape, 1)
    return jnp.where(lane < QK_ROPE, blk, 0.0)


def _qkv_fwd(u, cos, sin, wq, wkv, qag, kvag, qg, kg):
    s = u.shape[0]
    tm = min(TM_MM, s)

    def body(cq_ref, ckv_ref, kr_ref, cos_ref, sin_ref, wq_ref, wkv_ref, qag_ref, kvag_ref, qg_ref, kg_ref,
             q_ref, k_ref, v_ref):
        cq = cq_ref[...].astype(F32)
        cqn = (cq * lax.rsqrt(jnp.mean(cq * cq, axis=-1, keepdims=True) + EPS) * qag_ref[...]).astype(BF16)
        ckv = ckv_ref[...].astype(F32)
        ckvn = (ckv * lax.rsqrt(jnp.mean(ckv * ckv, axis=-1, keepdims=True) + EPS) * kvag_ref[...]).astype(BF16)
        kr = _k_rope_lanes(kr_ref[...].astype(F32))
        cosv, sinv, qgv, kgv = cos_ref[...], sin_ref[...], qg_ref[...], kg_ref[...]
        ss_r = jnp.sum(kr * kr, axis=-1, keepdims=True)
        krr = _rope(kr * kgv[:, QK_NOPE:], cosv, sinv)
        qf = _dot(cqn, wq_ref[...])
        kvf = _dot(ckvn, wkv_ref[...])
        heads = range(N_HEADS)
        qh = [qf[:, QK_PAD * h:QK_PAD * (h + 1)] for h in heads]
        kn = [kvf[:, 2 * V_HEAD * h:2 * V_HEAD * h + QK_NOPE] for h in heads]
        rq = [lax.rsqrt(jnp.sum(qh[h] * qh[h], axis=-1, keepdims=True) * (1.0 / QK_HEAD) + EPS) for h in heads]
        rk = [lax.rsqrt((jnp.sum(kn[h] * kn[h], axis=-1, keepdims=True) + ss_r) * (1.0 / QK_HEAD) + EPS) for h in heads]
        for h in heads:
            qn = qh[h] * rq[h] * qgv
            qo = jnp.concatenate([qn[:, :QK_NOPE], _rope(qn[:, QK_NOPE:], cosv, sinv)], axis=1) * (SCALE * LOG2E)
            q_ref[h] = qo.astype(BF16)
            vh = kvf[:, 2 * V_HEAD * h + QK_NOPE:2 * V_HEAD * (h + 1)]
            k_ref[h] = jnp.concatenate([kn[h] * kgv[:, :QK_NOPE] * rk[h], krr * rk[h]], axis=1).astype(BF16)
            v_ref[h] = jnp.concatenate([vh, jnp.ones_like(vh)], axis=1).astype(BF16)

    return pl.pallas_call(
        body, name="qkv_fwd", grid=(s // tm,),
        in_specs=_qkv_specs(tm) + [_full((Q_LORA, N_HEADS * QK_PAD)), _full((KV_LORA, 2 * D_ATTN)),
                                   _full((1, Q_LORA)), _full((1, KV_LORA)), _full((1, QK_PAD)), _full((1, QK_PAD))],
        out_specs=[pl.BlockSpec((N_HEADS, tm, QK_PAD), lambda i: (0, i, 0)),
                   pl.BlockSpec((N_HEADS, tm, QK_PAD), lambda i: (0, i, 0)),
                   pl.BlockSpec((N_HEADS, tm, 2 * V_HEAD), lambda i: (0, i, 0))],
        out_shape=[jax.ShapeDtypeStruct((N_HEADS, s, QK_PAD), BF16), jax.ShapeDtypeStruct((N_HEADS, s, QK_PAD), BF16),
                   jax.ShapeDtypeStruct((N_HEADS, s, 2 * V_HEAD), BF16)],
        compiler_params=_params(("parallel",)),
    )(u, u, u, cos, sin, wq, wkv, qag, kvag, qg, kg)


def _qkv_bwd(u, cos, sin, dq, dk, dv, dza, wq, wkv, qag, kvag, qg, kg, du):
    s = u.shape[0]
    tm = min(TM_ELEM, s)
    nt = s // tm

    def body(cq_ref, ckv_ref, kr_ref, cos_ref, sin_ref, dq_ref, dk_ref, dv_ref, dza_ref, wq_ref, wkv_ref, qag_ref,
             kvag_ref, qg_ref, kg_ref, du_in, du_ref, dwq_ref, dwkv_ref, dqag_ref, dkvag_ref, dqg_ref, dkg_ref,
             dwq_acc, dwkv_acc):
        del du_in
        i = pl.program_id(0)

        @pl.when(i == 0)
        def _():
            dwq_acc[...] = jnp.zeros_like(dwq_acc)
            dwkv_acc[...] = jnp.zeros_like(dwkv_acc)

        cq = cq_ref[...].astype(F32)
        rqa = lax.rsqrt(jnp.mean(cq * cq, axis=-1, keepdims=True) + EPS)
        xq = cq * rqa
        qagv = qag_ref[...]
        cqn = (xq * qagv).astype(BF16)
        ckv = ckv_ref[...].astype(F32)
        rkva = lax.rsqrt(jnp.mean(ckv * ckv, axis=-1, keepdims=True) + EPS)
        xkv = ckv * rkva
        kvagv = kvag_ref[...]
        ckvn = (xkv * kvagv).astype(BF16)
        kr = _k_rope_lanes(kr_ref[...].astype(F32))
        cosv, sinv, qgv, kgv = cos_ref[...], sin_ref[...], qg_ref[...], kg_ref[...]
        ss_r = jnp.sum(kr * kr, axis=-1, keepdims=True)
        dqg = jnp.zeros((1, QK_PAD), F32)
        dkg = jnp.zeros((1, QK_PAD), F32)
        dkr = jnp.zeros((tm, LANES), F32)
        qf = _dot(cqn, wq_ref[...])
        kvf = _dot(ckvn, wkv_ref[...])
        heads = range(N_HEADS)
        qh = [qf[:, QK_PAD * h:QK_PAD * (h + 1)] for h in heads]
        kn = [kvf[:, 2 * V_HEAD * h:2 * V_HEAD * h + QK_NOPE] for h in heads]
        rq = [lax.rsqrt(jnp.sum(qh[h] * qh[h], axis=-1, keepdims=True) * (1.0 / QK_HEAD) + EPS) for h in heads]
        rk = [lax.rsqrt((jnp.sum(kn[h] * kn[h], axis=-1, keepdims=True) + ss_r) * (1.0 / QK_HEAD) + EPS) for h in heads]
        xh = [qh[h] * rq[h] for h in heads]
        xk = [jnp.concatenate([kn[h], kr], axis=1) * rk[h] for h in heads]
        dyq, dyk = [], []
        for h in heads:
            g = dq_ref[h].astype(F32)
            dyq.append(jnp.concatenate([g[:, :QK_NOPE], _rope_t(g[:, QK_NOPE:], cosv, sinv)], axis=1))
            gk = dk_ref[h].astype(F32)
            dyk.append(jnp.concatenate([gk[:, :QK_NOPE], _rope_t(gk[:, QK_NOPE:], cosv, sinv)], axis=1))
        for h in heads:
            dqg = dqg + jnp.sum(dyq[h] * xh[h], axis=0, keepdims=True)
            dkg = dkg + jnp.sum(dyk[h] * xk[h], axis=0, keepdims=True)
        dqg = dqg * SCALE
        qgv = qgv * SCALE
        gdy = [dyq[h] * qgv for h in heads]
        gdyk = [dyk[h] * kgv for h in heads]
        tq_ = [jnp.sum(gdy[h] * xh[h], axis=-1, keepdims=True) * (1.0 / QK_HEAD) for h in heads]
        tk_ = [jnp.sum(gdyk[h] * xk[h], axis=-1, keepdims=True) * (1.0 / QK_HEAD) for h in heads]
        dqf = [(rq[h] * (gdy[h] - xh[h] * tq_[h])).astype(BF16) for h in heads]
        dkvf = []
        for h in heads:
            dxk = rk[h] * (gdyk[h] - xk[h] * tk_[h])
            dkr = dkr + dxk[:, QK_NOPE:]
            dkvf += [dxk[:, :QK_NOPE].astype(BF16), dv_ref[h]]
        dqf_b, dkvf_b = jnp.concatenate(dqf, axis=1), jnp.concatenate(dkvf, axis=1)
        dwq_acc[...] += _dot_tn(cqn, dqf_b)
        dwkv_acc[...] += _dot_tn(ckvn, dkvf_b)
        dcqn = _dot_nt(dqf_b, wq_ref[...])
        dckvn = _dot_nt(dkvf_b, wkv_ref[...])
        dqag = jnp.sum(dcqn * xq, axis=0, keepdims=True)
        dkvag = jnp.sum(dckvn * xkv, axis=0, keepdims=True)
        gq = dcqn * qagv
        dcq = rqa * (gq - xq * jnp.mean(gq * xq, axis=-1, keepdims=True))
        gkv = dckvn * kvagv
        dckv = rkva * (gkv - xkv * jnp.mean(gkv * xkv, axis=-1, keepdims=True))
        win = pltpu.roll(jnp.concatenate([dza_ref[...].astype(F32), jnp.zeros((tm, LANES), F32)], axis=1), QK_ROPE, 1)
        win = win + jnp.concatenate([dkr, jnp.zeros((tm, D_ATTN), F32)], axis=1)
        du_ref[...] = jnp.concatenate([dcq, dckv, win, jnp.zeros((tm, U_TAIL - ZA_LO - ZA_WIN), F32)], axis=1).astype(BF16)

        @pl.when(i == 0)
        def _():
            dqag_ref[...] = dqag
            dkvag_ref[...] = dkvag
            dqg_ref[...] = dqg
            dkg_ref[...] = dkg

        @pl.when(i > 0)
        def _():
            dqag_ref[...] += dqag
            dkvag_ref[...] += dkvag
            dqg_ref[...] += dqg
            dkg_ref[...] += dkg

        @pl.when(i == nt - 1)
        def _():
            dwq_ref[...] = dwq_acc[...].astype(BF16)
            dwkv_ref[...] = dwkv_acc[...].astype(BF16)

    head = lambda w: pl.BlockSpec((N_HEADS, tm, w), lambda i: (0, i, 0))
    wq_shape, wkv_shape = (Q_LORA, N_HEADS * QK_PAD), (KV_LORA, 2 * D_ATTN)
    return pl.pallas_call(
        body, name="qkv_bwd", grid=(nt,),
        in_specs=_qkv_specs(tm) + [head(QK_PAD), head(QK_PAD), head(V_HEAD), pl.BlockSpec((tm, D_ATTN), lambda i: (i, 0)),
                                   _full(wq_shape), _full(wkv_shape), _full((1, Q_LORA)), _full((1, KV_LORA)),
                                   _full((1, QK_PAD)), _full((1, QK_PAD)), ANY],
        out_specs=[pl.BlockSpec((tm, U_TAIL), lambda i: (i, U_COLS // U_TAIL - 1)), _full(wq_shape), _full(wkv_shape),
                   _full((1, Q_LORA)), _full((1, KV_LORA)), _full((1, QK_PAD)), _full((1, QK_PAD))],
        out_shape=[jax.ShapeDtypeStruct(du.shape, du.dtype), jax.ShapeDtypeStruct(wq_shape, BF16),
                   jax.ShapeDtypeStruct(wkv_shape, BF16), jax.ShapeDtypeStruct((1, Q_LORA), F32),
                   jax.ShapeDtypeStruct((1, KV_LORA), F32), jax.ShapeDtypeStruct((1, QK_PAD), F32),
                   jax.ShapeDtypeStruct((1, QK_PAD), F32)],
        scratch_shapes=[pltpu.VMEM(wq_shape, F32), pltpu.VMEM(wkv_shape, F32)],
        input_output_aliases={15: 0}, compiler_params=_params(("arbitrary",)),
    )(u, u, u, cos, sin, dq, dk, dv, dza, wq, wkv, qag, kvag, qg, kg, du)


def _flash_fwd(q, k, v):
    nh, s, _ = q.shape
    tq = min(TQ, s)
    nkv = KV_SPLIT
    tk = tq // nkv
    nq = s // tq
    nch = Q_CHAINS
    tc = tq // nch

    def body(q_ref, k_ref, v_ref, o_ref, lse_ref):
        i = pl.program_id(1)
        chains = [q_ref[0, r * tc:(r + 1) * tc, :] for r in range(nch)]

        def unit(r, j, carry, shift=None):
            m, acc = carry
            rows = pl.ds(pl.multiple_of(j * tk, tk), tk)
            sc = _dot_nt(chains[r], k_ref[0, rows, :])
            if shift is not None:
                qi = lax.broadcasted_iota(jnp.int32, sc.shape, 0)
                ki = lax.broadcasted_iota(jnp.int32, sc.shape, 1) + shift
                sc = jnp.where(ki <= qi, sc, NEG)
            m_new = jnp.maximum(m, jnp.max(sc, axis=-1, keepdims=True))
            p = jnp.exp2(sc - m_new).astype(BF16)
            return m_new, jnp.exp2(m - m_new) * acc + _dot(p, v_ref[0, rows, :])

        def trip(p, carry):
            for b in range(nkv):
                carry = tuple(unit(r, nkv * p + b, cr) for r, cr in enumerate(carry))
            return carry

        init = (jnp.full((tc, 1), NEG, F32), jnp.zeros((tc, 2 * V_HEAD), F32))
        carry = list(lax.fori_loop(0, i, trip, (init,) * nch))
        for b in range(nkv):
            for r in range(nch):
                shift = b * tk - r * tc
                if shift < tc:
                    carry[r] = unit(r, nkv * i + b, carry[r], None if shift + tk - 1 <= 0 else shift)
        for r, (m, acc) in enumerate(carry):
            l = acc[:, V_HEAD:]
            o_ref[r * tc:(r + 1) * tc, :] = (acc[:, :V_HEAD] / l).astype(BF16)
            lse = m + jnp.log(l[:, 0:1]) * LOG2E
            lse_ref[0, :, r * tc:(r + 1) * tc] = jnp.broadcast_to(lse, (tc, LANES)).T[0:1, :]

    return pl.pallas_call(
        body, name="flash_fwd", grid=(nh, nq),
        in_specs=[pl.BlockSpec((1, tq, QK_PAD), lambda h, i: (h, i, 0)),
                  pl.BlockSpec((1, s, QK_PAD), lambda h, i: (h, 0, 0)),
                  pl.BlockSpec((1, s, 2 * V_HEAD), lambda h, i: (h, 0, 0))],
        out_specs=[pl.BlockSpec((tq, V_HEAD), lambda h, i: (i, h)), pl.BlockSpec((1, 1, tq), lambda h, i: (h, 0, i))],
        out_shape=[jax.ShapeDtypeStruct((s, nh * V_HEAD), BF16), jax.ShapeDtypeStruct((nh, 1, s), F32)],
        compiler_params=_params(("parallel", "arbitrary")),
    )(q, k, v)


def _flash_bwd(q, k, v, do, lse, delta):
    nh, s, _ = q.shape
    tq = min(TQ, s)
    nq = s // tq
    kps = 2 if nq % 2 == 0 else 1
    ng = nq // kps

    def body(q_ref, k_ref, v_ref, do_ref, lse_ref, dl_ref, dq_ref, dk_ref, dv_ref, dq_acc):
        g = ng - 1 - pl.program_id(1)

        @pl.when(g == ng - 1)
        def _():
            dq_acc[...] = jnp.zeros_like(dq_acc)

        for sub in reversed(range(kps)):
            kv_block(q_ref, k_ref, v_ref, do_ref, lse_ref, dl_ref, dk_ref, dv_ref, dq_acc, g * kps + sub, sub)

        @pl.when(g == 0)
        def _():
            dq_ref[0] = dq_acc[...].astype(BF16)

    def kv_block(q_ref, k_ref, v_ref, do_ref, lse_ref, dl_ref, dk_ref, dv_ref, dq_acc, j, sub):
        own = slice(sub * tq, (sub + 1) * tq)
        kj, vj = k_ref[0, own, :], v_ref[0, own, :]

        def block(kk, vv, qq, dd, lse, dl, masked):
            st = _dot_nt(kk, qq)
            pt = jnp.exp2(st - lse)
            if masked:
                ki = lax.broadcasted_iota(jnp.int32, st.shape, 0)
                qx = lax.broadcasted_iota(jnp.int32, st.shape, 1)
                pt = jnp.where(ki <= qx, pt, 0.0)
            ddv = _dot(pt.astype(BF16), dd)
            dst = (pt * (_dot_nt(vv, dd) - dl)).astype(BF16)
            ddq = _dot_tn(dst, kk)
            return _dot(dst, qq), ddv, ddq

        def step(i, carry):
            dk, dv = carry
            rows = pl.ds(pl.multiple_of(i * tq, tq), tq)
            ddk, ddv, ddq = block(kj, vj, q_ref[0, rows, :], do_ref[rows, :], lse_ref[0, pl.ds(i, 1), :],
                                  dl_ref[0, pl.ds(i, 1), :], False)
            dq_acc[rows, :] += ddq
            return dk + ddk, dv + ddv

        th = tq // 2
        lse_j, dl_j = lse_ref[0, pl.ds(j, 1), :], dl_ref[0, pl.ds(j, 1), :]
        parts = []
        for kh, qh, masked in ((0, 0, True), (0, 1, False), (1, 1, True)):
            rows = pl.ds(pl.multiple_of(j * tq + qh * th, th), th)
            ks, qs = slice(kh * th, (kh + 1) * th), slice(qh * th, (qh + 1) * th)
            ddk, ddv, ddq = block(kj[ks], vj[ks], q_ref[0, rows, :], do_ref[rows, :], lse_j[:, qs], dl_j[:, qs], masked)
            dq_acc[rows, :] += ddq
            parts.append((ddk, ddv))
        carry = (jnp.concatenate([parts[0][0] + parts[1][0], parts[2][0]], axis=0),
                 jnp.concatenate([parts[0][1] + parts[1][1], parts[2][1]], axis=0))
        dk, dv = lax.fori_loop(j + 1, nq, step, carry)
        dk_ref[0, own, :] = (dk * LN2).astype(BF16)
        dv_ref[0, own, :] = dv.astype(BF16)

    return pl.pallas_call(
        body, name="flash_bwd", grid=(nh, ng),
        in_specs=[pl.BlockSpec((1, s, QK_PAD), lambda h, j: (h, 0, 0)),
                  pl.BlockSpec((1, kps * tq, QK_PAD), lambda h, g: (h, ng - 1 - g, 0)),
                  pl.BlockSpec((1, kps * tq, V_HEAD), lambda h, g: (h, ng - 1 - g, 0)),
                  pl.BlockSpec((s, V_HEAD), lambda h, j: (0, h)),
                  pl.BlockSpec((1, nq, tq), lambda h, j: (h, 0, 0)),
                  pl.BlockSpec((1, nq, tq), lambda h, j: (h, 0, 0))],
        out_specs=[pl.BlockSpec((1, s, QK_PAD), lambda h, j: (h, 0, 0)),
                   pl.BlockSpec((1, kps * tq, QK_PAD), lambda h, g: (h, ng - 1 - g, 0)),
                   pl.BlockSpec((1, kps * tq, V_HEAD), lambda h, g: (h, ng - 1 - g, 0))],
        out_shape=[jax.ShapeDtypeStruct((nh, s, QK_PAD), BF16), jax.ShapeDtypeStruct((nh, s, QK_PAD), BF16),
                   jax.ShapeDtypeStruct((nh, s, V_HEAD), BF16)],
        scratch_shapes=[pltpu.VMEM((s, QK_PAD), F32)],
        compiler_params=_params(("parallel", "arbitrary")),
    )(q, k, v, do, lse, delta)


def _tail(x, target, o, u, mod, w_out, conv_w):
    s, d = x.shape
    tm = min(TM_ELEM, s)

    def body(x_ref, t_ref, o_ref, za_ref, mod_ref, w_ref, xc_ref, bc_ref, cc_ref, zc_ref, xp_ref, cp_ref, cw_ref,
             gx_ref, dy_ref, ycat_ref, dyc_ref, do_ref, du_ref, delta_ref, dgate_ref, loss_ref):
        i = pl.program_id(0)
        za = pltpu.roll(za_ref[:, ZA_LO:ZA_LO + ZA_WIN].astype(F32), ZA_WIN - QK_ROPE, 1)[:, :D_ATTN]
        ov = o_ref[...].astype(F32)
        sg = _sigmoid(za)
        sl = za * sg
        ya = ov * sl
        y = _dot(ya.astype(BF16), w_ref[D_CONV:, :])
        yc = _conv_y(xc_ref, bc_ref, cc_ref, zc_ref, xp_ref, cp_ref, cw_ref, i == 0)
        y = y + _dot(yc.astype(BF16), w_ref[:D_CONV, :])
        ycat_ref[...] = jnp.concatenate([yc.T, ya.T], axis=0).astype(BF16)
        gate = mod_ref[:, 2 * d:3 * d]
        e = x_ref[...] + gate * y - t_ref[...]
        dout = e * (1.0 / d)
        gx_ref[...] = dout
        dy = (dout * gate).astype(BF16)
        dy_ref[...] = dy
        dycat = _dot_nt(dy, w_ref[...])
        dyc_ref[...] = dycat[:, :D_CONV].astype(BF16)
        dya = dycat[:, D_CONV:]
        dov = dya * sl
        do_ref[...] = dov.astype(BF16)
        du_ref[...] = (dya * ov * (sg * (1.0 + za * (1.0 - sg)))).astype(BF16)
        prod_t = (dov * ov).T
        for h in range(N_HEADS):
            delta_ref[h] = jnp.sum(prod_t[V_HEAD * h:V_HEAD * (h + 1), :], axis=0, keepdims=True)
        dgate = jnp.sum(dout * y, axis=0, keepdims=True)
        part = jnp.sum(jnp.sum(e * e, axis=0, keepdims=True), axis=1, keepdims=True) * (0.5 / d)
        part = jnp.broadcast_to(part, (1, LANES))

        @pl.when(i == 0)
        def _():
            dgate_ref[...] = dgate
            loss_ref[...] = part

        @pl.when(i > 0)
        def _():
            dgate_ref[...] += dgate
            loss_ref[...] += part

    tok = lambda w: pl.BlockSpec((tm, w), lambda i: (i, 0))
    return pl.pallas_call(
        body, name="tail", grid=(s // tm,),
        in_specs=[tok(d), tok(d), tok(D_ATTN), pl.BlockSpec((tm, U_TAIL), lambda i: (i, U_COLS // U_TAIL - 1)),
                  _full((1, 3 * d)), _full((d, d))] + _conv_specs(tm) + [_full((3, D_CONV))],
        out_specs=[tok(d), tok(d), pl.BlockSpec((d, tm), lambda i: (0, i)), tok(D_CONV), tok(D_ATTN), tok(D_ATTN),
                   pl.BlockSpec((N_HEADS, 1, tm), lambda i: (0, 0, i)), _full((1, d)), _full((1, LANES))],
        out_shape=[jax.ShapeDtypeStruct((s, d), F32), jax.ShapeDtypeStruct((s, d), BF16),
                   jax.ShapeDtypeStruct((d, s), BF16), jax.ShapeDtypeStruct((s, D_CONV), BF16),
                   jax.ShapeDtypeStruct((s, D_ATTN), BF16), jax.ShapeDtypeStruct((s, D_ATTN), BF16),
                   jax.ShapeDtypeStruct((N_HEADS, 1, s), F32), jax.ShapeDtypeStruct((1, d), F32),
                   jax.ShapeDtypeStruct((1, LANES), F32)],
        compiler_params=_params(("arbitrary",)),
    )(x, target, o, u, mod, w_out, u, u, u, u, u, u, conv_w)


def _norm_bwd(x, dh, gx1, norm_g, mod):
    s, d = x.shape
    tm = min(TM_MM, s)

    def body(x_ref, dh_ref, gx_ref, g_ref, mod_ref, o_ref, dshift_ref, dscale_ref, dg_ref):
        i = pl.program_id(0)
        gv, sc1 = g_ref[...], 1.0 + mod_ref[:, d:2 * d]
        gsc = gv * sc1
        half = NORM_ROWS // 2

        def group(c, acc):
            a_dh, a_dhxn = acc
            ks = range(NORM_GROUP)
            rows = [pl.ds(pl.multiple_of((c * NORM_GROUP + k) * NORM_ROWS, NORM_ROWS), NORM_ROWS) for k in ks]
            xv = [x_ref[rows[k], :] for k in ks]
            dhv = [dh_ref[rows[k], :].astype(F32) for k in ks]
            r = [lax.rsqrt(jnp.mean(xv[k] * xv[k], axis=-1, keepdims=True) + EPS) for k in ks]
            xn = [xv[k] * r[k] for k in ks]
            dxn = [dhv[k] * gsc for k in ks]
            t = [jnp.mean(dxn[k] * xn[k], axis=-1, keepdims=True) for k in ks]
            for k in ks:
                o_ref[rows[k], :] = gx_ref[rows[k], :] + r[k] * (dxn[k] - xn[k] * t[k])
                dhxn = dhv[k] * xn[k]
                a_dh = a_dh + dhv[k][:half] + dhv[k][half:]
                a_dhxn = a_dhxn + dhxn[:half] + dhxn[half:]
            return a_dh, a_dhxn

        zero = jnp.zeros((half, d), F32)
        a_dh, a_dhxn = lax.fori_loop(0, tm // (NORM_ROWS * NORM_GROUP), group, (zero, zero))
        dshift = jnp.sum(a_dh, axis=0, keepdims=True)
        s_dhxn = jnp.sum(a_dhxn, axis=0, keepdims=True)
        dscale, dg = s_dhxn * gv, s_dhxn * sc1

        @pl.when(i == 0)
        def _():
            dshift_ref[...] = dshift
            dscale_ref[...] = dscale
            dg_ref[...] = dg

        @pl.when(i > 0)
        def _():
            dshift_ref[...] += dshift
            dscale_ref[...] += dscale
            dg_ref[...] += dg

    tok = pl.BlockSpec((tm, d), lambda i: (i, 0))
    row = jax.ShapeDtypeStruct((1, d), F32)
    return pl.pallas_call(
        body, name="norm_bwd", grid=(s // tm,),
        in_specs=[tok, tok, tok, _full((1, d)), _full((1, 3 * d))],
        out_specs=[tok, _full((1, d)), _full((1, d)), _full((1, d))],
        out_shape=[jax.ShapeDtypeStruct((s, d), F32), row, row, row],
        compiler_params=_params(("arbitrary",)),
    )(x, dh, gx1, norm_g, mod)


def _adamw(w, g, m, v, name):
    rows, cols = w.shape
    tr = 256 if rows % 256 == 0 else rows
    tc = 512 if (rows > 256 and tr == rows and cols % 512 == 0) else cols

    def body(w_ref, g_ref, m_ref, v_ref, d_ref, nm_ref, nv_ref):
        gv = g_ref[...]
        nm = ADAM_B1 * m_ref[...] + (1.0 - ADAM_B1) * gv
        nv = ADAM_B2 * v_ref[...] + (1.0 - ADAM_B2) * (gv * gv)
        m_hat = nm / (1.0 - ADAM_B1 ** ADAM_STEP)
        v_hat = nv / (1.0 - ADAM_B2 ** ADAM_STEP)
        d_ref[...] = -ADAM_LR * (m_hat / (jnp.sqrt(v_hat) + ADAM_EPS) + ADAM_WD * w_ref[...])
        nm_ref[...] = nm
        nv_ref[...] = nv

    spec = pl.BlockSpec((tr, tc), lambda i, j: (i, j))
    shape = jax.ShapeDtypeStruct((rows, cols), F32)
    return pl.pallas_call(
        body, name=name, grid=(rows // tr, cols // tc), in_specs=[spec] * 4, out_specs=[spec] * 3, out_shape=[shape] * 3,
        compiler_params=_params(("parallel", "parallel")),
    )(w, g, m, v)


def _pad_cols(a, n):
    return jnp.pad(a, ((0, 0), (0, n - a.shape[1])))


def kernel(x, c, positions, ada_w, ada_b, norm_g, w_in, conv_w, q_a_g, w_q_b, kv_a_g, w_kv_b, q_g, k_g, w_out, loss_target, m_ada_w, m_ada_b, m_norm_g, m_w_in, m_conv_w, m_q_a_g, m_w_q_b, m_kv_a_g, m_w_kv_b, m_q_g, m_k_g, m_w_out, v_ada_w, v_ada_b, v_norm_g, v_w_in, v_conv_w, v_q_a_g, v_w_q_b, v_kv_a_g, v_w_kv_b, v_q_g, v_k_g, v_w_out):
    me = _my_index()
    s = x.shape[1]
    nq = s // min(TQ, s)
    x2, tgt = x[0], loss_target[0]
    w_in_l, w_q_l, w_kv_l, w_out_l, conv_l, ada_w_l = w_in[0], w_q_b[0], w_kv_b[0], w_out[0], conv_w[0], ada_w[0]
    ada_cols = ada_w_l.shape[1]

    small = jnp.concatenate([c.reshape(-1, LANES), conv_l.reshape(-1, LANES), jnp.zeros((5, LANES), F32)], axis=0)
    (small_g,) = _exchange(_GatherDirect([small]), [small], "gather_c")
    c_all = small_g[:, :D_MODEL // LANES].reshape(N_DEV, D_MODEL)
    conv_g = small_g[:, D_MODEL // LANES:D_MODEL // LANES + 3].transpose(1, 0, 2).reshape(3, D_CONV)

    ada_b_l = lax.dynamic_slice(ada_b, (0, me * ada_cols), (1, ada_cols))
    mod_cols = _ada_mod(jnp.pad(c_all, ((0, 8), (0, 0))), ada_w_l, ada_b_l)[:N_DEV]
    (mod_g,) = _exchange(_GatherDirect([mod_cols]), [mod_cols], "gather_mod")
    mod = lax.dynamic_index_in_dim(mod_g, me, axis=1, keepdims=False).reshape(1, 3 * D_MODEL)

    half = jnp.arange(0, QK_ROPE, 2, dtype=F32) / QK_ROPE
    inv_freq = ROPE_BASE ** (-half)
    zeros64 = jnp.zeros((LANES - QK_ROPE,), F32)
    invf = jnp.concatenate([inv_freq, inv_freq, zeros64]).reshape(1, LANES)
    sign = jnp.concatenate([-jnp.ones((32,), F32), jnp.ones((32,), F32), zeros64]).reshape(1, LANES)
    qg_p, kg_p = _pad_cols(q_g, QK_PAD), _pad_cols(k_g, QK_PAD)

    my_off = ((CW * me) % LANES).astype(jnp.int32)
    win = [_expand_w_in(w_in_l.T, my_off.reshape(1))]
    h, h_t, cos, sin, win_g = _norm_mod(x2, norm_g, mod, positions.reshape(s, 1), invf, sign, _Gather(win, relay=True, parts=4), win)
    w_in_p = _merge_w_in(win_g)
    rest = [_pad_wq(w_q_l.T), w_kv_l.astype(BF16), w_out_l.astype(BF16)]
    u, wq_g, wkv_g, w_out_g = _matmul(h, w_in_p, nt=False, out_dtype=BF16, tm=2 * TM_MM, tn=2048, name="in_proj",
                                      rider=_Gather(rest), rider_inputs=rest)
    w_out_g = w_out_g.reshape(D_MODEL, D_MODEL)
    wq_g = wq_g.transpose(1, 0, 2).reshape(Q_LORA, N_HEADS * QK_PAD)
    wkv_g = wkv_g.transpose(1, 0, 2).reshape(KV_LORA, 2 * D_ATTN)
    q, k, v = _qkv_fwd(u, cos, sin, wq_g, wkv_g, q_a_g, kv_a_g, qg_p, kg_p)
    o, lse = _flash_fwd(q, k, v)
    gx1, dy, ycat_t, dyc, do, dza, delta, dgate, loss_row = _tail(x2, tgt, o, u, mod, w_out_g, conv_g)

    dq, dk, dv = _flash_bwd(q, k, v, do, lse.reshape(N_HEADS, nq, s // nq), delta.reshape(N_HEADS, nq, s // nq))
    du, dconv = _conv_bwd(u, dyc, conv_g)
    du, dwq, dwkv, dqag, dkvag, dqg, dkg = _qkv_bwd(u, cos, sin, dq, dk, dv, dza, wq_g, wkv_g, q_a_g, kv_a_g, qg_p, kg_p, du)
    dwq = dwq.reshape(Q_LORA, N_HEADS, QK_PAD).transpose(1, 0, 2)
    dwkv = dwkv.reshape(KV_LORA, N_HEADS, 2 * V_HEAD).transpose(1, 0, 2)
    dw_in = _matmul(h_t, du, nt=False, out_dtype=BF16, tm=TM_MM, tn=768, name="dw_in")
    first = [dw_in, dwq, dwkv]
    dw_out, r_in, r_q, r_kv = _matmul(ycat_t, dy, nt=False, out_dtype=BF16, tm=TM_MM, tn=512, name="dw_out",
                                      rider=_SiblingExchange(first, [True, False, False]), rider_inputs=first)
    dw_out = dw_out.reshape(N_DEV, D_MODEL // N_DEV, D_MODEL)
    (r_out,) = _exchange(_SiblingExchange([dw_out], [False]), [dw_out], "rs_sibling_out")
    core = lax.axis_index("c").astype(jnp.int32)
    lo_tiles = ((CW * (2 * jnp.arange(4, dtype=jnp.int32) + core)) // LANES).astype(jnp.int32)
    pairs = [_add_window(dw_in, r_in, lo_tiles),
             *_add_pairs([dwq, dwkv, dw_out], [r_q, r_kv, r_out], core.reshape(1), "rs_add_rest")]
    dh, *quads = _matmul(du, w_in_p, nt=True, out_dtype=BF16, tm=2 * TM_MM, tn=512, name="dh",
                         rider=_ChipExchange(pairs), rider_inputs=pairs, a_resident=True)
    my_chip = 2 * lax.axis_index("x") + lax.axis_index("y")
    written = jnp.where(jnp.arange(4) == my_chip, (jnp.arange(4) + 1) % 4, jnp.arange(4))
    sel = jnp.concatenate([my_chip.reshape(1), written, ((EXP_W - my_off) % EXP_W).reshape(1)]).astype(jnp.int32)
    g_w_in_t = _final_sum(pairs[0], quads[0], sel, "rs_sum_in", unshift=True, keep_t=CW)
    g_w_q_t, g_w_kv, g_w_out = _final_sums(pairs[1:], quads[1:], sel, [QK_HEAD, None, None], "rs_sum_rest")
    grad_x, dshift, dscale, dng = _norm_bwd(x2, dh, gx1, norm_g, mod)

    row = jnp.concatenate([dshift, dscale, dgate, dng, dqag, dkvag, dqg, dkg, dconv[:3].reshape(1, 3 * D_CONV), loss_row], axis=1)
    (rows_g,) = _exchange(_GatherDirect([row]), [row], "gather_small")
    tot = _sum_leading(rows_g, F32, "sum_small")
    dmod_all = rows_g[:, 0, SM_MOD:SM_NG]
    g_ada_b = tot[:, SM_MOD:SM_NG]
    g_norm_g = tot[:, SM_NG:SM_QAG]
    g_q_a_g = tot[:, SM_QAG:SM_KVAG]
    g_kv_a_g = tot[:, SM_KVAG:SM_QG]
    g_q_g = tot[:, SM_QG:SM_QG + QK_HEAD]
    g_k_g = tot[:, SM_KG:SM_KG + QK_HEAD]
    conv_cols = conv_l.shape[1]
    g_conv = lax.dynamic_slice(tot[:, SM_CONV:SM_LOSS].reshape(3, D_CONV), (0, me * conv_cols), (3, conv_cols))
    loss = tot[0, SM_LOSS]
    dmod_my = lax.dynamic_slice(dmod_all, (0, me * ada_cols), (N_DEV, ada_cols))
    g_ada_w = _ada_w_grad(c_all.T, dmod_my)

    grads = dict(ada_w=g_ada_w, ada_b=g_ada_b, norm_g=g_norm_g, w_in=g_w_in_t, conv_w=g_conv, q_a_g=g_q_a_g, w_q_b=g_w_q_t,
                 kv_a_g=g_kv_a_g, w_kv_b=g_w_kv, q_g=g_q_g, k_g=g_k_g, w_out=g_w_out)
    weights = dict(ada_w=(ada_w, m_ada_w, v_ada_w), ada_b=(ada_b, m_ada_b, v_ada_b), norm_g=(norm_g, m_norm_g, v_norm_g),
                   w_in=(w_in, m_w_in, v_w_in), conv_w=(conv_w, m_conv_w, v_conv_w), q_a_g=(q_a_g, m_q_a_g, v_q_a_g),
                   w_q_b=(w_q_b, m_w_q_b, v_w_q_b), kv_a_g=(kv_a_g, m_kv_a_g, v_kv_a_g), w_kv_b=(w_kv_b, m_w_kv_b, v_w_kv_b),
                   q_g=(q_g, m_q_g, v_q_g), k_g=(k_g, m_k_g, v_k_g), w_out=(w_out, m_w_out, v_w_out))
    names = list(grads)
    out_g, out_d, out_m, out_v = [], [], [], []
    for n in names:
        w, m, v_ = weights[n]
        shape2 = w.shape[-2:] if w.ndim == 3 else (1, w.shape[-1])
        transposed = n in ("w_in", "w_q_b")
        to2 = (lambda a: a.reshape(shape2).T) if transposed else (lambda a: a.reshape(shape2))
        back = (lambda a: a.T.reshape(w.shape)) if transposed else (lambda a: a.reshape(w.shape))
        g2 = grads[n] if transposed else grads[n].reshape(shape2)
        d2, m2, v2 = _adamw(to2(w), g2, to2(m), to2(v_), "adamw_" + n)
        out_g.append(back(g2))
        out_d.append(back(d2))
        out_m.append(back(m2))
        out_v.append(back(v2))
    return (loss, grad_x.reshape(x.shape), *out_g, *out_d, *out_m, *out_v)
```

```python
import functools
import math

import jax
import jax.numpy as jnp
from jax import lax
from jax.experimental import pallas as pl
from jax.experimental.pallas import tpu as pltpu

F32 = jnp.float32
BF16 = jnp.bfloat16
MESH = pl.DeviceIdType.MESH

D_MODEL = 2048
D_CONV = 1024
N_HEADS = 8
QK_NOPE = 128
QK_ROPE = 64
QK_HEAD = QK_NOPE + QK_ROPE
V_HEAD = 128
D_ATTN = N_HEADS * V_HEAD
Q_LORA = 512
KV_LORA = 256
ROPE_BASE = 10000.0
IN_COLS = 4 * D_CONV + Q_LORA + KV_LORA + QK_ROPE + D_ATTN
EPS = 1e-6
ADAM_LR, ADAM_B1, ADAM_B2, ADAM_EPS, ADAM_WD, ADAM_STEP = 0.001, 0.9, 0.999, 1e-08, 0.01, 10

N_DEV = 8
LANES = 128
QK_PAD = 256
U_COLS = 6144
U_CQ, U_CKV, U_KR, U_ZA = 4096, 4608, 4864, 4928
U_TAIL = 2048
ZA_LO = U_ZA - (U_COLS - U_TAIL) - QK_ROPE
ZA_WIN = D_ATTN + LANES
CW = IN_COLS // 8
EXP_W = 896
W_LO = [(CW * d // 128) * 128 for d in range(8)]
W_OFF = [CW * d - lo for d, lo in enumerate(W_LO)]
SCALE = 1.0 / math.sqrt(QK_HEAD)
LOG2E = 1.4426950408889634
LN2 = 0.6931471805599453
NEG = -1e30
VMEM_LIMIT = 56 * 1024 * 1024

TM_ELEM = 256
NORM_ROWS = 16
NORM_GROUP = 4
TM_MM = 512
TQ = 1024
Q_CHAINS = 4
KV_SPLIT = 2

SM_MOD, SM_NG, SM_QAG, SM_KVAG, SM_QG, SM_KG, SM_CONV, SM_LOSS = 0, 6144, 8192, 8704, 8960, 9216, 9472, 12544
SM_COLS = 12672


def _params(sem=None):
    kw = dict(vmem_limit_bytes=VMEM_LIMIT)
    if sem is not None:
        kw["dimension_semantics"] = sem
    return pltpu.CompilerParams(**kw)


def _sigmoid(z):
    return 1.0 / (1.0 + jnp.exp(-z))


def _rot64(x):
    lane = lax.broadcasted_iota(jnp.int32, x.shape, 1)
    return jnp.where(lane < 32, pltpu.roll(x, 96, 1), pltpu.roll(x, 32, 1))


def _rope(x, cos, sin):
    return x * cos + _rot64(x) * sin


def _rope_t(d, cos, sin):
    return d * cos - _rot64(d) * sin


def _dot(a, b):
    return jnp.dot(a, b, preferred_element_type=F32)


def _dot_nt(a, b):
    return lax.dot_general(a, b, (((1,), (1,)), ((), ())), preferred_element_type=F32)


def _dot_tn(a, b):
    return lax.dot_general(a, b, (((0,), (0,)), ((), ())), preferred_element_type=F32)


def _my_index():
    return 4 * lax.axis_index("x") + 2 * lax.axis_index("y") + lax.axis_index("c")


ANY = pl.BlockSpec(memory_space=pl.ANY)


class _Gather:
    def __init__(self, blocks, relay=False, parts=1):
        self.relay = relay
        self.parts = parts
        self.rows = [b.shape[0] // parts for b in blocks]
        self.n = n = len(blocks) * parts
        self.out_shape = [jax.ShapeDtypeStruct((N_DEV,) + b.shape, b.dtype) for b in blocks]
        self.scratch = [pltpu.SemaphoreType.DMA((7 * n,)), pltpu.SemaphoreType.DMA((7 * n,)),
                        pltpu.SemaphoreType.DMA((n,))]

    @staticmethod
    def _places():
        x, y, c = lax.axis_index("x"), lax.axis_index("y"), lax.axis_index("c")
        return (x, y, c), (x, y, 1 - c), [(1 - x, y), (x, 1 - y), (1 - x, 1 - y)]

    def _src(self, ins, a):
        block, part = divmod(a, self.parts)
        return ins[block] if self.parts == 1 else ins[block].at[pl.ds(part * self.rows[block], self.rows[block])]

    def _dst(self, outs, a, place):
        block, part = divmod(a, self.parts)
        ref = outs[block].at[4 * place[0] + 2 * place[1] + place[2]]
        return ref if self.parts == 1 else ref.at[pl.ds(part * self.rows[block], self.rows[block])]

    def _copy(self, outs, sems, a, k, block, to, src=None):
        dst = self._dst(outs, a, block)
        return pltpu.make_async_remote_copy(
            src_ref=dst if src is None else src, dst_ref=dst, send_sem=sems[0].at[7 * a + k],
            recv_sem=sems[1].at[7 * a + k], device_id=to, device_id_type=MESH)

    def _first(self, ins, outs, sems):
        me, sibling, chips = self._places()
        first = []
        for a in range(self.n):
            first.append(self._copy(outs, sems, a, 0, me, sibling, src=self._src(ins, a)))
            first += [self._copy(outs, sems, a, 1 + j, me, (*chip, me[2]), src=self._src(ins, a))
                      for j, chip in enumerate(chips[:2] if self.relay else chips)]
        return first

    def _relays(self, outs, sems):
        if not self.relay:
            return []
        (x, y, c), _, _ = self._places()
        via = (jnp.where(c == 0, 1 - x, x), jnp.where(c == 0, y, 1 - y))
        to = (jnp.where(c == 0, x, 1 - x), jnp.where(c == 0, 1 - y, y))
        return [self._copy(outs, sems, a, 3, (*via, c), (*to, c)) for a in range(self.n)]

    def _passed(self, outs, sems):
        me, sibling, chips = self._places()
        return [self._copy(outs, sems, a, 4 + j, (*chip, me[2]), sibling)
                for a in range(self.n) for j, chip in enumerate(chips)]

    def _mine(self, ins, outs, sems):
        me, _, _ = self._places()
        return [pltpu.make_async_copy(self._src(ins, a), self._dst(outs, a, me), sems[2].at[a]) for a in range(self.n)]

    def start(self, ins, outs, sems):
        for cp in self._mine(ins, outs, sems) + self._first(ins, outs, sems):
            cp.start()

    def forward(self, ins, outs, sems):
        del ins
        me, _, chips = self._places()
        passed, relays = self._passed(outs, sems), self._relays(outs, sems)
        for a in range(self.n):
            for j, chip in enumerate(chips[:2] if self.relay else chips):
                self._copy(outs, sems, a, 1 + j, (*chip, me[2]), me).wait_recv()
                passed[3 * a + j].start()
            if self.relay:
                relays[a].start()
        if self.relay:
            for a in range(self.n):
                self._copy(outs, sems, a, 3, (*chips[2], me[2]), me).wait_recv()
                passed[3 * a + 2].start()

    def finish(self, ins, outs, sems):
        me, sibling, chips = self._places()
        for a in range(self.n):
            self._copy(outs, sems, a, 0, sibling, me).wait_recv()
            for j, chip in enumerate(chips):
                self._copy(outs, sems, a, 4 + j, (*chip, 1 - me[2]), me).wait_recv()
        for cp in self._first(ins, outs, sems) + self._relays(outs, sems) + self._passed(outs, sems):
            cp.wait_send()
        for cp in self._mine(ins, outs, sems):
            cp.wait()


class _GatherDirect:
    FLIPS = [(0, 0, 1), (1, 0, 0), (0, 1, 0), (1, 1, 0), (1, 0, 1), (0, 1, 1), (1, 1, 1)]

    def __init__(self, blocks):
        self.n = n = len(blocks)
        self.out_shape = [jax.ShapeDtypeStruct((N_DEV,) + b.shape, b.dtype) for b in blocks]
        self.scratch = [pltpu.SemaphoreType.DMA((7 * n,)), pltpu.SemaphoreType.DMA((7 * n,)),
                        pltpu.SemaphoreType.DMA((n,))]

    def _copies(self, ins, outs, sems):
        x, y, c = lax.axis_index("x"), lax.axis_index("y"), lax.axis_index("c")
        mine = 4 * x + 2 * y + c
        remote = [pltpu.make_async_remote_copy(
            src_ref=ins[a], dst_ref=outs[a].at[mine], send_sem=sems[0].at[7 * a + k], recv_sem=sems[1].at[7 * a + k],
            device_id=(1 - x if fx else x, 1 - y if fy else y, 1 - c if fc else c), device_id_type=MESH)
            for a in range(self.n) for k, (fx, fy, fc) in enumerate(self.FLIPS)]
        local = [pltpu.make_async_copy(ins[a], outs[a].at[mine], sems[2].at[a]) for a in range(self.n)]
        return remote + local

    def start(self, ins, outs, sems):
        for cp in self._copies(ins, outs, sems):
            cp.start()

    def forward(self, ins, outs, sems):
        pass

    def finish(self, ins, outs, sems):
        for cp in self._copies(ins, outs, sems):
            cp.wait()


class _ChipExchange:
    def __init__(self, arrays):
        self.n = n = len(arrays)
        self.out_shape = [jax.ShapeDtypeStruct(a.shape, a.dtype) for a in arrays]
        self.scratch = [pltpu.SemaphoreType.DMA((3 * n,)), pltpu.SemaphoreType.DMA((3 * n,))]

    def _copies(self, ins, outs, sems):
        x, y, c = lax.axis_index("x"), lax.axis_index("y"), lax.axis_index("c")
        return [pltpu.make_async_remote_copy(
            src_ref=ins[a].at[2 * px + py], dst_ref=outs[a].at[2 * x + y], send_sem=sems[0].at[3 * a + j],
            recv_sem=sems[1].at[3 * a + j], device_id=(px, py, c), device_id_type=MESH)
            for a in range(self.n) for j, (px, py) in enumerate([(1 - x, y), (x, 1 - y), (1 - x, 1 - y)])]

    def start(self, ins, outs, sems):
        for cp in self._copies(ins, outs, sems):
            cp.start()

    def forward(self, ins, outs, sems):
        pass

    def finish(self, ins, outs, sems):
        for cp in self._copies(ins, outs, sems):
            cp.wait()


class _SiblingExchange:
    def __init__(self, arrays, windowed):
        self.n = n = len(arrays)
        self.windowed = windowed
        self.out_shape = [jax.ShapeDtypeStruct((4, a.shape[0], EXP_W) if w else (4,) + a.shape[1:], a.dtype)
                          for a, w in zip(arrays, windowed)]
        self.scratch = [pltpu.SemaphoreType.DMA((4 * n,)), pltpu.SemaphoreType.DMA((4 * n,))]

    def _each(self, ins, outs, sems, act):
        x, y, c = lax.axis_index("x"), lax.axis_index("y"), lax.axis_index("c")

        def branch(c_val):
            for k in range(4):
                e = 2 * k + (1 - c_val)
                for a in range(self.n):
                    src = ins[a].at[:, pl.ds(W_LO[e], EXP_W)] if self.windowed[a] else ins[a].at[e]
                    act(pltpu.make_async_remote_copy(
                        src_ref=src, dst_ref=outs[a].at[k], send_sem=sems[0].at[4 * a + k], recv_sem=sems[1].at[4 * a + k],
                        device_id=(x, y, 1 - c), device_id_type=MESH))

        for c_val in (0, 1):
            pl.when(c == c_val)(functools.partial(branch, c_val))

    def start(self, ins, outs, sems):
        self._each(ins, outs, sems, lambda cp: cp.start())

    def forward(self, ins, outs, sems):
        pass

    def finish(self, ins, outs, sems):
        self._each(ins, outs, sems, lambda cp: cp.wait())


def _exchange(rider, arrays, name):
    n = len(arrays)

    def body(*refs):
        ins, outs, sems = refs[:n], refs[n:n + len(rider.out_shape)], refs[n + len(rider.out_shape):]
        rider.start(ins, outs, sems)
        rider.forward(ins, outs, sems)
        rider.finish(ins, outs, sems)

    return pl.pallas_call(body, name=name, out_shape=rider.out_shape, in_specs=[ANY] * n,
                          out_specs=[ANY] * len(rider.out_shape), scratch_shapes=rider.scratch)(*arrays)


def _add_window(dw_in, recv, lo_tiles):
    k, rows, _ = recv.shape
    tiles = EXP_W // LANES

    def body(t_ref, *refs):
        del t_ref
        r_ref, o_ref = refs[tiles], refs[tiles + 1]
        own = jnp.concatenate([w_ref[...] for w_ref in refs[:tiles]], axis=1)
        o_ref[0] = (own.astype(F32) + r_ref[0].astype(F32)).astype(o_ref.dtype)

    def tile(j):
        return pl.BlockSpec((rows, LANES), lambda i, t: (0, t[i] + j))

    spec = pl.BlockSpec((1, rows, EXP_W), lambda i, t: (i, 0, 0))
    grid_spec = pltpu.PrefetchScalarGridSpec(
        num_scalar_prefetch=1, grid=(k,), in_specs=[tile(j) for j in range(tiles)] + [spec], out_specs=spec)
    return pl.pallas_call(
        body, name="rs_add_in", grid_spec=grid_spec, out_shape=jax.ShapeDtypeStruct(recv.shape, recv.dtype),
        compiler_params=_params(("parallel",)),
    )(lo_tiles, *([dw_in] * tiles), recv)


def _final_sum(p, r, sel, name, unshift=False, keep_t=None):
    _, rows, cols = p.shape
    tr = 512 if rows % 512 == 0 else rows

    def body(sel_ref, p_ref, r0, r1, r2, r3, o_ref):
        own = p_ref[0].astype(F32)
        acc = None
        for k, r_ref in enumerate((r0, r1, r2, r3)):
            term = jnp.where(sel_ref[0] == k, own, r_ref[0].astype(F32))
            acc = term if acc is None else acc + term
        if unshift:
            acc = pltpu.roll(acc, sel_ref[5], 1)
        o_ref[...] = acc if keep_t is None else acc.T[:keep_t]

    def slot(k):
        return pl.BlockSpec((1, tr, cols), lambda i, t: (t[k], i, 0))

    if keep_t is None:
        out_spec, out_shape = pl.BlockSpec((tr, cols), lambda i, t: (i, 0)), (rows, cols)
    else:
        out_spec, out_shape = pl.BlockSpec((keep_t, tr), lambda i, t: (0, i)), (keep_t, rows)
    grid_spec = pltpu.PrefetchScalarGridSpec(
        num_scalar_prefetch=1, grid=(rows // tr,), in_specs=[slot(0), slot(1), slot(2), slot(3), slot(4)],
        out_specs=out_spec)
    return pl.pallas_call(
        body, name=name, grid_spec=grid_spec, out_shape=jax.ShapeDtypeStruct(out_shape, F32),
        compiler_params=_params(("parallel",)),
    )(sel, p, r, r, r, r)


def _expand_w_in(w_t, shift):
    cw, rows = w_t.shape
    tr = TM_MM
    pad = -cw % LANES

    def body(s_ref, w_ref, o_ref):
        w = jnp.concatenate([w_ref[...], jnp.zeros((pad, tr), F32)], axis=0).T
        w = jnp.concatenate([w, jnp.zeros((tr, EXP_W - cw - pad), F32)], axis=1)
        o_ref[...] = pltpu.roll(w, s_ref[0], 1).astype(BF16)

    grid_spec = pltpu.PrefetchScalarGridSpec(
        num_scalar_prefetch=1, grid=(rows // tr,), in_specs=[pl.BlockSpec((cw, tr), lambda i, t: (0, i))],
        out_specs=pl.BlockSpec((tr, EXP_W), lambda i, t: (i, 0)))
    return pl.pallas_call(
        body, name="expand_w_in", grid_spec=grid_spec, out_shape=jax.ShapeDtypeStruct((rows, EXP_W), BF16),
        compiler_params=_params(("arbitrary",)),
    )(shift, w_t)


def _pad_wq(w_t):
    cw, rows = w_t.shape

    def body(w_ref, o_ref):
        o_ref[...] = jnp.concatenate([w_ref[...], jnp.zeros((QK_PAD - cw, rows), F32)], axis=0).T.astype(BF16)

    return pl.pallas_call(
        body, name="pad_wq", out_shape=jax.ShapeDtypeStruct((rows, QK_PAD), BF16), compiler_params=_params(),
    )(w_t)


def _merge_w_in(e):
    _, rows, _ = e.shape
    tr = TM_MM

    def body(e_ref, o_ref):
        for t in range(U_COLS // LANES):
            lo, hi = t * LANES, (t + 1) * LANES
            parts = [e_ref[d, :, lo - W_LO[d]:hi - W_LO[d]] for d in range(N_DEV)
                     if CW * d < hi and CW * (d + 1) > lo]
            if not parts:
                tile = jnp.zeros((tr, LANES), BF16)
            elif len(parts) == 1:
                tile = parts[0]
            else:
                tile = (parts[0].astype(F32) + parts[1].astype(F32)).astype(BF16)
            o_ref[:, lo:hi] = tile

    return pl.pallas_call(
        body, name="merge_w_in", grid=(rows // tr,),
        in_specs=[pl.BlockSpec((N_DEV, tr, EXP_W), lambda i: (0, i, 0))],
        out_specs=pl.BlockSpec((tr, U_COLS), lambda i: (i, 0)), out_shape=jax.ShapeDtypeStruct((rows, U_COLS), BF16),
        compiler_params=_params(("parallel",)),
    )(e)


def _sum_leading(a, out_dtype, name):
    k, rows, cols = a.shape
    tr = min(rows, 1728 if rows % 1728 == 0 else rows)

    def body(a_ref, o_ref):
        acc = a_ref[0].astype(F32)
        for i in range(1, k):
            acc = acc + a_ref[i].astype(F32)
        o_ref[...] = acc.astype(out_dtype)

    return pl.pallas_call(
        body, name=name, grid=(rows // tr,),
        in_specs=[pl.BlockSpec((k, tr, cols), lambda i: (0, i, 0))],
        out_specs=pl.BlockSpec((tr, cols), lambda i: (i, 0)),
        out_shape=jax.ShapeDtypeStruct((rows, cols), out_dtype), compiler_params=_params(("parallel",)),
    )(a)


def _add_pairs(gs, recvs, core, name):
    n = len(gs)

    def body(c_ref, *refs):
        del c_ref
        for a in range(n):
            refs[2 * n + a][...] = (refs[a][...].astype(F32) + refs[n + a][...].astype(F32)).astype(refs[2 * n + a].dtype)

    def mine(r):
        return pl.BlockSpec((1,) + r.shape[1:], lambda i, c: (2 * i + c[0], 0, 0))

    def kth(r):
        return pl.BlockSpec((1,) + r.shape[1:], lambda i, c: (i, 0, 0))

    grid_spec = pltpu.PrefetchScalarGridSpec(
        num_scalar_prefetch=1, grid=(4,), in_specs=[mine(r) for r in recvs] + [kth(r) for r in recvs],
        out_specs=[kth(r) for r in recvs])
    return pl.pallas_call(
        body, name=name, grid_spec=grid_spec, out_shape=[jax.ShapeDtypeStruct(r.shape, r.dtype) for r in recvs],
        compiler_params=_params(("parallel",)),
    )(core, *gs, *recvs)


def _final_sums(ps, rs, sel, keep_t, name):
    n = len(ps)

    def body(sel_ref, *refs):
        for a in range(n):
            own = refs[5 * a][0].astype(F32)
            acc = None
            for k in range(4):
                term = jnp.where(sel_ref[0] == k, own, refs[5 * a + 1 + k][0].astype(F32))
                acc = term if acc is None else acc + term
            refs[5 * n + a][...] = acc if keep_t[a] is None else acc.T[:keep_t[a]]

    def slot(p, k):
        return pl.BlockSpec((1,) + p.shape[1:], lambda i, t: (t[k], 0, 0))

    out_shapes = [p.shape[1:] if kt is None else (kt, p.shape[1]) for p, kt in zip(ps, keep_t)]
    grid_spec = pltpu.PrefetchScalarGridSpec(
        num_scalar_prefetch=1, grid=(1,), in_specs=[slot(p, k) for p in ps for k in range(5)],
        out_specs=[pl.BlockSpec(sh, lambda i, t: (0, 0)) for sh in out_shapes])
    operands = [x for p, r in zip(ps, rs) for x in (p, r, r, r, r)]
    return pl.pallas_call(
        body, name=name, grid_spec=grid_spec, out_shape=[jax.ShapeDtypeStruct(sh, F32) for sh in out_shapes],
        compiler_params=_params(("arbitrary",)),
    )(sel, *operands)


def _ada_mod(c16, ada_w_l, ada_b_l):
    def body(c_ref, w_ref, b_ref, o_ref):
        cv = c_ref[...]
        sc = (cv * _sigmoid(cv)).astype(BF16)
        o_ref[...] = _dot(sc, w_ref[...].astype(BF16)) + b_ref[...]

    return pl.pallas_call(
        body, name="ada_mod", out_shape=jax.ShapeDtypeStruct((c16.shape[0], ada_w_l.shape[1]), F32),
        compiler_params=_params(),
    )(c16, ada_w_l, ada_b_l)


def _ada_w_grad(c_t, dmod_my):
    def body(c_ref, d_ref, o_ref):
        cv = c_ref[...]
        sc = cv * _sigmoid(cv)
        acc = sc[:, 0:1] * d_ref[0:1, :]
        for b in range(1, N_DEV):
            acc = acc + sc[:, b:b + 1] * d_ref[b:b + 1, :]
        o_ref[...] = acc

    return pl.pallas_call(
        body, name="ada_w_grad", out_shape=jax.ShapeDtypeStruct((c_t.shape[0], dmod_my.shape[1]), F32),
        compiler_params=_params(),
    )(c_t, dmod_my)


def _norm_mod(x, norm_g, mod, pos_col, invf, sign, rider, rider_inputs):
    s, d = x.shape
    tm = min(TM_MM, s)
    n_in, n_out = len(rider_inputs), len(rider.out_shape)
    steps = s // tm

    def body(x_ref, g_ref, mod_ref, p_ref, f_ref, s_ref, *rest):
        r_ins, (h_ref, ht_ref, cos_ref, sin_ref) = rest[:n_in], rest[n_in:n_in + 4]
        r_outs, sems = rest[n_in + 4:n_in + 4 + n_out], rest[n_in + 4 + n_out:]
        pl.when(pl.program_id(0) == 0)(functools.partial(rider.start, r_ins, r_outs, sems))
        xv = x_ref[...]
        r = lax.rsqrt(jnp.mean(xv * xv, axis=-1, keepdims=True) + EPS)
        hn = xv * r * g_ref[...]
        hv = hn * (1.0 + mod_ref[:, d:2 * d]) + mod_ref[:, 0:d]
        h_ref[...] = hv.astype(BF16)
        ht_ref[...] = hv.T.astype(BF16)
        ang = p_ref[...].astype(F32) * f_ref[...]
        sg = s_ref[...]
        cos_ref[...] = jnp.cos(ang) * jnp.abs(sg)
        sin_ref[...] = jnp.sin(ang) * sg

        @pl.when(pl.program_id(0) == steps - 1)
        def _():
            rider.forward(r_ins, r_outs, sems)
            rider.finish(r_ins, r_outs, sems)

    row = pl.BlockSpec((1, LANES), lambda i: (0, 0))
    tab = pl.BlockSpec((tm, LANES), lambda i: (i, 0))
    return pl.pallas_call(
        body, name="norm_mod", grid=(steps,),
        in_specs=[pl.BlockSpec((tm, d), lambda i: (i, 0)), pl.BlockSpec((1, d), lambda i: (0, 0)),
                  pl.BlockSpec((1, 3 * d), lambda i: (0, 0)), pl.BlockSpec((tm, 1), lambda i: (i, 0)), row, row]
        + [ANY] * n_in,
        out_specs=[pl.BlockSpec((tm, d), lambda i: (i, 0)), pl.BlockSpec((d, tm), lambda i: (0, i)), tab, tab] + [ANY] * n_out,
        out_shape=[jax.ShapeDtypeStruct((s, d), BF16), jax.ShapeDtypeStruct((d, s), BF16),
                   jax.ShapeDtypeStruct((s, LANES), F32), jax.ShapeDtypeStruct((s, LANES), F32)] + rider.out_shape,
        scratch_shapes=rider.scratch, compiler_params=_params(("arbitrary",)),
    )(x, norm_g, mod, pos_col, invf, sign, *rider_inputs)


def _matmul(a, b, *, nt, out_dtype, tm, tn, name, rider=None, rider_inputs=(), a_resident=False):
    m, kdim = a.shape
    n = b.shape[0] if nt else b.shape[1]
    tm, tn = min(tm, m), min(tn, n)
    n_in = len(rider_inputs)
    n_out = len(rider.out_shape) if rider else 0
    m_steps, n_steps = m // tm, n // tn
    steps = n_steps * m_steps
    inner = n_steps if a_resident else m_steps
    tile = (lambda o, i: (o, i)) if a_resident else (lambda o, i: (i, o))

    def body(a_ref, b_ref, *rest):
        r_ins, o_ref, r_outs, sems = rest[:n_in], rest[n_in], rest[n_in + 1:n_in + 1 + n_out], rest[n_in + 1 + n_out:]
        step = pl.program_id(0) * inner + pl.program_id(1)
        if rider:
            pl.when(step == 0)(functools.partial(rider.start, r_ins, r_outs, sems))
            pl.when(step == steps // 2)(functools.partial(rider.forward, r_ins, r_outs, sems))
        o = _dot_nt(a_ref[...], b_ref[...]) if nt else _dot(a_ref[...], b_ref[...])
        o_ref[...] = o.astype(out_dtype)
        if rider:
            pl.when(step == steps - 1)(functools.partial(rider.finish, r_ins, r_outs, sems))

    if nt:
        b_spec = pl.BlockSpec((tn, kdim), lambda o, i: (tile(o, i)[1], 0))
    else:
        b_spec = pl.BlockSpec((kdim, tn), lambda o, i: (0, tile(o, i)[1]))
    out = pl.pallas_call(
        body, name=name, grid=(m_steps, n_steps) if a_resident else (n_steps, m_steps),
        in_specs=[pl.BlockSpec((tm, kdim), lambda o, i: (tile(o, i)[0], 0)), b_spec] + [ANY] * n_in,
        out_specs=[pl.BlockSpec((tm, tn), tile)] + [ANY] * n_out,
        out_shape=[jax.ShapeDtypeStruct((m, n), out_dtype)] + (rider.out_shape if rider else []),
        scratch_shapes=rider.scratch if rider else [],
        compiler_params=_params(("arbitrary", "arbitrary") if rider else ("parallel", "parallel")),
    )(a, b, *rider_inputs)
    return out if rider else out[0]


HALO = 16


def _conv_specs(tm):
    def col(j):
        return pl.BlockSpec((tm, D_CONV), lambda i: (i, j))

    def prev(j):
        return pl.BlockSpec((HALO, D_CONV), lambda i: (jnp.maximum(i * (tm // HALO) - 1, 0), j))

    return [col(0), col(1), col(2), col(3), prev(0), prev(2)]


def _conv_y(xc_ref, bc_ref, cc_ref, zc_ref, xp_ref, cp_ref, w_ref, first):
    uc = cc_ref[...].astype(F32) * xc_ref[...].astype(F32)
    up = jnp.where(first, 0.0, cp_ref[...].astype(F32) * xp_ref[...].astype(F32))
    full = jnp.concatenate([up, uc], axis=0)
    u1 = pltpu.roll(full, 1, 0)[HALO:]
    u2 = pltpu.roll(full, 2, 0)[HALO:]
    w = w_ref[...]
    conv = w[0:1] * u2 + w[1:2] * u1 + w[2:3] * uc
    z = zc_ref[...].astype(F32)
    return bc_ref[...].astype(F32) * conv * (z * _sigmoid(z))


def _conv_bwd(u, dyc, conv_w):
    s = u.shape[0]
    tm = min(TM_MM, s)
    cb = D_CONV
    nt = s // tm

    def body(xc_ref, bc_ref, cc_ref, zc_ref, xp_ref, cp_ref, bn_ref, zn_ref, dy_ref, dyn_ref, w_ref, du_ref, dw_ref):
        i = pl.program_id(0)
        xc, cc = xc_ref[...].astype(F32), cc_ref[...].astype(F32)
        bc, z = bc_ref[...].astype(F32), zc_ref[...].astype(F32)
        uc = cc * xc
        up = jnp.where(i == 0, 0.0, cp_ref[...].astype(F32) * xp_ref[...].astype(F32))
        full = jnp.concatenate([up, uc], axis=0)
        u1 = pltpu.roll(full, 1, 0)[HALO:]
        u2 = pltpu.roll(full, 2, 0)[HALO:]
        w = w_ref[...]
        conv = w[0:1] * u2 + w[1:2] * u1 + w[2:3] * uc
        sg = _sigmoid(z)
        sz = z * sg
        dy = dy_ref[...].astype(F32)
        dconv = dy * bc * sz
        zn = zn_ref[...].astype(F32)
        dnext = dyn_ref[...].astype(F32) * bn_ref[...].astype(F32) * (zn * _sigmoid(zn))
        dnext = jnp.where(i == nt - 1, 0.0, dnext)
        fullb = jnp.concatenate([dconv, dnext], axis=0)
        nb = tm + HALO
        d1 = pltpu.roll(fullb, nb - 1, 0)[:tm]
        d2 = pltpu.roll(fullb, nb - 2, 0)[:tm]
        duc = w[2:3] * dconv + w[1:2] * d1 + w[0:1] * d2
        dzc = dy * bc * conv * (sg * (1.0 + z * (1.0 - sg)))
        du_ref[...] = jnp.concatenate([duc * cc, dy * conv * sz, duc * xc, dzc], axis=1).astype(BF16)
        dw = jnp.concatenate([jnp.sum(dconv * u2, axis=0, keepdims=True), jnp.sum(dconv * u1, axis=0, keepdims=True),
                              jnp.sum(dconv * uc, axis=0, keepdims=True), jnp.zeros((5, cb), F32)], axis=0)

        @pl.when(i == 0)
        def _():
            dw_ref[...] = dw

        @pl.when(i > 0)
        def _():
            dw_ref[...] += dw

    def col(j):
        return pl.BlockSpec((tm, cb), lambda i: (i, j))

    def prev(j):
        return pl.BlockSpec((HALO, cb), lambda i: (jnp.maximum(i * (tm // HALO) - 1, 0), j))

    def nxt(j):
        return pl.BlockSpec((HALO, cb), lambda i: (jnp.minimum((i + 1) * (tm // HALO), s // HALO - 1), j))

    return pl.pallas_call(
        body, name="conv_bwd", grid=(nt,),
        in_specs=[col(0), col(1), col(2), col(3), prev(0), prev(2), nxt(1), nxt(3), col(0), nxt(0),
                  pl.BlockSpec((3, cb), lambda i: (0, 0))],
        out_specs=[pl.BlockSpec((tm, 4 * cb), lambda i: (i, 0)), pl.BlockSpec((8, cb), lambda i: (0, 0))],
        out_shape=[jax.ShapeDtypeStruct((s, U_COLS), BF16), jax.ShapeDtypeStruct((8, cb), F32)],
        compiler_params=_params(("arbitrary",)),
    )(u, u, u, u, u, u, u, u, dyc, dyc, conv_w)


def _qkv_specs(tm):
    return [pl.BlockSpec((tm, Q_LORA), lambda i: (i, U_CQ // Q_LORA)),
            pl.BlockSpec((tm, KV_LORA), lambda i: (i, U_CKV // KV_LORA)),
            pl.BlockSpec((tm, LANES), lambda i: (i, U_KR // LANES)),
            pl.BlockSpec((tm, LANES), lambda i: (i, 0)), pl.BlockSpec((tm, LANES), lambda i: (i, 0))]


def _full(shape):
    return pl.BlockSpec(shape, lambda i: (0,) * len(shape))


def _k_rope_lanes(blk):
    lane = lax.broadcasted_iota(jnp.int32, blk.shape, 1)
    return jnp.where(lane < QK_ROPE, blk, 0.0)


def _qkv_fwd(u, cos, sin, wq, wkv, qag, kvag, qg, kg):
    s = u.shape[0]
    tm = min(TM_MM, s)

    def body(cq_ref, ckv_ref, kr_ref, cos_ref, sin_ref, wq_ref, wkv_ref, qag_ref, kvag_ref, qg_ref, kg_ref,
             q_ref, k_ref, v_ref):
        cq = cq_ref[...].astype(F32)
        cqn = (cq * lax.rsqrt(jnp.mean(cq * cq, axis=-1, keepdims=True) + EPS) * qag_ref[...]).astype(BF16)
        ckv = ckv_ref[...].astype(F32)
        ckvn = (ckv * lax.rsqrt(jnp.mean(ckv * ckv, axis=-1, keepdims=True) + EPS) * kvag_ref[...]).astype(BF16)
        kr = _k_rope_lanes(kr_ref[...].astype(F32))
        cosv, sinv, qgv, kgv = cos_ref[...], sin_ref[...], qg_ref[...], kg_ref[...]
        ss_r = jnp.sum(kr * kr, axis=-1, keepdims=True)
        krr = _rope(kr * kgv[:, QK_NOPE:], cosv, sinv)
        qf = _dot(cqn, wq_ref[...])
        kvf = _dot(ckvn, wkv_ref[...])
        heads = range(N_HEADS)
        qh = [qf[:, QK_PAD * h:QK_PAD * (h + 1)] for h in heads]
        kn = [kvf[:, 2 * V_HEAD * h:2 * V_HEAD * h + QK_NOPE] for h in heads]
        rq = [lax.rsqrt(jnp.sum(qh[h] * qh[h], axis=-1, keepdims=True) * (1.0 / QK_HEAD) + EPS) for h in heads]
        rk = [lax.rsqrt((jnp.sum(kn[h] * kn[h], axis=-1, keepdims=True) + ss_r) * (1.0 / QK_HEAD) + EPS) for h in heads]
        for h in heads:
            qn = qh[h] * rq[h] * qgv
            qo = jnp.concatenate([qn[:, :QK_NOPE], _rope(qn[:, QK_NOPE:], cosv, sinv)], axis=1) * (SCALE * LOG2E)
            q_ref[h] = qo.astype(BF16)
            vh = kvf[:, 2 * V_HEAD * h + QK_NOPE:2 * V_HEAD * (h + 1)]
            k_ref[h] = jnp.concatenate([kn[h] * kgv[:, :QK_NOPE] * rk[h], krr * rk[h]], axis=1).astype(BF16)
            v_ref[h] = jnp.concatenate([vh, jnp.ones_like(vh)], axis=1).astype(BF16)

    return pl.pallas_call(
        body, name="qkv_fwd", grid=(s // tm,),
        in_specs=_qkv_specs(tm) + [_full((Q_LORA, N_HEADS * QK_PAD)), _full((KV_LORA, 2 * D_ATTN)),
                                   _full((1, Q_LORA)), _full((1, KV_LORA)), _full((1, QK_PAD)), _full((1, QK_PAD))],
        out_specs=[pl.BlockSpec((N_HEADS, tm, QK_PAD), lambda i: (0, i, 0)),
                   pl.BlockSpec((N_HEADS, tm, QK_PAD), lambda i: (0, i, 0)),
                   pl.BlockSpec((N_HEADS, tm, 2 * V_HEAD), lambda i: (0, i, 0))],
        out_shape=[jax.ShapeDtypeStruct((N_HEADS, s, QK_PAD), BF16), jax.ShapeDtypeStruct((N_HEADS, s, QK_PAD), BF16),
                   jax.ShapeDtypeStruct((N_HEADS, s, 2 * V_HEAD), BF16)],
        compiler_params=_params(("parallel",)),
    )(u, u, u, cos, sin, wq, wkv, qag, kvag, qg, kg)


def _qkv_bwd(u, cos, sin, dq, dk, dv, dza, wq, wkv, qag, kvag, qg, kg, du):
    s = u.shape[0]
    tm = min(TM_ELEM, s)
    nt = s // tm

    def body(cq_ref, ckv_ref, kr_ref, cos_ref, sin_ref, dq_ref, dk_ref, dv_ref, dza_ref, wq_ref, wkv_ref, qag_ref,
             kvag_ref, qg_ref, kg_ref, du_in, du_ref, dwq_ref, dwkv_ref, dqag_ref, dkvag_ref, dqg_ref, dkg_ref,
             dwq_acc, dwkv_acc):
        del du_in
        i = pl.program_id(0)

        @pl.when(i == 0)
        def _():
            dwq_acc[...] = jnp.zeros_like(dwq_acc)
            dwkv_acc[...] = jnp.zeros_like(dwkv_acc)

        cq = cq_ref[...].astype(F32)
        rqa = lax.rsqrt(jnp.mean(cq * cq, axis=-1, keepdims=True) + EPS)
        xq = cq * rqa
        qagv = qag_ref[...]
        cqn = (xq * qagv).astype(BF16)
        ckv = ckv_ref[...].astype(F32)
        rkva = lax.rsqrt(jnp.mean(ckv * ckv, axis=-1, keepdims=True) + EPS)
        xkv = ckv * rkva
        kvagv = kvag_ref[...]
        ckvn = (xkv * kvagv).astype(BF16)
        kr = _k_rope_lanes(kr_ref[...].astype(F32))
        cosv, sinv, qgv, kgv = cos_ref[...], sin_ref[...], qg_ref[...], kg_ref[...]
        ss_r = jnp.sum(kr * kr, axis=-1, keepdims=True)
        dqg = jnp.zeros((1, QK_PAD), F32)
        dkg = jnp.zeros((1, QK_PAD), F32)
        dkr = jnp.zeros((tm, LANES), F32)
        qf = _dot(cqn, wq_ref[...])
        kvf = _dot(ckvn, wkv_ref[...])
        heads = range(N_HEADS)
        qh = [qf[:, QK_PAD * h:QK_PAD * (h + 1)] for h in heads]
        kn = [kvf[:, 2 * V_HEAD * h:2 * V_HEAD * h + QK_NOPE] for h in heads]
        rq = [lax.rsqrt(jnp.sum(qh[h] * qh[h], axis=-1, keepdims=True) * (1.0 / QK_HEAD) + EPS) for h in heads]
        rk = [lax.rsqrt((jnp.sum(kn[h] * kn[h], axis=-1, keepdims=True) + ss_r) * (1.0 / QK_HEAD) + EPS) for h in heads]
        xh = [qh[h] * rq[h] for h in heads]
        xk = [jnp.concatenate([kn[h], kr], axis=1) * rk[h] for h in heads]
        dyq, dyk = [], []
        for h in heads:
            g = dq_ref[h].astype(F32)
            dyq.append(jnp.concatenate([g[:, :QK_NOPE], _rope_t(g[:, QK_NOPE:], cosv, sinv)], axis=1))
            gk = dk_ref[h].astype(F32)
            dyk.append(jnp.concatenate([gk[:, :QK_NOPE], _rope_t(gk[:, QK_NOPE:], cosv, sinv)], axis=1))
        for h in heads:
            dqg = dqg + jnp.sum(dyq[h] * xh[h], axis=0, keepdims=True)
            dkg = dkg + jnp.sum(dyk[h] * xk[h], axis=0, keepdims=True)
        dqg = dqg * SCALE
        qgv = qgv * SCALE
        gdy = [dyq[h] * qgv for h in heads]
        gdyk = [dyk[h] * kgv for h in heads]
        tq_ = [jnp.sum(gdy[h] * xh[h], axis=-1, keepdims=True) * (1.0 / QK_HEAD) for h in heads]
        tk_ = [jnp.sum(gdyk[h] * xk[h], axis=-1, keepdims=True) * (1.0 / QK_HEAD) for h in heads]
        dqf = [(rq[h] * (gdy[h] - xh[h] * tq_[h])).astype(BF16) for h in heads]
        dkvf = []
        for h in heads:
            dxk = rk[h] * (gdyk[h] - xk[h] * tk_[h])
            dkr = dkr + dxk[:, QK_NOPE:]
            dkvf += [dxk[:, :QK_NOPE].astype(BF16), dv_ref[h]]
        dqf_b, dkvf_b = jnp.concatenate(dqf, axis=1), jnp.concatenate(dkvf, axis=1)
        dwq_acc[...] += _dot_tn(cqn, dqf_b)
        dwkv_acc[...] += _dot_tn(ckvn, dkvf_b)
        dcqn = _dot_nt(dqf_b, wq_ref[...])
        dckvn = _dot_nt(dkvf_b, wkv_ref[...])
        dqag = jnp.sum(dcqn * xq, axis=0, keepdims=True)
        dkvag = jnp.sum(dckvn * xkv, axis=0, keepdims=True)
        gq = dcqn * qagv
        dcq = rqa * (gq - xq * jnp.mean(gq * xq, axis=-1, keepdims=True))
        gkv = dckvn * kvagv
        dckv = rkva * (gkv - xkv * jnp.mean(gkv * xkv, axis=-1, keepdims=True))
        win = pltpu.roll(jnp.concatenate([dza_ref[...].astype(F32), jnp.zeros((tm, LANES), F32)], axis=1), QK_ROPE, 1)
        win = win + jnp.concatenate([dkr, jnp.zeros((tm, D_ATTN), F32)], axis=1)
        du_ref[...] = jnp.concatenate([dcq, dckv, win, jnp.zeros((tm, U_TAIL - ZA_LO - ZA_WIN), F32)], axis=1).astype(BF16)

        @pl.when(i == 0)
        def _():
            dqag_ref[...] = dqag
            dkvag_ref[...] = dkvag
            dqg_ref[...] = dqg
            dkg_ref[...] = dkg

        @pl.when(i > 0)
        def _():
            dqag_ref[...] += dqag
            dkvag_ref[...] += dkvag
            dqg_ref[...] += dqg
            dkg_ref[...] += dkg

        @pl.when(i == nt - 1)
        def _():
            dwq_ref[...] = dwq_acc[...].astype(BF16)
            dwkv_ref[...] = dwkv_acc[...].astype(BF16)

    head = lambda w: pl.BlockSpec((N_HEADS, tm, w), lambda i: (0, i, 0))
    wq_shape, wkv_shape = (Q_LORA, N_HEADS * QK_PAD), (KV_LORA, 2 * D_ATTN)
    return pl.pallas_call(
        body, name="qkv_bwd", grid=(nt,),
        in_specs=_qkv_specs(tm) + [head(QK_PAD), head(QK_PAD), head(V_HEAD), pl.BlockSpec((tm, D_ATTN), lambda i: (i, 0)),
                                   _full(wq_shape), _full(wkv_shape), _full((1, Q_LORA)), _full((1, KV_LORA)),
                                   _full((1, QK_PAD)), _full((1, QK_PAD)), ANY],
        out_specs=[pl.BlockSpec((tm, U_TAIL), lambda i: (i, U_COLS // U_TAIL - 1)), _full(wq_shape), _full(wkv_shape),
                   _full((1, Q_LORA)), _full((1, KV_LORA)), _full((1, QK_PAD)), _full((1, QK_PAD))],
        out_shape=[jax.ShapeDtypeStruct(du.shape, du.dtype), jax.ShapeDtypeStruct(wq_shape, BF16),
                   jax.ShapeDtypeStruct(wkv_shape, BF16), jax.ShapeDtypeStruct((1, Q_LORA), F32),
                   jax.ShapeDtypeStruct((1, KV_LORA), F32), jax.ShapeDtypeStruct((1, QK_PAD), F32),
                   jax.ShapeDtypeStruct((1, QK_PAD), F32)],
        scratch_shapes=[pltpu.VMEM(wq_shape, F32), pltpu.VMEM(wkv_shape, F32)],
        input_output_aliases={15: 0}, compiler_params=_params(("arbitrary",)),
    )(u, u, u, cos, sin, dq, dk, dv, dza, wq, wkv, qag, kvag, qg, kg, du)


def _flash_fwd(q, k, v):
    nh, s, _ = q.shape
    tq = min(TQ, s)
    nkv = KV_SPLIT
    tk = tq // nkv
    nq = s // tq
    nch = Q_CHAINS
    tc = tq // nch

    def body(q_ref, k_ref, v_ref, o_ref, lse_ref):
        i = pl.program_id(1)
        chains = [q_ref[0, r * tc:(r + 1) * tc, :] for r in range(nch)]

        def unit(r, j, carry, shift=None):
            m, acc = carry
            rows = pl.ds(pl.multiple_of(j * tk, tk), tk)
            sc = _dot_nt(chains[r], k_ref[0, rows, :])
            if shift is not None:
                qi = lax.broadcasted_iota(jnp.int32, sc.shape, 0)
                ki = lax.broadcasted_iota(jnp.int32, sc.shape, 1) + shift
                sc = jnp.where(ki <= qi, sc, NEG)
            m_new = jnp.maximum(m, jnp.max(sc, axis=-1, keepdims=True))
            p = jnp.exp2(sc - m_new).astype(BF16)
            return m_new, jnp.exp2(m - m_new) * acc + _dot(p, v_ref[0, rows, :])

        def trip(p, carry):
            for b in range(nkv):
                carry = tuple(unit(r, nkv * p + b, cr) for r, cr in enumerate(carry))
            return carry

        init = (jnp.full((tc, 1), NEG, F32), jnp.zeros((tc, 2 * V_HEAD), F32))
        carry = list(lax.fori_loop(0, i, trip, (init,) * nch))
        for b in range(nkv):
            for r in range(nch):
                shift = b * tk - r * tc
                if shift < tc:
                    carry[r] = unit(r, nkv * i + b, carry[r], None if shift + tk - 1 <= 0 else shift)
        for r, (m, acc) in enumerate(carry):
            l = acc[:, V_HEAD:]
            o_ref[r * tc:(r + 1) * tc, :] = (acc[:, :V_HEAD] / l).astype(BF16)
            lse = m + jnp.log(l[:, 0:1]) * LOG2E
            lse_ref[0, :, r * tc:(r + 1) * tc] = jnp.broadcast_to(lse, (tc, LANES)).T[0:1, :]

    return pl.pallas_call(
        body, name="flash_fwd", grid=(nh, nq),
        in_specs=[pl.BlockSpec((1, tq, QK_PAD), lambda h, i: (h, i, 0)),
                  pl.BlockSpec((1, s, QK_PAD), lambda h, i: (h, 0, 0)),
                  pl.BlockSpec((1, s, 2 * V_HEAD), lambda h, i: (h, 0, 0))],
        out_specs=[pl.BlockSpec((tq, V_HEAD), lambda h, i: (i, h)), pl.BlockSpec((1, 1, tq), lambda h, i: (h, 0, i))],
        out_shape=[jax.ShapeDtypeStruct((s, nh * V_HEAD), BF16), jax.ShapeDtypeStruct((nh, 1, s), F32)],
        compiler_params=_params(("parallel", "arbitrary")),
    )(q, k, v)


def _flash_bwd(q, k, v, do, lse, delta):
    nh, s, _ = q.shape
    tq = min(TQ, s)
    nq = s // tq
    kps = 2 if nq % 2 == 0 else 1
    ng = nq // kps

    def body(q_ref, k_ref, v_ref, do_ref, lse_ref, dl_ref, dq_ref, dk_ref, dv_ref, dq_acc):
        g = ng - 1 - pl.program_id(1)

        @pl.when(g == ng - 1)
        def _():
            dq_acc[...] = jnp.zeros_like(dq_acc)

        for sub in reversed(range(kps)):
            kv_block(q_ref, k_ref, v_ref, do_ref, lse_ref, dl_ref, dk_ref, dv_ref, dq_acc, g * kps + sub, sub)

        @pl.when(g == 0)
        def _():
            dq_ref[0] = dq_acc[...].astype(BF16)

    def kv_block(q_ref, k_ref, v_ref, do_ref, lse_ref, dl_ref, dk_ref, dv_ref, dq_acc, j, sub):
        own = slice(sub * tq, (sub + 1) * tq)
        kj, vj = k_ref[0, own, :], v_ref[0, own, :]

        def block(kk, vv, qq, dd, lse, dl, masked):
            st = _dot_nt(kk, qq)
            pt = jnp.exp2(st - lse)
            if masked:
                ki = lax.broadcasted_iota(jnp.int32, st.shape, 0)
                qx = lax.broadcasted_iota(jnp.int32, st.shape, 1)
                pt = jnp.where(ki <= qx, pt, 0.0)
            ddv = _dot(pt.astype(BF16), dd)
            dst = (pt * (_dot_nt(vv, dd) - dl)).astype(BF16)
            ddq = _dot_tn(dst, kk)
            return _dot(dst, qq), ddv, ddq

        def step(i, carry):
            dk, dv = carry
            rows = pl.ds(pl.multiple_of(i * tq, tq), tq)
            ddk, ddv, ddq = block(kj, vj, q_ref[0, rows, :], do_ref[rows, :], lse_ref[0, pl.ds(i, 1), :],
                                  dl_ref[0, pl.ds(i, 1), :], False)
            dq_acc[rows, :] += ddq
            return dk + ddk, dv + ddv

        th = tq // 2
        lse_j, dl_j = lse_ref[0, pl.ds(j, 1), :], dl_ref[0, pl.ds(j, 1), :]
        parts = []
        for kh, qh, masked in ((0, 0, True), (0, 1, False), (1, 1, True)):
            rows = pl.ds(pl.multiple_of(j * tq + qh * th, th), th)
            ks, qs = slice(kh * th, (kh + 1) * th), slice(qh * th, (qh + 1) * th)
            ddk, ddv, ddq = block(kj[ks], vj[ks], q_ref[0, rows, :], do_ref[rows, :], lse_j[:, qs], dl_j[:, qs], masked)
            dq_acc[rows, :] += ddq
            parts.append((ddk, ddv))
        carry = (jnp.concatenate([parts[0][0] + parts[1][0], parts[2][0]], axis=0),
                 jnp.concatenate([parts[0][1] + parts[1][1], parts[2][1]], axis=0))
        dk, dv = lax.fori_loop(j + 1, nq, step, carry)
        dk_ref[0, own, :] = (dk * LN2).astype(BF16)
        dv_ref[0, own, :] = dv.astype(BF16)

    return pl.pallas_call(
        body, name="flash_bwd", grid=(nh, ng),
        in_specs=[pl.BlockSpec((1, s, QK_PAD), lambda h, j: (h, 0, 0)),
                  pl.BlockSpec((1, kps * tq, QK_PAD), lambda h, g: (h, ng - 1 - g, 0)),
                  pl.BlockSpec((1, kps * tq, V_HEAD), lambda h, g: (h, ng - 1 - g, 0)),
                  pl.BlockSpec((s, V_HEAD), lambda h, j: (0, h)),
                  pl.BlockSpec((1, nq, tq), lambda h, j: (h, 0, 0)),
                  pl.BlockSpec((1, nq, tq), lambda h, j: (h, 0, 0))],
        out_specs=[pl.BlockSpec((1, s, QK_PAD), lambda h, j: (h, 0, 0)),
                   pl.BlockSpec((1, kps * tq, QK_PAD), lambda h, g: (h, ng - 1 - g, 0)),
                   pl.BlockSpec((1, kps * tq, V_HEAD), lambda h, g: (h, ng - 1 - g, 0))],
        out_shape=[jax.ShapeDtypeStruct((nh, s, QK_PAD), BF16), jax.ShapeDtypeStruct((nh, s, QK_PAD), BF16),
                   jax.ShapeDtypeStruct((nh, s, V_HEAD), BF16)],
        scratch_shapes=[pltpu.VMEM((s, QK_PAD), F32)],
        compiler_params=_params(("parallel", "arbitrary")),
    )(q, k, v, do, lse, delta)


def _tail(x, target, o, u, mod, w_out, conv_w):
    s, d = x.shape
    tm = min(TM_ELEM, s)

    def body(x_ref, t_ref, o_ref, za_ref, mod_ref, w_ref, xc_ref, bc_ref, cc_ref, zc_ref, xp_ref, cp_ref, cw_ref,
             gx_ref, dy_ref, ycat_ref, dyc_ref, do_ref, du_ref, delta_ref, dgate_ref, loss_ref):
        i = pl.program_id(0)
        za = pltpu.roll(za_ref[:, ZA_LO:ZA_LO + ZA_WIN].astype(F32), ZA_WIN - QK_ROPE, 1)[:, :D_ATTN]
        ov = o_ref[...].astype(F32)
        sg = _sigmoid(za)
        sl = za * sg
        ya = ov * sl
        y = _dot(ya.astype(BF16), w_ref[D_CONV:, :])
        yc = _conv_y(xc_ref, bc_ref, cc_ref, zc_ref, xp_ref, cp_ref, cw_ref, i == 0)
        y = y + _dot(yc.astype(BF16), w_ref[:D_CONV, :])
        ycat_ref[...] = jnp.concatenate([yc.T, ya.T], axis=0).astype(BF16)
        gate = mod_ref[:, 2 * d:3 * d]
        e = x_ref[...] + gate * y - t_ref[...]
        dout = e * (1.0 / d)
        gx_ref[...] = dout
        dy = (dout * gate).astype(BF16)
        dy_ref[...] = dy
        dycat = _dot_nt(dy, w_ref[...])
        dyc_ref[...] = dycat[:, :D_CONV].astype(BF16)
        dya = dycat[:, D_CONV:]
        dov = dya * sl
        do_ref[...] = dov.astype(BF16)
        du_ref[...] = (dya * ov * (sg * (1.0 + za * (1.0 - sg)))).astype(BF16)
        prod_t = (dov * ov).T
        for h in range(N_HEADS):
            delta_ref[h] = jnp.sum(prod_t[V_HEAD * h:V_HEAD * (h + 1), :], axis=0, keepdims=True)
        dgate = jnp.sum(dout * y, axis=0, keepdims=True)
        part = jnp.sum(jnp.sum(e * e, axis=0, keepdims=True), axis=1, keepdims=True) * (0.5 / d)
        part = jnp.broadcast_to(part, (1, LANES))

        @pl.when(i == 0)
        def _():
            dgate_ref[...] = dgate
            loss_ref[...] = part

        @pl.when(i > 0)
        def _():
            dgate_ref[...] += dgate
            loss_ref[...] += part

    tok = lambda w: pl.BlockSpec((tm, w), lambda i: (i, 0))
    return pl.pallas_call(
        body, name="tail", grid=(s // tm,),
        in_specs=[tok(d), tok(d), tok(D_ATTN), pl.BlockSpec((tm, U_TAIL), lambda i: (i, U_COLS // U_TAIL - 1)),
                  _full((1, 3 * d)), _full((d, d))] + _conv_specs(tm) + [_full((3, D_CONV))],
        out_specs=[tok(d), tok(d), pl.BlockSpec((d, tm), lambda i: (0, i)), tok(D_CONV), tok(D_ATTN), tok(D_ATTN),
                   pl.BlockSpec((N_HEADS, 1, tm), lambda i: (0, 0, i)), _full((1, d)), _full((1, LANES))],
        out_shape=[jax.ShapeDtypeStruct((s, d), F32), jax.ShapeDtypeStruct((s, d), BF16),
                   jax.ShapeDtypeStruct((d, s), BF16), jax.ShapeDtypeStruct((s, D_CONV), BF16),
                   jax.ShapeDtypeStruct((s, D_ATTN), BF16), jax.ShapeDtypeStruct((s, D_ATTN), BF16),
                   jax.ShapeDtypeStruct((N_HEADS, 1, s), F32), jax.ShapeDtypeStruct((1, d), F32),
                   jax.ShapeDtypeStruct((1, LANES), F32)],
        compiler_params=_params(("arbitrary",)),
    )(x, target, o, u, mod, w_out, u, u, u, u, u, u, conv_w)


def _norm_bwd(x, dh, gx1, norm_g, mod):
    s, d = x.shape
    tm = min(TM_MM, s)

    def body(x_ref, dh_ref, gx_ref, g_ref, mod_ref, o_ref, dshift_ref, dscale_ref, dg_ref):
        i = pl.program_id(0)
        gv, sc1 = g_ref[...], 1.0 + mod_ref[:, d:2 * d]
        gsc = gv * sc1
        half = NORM_ROWS // 2

        def group(c, acc):
            a_dh, a_dhxn = acc
            ks = range(NORM_GROUP)
            rows = [pl.ds(pl.multiple_of((c * NORM_GROUP + k) * NORM_ROWS, NORM_ROWS), NORM_ROWS) for k in ks]
            xv = [x_ref[rows[k], :] for k in ks]
            dhv = [dh_ref[rows[k], :].astype(F32) for k in ks]
            r = [lax.rsqrt(jnp.mean(xv[k] * xv[k], axis=-1, keepdims=True) + EPS) for k in ks]
            xn = [xv[k] * r[k] for k in ks]
            dxn = [dhv[k] * gsc for k in ks]
            t = [jnp.mean(dxn[k] * xn[k], axis=-1, keepdims=True) for k in ks]
            for k in ks:
                o_ref[rows[k], :] = gx_ref[rows[k], :] + r[k] * (dxn[k] - xn[k] * t[k])
                dhxn = dhv[k] * xn[k]
                a_dh = a_dh + dhv[k][:half] + dhv[k][half:]
                a_dhxn = a_dhxn + dhxn[:half] + dhxn[half:]
            return a_dh, a_dhxn

        zero = jnp.zeros((half, d), F32)
        a_dh, a_dhxn = lax.fori_loop(0, tm // (NORM_ROWS * NORM_GROUP), group, (zero, zero))
        dshift = jnp.sum(a_dh, axis=0, keepdims=True)
        s_dhxn = jnp.sum(a_dhxn, axis=0, keepdims=True)
        dscale, dg = s_dhxn * gv, s_dhxn * sc1

        @pl.when(i == 0)
        def _():
            dshift_ref[...] = dshift
            dscale_ref[...] = dscale
            dg_ref[...] = dg

        @pl.when(i > 0)
        def _():
            dshift_ref[...] += dshift
            dscale_ref[...] += dscale
            dg_ref[...] += dg

    tok = pl.BlockSpec((tm, d), lambda i: (i, 0))
    row = jax.ShapeDtypeStruct((1, d), F32)
    return pl.pallas_call(
        body, name="norm_bwd", grid=(s // tm,),
        in_specs=[tok, tok, tok, _full((1, d)), _full((1, 3 * d))],
        out_specs=[tok, _full((1, d)), _full((1, d)), _full((1, d))],
        out_shape=[jax.ShapeDtypeStruct((s, d), F32), row, row, row],
        compiler_params=_params(("arbitrary",)),
    )(x, dh, gx1, norm_g, mod)


def _adam_update(w, g, m, v):
    nm = ADAM_B1 * m + (1.0 - ADAM_B1) * g
    nv = ADAM_B2 * v + (1.0 - ADAM_B2) * (g * g)
    m_hat = nm / (1.0 - ADAM_B1 ** ADAM_STEP)
    v_hat = nv / (1.0 - ADAM_B2 ** ADAM_STEP)
    return -ADAM_LR * (m_hat / (jnp.sqrt(v_hat) + ADAM_EPS) + ADAM_WD * w), nm, nv


def _adamw_small(ws, gs, ms, vs):
    n = len(ws)

    def body(*refs):
        for a in range(n):
            w_ref, g_ref, m_ref, v_ref = (refs[k * n + a] for k in range(4))
            refs[4 * n + a][...], refs[5 * n + a][...], refs[6 * n + a][...] = _adam_update(
                w_ref[...], g_ref[...], m_ref[...], v_ref[...])

    shapes = [jax.ShapeDtypeStruct(w.shape, F32) for w in ws]
    out = pl.pallas_call(body, name="adamw_small", out_shape=shapes * 3, compiler_params=_params())(*ws, *gs, *ms, *vs)
    return out[:n], out[n:2 * n], out[2 * n:]


def _adamw(w, g, m, v, name):
    rows, cols = w.shape
    tr = 256 if rows % 256 == 0 else rows
    tc = 512 if (rows > 256 and tr == rows and cols % 512 == 0) else cols

    def body(w_ref, g_ref, m_ref, v_ref, d_ref, nm_ref, nv_ref):
        d_ref[...], nm_ref[...], nv_ref[...] = _adam_update(w_ref[...], g_ref[...], m_ref[...], v_ref[...])

    spec = pl.BlockSpec((tr, tc), lambda i, j: (i, j))
    shape = jax.ShapeDtypeStruct((rows, cols), F32)
    return pl.pallas_call(
        body, name=name, grid=(rows // tr, cols // tc), in_specs=[spec] * 4, out_specs=[spec] * 3, out_shape=[shape] * 3,
        compiler_params=_params(("parallel", "parallel")),
    )(w, g, m, v)


def _pad_cols(a, n):
    return jnp.pad(a, ((0, 0), (0, n - a.shape[1])))


def kernel(x, c, positions, ada_w, ada_b, norm_g, w_in, conv_w, q_a_g, w_q_b, kv_a_g, w_kv_b, q_g, k_g, w_out, loss_target, m_ada_w, m_ada_b, m_norm_g, m_w_in, m_conv_w, m_q_a_g, m_w_q_b, m_kv_a_g, m_w_kv_b, m_q_g, m_k_g, m_w_out, v_ada_w, v_ada_b, v_norm_g, v_w_in, v_conv_w, v_q_a_g, v_w_q_b, v_kv_a_g, v_w_kv_b, v_q_g, v_k_g, v_w_out):
    me = _my_index()
    s = x.shape[1]
    nq = s // min(TQ, s)
    x2, tgt = x[0], loss_target[0]
    w_in_l, w_q_l, w_kv_l, w_out_l, conv_l, ada_w_l = w_in[0], w_q_b[0], w_kv_b[0], w_out[0], conv_w[0], ada_w[0]
    ada_cols = ada_w_l.shape[1]

    small = jnp.concatenate([c.reshape(-1, LANES), conv_l.reshape(-1, LANES), jnp.zeros((5, LANES), F32)], axis=0)
    (small_g,) = _exchange(_GatherDirect([small]), [small], "gather_c")
    c_all = small_g[:, :D_MODEL // LANES].reshape(N_DEV, D_MODEL)
    conv_g = small_g[:, D_MODEL // LANES:D_MODEL // LANES + 3].transpose(1, 0, 2).reshape(3, D_CONV)

    ada_b_l = lax.dynamic_slice(ada_b, (0, me * ada_cols), (1, ada_cols))
    mod_cols = _ada_mod(jnp.pad(c_all, ((0, 8), (0, 0))), ada_w_l, ada_b_l)[:N_DEV]
    (mod_g,) = _exchange(_GatherDirect([mod_cols]), [mod_cols], "gather_mod")
    mod = lax.dynamic_index_in_dim(mod_g, me, axis=1, keepdims=False).reshape(1, 3 * D_MODEL)

    half = jnp.arange(0, QK_ROPE, 2, dtype=F32) / QK_ROPE
    inv_freq = ROPE_BASE ** (-half)
    zeros64 = jnp.zeros((LANES - QK_ROPE,), F32)
    invf = jnp.concatenate([inv_freq, inv_freq, zeros64]).reshape(1, LANES)
    sign = jnp.concatenate([-jnp.ones((32,), F32), jnp.ones((32,), F32), zeros64]).reshape(1, LANES)
    qg_p, kg_p = _pad_cols(q_g, QK_PAD), _pad_cols(k_g, QK_PAD)

    my_off = ((CW * me) % LANES).astype(jnp.int32)
    win = [_expand_w_in(w_in_l.T, my_off.reshape(1))]
    h, h_t, cos, sin, win_g = _norm_mod(x2, norm_g, mod, positions.reshape(s, 1), invf, sign, _Gather(win, relay=True, parts=4), win)
    w_in_p = _merge_w_in(win_g)
    rest = [_pad_wq(w_q_l.T), w_kv_l.astype(BF16), w_out_l.astype(BF16)]
    u, wq_g, wkv_g, w_out_g = _matmul(h, w_in_p, nt=False, out_dtype=BF16, tm=2 * TM_MM, tn=2048, name="in_proj",
                                      rider=_Gather(rest), rider_inputs=rest)
    w_out_g = w_out_g.reshape(D_MODEL, D_MODEL)
    wq_g = wq_g.transpose(1, 0, 2).reshape(Q_LORA, N_HEADS * QK_PAD)
    wkv_g = wkv_g.transpose(1, 0, 2).reshape(KV_LORA, 2 * D_ATTN)
    q, k, v = _qkv_fwd(u, cos, sin, wq_g, wkv_g, q_a_g, kv_a_g, qg_p, kg_p)
    o, lse = _flash_fwd(q, k, v)
    gx1, dy, ycat_t, dyc, do, dza, delta, dgate, loss_row = _tail(x2, tgt, o, u, mod, w_out_g, conv_g)

    dq, dk, dv = _flash_bwd(q, k, v, do, lse.reshape(N_HEADS, nq, s // nq), delta.reshape(N_HEADS, nq, s // nq))
    du, dconv = _conv_bwd(u, dyc, conv_g)
    du, dwq, dwkv, dqag, dkvag, dqg, dkg = _qkv_bwd(u, cos, sin, dq, dk, dv, dza, wq_g, wkv_g, q_a_g, kv_a_g, qg_p, kg_p, du)
    dwq = dwq.reshape(Q_LORA, N_HEADS, QK_PAD).transpose(1, 0, 2)
    dwkv = dwkv.reshape(KV_LORA, N_HEADS, 2 * V_HEAD).transpose(1, 0, 2)
    dw_in = _matmul(h_t, du, nt=False, out_dtype=BF16, tm=TM_MM, tn=768, name="dw_in")
    first = [dw_in, dwq, dwkv]
    dw_out, r_in, r_q, r_kv = _matmul(ycat_t, dy, nt=False, out_dtype=BF16, tm=TM_MM, tn=512, name="dw_out",
                                      rider=_SiblingExchange(first, [True, False, False]), rider_inputs=first)
    dw_out = dw_out.reshape(N_DEV, D_MODEL // N_DEV, D_MODEL)
    (r_out,) = _exchange(_SiblingExchange([dw_out], [False]), [dw_out], "rs_sibling_out")
    core = lax.axis_index("c").astype(jnp.int32)
    lo_tiles = ((CW * (2 * jnp.arange(4, dtype=jnp.int32) + core)) // LANES).astype(jnp.int32)
    pairs = [_add_window(dw_in, r_in, lo_tiles),
             *_add_pairs([dwq, dwkv, dw_out], [r_q, r_kv, r_out], core.reshape(1), "rs_add_rest")]
    dh, *quads = _matmul(du, w_in_p, nt=True, out_dtype=BF16, tm=2 * TM_MM, tn=512, name="dh",
                         rider=_ChipExchange(pairs), rider_inputs=pairs, a_resident=True)
    my_chip = 2 * lax.axis_index("x") + lax.axis_index("y")
    written = jnp.where(jnp.arange(4) == my_chip, (jnp.arange(4) + 1) % 4, jnp.arange(4))
    sel = jnp.concatenate([my_chip.reshape(1), written, ((EXP_W - my_off) % EXP_W).reshape(1)]).astype(jnp.int32)
    g_w_in_t = _final_sum(pairs[0], quads[0], sel, "rs_sum_in", unshift=True, keep_t=CW)
    g_w_q_t, g_w_kv, g_w_out = _final_sums(pairs[1:], quads[1:], sel, [QK_HEAD, None, None], "rs_sum_rest")
    grad_x, dshift, dscale, dng = _norm_bwd(x2, dh, gx1, norm_g, mod)

    row = jnp.concatenate([dshift, dscale, dgate, dng, dqag, dkvag, dqg, dkg, dconv[:3].reshape(1, 3 * D_CONV), loss_row], axis=1)
    (rows_g,) = _exchange(_GatherDirect([row]), [row], "gather_small")
    tot = _sum_leading(rows_g, F32, "sum_small")
    dmod_all = rows_g[:, 0, SM_MOD:SM_NG]
    g_ada_b = tot[:, SM_MOD:SM_NG]
    g_norm_g = tot[:, SM_NG:SM_QAG]
    g_q_a_g = tot[:, SM_QAG:SM_KVAG]
    g_kv_a_g = tot[:, SM_KVAG:SM_QG]
    g_q_g = tot[:, SM_QG:SM_QG + QK_HEAD]
    g_k_g = tot[:, SM_KG:SM_KG + QK_HEAD]
    conv_cols = conv_l.shape[1]
    g_conv = lax.dynamic_slice(tot[:, SM_CONV:SM_LOSS].reshape(3, D_CONV), (0, me * conv_cols), (3, conv_cols))
    loss = tot[0, SM_LOSS]
    dmod_my = lax.dynamic_slice(dmod_all, (0, me * ada_cols), (N_DEV, ada_cols))
    g_ada_w = _ada_w_grad(c_all.T, dmod_my)

    grads = dict(ada_w=g_ada_w, ada_b=g_ada_b, norm_g=g_norm_g, w_in=g_w_in_t, conv_w=g_conv, q_a_g=g_q_a_g, w_q_b=g_w_q_t,
                 kv_a_g=g_kv_a_g, w_kv_b=g_w_kv, q_g=g_q_g, k_g=g_k_g, w_out=g_w_out)
    weights = dict(ada_w=(ada_w, m_ada_w, v_ada_w), ada_b=(ada_b, m_ada_b, v_ada_b), norm_g=(norm_g, m_norm_g, v_norm_g),
                   w_in=(w_in, m_w_in, v_w_in), conv_w=(conv_w, m_conv_w, v_conv_w), q_a_g=(q_a_g, m_q_a_g, v_q_a_g),
                   w_q_b=(w_q_b, m_w_q_b, v_w_q_b), kv_a_g=(kv_a_g, m_kv_a_g, v_kv_a_g), w_kv_b=(w_kv_b, m_w_kv_b, v_w_kv_b),
                   q_g=(q_g, m_q_g, v_q_g), k_g=(k_g, m_k_g, v_k_g), w_out=(w_out, m_w_out, v_w_out))
    names = list(grads)
    big = ("ada_w", "w_in", "w_q_b", "w_kv_b", "w_out")
    small = [n for n in names if n not in big]

    def two_d(n, a):
        w = weights[n][0]
        return a.reshape(w.shape[-2:] if w.ndim == 3 else (1, w.shape[-1]))

    result = {}
    for n in big:
        transposed = n in ("w_in", "w_q_b")
        w2, m2, v2 = ((two_d(n, a).T if transposed else two_d(n, a)) for a in weights[n])
        g2 = grads[n] if transposed else two_d(n, grads[n])
        done = (g2, *_adamw(w2, g2, m2, v2, "adamw_" + n))
        result[n] = [a.T if transposed else a for a in done]
    g_small = [two_d(n, grads[n]) for n in small]
    w_s, m_s, v_s = ([two_d(n, weights[n][k]) for n in small] for k in range(3))
    d_s, nm_s, nv_s = _adamw_small(w_s, g_small, m_s, v_s)
    for a, n in enumerate(small):
        result[n] = [g_small[a], d_s[a], nm_s[a], nv_s[a]]
    outs = [[result[n][k].reshape(weights[n][0].shape) for n in names] for k in range(4)]
    return (loss, grad_x.reshape(x.shape), *outs[0], *outs[1], *outs[2], *outs[3])
```

```python
import functools
import math

import jax
import jax.numpy as jnp
from jax import lax
from jax.experimental import pallas as pl
from jax.experimental.pallas import tpu as pltpu

F32 = jnp.float32
BF16 = jnp.bfloat16
MESH = pl.DeviceIdType.MESH

D_MODEL = 2048
D_CONV = 1024
N_HEADS = 8
QK_NOPE = 128
QK_ROPE = 64
QK_HEAD = QK_NOPE + QK_ROPE
V_HEAD = 128
D_ATTN = N_HEADS * V_HEAD
Q_LORA = 512
KV_LORA = 256
ROPE_BASE = 10000.0
IN_COLS = 4 * D_CONV + Q_LORA + KV_LORA + QK_ROPE + D_ATTN
EPS = 1e-6
ADAM_LR, ADAM_B1, ADAM_B2, ADAM_EPS, ADAM_WD, ADAM_STEP = 0.001, 0.9, 0.999, 1e-08, 0.01, 10

N_DEV = 8
LANES = 128
QK_PAD = 256
U_COLS = 6144
U_CQ, U_CKV, U_KR, U_ZA = 4096, 4608, 4864, 4928
U_TAIL = 2048
ZA_LO = U_ZA - (U_COLS - U_TAIL) - QK_ROPE
ZA_WIN = D_ATTN + LANES
CW = IN_COLS // 8
EXP_W = 896
W_LO = [(CW * d // 128) * 128 for d in range(8)]
W_OFF = [CW * d - lo for d, lo in enumerate(W_LO)]
SCALE = 1.0 / math.sqrt(QK_HEAD)
LOG2E = 1.4426950408889634
LN2 = 0.6931471805599453
NEG = -1e30
VMEM_LIMIT = 56 * 1024 * 1024

TM_ELEM = 256
NORM_ROWS = 16
NORM_GROUP = 4
TM_MM = 512
TQ = 1024
Q_CHAINS = 4
KV_SPLIT = 2

SM_MOD, SM_NG, SM_QAG, SM_KVAG, SM_QG, SM_KG, SM_CONV, SM_LOSS = 0, 6144, 8192, 8704, 8960, 9216, 9472, 12544
SM_COLS = 12672


def _params(sem=None):
    kw = dict(vmem_limit_bytes=VMEM_LIMIT)
    if sem is not None:
        kw["dimension_semantics"] = sem
    return pltpu.CompilerParams(**kw)


def _sigmoid(z):
    return 1.0 / (1.0 + jnp.exp(-z))


def _rot64(x):
    lane = lax.broadcasted_iota(jnp.int32, x.shape, 1)
    return jnp.where(lane < 32, pltpu.roll(x, 96, 1), pltpu.roll(x, 32, 1))


def _rope(x, cos, sin):
    return x * cos + _rot64(x) * sin


def _rope_t(d, cos, sin):
    return d * cos - _rot64(d) * sin


def _dot(a, b):
    return jnp.dot(a, b, preferred_element_type=F32)


def _dot_nt(a, b):
    return lax.dot_general(a, b, (((1,), (1,)), ((), ())), preferred_element_type=F32)


def _dot_tn(a, b):
    return lax.dot_general(a, b, (((0,), (0,)), ((), ())), preferred_element_type=F32)


def _my_index():
    return 4 * lax.axis_index("x") + 2 * lax.axis_index("y") + lax.axis_index("c")


ANY = pl.BlockSpec(memory_space=pl.ANY)


class _Gather:
    def __init__(self, blocks, relay=False, parts=1):
        self.relay = relay
        self.parts = parts
        self.rows = [b.shape[0] // parts for b in blocks]
        self.n = n = len(blocks) * parts
        self.out_shape = [jax.ShapeDtypeStruct((N_DEV,) + b.shape, b.dtype) for b in blocks]
        self.scratch = [pltpu.SemaphoreType.DMA((7 * n,)), pltpu.SemaphoreType.DMA((7 * n,)),
                        pltpu.SemaphoreType.DMA((n,))]

    @staticmethod
    def _places():
        x, y, c = lax.axis_index("x"), lax.axis_index("y"), lax.axis_index("c")
        return (x, y, c), (x, y, 1 - c), [(1 - x, y), (x, 1 - y), (1 - x, 1 - y)]

    def _src(self, ins, a):
        block, part = divmod(a, self.parts)
        return ins[block] if self.parts == 1 else ins[block].at[pl.ds(part * self.rows[block], self.rows[block])]

    def _dst(self, outs, a, place):
        block, part = divmod(a, self.parts)
        ref = outs[block].at[4 * place[0] + 2 * place[1] + place[2]]
        return ref if self.parts == 1 else ref.at[pl.ds(part * self.rows[block], self.rows[block])]

    def _copy(self, outs, sems, a, k, block, to, src=None):
        dst = self._dst(outs, a, block)
        return pltpu.make_async_remote_copy(
            src_ref=dst if src is None else src, dst_ref=dst, send_sem=sems[0].at[7 * a + k],
            recv_sem=sems[1].at[7 * a + k], device_id=to, device_id_type=MESH)

    def _first(self, ins, outs, sems):
        me, sibling, chips = self._places()
        first = []
        for a in range(self.n):
            first.append(self._copy(outs, sems, a, 0, me, sibling, src=self._src(ins, a)))
            first += [self._copy(outs, sems, a, 1 + j, me, (*chip, me[2]), src=self._src(ins, a))
                      for j, chip in enumerate(chips[:2] if self.relay else chips)]
        return first

    def _relays(self, outs, sems):
        if not self.relay:
            return []
        (x, y, c), _, _ = self._places()
        via = (jnp.where(c == 0, 1 - x, x), jnp.where(c == 0, y, 1 - y))
        to = (jnp.where(c == 0, x, 1 - x), jnp.where(c == 0, 1 - y, y))
        return [self._copy(outs, sems, a, 3, (*via, c), (*to, c)) for a in range(self.n)]

    def _passed(self, outs, sems):
        me, sibling, chips = self._places()
        return [self._copy(outs, sems, a, 4 + j, (*chip, me[2]), sibling)
                for a in range(self.n) for j, chip in enumerate(chips)]

    def _mine(self, ins, outs, sems):
        me, _, _ = self._places()
        return [pltpu.make_async_copy(self._src(ins, a), self._dst(outs, a, me), sems[2].at[a]) for a in range(self.n)]

    def start(self, ins, outs, sems):
        for cp in self._mine(ins, outs, sems) + self._first(ins, outs, sems):
            cp.start()

    def forward(self, ins, outs, sems):
        del ins
        me, _, chips = self._places()
        passed, relays = self._passed(outs, sems), self._relays(outs, sems)
        for a in range(self.n):
            for j, chip in enumerate(chips[:2] if self.relay else chips):
                self._copy(outs, sems, a, 1 + j, (*chip, me[2]), me).wait_recv()
                passed[3 * a + j].start()
            if self.relay:
                relays[a].start()
        if self.relay:
            for a in range(self.n):
                self._copy(outs, sems, a, 3, (*chips[2], me[2]), me).wait_recv()
                passed[3 * a + 2].start()

    def finish(self, ins, outs, sems):
        me, sibling, chips = self._places()
        for a in range(self.n):
            self._copy(outs, sems, a, 0, sibling, me).wait_recv()
            for j, chip in enumerate(chips):
                self._copy(outs, sems, a, 4 + j, (*chip, 1 - me[2]), me).wait_recv()
        for cp in self._first(ins, outs, sems) + self._relays(outs, sems) + self._passed(outs, sems):
            cp.wait_send()
        for cp in self._mine(ins, outs, sems):
            cp.wait()


class _GatherDirect:
    FLIPS = [(0, 0, 1), (1, 0, 0), (0, 1, 0), (1, 1, 0), (1, 0, 1), (0, 1, 1), (1, 1, 1)]

    def __init__(self, blocks):
        self.n = n = len(blocks)
        self.out_shape = [jax.ShapeDtypeStruct((N_DEV,) + b.shape, b.dtype) for b in blocks]
        self.scratch = [pltpu.SemaphoreType.DMA((7 * n,)), pltpu.SemaphoreType.DMA((7 * n,)),
                        pltpu.SemaphoreType.DMA((n,))]

    def _copies(self, ins, outs, sems):
        x, y, c = lax.axis_index("x"), lax.axis_index("y"), lax.axis_index("c")
        mine = 4 * x + 2 * y + c
        remote = [pltpu.make_async_remote_copy(
            src_ref=ins[a], dst_ref=outs[a].at[mine], send_sem=sems[0].at[7 * a + k], recv_sem=sems[1].at[7 * a + k],
            device_id=(1 - x if fx else x, 1 - y if fy else y, 1 - c if fc else c), device_id_type=MESH)
            for a in range(self.n) for k, (fx, fy, fc) in enumerate(self.FLIPS)]
        local = [pltpu.make_async_copy(ins[a], outs[a].at[mine], sems[2].at[a]) for a in range(self.n)]
        return remote + local

    def start(self, ins, outs, sems):
        for cp in self._copies(ins, outs, sems):
            cp.start()

    def forward(self, ins, outs, sems):
        pass

    def finish(self, ins, outs, sems):
        for cp in self._copies(ins, outs, sems):
            cp.wait()


class _ChipExchange:
    def __init__(self, arrays):
        self.n = n = len(arrays)
        self.out_shape = [jax.ShapeDtypeStruct(a.shape, a.dtype) for a in arrays]
        self.scratch = [pltpu.SemaphoreType.DMA((3 * n,)), pltpu.SemaphoreType.DMA((3 * n,))]

    def _copies(self, ins, outs, sems):
        x, y, c = lax.axis_index("x"), lax.axis_index("y"), lax.axis_index("c")
        return [pltpu.make_async_remote_copy(
            src_ref=ins[a].at[2 * px + py], dst_ref=outs[a].at[2 * x + y], send_sem=sems[0].at[3 * a + j],
            recv_sem=sems[1].at[3 * a + j], device_id=(px, py, c), device_id_type=MESH)
            for a in range(self.n) for j, (px, py) in enumerate([(1 - x, y), (x, 1 - y), (1 - x, 1 - y)])]

    def start(self, ins, outs, sems):
        for cp in self._copies(ins, outs, sems):
            cp.start()

    def forward(self, ins, outs, sems):
        pass

    def finish(self, ins, outs, sems):
        for cp in self._copies(ins, outs, sems):
            cp.wait()


class _SiblingExchange:
    def __init__(self, arrays, windowed):
        self.n = n = len(arrays)
        self.windowed = windowed
        self.out_shape = [jax.ShapeDtypeStruct((4, a.shape[0], EXP_W) if w else (4,) + a.shape[1:], a.dtype)
                          for a, w in zip(arrays, windowed)]
        self.scratch = [pltpu.SemaphoreType.DMA((4 * n,)), pltpu.SemaphoreType.DMA((4 * n,))]

    def _each(self, ins, outs, sems, act):
        x, y, c = lax.axis_index("x"), lax.axis_index("y"), lax.axis_index("c")

        def branch(c_val):
            for k in range(4):
                e = 2 * k + (1 - c_val)
                for a in range(self.n):
                    src = ins[a].at[:, pl.ds(W_LO[e], EXP_W)] if self.windowed[a] else ins[a].at[e]
                    act(pltpu.make_async_remote_copy(
                        src_ref=src, dst_ref=outs[a].at[k], send_sem=sems[0].at[4 * a + k], recv_sem=sems[1].at[4 * a + k],
                        device_id=(x, y, 1 - c), device_id_type=MESH))

        for c_val in (0, 1):
            pl.when(c == c_val)(functools.partial(branch, c_val))

    def start(self, ins, outs, sems):
        self._each(ins, outs, sems, lambda cp: cp.start())

    def forward(self, ins, outs, sems):
        pass

    def finish(self, ins, outs, sems):
        self._each(ins, outs, sems, lambda cp: cp.wait())


def _exchange(rider, arrays, name):
    n = len(arrays)

    def body(*refs):
        ins, outs, sems = refs[:n], refs[n:n + len(rider.out_shape)], refs[n + len(rider.out_shape):]
        rider.start(ins, outs, sems)
        rider.forward(ins, outs, sems)
        rider.finish(ins, outs, sems)

    return pl.pallas_call(body, name=name, out_shape=rider.out_shape, in_specs=[ANY] * n,
                          out_specs=[ANY] * len(rider.out_shape), scratch_shapes=rider.scratch)(*arrays)


def _add_window(dw_in, recv, lo_tiles):
    k, rows, _ = recv.shape
    tiles = EXP_W // LANES

    def body(t_ref, *refs):
        del t_ref
        r_ref, o_ref = refs[tiles], refs[tiles + 1]
        own = jnp.concatenate([w_ref[...] for w_ref in refs[:tiles]], axis=1)
        o_ref[0] = (own.astype(F32) + r_ref[0].astype(F32)).astype(o_ref.dtype)

    def tile(j):
        return pl.BlockSpec((rows, LANES), lambda i, t: (0, t[i] + j))

    spec = pl.BlockSpec((1, rows, EXP_W), lambda i, t: (i, 0, 0))
    grid_spec = pltpu.PrefetchScalarGridSpec(
        num_scalar_prefetch=1, grid=(k,), in_specs=[tile(j) for j in range(tiles)] + [spec], out_specs=spec)
    return pl.pallas_call(
        body, name="rs_add_in", grid_spec=grid_spec, out_shape=jax.ShapeDtypeStruct(recv.shape, recv.dtype),
        compiler_params=_params(("parallel",)),
    )(lo_tiles, *([dw_in] * tiles), recv)


def _final_sum(p, r, sel, name, unshift=False, keep_t=None):
    _, rows, cols = p.shape
    tr = 512 if rows % 512 == 0 else rows

    def body(sel_ref, p_ref, r0, r1, r2, r3, o_ref):
        own = p_ref[0].astype(F32)
        acc = None
        for k, r_ref in enumerate((r0, r1, r2, r3)):
            term = jnp.where(sel_ref[0] == k, own, r_ref[0].astype(F32))
            acc = term if acc is None else acc + term
        if unshift:
            acc = pltpu.roll(acc, sel_ref[5], 1)
        o_ref[...] = acc if keep_t is None else acc.T[:keep_t]

    def slot(k):
        return pl.BlockSpec((1, tr, cols), lambda i, t: (t[k], i, 0))

    if keep_t is None:
        out_spec, out_shape = pl.BlockSpec((tr, cols), lambda i, t: (i, 0)), (rows, cols)
    else:
        out_spec, out_shape = pl.BlockSpec((keep_t, tr), lambda i, t: (0, i)), (keep_t, rows)
    grid_spec = pltpu.PrefetchScalarGridSpec(
        num_scalar_prefetch=1, grid=(rows // tr,), in_specs=[slot(0), slot(1), slot(2), slot(3), slot(4)],
        out_specs=out_spec)
    return pl.pallas_call(
        body, name=name, grid_spec=grid_spec, out_shape=jax.ShapeDtypeStruct(out_shape, F32),
        compiler_params=_params(("parallel",)),
    )(sel, p, r, r, r, r)


def _expand_w_in(w_t, shift):
    cw, rows = w_t.shape
    tr = TM_MM
    pad = -cw % LANES

    def body(s_ref, w_ref, o_ref):
        w = jnp.concatenate([w_ref[...], jnp.zeros((pad, tr), F32)], axis=0).T
        w = jnp.concatenate([w, jnp.zeros((tr, EXP_W - cw - pad), F32)], axis=1)
        o_ref[...] = pltpu.roll(w, s_ref[0], 1).astype(BF16)

    grid_spec = pltpu.PrefetchScalarGridSpec(
        num_scalar_prefetch=1, grid=(rows // tr,), in_specs=[pl.BlockSpec((cw, tr), lambda i, t: (0, i))],
        out_specs=pl.BlockSpec((tr, EXP_W), lambda i, t: (i, 0)))
    return pl.pallas_call(
        body, name="expand_w_in", grid_spec=grid_spec, out_shape=jax.ShapeDtypeStruct((rows, EXP_W), BF16),
        compiler_params=_params(("arbitrary",)),
    )(shift, w_t)


def _pad_wq(w_t):
    cw, rows = w_t.shape

    def body(w_ref, o_ref):
        o_ref[...] = jnp.concatenate([w_ref[...], jnp.zeros((QK_PAD - cw, rows), F32)], axis=0).T.astype(BF16)

    return pl.pallas_call(
        body, name="pad_wq", out_shape=jax.ShapeDtypeStruct((rows, QK_PAD), BF16), compiler_params=_params(),
    )(w_t)


def _merge_w_in(e):
    _, rows, _ = e.shape
    tr = TM_MM

    def body(e_ref, o_ref):
        for t in range(U_COLS // LANES):
            lo, hi = t * LANES, (t + 1) * LANES
            parts = [e_ref[d, :, lo - W_LO[d]:hi - W_LO[d]] for d in range(N_DEV)
                     if CW * d < hi and CW * (d + 1) > lo]
            if not parts:
                tile = jnp.zeros((tr, LANES), BF16)
            elif len(parts) == 1:
                tile = parts[0]
            else:
                tile = (parts[0].astype(F32) + parts[1].astype(F32)).astype(BF16)
            o_ref[:, lo:hi] = tile

    return pl.pallas_call(
        body, name="merge_w_in", grid=(rows // tr,),
        in_specs=[pl.BlockSpec((N_DEV, tr, EXP_W), lambda i: (0, i, 0))],
        out_specs=pl.BlockSpec((tr, U_COLS), lambda i: (i, 0)), out_shape=jax.ShapeDtypeStruct((rows, U_COLS), BF16),
        compiler_params=_params(("parallel",)),
    )(e)


def _sum_leading(a, out_dtype, name):
    k, rows, cols = a.shape
    tr = min(rows, 1728 if rows % 1728 == 0 else rows)

    def body(a_ref, o_ref):
        acc = a_ref[0].astype(F32)
        for i in range(1, k):
            acc = acc + a_ref[i].astype(F32)
        o_ref[...] = acc.astype(out_dtype)

    return pl.pallas_call(
        body, name=name, grid=(rows // tr,),
        in_specs=[pl.BlockSpec((k, tr, cols), lambda i: (0, i, 0))],
        out_specs=pl.BlockSpec((tr, cols), lambda i: (i, 0)),
        out_shape=jax.ShapeDtypeStruct((rows, cols), out_dtype), compiler_params=_params(("parallel",)),
    )(a)


def _add_pairs(gs, recvs, core, name):
    n = len(gs)

    def body(c_ref, *refs):
        del c_ref
        for a in range(n):
            refs[2 * n + a][...] = (refs[a][...].astype(F32) + refs[n + a][...].astype(F32)).astype(refs[2 * n + a].dtype)

    def mine(r):
        return pl.BlockSpec((1,) + r.shape[1:], lambda i, c: (2 * i + c[0], 0, 0))

    def kth(r):
        return pl.BlockSpec((1,) + r.shape[1:], lambda i, c: (i, 0, 0))

    grid_spec = pltpu.PrefetchScalarGridSpec(
        num_scalar_prefetch=1, grid=(4,), in_specs=[mine(r) for r in recvs] + [kth(r) for r in recvs],
        out_specs=[kth(r) for r in recvs])
    return pl.pallas_call(
        body, name=name, grid_spec=grid_spec, out_shape=[jax.ShapeDtypeStruct(r.shape, r.dtype) for r in recvs],
        compiler_params=_params(("parallel",)),
    )(core, *gs, *recvs)


def _final_sums(ps, rs, sel, keep_t, name):
    n = len(ps)

    def body(sel_ref, *refs):
        for a in range(n):
            own = refs[5 * a][0].astype(F32)
            acc = None
            for k in range(4):
                term = jnp.where(sel_ref[0] == k, own, refs[5 * a + 1 + k][0].astype(F32))
                acc = term if acc is None else acc + term
            refs[5 * n + a][...] = acc if keep_t[a] is None else acc.T[:keep_t[a]]

    def slot(p, k):
        return pl.BlockSpec((1,) + p.shape[1:], lambda i, t: (t[k], 0, 0))

    out_shapes = [p.shape[1:] if kt is None else (kt, p.shape[1]) for p, kt in zip(ps, keep_t)]
    grid_spec = pltpu.PrefetchScalarGridSpec(
        num_scalar_prefetch=1, grid=(1,), in_specs=[slot(p, k) for p in ps for k in range(5)],
        out_specs=[pl.BlockSpec(sh, lambda i, t: (0, 0)) for sh in out_shapes])
    operands = [x for p, r in zip(ps, rs) for x in (p, r, r, r, r)]
    return pl.pallas_call(
        body, name=name, grid_spec=grid_spec, out_shape=[jax.ShapeDtypeStruct(sh, F32) for sh in out_shapes],
        compiler_params=_params(("arbitrary",)),
    )(sel, *operands)


def _ada_mod(c16, ada_w_l, ada_b_l):
    def body(c_ref, w_ref, b_ref, o_ref):
        cv = c_ref[...]
        sc = (cv * _sigmoid(cv)).astype(BF16)
        o_ref[...] = _dot(sc, w_ref[...].astype(BF16)) + b_ref[...]

    return pl.pallas_call(
        body, name="ada_mod", out_shape=jax.ShapeDtypeStruct((c16.shape[0], ada_w_l.shape[1]), F32),
        compiler_params=_params(),
    )(c16, ada_w_l, ada_b_l)


def _adamw_ada_w(c_t, dmod_my, w, m, v):
    rows, cols = w.shape
    tr = min(TM_ELEM, rows)

    def body(c_ref, d_ref, w_ref, m_ref, v_ref, g_ref, dl_ref, nm_ref, nv_ref):
        cv = c_ref[...]
        sc = cv * _sigmoid(cv)
        acc = sc[:, 0:1] * d_ref[0:1, :]
        for b in range(1, N_DEV):
            acc = acc + sc[:, b:b + 1] * d_ref[b:b + 1, :]
        g_ref[...] = acc
        dl_ref[...], nm_ref[...], nv_ref[...] = _adam_update(w_ref[...], acc, m_ref[...], v_ref[...])

    spec = pl.BlockSpec((tr, cols), lambda i: (i, 0))
    return pl.pallas_call(
        body, name="adamw_ada_w", grid=(rows // tr,),
        in_specs=[pl.BlockSpec((tr, N_DEV), lambda i: (i, 0)), pl.BlockSpec(dmod_my.shape, lambda i: (0, 0)), spec, spec, spec],
        out_specs=[spec] * 4, out_shape=[jax.ShapeDtypeStruct((rows, cols), F32)] * 4,
        compiler_params=_params(("parallel",)),
    )(c_t, dmod_my, w, m, v)


def _norm_mod(x, norm_g, mod, pos_col, invf, sign, rider, rider_inputs):
    s, d = x.shape
    tm = min(TM_MM, s)
    n_in, n_out = len(rider_inputs), len(rider.out_shape)
    steps = s // tm

    def body(x_ref, g_ref, mod_ref, p_ref, f_ref, s_ref, *rest):
        r_ins, (h_ref, ht_ref, cos_ref, sin_ref) = rest[:n_in], rest[n_in:n_in + 4]
        r_outs, sems = rest[n_in + 4:n_in + 4 + n_out], rest[n_in + 4 + n_out:]
        pl.when(pl.program_id(0) == 0)(functools.partial(rider.start, r_ins, r_outs, sems))
        xv = x_ref[...]
        r = lax.rsqrt(jnp.mean(xv * xv, axis=-1, keepdims=True) + EPS)
        hn = xv * r * g_ref[...]
        hv = hn * (1.0 + mod_ref[:, d:2 * d]) + mod_ref[:, 0:d]
        h_ref[...] = hv.astype(BF16)
        ht_ref[...] = hv.T.astype(BF16)
        ang = p_ref[...].astype(F32) * f_ref[...]
        sg = s_ref[...]
        cos_ref[...] = jnp.cos(ang) * jnp.abs(sg)
        sin_ref[...] = jnp.sin(ang) * sg

        @pl.when(pl.program_id(0) == steps - 1)
        def _():
            rider.forward(r_ins, r_outs, sems)
            rider.finish(r_ins, r_outs, sems)

    row = pl.BlockSpec((1, LANES), lambda i: (0, 0))
    tab = pl.BlockSpec((tm, LANES), lambda i: (i, 0))
    return pl.pallas_call(
        body, name="norm_mod", grid=(steps,),
        in_specs=[pl.BlockSpec((tm, d), lambda i: (i, 0)), pl.BlockSpec((1, d), lambda i: (0, 0)),
                  pl.BlockSpec((1, 3 * d), lambda i: (0, 0)), pl.BlockSpec((tm, 1), lambda i: (i, 0)), row, row]
        + [ANY] * n_in,
        out_specs=[pl.BlockSpec((tm, d), lambda i: (i, 0)), pl.BlockSpec((d, tm), lambda i: (0, i)), tab, tab] + [ANY] * n_out,
        out_shape=[jax.ShapeDtypeStruct((s, d), BF16), jax.ShapeDtypeStruct((d, s), BF16),
                   jax.ShapeDtypeStruct((s, LANES), F32), jax.ShapeDtypeStruct((s, LANES), F32)] + rider.out_shape,
        scratch_shapes=rider.scratch, compiler_params=_params(("arbitrary",)),
    )(x, norm_g, mod, pos_col, invf, sign, *rider_inputs)


def _matmul(a, b, *, nt, out_dtype, tm, tn, name, rider=None, rider_inputs=(), a_resident=False):
    m, kdim = a.shape
    n = b.shape[0] if nt else b.shape[1]
    tm, tn = min(tm, m), min(tn, n)
    n_in = len(rider_inputs)
    n_out = len(rider.out_shape) if rider else 0
    m_steps, n_steps = m // tm, n // tn
    steps = n_steps * m_steps
    inner = n_steps if a_resident else m_steps
    tile = (lambda o, i: (o, i)) if a_resident else (lambda o, i: (i, o))

    def body(a_ref, b_ref, *rest):
        r_ins, o_ref, r_outs, sems = rest[:n_in], rest[n_in], rest[n_in + 1:n_in + 1 + n_out], rest[n_in + 1 + n_out:]
        step = pl.program_id(0) * inner + pl.program_id(1)
        if rider:
            pl.when(step == 0)(functools.partial(rider.start, r_ins, r_outs, sems))
            pl.when(step == steps // 2)(functools.partial(rider.forward, r_ins, r_outs, sems))
        o = _dot_nt(a_ref[...], b_ref[...]) if nt else _dot(a_ref[...], b_ref[...])
        o_ref[...] = o.astype(out_dtype)
        if rider:
            pl.when(step == steps - 1)(functools.partial(rider.finish, r_ins, r_outs, sems))

    if nt:
        b_spec = pl.BlockSpec((tn, kdim), lambda o, i: (tile(o, i)[1], 0))
    else:
        b_spec = pl.BlockSpec((kdim, tn), lambda o, i: (0, tile(o, i)[1]))
    out = pl.pallas_call(
        body, name=name, grid=(m_steps, n_steps) if a_resident else (n_steps, m_steps),
        in_specs=[pl.BlockSpec((tm, kdim), lambda o, i: (tile(o, i)[0], 0)), b_spec] + [ANY] * n_in,
        out_specs=[pl.BlockSpec((tm, tn), tile)] + [ANY] * n_out,
        out_shape=[jax.ShapeDtypeStruct((m, n), out_dtype)] + (rider.out_shape if rider else []),
        scratch_shapes=rider.scratch if rider else [],
        compiler_params=_params(("arbitrary", "arbitrary") if rider else ("parallel", "parallel")),
    )(a, b, *rider_inputs)
    return out if rider else out[0]


HALO = 16


def _conv_specs(tm):
    def col(j):
        return pl.BlockSpec((tm, D_CONV), lambda i: (i, j))

    def prev(j):
        return pl.BlockSpec((HALO, D_CONV), lambda i: (jnp.maximum(i * (tm // HALO) - 1, 0), j))

    return [col(0), col(1), col(2), col(3), prev(0), prev(2)]


def _conv_y(xc_ref, bc_ref, cc_ref, zc_ref, xp_ref, cp_ref, w_ref, first):
    uc = cc_ref[...].astype(F32) * xc_ref[...].astype(F32)
    up = jnp.where(first, 0.0, cp_ref[...].astype(F32) * xp_ref[...].astype(F32))
    full = jnp.concatenate([up, uc], axis=0)
    u1 = pltpu.roll(full, 1, 0)[HALO:]
    u2 = pltpu.roll(full, 2, 0)[HALO:]
    w = w_ref[...]
    conv = w[0:1] * u2 + w[1:2] * u1 + w[2:3] * uc
    z = zc_ref[...].astype(F32)
    return bc_ref[...].astype(F32) * conv * (z * _sigmoid(z))


def _conv_bwd(u, dyc, conv_w):
    s = u.shape[0]
    tm = min(TM_MM, s)
    cb = D_CONV
    nt = s // tm

    def body(xc_ref, bc_ref, cc_ref, zc_ref, xp_ref, cp_ref, bn_ref, zn_ref, dy_ref, dyn_ref, w_ref, du_ref, dw_ref):
        i = pl.program_id(0)
        xc, cc = xc_ref[...].astype(F32), cc_ref[...].astype(F32)
        bc, z = bc_ref[...].astype(F32), zc_ref[...].astype(F32)
        uc = cc * xc
        up = jnp.where(i == 0, 0.0, cp_ref[...].astype(F32) * xp_ref[...].astype(F32))
        full = jnp.concatenate([up, uc], axis=0)
        u1 = pltpu.roll(full, 1, 0)[HALO:]
        u2 = pltpu.roll(full, 2, 0)[HALO:]
        w = w_ref[...]
        conv = w[0:1] * u2 + w[1:2] * u1 + w[2:3] * uc
        sg = _sigmoid(z)
        sz = z * sg
        dy = dy_ref[...].astype(F32)
        dconv = dy * bc * sz
        zn = zn_ref[...].astype(F32)
        dnext = dyn_ref[...].astype(F32) * bn_ref[...].astype(F32) * (zn * _sigmoid(zn))
        dnext = jnp.where(i == nt - 1, 0.0, dnext)
        fullb = jnp.concatenate([dconv, dnext], axis=0)
        nb = tm + HALO
        d1 = pltpu.roll(fullb, nb - 1, 0)[:tm]
        d2 = pltpu.roll(fullb, nb - 2, 0)[:tm]
        duc = w[2:3] * dconv + w[1:2] * d1 + w[0:1] * d2
        dzc = dy * bc * conv * (sg * (1.0 + z * (1.0 - sg)))
        du_ref[...] = jnp.concatenate([duc * cc, dy * conv * sz, duc * xc, dzc], axis=1).astype(BF16)
        dw = jnp.concatenate([jnp.sum(dconv * u2, axis=0, keepdims=True), jnp.sum(dconv * u1, axis=0, keepdims=True),
                              jnp.sum(dconv * uc, axis=0, keepdims=True), jnp.zeros((5, cb), F32)], axis=0)

        @pl.when(i == 0)
        def _():
            dw_ref[...] = dw

        @pl.when(i > 0)
        def _():
            dw_ref[...] += dw

    def col(j):
        return pl.BlockSpec((tm, cb), lambda i: (i, j))

    def prev(j):
        return pl.BlockSpec((HALO, cb), lambda i: (jnp.maximum(i * (tm // HALO) - 1, 0), j))

    def nxt(j):
        return pl.BlockSpec((HALO, cb), lambda i: (jnp.minimum((i + 1) * (tm // HALO), s // HALO - 1), j))

    return pl.pallas_call(
        body, name="conv_bwd", grid=(nt,),
        in_specs=[col(0), col(1), col(2), col(3), prev(0), prev(2), nxt(1), nxt(3), col(0), nxt(0),
                  pl.BlockSpec((3, cb), lambda i: (0, 0))],
        out_specs=[pl.BlockSpec((tm, 4 * cb), lambda i: (i, 0)), pl.BlockSpec((8, cb), lambda i: (0, 0))],
        out_shape=[jax.ShapeDtypeStruct((s, U_COLS), BF16), jax.ShapeDtypeStruct((8, cb), F32)],
        compiler_params=_params(("arbitrary",)),
    )(u, u, u, u, u, u, u, u, dyc, dyc, conv_w)


def _qkv_specs(tm):
    return [pl.BlockSpec((tm, Q_LORA), lambda i: (i, U_CQ // Q_LORA)),
            pl.BlockSpec((tm, KV_LORA), lambda i: (i, U_CKV // KV_LORA)),
            pl.BlockSpec((tm, LANES), lambda i: (i, U_KR // LANES)),
            pl.BlockSpec((tm, LANES), lambda i: (i, 0)), pl.BlockSpec((tm, LANES), lambda i: (i, 0))]


def _full(shape):
    return pl.BlockSpec(shape, lambda i: (0,) * len(shape))


def _k_rope_lanes(blk):
    lane = lax.broadcasted_iota(jnp.int32, blk.shape, 1)
    return jnp.where(lane < QK_ROPE, blk, 0.0)


def _qkv_fwd(u, cos, sin, wq, wkv, qag, kvag, qg, kg):
    s = u.shape[0]
    tm = min(TM_MM, s)

    def body(cq_ref, ckv_ref, kr_ref, cos_ref, sin_ref, wq_ref, wkv_ref, qag_ref, kvag_ref, qg_ref, kg_ref,
             q_ref, k_ref, v_ref):
        cq = cq_ref[...].astype(F32)
        cqn = (cq * lax.rsqrt(jnp.mean(cq * cq, axis=-1, keepdims=True) + EPS) * qag_ref[...]).astype(BF16)
        ckv = ckv_ref[...].astype(F32)
        ckvn = (ckv * lax.rsqrt(jnp.mean(ckv * ckv, axis=-1, keepdims=True) + EPS) * kvag_ref[...]).astype(BF16)
        kr = _k_rope_lanes(kr_ref[...].astype(F32))
        cosv, sinv, qgv, kgv = cos_ref[...], sin_ref[...], qg_ref[...], kg_ref[...]
        ss_r = jnp.sum(kr * kr, axis=-1, keepdims=True)
        krr = _rope(kr * kgv[:, QK_NOPE:], cosv, sinv)
        qf = _dot(cqn, wq_ref[...])
        kvf = _dot(ckvn, wkv_ref[...])
        heads = range(N_HEADS)
        qh = [qf[:, QK_PAD * h:QK_PAD * (h + 1)] for h in heads]
        kn = [kvf[:, 2 * V_HEAD * h:2 * V_HEAD * h + QK_NOPE] for h in heads]
        rq = [lax.rsqrt(jnp.sum(qh[h] * qh[h], axis=-1, keepdims=True) * (1.0 / QK_HEAD) + EPS) for h in heads]
        rk = [lax.rsqrt((jnp.sum(kn[h] * kn[h], axis=-1, keepdims=True) + ss_r) * (1.0 / QK_HEAD) + EPS) for h in heads]
        for h in heads:
            qn = qh[h] * rq[h] * qgv
            qo = jnp.concatenate([qn[:, :QK_NOPE], _rope(qn[:, QK_NOPE:], cosv, sinv)], axis=1) * (SCALE * LOG2E)
            q_ref[h] = qo.astype(BF16)
            vh = kvf[:, 2 * V_HEAD * h + QK_NOPE:2 * V_HEAD * (h + 1)]
            k_ref[h] = jnp.concatenate([kn[h] * kgv[:, :QK_NOPE] * rk[h], krr * rk[h]], axis=1).astype(BF16)
            v_ref[h] = jnp.concatenate([vh, jnp.ones_like(vh)], axis=1).astype(BF16)

    return pl.pallas_call(
        body, name="qkv_fwd", grid=(s // tm,),
        in_specs=_qkv_specs(tm) + [_full((Q_LORA, N_HEADS * QK_PAD)), _full((KV_LORA, 2 * D_ATTN)),
                                   _full((1, Q_LORA)), _full((1, KV_LORA)), _full((1, QK_PAD)), _full((1, QK_PAD))],
        out_specs=[pl.BlockSpec((N_HEADS, tm, QK_PAD), lambda i: (0, i, 0)),
                   pl.BlockSpec((N_HEADS, tm, QK_PAD), lambda i: (0, i, 0)),
                   pl.BlockSpec((N_HEADS, tm, 2 * V_HEAD), lambda i: (0, i, 0))],
        out_shape=[jax.ShapeDtypeStruct((N_HEADS, s, QK_PAD), BF16), jax.ShapeDtypeStruct((N_HEADS, s, QK_PAD), BF16),
                   jax.ShapeDtypeStruct((N_HEADS, s, 2 * V_HEAD), BF16)],
        compiler_params=_params(("parallel",)),
    )(u, u, u, cos, sin, wq, wkv, qag, kvag, qg, kg)


def _qkv_bwd(u, cos, sin, dq, dk, dv, dza, wq, wkv, qag, kvag, qg, kg, du):
    s = u.shape[0]
    tm = min(TM_ELEM, s)
    nt = s // tm

    def body(cq_ref, ckv_ref, kr_ref, cos_ref, sin_ref, dq_ref, dk_ref, dv_ref, dza_ref, wq_ref, wkv_ref, qag_ref,
             kvag_ref, qg_ref, kg_ref, du_in, du_ref, dwq_ref, dwkv_ref, dqag_ref, dkvag_ref, dqg_ref, dkg_ref,
             dwq_acc, dwkv_acc):
        del du_in
        i = pl.program_id(0)

        @pl.when(i == 0)
        def _():
            dwq_acc[...] = jnp.zeros_like(dwq_acc)
            dwkv_acc[...] = jnp.zeros_like(dwkv_acc)

        cq = cq_ref[...].astype(F32)
        rqa = lax.rsqrt(jnp.mean(cq * cq, axis=-1, keepdims=True) + EPS)
        xq = cq * rqa
        qagv = qag_ref[...]
        cqn = (xq * qagv).astype(BF16)
        ckv = ckv_ref[...].astype(F32)
        rkva = lax.rsqrt(jnp.mean(ckv * ckv, axis=-1, keepdims=True) + EPS)
        xkv = ckv * rkva
        kvagv = kvag_ref[...]
        ckvn = (xkv * kvagv).astype(BF16)
        kr = _k_rope_lanes(kr_ref[...].astype(F32))
        cosv, sinv, qgv, kgv = cos_ref[...], sin_ref[...], qg_ref[...], kg_ref[...]
        ss_r = jnp.sum(kr * kr, axis=-1, keepdims=True)
        dqg = jnp.zeros((1, QK_PAD), F32)
        dkg = jnp.zeros((1, QK_PAD), F32)
        dkr = jnp.zeros((tm, LANES), F32)
        qf = _dot(cqn, wq_ref[...])
        kvf = _dot(ckvn, wkv_ref[...])
        heads = range(N_HEADS)
        qh = [qf[:, QK_PAD * h:QK_PAD * (h + 1)] for h in heads]
        kn = [kvf[:, 2 * V_HEAD * h:2 * V_HEAD * h + QK_NOPE] for h in heads]
        rq = [lax.rsqrt(jnp.sum(qh[h] * qh[h], axis=-1, keepdims=True) * (1.0 / QK_HEAD) + EPS) for h in heads]
        rk = [lax.rsqrt((jnp.sum(kn[h] * kn[h], axis=-1, keepdims=True) + ss_r) * (1.0 / QK_HEAD) + EPS) for h in heads]
        xh = [qh[h] * rq[h] for h in heads]
        xk = [jnp.concatenate([kn[h], kr], axis=1) * rk[h] for h in heads]
        dyq, dyk = [], []
        for h in heads:
            g = dq_ref[h].astype(F32)
            dyq.append(jnp.concatenate([g[:, :QK_NOPE], _rope_t(g[:, QK_NOPE:], cosv, sinv)], axis=1))
            gk = dk_ref[h].astype(F32)
            dyk.append(jnp.concatenate([gk[:, :QK_NOPE], _rope_t(gk[:, QK_NOPE:], cosv, sinv)], axis=1))
        for h in heads:
            dqg = dqg + jnp.sum(dyq[h] * xh[h], axis=0, keepdims=True)
            dkg = dkg + jnp.sum(dyk[h] * xk[h], axis=0, keepdims=True)
        dqg = dqg * SCALE
        qgv = qgv * SCALE
        gdy = [dyq[h] * qgv for h in heads]
        gdyk = [dyk[h] * kgv for h in heads]
        tq_ = [jnp.sum(gdy[h] * xh[h], axis=-1, keepdims=True) * (1.0 / QK_HEAD) for h in heads]
        tk_ = [jnp.sum(gdyk[h] * xk[h], axis=-1, keepdims=True) * (1.0 / QK_HEAD) for h in heads]
        dqf = [(rq[h] * (gdy[h] - xh[h] * tq_[h])).astype(BF16) for h in heads]
        dkvf = []
        for h in heads:
            dxk = rk[h] * (gdyk[h] - xk[h] * tk_[h])
            dkr = dkr + dxk[:, QK_NOPE:]
            dkvf += [dxk[:, :QK_NOPE].astype(BF16), dv_ref[h]]
        dqf_b, dkvf_b = jnp.concatenate(dqf, axis=1), jnp.concatenate(dkvf, axis=1)
        dwq_acc[...] += _dot_tn(cqn, dqf_b)
        dwkv_acc[...] += _dot_tn(ckvn, dkvf_b)
        dcqn = _dot_nt(dqf_b, wq_ref[...])
        dckvn = _dot_nt(dkvf_b, wkv_ref[...])
        dqag = jnp.sum(dcqn * xq, axis=0, keepdims=True)
        dkvag = jnp.sum(dckvn * xkv, axis=0, keepdims=True)
        gq = dcqn * qagv
        dcq = rqa * (gq - xq * jnp.mean(gq * xq, axis=-1, keepdims=True))
        gkv = dckvn * kvagv
        dckv = rkva * (gkv - xkv * jnp.mean(gkv * xkv, axis=-1, keepdims=True))
        win = pltpu.roll(jnp.concatenate([dza_ref[...].astype(F32), jnp.zeros((tm, LANES), F32)], axis=1), QK_ROPE, 1)
        win = win + jnp.concatenate([dkr, jnp.zeros((tm, D_ATTN), F32)], axis=1)
        du_ref[...] = jnp.concatenate([dcq, dckv, win, jnp.zeros((tm, U_TAIL - ZA_LO - ZA_WIN), F32)], axis=1).astype(BF16)

        @pl.when(i == 0)
        def _():
            dqag_ref[...] = dqag
            dkvag_ref[...] = dkvag
            dqg_ref[...] = dqg
            dkg_ref[...] = dkg

        @pl.when(i > 0)
        def _():
            dqag_ref[...] += dqag
            dkvag_ref[...] += dkvag
            dqg_ref[...] += dqg
            dkg_ref[...] += dkg

        @pl.when(i == nt - 1)
        def _():
            dwq_ref[...] = dwq_acc[...].astype(BF16)
            dwkv_ref[...] = dwkv_acc[...].astype(BF16)

    head = lambda w: pl.BlockSpec((N_HEADS, tm, w), lambda i: (0, i, 0))
    wq_shape, wkv_shape = (Q_LORA, N_HEADS * QK_PAD), (KV_LORA, 2 * D_ATTN)
    return pl.pallas_call(
        body, name="qkv_bwd", grid=(nt,),
        in_specs=_qkv_specs(tm) + [head(QK_PAD), head(QK_PAD), head(V_HEAD), pl.BlockSpec((tm, D_ATTN), lambda i: (i, 0)),
                                   _full(wq_shape), _full(wkv_shape), _full((1, Q_LORA)), _full((1, KV_LORA)),
                                   _full((1, QK_PAD)), _full((1, QK_PAD)), ANY],
        out_specs=[pl.BlockSpec((tm, U_TAIL), lambda i: (i, U_COLS // U_TAIL - 1)), _full(wq_shape), _full(wkv_shape),
                   _full((1, Q_LORA)), _full((1, KV_LORA)), _full((1, QK_PAD)), _full((1, QK_PAD))],
        out_shape=[jax.ShapeDtypeStruct(du.shape, du.dtype), jax.ShapeDtypeStruct(wq_shape, BF16),
                   jax.ShapeDtypeStruct(wkv_shape, BF16), jax.ShapeDtypeStruct((1, Q_LORA), F32),
                   jax.ShapeDtypeStruct((1, KV_LORA), F32), jax.ShapeDtypeStruct((1, QK_PAD), F32),
                   jax.ShapeDtypeStruct((1, QK_PAD), F32)],
        scratch_shapes=[pltpu.VMEM(wq_shape, F32), pltpu.VMEM(wkv_shape, F32)],
        input_output_aliases={15: 0}, compiler_params=_params(("arbitrary",)),
    )(u, u, u, cos, sin, dq, dk, dv, dza, wq, wkv, qag, kvag, qg, kg, du)


def _flash_fwd(q, k, v):
    nh, s, _ = q.shape
    tq = min(TQ, s)
    nkv = KV_SPLIT
    tk = tq // nkv
    nq = s // tq
    nch = Q_CHAINS
    tc = tq // nch

    def body(q_ref, k_ref, v_ref, o_ref, lse_ref):
        i = pl.program_id(1)
        chains = [q_ref[0, r * tc:(r + 1) * tc, :] for r in range(nch)]

        def unit(r, j, carry, shift=None):
            m, acc = carry
            rows = pl.ds(pl.multiple_of(j * tk, tk), tk)
            sc = _dot_nt(chains[r], k_ref[0, rows, :])
            if shift is not None:
                qi = lax.broadcasted_iota(jnp.int32, sc.shape, 0)
                ki = lax.broadcasted_iota(jnp.int32, sc.shape, 1) + shift
                sc = jnp.where(ki <= qi, sc, NEG)
            m_new = jnp.maximum(m, jnp.max(sc, axis=-1, keepdims=True))
            p = jnp.exp2(sc - m_new).astype(BF16)
            return m_new, jnp.exp2(m - m_new) * acc + _dot(p, v_ref[0, rows, :])

        def trip(p, carry):
            for b in range(nkv):
                carry = tuple(unit(r, nkv * p + b, cr) for r, cr in enumerate(carry))
            return carry

        init = (jnp.full((tc, 1), NEG, F32), jnp.zeros((tc, 2 * V_HEAD), F32))
        carry = list(lax.fori_loop(0, i, trip, (init,) * nch))
        for b in range(nkv):
            for r in range(nch):
                shift = b * tk - r * tc
                if shift < tc:
                    carry[r] = unit(r, nkv * i + b, carry[r], None if shift + tk - 1 <= 0 else shift)
        for r, (m, acc) in enumerate(carry):
            l = acc[:, V_HEAD:]
            o_ref[r * tc:(r + 1) * tc, :] = (acc[:, :V_HEAD] / l).astype(BF16)
            lse = m + jnp.log(l[:, 0:1]) * LOG2E
            lse_ref[0, :, r * tc:(r + 1) * tc] = jnp.broadcast_to(lse, (tc, LANES)).T[0:1, :]

    return pl.pallas_call(
        body, name="flash_fwd", grid=(nh, nq),
        in_specs=[pl.BlockSpec((1, tq, QK_PAD), lambda h, i: (h, i, 0)),
                  pl.BlockSpec((1, s, QK_PAD), lambda h, i: (h, 0, 0)),
                  pl.BlockSpec((1, s, 2 * V_HEAD), lambda h, i: (h, 0, 0))],
        out_specs=[pl.BlockSpec((tq, V_HEAD), lambda h, i: (i, h)), pl.BlockSpec((1, 1, tq), lambda h, i: (h, 0, i))],
        out_shape=[jax.ShapeDtypeStruct((s, nh * V_HEAD), BF16), jax.ShapeDtypeStruct((nh, 1, s), F32)],
        compiler_params=_params(("parallel", "arbitrary")),
    )(q, k, v)


def _flash_bwd(q, k, v, do, lse, delta):
    nh, s, _ = q.shape
    tq = min(TQ, s)
    nq = s // tq
    kps = 2 if nq % 2 == 0 else 1
    ng = nq // kps

    def body(q_ref, k_ref, v_ref, do_ref, lse_ref, dl_ref, dq_ref, dk_ref, dv_ref, dq_acc):
        g = ng - 1 - pl.program_id(1)

        @pl.when(g == ng - 1)
        def _():
            dq_acc[...] = jnp.zeros_like(dq_acc)

        for sub in reversed(range(kps)):
            kv_block(q_ref, k_ref, v_ref, do_ref, lse_ref, dl_ref, dk_ref, dv_ref, dq_acc, g * kps + sub, sub)

        @pl.when(g == 0)
        def _():
            dq_ref[0] = dq_acc[...].astype(BF16)

    def kv_block(q_ref, k_ref, v_ref, do_ref, lse_ref, dl_ref, dk_ref, dv_ref, dq_acc, j, sub):
        own = slice(sub * tq, (sub + 1) * tq)
        kj, vj = k_ref[0, own, :], v_ref[0, own, :]

        def block(kk, vv, qq, dd, lse, dl, masked):
            st = _dot_nt(kk, qq)
            pt = jnp.exp2(st - lse)
            if masked:
                ki = lax.broadcasted_iota(jnp.int32, st.shape, 0)
                qx = lax.broadcasted_iota(jnp.int32, st.shape, 1)
                pt = jnp.where(ki <= qx, pt, 0.0)
            ddv = _dot(pt.astype(BF16), dd)
            dst = (pt * (_dot_nt(vv, dd) - dl)).astype(BF16)
            ddq = _dot_tn(dst, kk)
            return _dot(dst, qq), ddv, ddq

        def step(i, carry):
            dk, dv = carry
            rows = pl.ds(pl.multiple_of(i * tq, tq), tq)
            ddk, ddv, ddq = block(kj, vj, q_ref[0, rows, :], do_ref[rows, :], lse_ref[0, pl.ds(i, 1), :],
                                  dl_ref[0, pl.ds(i, 1), :], False)
            dq_acc[rows, :] += ddq
            return dk + ddk, dv + ddv

        th = tq // 2
        lse_j, dl_j = lse_ref[0, pl.ds(j, 1), :], dl_ref[0, pl.ds(j, 1), :]
        parts = []
        for kh, qh, masked in ((0, 0, True), (0, 1, False), (1, 1, True)):
            rows = pl.ds(pl.multiple_of(j * tq + qh * th, th), th)
            ks, qs = slice(kh * th, (kh + 1) * th), slice(qh * th, (qh + 1) * th)
            ddk, ddv, ddq = block(kj[ks], vj[ks], q_ref[0, rows, :], do_ref[rows, :], lse_j[:, qs], dl_j[:, qs], masked)
            dq_acc[rows, :] += ddq
            parts.append((ddk, ddv))
        carry = (jnp.concatenate([parts[0][0] + parts[1][0], parts[2][0]], axis=0),
                 jnp.concatenate([parts[0][1] + parts[1][1], parts[2][1]], axis=0))
        dk, dv = lax.fori_loop(j + 1, nq, step, carry)
        dk_ref[0, own, :] = (dk * LN2).astype(BF16)
        dv_ref[0, own, :] = dv.astype(BF16)

    return pl.pallas_call(
        body, name="flash_bwd", grid=(nh, ng),
        in_specs=[pl.BlockSpec((1, s, QK_PAD), lambda h, j: (h, 0, 0)),
                  pl.BlockSpec((1, kps * tq, QK_PAD), lambda h, g: (h, ng - 1 - g, 0)),
                  pl.BlockSpec((1, kps * tq, V_HEAD), lambda h, g: (h, ng - 1 - g, 0)),
                  pl.BlockSpec((s, V_HEAD), lambda h, j: (0, h)),
                  pl.BlockSpec((1, nq, tq), lambda h, j: (h, 0, 0)),
                  pl.BlockSpec((1, nq, tq), lambda h, j: (h, 0, 0))],
        out_specs=[pl.BlockSpec((1, s, QK_PAD), lambda h, j: (h, 0, 0)),
                   pl.BlockSpec((1, kps * tq, QK_PAD), lambda h, g: (h, ng - 1 - g, 0)),
                   pl.BlockSpec((1, kps * tq, V_HEAD), lambda h, g: (h, ng - 1 - g, 0))],
        out_shape=[jax.ShapeDtypeStruct((nh, s, QK_PAD), BF16), jax.ShapeDtypeStruct((nh, s, QK_PAD), BF16),
                   jax.ShapeDtypeStruct((nh, s, V_HEAD), BF16)],
        scratch_shapes=[pltpu.VMEM((s, QK_PAD), F32)],
        compiler_params=_params(("parallel", "arbitrary")),
    )(q, k, v, do, lse, delta)


def _tail(x, target, o, u, mod, w_out, conv_w):
    s, d = x.shape
    tm = min(TM_ELEM, s)

    def body(x_ref, t_ref, o_ref, za_ref, mod_ref, w_ref, xc_ref, bc_ref, cc_ref, zc_ref, xp_ref, cp_ref, cw_ref,
             gx_ref, dy_ref, ycat_ref, dyc_ref, do_ref, du_ref, delta_ref, dgate_ref, loss_ref):
        i = pl.program_id(0)
        za = pltpu.roll(za_ref[:, ZA_LO:ZA_LO + ZA_WIN].astype(F32), ZA_WIN - QK_ROPE, 1)[:, :D_ATTN]
        ov = o_ref[...].astype(F32)
        sg = _sigmoid(za)
        sl = za * sg
        ya = ov * sl
        y = _dot(ya.astype(BF16), w_ref[D_CONV:, :])
        yc = _conv_y(xc_ref, bc_ref, cc_ref, zc_ref, xp_ref, cp_ref, cw_ref, i == 0)
        y = y + _dot(yc.astype(BF16), w_ref[:D_CONV, :])
        ycat_ref[...] = jnp.concatenate([yc.T, ya.T], axis=0).astype(BF16)
        gate = mod_ref[:, 2 * d:3 * d]
        e = x_ref[...] + gate * y - t_ref[...]
        dout = e * (1.0 / d)
        gx_ref[...] = dout
        dy = (dout * gate).astype(BF16)
        dy_ref[...] = dy
        dycat = _dot_nt(dy, w_ref[...])
        dyc_ref[...] = dycat[:, :D_CONV].astype(BF16)
        dya = dycat[:, D_CONV:]
        dov = dya * sl
        do_ref[...] = dov.astype(BF16)
        du_ref[...] = (dya * ov * (sg * (1.0 + za * (1.0 - sg)))).astype(BF16)
        prod_t = (dov * ov).T
        for h in range(N_HEADS):
            delta_ref[h] = jnp.sum(prod_t[V_HEAD * h:V_HEAD * (h + 1), :], axis=0, keepdims=True)
        dgate = jnp.sum(dout * y, axis=0, keepdims=True)
        part = jnp.sum(jnp.sum(e * e, axis=0, keepdims=True), axis=1, keepdims=True) * (0.5 / d)
        part = jnp.broadcast_to(part, (1, LANES))

        @pl.when(i == 0)
        def _():
            dgate_ref[...] = dgate
            loss_ref[...] = part

        @pl.when(i > 0)
        def _():
            dgate_ref[...] += dgate
            loss_ref[...] += part

    tok = lambda w: pl.BlockSpec((tm, w), lambda i: (i, 0))
    return pl.pallas_call(
        body, name="tail", grid=(s // tm,),
        in_specs=[tok(d), tok(d), tok(D_ATTN), pl.BlockSpec((tm, U_TAIL), lambda i: (i, U_COLS // U_TAIL - 1)),
                  _full((1, 3 * d)), _full((d, d))] + _conv_specs(tm) + [_full((3, D_CONV))],
        out_specs=[tok(d), tok(d), pl.BlockSpec((d, tm), lambda i: (0, i)), tok(D_CONV), tok(D_ATTN), tok(D_ATTN),
                   pl.BlockSpec((N_HEADS, 1, tm), lambda i: (0, 0, i)), _full((1, d)), _full((1, LANES))],
        out_shape=[jax.ShapeDtypeStruct((s, d), F32), jax.ShapeDtypeStruct((s, d), BF16),
                   jax.ShapeDtypeStruct((d, s), BF16), jax.ShapeDtypeStruct((s, D_CONV), BF16),
                   jax.ShapeDtypeStruct((s, D_ATTN), BF16), jax.ShapeDtypeStruct((s, D_ATTN), BF16),
                   jax.ShapeDtypeStruct((N_HEADS, 1, s), F32), jax.ShapeDtypeStruct((1, d), F32),
                   jax.ShapeDtypeStruct((1, LANES), F32)],
        compiler_params=_params(("arbitrary",)),
    )(x, target, o, u, mod, w_out, u, u, u, u, u, u, conv_w)


def _norm_bwd(x, dh, gx1, norm_g, mod):
    s, d = x.shape
    tm = min(TM_MM, s)

    def body(x_ref, dh_ref, gx_ref, g_ref, mod_ref, o_ref, dshift_ref, dscale_ref, dg_ref):
        i = pl.program_id(0)
        gv, sc1 = g_ref[...], 1.0 + mod_ref[:, d:2 * d]
        gsc = gv * sc1
        half = NORM_ROWS // 2

        def group(c, acc):
            a_dh, a_dhxn = acc
            ks = range(NORM_GROUP)
            rows = [pl.ds(pl.multiple_of((c * NORM_GROUP + k) * NORM_ROWS, NORM_ROWS), NORM_ROWS) for k in ks]
            xv = [x_ref[rows[k], :] for k in ks]
            dhv = [dh_ref[rows[k], :].astype(F32) for k in ks]
            r = [lax.rsqrt(jnp.mean(xv[k] * xv[k], axis=-1, keepdims=True) + EPS) for k in ks]
            xn = [xv[k] * r[k] for k in ks]
            dxn = [dhv[k] * gsc for k in ks]
            t = [jnp.mean(dxn[k] * xn[k], axis=-1, keepdims=True) for k in ks]
            for k in ks:
                o_ref[rows[k], :] = gx_ref[rows[k], :] + r[k] * (dxn[k] - xn[k] * t[k])
                dhxn = dhv[k] * xn[k]
                a_dh = a_dh + dhv[k][:half] + dhv[k][half:]
                a_dhxn = a_dhxn + dhxn[:half] + dhxn[half:]
            return a_dh, a_dhxn

        zero = jnp.zeros((half, d), F32)
        a_dh, a_dhxn = lax.fori_loop(0, tm // (NORM_ROWS * NORM_GROUP), group, (zero, zero))
        dshift = jnp.sum(a_dh, axis=0, keepdims=True)
        s_dhxn = jnp.sum(a_dhxn, axis=0, keepdims=True)
        dscale, dg = s_dhxn * gv, s_dhxn * sc1

        @pl.when(i == 0)
        def _():
            dshift_ref[...] = dshift
            dscale_ref[...] = dscale
            dg_ref[...] = dg

        @pl.when(i > 0)
        def _():
            dshift_ref[...] += dshift
            dscale_ref[...] += dscale
            dg_ref[...] += dg

    tok = pl.BlockSpec((tm, d), lambda i: (i, 0))
    row = jax.ShapeDtypeStruct((1, d), F32)
    return pl.pallas_call(
        body, name="norm_bwd", grid=(s // tm,),
        in_specs=[tok, tok, tok, _full((1, d)), _full((1, 3 * d))],
        out_specs=[tok, _full((1, d)), _full((1, d)), _full((1, d))],
        out_shape=[jax.ShapeDtypeStruct((s, d), F32), row, row, row],
        compiler_params=_params(("arbitrary",)),
    )(x, dh, gx1, norm_g, mod)


def _adam_update(w, g, m, v):
    nm = ADAM_B1 * m + (1.0 - ADAM_B1) * g
    nv = ADAM_B2 * v + (1.0 - ADAM_B2) * (g * g)
    m_hat = nm / (1.0 - ADAM_B1 ** ADAM_STEP)
    v_hat = nv / (1.0 - ADAM_B2 ** ADAM_STEP)
    return -ADAM_LR * (m_hat / (jnp.sqrt(v_hat) + ADAM_EPS) + ADAM_WD * w), nm, nv


def _adamw_small(ws, gs, ms, vs):
    n = len(ws)

    def body(*refs):
        for a in range(n):
            w_ref, g_ref, m_ref, v_ref = (refs[k * n + a] for k in range(4))
            refs[4 * n + a][...], refs[5 * n + a][...], refs[6 * n + a][...] = _adam_update(
                w_ref[...], g_ref[...], m_ref[...], v_ref[...])

    shapes = [jax.ShapeDtypeStruct(w.shape, F32) for w in ws]
    out = pl.pallas_call(body, name="adamw_small", out_shape=shapes * 3, compiler_params=_params())(*ws, *gs, *ms, *vs)
    return out[:n], out[n:2 * n], out[2 * n:]


def _adamw(w, g, m, v, name):
    rows, cols = w.shape
    tr = 256 if rows % 256 == 0 else rows
    tc = 512 if (rows > 256 and tr == rows and cols % 512 == 0) else cols

    def body(w_ref, g_ref, m_ref, v_ref, d_ref, nm_ref, nv_ref):
        d_ref[...], nm_ref[...], nv_ref[...] = _adam_update(w_ref[...], g_ref[...], m_ref[...], v_ref[...])

    spec = pl.BlockSpec((tr, tc), lambda i, j: (i, j))
    shape = jax.ShapeDtypeStruct((rows, cols), F32)
    return pl.pallas_call(
        body, name=name, grid=(rows // tr, cols // tc), in_specs=[spec] * 4, out_specs=[spec] * 3, out_shape=[shape] * 3,
        compiler_params=_params(("parallel", "parallel")),
    )(w, g, m, v)


def _pad_cols(a, n):
    return jnp.pad(a, ((0, 0), (0, n - a.shape[1])))


def kernel(x, c, positions, ada_w, ada_b, norm_g, w_in, conv_w, q_a_g, w_q_b, kv_a_g, w_kv_b, q_g, k_g, w_out, loss_target, m_ada_w, m_ada_b, m_norm_g, m_w_in, m_conv_w, m_q_a_g, m_w_q_b, m_kv_a_g, m_w_kv_b, m_q_g, m_k_g, m_w_out, v_ada_w, v_ada_b, v_norm_g, v_w_in, v_conv_w, v_q_a_g, v_w_q_b, v_kv_a_g, v_w_kv_b, v_q_g, v_k_g, v_w_out):
    me = _my_index()
    s = x.shape[1]
    nq = s // min(TQ, s)
    x2, tgt = x[0], loss_target[0]
    w_in_l, w_q_l, w_kv_l, w_out_l, conv_l, ada_w_l = w_in[0], w_q_b[0], w_kv_b[0], w_out[0], conv_w[0], ada_w[0]
    ada_cols = ada_w_l.shape[1]

    small = jnp.concatenate([c.reshape(-1, LANES), conv_l.reshape(-1, LANES), jnp.zeros((5, LANES), F32)], axis=0)
    (small_g,) = _exchange(_GatherDirect([small]), [small], "gather_c")
    c_all = small_g[:, :D_MODEL // LANES].reshape(N_DEV, D_MODEL)
    conv_g = small_g[:, D_MODEL // LANES:D_MODEL // LANES + 3].transpose(1, 0, 2).reshape(3, D_CONV)

    ada_b_l = lax.dynamic_slice(ada_b, (0, me * ada_cols), (1, ada_cols))
    mod_cols = _ada_mod(jnp.pad(c_all, ((0, 8), (0, 0))), ada_w_l, ada_b_l)[:N_DEV]
    (mod_g,) = _exchange(_GatherDirect([mod_cols]), [mod_cols], "gather_mod")
    mod = lax.dynamic_index_in_dim(mod_g, me, axis=1, keepdims=False).reshape(1, 3 * D_MODEL)

    half = jnp.arange(0, QK_ROPE, 2, dtype=F32) / QK_ROPE
    inv_freq = ROPE_BASE ** (-half)
    zeros64 = jnp.zeros((LANES - QK_ROPE,), F32)
    invf = jnp.concatenate([inv_freq, inv_freq, zeros64]).reshape(1, LANES)
    sign = jnp.concatenate([-jnp.ones((32,), F32), jnp.ones((32,), F32), zeros64]).reshape(1, LANES)
    qg_p, kg_p = _pad_cols(q_g, QK_PAD), _pad_cols(k_g, QK_PAD)

    my_off = ((CW * me) % LANES).astype(jnp.int32)
    win = [_expand_w_in(w_in_l.T, my_off.reshape(1))]
    h, h_t, cos, sin, win_g = _norm_mod(x2, norm_g, mod, positions.reshape(s, 1), invf, sign, _Gather(win, relay=True, parts=4), win)
    w_in_p = _merge_w_in(win_g)
    rest = [_pad_wq(w_q_l.T), w_kv_l.astype(BF16), w_out_l.astype(BF16)]
    u, wq_g, wkv_g, w_out_g = _matmul(h, w_in_p, nt=False, out_dtype=BF16, tm=2 * TM_MM, tn=2048, name="in_proj",
                                      rider=_Gather(rest), rider_inputs=rest)
    w_out_g = w_out_g.reshape(D_MODEL, D_MODEL)
    wq_g = wq_g.transpose(1, 0, 2).reshape(Q_LORA, N_HEADS * QK_PAD)
    wkv_g = wkv_g.transpose(1, 0, 2).reshape(KV_LORA, 2 * D_ATTN)
    q, k, v = _qkv_fwd(u, cos, sin, wq_g, wkv_g, q_a_g, kv_a_g, qg_p, kg_p)
    o, lse = _flash_fwd(q, k, v)
    gx1, dy, ycat_t, dyc, do, dza, delta, dgate, loss_row = _tail(x2, tgt, o, u, mod, w_out_g, conv_g)

    dq, dk, dv = _flash_bwd(q, k, v, do, lse.reshape(N_HEADS, nq, s // nq), delta.reshape(N_HEADS, nq, s // nq))
    du, dconv = _conv_bwd(u, dyc, conv_g)
    du, dwq, dwkv, dqag, dkvag, dqg, dkg = _qkv_bwd(u, cos, sin, dq, dk, dv, dza, wq_g, wkv_g, q_a_g, kv_a_g, qg_p, kg_p, du)
    dwq = dwq.reshape(Q_LORA, N_HEADS, QK_PAD).transpose(1, 0, 2)
    dwkv = dwkv.reshape(KV_LORA, N_HEADS, 2 * V_HEAD).transpose(1, 0, 2)
    dw_in = _matmul(h_t, du, nt=False, out_dtype=BF16, tm=TM_MM, tn=768, name="dw_in")
    first = [dw_in, dwq, dwkv]
    dw_out, r_in, r_q, r_kv = _matmul(ycat_t, dy, nt=False, out_dtype=BF16, tm=TM_MM, tn=512, name="dw_out",
                                      rider=_SiblingExchange(first, [True, False, False]), rider_inputs=first)
    dw_out = dw_out.reshape(N_DEV, D_MODEL // N_DEV, D_MODEL)
    (r_out,) = _exchange(_SiblingExchange([dw_out], [False]), [dw_out], "rs_sibling_out")
    core = lax.axis_index("c").astype(jnp.int32)
    lo_tiles = ((CW * (2 * jnp.arange(4, dtype=jnp.int32) + core)) // LANES).astype(jnp.int32)
    pairs = [_add_window(dw_in, r_in, lo_tiles),
             *_add_pairs([dwq, dwkv, dw_out], [r_q, r_kv, r_out], core.reshape(1), "rs_add_rest")]
    dh, *quads = _matmul(du, w_in_p, nt=True, out_dtype=BF16, tm=2 * TM_MM, tn=512, name="dh",
                         rider=_ChipExchange(pairs), rider_inputs=pairs, a_resident=True)
    my_chip = 2 * lax.axis_index("x") + lax.axis_index("y")
    written = jnp.where(jnp.arange(4) == my_chip, (jnp.arange(4) + 1) % 4, jnp.arange(4))
    sel = jnp.concatenate([my_chip.reshape(1), written, ((EXP_W - my_off) % EXP_W).reshape(1)]).astype(jnp.int32)
    g_w_in_t = _final_sum(pairs[0], quads[0], sel, "rs_sum_in", unshift=True, keep_t=CW)
    g_w_q_t, g_w_kv, g_w_out = _final_sums(pairs[1:], quads[1:], sel, [QK_HEAD, None, None], "rs_sum_rest")
    grad_x, dshift, dscale, dng = _norm_bwd(x2, dh, gx1, norm_g, mod)

    row = jnp.concatenate([dshift, dscale, dgate, dng, dqag, dkvag, dqg, dkg, dconv[:3].reshape(1, 3 * D_CONV), loss_row], axis=1)
    (rows_g,) = _exchange(_GatherDirect([row]), [row], "gather_small")
    tot = _sum_leading(rows_g, F32, "sum_small")
    dmod_all = rows_g[:, 0, SM_MOD:SM_NG]
    g_ada_b = tot[:, SM_MOD:SM_NG]
    g_norm_g = tot[:, SM_NG:SM_QAG]
    g_q_a_g = tot[:, SM_QAG:SM_KVAG]
    g_kv_a_g = tot[:, SM_KVAG:SM_QG]
    g_q_g = tot[:, SM_QG:SM_QG + QK_HEAD]
    g_k_g = tot[:, SM_KG:SM_KG + QK_HEAD]
    conv_cols = conv_l.shape[1]
    g_conv = lax.dynamic_slice(tot[:, SM_CONV:SM_LOSS].reshape(3, D_CONV), (0, me * conv_cols), (3, conv_cols))
    loss = tot[0, SM_LOSS]
    dmod_my = lax.dynamic_slice(dmod_all, (0, me * ada_cols), (N_DEV, ada_cols))

    grads = dict(ada_w=None, ada_b=g_ada_b, norm_g=g_norm_g, w_in=g_w_in_t, conv_w=g_conv, q_a_g=g_q_a_g, w_q_b=g_w_q_t,
                 kv_a_g=g_kv_a_g, w_kv_b=g_w_kv, q_g=g_q_g, k_g=g_k_g, w_out=g_w_out)
    weights = dict(ada_w=(ada_w, m_ada_w, v_ada_w), ada_b=(ada_b, m_ada_b, v_ada_b), norm_g=(norm_g, m_norm_g, v_norm_g),
                   w_in=(w_in, m_w_in, v_w_in), conv_w=(conv_w, m_conv_w, v_conv_w), q_a_g=(q_a_g, m_q_a_g, v_q_a_g),
                   w_q_b=(w_q_b, m_w_q_b, v_w_q_b), kv_a_g=(kv_a_g, m_kv_a_g, v_kv_a_g), w_kv_b=(w_kv_b, m_w_kv_b, v_w_kv_b),
                   q_g=(q_g, m_q_g, v_q_g), k_g=(k_g, m_k_g, v_k_g), w_out=(w_out, m_w_out, v_w_out))
    names = list(grads)
    big = ("ada_w", "w_in", "w_q_b", "w_kv_b", "w_out")
    small = [n for n in names if n not in big]

    def two_d(n, a):
        w = weights[n][0]
        return a.reshape(w.shape[-2:] if w.ndim == 3 else (1, w.shape[-1]))

    result = {}
    result["ada_w"] = _adamw_ada_w(c_all.T, dmod_my, *(two_d("ada_w", a) for a in weights["ada_w"]))
    for n in big[1:]:
        transposed = n in ("w_in", "w_q_b")
        w2, m2, v2 = ((two_d(n, a).T if transposed else two_d(n, a)) for a in weights[n])
        g2 = grads[n] if transposed else two_d(n, grads[n])
        done = (g2, *_adamw(w2, g2, m2, v2, "adamw_" + n))
        result[n] = [a.T if transposed else a for a in done]
    g_small = [two_d(n, grads[n]) for n in small]
    w_s, m_s, v_s = ([two_d(n, weights[n][k]) for n in small] for k in range(3))
    d_s, nm_s, nv_s = _adamw_small(w_s, g_small, m_s, v_s)
    for a, n in enumerate(small):
        result[n] = [g_small[a], d_s[a], nm_s[a], nv_s[a]]
    outs = [[result[n][k].reshape(weights[n][0].shape) for n in names] for k in range(4)]
    return (loss, grad_x.reshape(x.shape), *outs[0], *outs[1], *outs[2], *outs[3])
```

```python
import functools
import math

import jax
import jax.numpy as jnp
from jax import lax
from jax.experimental import pallas as pl
from jax.experimental.pallas import tpu as pltpu

F32 = jnp.float32
BF16 = jnp.bfloat16
MESH = pl.DeviceIdType.MESH

D_MODEL = 2048
D_CONV = 1024
N_HEADS = 8
QK_NOPE = 128
QK_ROPE = 64
QK_HEAD = QK_NOPE + QK_ROPE
V_HEAD = 128
D_ATTN = N_HEADS * V_HEAD
Q_LORA = 512
KV_LORA = 256
ROPE_BASE = 10000.0
IN_COLS = 4 * D_CONV + Q_LORA + KV_LORA + QK_ROPE + D_ATTN
EPS = 1e-6
ADAM_LR, ADAM_B1, ADAM_B2, ADAM_EPS, ADAM_WD, ADAM_STEP = 0.001, 0.9, 0.999, 1e-08, 0.01, 10

N_DEV = 8
LANES = 128
QK_PAD = 256
U_COLS = 6144
U_CQ, U_CKV, U_KR, U_ZA = 4096, 4608, 4864, 4928
U_TAIL = 2048
ZA_LO = U_ZA - (U_COLS - U_TAIL) - QK_ROPE
ZA_WIN = D_ATTN + LANES
CW = IN_COLS // 8
EXP_W = 896
W_LO = [(CW * d // 128) * 128 for d in range(8)]
W_OFF = [CW * d - lo for d, lo in enumerate(W_LO)]
SCALE = 1.0 / math.sqrt(QK_HEAD)
LOG2E = 1.4426950408889634
LN2 = 0.6931471805599453
NEG = -1e30
VMEM_LIMIT = 56 * 1024 * 1024

TM_ELEM = 256
NORM_ROWS = 16
NORM_GROUP = 4
TM_MM = 512
TQ = 1024
Q_CHAINS = 4
KV_SPLIT = 2

SM_MOD, SM_NG, SM_QAG, SM_KVAG, SM_QG, SM_KG, SM_CONV, SM_LOSS = 0, 6144, 8192, 8704, 8960, 9216, 9472, 12544
SM_COLS = 12672


def _params(sem=None):
    kw = dict(vmem_limit_bytes=VMEM_LIMIT)
    if sem is not None:
        kw["dimension_semantics"] = sem
    return pltpu.CompilerParams(**kw)


def _sigmoid(z):
    return 1.0 / (1.0 + jnp.exp(-z))


def _rot64(x):
    lane = lax.broadcasted_iota(jnp.int32, x.shape, 1)
    return jnp.where(lane < 32, pltpu.roll(x, 96, 1), pltpu.roll(x, 32, 1))


def _rope(x, cos, sin):
    return x * cos + _rot64(x) * sin


def _rope_t(d, cos, sin):
    return d * cos - _rot64(d) * sin


def _dot(a, b):
    return jnp.dot(a, b, preferred_element_type=F32)


def _dot_nt(a, b):
    return lax.dot_general(a, b, (((1,), (1,)), ((), ())), preferred_element_type=F32)


def _dot_tn(a, b):
    return lax.dot_general(a, b, (((0,), (0,)), ((), ())), preferred_element_type=F32)


def _my_index():
    return 4 * lax.axis_index("x") + 2 * lax.axis_index("y") + lax.axis_index("c")


ANY = pl.BlockSpec(memory_space=pl.ANY)


class _Gather:
    def __init__(self, blocks, relay=False, parts=1):
        self.relay = relay
        self.parts = parts
        self.rows = [b.shape[0] // parts for b in blocks]
        self.n = n = len(blocks) * parts
        self.out_shape = [jax.ShapeDtypeStruct((N_DEV,) + b.shape, b.dtype) for b in blocks]
        self.scratch = [pltpu.SemaphoreType.DMA((7 * n,)), pltpu.SemaphoreType.DMA((7 * n,)),
                        pltpu.SemaphoreType.DMA((n,))]

    @staticmethod
    def _places():
        x, y, c = lax.axis_index("x"), lax.axis_index("y"), lax.axis_index("c")
        return (x, y, c), (x, y, 1 - c), [(1 - x, y), (x, 1 - y), (1 - x, 1 - y)]

    def _src(self, ins, a):
        block, part = divmod(a, self.parts)
        return ins[block] if self.parts == 1 else ins[block].at[pl.ds(part * self.rows[block], self.rows[block])]

    def _dst(self, outs, a, place):
        block, part = divmod(a, self.parts)
        ref = outs[block].at[4 * place[0] + 2 * place[1] + place[2]]
        return ref if self.parts == 1 else ref.at[pl.ds(part * self.rows[block], self.rows[block])]

    def _copy(self, outs, sems, a, k, block, to, src=None):
        dst = self._dst(outs, a, block)
        return pltpu.make_async_remote_copy(
            src_ref=dst if src is None else src, dst_ref=dst, send_sem=sems[0].at[7 * a + k],
            recv_sem=sems[1].at[7 * a + k], device_id=to, device_id_type=MESH)

    def _first(self, ins, outs, sems):
        me, sibling, chips = self._places()
        first = []
        for a in range(self.n):
            first.append(self._copy(outs, sems, a, 0, me, sibling, src=self._src(ins, a)))
            first += [self._copy(outs, sems, a, 1 + j, me, (*chip, me[2]), src=self._src(ins, a))
                      for j, chip in enumerate(chips[:2] if self.relay else chips)]
        return first

    def _relays(self, outs, sems):
        if not self.relay:
            return []
        (x, y, c), _, _ = self._places()
        via = (jnp.where(c == 0, 1 - x, x), jnp.where(c == 0, y, 1 - y))
        to = (jnp.where(c == 0, x, 1 - x), jnp.where(c == 0, 1 - y, y))
        return [self._copy(outs, sems, a, 3, (*via, c), (*to, c)) for a in range(self.n)]

    def _passed(self, outs, sems):
        me, sibling, chips = self._places()
        return [self._copy(outs, sems, a, 4 + j, (*chip, me[2]), sibling)
                for a in range(self.n) for j, chip in enumerate(chips)]

    def _mine(self, ins, outs, sems):
        me, _, _ = self._places()
        return [pltpu.make_async_copy(self._src(ins, a), self._dst(outs, a, me), sems[2].at[a]) for a in range(self.n)]

    def start(self, ins, outs, sems):
        for cp in self._mine(ins, outs, sems) + self._first(ins, outs, sems):
            cp.start()

    def forward(self, ins, outs, sems):
        del ins
        me, _, chips = self._places()
        passed, relays = self._passed(outs, sems), self._relays(outs, sems)
        for a in range(self.n):
            for j, chip in enumerate(chips[:2] if self.relay else chips):
                self._copy(outs, sems, a, 1 + j, (*chip, me[2]), me).wait_recv()
                passed[3 * a + j].start()
            if self.relay:
                relays[a].start()
        if self.relay:
            for a in range(self.n):
                self._copy(outs, sems, a, 3, (*chips[2], me[2]), me).wait_recv()
                passed[3 * a + 2].start()

    def finish(self, ins, outs, sems):
        me, sibling, chips = self._places()
        for a in range(self.n):
            self._copy(outs, sems, a, 0, sibling, me).wait_recv()
            for j, chip in enumerate(chips):
                self._copy(outs, sems, a, 4 + j, (*chip, 1 - me[2]), me).wait_recv()
        for cp in self._first(ins, outs, sems) + self._relays(outs, sems) + self._passed(outs, sems):
            cp.wait_send()
        for cp in self._mine(ins, outs, sems):
            cp.wait()


class _GatherDirect:
    FLIPS = [(0, 0, 1), (1, 0, 0), (0, 1, 0), (1, 1, 0), (1, 0, 1), (0, 1, 1), (1, 1, 1)]

    def __init__(self, blocks):
        self.n = n = len(blocks)
        self.out_shape = [jax.ShapeDtypeStruct((N_DEV,) + b.shape, b.dtype) for b in blocks]
        self.scratch = [pltpu.SemaphoreType.DMA((7 * n,)), pltpu.SemaphoreType.DMA((7 * n,)),
                        pltpu.SemaphoreType.DMA((n,))]

    def _copies(self, ins, outs, sems):
        x, y, c = lax.axis_index("x"), lax.axis_index("y"), lax.axis_index("c")
        mine = 4 * x + 2 * y + c
        remote = [pltpu.make_async_remote_copy(
            src_ref=ins[a], dst_ref=outs[a].at[mine], send_sem=sems[0].at[7 * a + k], recv_sem=sems[1].at[7 * a + k],
            device_id=(1 - x if fx else x, 1 - y if fy else y, 1 - c if fc else c), device_id_type=MESH)
            for a in range(self.n) for k, (fx, fy, fc) in enumerate(self.FLIPS)]
        local = [pltpu.make_async_copy(ins[a], outs[a].at[mine], sems[2].at[a]) for a in range(self.n)]
        return remote + local

    def start(self, ins, outs, sems):
        for cp in self._copies(ins, outs, sems):
            cp.start()

    def forward(self, ins, outs, sems):
        pass

    def finish(self, ins, outs, sems):
        for cp in self._copies(ins, outs, sems):
            cp.wait()


class _ChipExchange:
    def __init__(self, arrays):
        self.n = n = len(arrays)
        self.out_shape = [jax.ShapeDtypeStruct(a.shape, a.dtype) for a in arrays]
        self.scratch = [pltpu.SemaphoreType.DMA((3 * n,)), pltpu.SemaphoreType.DMA((3 * n,))]

    def _copies(self, ins, outs, sems):
        x, y, c = lax.axis_index("x"), lax.axis_index("y"), lax.axis_index("c")
        return [pltpu.make_async_remote_copy(
            src_ref=ins[a].at[2 * px + py], dst_ref=outs[a].at[2 * x + y], send_sem=sems[0].at[3 * a + j],
            recv_sem=sems[1].at[3 * a + j], device_id=(px, py, c), device_id_type=MESH)
            for a in range(self.n) for j, (px, py) in enumerate([(1 - x, y), (x, 1 - y), (1 - x, 1 - y)])]

    def start(self, ins, outs, sems):
        for cp in self._copies(ins, outs, sems):
            cp.start()

    def forward(self, ins, outs, sems):
        pass

    def finish(self, ins, outs, sems):
        for cp in self._copies(ins, outs, sems):
            cp.wait()


class _SiblingExchange:
    def __init__(self, arrays, windowed):
        self.n = n = len(arrays)
        self.windowed = windowed
        self.out_shape = [jax.ShapeDtypeStruct((4, a.shape[0], EXP_W) if w else (4,) + a.shape[1:], a.dtype)
                          for a, w in zip(arrays, windowed)]
        self.scratch = [pltpu.SemaphoreType.DMA((4 * n,)), pltpu.SemaphoreType.DMA((4 * n,))]

    def _each(self, ins, outs, sems, act):
        x, y, c = lax.axis_index("x"), lax.axis_index("y"), lax.axis_index("c")

        def branch(c_val):
            for k in range(4):
                e = 2 * k + (1 - c_val)
                for a in range(self.n):
                    src = ins[a].at[:, pl.ds(W_LO[e], EXP_W)] if self.windowed[a] else ins[a].at[e]
                    act(pltpu.make_async_remote_copy(
                        src_ref=src, dst_ref=outs[a].at[k], send_sem=sems[0].at[4 * a + k], recv_sem=sems[1].at[4 * a + k],
                        device_id=(x, y, 1 - c), device_id_type=MESH))

        for c_val in (0, 1):
            pl.when(c == c_val)(functools.partial(branch, c_val))

    def start(self, ins, outs, sems):
        self._each(ins, outs, sems, lambda cp: cp.start())

    def forward(self, ins, outs, sems):
        pass

    def finish(self, ins, outs, sems):
        self._each(ins, outs, sems, lambda cp: cp.wait())


def _exchange(rider, arrays, name):
    n = len(arrays)

    def body(*refs):
        ins, outs, sems = refs[:n], refs[n:n + len(rider.out_shape)], refs[n + len(rider.out_shape):]
        rider.start(ins, outs, sems)
        rider.forward(ins, outs, sems)
        rider.finish(ins, outs, sems)

    return pl.pallas_call(body, name=name, out_shape=rider.out_shape, in_specs=[ANY] * n,
                          out_specs=[ANY] * len(rider.out_shape), scratch_shapes=rider.scratch)(*arrays)


def _add_window(dw_in, recv, lo_tiles):
    k, rows, _ = recv.shape
    tiles = EXP_W // LANES

    def body(t_ref, *refs):
        del t_ref
        r_ref, o_ref = refs[tiles], refs[tiles + 1]
        own = jnp.concatenate([w_ref[...] for w_ref in refs[:tiles]], axis=1)
        o_ref[0] = (own.astype(F32) + r_ref[0].astype(F32)).astype(o_ref.dtype)

    def tile(j):
        return pl.BlockSpec((rows, LANES), lambda i, t: (0, t[i] + j))

    spec = pl.BlockSpec((1, rows, EXP_W), lambda i, t: (i, 0, 0))
    grid_spec = pltpu.PrefetchScalarGridSpec(
        num_scalar_prefetch=1, grid=(k,), in_specs=[tile(j) for j in range(tiles)] + [spec], out_specs=spec)
    return pl.pallas_call(
        body, name="rs_add_in", grid_spec=grid_spec, out_shape=jax.ShapeDtypeStruct(recv.shape, recv.dtype),
        compiler_params=_params(("parallel",)),
    )(lo_tiles, *([dw_in] * tiles), recv)


def _final_sum(p, r, sel, name, unshift=False, keep_t=None):
    _, rows, cols = p.shape
    tr = 512 if rows % 512 == 0 else rows

    def body(sel_ref, p_ref, r0, r1, r2, r3, o_ref):
        own = p_ref[0].astype(F32)
        acc = None
        for k, r_ref in enumerate((r0, r1, r2, r3)):
            term = jnp.where(sel_ref[0] == k, own, r_ref[0].astype(F32))
            acc = term if acc is None else acc + term
        if unshift:
            acc = pltpu.roll(acc, sel_ref[5], 1)
        o_ref[...] = acc if keep_t is None else acc.T[:keep_t]

    def slot(k):
        return pl.BlockSpec((1, tr, cols), lambda i, t: (t[k], i, 0))

    if keep_t is None:
        out_spec, out_shape = pl.BlockSpec((tr, cols), lambda i, t: (i, 0)), (rows, cols)
    else:
        out_spec, out_shape = pl.BlockSpec((keep_t, tr), lambda i, t: (0, i)), (keep_t, rows)
    grid_spec = pltpu.PrefetchScalarGridSpec(
        num_scalar_prefetch=1, grid=(rows // tr,), in_specs=[slot(0), slot(1), slot(2), slot(3), slot(4)],
        out_specs=out_spec)
    return pl.pallas_call(
        body, name=name, grid_spec=grid_spec, out_shape=jax.ShapeDtypeStruct(out_shape, F32),
        compiler_params=_params(("parallel",)),
    )(sel, p, r, r, r, r)


def _expand_w_in(w_t, shift):
    cw, rows = w_t.shape
    tr = TM_MM
    pad = -cw % LANES

    def body(s_ref, w_ref, o_ref):
        w = jnp.concatenate([w_ref[...], jnp.zeros((pad, tr), F32)], axis=0).T
        w = jnp.concatenate([w, jnp.zeros((tr, EXP_W - cw - pad), F32)], axis=1)
        o_ref[...] = pltpu.roll(w, s_ref[0], 1).astype(BF16)

    grid_spec = pltpu.PrefetchScalarGridSpec(
        num_scalar_prefetch=1, grid=(rows // tr,), in_specs=[pl.BlockSpec((cw, tr), lambda i, t: (0, i))],
        out_specs=pl.BlockSpec((tr, EXP_W), lambda i, t: (i, 0)))
    return pl.pallas_call(
        body, name="expand_w_in", grid_spec=grid_spec, out_shape=jax.ShapeDtypeStruct((rows, EXP_W), BF16),
        compiler_params=_params(("arbitrary",)),
    )(shift, w_t)


def _pad_wq(w_t):
    cw, rows = w_t.shape

    def body(w_ref, o_ref):
        o_ref[...] = jnp.concatenate([w_ref[...], jnp.zeros((QK_PAD - cw, rows), F32)], axis=0).T.astype(BF16)

    return pl.pallas_call(
        body, name="pad_wq", out_shape=jax.ShapeDtypeStruct((rows, QK_PAD), BF16), compiler_params=_params(),
    )(w_t)


def _merge_w_in(e):
    _, rows, _ = e.shape
    tr = TM_MM

    def body(e_ref, o_ref):
        for t in range(U_COLS // LANES):
            lo, hi = t * LANES, (t + 1) * LANES
            parts = [e_ref[d, :, lo - W_LO[d]:hi - W_LO[d]] for d in range(N_DEV)
                     if CW * d < hi and CW * (d + 1) > lo]
            if not parts:
                tile = jnp.zeros((tr, LANES), BF16)
            elif len(parts) == 1:
                tile = parts[0]
            else:
                tile = (parts[0].astype(F32) + parts[1].astype(F32)).astype(BF16)
            o_ref[:, lo:hi] = tile

    return pl.pallas_call(
        body, name="merge_w_in", grid=(rows // tr,),
        in_specs=[pl.BlockSpec((N_DEV, tr, EXP_W), lambda i: (0, i, 0))],
        out_specs=pl.BlockSpec((tr, U_COLS), lambda i: (i, 0)), out_shape=jax.ShapeDtypeStruct((rows, U_COLS), BF16),
        compiler_params=_params(("parallel",)),
    )(e)


def _sum_leading(a, out_dtype, name):
    k, rows, cols = a.shape
    tr = min(rows, 1728 if rows % 1728 == 0 else rows)

    def body(a_ref, o_ref):
        acc = a_ref[0].astype(F32)
        for i in range(1, k):
            acc = acc + a_ref[i].astype(F32)
        o_ref[...] = acc.astype(out_dtype)

    return pl.pallas_call(
        body, name=name, grid=(rows // tr,),
        in_specs=[pl.BlockSpec((k, tr, cols), lambda i: (0, i, 0))],
        out_specs=pl.BlockSpec((tr, cols), lambda i: (i, 0)),
        out_shape=jax.ShapeDtypeStruct((rows, cols), out_dtype), compiler_params=_params(("parallel",)),
    )(a)


def _add_pairs(gs, recvs, core, name):
    n = len(gs)

    def body(c_ref, *refs):
        del c_ref
        for a in range(n):
            refs[2 * n + a][...] = (refs[a][...].astype(F32) + refs[n + a][...].astype(F32)).astype(refs[2 * n + a].dtype)

    def mine(r):
        return pl.BlockSpec((1,) + r.shape[1:], lambda i, c: (2 * i + c[0], 0, 0))

    def kth(r):
        return pl.BlockSpec((1,) + r.shape[1:], lambda i, c: (i, 0, 0))

    grid_spec = pltpu.PrefetchScalarGridSpec(
        num_scalar_prefetch=1, grid=(4,), in_specs=[mine(r) for r in recvs] + [kth(r) for r in recvs],
        out_specs=[kth(r) for r in recvs])
    return pl.pallas_call(
        body, name=name, grid_spec=grid_spec, out_shape=[jax.ShapeDtypeStruct(r.shape, r.dtype) for r in recvs],
        compiler_params=_params(("parallel",)),
    )(core, *gs, *recvs)


def _final_sums(ps, rs, sel, keep_t, name):
    n = len(ps)

    def body(sel_ref, *refs):
        for a in range(n):
            own = refs[5 * a][0].astype(F32)
            acc = None
            for k in range(4):
                term = jnp.where(sel_ref[0] == k, own, refs[5 * a + 1 + k][0].astype(F32))
                acc = term if acc is None else acc + term
            refs[5 * n + a][...] = acc if keep_t[a] is None else acc.T[:keep_t[a]]

    def slot(p, k):
        return pl.BlockSpec((1,) + p.shape[1:], lambda i, t: (t[k], 0, 0))

    out_shapes = [p.shape[1:] if kt is None else (kt, p.shape[1]) for p, kt in zip(ps, keep_t)]
    grid_spec = pltpu.PrefetchScalarGridSpec(
        num_scalar_prefetch=1, grid=(1,), in_specs=[slot(p, k) for p in ps for k in range(5)],
        out_specs=[pl.BlockSpec(sh, lambda i, t: (0, 0)) for sh in out_shapes])
    operands = [x for p, r in zip(ps, rs) for x in (p, r, r, r, r)]
    return pl.pallas_call(
        body, name=name, grid_spec=grid_spec, out_shape=[jax.ShapeDtypeStruct(sh, F32) for sh in out_shapes],
        compiler_params=_params(("arbitrary",)),
    )(sel, *operands)


def _ada_mod(c16, ada_w_l, ada_b_l):
    def body(c_ref, w_ref, b_ref, o_ref):
        cv = c_ref[...]
        sc = (cv * _sigmoid(cv)).astype(BF16)
        o_ref[...] = _dot(sc, w_ref[...].astype(BF16)) + b_ref[...]

    return pl.pallas_call(
        body, name="ada_mod", out_shape=jax.ShapeDtypeStruct((c16.shape[0], ada_w_l.shape[1]), F32),
        compiler_params=_params(),
    )(c16, ada_w_l, ada_b_l)


def _ada_w_grad(c_t, dmod_my):
    def body(c_ref, d_ref, o_ref):
        cv = c_ref[...]
        sc = cv * _sigmoid(cv)
        acc = sc[:, 0:1] * d_ref[0:1, :]
        for b in range(1, N_DEV):
            acc = acc + sc[:, b:b + 1] * d_ref[b:b + 1, :]
        o_ref[...] = acc

    return pl.pallas_call(
        body, name="ada_w_grad", out_shape=jax.ShapeDtypeStruct((c_t.shape[0], dmod_my.shape[1]), F32),
        compiler_params=_params(),
    )(c_t, dmod_my)


def _norm_mod(x, norm_g, mod, pos_col, invf, sign, rider, rider_inputs):
    s, d = x.shape
    tm = min(TM_MM, s)
    n_in, n_out = len(rider_inputs), len(rider.out_shape)
    steps = s // tm

    def body(x_ref, g_ref, mod_ref, p_ref, f_ref, s_ref, *rest):
        r_ins, (h_ref, ht_ref, cos_ref, sin_ref) = rest[:n_in], rest[n_in:n_in + 4]
        r_outs, sems = rest[n_in + 4:n_in + 4 + n_out], rest[n_in + 4 + n_out:]
        pl.when(pl.program_id(0) == 0)(functools.partial(rider.start, r_ins, r_outs, sems))
        xv = x_ref[...]
        r = lax.rsqrt(jnp.mean(xv * xv, axis=-1, keepdims=True) + EPS)
        hn = xv * r * g_ref[...]
        hv = hn * (1.0 + mod_ref[:, d:2 * d]) + mod_ref[:, 0:d]
        h_ref[...] = hv.astype(BF16)
        ht_ref[...] = hv.T.astype(BF16)
        ang = p_ref[...].astype(F32) * f_ref[...]
        sg = s_ref[...]
        cos_ref[...] = jnp.cos(ang) * jnp.abs(sg)
        sin_ref[...] = jnp.sin(ang) * sg

        @pl.when(pl.program_id(0) == steps - 1)
        def _():
            rider.forward(r_ins, r_outs, sems)
            rider.finish(r_ins, r_outs, sems)

    row = pl.BlockSpec((1, LANES), lambda i: (0, 0))
    tab = pl.BlockSpec((tm, LANES), lambda i: (i, 0))
    return pl.pallas_call(
        body, name="norm_mod", grid=(steps,),
        in_specs=[pl.BlockSpec((tm, d), lambda i: (i, 0)), pl.BlockSpec((1, d), lambda i: (0, 0)),
                  pl.BlockSpec((1, 3 * d), lambda i: (0, 0)), pl.BlockSpec((tm, 1), lambda i: (i, 0)), row, row]
        + [ANY] * n_in,
        out_specs=[pl.BlockSpec((tm, d), lambda i: (i, 0)), pl.BlockSpec((d, tm), lambda i: (0, i)), tab, tab] + [ANY] * n_out,
        out_shape=[jax.ShapeDtypeStruct((s, d), BF16), jax.ShapeDtypeStruct((d, s), BF16),
                   jax.ShapeDtypeStruct((s, LANES), F32), jax.ShapeDtypeStruct((s, LANES), F32)] + rider.out_shape,
        scratch_shapes=rider.scratch, compiler_params=_params(("arbitrary",)),
    )(x, norm_g, mod, pos_col, invf, sign, *rider_inputs)


def _matmul(a, b, *, nt, out_dtype, tm, tn, name, rider=None, rider_inputs=(), a_resident=False):
    m, kdim = a.shape
    n = b.shape[0] if nt else b.shape[1]
    tm, tn = min(tm, m), min(tn, n)
    n_in = len(rider_inputs)
    n_out = len(rider.out_shape) if rider else 0
    m_steps, n_steps = m // tm, n // tn
    steps = n_steps * m_steps
    inner = n_steps if a_resident else m_steps
    tile = (lambda o, i: (o, i)) if a_resident else (lambda o, i: (i, o))

    def body(a_ref, b_ref, *rest):
        r_ins, o_ref, r_outs, sems = rest[:n_in], rest[n_in], rest[n_in + 1:n_in + 1 + n_out], rest[n_in + 1 + n_out:]
        step = pl.program_id(0) * inner + pl.program_id(1)
        if rider:
            pl.when(step == 0)(functools.partial(rider.start, r_ins, r_outs, sems))
            pl.when(step == steps // 2)(functools.partial(rider.forward, r_ins, r_outs, sems))
        o = _dot_nt(a_ref[...], b_ref[...]) if nt else _dot(a_ref[...], b_ref[...])
        o_ref[...] = o.astype(out_dtype)
        if rider:
            pl.when(step == steps - 1)(functools.partial(rider.finish, r_ins, r_outs, sems))

    if nt:
        b_spec = pl.BlockSpec((tn, kdim), lambda o, i: (tile(o, i)[1], 0))
    else:
        b_spec = pl.BlockSpec((kdim, tn), lambda o, i: (0, tile(o, i)[1]))
    out = pl.pallas_call(
        body, name=name, grid=(m_steps, n_steps) if a_resident else (n_steps, m_steps),
        in_specs=[pl.BlockSpec((tm, kdim), lambda o, i: (tile(o, i)[0], 0)), b_spec] + [ANY] * n_in,
        out_specs=[pl.BlockSpec((tm, tn), tile)] + [ANY] * n_out,
        out_shape=[jax.ShapeDtypeStruct((m, n), out_dtype)] + (rider.out_shape if rider else []),
        scratch_shapes=rider.scratch if rider else [],
        compiler_params=_params(("arbitrary", "arbitrary") if rider else ("parallel", "parallel")),
    )(a, b, *rider_inputs)
    return out if rider else out[0]


HALO = 16


def _conv_specs(tm):
    def col(j):
        return pl.BlockSpec((tm, D_CONV), lambda i: (i, j))

    def prev(j):
        return pl.BlockSpec((HALO, D_CONV), lambda i: (jnp.maximum(i * (tm // HALO) - 1, 0), j))

    return [col(0), col(1), col(2), col(3), prev(0), prev(2)]


def _conv_y(xc_ref, bc_ref, cc_ref, zc_ref, xp_ref, cp_ref, w_ref, first):
    uc = cc_ref[...].astype(F32) * xc_ref[...].astype(F32)
    up = jnp.where(first, 0.0, cp_ref[...].astype(F32) * xp_ref[...].astype(F32))
    full = jnp.concatenate([up, uc], axis=0)
    u1 = pltpu.roll(full, 1, 0)[HALO:]
    u2 = pltpu.roll(full, 2, 0)[HALO:]
    w = w_ref[...]
    conv = w[0:1] * u2 + w[1:2] * u1 + w[2:3] * uc
    z = zc_ref[...].astype(F32)
    return bc_ref[...].astype(F32) * conv * (z * _sigmoid(z))


def _conv_bwd(u, dyc, conv_w):
    s = u.shape[0]
    tm = min(TM_MM, s)
    cb = D_CONV
    nt = s // tm

    def body(xc_ref, bc_ref, cc_ref, zc_ref, xp_ref, cp_ref, bn_ref, zn_ref, dy_ref, dyn_ref, w_ref, du_ref, dw_ref):
        i = pl.program_id(0)
        xc, cc = xc_ref[...].astype(F32), cc_ref[...].astype(F32)
        bc, z = bc_ref[...].astype(F32), zc_ref[...].astype(F32)
        uc = cc * xc
        up = jnp.where(i == 0, 0.0, cp_ref[...].astype(F32) * xp_ref[...].astype(F32))
        full = jnp.concatenate([up, uc], axis=0)
        u1 = pltpu.roll(full, 1, 0)[HALO:]
        u2 = pltpu.roll(full, 2, 0)[HALO:]
        w = w_ref[...]
        conv = w[0:1] * u2 + w[1:2] * u1 + w[2:3] * uc
        sg = _sigmoid(z)
        sz = z * sg
        dy = dy_ref[...].astype(F32)
        dconv = dy * bc * sz
        zn = zn_ref[...].astype(F32)
        dnext = dyn_ref[...].astype(F32) * bn_ref[...].astype(F32) * (zn * _sigmoid(zn))
        dnext = jnp.where(i == nt - 1, 0.0, dnext)
        fullb = jnp.concatenate([dconv, dnext], axis=0)
        nb = tm + HALO
        d1 = pltpu.roll(fullb, nb - 1, 0)[:tm]
        d2 = pltpu.roll(fullb, nb - 2, 0)[:tm]
        duc = w[2:3] * dconv + w[1:2] * d1 + w[0:1] * d2
        dzc = dy * bc * conv * (sg * (1.0 + z * (1.0 - sg)))
        du_ref[...] = jnp.concatenate([duc * cc, dy * conv * sz, duc * xc, dzc], axis=1).astype(BF16)
        dw = jnp.concatenate([jnp.sum(dconv * u2, axis=0, keepdims=True), jnp.sum(dconv * u1, axis=0, keepdims=True),
                              jnp.sum(dconv * uc, axis=0, keepdims=True), jnp.zeros((5, cb), F32)], axis=0)

        @pl.when(i == 0)
        def _():
            dw_ref[...] = dw

        @pl.when(i > 0)
        def _():
            dw_ref[...] += dw

    def col(j):
        return pl.BlockSpec((tm, cb), lambda i: (i, j))

    def prev(j):
        return pl.BlockSpec((HALO, cb), lambda i: (jnp.maximum(i * (tm // HALO) - 1, 0), j))

    def nxt(j):
        return pl.BlockSpec((HALO, cb), lambda i: (jnp.minimum((i + 1) * (tm // HALO), s // HALO - 1), j))

    return pl.pallas_call(
        body, name="conv_bwd", grid=(nt,),
        in_specs=[col(0), col(1), col(2), col(3), prev(0), prev(2), nxt(1), nxt(3), col(0), nxt(0),
                  pl.BlockSpec((3, cb), lambda i: (0, 0))],
        out_specs=[pl.BlockSpec((tm, 4 * cb), lambda i: (i, 0)), pl.BlockSpec((8, cb), lambda i: (0, 0))],
        out_shape=[jax.ShapeDtypeStruct((s, U_COLS), BF16), jax.ShapeDtypeStruct((8, cb), F32)],
        compiler_params=_params(("arbitrary",)),
    )(u, u, u, u, u, u, u, u, dyc, dyc, conv_w)


def _qkv_specs(tm):
    return [pl.BlockSpec((tm, Q_LORA), lambda i: (i, U_CQ // Q_LORA)),
            pl.BlockSpec((tm, KV_LORA), lambda i: (i, U_CKV // KV_LORA)),
            pl.BlockSpec((tm, LANES), lambda i: (i, U_KR // LANES)),
            pl.BlockSpec((tm, LANES), lambda i: (i, 0)), pl.BlockSpec((tm, LANES), lambda i: (i, 0))]


def _full(shape):
    return pl.BlockSpec(shape, lambda i: (0,) * len(shape))


def _k_rope_lanes(blk):
    lane = lax.broadcasted_iota(jnp.int32, blk.shape, 1)
    return jnp.where(lane < QK_ROPE, blk, 0.0)


def _qkv_fwd(u, cos, sin, wq, wkv, qag, kvag, qg, kg):
    s = u.shape[0]
    tm = min(TM_MM, s)

    def body(cq_ref, ckv_ref, kr_ref, cos_ref, sin_ref, wq_ref, wkv_ref, qag_ref, kvag_ref, qg_ref, kg_ref,
             q_ref, k_ref, v_ref):
        cq = cq_ref[...].astype(F32)
        cqn = (cq * lax.rsqrt(jnp.mean(cq * cq, axis=-1, keepdims=True) + EPS) * qag_ref[...]).astype(BF16)
        ckv = ckv_ref[...].astype(F32)
        ckvn = (ckv * lax.rsqrt(jnp.mean(ckv * ckv, axis=-1, keepdims=True) + EPS) * kvag_ref[...]).astype(BF16)
        kr = _k_rope_lanes(kr_ref[...].astype(F32))
        cosv, sinv, qgv, kgv = cos_ref[...], sin_ref[...], qg_ref[...], kg_ref[...]
        ss_r = jnp.sum(kr * kr, axis=-1, keepdims=True)
        krr = _rope(kr * kgv[:, QK_NOPE:], cosv, sinv)
        qf = _dot(cqn, wq_ref[...])
        kvf = _dot(ckvn, wkv_ref[...])
        heads = range(N_HEADS)
        qh = [qf[:, QK_PAD * h:QK_PAD * (h + 1)] for h in heads]
        kn = [kvf[:, 2 * V_HEAD * h:2 * V_HEAD * h + QK_NOPE] for h in heads]
        rq = [lax.rsqrt(jnp.sum(qh[h] * qh[h], axis=-1, keepdims=True) * (1.0 / QK_HEAD) + EPS) for h in heads]
        rk = [lax.rsqrt((jnp.sum(kn[h] * kn[h], axis=-1, keepdims=True) + ss_r) * (1.0 / QK_HEAD) + EPS) for h in heads]
        for h in heads:
            qn = qh[h] * rq[h] * qgv
            qo = jnp.concatenate([qn[:, :QK_NOPE], _rope(qn[:, QK_NOPE:], cosv, sinv)], axis=1) * (SCALE * LOG2E)
            q_ref[h] = qo.astype(BF16)
            vh = kvf[:, 2 * V_HEAD * h + QK_NOPE:2 * V_HEAD * (h + 1)]
            k_ref[h] = jnp.concatenate([kn[h] * kgv[:, :QK_NOPE] * rk[h], krr * rk[h]], axis=1).astype(BF16)
            v_ref[h] = jnp.concatenate([vh, jnp.ones_like(vh)], axis=1).astype(BF16)

    return pl.pallas_call(
        body, name="qkv_fwd", grid=(s // tm,),
        in_specs=_qkv_specs(tm) + [_full((Q_LORA, N_HEADS * QK_PAD)), _full((KV_LORA, 2 * D_ATTN)),
                                   _full((1, Q_LORA)), _full((1, KV_LORA)), _full((1, QK_PAD)), _full((1, QK_PAD))],
        out_specs=[pl.BlockSpec((N_HEADS, tm, QK_PAD), lambda i: (0, i, 0)),
                   pl.BlockSpec((N_HEADS, tm, QK_PAD), lambda i: (0, i, 0)),
                   pl.BlockSpec((N_HEADS, tm, 2 * V_HEAD), lambda i: (0, i, 0))],
        out_shape=[jax.ShapeDtypeStruct((N_HEADS, s, QK_PAD), BF16), jax.ShapeDtypeStruct((N_HEADS, s, QK_PAD), BF16),
                   jax.ShapeDtypeStruct((N_HEADS, s, 2 * V_HEAD), BF16)],
        compiler_params=_params(("parallel",)),
    )(u, u, u, cos, sin, wq, wkv, qag, kvag, qg, kg)


def _qkv_bwd(u, cos, sin, dq, dk, dv, dza, wq, wkv, qag, kvag, qg, kg, du):
    s = u.shape[0]
    tm = min(TM_ELEM, s)
    nt = s // tm

    def body(cq_ref, ckv_ref, kr_ref, cos_ref, sin_ref, dq_ref, dk_ref, dv_ref, dza_ref, wq_ref, wkv_ref, qag_ref,
             kvag_ref, qg_ref, kg_ref, du_in, du_ref, dwq_ref, dwkv_ref, dqag_ref, dkvag_ref, dqg_ref, dkg_ref,
             dwq_acc, dwkv_acc):
        del du_in
        i = pl.program_id(0)

        @pl.when(i == 0)
        def _():
            dwq_acc[...] = jnp.zeros_like(dwq_acc)
            dwkv_acc[...] = jnp.zeros_like(dwkv_acc)

        cq = cq_ref[...].astype(F32)
        rqa = lax.rsqrt(jnp.mean(cq * cq, axis=-1, keepdims=True) + EPS)
        xq = cq * rqa
        qagv = qag_ref[...]
        cqn = (xq * qagv).astype(BF16)
        ckv = ckv_ref[...].astype(F32)
        rkva = lax.rsqrt(jnp.mean(ckv * ckv, axis=-1, keepdims=True) + EPS)
        xkv = ckv * rkva
        kvagv = kvag_ref[...]
        ckvn = (xkv * kvagv).astype(BF16)
        kr = _k_rope_lanes(kr_ref[...].astype(F32))
        cosv, sinv, qgv, kgv = cos_ref[...], sin_ref[...], qg_ref[...], kg_ref[...]
        ss_r = jnp.sum(kr * kr, axis=-1, keepdims=True)
        dqg = jnp.zeros((1, QK_PAD), F32)
        dkg = jnp.zeros((1, QK_PAD), F32)
        dkr = jnp.zeros((tm, LANES), F32)
        qf = _dot(cqn, wq_ref[...])
        kvf = _dot(ckvn, wkv_ref[...])
        heads = range(N_HEADS)
        qh = [qf[:, QK_PAD * h:QK_PAD * (h + 1)] for h in heads]
        kn = [kvf[:, 2 * V_HEAD * h:2 * V_HEAD * h + QK_NOPE] for h in heads]
        rq = [lax.rsqrt(jnp.sum(qh[h] * qh[h], axis=-1, keepdims=True) * (1.0 / QK_HEAD) + EPS) for h in heads]
        rk = [lax.rsqrt((jnp.sum(kn[h] * kn[h], axis=-1, keepdims=True) + ss_r) * (1.0 / QK_HEAD) + EPS) for h in heads]
        xh = [qh[h] * rq[h] for h in heads]
        xk = [jnp.concatenate([kn[h], kr], axis=1) * rk[h] for h in heads]
        dyq, dyk = [], []
        for h in heads:
            g = dq_ref[h].astype(F32)
            dyq.append(jnp.concatenate([g[:, :QK_NOPE], _rope_t(g[:, QK_NOPE:], cosv, sinv)], axis=1))
            gk = dk_ref[h].astype(F32)
            dyk.append(jnp.concatenate([gk[:, :QK_NOPE], _rope_t(gk[:, QK_NOPE:], cosv, sinv)], axis=1))
        for h in heads:
            dqg = dqg + jnp.sum(dyq[h] * xh[h], axis=0, keepdims=True)
            dkg = dkg + jnp.sum(dyk[h] * xk[h], axis=0, keepdims=True)
        dqg = dqg * SCALE
        qgv = qgv * SCALE
        gdy = [dyq[h] * qgv for h in heads]
        gdyk = [dyk[h] * kgv for h in heads]
        tq_ = [jnp.sum(gdy[h] * xh[h], axis=-1, keepdims=True) * (1.0 / QK_HEAD) for h in heads]
        tk_ = [jnp.sum(gdyk[h] * xk[h], axis=-1, keepdims=True) * (1.0 / QK_HEAD) for h in heads]
        dqf = [(rq[h] * (gdy[h] - xh[h] * tq_[h])).astype(BF16) for h in heads]
        dkvf = []
        for h in heads:
            dxk = rk[h] * (gdyk[h] - xk[h] * tk_[h])
            dkr = dkr + dxk[:, QK_NOPE:]
            dkvf += [dxk[:, :QK_NOPE].astype(BF16), dv_ref[h]]
        dqf_b, dkvf_b = jnp.concatenate(dqf, axis=1), jnp.concatenate(dkvf, axis=1)
        dwq_acc[...] += _dot_tn(cqn, dqf_b)
        dwkv_acc[...] += _dot_tn(ckvn, dkvf_b)
        dcqn = _dot_nt(dqf_b, wq_ref[...])
        dckvn = _dot_nt(dkvf_b, wkv_ref[...])
        dqag = jnp.sum(dcqn * xq, axis=0, keepdims=True)
        dkvag = jnp.sum(dckvn * xkv, axis=0, keepdims=True)
        gq = dcqn * qagv
        dcq = rqa * (gq - xq * jnp.mean(gq * xq, axis=-1, keepdims=True))
        gkv = dckvn * kvagv
        dckv = rkva * (gkv - xkv * jnp.mean(gkv * xkv, axis=-1, keepdims=True))
        win = pltpu.roll(jnp.concatenate([dza_ref[...].astype(F32), jnp.zeros((tm, LANES), F32)], axis=1), QK_ROPE, 1)
        win = win + jnp.concatenate([dkr, jnp.zeros((tm, D_ATTN), F32)], axis=1)
        du_ref[...] = jnp.concatenate([dcq, dckv, win, jnp.zeros((tm, U_TAIL - ZA_LO - ZA_WIN), F32)], axis=1).astype(BF16)

        @pl.when(i == 0)
        def _():
            dqag_ref[...] = dqag
            dkvag_ref[...] = dkvag
            dqg_ref[...] = dqg
            dkg_ref[...] = dkg

        @pl.when(i > 0)
        def _():
            dqag_ref[...] += dqag
            dkvag_ref[...] += dkvag
            dqg_ref[...] += dqg
            dkg_ref[...] += dkg

        @pl.when(i == nt - 1)
        def _():
            dwq_ref[...] = dwq_acc[...].astype(BF16)
            dwkv_ref[...] = dwkv_acc[...].astype(BF16)

    head = lambda w: pl.BlockSpec((N_HEADS, tm, w), lambda i: (0, i, 0))
    wq_shape, wkv_shape = (Q_LORA, N_HEADS * QK_PAD), (KV_LORA, 2 * D_ATTN)
    return pl.pallas_call(
        body, name="qkv_bwd", grid=(nt,),
        in_specs=_qkv_specs(tm) + [head(QK_PAD), head(QK_PAD), head(V_HEAD), pl.BlockSpec((tm, D_ATTN), lambda i: (i, 0)),
                                   _full(wq_shape), _full(wkv_shape), _full((1, Q_LORA)), _full((1, KV_LORA)),
                                   _full((1, QK_PAD)), _full((1, QK_PAD)), ANY],
        out_specs=[pl.BlockSpec((tm, U_TAIL), lambda i: (i, U_COLS // U_TAIL - 1)), _full(wq_shape), _full(wkv_shape),
                   _full((1, Q_LORA)), _full((1, KV_LORA)), _full((1, QK_PAD)), _full((1, QK_PAD))],
        out_shape=[jax.ShapeDtypeStruct(du.shape, du.dtype), jax.ShapeDtypeStruct(wq_shape, BF16),
                   jax.ShapeDtypeStruct(wkv_shape, BF16), jax.ShapeDtypeStruct((1, Q_LORA), F32),
                   jax.ShapeDtypeStruct((1, KV_LORA), F32), jax.ShapeDtypeStruct((1, QK_PAD), F32),
                   jax.ShapeDtypeStruct((1, QK_PAD), F32)],
        scratch_shapes=[pltpu.VMEM(wq_shape, F32), pltpu.VMEM(wkv_shape, F32)],
        input_output_aliases={15: 0}, compiler_params=_params(("arbitrary",)),
    )(u, u, u, cos, sin, dq, dk, dv, dza, wq, wkv, qag, kvag, qg, kg, du)


def _flash_fwd(q, k, v):
    nh, s, _ = q.shape
    tq = min(TQ, s)
    nkv = KV_SPLIT
    tk = tq // nkv
    nq = s // tq
    nch = Q_CHAINS
    tc = tq // nch

    def body(q_ref, k_ref, v_ref, o_ref, lse_ref):
        i = pl.program_id(1)
        chains = [q_ref[0, r * tc:(r + 1) * tc, :] for r in range(nch)]

        def unit(r, j, carry, shift=None):
            m, acc = carry
            rows = pl.ds(pl.multiple_of(j * tk, tk), tk)
            sc = _dot_nt(chains[r], k_ref[0, rows, :])
            if shift is not None:
                qi = lax.broadcasted_iota(jnp.int32, sc.shape, 0)
                ki = lax.broadcasted_iota(jnp.int32, sc.shape, 1) + shift
                sc = jnp.where(ki <= qi, sc, NEG)
            m_new = jnp.maximum(m, jnp.max(sc, axis=-1, keepdims=True))
            p = jnp.exp2(sc - m_new).astype(BF16)
            return m_new, jnp.exp2(m - m_new) * acc + _dot(p, v_ref[0, rows, :])

        def trip(p, carry):
            for b in range(nkv):
                carry = tuple(unit(r, nkv * p + b, cr) for r, cr in enumerate(carry))
            return carry

        init = (jnp.full((tc, 1), NEG, F32), jnp.zeros((tc, 2 * V_HEAD), F32))
        carry = list(lax.fori_loop(0, i, trip, (init,) * nch))
        for b in range(nkv):
            for r in range(nch):
                shift = b * tk - r * tc
                if shift < tc:
                    carry[r] = unit(r, nkv * i + b, carry[r], None if shift + tk - 1 <= 0 else shift)
        for r, (m, acc) in enumerate(carry):
            l = acc[:, V_HEAD:]
            o_ref[r * tc:(r + 1) * tc, :] = (acc[:, :V_HEAD] / l).astype(BF16)
            lse = m + jnp.log(l[:, 0:1]) * LOG2E
            lse_ref[0, :, r * tc:(r + 1) * tc] = jnp.broadcast_to(lse, (tc, LANES)).T[0:1, :]

    return pl.pallas_call(
        body, name="flash_fwd", grid=(nh, nq),
        in_specs=[pl.BlockSpec((1, tq, QK_PAD), lambda h, i: (h, i, 0)),
                  pl.BlockSpec((1, s, QK_PAD), lambda h, i: (h, 0, 0)),
                  pl.BlockSpec((1, s, 2 * V_HEAD), lambda h, i: (h, 0, 0))],
        out_specs=[pl.BlockSpec((tq, V_HEAD), lambda h, i: (i, h)), pl.BlockSpec((1, 1, tq), lambda h, i: (h, 0, i))],
        out_shape=[jax.ShapeDtypeStruct((s, nh * V_HEAD), BF16), jax.ShapeDtypeStruct((nh, 1, s), F32)],
        compiler_params=_params(("parallel", "arbitrary")),
    )(q, k, v)


def _flash_bwd(q, k, v, do, lse, delta):
    nh, s, _ = q.shape
    tq = min(TQ, s)
    nq = s // tq
    kps = 2 if nq % 2 == 0 else 1
    ng = nq // kps

    def body(q_ref, k_ref, v_ref, do_ref, lse_ref, dl_ref, dq_ref, dk_ref, dv_ref, dq_acc):
        g = ng - 1 - pl.program_id(1)

        @pl.when(g == ng - 1)
        def _():
            dq_acc[...] = jnp.zeros_like(dq_acc)

        for sub in reversed(range(kps)):
            kv_block(q_ref, k_ref, v_ref, do_ref, lse_ref, dl_ref, dk_ref, dv_ref, dq_acc, g * kps + sub, sub)

        @pl.when(g == 0)
        def _():
            dq_ref[0] = dq_acc[...].astype(BF16)

    def kv_block(q_ref, k_ref, v_ref, do_ref, lse_ref, dl_ref, dk_ref, dv_ref, dq_acc, j, sub):
        own = slice(sub * tq, (sub + 1) * tq)
        kj, vj = k_ref[0, own, :], v_ref[0, own, :]

        def block(kk, vv, qq, dd, lse, dl, masked):
            st = _dot_nt(kk, qq)
            pt = jnp.exp2(st - lse)
            if masked:
                ki = lax.broadcasted_iota(jnp.int32, st.shape, 0)
                qx = lax.broadcasted_iota(jnp.int32, st.shape, 1)
                pt = jnp.where(ki <= qx, pt, 0.0)
            ddv = _dot(pt.astype(BF16), dd)
            dst = (pt * (_dot_nt(vv, dd) - dl)).astype(BF16)
            ddq = _dot_tn(dst, kk)
            return _dot(dst, qq), ddv, ddq

        def step(i, carry):
            dk, dv = carry
            rows = pl.ds(pl.multiple_of(i * tq, tq), tq)
            ddk, ddv, ddq = block(kj, vj, q_ref[0, rows, :], do_ref[rows, :], lse_ref[0, pl.ds(i, 1), :],
                                  dl_ref[0, pl.ds(i, 1), :], False)
            dq_acc[rows, :] += ddq
            return dk + ddk, dv + ddv

        th = tq // 2
        lse_j, dl_j = lse_ref[0, pl.ds(j, 1), :], dl_ref[0, pl.ds(j, 1), :]
        parts = []
        for kh, qh, masked in ((0, 0, True), (0, 1, False), (1, 1, True)):
            rows = pl.ds(pl.multiple_of(j * tq + qh * th, th), th)
            ks, qs = slice(kh * th, (kh + 1) * th), slice(qh * th, (qh + 1) * th)
            ddk, ddv, ddq = block(kj[ks], vj[ks], q_ref[0, rows, :], do_ref[rows, :], lse_j[:, qs], dl_j[:, qs], masked)
            dq_acc[rows, :] += ddq
            parts.append((ddk, ddv))
        carry = (jnp.concatenate([parts[0][0] + parts[1][0], parts[2][0]], axis=0),
                 jnp.concatenate([parts[0][1] + parts[1][1], parts[2][1]], axis=0))
        dk, dv = lax.fori_loop(j + 1, nq, step, carry)
        dk_ref[0, own, :] = (dk * LN2).astype(BF16)
        dv_ref[0, own, :] = dv.astype(BF16)

    return pl.pallas_call(
        body, name="flash_bwd", grid=(nh, ng),
        in_specs=[pl.BlockSpec((1, s, QK_PAD), lambda h, j: (h, 0, 0)),
                  pl.BlockSpec((1, kps * tq, QK_PAD), lambda h, g: (h, ng - 1 - g, 0)),
                  pl.BlockSpec((1, kps * tq, V_HEAD), lambda h, g: (h, ng - 1 - g, 0)),
                  pl.BlockSpec((s, V_HEAD), lambda h, j: (0, h)),
                  pl.BlockSpec((1, nq, tq), lambda h, j: (h, 0, 0)),
                  pl.BlockSpec((1, nq, tq), lambda h, j: (h, 0, 0))],
        out_specs=[pl.BlockSpec((1, s, QK_PAD), lambda h, j: (h, 0, 0)),
                   pl.BlockSpec((1, kps * tq, QK_PAD), lambda h, g: (h, ng - 1 - g, 0)),
                   pl.BlockSpec((1, kps * tq, V_HEAD), lambda h, g: (h, ng - 1 - g, 0))],
        out_shape=[jax.ShapeDtypeStruct((nh, s, QK_PAD), BF16), jax.ShapeDtypeStruct((nh, s, QK_PAD), BF16),
                   jax.ShapeDtypeStruct((nh, s, V_HEAD), BF16)],
        scratch_shapes=[pltpu.VMEM((s, QK_PAD), F32)],
        compiler_params=_params(("parallel", "arbitrary")),
    )(q, k, v, do, lse, delta)


def _tail(x, target, o, u, mod, w_out, conv_w):
    s, d = x.shape
    tm = min(TM_ELEM, s)

    def body(x_ref, t_ref, o_ref, za_ref, mod_ref, w_ref, xc_ref, bc_ref, cc_ref, zc_ref, xp_ref, cp_ref, cw_ref,
             gx_ref, dy_ref, ycat_ref, dyc_ref, do_ref, du_ref, delta_ref, dgate_ref, loss_ref):
        i = pl.program_id(0)
        za = pltpu.roll(za_ref[:, ZA_LO:ZA_LO + ZA_WIN].astype(F32), ZA_WIN - QK_ROPE, 1)[:, :D_ATTN]
        ov = o_ref[...].astype(F32)
        sg = _sigmoid(za)
        sl = za * sg
        ya = ov * sl
        y = _dot(ya.astype(BF16), w_ref[D_CONV:, :])
        yc = _conv_y(xc_ref, bc_ref, cc_ref, zc_ref, xp_ref, cp_ref, cw_ref, i == 0)
        y = y + _dot(yc.astype(BF16), w_ref[:D_CONV, :])
        ycat_ref[...] = jnp.concatenate([yc.T, ya.T], axis=0).astype(BF16)
        gate = mod_ref[:, 2 * d:3 * d]
        e = x_ref[...] + gate * y - t_ref[...]
        dout = e * (1.0 / d)
        gx_ref[...] = dout
        dy = (dout * gate).astype(BF16)
        dy_ref[...] = dy
        dycat = _dot_nt(dy, w_ref[...])
        dyc_ref[...] = dycat[:, :D_CONV].astype(BF16)
        dya = dycat[:, D_CONV:]
        dov = dya * sl
        do_ref[...] = dov.astype(BF16)
        du_ref[...] = (dya * ov * (sg * (1.0 + za * (1.0 - sg)))).astype(BF16)
        prod_t = (dov * ov).T
        for h in range(N_HEADS):
            delta_ref[h] = jnp.sum(prod_t[V_HEAD * h:V_HEAD * (h + 1), :], axis=0, keepdims=True)
        dgate = jnp.sum(dout * y, axis=0, keepdims=True)
        part = jnp.sum(jnp.sum(e * e, axis=0, keepdims=True), axis=1, keepdims=True) * (0.5 / d)
        part = jnp.broadcast_to(part, (1, LANES))

        @pl.when(i == 0)
        def _():
            dgate_ref[...] = dgate
            loss_ref[...] = part

        @pl.when(i > 0)
        def _():
            dgate_ref[...] += dgate
            loss_ref[...] += part

    tok = lambda w: pl.BlockSpec((tm, w), lambda i: (i, 0))
    return pl.pallas_call(
        body, name="tail", grid=(s // tm,),
        in_specs=[tok(d), tok(d), tok(D_ATTN), pl.BlockSpec((tm, U_TAIL), lambda i: (i, U_COLS // U_TAIL - 1)),
                  _full((1, 3 * d)), _full((d, d))] + _conv_specs(tm) + [_full((3, D_CONV))],
        out_specs=[tok(d), tok(d), pl.BlockSpec((d, tm), lambda i: (0, i)), tok(D_CONV), tok(D_ATTN), tok(D_ATTN),
                   pl.BlockSpec((N_HEADS, 1, tm), lambda i: (0, 0, i)), _full((1, d)), _full((1, LANES))],
        out_shape=[jax.ShapeDtypeStruct((s, d), F32), jax.ShapeDtypeStruct((s, d), BF16),
                   jax.ShapeDtypeStruct((d, s), BF16), jax.ShapeDtypeStruct((s, D_CONV), BF16),
                   jax.ShapeDtypeStruct((s, D_ATTN), BF16), jax.ShapeDtypeStruct((s, D_ATTN), BF16),
                   jax.ShapeDtypeStruct((N_HEADS, 1, s), F32), jax.ShapeDtypeStruct((1, d), F32),
                   jax.ShapeDtypeStruct((1, LANES), F32)],
        compiler_params=_params(("arbitrary",)),
    )(x, target, o, u, mod, w_out, u, u, u, u, u, u, conv_w)


def _norm_bwd(x, dh, gx1, norm_g, mod):
    s, d = x.shape
    tm = min(TM_MM, s)

    def body(x_ref, dh_ref, gx_ref, g_ref, mod_ref, o_ref, dshift_ref, dscale_ref, dg_ref):
        i = pl.program_id(0)
        gv, sc1 = g_ref[...], 1.0 + mod_ref[:, d:2 * d]
        gsc = gv * sc1
        half = NORM_ROWS // 2

        def group(c, acc):
            a_dh, a_dhxn = acc
            ks = range(NORM_GROUP)
            rows = [pl.ds(pl.multiple_of((c * NORM_GROUP + k) * NORM_ROWS, NORM_ROWS), NORM_ROWS) for k in ks]
            xv = [x_ref[rows[k], :] for k in ks]
            dhv = [dh_ref[rows[k], :].astype(F32) for k in ks]
            r = [lax.rsqrt(jnp.mean(xv[k] * xv[k], axis=-1, keepdims=True) + EPS) for k in ks]
            xn = [xv[k] * r[k] for k in ks]
            dxn = [dhv[k] * gsc for k in ks]
            t = [jnp.mean(dxn[k] * xn[k], axis=-1, keepdims=True) for k in ks]
            for k in ks:
                o_ref[rows[k], :] = gx_ref[rows[k], :] + r[k] * (dxn[k] - xn[k] * t[k])
                dhxn = dhv[k] * xn[k]
                a_dh = a_dh + dhv[k][:half] + dhv[k][half:]
                a_dhxn = a_dhxn + dhxn[:half] + dhxn[half:]
            return a_dh, a_dhxn

        zero = jnp.zeros((half, d), F32)
        a_dh, a_dhxn = lax.fori_loop(0, tm // (NORM_ROWS * NORM_GROUP), group, (zero, zero))
        dshift = jnp.sum(a_dh, axis=0, keepdims=True)
        s_dhxn = jnp.sum(a_dhxn, axis=0, keepdims=True)
        dscale, dg = s_dhxn * gv, s_dhxn * sc1

        @pl.when(i == 0)
        def _():
            dshift_ref[...] = dshift
            dscale_ref[...] = dscale
            dg_ref[...] = dg

        @pl.when(i > 0)
        def _():
            dshift_ref[...] += dshift
            dscale_ref[...] += dscale
            dg_ref[...] += dg

    tok = pl.BlockSpec((tm, d), lambda i: (i, 0))
    row = jax.ShapeDtypeStruct((1, d), F32)
    return pl.pallas_call(
        body, name="norm_bwd", grid=(s // tm,),
        in_specs=[tok, tok, tok, _full((1, d)), _full((1, 3 * d))],
        out_specs=[tok, _full((1, d)), _full((1, d)), _full((1, d))],
        out_shape=[jax.ShapeDtypeStruct((s, d), F32), row, row, row],
        compiler_params=_params(("arbitrary",)),
    )(x, dh, gx1, norm_g, mod)


def _adam_update(w, g, m, v):
    nm = ADAM_B1 * m + (1.0 - ADAM_B1) * g
    nv = ADAM_B2 * v + (1.0 - ADAM_B2) * (g * g)
    m_hat = nm / (1.0 - ADAM_B1 ** ADAM_STEP)
    v_hat = nv / (1.0 - ADAM_B2 ** ADAM_STEP)
    return -ADAM_LR * (m_hat / (jnp.sqrt(v_hat) + ADAM_EPS) + ADAM_WD * w), nm, nv


def _adamw_small(ws, gs, ms, vs):
    n = len(ws)

    def body(*refs):
        for a in range(n):
            w_ref, g_ref, m_ref, v_ref = (refs[k * n + a] for k in range(4))
            refs[4 * n + a][...], refs[5 * n + a][...], refs[6 * n + a][...] = _adam_update(
                w_ref[...], g_ref[...], m_ref[...], v_ref[...])

    shapes = [jax.ShapeDtypeStruct(w.shape, F32) for w in ws]
    out = pl.pallas_call(body, name="adamw_small", out_shape=shapes * 3, compiler_params=_params())(*ws, *gs, *ms, *vs)
    return out[:n], out[n:2 * n], out[2 * n:]


def _adamw(w, g, m, v, name, rider=None, rider_inputs=()):
    rows, cols = w.shape
    tr = 256 if rows % 256 == 0 else rows
    tc = 512 if (rows > 256 and tr == rows and cols % 512 == 0) else cols
    n_in = len(rider_inputs)
    n_out = len(rider.out_shape) if rider else 0
    inner = cols // tc
    steps = (rows // tr) * inner

    def body(w_ref, g_ref, m_ref, v_ref, *rest):
        r_ins, (d_ref, nm_ref, nv_ref) = rest[:n_in], rest[n_in:n_in + 3]
        r_outs, sems = rest[n_in + 3:n_in + 3 + n_out], rest[n_in + 3 + n_out:]
        step = pl.program_id(0) * inner + pl.program_id(1)
        if rider:
            pl.when(step == 0)(functools.partial(rider.start, r_ins, r_outs, sems))
            pl.when(step == steps // 2)(functools.partial(rider.forward, r_ins, r_outs, sems))
        d_ref[...], nm_ref[...], nv_ref[...] = _adam_update(w_ref[...], g_ref[...], m_ref[...], v_ref[...])
        if rider:
            pl.when(step == steps - 1)(functools.partial(rider.finish, r_ins, r_outs, sems))

    spec = pl.BlockSpec((tr, tc), lambda i, j: (i, j))
    shape = jax.ShapeDtypeStruct((rows, cols), F32)
    return pl.pallas_call(
        body, name=name, grid=(rows // tr, inner), in_specs=[spec] * 4 + [ANY] * n_in,
        out_specs=[spec] * 3 + [ANY] * n_out, out_shape=[shape] * 3 + (rider.out_shape if rider else []),
        scratch_shapes=rider.scratch if rider else [],
        compiler_params=_params(("arbitrary", "arbitrary") if rider else ("parallel", "parallel")),
    )(w, g, m, v, *rider_inputs)


def _pad_cols(a, n):
    return jnp.pad(a, ((0, 0), (0, n - a.shape[1])))


def kernel(x, c, positions, ada_w, ada_b, norm_g, w_in, conv_w, q_a_g, w_q_b, kv_a_g, w_kv_b, q_g, k_g, w_out, loss_target, m_ada_w, m_ada_b, m_norm_g, m_w_in, m_conv_w, m_q_a_g, m_w_q_b, m_kv_a_g, m_w_kv_b, m_q_g, m_k_g, m_w_out, v_ada_w, v_ada_b, v_norm_g, v_w_in, v_conv_w, v_q_a_g, v_w_q_b, v_kv_a_g, v_w_kv_b, v_q_g, v_k_g, v_w_out):
    me = _my_index()
    s = x.shape[1]
    nq = s // min(TQ, s)
    x2, tgt = x[0], loss_target[0]
    w_in_l, w_q_l, w_kv_l, w_out_l, conv_l, ada_w_l = w_in[0], w_q_b[0], w_kv_b[0], w_out[0], conv_w[0], ada_w[0]
    ada_cols = ada_w_l.shape[1]

    small = jnp.concatenate([c.reshape(-1, LANES), conv_l.reshape(-1, LANES), jnp.zeros((5, LANES), F32)], axis=0)
    (small_g,) = _exchange(_GatherDirect([small]), [small], "gather_c")
    c_all = small_g[:, :D_MODEL // LANES].reshape(N_DEV, D_MODEL)
    conv_g = small_g[:, D_MODEL // LANES:D_MODEL // LANES + 3].transpose(1, 0, 2).reshape(3, D_CONV)

    ada_b_l = lax.dynamic_slice(ada_b, (0, me * ada_cols), (1, ada_cols))
    mod_cols = _ada_mod(jnp.pad(c_all, ((0, 8), (0, 0))), ada_w_l, ada_b_l)[:N_DEV]
    (mod_g,) = _exchange(_GatherDirect([mod_cols]), [mod_cols], "gather_mod")
    mod = lax.dynamic_index_in_dim(mod_g, me, axis=1, keepdims=False).reshape(1, 3 * D_MODEL)

    half = jnp.arange(0, QK_ROPE, 2, dtype=F32) / QK_ROPE
    inv_freq = ROPE_BASE ** (-half)
    zeros64 = jnp.zeros((LANES - QK_ROPE,), F32)
    invf = jnp.concatenate([inv_freq, inv_freq, zeros64]).reshape(1, LANES)
    sign = jnp.concatenate([-jnp.ones((32,), F32), jnp.ones((32,), F32), zeros64]).reshape(1, LANES)
    qg_p, kg_p = _pad_cols(q_g, QK_PAD), _pad_cols(k_g, QK_PAD)

    my_off = ((CW * me) % LANES).astype(jnp.int32)
    win = [_expand_w_in(w_in_l.T, my_off.reshape(1))]
    h, h_t, cos, sin, win_g = _norm_mod(x2, norm_g, mod, positions.reshape(s, 1), invf, sign, _Gather(win, relay=True, parts=4), win)
    w_in_p = _merge_w_in(win_g)
    rest = [_pad_wq(w_q_l.T), w_kv_l.astype(BF16), w_out_l.astype(BF16)]
    u, wq_g, wkv_g, w_out_g = _matmul(h, w_in_p, nt=False, out_dtype=BF16, tm=2 * TM_MM, tn=2048, name="in_proj",
                                      rider=_Gather(rest), rider_inputs=rest)
    w_out_g = w_out_g.reshape(D_MODEL, D_MODEL)
    wq_g = wq_g.transpose(1, 0, 2).reshape(Q_LORA, N_HEADS * QK_PAD)
    wkv_g = wkv_g.transpose(1, 0, 2).reshape(KV_LORA, 2 * D_ATTN)
    q, k, v = _qkv_fwd(u, cos, sin, wq_g, wkv_g, q_a_g, kv_a_g, qg_p, kg_p)
    o, lse = _flash_fwd(q, k, v)
    gx1, dy, ycat_t, dyc, do, dza, delta, dgate, loss_row = _tail(x2, tgt, o, u, mod, w_out_g, conv_g)

    dq, dk, dv = _flash_bwd(q, k, v, do, lse.reshape(N_HEADS, nq, s // nq), delta.reshape(N_HEADS, nq, s // nq))
    du, dconv = _conv_bwd(u, dyc, conv_g)
    du, dwq, dwkv, dqag, dkvag, dqg, dkg = _qkv_bwd(u, cos, sin, dq, dk, dv, dza, wq_g, wkv_g, q_a_g, kv_a_g, qg_p, kg_p, du)
    dwq = dwq.reshape(Q_LORA, N_HEADS, QK_PAD).transpose(1, 0, 2)
    dwkv = dwkv.reshape(KV_LORA, N_HEADS, 2 * V_HEAD).transpose(1, 0, 2)
    dw_in = _matmul(h_t, du, nt=False, out_dtype=BF16, tm=TM_MM, tn=768, name="dw_in")
    first = [dw_in, dwq, dwkv]
    dw_out, r_in, r_q, r_kv = _matmul(ycat_t, dy, nt=False, out_dtype=BF16, tm=TM_MM, tn=512, name="dw_out",
                                      rider=_SiblingExchange(first, [True, False, False]), rider_inputs=first)
    dw_out = dw_out.reshape(N_DEV, D_MODEL // N_DEV, D_MODEL)
    (r_out,) = _exchange(_SiblingExchange([dw_out], [False]), [dw_out], "rs_sibling_out")
    core = lax.axis_index("c").astype(jnp.int32)
    lo_tiles = ((CW * (2 * jnp.arange(4, dtype=jnp.int32) + core)) // LANES).astype(jnp.int32)
    pairs = [_add_window(dw_in, r_in, lo_tiles),
             *_add_pairs([dwq, dwkv, dw_out], [r_q, r_kv, r_out], core.reshape(1), "rs_add_rest")]
    dh, *quads = _matmul(du, w_in_p, nt=True, out_dtype=BF16, tm=2 * TM_MM, tn=512, name="dh",
                         rider=_ChipExchange(pairs), rider_inputs=pairs, a_resident=True)
    my_chip = 2 * lax.axis_index("x") + lax.axis_index("y")
    written = jnp.where(jnp.arange(4) == my_chip, (jnp.arange(4) + 1) % 4, jnp.arange(4))
    sel = jnp.concatenate([my_chip.reshape(1), written, ((EXP_W - my_off) % EXP_W).reshape(1)]).astype(jnp.int32)
    g_w_in_t = _final_sum(pairs[0], quads[0], sel, "rs_sum_in", unshift=True, keep_t=CW)
    g_w_q_t, g_w_kv, g_w_out = _final_sums(pairs[1:], quads[1:], sel, [QK_HEAD, None, None], "rs_sum_rest")
    grad_x, dshift, dscale, dng = _norm_bwd(x2, dh, gx1, norm_g, mod)

    row = jnp.concatenate([dshift, dscale, dgate, dng, dqag, dkvag, dqg, dkg, dconv[:3].reshape(1, 3 * D_CONV), loss_row], axis=1)
    *adam_w_in, rows_g = _adamw(w_in_l.T, g_w_in_t, m_w_in[0].T, v_w_in[0].T, "adamw_w_in",
                                rider=_GatherDirect([row]), rider_inputs=[row])
    tot = _sum_leading(rows_g, F32, "sum_small")
    dmod_all = rows_g[:, 0, SM_MOD:SM_NG]
    g_ada_b = tot[:, SM_MOD:SM_NG]
    g_norm_g = tot[:, SM_NG:SM_QAG]
    g_q_a_g = tot[:, SM_QAG:SM_KVAG]
    g_kv_a_g = tot[:, SM_KVAG:SM_QG]
    g_q_g = tot[:, SM_QG:SM_QG + QK_HEAD]
    g_k_g = tot[:, SM_KG:SM_KG + QK_HEAD]
    conv_cols = conv_l.shape[1]
    g_conv = lax.dynamic_slice(tot[:, SM_CONV:SM_LOSS].reshape(3, D_CONV), (0, me * conv_cols), (3, conv_cols))
    loss = tot[0, SM_LOSS]
    dmod_my = lax.dynamic_slice(dmod_all, (0, me * ada_cols), (N_DEV, ada_cols))
    g_ada_w = _ada_w_grad(c_all.T, dmod_my)

    grads = dict(ada_w=g_ada_w, ada_b=g_ada_b, norm_g=g_norm_g, w_in=g_w_in_t, conv_w=g_conv, q_a_g=g_q_a_g, w_q_b=g_w_q_t,
                 kv_a_g=g_kv_a_g, w_kv_b=g_w_kv, q_g=g_q_g, k_g=g_k_g, w_out=g_w_out)
    weights = dict(ada_w=(ada_w, m_ada_w, v_ada_w), ada_b=(ada_b, m_ada_b, v_ada_b), norm_g=(norm_g, m_norm_g, v_norm_g),
                   w_in=(w_in, m_w_in, v_w_in), conv_w=(conv_w, m_conv_w, v_conv_w), q_a_g=(q_a_g, m_q_a_g, v_q_a_g),
                   w_q_b=(w_q_b, m_w_q_b, v_w_q_b), kv_a_g=(kv_a_g, m_kv_a_g, v_kv_a_g), w_kv_b=(w_kv_b, m_w_kv_b, v_w_kv_b),
                   q_g=(q_g, m_q_g, v_q_g), k_g=(k_g, m_k_g, v_k_g), w_out=(w_out, m_w_out, v_w_out))
    names = list(grads)
    big = ("ada_w", "w_in", "w_q_b", "w_kv_b", "w_out")
    small = [n for n in names if n not in big]

    def two_d(n, a):
        w = weights[n][0]
        return a.reshape(w.shape[-2:] if w.ndim == 3 else (1, w.shape[-1]))

    result = {}
    for n in big:
        transposed = n in ("w_in", "w_q_b")
        w2, m2, v2 = ((two_d(n, a).T if transposed else two_d(n, a)) for a in weights[n])
        g2 = grads[n] if transposed else two_d(n, grads[n])
        done = (g2, *(adam_w_in if n == "w_in" else _adamw(w2, g2, m2, v2, "adamw_" + n)))
        result[n] = [a.T if transposed else a for a in done]
    g_small = [two_d(n, grads[n]) for n in small]
    w_s, m_s, v_s = ([two_d(n, weights[n][k]) for n in small] for k in range(3))
    d_s, nm_s, nv_s = _adamw_small(w_s, g_small, m_s, v_s)
    for a, n in enumerate(small):
        result[n] = [g_small[a], d_s[a], nm_s[a], nv_s[a]]
    outs = [[result[n][k].reshape(weights[n][0].shape) for n in names] for k in range(4)]
    return (loss, grad_x.reshape(x.shape), *outs[0], *outs[1], *outs[2], *outs[3])
```

```python
import functools
import math

import jax
import jax.numpy as jnp
from jax import lax
from jax.experimental import pallas as pl
from jax.experimental.pallas import tpu as pltpu

F32 = jnp.float32
BF16 = jnp.bfloat16
MESH = pl.DeviceIdType.MESH

D_MODEL = 2048
D_CONV = 1024
N_HEADS = 8
QK_NOPE = 128
QK_ROPE = 64
QK_HEAD = QK_NOPE + QK_ROPE
V_HEAD = 128
D_ATTN = N_HEADS * V_HEAD
Q_LORA = 512
KV_LORA = 256
ROPE_BASE = 10000.0
IN_COLS = 4 * D_CONV + Q_LORA + KV_LORA + QK_ROPE + D_ATTN
EPS = 1e-6
ADAM_LR, ADAM_B1, ADAM_B2, ADAM_EPS, ADAM_WD, ADAM_STEP = 0.001, 0.9, 0.999, 1e-08, 0.01, 10

N_DEV = 8
LANES = 128
QK_PAD = 256
U_COLS = 6144
U_CQ, U_CKV, U_KR, U_ZA = 4096, 4608, 4864, 4928
U_TAIL = 2048
ZA_LO = U_ZA - (U_COLS - U_TAIL) - QK_ROPE
ZA_WIN = D_ATTN + LANES
CW = IN_COLS // 8
EXP_W = 896
W_LO = [(CW * d // 128) * 128 for d in range(8)]
W_OFF = [CW * d - lo for d, lo in enumerate(W_LO)]
SCALE = 1.0 / math.sqrt(QK_HEAD)
LOG2E = 1.4426950408889634
LN2 = 0.6931471805599453
NEG = -1e30
VMEM_LIMIT = 56 * 1024 * 1024

TM_ELEM = 256
NORM_ROWS = 16
NORM_GROUP = 4
TM_MM = 512
TQ = 1024
Q_CHAINS = 4
KV_SPLIT = 2

SM_MOD, SM_NG, SM_QAG, SM_KVAG, SM_QG, SM_KG, SM_CONV, SM_LOSS = 0, 6144, 8192, 8704, 8960, 9216, 9472, 12544
SM_COLS = 12672


def _params(sem=None):
    kw = dict(vmem_limit_bytes=VMEM_LIMIT)
    if sem is not None:
        kw["dimension_semantics"] = sem
    return pltpu.CompilerParams(**kw)


def _sigmoid(z):
    return 1.0 / (1.0 + jnp.exp(-z))


def _rot64(x):
    lane = lax.broadcasted_iota(jnp.int32, x.shape, 1)
    return jnp.where(lane < 32, pltpu.roll(x, 96, 1), pltpu.roll(x, 32, 1))


def _rope(x, cos, sin):
    return x * cos + _rot64(x) * sin


def _rope_t(d, cos, sin):
    return d * cos - _rot64(d) * sin


def _dot(a, b):
    return jnp.dot(a, b, preferred_element_type=F32)


def _dot_nt(a, b):
    return lax.dot_general(a, b, (((1,), (1,)), ((), ())), preferred_element_type=F32)


def _dot_tn(a, b):
    return lax.dot_general(a, b, (((0,), (0,)), ((), ())), preferred_element_type=F32)


def _my_index():
    return 4 * lax.axis_index("x") + 2 * lax.axis_index("y") + lax.axis_index("c")


ANY = pl.BlockSpec(memory_space=pl.ANY)


class _Gather:
    def __init__(self, blocks, relay=False, parts=1):
        self.relay = relay
        self.parts = parts
        self.rows = [b.shape[0] // parts for b in blocks]
        self.n = n = len(blocks) * parts
        self.out_shape = [jax.ShapeDtypeStruct((N_DEV,) + b.shape, b.dtype) for b in blocks]
        self.scratch = [pltpu.SemaphoreType.DMA((7 * n,)), pltpu.SemaphoreType.DMA((7 * n,)),
                        pltpu.SemaphoreType.DMA((n,))]

    @staticmethod
    def _places():
        x, y, c = lax.axis_index("x"), lax.axis_index("y"), lax.axis_index("c")
        return (x, y, c), (x, y, 1 - c), [(1 - x, y), (x, 1 - y), (1 - x, 1 - y)]

    def _src(self, ins, a):
        block, part = divmod(a, self.parts)
        return ins[block] if self.parts == 1 else ins[block].at[pl.ds(part * self.rows[block], self.rows[block])]

    def _dst(self, outs, a, place):
        block, part = divmod(a, self.parts)
        ref = outs[block].at[4 * place[0] + 2 * place[1] + place[2]]
        return ref if self.parts == 1 else ref.at[pl.ds(part * self.rows[block], self.rows[block])]

    def _copy(self, outs, sems, a, k, block, to, src=None):
        dst = self._dst(outs, a, block)
        return pltpu.make_async_remote_copy(
            src_ref=dst if src is None else src, dst_ref=dst, send_sem=sems[0].at[7 * a + k],
            recv_sem=sems[1].at[7 * a + k], device_id=to, device_id_type=MESH)

    def _first(self, ins, outs, sems):
        me, sibling, chips = self._places()
        first = []
        for a in range(self.n):
            first.append(self._copy(outs, sems, a, 0, me, sibling, src=self._src(ins, a)))
            first += [self._copy(outs, sems, a, 1 + j, me, (*chip, me[2]), src=self._src(ins, a))
                      for j, chip in enumerate(chips[:2] if self.relay else chips)]
        return first

    def _relays(self, outs, sems):
        if not self.relay:
            return []
        (x, y, c), _, _ = self._places()
        via = (jnp.where(c == 0, 1 - x, x), jnp.where(c == 0, y, 1 - y))
        to = (jnp.where(c == 0, x, 1 - x), jnp.where(c == 0, 1 - y, y))
        return [self._copy(outs, sems, a, 3, (*via, c), (*to, c)) for a in range(self.n)]

    def _passed(self, outs, sems):
        me, sibling, chips = self._places()
        return [self._copy(outs, sems, a, 4 + j, (*chip, me[2]), sibling)
                for a in range(self.n) for j, chip in enumerate(chips)]

    def _mine(self, ins, outs, sems):
        me, _, _ = self._places()
        return [pltpu.make_async_copy(self._src(ins, a), self._dst(outs, a, me), sems[2].at[a]) for a in range(self.n)]

    def start(self, ins, outs, sems):
        for cp in self._mine(ins, outs, sems) + self._first(ins, outs, sems):
            cp.start()

    def forward(self, ins, outs, sems):
        del ins
        me, _, chips = self._places()
        passed, relays = self._passed(outs, sems), self._relays(outs, sems)
        for a in range(self.n):
            for j, chip in enumerate(chips[:2] if self.relay else chips):
                self._copy(outs, sems, a, 1 + j, (*chip, me[2]), me).wait_recv()
                passed[3 * a + j].start()
            if self.relay:
                relays[a].start()
        if self.relay:
            for a in range(self.n):
                self._copy(outs, sems, a, 3, (*chips[2], me[2]), me).wait_recv()
                passed[3 * a + 2].start()

    def finish(self, ins, outs, sems):
        me, sibling, chips = self._places()
        for a in range(self.n):
            self._copy(outs, sems, a, 0, sibling, me).wait_recv()
            for j, chip in enumerate(chips):
                self._copy(outs, sems, a, 4 + j, (*chip, 1 - me[2]), me).wait_recv()
        for cp in self._first(ins, outs, sems) + self._relays(outs, sems) + self._passed(outs, sems):
            cp.wait_send()
        for cp in self._mine(ins, outs, sems):
            cp.wait()


class _GatherDirect:
    FLIPS = [(0, 0, 1), (1, 0, 0), (0, 1, 0), (1, 1, 0), (1, 0, 1), (0, 1, 1), (1, 1, 1)]

    def __init__(self, blocks):
        self.n = n = len(blocks)
        self.out_shape = [jax.ShapeDtypeStruct((N_DEV,) + b.shape, b.dtype) for b in blocks]
        self.scratch = [pltpu.SemaphoreType.DMA((7 * n,)), pltpu.SemaphoreType.DMA((7 * n,)),
                        pltpu.SemaphoreType.DMA((n,))]

    def _copies(self, ins, outs, sems):
        x, y, c = lax.axis_index("x"), lax.axis_index("y"), lax.axis_index("c")
        mine = 4 * x + 2 * y + c
        remote = [pltpu.make_async_remote_copy(
            src_ref=ins[a], dst_ref=outs[a].at[mine], send_sem=sems[0].at[7 * a + k], recv_sem=sems[1].at[7 * a + k],
            device_id=(1 - x if fx else x, 1 - y if fy else y, 1 - c if fc else c), device_id_type=MESH)
            for a in range(self.n) for k, (fx, fy, fc) in enumerate(self.FLIPS)]
        local = [pltpu.make_async_copy(ins[a], outs[a].at[mine], sems[2].at[a]) for a in range(self.n)]
        return remote + local

    def start(self, ins, outs, sems):
        for cp in self._copies(ins, outs, sems):
            cp.start()

    def forward(self, ins, outs, sems):
        pass

    def finish(self, ins, outs, sems):
        for cp in self._copies(ins, outs, sems):
            cp.wait()


class _ChipExchange:
    def __init__(self, arrays):
        self.n = n = len(arrays)
        self.out_shape = [jax.ShapeDtypeStruct(a.shape, a.dtype) for a in arrays]
        self.scratch = [pltpu.SemaphoreType.DMA((3 * n,)), pltpu.SemaphoreType.DMA((3 * n,))]

    def _copies(self, ins, outs, sems):
        x, y, c = lax.axis_index("x"), lax.axis_index("y"), lax.axis_index("c")
        return [pltpu.make_async_remote_copy(
            src_ref=ins[a].at[2 * px + py], dst_ref=outs[a].at[2 * x + y], send_sem=sems[0].at[3 * a + j],
            recv_sem=sems[1].at[3 * a + j], device_id=(px, py, c), device_id_type=MESH)
            for a in range(self.n) for j, (px, py) in enumerate([(1 - x, y), (x, 1 - y), (1 - x, 1 - y)])]

    def start(self, ins, outs, sems):
        for cp in self._copies(ins, outs, sems):
            cp.start()

    def forward(self, ins, outs, sems):
        pass

    def finish(self, ins, outs, sems):
        for cp in self._copies(ins, outs, sems):
            cp.wait()


class _SiblingExchange:
    def __init__(self, arrays, windowed):
        self.n = n = len(arrays)
        self.windowed = windowed
        self.out_shape = [jax.ShapeDtypeStruct((4, a.shape[0], EXP_W) if w else (4,) + a.shape[1:], a.dtype)
                          for a, w in zip(arrays, windowed)]
        self.scratch = [pltpu.SemaphoreType.DMA((4 * n,)), pltpu.SemaphoreType.DMA((4 * n,))]

    def _each(self, ins, outs, sems, act):
        x, y, c = lax.axis_index("x"), lax.axis_index("y"), lax.axis_index("c")

        def branch(c_val):
            for k in range(4):
                e = 2 * k + (1 - c_val)
                for a in range(self.n):
                    src = ins[a].at[:, pl.ds(W_LO[e], EXP_W)] if self.windowed[a] else ins[a].at[e]
                    act(pltpu.make_async_remote_copy(
                        src_ref=src, dst_ref=outs[a].at[k], send_sem=sems[0].at[4 * a + k], recv_sem=sems[1].at[4 * a + k],
                        device_id=(x, y, 1 - c), device_id_type=MESH))

        for c_val in (0, 1):
            pl.when(c == c_val)(functools.partial(branch, c_val))

    def start(self, ins, outs, sems):
        self._each(ins, outs, sems, lambda cp: cp.start())

    def forward(self, ins, outs, sems):
        pass

    def finish(self, ins, outs, sems):
        self._each(ins, outs, sems, lambda cp: cp.wait())


def _exchange(rider, arrays, name):
    n = len(arrays)

    def body(*refs):
        ins, outs, sems = refs[:n], refs[n:n + len(rider.out_shape)], refs[n + len(rider.out_shape):]
        rider.start(ins, outs, sems)
        rider.forward(ins, outs, sems)
        rider.finish(ins, outs, sems)

    return pl.pallas_call(body, name=name, out_shape=rider.out_shape, in_specs=[ANY] * n,
                          out_specs=[ANY] * len(rider.out_shape), scratch_shapes=rider.scratch)(*arrays)


def _add_window(dw_in, recv, lo_tiles):
    k, rows, _ = recv.shape
    tiles = EXP_W // LANES

    def body(t_ref, *refs):
        del t_ref
        r_ref, o_ref = refs[tiles], refs[tiles + 1]
        own = jnp.concatenate([w_ref[...] for w_ref in refs[:tiles]], axis=1)
        o_ref[0] = (own.astype(F32) + r_ref[0].astype(F32)).astype(o_ref.dtype)

    def tile(j):
        return pl.BlockSpec((rows, LANES), lambda i, t: (0, t[i] + j))

    spec = pl.BlockSpec((1, rows, EXP_W), lambda i, t: (i, 0, 0))
    grid_spec = pltpu.PrefetchScalarGridSpec(
        num_scalar_prefetch=1, grid=(k,), in_specs=[tile(j) for j in range(tiles)] + [spec], out_specs=spec)
    return pl.pallas_call(
        body, name="rs_add_in", grid_spec=grid_spec, out_shape=jax.ShapeDtypeStruct(recv.shape, recv.dtype),
        compiler_params=_params(("parallel",)),
    )(lo_tiles, *([dw_in] * tiles), recv)


def _final_sum(p, r, sel, name, unshift=False, keep_t=None):
    _, rows, cols = p.shape
    tr = 512 if rows % 512 == 0 else rows

    def body(sel_ref, p_ref, r0, r1, r2, r3, o_ref):
        own = p_ref[0].astype(F32)
        acc = None
        for k, r_ref in enumerate((r0, r1, r2, r3)):
            term = jnp.where(sel_ref[0] == k, own, r_ref[0].astype(F32))
            acc = term if acc is None else acc + term
        if unshift:
            acc = pltpu.roll(acc, sel_ref[5], 1)
        o_ref[...] = acc if keep_t is None else acc.T[:keep_t]

    def slot(k):
        return pl.BlockSpec((1, tr, cols), lambda i, t: (t[k], i, 0))

    if keep_t is None:
        out_spec, out_shape = pl.BlockSpec((tr, cols), lambda i, t: (i, 0)), (rows, cols)
    else:
        out_spec, out_shape = pl.BlockSpec((keep_t, tr), lambda i, t: (0, i)), (keep_t, rows)
    grid_spec = pltpu.PrefetchScalarGridSpec(
        num_scalar_prefetch=1, grid=(rows // tr,), in_specs=[slot(0), slot(1), slot(2), slot(3), slot(4)],
        out_specs=out_spec)
    return pl.pallas_call(
        body, name=name, grid_spec=grid_spec, out_shape=jax.ShapeDtypeStruct(out_shape, F32),
        compiler_params=_params(("parallel",)),
    )(sel, p, r, r, r, r)


def _expand_w_in(w_t, shift):
    cw, rows = w_t.shape
    tr = TM_MM
    pad = -cw % LANES

    def body(s_ref, w_ref, o_ref):
        w = jnp.concatenate([w_ref[...], jnp.zeros((pad, tr), F32)], axis=0).T
        w = jnp.concatenate([w, jnp.zeros((tr, EXP_W - cw - pad), F32)], axis=1)
        o_ref[...] = pltpu.roll(w, s_ref[0], 1).astype(BF16)

    grid_spec = pltpu.PrefetchScalarGridSpec(
        num_scalar_prefetch=1, grid=(rows // tr,), in_specs=[pl.BlockSpec((cw, tr), lambda i, t: (0, i))],
        out_specs=pl.BlockSpec((tr, EXP_W), lambda i, t: (i, 0)))
    return pl.pallas_call(
        body, name="expand_w_in", grid_spec=grid_spec, out_shape=jax.ShapeDtypeStruct((rows, EXP_W), BF16),
        compiler_params=_params(("arbitrary",)),
    )(shift, w_t)


def _pad_wq(w_t):
    cw, rows = w_t.shape

    def body(w_ref, o_ref):
        o_ref[...] = jnp.concatenate([w_ref[...], jnp.zeros((QK_PAD - cw, rows), F32)], axis=0).T.astype(BF16)

    return pl.pallas_call(
        body, name="pad_wq", out_shape=jax.ShapeDtypeStruct((rows, QK_PAD), BF16), compiler_params=_params(),
    )(w_t)


def _merge_w_in(e):
    _, rows, _ = e.shape
    tr = TM_MM

    def body(e_ref, o_ref):
        for t in range(U_COLS // LANES):
            lo, hi = t * LANES, (t + 1) * LANES
            parts = [e_ref[d, :, lo - W_LO[d]:hi - W_LO[d]] for d in range(N_DEV)
                     if CW * d < hi and CW * (d + 1) > lo]
            if not parts:
                tile = jnp.zeros((tr, LANES), BF16)
            elif len(parts) == 1:
                tile = parts[0]
            else:
                tile = (parts[0].astype(F32) + parts[1].astype(F32)).astype(BF16)
            o_ref[:, lo:hi] = tile

    return pl.pallas_call(
        body, name="merge_w_in", grid=(rows // tr,),
        in_specs=[pl.BlockSpec((N_DEV, tr, EXP_W), lambda i: (0, i, 0))],
        out_specs=pl.BlockSpec((tr, U_COLS), lambda i: (i, 0)), out_shape=jax.ShapeDtypeStruct((rows, U_COLS), BF16),
        compiler_params=_params(("parallel",)),
    )(e)


def _sum_leading(a, out_dtype, name):
    k, rows, cols = a.shape
    tr = min(rows, 1728 if rows % 1728 == 0 else rows)

    def body(a_ref, o_ref):
        acc = a_ref[0].astype(F32)
        for i in range(1, k):
            acc = acc + a_ref[i].astype(F32)
        o_ref[...] = acc.astype(out_dtype)

    return pl.pallas_call(
        body, name=name, grid=(rows // tr,),
        in_specs=[pl.BlockSpec((k, tr, cols), lambda i: (0, i, 0))],
        out_specs=pl.BlockSpec((tr, cols), lambda i: (i, 0)),
        out_shape=jax.ShapeDtypeStruct((rows, cols), out_dtype), compiler_params=_params(("parallel",)),
    )(a)


def _add_pairs(gs, recvs, core, name):
    n = len(gs)

    def body(c_ref, *refs):
        del c_ref
        for a in range(n):
            refs[2 * n + a][...] = (refs[a][...].astype(F32) + refs[n + a][...].astype(F32)).astype(refs[2 * n + a].dtype)

    def mine(r):
        return pl.BlockSpec((1,) + r.shape[1:], lambda i, c: (2 * i + c[0], 0, 0))

    def kth(r):
        return pl.BlockSpec((1,) + r.shape[1:], lambda i, c: (i, 0, 0))

    grid_spec = pltpu.PrefetchScalarGridSpec(
        num_scalar_prefetch=1, grid=(4,), in_specs=[mine(r) for r in recvs] + [kth(r) for r in recvs],
        out_specs=[kth(r) for r in recvs])
    return pl.pallas_call(
        body, name=name, grid_spec=grid_spec, out_shape=[jax.ShapeDtypeStruct(r.shape, r.dtype) for r in recvs],
        compiler_params=_params(("parallel",)),
    )(core, *gs, *recvs)


def _final_sums(ps, rs, sel, keep_t, name):
    n = len(ps)

    def body(sel_ref, *refs):
        for a in range(n):
            own = refs[5 * a][0].astype(F32)
            acc = None
            for k in range(4):
                term = jnp.where(sel_ref[0] == k, own, refs[5 * a + 1 + k][0].astype(F32))
                acc = term if acc is None else acc + term
            refs[5 * n + a][...] = acc if keep_t[a] is None else acc.T[:keep_t[a]]

    def slot(p, k):
        return pl.BlockSpec((1,) + p.shape[1:], lambda i, t: (t[k], 0, 0))

    out_shapes = [p.shape[1:] if kt is None else (kt, p.shape[1]) for p, kt in zip(ps, keep_t)]
    grid_spec = pltpu.PrefetchScalarGridSpec(
        num_scalar_prefetch=1, grid=(1,), in_specs=[slot(p, k) for p in ps for k in range(5)],
        out_specs=[pl.BlockSpec(sh, lambda i, t: (0, 0)) for sh in out_shapes])
    operands = [x for p, r in zip(ps, rs) for x in (p, r, r, r, r)]
    return pl.pallas_call(
        body, name=name, grid_spec=grid_spec, out_shape=[jax.ShapeDtypeStruct(sh, F32) for sh in out_shapes],
        compiler_params=_params(("arbitrary",)),
    )(sel, *operands)


def _ada_mod(c16, ada_w_l, ada_b_l):
    def body(c_ref, w_ref, b_ref, o_ref):
        cv = c_ref[...]
        sc = (cv * _sigmoid(cv)).astype(BF16)
        o_ref[...] = _dot(sc, w_ref[...].astype(BF16)) + b_ref[...]

    return pl.pallas_call(
        body, name="ada_mod", out_shape=jax.ShapeDtypeStruct((c16.shape[0], ada_w_l.shape[1]), F32),
        compiler_params=_params(),
    )(c16, ada_w_l, ada_b_l)


def _ada_w_grad(c_t, dmod_my):
    def body(c_ref, d_ref, o_ref):
        cv = c_ref[...]
        sc = cv * _sigmoid(cv)
        acc = sc[:, 0:1] * d_ref[0:1, :]
        for b in range(1, N_DEV):
            acc = acc + sc[:, b:b + 1] * d_ref[b:b + 1, :]
        o_ref[...] = acc

    return pl.pallas_call(
        body, name="ada_w_grad", out_shape=jax.ShapeDtypeStruct((c_t.shape[0], dmod_my.shape[1]), F32),
        compiler_params=_params(),
    )(c_t, dmod_my)


def _norm_mod(x, norm_g, mod, pos_col, invf, sign, rider, rider_inputs):
    s, d = x.shape
    tm = min(TM_MM, s)
    n_in, n_out = len(rider_inputs), len(rider.out_shape)
    steps = s // tm

    def body(x_ref, g_ref, mod_ref, p_ref, f_ref, s_ref, *rest):
        r_ins, (h_ref, ht_ref, cos_ref, sin_ref) = rest[:n_in], rest[n_in:n_in + 4]
        r_outs, sems = rest[n_in + 4:n_in + 4 + n_out], rest[n_in + 4 + n_out:]
        pl.when(pl.program_id(0) == 0)(functools.partial(rider.start, r_ins, r_outs, sems))
        xv = x_ref[...]
        r = lax.rsqrt(jnp.mean(xv * xv, axis=-1, keepdims=True) + EPS)
        hn = xv * r * g_ref[...]
        hv = hn * (1.0 + mod_ref[:, d:2 * d]) + mod_ref[:, 0:d]
        h_ref[...] = hv.astype(BF16)
        ht_ref[...] = hv.T.astype(BF16)
        ang = p_ref[...].astype(F32) * f_ref[...]
        sg = s_ref[...]
        cos_ref[...] = jnp.cos(ang) * jnp.abs(sg)
        sin_ref[...] = jnp.sin(ang) * sg

        @pl.when(pl.program_id(0) == steps - 1)
        def _():
            rider.forward(r_ins, r_outs, sems)
            rider.finish(r_ins, r_outs, sems)

    row = pl.BlockSpec((1, LANES), lambda i: (0, 0))
    tab = pl.BlockSpec((tm, LANES), lambda i: (i, 0))
    return pl.pallas_call(
        body, name="norm_mod", grid=(steps,),
        in_specs=[pl.BlockSpec((tm, d), lambda i: (i, 0)), pl.BlockSpec((1, d), lambda i: (0, 0)),
                  pl.BlockSpec((1, 3 * d), lambda i: (0, 0)), pl.BlockSpec((tm, 1), lambda i: (i, 0)), row, row]
        + [ANY] * n_in,
        out_specs=[pl.BlockSpec((tm, d), lambda i: (i, 0)), pl.BlockSpec((d, tm), lambda i: (0, i)), tab, tab] + [ANY] * n_out,
        out_shape=[jax.ShapeDtypeStruct((s, d), BF16), jax.ShapeDtypeStruct((d, s), BF16),
                   jax.ShapeDtypeStruct((s, LANES), F32), jax.ShapeDtypeStruct((s, LANES), F32)] + rider.out_shape,
        scratch_shapes=rider.scratch, compiler_params=_params(("arbitrary",)),
    )(x, norm_g, mod, pos_col, invf, sign, *rider_inputs)


def _matmul(a, b, *, nt, out_dtype, tm, tn, name, rider=None, rider_inputs=(), a_resident=False):
    m, kdim = a.shape
    n = b.shape[0] if nt else b.shape[1]
    tm, tn = min(tm, m), min(tn, n)
    n_in = len(rider_inputs)
    n_out = len(rider.out_shape) if rider else 0
    m_steps, n_steps = m // tm, n // tn
    steps = n_steps * m_steps
    inner = n_steps if a_resident else m_steps
    tile = (lambda o, i: (o, i)) if a_resident else (lambda o, i: (i, o))

    def body(a_ref, b_ref, *rest):
        r_ins, o_ref, r_outs, sems = rest[:n_in], rest[n_in], rest[n_in + 1:n_in + 1 + n_out], rest[n_in + 1 + n_out:]
        step = pl.program_id(0) * inner + pl.program_id(1)
        if rider:
            pl.when(step == 0)(functools.partial(rider.start, r_ins, r_outs, sems))
            pl.when(step == steps // 2)(functools.partial(rider.forward, r_ins, r_outs, sems))
        o = _dot_nt(a_ref[...], b_ref[...]) if nt else _dot(a_ref[...], b_ref[...])
        o_ref[...] = o.astype(out_dtype)
        if rider:
            pl.when(step == steps - 1)(functools.partial(rider.finish, r_ins, r_outs, sems))

    if nt:
        b_spec = pl.BlockSpec((tn, kdim), lambda o, i: (tile(o, i)[1], 0))
    else:
        b_spec = pl.BlockSpec((kdim, tn), lambda o, i: (0, tile(o, i)[1]))
    out = pl.pallas_call(
        body, name=name, grid=(m_steps, n_steps) if a_resident else (n_steps, m_steps),
        in_specs=[pl.BlockSpec((tm, kdim), lambda o, i: (tile(o, i)[0], 0)), b_spec] + [ANY] * n_in,
        out_specs=[pl.BlockSpec((tm, tn), tile)] + [ANY] * n_out,
        out_shape=[jax.ShapeDtypeStruct((m, n), out_dtype)] + (rider.out_shape if rider else []),
        scratch_shapes=rider.scratch if rider else [],
        compiler_params=_params(("arbitrary", "arbitrary") if rider else ("parallel", "parallel")),
    )(a, b, *rider_inputs)
    return out if rider else out[0]


HALO = 16


def _conv_specs(tm):
    def col(j):
        return pl.BlockSpec((tm, D_CONV), lambda i: (i, j))

    def prev(j):
        return pl.BlockSpec((HALO, D_CONV), lambda i: (jnp.maximum(i * (tm // HALO) - 1, 0), j))

    return [col(0), col(1), col(2), col(3), prev(0), prev(2)]


def _conv_y(xc_ref, bc_ref, cc_ref, zc_ref, xp_ref, cp_ref, w_ref, first):
    uc = cc_ref[...].astype(F32) * xc_ref[...].astype(F32)
    up = jnp.where(first, 0.0, cp_ref[...].astype(F32) * xp_ref[...].astype(F32))
    full = jnp.concatenate([up, uc], axis=0)
    u1 = pltpu.roll(full, 1, 0)[HALO:]
    u2 = pltpu.roll(full, 2, 0)[HALO:]
    w = w_ref[...]
    conv = w[0:1] * u2 + w[1:2] * u1 + w[2:3] * uc
    z = zc_ref[...].astype(F32)
    return bc_ref[...].astype(F32) * conv * (z * _sigmoid(z))


def _conv_bwd(u, dyc, conv_w):
    s = u.shape[0]
    tm = min(TM_MM, s)
    cb = D_CONV
    nt = s // tm

    def body(xc_ref, bc_ref, cc_ref, zc_ref, xp_ref, cp_ref, bn_ref, zn_ref, dy_ref, dyn_ref, w_ref, du_ref, dw_ref):
        i = pl.program_id(0)
        xc, cc = xc_ref[...].astype(F32), cc_ref[...].astype(F32)
        bc, z = bc_ref[...].astype(F32), zc_ref[...].astype(F32)
        uc = cc * xc
        up = jnp.where(i == 0, 0.0, cp_ref[...].astype(F32) * xp_ref[...].astype(F32))
        full = jnp.concatenate([up, uc], axis=0)
        u1 = pltpu.roll(full, 1, 0)[HALO:]
        u2 = pltpu.roll(full, 2, 0)[HALO:]
        w = w_ref[...]
        conv = w[0:1] * u2 + w[1:2] * u1 + w[2:3] * uc
        sg = _sigmoid(z)
        sz = z * sg
        dy = dy_ref[...].astype(F32)
        dconv = dy * bc * sz
        zn = zn_ref[...].astype(F32)
        dnext = dyn_ref[...].astype(F32) * bn_ref[...].astype(F32) * (zn * _sigmoid(zn))
        dnext = jnp.where(i == nt - 1, 0.0, dnext)
        fullb = jnp.concatenate([dconv, dnext], axis=0)
        nb = tm + HALO
        d1 = pltpu.roll(fullb, nb - 1, 0)[:tm]
        d2 = pltpu.roll(fullb, nb - 2, 0)[:tm]
        duc = w[2:3] * dconv + w[1:2] * d1 + w[0:1] * d2
        dzc = dy * bc * conv * (sg * (1.0 + z * (1.0 - sg)))
        du_ref[...] = jnp.concatenate([duc * cc, dy * conv * sz, duc * xc, dzc], axis=1).astype(BF16)
        dw = jnp.concatenate([jnp.sum(dconv * u2, axis=0, keepdims=True), jnp.sum(dconv * u1, axis=0, keepdims=True),
                              jnp.sum(dconv * uc, axis=0, keepdims=True), jnp.zeros((5, cb), F32)], axis=0)

        @pl.when(i == 0)
        def _():
            dw_ref[...] = dw

        @pl.when(i > 0)
        def _():
            dw_ref[...] += dw

    def col(j):
        return pl.BlockSpec((tm, cb), lambda i: (i, j))

    def prev(j):
        return pl.BlockSpec((HALO, cb), lambda i: (jnp.maximum(i * (tm // HALO) - 1, 0), j))

    def nxt(j):
        return pl.BlockSpec((HALO, cb), lambda i: (jnp.minimum((i + 1) * (tm // HALO), s // HALO - 1), j))

    return pl.pallas_call(
        body, name="conv_bwd", grid=(nt,),
        in_specs=[col(0), col(1), col(2), col(3), prev(0), prev(2), nxt(1), nxt(3), col(0), nxt(0),
                  pl.BlockSpec((3, cb), lambda i: (0, 0))],
        out_specs=[pl.BlockSpec((tm, 4 * cb), lambda i: (i, 0)), pl.BlockSpec((8, cb), lambda i: (0, 0))],
        out_shape=[jax.ShapeDtypeStruct((s, U_COLS), BF16), jax.ShapeDtypeStruct((8, cb), F32)],
        compiler_params=_params(("arbitrary",)),
    )(u, u, u, u, u, u, u, u, dyc, dyc, conv_w)


def _qkv_specs(tm):
    return [pl.BlockSpec((tm, Q_LORA), lambda i: (i, U_CQ // Q_LORA)),
            pl.BlockSpec((tm, KV_LORA), lambda i: (i, U_CKV // KV_LORA)),
            pl.BlockSpec((tm, LANES), lambda i: (i, U_KR // LANES)),
            pl.BlockSpec((tm, LANES), lambda i: (i, 0)), pl.BlockSpec((tm, LANES), lambda i: (i, 0))]


def _full(shape):
    return pl.BlockSpec(shape, lambda i: (0,) * len(shape))


def _k_rope_lanes(blk):
    lane = lax.broadcasted_iota(jnp.int32, blk.shape, 1)
    return jnp.where(lane < QK_ROPE, blk, 0.0)


def _qkv_fwd(u, cos, sin, wq, wkv, qag, kvag, qg, kg):
    s = u.shape[0]
    tm = min(TM_MM, s)

    def body(cq_ref, ckv_ref, kr_ref, cos_ref, sin_ref, wq_ref, wkv_ref, qag_ref, kvag_ref, qg_ref, kg_ref,
             q_ref, k_ref, v_ref):
        cq = cq_ref[...].astype(F32)
        cqn = (cq * lax.rsqrt(jnp.mean(cq * cq, axis=-1, keepdims=True) + EPS) * qag_ref[...]).astype(BF16)
        ckv = ckv_ref[...].astype(F32)
        ckvn = (ckv * lax.rsqrt(jnp.mean(ckv * ckv, axis=-1, keepdims=True) + EPS) * kvag_ref[...]).astype(BF16)
        kr = _k_rope_lanes(kr_ref[...].astype(F32))
        cosv, sinv, qgv, kgv = cos_ref[...], sin_ref[...], qg_ref[...], kg_ref[...]
        ss_r = jnp.sum(kr * kr, axis=-1, keepdims=True)
        krr = _rope(kr * kgv[:, QK_NOPE:], cosv, sinv)
        qf = _dot(cqn, wq_ref[...])
        kvf = _dot(ckvn, wkv_ref[...])
        heads = range(N_HEADS)
        qh = [qf[:, QK_PAD * h:QK_PAD * (h + 1)] for h in heads]
        kn = [kvf[:, 2 * V_HEAD * h:2 * V_HEAD * h + QK_NOPE] for h in heads]
        rq = [lax.rsqrt(jnp.sum(qh[h] * qh[h], axis=-1, keepdims=True) * (1.0 / QK_HEAD) + EPS) for h in heads]
        rk = [lax.rsqrt((jnp.sum(kn[h] * kn[h], axis=-1, keepdims=True) + ss_r) * (1.0 / QK_HEAD) + EPS) for h in heads]
        for h in heads:
            qn = qh[h] * rq[h] * qgv
            qo = jnp.concatenate([qn[:, :QK_NOPE], _rope(qn[:, QK_NOPE:], cosv, sinv)], axis=1) * (SCALE * LOG2E)
            q_ref[h] = qo.astype(BF16)
            vh = kvf[:, 2 * V_HEAD * h + QK_NOPE:2 * V_HEAD * (h + 1)]
            k_ref[h] = jnp.concatenate([kn[h] * kgv[:, :QK_NOPE] * rk[h], krr * rk[h]], axis=1).astype(BF16)
            v_ref[h] = jnp.concatenate([vh, jnp.ones_like(vh)], axis=1).astype(BF16)

    return pl.pallas_call(
        body, name="qkv_fwd", grid=(s // tm,),
        in_specs=_qkv_specs(tm) + [_full((Q_LORA, N_HEADS * QK_PAD)), _full((KV_LORA, 2 * D_ATTN)),
                                   _full((1, Q_LORA)), _full((1, KV_LORA)), _full((1, QK_PAD)), _full((1, QK_PAD))],
        out_specs=[pl.BlockSpec((N_HEADS, tm, QK_PAD), lambda i: (0, i, 0)),
                   pl.BlockSpec((N_HEADS, tm, QK_PAD), lambda i: (0, i, 0)),
                   pl.BlockSpec((N_HEADS, tm, 2 * V_HEAD), lambda i: (0, i, 0))],
        out_shape=[jax.ShapeDtypeStruct((N_HEADS, s, QK_PAD), BF16), jax.ShapeDtypeStruct((N_HEADS, s, QK_PAD), BF16),
                   jax.ShapeDtypeStruct((N_HEADS, s, 2 * V_HEAD), BF16)],
        compiler_params=_params(("parallel",)),
    )(u, u, u, cos, sin, wq, wkv, qag, kvag, qg, kg)


def _qkv_bwd(u, cos, sin, dq, dk, dv, dza, wq, wkv, qag, kvag, qg, kg, du):
    s = u.shape[0]
    tm = min(TM_ELEM, s)
    nt = s // tm

    def body(cq_ref, ckv_ref, kr_ref, cos_ref, sin_ref, dq_ref, dk_ref, dv_ref, dza_ref, wq_ref, wkv_ref, qag_ref,
             kvag_ref, qg_ref, kg_ref, du_in, du_ref, dwq_ref, dwkv_ref, dqag_ref, dkvag_ref, dqg_ref, dkg_ref,
             dwq_acc, dwkv_acc):
        del du_in
        i = pl.program_id(0)

        @pl.when(i == 0)
        def _():
            dwq_acc[...] = jnp.zeros_like(dwq_acc)
            dwkv_acc[...] = jnp.zeros_like(dwkv_acc)

        cq = cq_ref[...].astype(F32)
        rqa = lax.rsqrt(jnp.mean(cq * cq, axis=-1, keepdims=True) + EPS)
        xq = cq * rqa
        qagv = qag_ref[...]
        cqn = (xq * qagv).astype(BF16)
        ckv = ckv_ref[...].astype(F32)
        rkva = lax.rsqrt(jnp.mean(ckv * ckv, axis=-1, keepdims=True) + EPS)
        xkv = ckv * rkva
        kvagv = kvag_ref[...]
        ckvn = (xkv * kvagv).astype(BF16)
        kr = _k_rope_lanes(kr_ref[...].astype(F32))
        cosv, sinv, qgv, kgv = cos_ref[...], sin_ref[...], qg_ref[...], kg_ref[...]
        ss_r = jnp.sum(kr * kr, axis=-1, keepdims=True)
        dqg = jnp.zeros((1, QK_PAD), F32)
        dkg = jnp.zeros((1, QK_PAD), F32)
        dkr = jnp.zeros((tm, LANES), F32)
        qf = _dot(cqn, wq_ref[...])
        kvf = _dot(ckvn, wkv_ref[...])
        heads = range(N_HEADS)
        qh = [qf[:, QK_PAD * h:QK_PAD * (h + 1)] for h in heads]
        kn = [kvf[:, 2 * V_HEAD * h:2 * V_HEAD * h + QK_NOPE] for h in heads]
        rq = [lax.rsqrt(jnp.sum(qh[h] * qh[h], axis=-1, keepdims=True) * (1.0 / QK_HEAD) + EPS) for h in heads]
        rk = [lax.rsqrt((jnp.sum(kn[h] * kn[h], axis=-1, keepdims=True) + ss_r) * (1.0 / QK_HEAD) + EPS) for h in heads]
        xh = [qh[h] * rq[h] for h in heads]
        xk = [jnp.concatenate([kn[h], kr], axis=1) * rk[h] for h in heads]
        dyq, dyk = [], []
        for h in heads:
            g = dq_ref[h].astype(F32)
            dyq.append(jnp.concatenate([g[:, :QK_NOPE], _rope_t(g[:, QK_NOPE:], cosv, sinv)], axis=1))
            gk = dk_ref[h].astype(F32)
            dyk.append(jnp.concatenate([gk[:, :QK_NOPE], _rope_t(gk[:, QK_NOPE:], cosv, sinv)], axis=1))
        for h in heads:
            dqg = dqg + jnp.sum(dyq[h] * xh[h], axis=0, keepdims=True)
            dkg = dkg + jnp.sum(dyk[h] * xk[h], axis=0, keepdims=True)
        dqg = dqg * SCALE
        qgv = qgv * SCALE
        gdy = [dyq[h] * qgv for h in heads]
        gdyk = [dyk[h] * kgv for h in heads]
        tq_ = [jnp.sum(gdy[h] * xh[h], axis=-1, keepdims=True) * (1.0 / QK_HEAD) for h in heads]
        tk_ = [jnp.sum(gdyk[h] * xk[h], axis=-1, keepdims=True) * (1.0 / QK_HEAD) for h in heads]
        dqf = [(rq[h] * (gdy[h] - xh[h] * tq_[h])).astype(BF16) for h in heads]
        dkvf = []
        for h in heads:
            dxk = rk[h] * (gdyk[h] - xk[h] * tk_[h])
            dkr = dkr + dxk[:, QK_NOPE:]
            dkvf += [dxk[:, :QK_NOPE].astype(BF16), dv_ref[h]]
        dqf_b, dkvf_b = jnp.concatenate(dqf, axis=1), jnp.concatenate(dkvf, axis=1)
        dwq_acc[...] += _dot_tn(cqn, dqf_b)
        dwkv_acc[...] += _dot_tn(ckvn, dkvf_b)
        dcqn = _dot_nt(dqf_b, wq_ref[...])
        dckvn = _dot_nt(dkvf_b, wkv_ref[...])
        dqag = jnp.sum(dcqn * xq, axis=0, keepdims=True)
        dkvag = jnp.sum(dckvn * xkv, axis=0, keepdims=True)
        gq = dcqn * qagv
        dcq = rqa * (gq - xq * jnp.mean(gq * xq, axis=-1, keepdims=True))
        gkv = dckvn * kvagv
        dckv = rkva * (gkv - xkv * jnp.mean(gkv * xkv, axis=-1, keepdims=True))
        win = pltpu.roll(jnp.concatenate([dza_ref[...].astype(F32), jnp.zeros((tm, LANES), F32)], axis=1), QK_ROPE, 1)
        win = win + jnp.concatenate([dkr, jnp.zeros((tm, D_ATTN), F32)], axis=1)
        du_ref[...] = jnp.concatenate([dcq, dckv, win, jnp.zeros((tm, U_TAIL - ZA_LO - ZA_WIN), F32)], axis=1).astype(BF16)

        @pl.when(i == 0)
        def _():
            dqag_ref[...] = dqag
            dkvag_ref[...] = dkvag
            dqg_ref[...] = dqg
            dkg_ref[...] = dkg

        @pl.when(i > 0)
        def _():
            dqag_ref[...] += dqag
            dkvag_ref[...] += dkvag
            dqg_ref[...] += dqg
            dkg_ref[...] += dkg

        @pl.when(i == nt - 1)
        def _():
            dwq_ref[...] = dwq_acc[...].astype(BF16)
            dwkv_ref[...] = dwkv_acc[...].astype(BF16)

    head = lambda w: pl.BlockSpec((N_HEADS, tm, w), lambda i: (0, i, 0))
    wq_shape, wkv_shape = (Q_LORA, N_HEADS * QK_PAD), (KV_LORA, 2 * D_ATTN)
    return pl.pallas_call(
        body, name="qkv_bwd", grid=(nt,),
        in_specs=_qkv_specs(tm) + [head(QK_PAD), head(QK_PAD), head(V_HEAD), pl.BlockSpec((tm, D_ATTN), lambda i: (i, 0)),
                                   _full(wq_shape), _full(wkv_shape), _full((1, Q_LORA)), _full((1, KV_LORA)),
                                   _full((1, QK_PAD)), _full((1, QK_PAD)), ANY],
        out_specs=[pl.BlockSpec((tm, U_TAIL), lambda i: (i, U_COLS // U_TAIL - 1)), _full(wq_shape), _full(wkv_shape),
                   _full((1, Q_LORA)), _full((1, KV_LORA)), _full((1, QK_PAD)), _full((1, QK_PAD))],
        out_shape=[jax.ShapeDtypeStruct(du.shape, du.dtype), jax.ShapeDtypeStruct(wq_shape, BF16),
                   jax.ShapeDtypeStruct(wkv_shape, BF16), jax.ShapeDtypeStruct((1, Q_LORA), F32),
                   jax.ShapeDtypeStruct((1, KV_LORA), F32), jax.ShapeDtypeStruct((1, QK_PAD), F32),
                   jax.ShapeDtypeStruct((1, QK_PAD), F32)],
        scratch_shapes=[pltpu.VMEM(wq_shape, F32), pltpu.VMEM(wkv_shape, F32)],
        input_output_aliases={15: 0}, compiler_params=_params(("arbitrary",)),
    )(u, u, u, cos, sin, dq, dk, dv, dza, wq, wkv, qag, kvag, qg, kg, du)


def _flash_fwd(q, k, v):
    nh, s, _ = q.shape
    tq = min(TQ, s)
    nkv = KV_SPLIT
    tk = tq // nkv
    nq = s // tq
    nch = Q_CHAINS
    tc = tq // nch

    def body(q_ref, k_ref, v_ref, o_ref, lse_ref):
        i = pl.program_id(1)
        chains = [q_ref[0, r * tc:(r + 1) * tc, :] for r in range(nch)]

        def unit(r, j, carry, shift=None):
            m, acc = carry
            rows = pl.ds(pl.multiple_of(j * tk, tk), tk)
            sc = _dot_nt(chains[r], k_ref[0, rows, :])
            if shift is not None:
                qi = lax.broadcasted_iota(jnp.int32, sc.shape, 0)
                ki = lax.broadcasted_iota(jnp.int32, sc.shape, 1) + shift
                sc = jnp.where(ki <= qi, sc, NEG)
            m_new = jnp.maximum(m, jnp.max(sc, axis=-1, keepdims=True))
            p = jnp.exp2(sc - m_new).astype(BF16)
            return m_new, jnp.exp2(m - m_new) * acc + _dot(p, v_ref[0, rows, :])

        def trip(p, carry):
            for b in range(nkv):
                carry = tuple(unit(r, nkv * p + b, cr) for r, cr in enumerate(carry))
            return carry

        init = (jnp.full((tc, 1), NEG, F32), jnp.zeros((tc, 2 * V_HEAD), F32))
        carry = list(lax.fori_loop(0, i, trip, (init,) * nch))
        for b in range(nkv):
            for r in range(nch):
                shift = b * tk - r * tc
                if shift < tc:
                    carry[r] = unit(r, nkv * i + b, carry[r], None if shift + tk - 1 <= 0 else shift)
        for r, (m, acc) in enumerate(carry):
            l = acc[:, V_HEAD:]
            o_ref[r * tc:(r + 1) * tc, :] = (acc[:, :V_HEAD] / l).astype(BF16)
            lse = m + jnp.log(l[:, 0:1]) * LOG2E
            lse_ref[0, :, r * tc:(r + 1) * tc] = jnp.broadcast_to(lse, (tc, LANES)).T[0:1, :]

    return pl.pallas_call(
        body, name="flash_fwd", grid=(nh, nq),
        in_specs=[pl.BlockSpec((1, tq, QK_PAD), lambda h, i: (h, i, 0)),
                  pl.BlockSpec((1, s, QK_PAD), lambda h, i: (h, 0, 0)),
                  pl.BlockSpec((1, s, 2 * V_HEAD), lambda h, i: (h, 0, 0))],
        out_specs=[pl.BlockSpec((tq, V_HEAD), lambda h, i: (i, h)), pl.BlockSpec((1, 1, tq), lambda h, i: (h, 0, i))],
        out_shape=[jax.ShapeDtypeStruct((s, nh * V_HEAD), BF16), jax.ShapeDtypeStruct((nh, 1, s), F32)],
        compiler_params=_params(("parallel", "arbitrary")),
    )(q, k, v)


def _flash_bwd(q, k, v, do, lse, delta):
    nh, s, _ = q.shape
    tq = min(TQ, s)
    nq = s // tq
    kps = 2 if nq % 2 == 0 else 1
    ng = nq // kps

    def body(q_ref, k_ref, v_ref, do_ref, lse_ref, dl_ref, dq_ref, dk_ref, dv_ref, dq_acc):
        g = ng - 1 - pl.program_id(1)

        @pl.when(g == ng - 1)
        def _():
            dq_acc[...] = jnp.zeros_like(dq_acc)

        for sub in reversed(range(kps)):
            kv_block(q_ref, k_ref, v_ref, do_ref, lse_ref, dl_ref, dk_ref, dv_ref, dq_acc, g * kps + sub, sub)

        @pl.when(g == 0)
        def _():
            dq_ref[0] = dq_acc[...].astype(BF16)

    def kv_block(q_ref, k_ref, v_ref, do_ref, lse_ref, dl_ref, dk_ref, dv_ref, dq_acc, j, sub):
        own = slice(sub * tq, (sub + 1) * tq)
        kj, vj = k_ref[0, own, :], v_ref[0, own, :]

        def block(kk, vv, qq, dd, lse, dl, masked):
            st = _dot_nt(kk, qq)
            pt = jnp.exp2(st - lse)
            if masked:
                ki = lax.broadcasted_iota(jnp.int32, st.shape, 0)
                qx = lax.broadcasted_iota(jnp.int32, st.shape, 1)
                pt = jnp.where(ki <= qx, pt, 0.0)
            ddv = _dot(pt.astype(BF16), dd)
            dst = (pt * (_dot_nt(vv, dd) - dl)).astype(BF16)
            ddq = _dot_tn(dst, kk)
            return _dot(dst, qq), ddv, ddq

        def step(i, carry):
            dk, dv = carry
            rows = pl.ds(pl.multiple_of(i * tq, tq), tq)
            ddk, ddv, ddq = block(kj, vj, q_ref[0, rows, :], do_ref[rows, :], lse_ref[0, pl.ds(i, 1), :],
                                  dl_ref[0, pl.ds(i, 1), :], False)
            dq_acc[rows, :] += ddq
            return dk + ddk, dv + ddv

        th = tq // 2
        lse_j, dl_j = lse_ref[0, pl.ds(j, 1), :], dl_ref[0, pl.ds(j, 1), :]
        parts = []
        for kh, qh, masked in ((0, 0, True), (0, 1, False), (1, 1, True)):
            rows = pl.ds(pl.multiple_of(j * tq + qh * th, th), th)
            ks, qs = slice(kh * th, (kh + 1) * th), slice(qh * th, (qh + 1) * th)
            ddk, ddv, ddq = block(kj[ks], vj[ks], q_ref[0, rows, :], do_ref[rows, :], lse_j[:, qs], dl_j[:, qs], masked)
            dq_acc[rows, :] += ddq
            parts.append((ddk, ddv))
        carry = (jnp.concatenate([parts[0][0] + parts[1][0], parts[2][0]], axis=0),
                 jnp.concatenate([parts[0][1] + parts[1][1], parts[2][1]], axis=0))
        dk, dv = lax.fori_loop(j + 1, nq, step, carry)
        dk_ref[0, own, :] = (dk * LN2).astype(BF16)
        dv_ref[0, own, :] = dv.astype(BF16)

    return pl.pallas_call(
        body, name="flash_bwd", grid=(nh, ng),
        in_specs=[pl.BlockSpec((1, s, QK_PAD), lambda h, j: (h, 0, 0)),
                  pl.BlockSpec((1, kps * tq, QK_PAD), lambda h, g: (h, ng - 1 - g, 0)),
                  pl.BlockSpec((1, kps * tq, V_HEAD), lambda h, g: (h, ng - 1 - g, 0)),
                  pl.BlockSpec((s, V_HEAD), lambda h, j: (0, h)),
                  pl.BlockSpec((1, nq, tq), lambda h, j: (h, 0, 0)),
                  pl.BlockSpec((1, nq, tq), lambda h, j: (h, 0, 0))],
        out_specs=[pl.BlockSpec((1, s, QK_PAD), lambda h, j: (h, 0, 0)),
                   pl.BlockSpec((1, kps * tq, QK_PAD), lambda h, g: (h, ng - 1 - g, 0)),
                   pl.BlockSpec((1, kps * tq, V_HEAD), lambda h, g: (h, ng - 1 - g, 0))],
        out_shape=[jax.ShapeDtypeStruct((nh, s, QK_PAD), BF16), jax.ShapeDtypeStruct((nh, s, QK_PAD), BF16),
                   jax.ShapeDtypeStruct((nh, s, V_HEAD), BF16)],
        scratch_shapes=[pltpu.VMEM((s, QK_PAD), F32)],
        compiler_params=_params(("parallel", "arbitrary")),
    )(q, k, v, do, lse, delta)


def _tail(x, target, o, u, mod, w_out, conv_w):
    s, d = x.shape
    tm = min(TM_ELEM, s)

    def body(x_ref, t_ref, o_ref, za_ref, mod_ref, w_ref, xc_ref, bc_ref, cc_ref, zc_ref, xp_ref, cp_ref, cw_ref,
             gx_ref, dy_ref, ycat_ref, dyc_ref, do_ref, du_ref, delta_ref, dgate_ref, loss_ref):
        i = pl.program_id(0)
        za = pltpu.roll(za_ref[:, ZA_LO:ZA_LO + ZA_WIN].astype(F32), ZA_WIN - QK_ROPE, 1)[:, :D_ATTN]
        ov = o_ref[...].astype(F32)
        sg = _sigmoid(za)
        sl = za * sg
        ya = ov * sl
        y = _dot(ya.astype(BF16), w_ref[D_CONV:, :])
        yc = _conv_y(xc_ref, bc_ref, cc_ref, zc_ref, xp_ref, cp_ref, cw_ref, i == 0)
        y = y + _dot(yc.astype(BF16), w_ref[:D_CONV, :])
        ycat_ref[...] = jnp.concatenate([yc.T, ya.T], axis=0).astype(BF16)
        gate = mod_ref[:, 2 * d:3 * d]
        e = x_ref[...] + gate * y - t_ref[...]
        dout = e * (1.0 / d)
        gx_ref[...] = dout
        dy = (dout * gate).astype(BF16)
        dy_ref[...] = dy
        dycat = _dot_nt(dy, w_ref[...])
        dyc_ref[...] = dycat[:, :D_CONV].astype(BF16)
        dya = dycat[:, D_CONV:]
        dov = dya * sl
        do_ref[...] = dov.astype(BF16)
        du_ref[...] = (dya * ov * (sg * (1.0 + za * (1.0 - sg)))).astype(BF16)
        prod_t = (dov * ov).T
        for h in range(N_HEADS):
            delta_ref[h] = jnp.sum(prod_t[V_HEAD * h:V_HEAD * (h + 1), :], axis=0, keepdims=True)
        dgate = jnp.sum(dout * y, axis=0, keepdims=True)
        part = jnp.sum(jnp.sum(e * e, axis=0, keepdims=True), axis=1, keepdims=True) * (0.5 / d)
        part = jnp.broadcast_to(part, (1, LANES))

        @pl.when(i == 0)
        def _():
            dgate_ref[...] = dgate
            loss_ref[...] = part

        @pl.when(i > 0)
        def _():
            dgate_ref[...] += dgate
            loss_ref[...] += part

    tok = lambda w: pl.BlockSpec((tm, w), lambda i: (i, 0))
    return pl.pallas_call(
        body, name="tail", grid=(s // tm,),
        in_specs=[tok(d), tok(d), tok(D_ATTN), pl.BlockSpec((tm, U_TAIL), lambda i: (i, U_COLS // U_TAIL - 1)),
                  _full((1, 3 * d)), _full((d, d))] + _conv_specs(tm) + [_full((3, D_CONV))],
        out_specs=[tok(d), tok(d), pl.BlockSpec((d, tm), lambda i: (0, i)), tok(D_CONV), tok(D_ATTN), tok(D_ATTN),
                   pl.BlockSpec((N_HEADS, 1, tm), lambda i: (0, 0, i)), _full((1, d)), _full((1, LANES))],
        out_shape=[jax.ShapeDtypeStruct((s, d), F32), jax.ShapeDtypeStruct((s, d), BF16),
                   jax.ShapeDtypeStruct((d, s), BF16), jax.ShapeDtypeStruct((s, D_CONV), BF16),
                   jax.ShapeDtypeStruct((s, D_ATTN), BF16), jax.ShapeDtypeStruct((s, D_ATTN), BF16),
                   jax.ShapeDtypeStruct((N_HEADS, 1, s), F32), jax.ShapeDtypeStruct((1, d), F32),
                   jax.ShapeDtypeStruct((1, LANES), F32)],
        compiler_params=_params(("arbitrary",)),
    )(x, target, o, u, mod, w_out, u, u, u, u, u, u, conv_w)


def _norm_bwd(x, dh, gx1, norm_g, mod):
    s, d = x.shape
    tm = min(TM_MM, s)

    def body(x_ref, dh_ref, gx_ref, g_ref, mod_ref, o_ref, dshift_ref, dscale_ref, dg_ref):
        i = pl.program_id(0)
        gv, sc1 = g_ref[...], 1.0 + mod_ref[:, d:2 * d]
        gsc = gv * sc1
        half = NORM_ROWS // 2

        def group(c, acc):
            a_dh, a_dhxn = acc
            ks = range(NORM_GROUP)
            rows = [pl.ds(pl.multiple_of((c * NORM_GROUP + k) * NORM_ROWS, NORM_ROWS), NORM_ROWS) for k in ks]
            xv = [x_ref[rows[k], :] for k in ks]
            dhv = [dh_ref[rows[k], :].astype(F32) for k in ks]
            r = [lax.rsqrt(jnp.mean(xv[k] * xv[k], axis=-1, keepdims=True) + EPS) for k in ks]
            xn = [xv[k] * r[k] for k in ks]
            dxn = [dhv[k] * gsc for k in ks]
            t = [jnp.mean(dxn[k] * xn[k], axis=-1, keepdims=True) for k in ks]
            for k in ks:
                o_ref[rows[k], :] = gx_ref[rows[k], :] + r[k] * (dxn[k] - xn[k] * t[k])
                dhxn = dhv[k] * xn[k]
                a_dh = a_dh + dhv[k][:half] + dhv[k][half:]
                a_dhxn = a_dhxn + dhxn[:half] + dhxn[half:]
            return a_dh, a_dhxn

        zero = jnp.zeros((half, d), F32)
        a_dh, a_dhxn = lax.fori_loop(0, tm // (NORM_ROWS * NORM_GROUP), group, (zero, zero))
        dshift = jnp.sum(a_dh, axis=0, keepdims=True)
        s_dhxn = jnp.sum(a_dhxn, axis=0, keepdims=True)
        dscale, dg = s_dhxn * gv, s_dhxn * sc1

        @pl.when(i == 0)
        def _():
            dshift_ref[...] = dshift
            dscale_ref[...] = dscale
            dg_ref[...] = dg

        @pl.when(i > 0)
        def _():
            dshift_ref[...] += dshift
            dscale_ref[...] += dscale
            dg_ref[...] += dg

    tok = pl.BlockSpec((tm, d), lambda i: (i, 0))
    row = jax.ShapeDtypeStruct((1, d), F32)
    return pl.pallas_call(
        body, name="norm_bwd", grid=(s // tm,),
        in_specs=[tok, tok, tok, _full((1, d)), _full((1, 3 * d))],
        out_specs=[tok, _full((1, d)), _full((1, d)), _full((1, d))],
        out_shape=[jax.ShapeDtypeStruct((s, d), F32), row, row, row],
        compiler_params=_params(("arbitrary",)),
    )(x, dh, gx1, norm_g, mod)


def _adam_update(w, g, m, v):
    nm = ADAM_B1 * m + (1.0 - ADAM_B1) * g
    nv = ADAM_B2 * v + (1.0 - ADAM_B2) * (g * g)
    m_hat = nm / (1.0 - ADAM_B1 ** ADAM_STEP)
    v_hat = nv / (1.0 - ADAM_B2 ** ADAM_STEP)
    return -ADAM_LR * (m_hat / (jnp.sqrt(v_hat) + ADAM_EPS) + ADAM_WD * w), nm, nv


def _adamw_small(ws, gs, ms, vs):
    n = len(ws)

    def body(*refs):
        for a in range(n):
            w_ref, g_ref, m_ref, v_ref = (refs[k * n + a] for k in range(4))
            refs[4 * n + a][...], refs[5 * n + a][...], refs[6 * n + a][...] = _adam_update(
                w_ref[...], g_ref[...], m_ref[...], v_ref[...])

    shapes = [jax.ShapeDtypeStruct(w.shape, F32) for w in ws]
    out = pl.pallas_call(body, name="adamw_small", out_shape=shapes * 3, compiler_params=_params())(*ws, *gs, *ms, *vs)
    return out[:n], out[n:2 * n], out[2 * n:]


def _adamw(w, g, m, v, name, rider=None, rider_inputs=()):
    rows, cols = w.shape
    tr = 256 if rows % 256 == 0 else rows
    tc = 512 if (rows > 256 and tr == rows and cols % 512 == 0) else cols
    n_in = len(rider_inputs)
    n_out = len(rider.out_shape) if rider else 0
    inner = cols // tc
    steps = (rows // tr) * inner

    def body(w_ref, g_ref, m_ref, v_ref, *rest):
        r_ins, (d_ref, nm_ref, nv_ref) = rest[:n_in], rest[n_in:n_in + 3]
        r_outs, sems = rest[n_in + 3:n_in + 3 + n_out], rest[n_in + 3 + n_out:]
        step = pl.program_id(0) * inner + pl.program_id(1)
        if rider:
            pl.when(step == 0)(functools.partial(rider.start, r_ins, r_outs, sems))
            pl.when(step == steps // 2)(functools.partial(rider.forward, r_ins, r_outs, sems))
        d_ref[...], nm_ref[...], nv_ref[...] = _adam_update(w_ref[...], g_ref[...], m_ref[...], v_ref[...])
        if rider:
            pl.when(step == steps - 1)(functools.partial(rider.finish, r_ins, r_outs, sems))

    spec = pl.BlockSpec((tr, tc), lambda i, j: (i, j))
    shape = jax.ShapeDtypeStruct((rows, cols), F32)
    return pl.pallas_call(
        body, name=name, grid=(rows // tr, inner), in_specs=[spec] * 4 + [ANY] * n_in,
        out_specs=[spec] * 3 + [ANY] * n_out, out_shape=[shape] * 3 + (rider.out_shape if rider else []),
        scratch_shapes=rider.scratch if rider else [],
        compiler_params=_params(("arbitrary", "arbitrary") if rider else ("parallel", "parallel")),
    )(w, g, m, v, *rider_inputs)


def _pad_cols(a, n):
    return jnp.pad(a, ((0, 0), (0, n - a.shape[1])))


def kernel(x, c, positions, ada_w, ada_b, norm_g, w_in, conv_w, q_a_g, w_q_b, kv_a_g, w_kv_b, q_g, k_g, w_out, loss_target, m_ada_w, m_ada_b, m_norm_g, m_w_in, m_conv_w, m_q_a_g, m_w_q_b, m_kv_a_g, m_w_kv_b, m_q_g, m_k_g, m_w_out, v_ada_w, v_ada_b, v_norm_g, v_w_in, v_conv_w, v_q_a_g, v_w_q_b, v_kv_a_g, v_w_kv_b, v_q_g, v_k_g, v_w_out):
    me = _my_index()
    s = x.shape[1]
    nq = s // min(TQ, s)
    x2, tgt = x[0], loss_target[0]
    w_in_l, w_q_l, w_kv_l, w_out_l, conv_l, ada_w_l = w_in[0], w_q_b[0], w_kv_b[0], w_out[0], conv_w[0], ada_w[0]
    ada_cols = ada_w_l.shape[1]

    small = jnp.concatenate([c.reshape(-1, LANES), conv_l.reshape(-1, LANES), jnp.zeros((5, LANES), F32)], axis=0)
    (small_g,) = _exchange(_GatherDirect([small]), [small], "gather_c")
    c_all = small_g[:, :D_MODEL // LANES].reshape(N_DEV, D_MODEL)
    conv_g = small_g[:, D_MODEL // LANES:D_MODEL // LANES + 3].transpose(1, 0, 2).reshape(3, D_CONV)

    ada_b_l = lax.dynamic_slice(ada_b, (0, me * ada_cols), (1, ada_cols))
    mod_cols = _ada_mod(jnp.pad(c_all, ((0, 8), (0, 0))), ada_w_l, ada_b_l)[:N_DEV]
    (mod_g,) = _exchange(_GatherDirect([mod_cols]), [mod_cols], "gather_mod")
    mod = lax.dynamic_index_in_dim(mod_g, me, axis=1, keepdims=False).reshape(1, 3 * D_MODEL)

    half = jnp.arange(0, QK_ROPE, 2, dtype=F32) / QK_ROPE
    inv_freq = ROPE_BASE ** (-half)
    zeros64 = jnp.zeros((LANES - QK_ROPE,), F32)
    invf = jnp.concatenate([inv_freq, inv_freq, zeros64]).reshape(1, LANES)
    sign = jnp.concatenate([-jnp.ones((32,), F32), jnp.ones((32,), F32), zeros64]).reshape(1, LANES)
    qg_p, kg_p = _pad_cols(q_g, QK_PAD), _pad_cols(k_g, QK_PAD)

    my_off = ((CW * me) % LANES).astype(jnp.int32)
    win = [_expand_w_in(w_in_l.T, my_off.reshape(1))]
    h, h_t, cos, sin, win_g = _norm_mod(x2, norm_g, mod, positions.reshape(s, 1), invf, sign, _Gather(win, relay=True, parts=4), win)
    w_in_p = _merge_w_in(win_g)
    rest = [_pad_wq(w_q_l.T), w_kv_l.astype(BF16), w_out_l.astype(BF16)]
    u, wq_g, wkv_g, w_out_g = _matmul(h, w_in_p, nt=False, out_dtype=BF16, tm=2 * TM_MM, tn=2048, name="in_proj",
                                      rider=_Gather(rest), rider_inputs=rest)
    w_out_g = w_out_g.reshape(D_MODEL, D_MODEL)
    wq_g = wq_g.transpose(1, 0, 2).reshape(Q_LORA, N_HEADS * QK_PAD)
    wkv_g = wkv_g.transpose(1, 0, 2).reshape(KV_LORA, 2 * D_ATTN)
    q, k, v = _qkv_fwd(u, cos, sin, wq_g, wkv_g, q_a_g, kv_a_g, qg_p, kg_p)
    o, lse = _flash_fwd(q, k, v)
    gx1, dy, ycat_t, dyc, do, dza, delta, dgate, loss_row = _tail(x2, tgt, o, u, mod, w_out_g, conv_g)

    dq, dk, dv = _flash_bwd(q, k, v, do, lse.reshape(N_HEADS, nq, s // nq), delta.reshape(N_HEADS, nq, s // nq))
    du, dconv = _conv_bwd(u, dyc, conv_g)
    du, dwq, dwkv, dqag, dkvag, dqg, dkg = _qkv_bwd(u, cos, sin, dq, dk, dv, dza, wq_g, wkv_g, q_a_g, kv_a_g, qg_p, kg_p, du)
    dwq = dwq.reshape(Q_LORA, N_HEADS, QK_PAD).transpose(1, 0, 2)
    dwkv = dwkv.reshape(KV_LORA, N_HEADS, 2 * V_HEAD).transpose(1, 0, 2)
    dw_in = _matmul(h_t, du, nt=False, out_dtype=BF16, tm=TM_MM, tn=768, name="dw_in")
    first = [dw_in, dwq, dwkv]
    dw_out, r_in, r_q, r_kv = _matmul(ycat_t, dy, nt=False, out_dtype=BF16, tm=TM_MM, tn=512, name="dw_out",
                                      rider=_SiblingExchange(first, [True, False, False]), rider_inputs=first)
    dw_out = dw_out.reshape(N_DEV, D_MODEL // N_DEV, D_MODEL)
    (r_out,) = _exchange(_SiblingExchange([dw_out], [False]), [dw_out], "rs_sibling_out")
    core = lax.axis_index("c").astype(jnp.int32)
    lo_tiles = ((CW * (2 * jnp.arange(4, dtype=jnp.int32) + core)) // LANES).astype(jnp.int32)
    pairs = [_add_window(dw_in, r_in, lo_tiles),
             *_add_pairs([dwq, dwkv, dw_out], [r_q, r_kv, r_out], core.reshape(1), "rs_add_rest")]
    dh, *quads = _matmul(du, w_in_p, nt=True, out_dtype=BF16, tm=2 * TM_MM, tn=512, name="dh",
                         rider=_ChipExchange(pairs), rider_inputs=pairs, a_resident=True)
    my_chip = 2 * lax.axis_index("x") + lax.axis_index("y")
    written = jnp.where(jnp.arange(4) == my_chip, (jnp.arange(4) + 1) % 4, jnp.arange(4))
    sel = jnp.concatenate([my_chip.reshape(1), written, ((EXP_W - my_off) % EXP_W).reshape(1)]).astype(jnp.int32)
    g_w_in_t = _final_sum(pairs[0], quads[0], sel, "rs_sum_in", unshift=True, keep_t=CW)
    g_w_q_t, g_w_kv, g_w_out = _final_sums(pairs[1:], quads[1:], sel, [QK_HEAD, None, None], "rs_sum_rest")
    grad_x, dshift, dscale, dng = _norm_bwd(x2, dh, gx1, norm_g, mod)

    row = jnp.concatenate([dshift, dscale, dgate, dng, dqag, dkvag, dqg, dkg, dconv[:3].reshape(1, 3 * D_CONV), loss_row], axis=1)
    w_t, m_t, v_t = (a.reshape(a.shape[-2:]).T for a in (w_in, m_w_in, v_w_in))
    *adam_w_in, rows_g = _adamw(w_t, g_w_in_t, m_t, v_t, "adamw_w_in", rider=_GatherDirect([row]), rider_inputs=[row])
    tot = _sum_leading(rows_g, F32, "sum_small")
    dmod_all = rows_g[:, 0, SM_MOD:SM_NG]
    g_ada_b = tot[:, SM_MOD:SM_NG]
    g_norm_g = tot[:, SM_NG:SM_QAG]
    g_q_a_g = tot[:, SM_QAG:SM_KVAG]
    g_kv_a_g = tot[:, SM_KVAG:SM_QG]
    g_q_g = tot[:, SM_QG:SM_QG + QK_HEAD]
    g_k_g = tot[:, SM_KG:SM_KG + QK_HEAD]
    conv_cols = conv_l.shape[1]
    g_conv = lax.dynamic_slice(tot[:, SM_CONV:SM_LOSS].reshape(3, D_CONV), (0, me * conv_cols), (3, conv_cols))
    loss = tot[0, SM_LOSS]
    dmod_my = lax.dynamic_slice(dmod_all, (0, me * ada_cols), (N_DEV, ada_cols))
    g_ada_w = _ada_w_grad(c_all.T, dmod_my)

    grads = dict(ada_w=g_ada_w, ada_b=g_ada_b, norm_g=g_norm_g, w_in=g_w_in_t, conv_w=g_conv, q_a_g=g_q_a_g, w_q_b=g_w_q_t,
                 kv_a_g=g_kv_a_g, w_kv_b=g_w_kv, q_g=g_q_g, k_g=g_k_g, w_out=g_w_out)
    weights = dict(ada_w=(ada_w, m_ada_w, v_ada_w), ada_b=(ada_b, m_ada_b, v_ada_b), norm_g=(norm_g, m_norm_g, v_norm_g),
                   w_in=(w_in, m_w_in, v_w_in), conv_w=(conv_w, m_conv_w, v_conv_w), q_a_g=(q_a_g, m_q_a_g, v_q_a_g),
                   w_q_b=(w_q_b, m_w_q_b, v_w_q_b), kv_a_g=(kv_a_g, m_kv_a_g, v_kv_a_g), w_kv_b=(w_kv_b, m_w_kv_b, v_w_kv_b),
                   q_g=(q_g, m_q_g, v_q_g), k_g=(k_g, m_k_g, v_k_g), w_out=(w_out, m_w_out, v_w_out))
    names = list(grads)
    big = ("ada_w", "w_in", "w_q_b", "w_kv_b", "w_out")
    small = [n for n in names if n not in big]

    def two_d(n, a):
        w = weights[n][0]
        return a.reshape(w.shape[-2:] if w.ndim == 3 else (1, w.shape[-1]))

    result = {}
    for n in big:
        transposed = n in ("w_in", "w_q_b")
        w2, m2, v2 = ((two_d(n, a).T if transposed else two_d(n, a)) for a in weights[n])
        g2 = grads[n] if transposed else two_d(n, grads[n])
        done = (g2, *(adam_w_in if n == "w_in" else _adamw(w2, g2, m2, v2, "adamw_" + n)))
        result[n] = [a.T if transposed else a for a in done]
    g_small = [two_d(n, grads[n]) for n in small]
    w_s, m_s, v_s = ([two_d(n, weights[n][k]) for n in small] for k in range(3))
    d_s, nm_s, nv_s = _adamw_small(w_s, g_small, m_s, v_s)
    for a, n in enumerate(small):
        result[n] = [g_small[a], d_s[a], nm_s[a], nv_s[a]]
    outs = [[result[n][k].reshape(weights[n][0].shape) for n in names] for k in range(4)]
    return (loss, grad_x.reshape(x.shape), *outs[0], *outs[1], *outs[2], *outs[3])
```
